```python
import jax, jax.numpy as jnp
from jax import lax
import numpy as np

D_MODEL = 2048
BATCH = 8
SEQ = 4096
DEPTH = 1

D_MIX = D_MODEL
D_POOL = D_MIX // 2
D_LRU = D_MIX - D_POOL
POOL_WINDOWS = (2, 4, 8, 16)
N_POOL_GROUPS = len(POOL_WINDOWS)
POOL_GROUP = D_POOL // N_POOL_GROUPS
N_LRU_HEADS = 8
LRU_HEAD = D_LRU // N_LRU_HEADS
CONV_WIDTH = 4
CONV_PAD = (1, 2)
RG_C = 8.0
D_FF = 4 * D_MODEL
LN_EPS = 1e-5
DEEPNORM_ALPHA = (2 * DEPTH) ** 0.25
DEEPNORM_BETA = (8 * DEPTH) ** -0.25

kernel_name = "hybrid_pool_rglru_deepnorm_encoder"


def layer_norm(x, g, b):
    xf = x.astype(jnp.float32)
    mu = jnp.mean(xf, axis=-1, keepdims=True)
    var = jnp.mean(jnp.square(xf - mu), axis=-1, keepdims=True)
    y = (xf - mu) * lax.rsqrt(var + LN_EPS) * g.astype(jnp.float32) + b.astype(jnp.float32)
    return y.astype(x.dtype)


def window_mean(u, w):
    S = u.shape[1]
    c = jnp.pad(jnp.cumsum(u.astype(jnp.float32), axis=1), ((0, 0), (1, 0), (0, 0)))
    t = jnp.arange(S)
    lo = jnp.clip(t - w // 2, 0, S)
    hi = jnp.clip(t + w // 2, 0, S)
    s = jnp.take(c, hi, axis=1) - jnp.take(c, lo, axis=1)
    cnt = (hi - lo).astype(jnp.float32)
    return s / cnt[None, :, None]


def multiscale_pool(u, w_pool, pool_scale):
    B, S, _ = u.shape
    ug = u.reshape(B, S, N_POOL_GROUPS, POOL_GROUP)
    means = jnp.stack([window_mean(ug[:, :, g], w) for g, w in enumerate(POOL_WINDOWS)], axis=2)
    d = (means - ug.astype(jnp.float32)).astype(u.dtype)
    out = jnp.einsum('bsgi,gio->bsgo', d, w_pool).reshape(B, S, D_POOL)
    return out * pool_scale


def depthwise_conv(u, conv_w, conv_b):
    y = lax.conv_general_dilated(u, conv_w, window_strides=(1,), padding=[CONV_PAD],
                                 dimension_numbers=('NWC', 'WIO', 'NWC'),
                                 feature_group_count=u.shape[-1])
    return y + conv_b


def _combine(c1, c2):
    a1, b1 = c1
    a2, b2 = c2
    return a1 * a2, a2 * b1 + b2


def linear_scan(a, b, reverse):
    return lax.associative_scan(_combine, (a, b), reverse=reverse, axis=1)[1]


def rg_lru_bidirectional(xc, w_a, b_a, w_i, b_i, lam):
    B, S, C = xc.shape
    xh = xc.reshape(B, S, N_LRU_HEADS, LRU_HEAD)
    pre_r = jnp.einsum('bshi,nhio->nbsho', xh, w_a).reshape(2, B, S, C)
    pre_i = jnp.einsum('bshi,nhio->nbsho', xh, w_i).reshape(2, B, S, C)
    r = jax.nn.sigmoid(pre_r.astype(jnp.float32) + b_a.astype(jnp.float32)[:, None, None, :])
    i = jax.nn.sigmoid(pre_i.astype(jnp.float32) + b_i.astype(jnp.float32)[:, None, None, :])
    log_a = -RG_C * r * jax.nn.softplus(-lam.astype(jnp.float32))[:, None, None, :]
    a = jnp.exp(log_a)
    mult = jnp.sqrt(-jnp.expm1(2.0 * log_a))
    b = mult * i * xc.astype(jnp.float32)[None]
    h_fwd = linear_scan(a[0], b[0], False)
    h_bwd = linear_scan(a[1], b[1], True)
    return (h_fwd + h_bwd).astype(xc.dtype)


def hybrid_mixer(x, w_in, w_pool, pool_scale, conv_w, conv_b,
                 w_rg_a, b_rg_a, w_rg_i, b_rg_i, rg_lambda, w_out):
    proj = jnp.einsum('bsd,de->bse', x, w_in)
    u_pool = proj[..., :D_POOL]
    u_rec = proj[..., D_POOL:D_POOL + D_LRU]
    u_gate = proj[..., D_POOL + D_LRU:]
    y_pool = multiscale_pool(u_pool, w_pool, pool_scale)
    xc = depthwise_conv(u_rec, conv_w, conv_b)
    h = rg_lru_bidirectional(xc, w_rg_a, b_rg_a, w_rg_i, b_rg_i, rg_lambda)
    y_rec = h * jax.nn.gelu(u_gate)
    y = jnp.concatenate([y_pool, y_rec], axis=-1)
    return jnp.einsum('bse,ed->bsd', y, w_out)


def squared_relu_mlp(x, w1, w2):
    h = jnp.square(jax.nn.relu(jnp.einsum('bsd,df->bsf', x, w1)))
    return jnp.einsum('bsf,fd->bsd', h, w2)


def _fwd_setup_inputs(seed: int = 0) -> dict:
    key = jax.random.key(seed)
    ks = jax.random.split(key, 20)
    f32 = jnp.float32

    def nrm(k, shape, scale):
        return jax.random.normal(k, shape, f32) * scale

    x = jax.random.normal(ks[0], (BATCH, SEQ, D_MODEL), f32)
    ln_mix_g = 1.0 + nrm(ks[1], (DEPTH, D_MODEL), 0.02)
    ln_mix_b = nrm(ks[2], (DEPTH, D_MODEL), 0.02)
    w_in = nrm(ks[3], (DEPTH, D_MODEL, D_POOL + 2 * D_LRU), D_MODEL ** -0.5)
    w_pool = nrm(ks[4], (DEPTH, N_POOL_GROUPS, POOL_GROUP, POOL_GROUP), POOL_GROUP ** -0.5)
    pool_scale = 1.0 + nrm(ks[5], (DEPTH, D_POOL), 0.1)
    conv_w = nrm(ks[6], (DEPTH, CONV_WIDTH, 1, D_LRU), CONV_WIDTH ** -0.5)
    conv_b = nrm(ks[7], (DEPTH, D_LRU), 0.02)
    w_rg_a = nrm(ks[8], (DEPTH, 2, N_LRU_HEADS, LRU_HEAD, LRU_HEAD), LRU_HEAD ** -0.5)
    b_rg_a = nrm(ks[9], (DEPTH, 2, D_LRU), 0.02)
    w_rg_i = nrm(ks[10], (DEPTH, 2, N_LRU_HEADS, LRU_HEAD, LRU_HEAD), LRU_HEAD ** -0.5)
    b_rg_i = nrm(ks[11], (DEPTH, 2, D_LRU), 0.02)
    u = jax.random.uniform(ks[12], (DEPTH, 2, D_LRU), f32, minval=0.9, maxval=0.999)
    s = u ** (1.0 / RG_C)
    rg_lambda = jnp.log(s) - jnp.log1p(-s)
    w_out = nrm(ks[13], (DEPTH, D_MIX, D_MODEL), D_MIX ** -0.5 * DEEPNORM_BETA)
    ln_ffn_g = 1.0 + nrm(ks[14], (DEPTH, D_MODEL), 0.02)
    ln_ffn_b = nrm(ks[15], (DEPTH, D_MODEL), 0.02)
    w_mlp_in = nrm(ks[16], (DEPTH, D_MODEL, D_FF), D_MODEL ** -0.5 * DEEPNORM_BETA)
    w_mlp_out = nrm(ks[17], (DEPTH, D_FF, D_MODEL), D_FF ** -0.5 * DEEPNORM_BETA)
    return {"x": x, "ln_mix_g": ln_mix_g, "ln_mix_b": ln_mix_b, "w_in": w_in,
            "w_pool": w_pool, "pool_scale": pool_scale, "conv_w": conv_w, "conv_b": conv_b,
            "w_rg_a": w_rg_a, "b_rg_a": b_rg_a, "w_rg_i": w_rg_i, "b_rg_i": b_rg_i,
            "rg_lambda": rg_lambda, "w_out": w_out, "ln_ffn_g": ln_ffn_g, "ln_ffn_b": ln_ffn_b,
            "w_mlp_in": w_mlp_in, "w_mlp_out": w_mlp_out}


def _fwd_reference(x, ln_mix_g, ln_mix_b, w_in, w_pool, pool_scale, conv_w, conv_b,
              w_rg_a, b_rg_a, w_rg_i, b_rg_i, rg_lambda, w_out, ln_ffn_g, ln_ffn_b,
              w_mlp_in, w_mlp_out):
    for l in range(DEPTH):
        mix = hybrid_mixer(x, w_in[l], w_pool[l], pool_scale[l], conv_w[l], conv_b[l],
                           w_rg_a[l], b_rg_a[l], w_rg_i[l], b_rg_i[l], rg_lambda[l], w_out[l])
        x = layer_norm(DEEPNORM_ALPHA * x + mix, ln_mix_g[l], ln_mix_b[l])
        ffn = squared_relu_mlp(x, w_mlp_in[l], w_mlp_out[l])
        x = layer_norm(DEEPNORM_ALPHA * x + ffn, ln_ffn_g[l], ln_ffn_b[l])
    return x


import jax as _jax
import jax.numpy as _jnp

TWIN_FORMAT = 'train_step'
FWD_PARAMS = ['x', 'ln_mix_g', 'ln_mix_b', 'w_in', 'w_pool', 'pool_scale', 'conv_w', 'conv_b', 'w_rg_a', 'b_rg_a', 'w_rg_i', 'b_rg_i', 'rg_lambda', 'w_out', 'ln_ffn_g', 'ln_ffn_b', 'w_mlp_in', 'w_mlp_out']
TWIN_WEIGHTS = ['ln_mix_g', 'ln_mix_b', 'w_in', 'w_pool', 'pool_scale', 'conv_w', 'conv_b', 'w_rg_a', 'b_rg_a', 'w_rg_i', 'b_rg_i', 'rg_lambda', 'w_out', 'ln_ffn_g', 'ln_ffn_b', 'w_mlp_in', 'w_mlp_out']
TWIN_DIFF_INPUT = 'x'
TWIN_INPUTS = ['x', 'ln_mix_g', 'ln_mix_b', 'w_in', 'w_pool', 'pool_scale', 'conv_w', 'conv_b', 'w_rg_a', 'b_rg_a', 'w_rg_i', 'b_rg_i', 'rg_lambda', 'w_out', 'ln_ffn_g', 'ln_ffn_b', 'w_mlp_in', 'w_mlp_out', 'loss_target', 'm_ln_mix_g', 'm_ln_mix_b', 'm_w_in', 'm_w_pool', 'm_pool_scale', 'm_conv_w', 'm_conv_b', 'm_w_rg_a', 'm_b_rg_a', 'm_w_rg_i', 'm_b_rg_i', 'm_rg_lambda', 'm_w_out', 'm_ln_ffn_g', 'm_ln_ffn_b', 'm_w_mlp_in', 'm_w_mlp_out', 'v_ln_mix_g', 'v_ln_mix_b', 'v_w_in', 'v_w_pool', 'v_pool_scale', 'v_conv_w', 'v_conv_b', 'v_w_rg_a', 'v_b_rg_a', 'v_w_rg_i', 'v_b_rg_i', 'v_rg_lambda', 'v_w_out', 'v_ln_ffn_g', 'v_ln_ffn_b', 'v_w_mlp_in', 'v_w_mlp_out']
TWIN_OUTPUTS = ['loss', 'grad_x', 'grad_ln_mix_g', 'grad_ln_mix_b', 'grad_w_in', 'grad_w_pool', 'grad_pool_scale', 'grad_conv_w', 'grad_conv_b', 'grad_w_rg_a', 'grad_b_rg_a', 'grad_w_rg_i', 'grad_b_rg_i', 'grad_rg_lambda', 'grad_w_out', 'grad_ln_ffn_g', 'grad_ln_ffn_b', 'grad_w_mlp_in', 'grad_w_mlp_out', 'delta_ln_mix_g', 'delta_ln_mix_b', 'delta_w_in', 'delta_w_pool', 'delta_pool_scale', 'delta_conv_w', 'delta_conv_b', 'delta_w_rg_a', 'delta_b_rg_a', 'delta_w_rg_i', 'delta_b_rg_i', 'delta_rg_lambda', 'delta_w_out', 'delta_ln_ffn_g', 'delta_ln_ffn_b', 'delta_w_mlp_in', 'delta_w_mlp_out', 'new_m_ln_mix_g', 'new_m_ln_mix_b', 'new_m_w_in', 'new_m_w_pool', 'new_m_pool_scale', 'new_m_conv_w', 'new_m_conv_b', 'new_m_w_rg_a', 'new_m_b_rg_a', 'new_m_w_rg_i', 'new_m_b_rg_i', 'new_m_rg_lambda', 'new_m_w_out', 'new_m_ln_ffn_g', 'new_m_ln_ffn_b', 'new_m_w_mlp_in', 'new_m_w_mlp_out', 'new_v_ln_mix_g', 'new_v_ln_mix_b', 'new_v_w_in', 'new_v_w_pool', 'new_v_pool_scale', 'new_v_conv_w', 'new_v_conv_b', 'new_v_w_rg_a', 'new_v_b_rg_a', 'new_v_w_rg_i', 'new_v_b_rg_i', 'new_v_rg_lambda', 'new_v_w_out', 'new_v_ln_ffn_g', 'new_v_ln_ffn_b', 'new_v_w_mlp_in', 'new_v_w_mlp_out']
TWIN_LEAF_KINDS = {'loss': 'loss', 'grad_x': 'grad_x', 'grad_ln_mix_g': 'grad_w', 'grad_ln_mix_b': 'grad_w', 'grad_w_in': 'grad_w', 'grad_w_pool': 'grad_w', 'grad_pool_scale': 'grad_w', 'grad_conv_w': 'grad_w', 'grad_conv_b': 'grad_w', 'grad_w_rg_a': 'grad_w', 'grad_b_rg_a': 'grad_w', 'grad_w_rg_i': 'grad_w', 'grad_b_rg_i': 'grad_w', 'grad_rg_lambda': 'grad_w', 'grad_w_out': 'grad_w', 'grad_ln_ffn_g': 'grad_w', 'grad_ln_ffn_b': 'grad_w', 'grad_w_mlp_in': 'grad_w', 'grad_w_mlp_out': 'grad_w', 'delta_ln_mix_g': 'delta_w', 'delta_ln_mix_b': 'delta_w', 'delta_w_in': 'delta_w', 'delta_w_pool': 'delta_w', 'delta_pool_scale': 'delta_w', 'delta_conv_w': 'delta_w', 'delta_conv_b': 'delta_w', 'delta_w_rg_a': 'delta_w', 'delta_b_rg_a': 'delta_w', 'delta_w_rg_i': 'delta_w', 'delta_b_rg_i': 'delta_w', 'delta_rg_lambda': 'delta_w', 'delta_w_out': 'delta_w', 'delta_ln_ffn_g': 'delta_w', 'delta_ln_ffn_b': 'delta_w', 'delta_w_mlp_in': 'delta_w', 'delta_w_mlp_out': 'delta_w', 'new_m_ln_mix_g': 'new_m', 'new_m_ln_mix_b': 'new_m', 'new_m_w_in': 'new_m', 'new_m_w_pool': 'new_m', 'new_m_pool_scale': 'new_m', 'new_m_conv_w': 'new_m', 'new_m_conv_b': 'new_m', 'new_m_w_rg_a': 'new_m', 'new_m_b_rg_a': 'new_m', 'new_m_w_rg_i': 'new_m', 'new_m_b_rg_i': 'new_m', 'new_m_rg_lambda': 'new_m', 'new_m_w_out': 'new_m', 'new_m_ln_ffn_g': 'new_m', 'new_m_ln_ffn_b': 'new_m', 'new_m_w_mlp_in': 'new_m', 'new_m_w_mlp_out': 'new_m', 'new_v_ln_mix_g': 'new_v', 'new_v_ln_mix_b': 'new_v', 'new_v_w_in': 'new_v', 'new_v_w_pool': 'new_v', 'new_v_pool_scale': 'new_v', 'new_v_conv_w': 'new_v', 'new_v_conv_b': 'new_v', 'new_v_w_rg_a': 'new_v', 'new_v_b_rg_a': 'new_v', 'new_v_w_rg_i': 'new_v', 'new_v_b_rg_i': 'new_v', 'new_v_rg_lambda': 'new_v', 'new_v_w_out': 'new_v', 'new_v_ln_ffn_g': 'new_v', 'new_v_ln_ffn_b': 'new_v', 'new_v_w_mlp_in': 'new_v', 'new_v_w_mlp_out': 'new_v'}


def _forward(args):
    return _fwd_reference(*[args[k] for k in FWD_PARAMS])


def _output_shape():
    def fwd():
        inp = _fwd_setup_inputs(0)
        return _fwd_reference(*[inp[k] for k in FWD_PARAMS])
    out = _jax.eval_shape(fwd)
    return out.shape, out.dtype

N_MICROBATCH = 1
ADAM_LR = 0.001
ADAM_B1 = 0.9
ADAM_B2 = 0.999
ADAM_EPS = 1e-08
ADAM_WD = 0.01
ADAM_STEP = 10
PER_EXAMPLE_BATCH_AXIS = {'x': 0, 'loss_target': 0}
SHARED_INPUTS = []
_WEIGHT_DTYPES = {'ln_mix_g': _jnp.float32, 'ln_mix_b': _jnp.float32, 'w_in': _jnp.float32, 'w_pool': _jnp.float32, 'pool_scale': _jnp.float32, 'conv_w': _jnp.float32, 'conv_b': _jnp.float32, 'w_rg_a': _jnp.float32, 'b_rg_a': _jnp.float32, 'w_rg_i': _jnp.float32, 'b_rg_i': _jnp.float32, 'rg_lambda': _jnp.float32, 'w_out': _jnp.float32, 'ln_ffn_g': _jnp.float32, 'ln_ffn_b': _jnp.float32, 'w_mlp_in': _jnp.float32, 'w_mlp_out': _jnp.float32}
MOMENT_SCALE = {'ln_mix_g': 5.734009e-01, 'ln_mix_b': 2.804782e-01, 'w_in': 2.952626e-02, 'w_pool': 3.693059e-02, 'pool_scale': 3.710622e-02, 'conv_w': 2.747398e-02, 'conv_b': 5.346079e-01, 'w_rg_a': 6.854341e-03, 'b_rg_a': 5.136600e-03, 'w_rg_i': 1.241745e-02, 'b_rg_i': 6.432168e-03, 'rg_lambda': 9.425751e-03, 'w_out': 5.543016e-02, 'ln_ffn_g': 1.599607e+01, 'ln_ffn_b': 1.583469e+00, 'w_mlp_in': 1.827158e-02, 'w_mlp_out': 4.646896e-02}


def _to_microbatches(a, axis):
    t = _jnp.moveaxis(a, axis, 0)
    t = t.reshape((N_MICROBATCH, t.shape[0] // N_MICROBATCH) + t.shape[1:])
    return _jnp.moveaxis(t, 1, axis + 1)


def setup_inputs(seed: int = 0) -> dict:
    inp = _fwd_setup_inputs(seed)
    key = _jax.random.fold_in(_jax.random.key(seed), 7919)
    shape, _ = _output_shape()
    out = dict(inp)
    out["loss_target"] = _jax.random.normal(_jax.random.fold_in(key, 0), shape, _jnp.float32)
    for i, name in enumerate(TWIN_WEIGHTS):
        w = inp[name].astype(_jnp.float32)
        if MOMENT_SCALE is None:
            s = _jnp.sqrt(_jnp.mean(_jnp.square(w)) + 1e-30)
        else:
            s = MOMENT_SCALE[name]
        km, kv = _jax.random.split(_jax.random.fold_in(key, i + 1))
        out[name] = w
        out["m_" + name] = s * _jax.random.normal(km, w.shape, _jnp.float32)
        out["v_" + name] = (s * s) * _jax.random.uniform(kv, w.shape, _jnp.float32, 0.5, 1.5)
    if N_MICROBATCH > 1:
        for name, axis in PER_EXAMPLE_BATCH_AXIS.items():
            out[name] = _to_microbatches(out[name], axis)
    return {'x': out['x'], 'ln_mix_g': out['ln_mix_g'], 'ln_mix_b': out['ln_mix_b'], 'w_in': out['w_in'], 'w_pool': out['w_pool'], 'pool_scale': out['pool_scale'], 'conv_w': out['conv_w'], 'conv_b': out['conv_b'], 'w_rg_a': out['w_rg_a'], 'b_rg_a': out['b_rg_a'], 'w_rg_i': out['w_rg_i'], 'b_rg_i': out['b_rg_i'], 'rg_lambda': out['rg_lambda'], 'w_out': out['w_out'], 'ln_ffn_g': out['ln_ffn_g'], 'ln_ffn_b': out['ln_ffn_b'], 'w_mlp_in': out['w_mlp_in'], 'w_mlp_out': out['w_mlp_out'], 'loss_target': out['loss_target'], 'm_ln_mix_g': out['m_ln_mix_g'], 'm_ln_mix_b': out['m_ln_mix_b'], 'm_w_in': out['m_w_in'], 'm_w_pool': out['m_w_pool'], 'm_pool_scale': out['m_pool_scale'], 'm_conv_w': out['m_conv_w'], 'm_conv_b': out['m_conv_b'], 'm_w_rg_a': out['m_w_rg_a'], 'm_b_rg_a': out['m_b_rg_a'], 'm_w_rg_i': out['m_w_rg_i'], 'm_b_rg_i': out['m_b_rg_i'], 'm_rg_lambda': out['m_rg_lambda'], 'm_w_out': out['m_w_out'], 'm_ln_ffn_g': out['m_ln_ffn_g'], 'm_ln_ffn_b': out['m_ln_ffn_b'], 'm_w_mlp_in': out['m_w_mlp_in'], 'm_w_mlp_out': out['m_w_mlp_out'], 'v_ln_mix_g': out['v_ln_mix_g'], 'v_ln_mix_b': out['v_ln_mix_b'], 'v_w_in': out['v_w_in'], 'v_w_pool': out['v_w_pool'], 'v_pool_scale': out['v_pool_scale'], 'v_conv_w': out['v_conv_w'], 'v_conv_b': out['v_conv_b'], 'v_w_rg_a': out['v_w_rg_a'], 'v_b_rg_a': out['v_b_rg_a'], 'v_w_rg_i': out['v_w_rg_i'], 'v_b_rg_i': out['v_b_rg_i'], 'v_rg_lambda': out['v_rg_lambda'], 'v_w_out': out['v_w_out'], 'v_ln_ffn_g': out['v_ln_ffn_g'], 'v_ln_ffn_b': out['v_ln_ffn_b'], 'v_w_mlp_in': out['v_w_mlp_in'], 'v_w_mlp_out': out['v_w_mlp_out']}


def _loss(weights, diff, rest, loss_target):
    with _jax.named_scope("forward"):
        args = {**rest, TWIN_DIFF_INPUT: diff, **{k: w.astype(_WEIGHT_DTYPES[k]) for k, w in weights.items()}}
        y = _forward(args)
    with _jax.named_scope("loss_head"):
        err = _jnp.square(y.astype(_jnp.float32) - loss_target)
        return 0.5 * _jnp.sum(_jnp.mean(err, axis=-1)) if err.ndim else 0.5 * err


def _adamw(w, g, m, v):
    m = ADAM_B1 * m + (1.0 - ADAM_B1) * g
    v = ADAM_B2 * v + (1.0 - ADAM_B2) * _jnp.square(g)
    m_hat = m / (1.0 - ADAM_B1 ** ADAM_STEP)
    v_hat = v / (1.0 - ADAM_B2 ** ADAM_STEP)
    delta = -ADAM_LR * (m_hat / (_jnp.sqrt(v_hat) + ADAM_EPS) + ADAM_WD * w)
    return delta, m, v


def reference(x, ln_mix_g, ln_mix_b, w_in, w_pool, pool_scale, conv_w, conv_b, w_rg_a, b_rg_a, w_rg_i, b_rg_i, rg_lambda, w_out, ln_ffn_g, ln_ffn_b, w_mlp_in, w_mlp_out, loss_target, m_ln_mix_g, m_ln_mix_b, m_w_in, m_w_pool, m_pool_scale, m_conv_w, m_conv_b, m_w_rg_a, m_b_rg_a, m_w_rg_i, m_b_rg_i, m_rg_lambda, m_w_out, m_ln_ffn_g, m_ln_ffn_b, m_w_mlp_in, m_w_mlp_out, v_ln_mix_g, v_ln_mix_b, v_w_in, v_w_pool, v_pool_scale, v_conv_w, v_conv_b, v_w_rg_a, v_b_rg_a, v_w_rg_i, v_b_rg_i, v_rg_lambda, v_w_out, v_ln_ffn_g, v_ln_ffn_b, v_w_mlp_in, v_w_mlp_out):
    given = dict(x=x, ln_mix_g=ln_mix_g, ln_mix_b=ln_mix_b, w_in=w_in, w_pool=w_pool, pool_scale=pool_scale, conv_w=conv_w, conv_b=conv_b, w_rg_a=w_rg_a, b_rg_a=b_rg_a, w_rg_i=w_rg_i, b_rg_i=b_rg_i, rg_lambda=rg_lambda, w_out=w_out, ln_ffn_g=ln_ffn_g, ln_ffn_b=ln_ffn_b, w_mlp_in=w_mlp_in, w_mlp_out=w_mlp_out, loss_target=loss_target, m_ln_mix_g=m_ln_mix_g, m_ln_mix_b=m_ln_mix_b, m_w_in=m_w_in, m_w_pool=m_w_pool, m_pool_scale=m_pool_scale, m_conv_w=m_conv_w, m_conv_b=m_conv_b, m_w_rg_a=m_w_rg_a, m_b_rg_a=m_b_rg_a, m_w_rg_i=m_w_rg_i, m_b_rg_i=m_b_rg_i, m_rg_lambda=m_rg_lambda, m_w_out=m_w_out, m_ln_ffn_g=m_ln_ffn_g, m_ln_ffn_b=m_ln_ffn_b, m_w_mlp_in=m_w_mlp_in, m_w_mlp_out=m_w_mlp_out, v_ln_mix_g=v_ln_mix_g, v_ln_mix_b=v_ln_mix_b, v_w_in=v_w_in, v_w_pool=v_w_pool, v_pool_scale=v_pool_scale, v_conv_w=v_conv_w, v_conv_b=v_conv_b, v_w_rg_a=v_w_rg_a, v_b_rg_a=v_b_rg_a, v_w_rg_i=v_w_rg_i, v_b_rg_i=v_b_rg_i, v_rg_lambda=v_rg_lambda, v_w_out=v_w_out, v_ln_ffn_g=v_ln_ffn_g, v_ln_ffn_b=v_ln_ffn_b, v_w_mlp_in=v_w_mlp_in, v_w_mlp_out=v_w_mlp_out)
    weights = {n: given[n] for n in TWIN_WEIGHTS}
    shared = {n: given[n] for n in SHARED_INPUTS}
    per_example = {n: given[n] for n in ['x']}
    grad_fn = _jax.value_and_grad(_loss, argnums=(0, 1))

    def one_microbatch(ex, loss_target):
        ex = dict(ex)
        diff = ex.pop(TWIN_DIFF_INPUT)
        return grad_fn(weights, diff, {**shared, **ex}, loss_target)

    if N_MICROBATCH == 1:
        loss, (grad_w, grad_x) = one_microbatch(per_example, given["loss_target"])
    else:
        def body(carry, xs):
            loss_sum, grad_sum = carry
            l_k, (gw_k, gx_k) = one_microbatch(xs[0], xs[1])
            with _jax.named_scope("update"):
                return (loss_sum + l_k, _jax.tree.map(_jnp.add, grad_sum, gw_k)), gx_k

        init = (_jnp.zeros((), _jnp.float32), _jax.tree.map(_jnp.zeros_like, weights))
        (loss, grad_w), grad_x = _jax.lax.scan(body, init, (per_example, given["loss_target"]))
    with _jax.named_scope("update"):
        delta_w, new_m, new_v = {}, {}, {}
        for n in TWIN_WEIGHTS:
            delta_w[n], new_m[n], new_v[n] = _adamw(weights[n], grad_w[n], given["m_" + n], given["v_" + n])
    return (loss, grad_x, *[grad_w[n] for n in TWIN_WEIGHTS], *[delta_w[n] for n in TWIN_WEIGHTS],
            *[new_m[n] for n in TWIN_WEIGHTS], *[new_v[n] for n in TWIN_WEIGHTS])
```

```python
import functools

import jax
import jax.numpy as jnp
from jax import lax
from jax.experimental import pallas as pl
from jax.experimental.pallas import tpu as pltpu

F32 = jnp.float32
BF16 = jnp.bfloat16
MESH = pl.DeviceIdType.MESH
ANY = pl.BlockSpec(memory_space=pl.ANY)

N_DEV = 8
POOL_WINDOWS = (2, 4, 8, 16)
N_HEADS = 8
RG_C = 8.0
LN_EPS = 1e-5
ALPHA = 2.0 ** 0.25
ADAM_LR = 0.001
ADAM_B1 = 0.9
ADAM_B2 = 0.999
ADAM_EPS = 1e-08
ADAM_WD = 0.01
ADAM_STEP = 10

VMEM_LIMIT = 56 * 1024 * 1024
WIN_HALO = 16
CONV_HALO = 8
SMALL_ROWS = 16


def _params(n_grid):
    return pltpu.CompilerParams(dimension_semantics=("arbitrary",) * n_grid, vmem_limit_bytes=VMEM_LIMIT)


def _shift(v, j):
    n = v.shape[0]
    s = (-j) % n
    return v if s == 0 else pltpu.roll(v, s, 0)


def _expm1(x):
    poly = x * (1.0 + x * (0.5 + x * (1.0 / 6.0 + x * (1.0 / 24.0 + x * (1.0 / 120.0)))))
    return jnp.where(jnp.abs(x) < 0.1, poly, jnp.exp(x) - 1.0)


def _softplus(z):
    e = jnp.exp(-jnp.abs(z))
    u = 1.0 + e
    log1p = jnp.where(u == 1.0, e, jnp.log(u) * (e / jnp.where(u == 1.0, 1.0, u - 1.0)))
    return jnp.maximum(z, 0.0) + log1p


_GELU_C = 0.7978845608028654
_GELU_K = 0.044715


def _gelu_and_grad(x):
    x2 = x * x
    t = jnp.tanh(_GELU_C * (x + _GELU_K * x * x2))
    g = 0.5 * x * (1.0 + t)
    dg = 0.5 * (1.0 + t) + 0.5 * x * (1.0 - t * t) * (_GELU_C * (1.0 + 3.0 * _GELU_K * x2))
    return g, dg


def _ln_fwd(z, g, b):
    mu = jnp.mean(z, axis=-1, keepdims=True)
    zc = z - mu
    var = jnp.mean(zc * zc, axis=-1, keepdims=True)
    rstd = lax.rsqrt(var + LN_EPS)
    xhat = zc * rstd
    return xhat * g + b, xhat, rstd


def _ln_bwd(dy, xhat, rstd, g):
    dxhat = dy * g
    m1 = jnp.mean(dxhat, axis=-1, keepdims=True)
    m2 = jnp.mean(dxhat * xhat, axis=-1, keepdims=True)
    dz = rstd * (dxhat - m1 - xhat * m2)
    dg = jnp.sum(dy * xhat, axis=0, keepdims=True)
    db = jnp.sum(dy, axis=0, keepdims=True)
    return dz, dg, db


def _acc_rows(ref, first, val):
    @pl.when(first)
    def _():
        ref[...] = val

    @pl.when(jnp.logical_not(first))
    def _():
        ref[...] += val


def _matmul(name, a, b, a_spec, b_spec, *, grid, ta=False, tb=False, acc_shape, extras=(), out_shape, out_specs,
            epilogue):
    ni, nj, nk = grid
    n_ex = len(extras)
    n_out = len(out_shape)
    dims = (((0 if ta else 1,), (1 if tb else 0,)), ((), ()))

    def body(a_ref, b_ref, *rest):
        ex_refs = rest[:n_ex]
        out_refs = rest[n_ex:n_ex + n_out]
        i = pl.program_id(0)
        k = pl.program_id(2)
        part = lax.dot_general(a_ref[...].astype(BF16), b_ref[...].astype(BF16), dims, preferred_element_type=F32)
        if nk == 1:
            epilogue(part, i, ex_refs, out_refs)
        else:
            acc_ref = rest[n_ex + n_out]

            @pl.when(k == 0)
            def _():
                acc_ref[...] = part

            @pl.when(k > 0)
            def _():
                acc_ref[...] += part

            @pl.when(k == nk - 1)
            def _():
                epilogue(acc_ref[...], i, ex_refs, out_refs)

    return pl.pallas_call(
        body, name=name, grid=(ni, nj, nk),
        in_specs=[a_spec, b_spec] + [s for _, s in extras],
        out_specs=list(out_specs), out_shape=list(out_shape),
        scratch_shapes=[] if nk == 1 else [pltpu.VMEM(acc_shape, F32)],
        compiler_params=_params(3),
    )(a, b, *[x for x, _ in extras])


def _bs(shape, fn):
    return pl.BlockSpec(shape, fn)


def _where_am_i():
    x, y, c = lax.axis_index("x"), lax.axis_index("y"), lax.axis_index("c")
    return x, y, c


def _dev_index(p):
    return 4 * p[0] + 2 * p[1] + p[2]


def _slab(ref, axis, idx, size):
    sl = [slice(None)] * len(ref.shape)
    sl[axis] = pl.ds(idx * size, size)
    return ref.at[tuple(sl)]


def _all_gather(name, items):
    n = len(items)
    shapes = []
    for shard, axis in items:
        s = list(shard.shape)
        s[axis] *= N_DEV
        shapes.append(jax.ShapeDtypeStruct(tuple(s), shard.dtype))

    def body(*refs):
        in_refs, out_refs = refs[:n], refs[n:2 * n]
        send_sems, recv_sems, local_sems = refs[2 * n:]
        x, y, c = _where_am_i()
        me, sibling = (x, y, c), (x, y, 1 - c)
        chips = [(1 - x, y), (x, 1 - y), (1 - x, 1 - y)]

        def blk(a, p):
            axis = items[a][1]
            return _slab(out_refs[a], axis, _dev_index(p), items[a][0].shape[axis])

        def copy(a, k, block, to, src=None):
            return pltpu.make_async_remote_copy(
                src_ref=blk(a, block) if src is None else src, dst_ref=blk(a, block),
                send_sem=send_sems.at[a, k], recv_sem=recv_sems.at[a, k], device_id=to, device_id_type=MESH)

        mine = [pltpu.make_async_copy(in_refs[a], blk(a, me), local_sems.at[a]) for a in range(n)]
        for cp in mine:
            cp.start()
        first = []
        for a in range(n):
            first.append(copy(a, 0, me, sibling, src=in_refs[a]))
            first += [copy(a, 1 + j, me, (*chip, c), src=in_refs[a]) for j, chip in enumerate(chips)]
        for cp in first:
            cp.start()
        passed = []
        for a in range(n):
            for j, chip in enumerate(chips):
                copy(a, 1 + j, (*chip, c), me).wait_recv()
                fw = copy(a, 4 + j, (*chip, c), sibling)
                fw.start()
                passed.append(fw)
        for a in range(n):
            copy(a, 0, sibling, me).wait_recv()
            for j, chip in enumerate(chips):
                copy(a, 4 + j, (*chip, 1 - c), me).wait_recv()
        for cp in first + passed:
            cp.wait_send()
        for cp in mine:
            cp.wait()

    outs = pl.pallas_call(
        body, name=name, out_shape=shapes, in_specs=[ANY] * n, out_specs=[ANY] * n,
        scratch_shapes=[pltpu.SemaphoreType.DMA((n, 7)), pltpu.SemaphoreType.DMA((n, 7)),
                        pltpu.SemaphoreType.DMA((n,))],
    )(*[s for s, _ in items])
    return list(outs)


def _all_to_all(name, items):
    n = len(items)
    shapes, sizes = [], []
    for full, axis in items:
        s = list(full.shape)
        s[axis] //= N_DEV
        sizes.append(s[axis])
        shapes.append(jax.ShapeDtypeStruct((N_DEV, *s), full.dtype))

    def body(*refs):
        in_refs, out_refs = refs[:n], refs[n:2 * n]
        send_sems, recv_sems, local_sems = refs[2 * n:]
        x, y, c = _where_am_i()
        me = (x, y, c)
        my_idx = _dev_index(me)

        def peer(mask):
            return tuple((1 - v) if (mask >> (2 - b)) & 1 else v for b, v in enumerate(me))

        def copy(a, mask, sender, to):
            src = _slab(in_refs[a], items[a][1], _dev_index(to), sizes[a])
            dst = out_refs[a].at[_dev_index(sender)]
            return pltpu.make_async_remote_copy(
                src_ref=src, dst_ref=dst, send_sem=send_sems.at[a, mask - 1], recv_sem=recv_sems.at[a, mask - 1],
                device_id=to, device_id_type=MESH)

        mine = [pltpu.make_async_copy(_slab(in_refs[a], items[a][1], my_idx, sizes[a]), out_refs[a].at[my_idx],
                                      local_sems.at[a]) for a in range(n)]
        for cp in mine:
            cp.start()
        sends = [copy(a, mask, me, peer(mask)) for a in range(n) for mask in range(1, N_DEV)]
        for cp in sends:
            cp.start()
        for a in range(n):
            for mask in range(1, N_DEV):
                copy(a, mask, peer(mask), me).wait_recv()
        for cp in sends:
            cp.wait_send()
        for cp in mine:
            cp.wait()

    outs = pl.pallas_call(
        body, name=name, out_shape=shapes, in_specs=[ANY] * n, out_specs=[ANY] * n,
        scratch_shapes=[pltpu.SemaphoreType.DMA((n, 7)), pltpu.SemaphoreType.DMA((n, 7)),
                        pltpu.SemaphoreType.DMA((n,))],
    )(*[f for f, _ in items])
    return list(outs)


def _win_sum(ext, w, off):
    s = ext + _shift(ext, -1)
    if w >= 4:
        s = _shift(s, -1) + _shift(s, 1)
    if w >= 8:
        s = _shift(s, -2) + _shift(s, 2)
    if w >= 16:
        s = _shift(s, -4) + _shift(s, 4)
    return _shift(s, off) if off else s


def _inv_count(r0, t, w, seq):
    pos = r0 + lax.broadcasted_iota(jnp.int32, (t, 1), 0)
    cnt = jnp.minimum(pos + w // 2, seq) - jnp.maximum(pos - w // 2, 0)
    return 1.0 / cnt.astype(F32)


def _pool_fwd(p3, w_pool, pool_scale, seq, d_model):
    dp = d_model // 2
    pg = dp // len(POOL_WINDOWS)
    t = min(128, seq)
    n_chunks = seq // t
    h = WIN_HALO

    def body(u_ref, w_ref, sc_ref, d_ref, y_ref, pad_ref):
        g = pl.program_id(0)
        zeros = jnp.zeros((h, pg), F32)
        pad_ref[0:h, :] = zeros
        pad_ref[h + seq:h + seq + h, :] = zeros

        def fill(ci, _):
            r0 = pl.multiple_of(ci * t, t)
            pad_ref[pl.ds(h + r0, t), :] = u_ref[pl.ds(r0, t), :]
            return 0

        lax.fori_loop(0, n_chunks, fill, 0)
        wmat = w_ref[...]
        scale = sc_ref[...]
        for gi, w in enumerate(POOL_WINDOWS):
            @pl.when(g == gi)
            def _(w=w):
                def chunk(ci, _):
                    r0 = pl.multiple_of(ci * t, t)
                    ext = pad_ref[pl.ds(r0, t + 2 * h), :]
                    mean = _win_sum(ext, w, 0)[h:h + t, :] * _inv_count(r0, t, w, seq)
                    d = (mean - ext[h:h + t, :]).astype(BF16)
                    d_ref[pl.ds(r0, t), :] = d
                    q = jnp.dot(d, wmat, preferred_element_type=F32)
                    y_ref[pl.ds(r0, t), :] = (q * scale).astype(BF16)
                    return 0

                lax.fori_loop(0, n_chunks, chunk, 0)

    return pl.pallas_call(
        body, name="pool_fwd", grid=(len(POOL_WINDOWS),),
        in_specs=[_bs((None, seq, pg), lambda g: (0, 0, g)), _bs((None, pg, pg), lambda g: (g, 0, 0)),
                  _bs((1, pg), lambda g: (0, g))],
        out_specs=[_bs((seq, pg), lambda g: (0, g)), _bs((seq, pg), lambda g: (0, g))],
        out_shape=[jax.ShapeDtypeStruct((seq, dp), BF16), jax.ShapeDtypeStruct((seq, d_model), BF16)],
        scratch_shapes=[pltpu.VMEM((seq + 2 * h, pg), F32)],
        compiler_params=_params(1),
    )(p3, w_pool, pool_scale)


def _pool_bwd(d, dy, w_pool, pool_scale, seq, d_model):
    dp = d_model // 2
    pg = dp // len(POOL_WINDOWS)
    t = min(128, seq)
    n_chunks = seq // t
    h = WIN_HALO
    tn_dims = (((0,), (0,)), ((), ()))
    nt_dims = (((1,), (1,)), ((), ()))

    def body(d_ref, dy_ref, w_ref, sc_ref, du_ref, dw_ref, dsc_ref, pad_ref, dd_ref):
        g = pl.program_id(0)
        zeros = jnp.zeros((h, pg), F32)
        pad_ref[0:h, :] = zeros
        pad_ref[h + seq:h + seq + h, :] = zeros
        wmat = w_ref[...]
        scale = sc_ref[...]
        for gi, w in enumerate(POOL_WINDOWS):
            @pl.when(g == gi)
            def _(w=w):
                dw_ref[...] = jnp.zeros((pg, pg), F32)

                def first(ci, dsc):
                    r0 = pl.multiple_of(ci * t, t)
                    dv = d_ref[pl.ds(r0, t), :]
                    dyv = dy_ref[pl.ds(r0, t), :]
                    q = jnp.dot(dv, wmat, preferred_element_type=F32)
                    dsc = dsc + jnp.sum(dyv * q, axis=0, keepdims=True)
                    dq = (dyv * scale).astype(BF16)
                    dw_ref[...] += lax.dot_general(dv, dq, tn_dims, preferred_element_type=F32)
                    dd = lax.dot_general(dq, wmat, nt_dims, preferred_element_type=F32)
                    dd_ref[pl.ds(r0, t), :] = dd
                    pad_ref[pl.ds(h + r0, t), :] = dd * _inv_count(r0, t, w, seq)
                    return dsc

                dsc_ref[...] = lax.fori_loop(0, n_chunks, first, jnp.zeros((1, pg), F32))

                def second(ci, _):
                    r0 = pl.multiple_of(ci * t, t)
                    ext = pad_ref[pl.ds(r0, t + 2 * h), :]
                    back = _win_sum(ext, w, 1)[h:h + t, :]
                    du_ref[pl.ds(r0, t), :] = (back - dd_ref[pl.ds(r0, t), :]).astype(BF16)
                    return 0

                lax.fori_loop(0, n_chunks, second, 0)

    return pl.pallas_call(
        body, name="pool_bwd", grid=(len(POOL_WINDOWS),),
        in_specs=[_bs((seq, pg), lambda g: (0, g)), _bs((seq, pg), lambda g: (0, g)),
                  _bs((None, pg, pg), lambda g: (g, 0, 0)), _bs((1, pg), lambda g: (0, g))],
        out_specs=[_bs((None, seq, pg), lambda g: (2, 0, g)), _bs((None, pg, pg), lambda g: (g, 0, 0)),
                   _bs((1, pg), lambda g: (0, g))],
        out_shape=[jax.ShapeDtypeStruct((3, seq, dp), BF16), jax.ShapeDtypeStruct((len(POOL_WINDOWS), pg, pg), F32),
                   jax.ShapeDtypeStruct((1, dp), F32)],
        scratch_shapes=[pltpu.VMEM((seq + 2 * h, pg), F32), pltpu.VMEM((seq, pg), F32)],
        compiler_params=_params(1),
    )(d, dy, w_pool, pool_scale)


def _segment_scan(seg_len, loads, hs_refs, ps_refs, stores):
    lanes = hs_refs[0].shape[-1]
    row = lax.broadcasted_iota(jnp.int32, (8, lanes), 0)

    def pos_of(n, s):
        return s if n == 0 else seg_len - 1 - s

    def step1(s, carry):
        out = []
        for n in range(2):
            hh, pp = carry[n]
            pos = pos_of(n, s)
            coef, inp = loads[n](pos)
            hh = coef * hh + inp
            pp = coef * pp
            hs_refs[n][pl.ds(pl.multiple_of(pos * 8, 8), 8), :] = hh
            ps_refs[n][pl.ds(pl.multiple_of(pos * 8, 8), 8), :] = pp
            out.append((hh, pp))
        return tuple(out)

    init = tuple((jnp.zeros((8, lanes), F32), jnp.ones((8, lanes), F32)) for _ in range(2))
    ends = lax.fori_loop(0, seg_len, step1, init, unroll=4)

    entry = []
    for n in range(2):
        bb, aa = ends[n]
        for sh in (1, 2, 4):
            if n == 0:
                ok = row >= sh
                ap = jnp.where(ok, pltpu.roll(aa, sh, 0), 1.0)
                bp = jnp.where(ok, pltpu.roll(bb, sh, 0), 0.0)
            else:
                ok = row < 8 - sh
                ap = jnp.where(ok, pltpu.roll(aa, 8 - sh, 0), 1.0)
                bp = jnp.where(ok, pltpu.roll(bb, 8 - sh, 0), 0.0)
            bb = aa * bp + bb
            aa = aa * ap
        if n == 0:
            entry.append(jnp.where(row >= 1, pltpu.roll(bb, 1, 0), 0.0))
        else:
            entry.append(jnp.where(row < 7, pltpu.roll(bb, 7, 0), 0.0))

    def step2(s, _):
        for n in range(2):
            at = pl.ds(pl.multiple_of(s * 8, 8), 8)
            stores[n](s, hs_refs[n][at, :] + ps_refs[n][at, :] * entry[n])
        return 0

    lax.fori_loop(0, seg_len, step2, 0, unroll=4)


def _gates(xc, n, wa_ref, wi_ref, pk_ref, sp):
    xcb = xc.astype(BF16)
    r = jax.nn.sigmoid(jnp.dot(xcb, wa_ref[n], preferred_element_type=F32) + pk_ref[pl.ds(4 + n, 1), :])
    i = jax.nn.sigmoid(jnp.dot(xcb, wi_ref[n], preferred_element_type=F32) + pk_ref[pl.ds(6 + n, 1), :])
    log_a = (-RG_C * r) * sp[n]
    a = jnp.exp(log_a)
    m = jnp.sqrt(-_expm1(2.0 * log_a))
    return xcb, r, i, a, m


def _conv_chunk(upad_ref, pk_ref, cb, r0, t):
    ext = upad_ref[pl.ds(r0, t + 2 * CONV_HALO), :]
    acc = pk_ref[pl.ds(1, 1), :] * ext
    for k in (0, 2, 3):
        acc = acc + pk_ref[pl.ds(k, 1), :] * _shift(ext, k - 1)
    return acc[CONV_HALO:CONV_HALO + t, :] + cb, ext


def _lru_fwd(p3, y_in, pack, conv_b, wa, wi, seq, d_model):
    dl = d_model // 2
    lh = dl // N_HEADS
    t = min(128, seq)
    n_chunks = seq // t
    seg = seq // 8
    hal = CONV_HALO
    first_rec_block = (d_model - dl) // lh

    def body(ur_ref, ug_ref, pk_ref, cb_ref, wa_ref, wi_ref, yin_ref, y_ref, h0_ref, h1_ref,
             upad, a_scr, b_scr, hs0, hs1, ps0, ps1):
        del yin_ref
        zeros = jnp.zeros((hal, lh), F32)
        upad[0:hal, :] = zeros
        upad[hal + seq:hal + seq + hal, :] = zeros
        for ref in (h0_ref, h1_ref):
            ref[0:hal, :] = zeros
            ref[hal + seq:hal + seq + hal, :] = zeros

        def fill(ci, _):
            r0 = pl.multiple_of(ci * t, t)
            upad[pl.ds(hal + r0, t), :] = ur_ref[pl.ds(r0, t), :]
            return 0

        lax.fori_loop(0, n_chunks, fill, 0)
        cb = cb_ref[...]
        sp = [_softplus(-pk_ref[pl.ds(8 + n, 1), :]) for n in range(2)]

        def chunk(ci, _):
            r0 = pl.multiple_of(ci * t, t)
            xc, _ext = _conv_chunk(upad, pk_ref, cb, r0, t)
            for n in range(2):
                _, _, i, a, m = _gates(xc, n, wa_ref, wi_ref, pk_ref, sp)
                a_scr[n, pl.ds(r0, t), :] = a
                b_scr[n, pl.ds(r0, t), :] = (m * i) * xc
            return 0

        lax.fori_loop(0, n_chunks, chunk, 0)

        def load(n):
            return lambda pos: (a_scr[n, pl.ds(pos, 8, stride=seg), :], b_scr[n, pl.ds(pos, 8, stride=seg), :])

        def store(n, ref):
            def put(s, v):
                ref[pl.ds(hal + s, 8, stride=seg), :] = v
            return put

        _segment_scan(seg, [load(0), load(1)], [hs0, hs1], [ps0, ps1], [store(0, h0_ref), store(1, h1_ref)])

        def out(ci, _):
            r0 = pl.multiple_of(ci * t, t)
            hsum = h0_ref[pl.ds(hal + r0, t), :] + h1_ref[pl.ds(hal + r0, t), :]
            gl, _dg = _gelu_and_grad(ug_ref[pl.ds(r0, t), :])
            y_ref[pl.ds(r0, t), :] = (hsum * gl).astype(BF16)
            return 0

        lax.fori_loop(0, n_chunks, out, 0)

    return pl.pallas_call(
        body, name="lru_fwd", grid=(N_HEADS,),
        in_specs=[_bs((None, seq, lh), lambda h: (1, 0, h)), _bs((None, seq, lh), lambda h: (2, 0, h)),
                  _bs((None, SMALL_ROWS, lh), lambda h: (h, 0, 0)), _bs((1, lh), lambda h: (0, h)),
                  _bs((2, None, lh, lh), lambda h: (0, h, 0, 0)), _bs((2, None, lh, lh), lambda h: (0, h, 0, 0)),
                  ANY],
        out_specs=[_bs((seq, lh), lambda h: (0, first_rec_block + h)),
                   _bs((seq + 2 * hal, lh), lambda h: (0, h)), _bs((seq + 2 * hal, lh), lambda h: (0, h))],
        out_shape=[jax.ShapeDtypeStruct((seq, d_model), BF16), jax.ShapeDtypeStruct((seq + 2 * hal, dl), F32),
                   jax.ShapeDtypeStruct((seq + 2 * hal, dl), F32)],
        scratch_shapes=[pltpu.VMEM((seq + 2 * hal, lh), F32), pltpu.VMEM((2, seq, lh), F32),
                        pltpu.VMEM((2, seq, lh), F32)] + [pltpu.VMEM((seq, lh), F32)] * 4,
        input_output_aliases={6: 0},
        compiler_params=_params(1),
    )(p3, p3, pack, conv_b, wa, wi, y_in)


def _lru_bwd(p3, dy, h0p, h1p, dproj_in, pack, conv_b, wa, wi, seq, d_model):
    dl = d_model // 2
    lh = dl // N_HEADS
    t = min(128, seq)
    n_chunks = seq // t
    seg = seq // 8
    hal = CONV_HALO
    first_rec_block = (d_model - dl) // lh
    tn_dims = (((0,), (0,)), ((), ()))
    nt_dims = (((1,), (1,)), ((), ()))

    def body(ur_ref, ug_ref, dy_ref, h0_ref, h1_ref, pk_ref, cb_ref, wa_ref, wi_ref, din_ref,
             dpr_ref, dpk_ref, dcb_ref, dwa_ref, dwi_ref,
             upad, a_scr, dh_scr, g_scr, dxc_pad, hs0, hs1, ps0, ps1):
        del din_ref
        zeros = jnp.zeros((hal, lh), F32)
        for ref in (upad, dxc_pad):
            ref[0:hal, :] = zeros
            ref[hal + seq:hal + seq + hal, :] = zeros
        for n in range(2):
            a_scr[n, 0:hal, :] = zeros
            a_scr[n, hal + seq:hal + seq + hal, :] = zeros

        def fill(ci, _):
            r0 = pl.multiple_of(ci * t, t)
            upad[pl.ds(hal + r0, t), :] = ur_ref[pl.ds(r0, t), :]
            return 0

        lax.fori_loop(0, n_chunks, fill, 0)
        cb = cb_ref[...]
        lam = [pk_ref[pl.ds(8 + n, 1), :] for n in range(2)]
        sp = [_softplus(-lam[n]) for n in range(2)]

        def chunk1(ci, _):
            r0 = pl.multiple_of(ci * t, t)
            xc, _ext = _conv_chunk(upad, pk_ref, cb, r0, t)
            for n in range(2):
                _, _, _, a, _ = _gates(xc, n, wa_ref, wi_ref, pk_ref, sp)
                a_scr[n, pl.ds(hal + r0, t), :] = a
            hsum = h0_ref[pl.ds(hal + r0, t), :] + h1_ref[pl.ds(hal + r0, t), :]
            gl, dgl = _gelu_and_grad(ug_ref[pl.ds(r0, t), :])
            dyv = dy_ref[pl.ds(r0, t), :]
            dh_scr[pl.ds(r0, t), :] = dyv * gl
            dpr_ref[1, pl.ds(r0, t), :] = ((dyv * hsum) * dgl).astype(BF16)
            return 0

        lax.fori_loop(0, n_chunks, chunk1, 0)

        def load(n):
            def get(pos):
                coef = a_scr[n, pl.ds(hal + pos + (1 if n == 0 else -1), 8, stride=seg), :]
                return coef, dh_scr[pl.ds(pos, 8, stride=seg), :]
            return get

        def store(n):
            def put(s, v):
                g_scr[n, pl.ds(s, 8, stride=seg), :] = v
            return put

        _segment_scan(seg, [load(1), load(0)], [hs0, hs1], [ps0, ps1], [store(1), store(0)])

        dwa_ref[...] = jnp.zeros((2, lh, lh), F32)
        dwi_ref[...] = jnp.zeros((2, lh, lh), F32)

        def chunk3(ci, carry):
            dba, dbi, dlam, dcb = carry
            r0 = pl.multiple_of(ci * t, t)
            xc, _ext = _conv_chunk(upad, pk_ref, cb, r0, t)
            dxc = jnp.zeros((t, lh), F32)
            dba, dbi, dlam = list(dba), list(dbi), list(dlam)
            for n in range(2):
                xcb, r, i, a, m = _gates(xc, n, wa_ref, wi_ref, pk_ref, sp)
                hext = (h0_ref if n == 0 else h1_ref)[pl.ds(r0, t + 2 * hal), :]
                hprev = _shift(hext, -1 if n == 0 else 1)[hal:hal + t, :]
                gb = g_scr[n, pl.ds(r0, t), :]
                da = gb * hprev
                dm = gb * i * xc
                di = gb * m * xc
                dxc = dxc + gb * (m * i)
                dlog_a = da * a - dm * (a * a) / m
                dr = dlog_a * (-RG_C * sp[n])
                dlam[n] = dlam[n] + jnp.sum(dlog_a * r, axis=0, keepdims=True)
                dpr = dr * r * (1.0 - r)
                dpi = di * i * (1.0 - i)
                dba[n] = dba[n] + jnp.sum(dpr, axis=0, keepdims=True)
                dbi[n] = dbi[n] + jnp.sum(dpi, axis=0, keepdims=True)
                dprb, dpib = dpr.astype(BF16), dpi.astype(BF16)
                dwa_ref[n] += lax.dot_general(xcb, dprb, tn_dims, preferred_element_type=F32)
                dwi_ref[n] += lax.dot_general(xcb, dpib, tn_dims, preferred_element_type=F32)
                dxc = dxc + lax.dot_general(dprb, wa_ref[n], nt_dims, preferred_element_type=F32)
                dxc = dxc + lax.dot_general(dpib, wi_ref[n], nt_dims, preferred_element_type=F32)
            dxc_pad[pl.ds(hal + r0, t), :] = dxc
            dcb = dcb + jnp.sum(dxc, axis=0, keepdims=True)
            return tuple(dba), tuple(dbi), tuple(dlam), dcb

        zr = jnp.zeros((1, lh), F32)
        dba, dbi, dlam, dcb = lax.fori_loop(0, n_chunks, chunk3, ((zr, zr), (zr, zr), (zr, zr), zr))
        dcb_ref[...] = dcb
        for n in range(2):
            dpk_ref[pl.ds(4 + n, 1), :] = dba[n]
            dpk_ref[pl.ds(6 + n, 1), :] = dbi[n]
            dpk_ref[pl.ds(8 + n, 1), :] = dlam[n] * (RG_C * jax.nn.sigmoid(-lam[n]))
        dpk_ref[pl.ds(10, SMALL_ROWS - 10), :] = jnp.zeros((SMALL_ROWS - 10, lh), F32)

        def chunk4(ci, dtap):
            r0 = pl.multiple_of(ci * t, t)
            gext = dxc_pad[pl.ds(r0, t + 2 * hal), :]
            uext = upad[pl.ds(r0, t + 2 * hal), :]
            gmid = gext[hal:hal + t, :]
            du = pk_ref[pl.ds(1, 1), :] * gext
            for k in (0, 2, 3):
                du = du + pk_ref[pl.ds(k, 1), :] * _shift(gext, 1 - k)
            dpr_ref[0, pl.ds(r0, t), :] = du[hal:hal + t, :].astype(BF16)
            out = []
            for k in range(4):
                usl = _shift(uext, k - 1)[hal:hal + t, :]
                out.append(dtap[k] + jnp.sum(gmid * usl, axis=0, keepdims=True))
            return tuple(out)

        dtap = lax.fori_loop(0, n_chunks, chunk4, (zr, zr, zr, zr))
        for k in range(4):
            dpk_ref[pl.ds(k, 1), :] = dtap[k]

    return pl.pallas_call(
        body, name="lru_bwd", grid=(N_HEADS,),
        in_specs=[_bs((None, seq, lh), lambda h: (1, 0, h)), _bs((None, seq, lh), lambda h: (2, 0, h)),
                  _bs((seq, lh), lambda h: (0, first_rec_block + h)),
                  _bs((seq + 2 * hal, lh), lambda h: (0, h)), _bs((seq + 2 * hal, lh), lambda h: (0, h)),
                  _bs((None, SMALL_ROWS, lh), lambda h: (h, 0, 0)), _bs((1, lh), lambda h: (0, h)),
                  _bs((2, None, lh, lh), lambda h: (0, h, 0, 0)), _bs((2, None, lh, lh), lambda h: (0, h, 0, 0)),
                  ANY],
        out_specs=[_bs((2, seq, lh), lambda h: (0, 0, h)), _bs((None, SMALL_ROWS, lh), lambda h: (h, 0, 0)),
                   _bs((1, lh), lambda h: (0, h)),
                   _bs((2, None, lh, lh), lambda h: (0, h, 0, 0)), _bs((2, None, lh, lh), lambda h: (0, h, 0, 0))],
        out_shape=[jax.ShapeDtypeStruct((3, seq, dl), BF16), jax.ShapeDtypeStruct((N_HEADS, SMALL_ROWS, lh), F32),
                   jax.ShapeDtypeStruct((1, dl), F32),
                   jax.ShapeDtypeStruct((2, N_HEADS, lh, lh), F32), jax.ShapeDtypeStruct((2, N_HEADS, lh, lh), F32)],
        scratch_shapes=[pltpu.VMEM((seq + 2 * hal, lh), F32), pltpu.VMEM((2, seq + 2 * hal, lh), F32),
                        pltpu.VMEM((seq, lh), F32), pltpu.VMEM((2, seq, lh), F32),
                        pltpu.VMEM((seq + 2 * hal, lh), F32)] + [pltpu.VMEM((seq, lh), F32)] * 4,
        input_output_aliases={9: 0},
        compiler_params=_params(1),
    )(p3, p3, dy, h0p, h1p, pack, conv_b, wa, wi, dproj_in)


def _adamw_values(w, g, m, v):
    m = ADAM_B1 * m + (1.0 - ADAM_B1) * g
    v = ADAM_B2 * v + (1.0 - ADAM_B2) * (g * g)
    m_hat = m / (1.0 - ADAM_B1 ** ADAM_STEP)
    v_hat = v / (1.0 - ADAM_B2 ** ADAM_STEP)
    delta = -ADAM_LR * (m_hat / (jnp.sqrt(v_hat) + ADAM_EPS) + ADAM_WD * w)
    return delta, m, v


def _sum_adamw(name, parts, w, m, v):
    rows, cols = w.shape
    tr = rows
    while tr * cols * 4 > 1024 * 1024 and tr % 16 == 0:
        tr //= 2

    def body(p_ref, w_ref, m_ref, v_ref, g_ref, d_ref, mo_ref, vo_ref):
        g = p_ref[0].astype(F32)
        for s in range(1, N_DEV):
            g = g + p_ref[s].astype(F32)
        delta, mn, vn = _adamw_values(w_ref[...], g, m_ref[...], v_ref[...])
        g_ref[...] = g
        d_ref[...] = delta
        mo_ref[...] = mn
        vo_ref[...] = vn

    spec = _bs((tr, cols), lambda i: (i, 0))
    return pl.pallas_call(
        body, name=name, grid=(rows // tr,),
        in_specs=[_bs((N_DEV, tr, cols), lambda i: (0, i, 0)), spec, spec, spec],
        out_specs=[spec] * 4, out_shape=[jax.ShapeDtypeStruct((rows, cols), F32)] * 4,
        compiler_params=_params(1),
    )(parts, w, m, v)


def _rows128(a):
    return a.reshape(-1, 128)


def kernel(x, ln_mix_g, ln_mix_b, w_in, w_pool, pool_scale, conv_w, conv_b, w_rg_a, b_rg_a, w_rg_i, b_rg_i, rg_lambda, w_out, ln_ffn_g, ln_ffn_b, w_mlp_in, w_mlp_out, loss_target, m_ln_mix_g, m_ln_mix_b, m_w_in, m_w_pool, m_pool_scale, m_conv_w, m_conv_b, m_w_rg_a, m_b_rg_a, m_w_rg_i, m_b_rg_i, m_rg_lambda, m_w_out, m_ln_ffn_g, m_ln_ffn_b, m_w_mlp_in, m_w_mlp_out, v_ln_mix_g, v_ln_mix_b, v_w_in, v_w_pool, v_pool_scale, v_conv_w, v_conv_b, v_w_rg_a, v_b_rg_a, v_w_rg_i, v_b_rg_i, v_rg_lambda, v_w_out, v_ln_ffn_g, v_ln_ffn_b, v_w_mlp_in, v_w_mlp_out):
    seq, d_model = x.shape[1], x.shape[2]
    dh = d_model // 2
    lh = dh // N_HEADS
    pg = dh // len(POOL_WINDOWS)
    d_ff = w_mlp_in.shape[2] * N_DEV
    assert lh == 128 and conv_w.shape[3] == lh and w_pool.shape[2] * N_DEV == pg
    tm = min(512, seq)
    ni = seq // tm

    xs = x[0]
    tgt = loss_target[0]

    def small_pack(cw, ba, bi, lam):
        return jnp.concatenate([cw.reshape(4, lh), ba.reshape(2, lh), bi.reshape(2, lh), lam.reshape(2, lh),
                                jnp.zeros((SMALL_ROWS - 10, lh), F32)], axis=0)

    pack_mine = small_pack(conv_w, b_rg_a, b_rg_i, rg_lambda)
    win_full, wpool_full, wout_full, pack_full = _all_gather("gather_mixer", [
        (w_in[0].astype(BF16), 1), (w_pool[0].astype(BF16), 1), (w_out[0].astype(BF16), 0),
        (pack_mine[None], 0)])
    w1_full, w2_full = _all_gather("gather_mlp", [(w_mlp_in[0].astype(BF16), 1), (w_mlp_out[0].astype(BF16), 0)])
    wa_b = w_rg_a[0].astype(BF16)
    wi_b = w_rg_i[0].astype(BF16)

    def proj_epi(acc, i, ex, out):
        out[0][...] = acc

    (p3,) = _matmul(
        "proj", xs, win_full, _bs((tm, d_model), lambda i, j, k: (i, 0)), _bs((d_model, dh), lambda i, j, k: (0, j)),
        grid=(ni, 3, 1), acc_shape=None, out_shape=[jax.ShapeDtypeStruct((3, seq, dh), F32)],
        out_specs=[_bs((None, tm, dh), lambda i, j, k: (j, i, 0))], epilogue=proj_epi)

    d_pool, y_half = _pool_fwd(p3, wpool_full, pool_scale, seq, d_model)
    y, h0p, h1p = _lru_fwd(p3, y_half, pack_full, conv_b, wa_b, wi_b, seq, d_model)

    tm2 = min(256, seq)

    def mix_epi(acc, i, ex, out):
        x_ref, g_ref, b_ref = ex
        z = ALPHA * x_ref[...] + acc
        x1, _, _ = _ln_fwd(z, g_ref[...], b_ref[...])
        out[0][...] = z
        out[1][...] = x1
        out[2][...] = x1.astype(BF16)

    row_full = lambda i, j, k: (i, 0)
    vec = lambda i, j, k: (0, 0)
    z1, x1, x1b = _matmul(
        "mix_out", y, wout_full, _bs((tm2, d_model), row_full), _bs((d_model, d_model), vec),
        grid=(seq // tm2, 1, 1), acc_shape=None,
        extras=[(xs, _bs((tm2, d_model), row_full)), (ln_mix_g, _bs((1, d_model), vec)),
                (ln_mix_b, _bs((1, d_model), vec))],
        out_shape=[jax.ShapeDtypeStruct((seq, d_model), F32), jax.ShapeDtypeStruct((seq, d_model), F32),
                   jax.ShapeDtypeStruct((seq, d_model), BF16)],
        out_specs=[_bs((tm2, d_model), row_full)] * 3, epilogue=mix_epi)

    tn = min(1024, d_ff)

    def mlp_in_epi(acc, i, ex, out):
        out[0][...] = acc
        h = jnp.maximum(acc, 0.0)
        out[1][...] = (h * h).astype(BF16)

    pre, hmid = _matmul(
        "mlp_in", x1b, w1_full, _bs((tm, d_model), lambda i, j, k: (i, 0)), _bs((d_model, tn), lambda i, j, k: (0, j)),
        grid=(ni, d_ff // tn, 1), acc_shape=None,
        out_shape=[jax.ShapeDtypeStruct((seq, d_ff), F32), jax.ShapeDtypeStruct((seq, d_ff), BF16)],
        out_specs=[_bs((tm, tn), lambda i, j, k: (i, j))] * 2, epilogue=mlp_in_epi)

    tk = min(1024, d_ff)

    def mlp_out_epi(acc, i, ex, out):
        x1_ref, t_ref, g_ref, b_ref = ex
        z = ALPHA * x1_ref[...] + acc
        g = g_ref[...]
        x2, xhat, rstd = _ln_fwd(z, g, b_ref[...])
        err = x2 - t_ref[...]
        part = 0.5 * jnp.sum(jnp.mean(err * err, axis=-1, keepdims=True), axis=0, keepdims=True)
        dz, dg, db = _ln_bwd(err * (1.0 / d_model), xhat, rstd, g)
        out[0][...] = dz
        out[1][...] = dz.astype(BF16)
        _acc_rows(out[2], i == 0, dg)
        _acc_rows(out[3], i == 0, db)
        _acc_rows(out[4], i == 0, jnp.broadcast_to(part, (8, 128)))

    dz2, dz2b, g_ffn_g, g_ffn_b, loss_part = _matmul(
        "mlp_out", hmid, w2_full, _bs((tm2, tk), lambda i, j, k: (i, k)), _bs((tk, d_model), lambda i, j, k: (k, 0)),
        grid=(seq // tm2, 1, d_ff // tk), acc_shape=(tm2, d_model),
        extras=[(x1, _bs((tm2, d_model), row_full)), (tgt, _bs((tm2, d_model), row_full)),
                (ln_ffn_g, _bs((1, d_model), vec)), (ln_ffn_b, _bs((1, d_model), vec))],
        out_shape=[jax.ShapeDtypeStruct((seq, d_model), F32), jax.ShapeDtypeStruct((seq, d_model), BF16),
                   jax.ShapeDtypeStruct((1, d_model), F32), jax.ShapeDtypeStruct((1, d_model), F32),
                   jax.ShapeDtypeStruct((8, 128), F32)],
        out_specs=[_bs((tm2, d_model), row_full)] * 2 + [_bs((1, d_model), vec)] * 2 + [_bs((8, 128), vec)],
        epilogue=mlp_out_epi)

    def dpre_epi(acc, i, ex, out):
        out[0][...] = (acc * (2.0 * jnp.maximum(ex[0][...], 0.0))).astype(BF16)

    (dpre,) = _matmul(
        "mlp_dpre", dz2b, w2_full, _bs((tm, d_model), lambda i, j, k: (i, 0)), _bs((tn, d_model), lambda i, j, k: (j, 0)),
        grid=(ni, d_ff // tn, 1), tb=True, acc_shape=None,
        extras=[(pre, _bs((tm, tn), lambda i, j, k: (i, j)))],
        out_shape=[jax.ShapeDtypeStruct((seq, d_ff), BF16)], out_specs=[_bs((tm, tn), lambda i, j, k: (i, j))],
        epilogue=dpre_epi)

    def dx1_epi(acc, i, ex, out):
        dz2_ref, z1_ref, g_ref, b_ref = ex
        g = g_ref[...]
        _, xhat, rstd = _ln_fwd(z1_ref[...], g, b_ref[...])
        dz, dg, db = _ln_bwd(ALPHA * dz2_ref[...] + acc, xhat, rstd, g)
        out[0][...] = dz
        out[1][...] = dz.astype(BF16)
        _acc_rows(out[2], i == 0, dg)
        _acc_rows(out[3], i == 0, db)

    dz1, dz1b, g_mix_g, g_mix_b = _matmul(
        "mlp_dx", dpre, w1_full, _bs((tm2, tk), lambda i, j, k: (i, k)), _bs((d_model, tk), lambda i, j, k: (0, k)),
        grid=(seq // tm2, 1, d_ff // tk), tb=True, acc_shape=(tm2, d_model),
        extras=[(dz2, _bs((tm2, d_model), row_full)), (z1, _bs((tm2, d_model), row_full)),
                (ln_mix_g, _bs((1, d_model), vec)), (ln_mix_b, _bs((1, d_model), vec))],
        out_shape=[jax.ShapeDtypeStruct((seq, d_model), F32), jax.ShapeDtypeStruct((seq, d_model), BF16),
                   jax.ShapeDtypeStruct((1, d_model), F32), jax.ShapeDtypeStruct((1, d_model), F32)],
        out_specs=[_bs((tm2, d_model), row_full)] * 2 + [_bs((1, d_model), vec)] * 2, epilogue=dx1_epi)

    def plain_epi(acc, i, ex, out):
        out[0][...] = acc

    tks = min(512, seq)
    nks = seq // tks
    twm = 512
    (g_w2,) = _matmul(
        "grad_w_mlp_out", hmid, dz2b, _bs((tks, twm), lambda i, j, k: (k, i)), _bs((tks, d_model), lambda i, j, k: (k, 0)),
        grid=(d_ff // twm, 1, nks), ta=True, acc_shape=(twm, d_model),
        out_shape=[jax.ShapeDtypeStruct((d_ff, d_model), F32)], out_specs=[_bs((twm, d_model), lambda i, j, k: (i, 0))],
        epilogue=plain_epi)
    twn = min(2048, d_ff)
    (g_w1,) = _matmul(
        "grad_w_mlp_in", x1b, dpre, _bs((tks, twm), lambda i, j, k: (k, i)), _bs((tks, twn), lambda i, j, k: (k, j)),
        grid=(d_model // twm, d_ff // twn, nks), ta=True, acc_shape=(twm, twn),
        out_shape=[jax.ShapeDtypeStruct((d_model, d_ff), F32)], out_specs=[_bs((twm, twn), lambda i, j, k: (i, j))],
        epilogue=plain_epi)

    (dy,) = _matmul(
        "mix_dy", dz1b, wout_full, _bs((tm, d_model), lambda i, j, k: (i, 0)), _bs((dh, d_model), lambda i, j, k: (j, 0)),
        grid=(ni, 2, 1), tb=True, acc_shape=None,
        out_shape=[jax.ShapeDtypeStruct((seq, d_model), F32)], out_specs=[_bs((tm, dh), lambda i, j, k: (i, j))],
        epilogue=plain_epi)
    (g_wout,) = _matmul(
        "grad_w_out", y, dz1b, _bs((tks, twm), lambda i, j, k: (k, i)), _bs((tks, d_model), lambda i, j, k: (k, 0)),
        grid=(d_model // twm, 1, nks), ta=True, acc_shape=(twm, d_model),
        out_shape=[jax.ShapeDtypeStruct((d_model, d_model), F32)], out_specs=[_bs((twm, d_model), lambda i, j, k: (i, 0))],
        epilogue=plain_epi)

    dproj_pool, g_wpool, g_pscale = _pool_bwd(d_pool, dy, wpool_full, pool_scale, seq, d_model)
    dproj3, g_pack, g_convb, g_wa, g_wi = _lru_bwd(p3, dy, h0p, h1p, dproj_pool, pack_full, conv_b, wa_b, wi_b,
                                                   seq, d_model)

    def dx_epi(acc, i, ex, out):
        out[0][...] = ALPHA * ex[0][...] + acc

    (dx,) = _matmul(
        "grad_x", dproj3, win_full, _bs((None, tm, dh), lambda i, j, k: ((k + 2) % 3, i, 0)),
        _bs((dh, dh), lambda i, j, k: (j, k)),
        grid=(ni, 2, 3), tb=True, acc_shape=(tm, dh),
        extras=[(dz1, _bs((tm, dh), lambda i, j, k: (i, j)))],
        out_shape=[jax.ShapeDtypeStruct((seq, d_model), F32)], out_specs=[_bs((tm, dh), lambda i, j, k: (i, j))],
        epilogue=dx_epi)
    (g_win,) = _matmul(
        "grad_w_in", xs, dproj3, _bs((tks, twm), lambda i, j, k: (k, i)),
        _bs((None, tks, dh), lambda i, j, k: ((j + 2) % 3, k, 0)),
        grid=(d_model // twm, 3, nks), ta=True, acc_shape=(twm, dh),
        out_shape=[jax.ShapeDtypeStruct((d_model, 3 * dh), F32)], out_specs=[_bs((twm, dh), lambda i, j, k: (i, j))],
        epilogue=plain_epi)

    r_win, r_wpool, r_wout, r_w1, r_w2 = _all_to_all("scatter_grads", [
        (g_win, 1), (g_wpool, 1), (g_wout, 0), (g_w1, 1), (g_w2, 0)])

    rep_parts = [_rows128(g_wa), _rows128(g_wi), _rows128(g_mix_g), _rows128(g_mix_b), _rows128(g_ffn_g),
                 _rows128(g_ffn_b), _rows128(g_pscale), _rows128(g_convb)]
    rep_rows = [p.shape[0] for p in rep_parts]
    n_rep = sum(rep_rows)
    small = jnp.concatenate(rep_parts + [_rows128(g_pack)], axis=0)
    (small_all,) = _all_gather("gather_small_grads", [(small[None], 0)])

    def adam_big(name, parts, w, m, v):
        shp = w.shape
        two = lambda a: a.reshape(-1, shp[-1])
        res = _sum_adamw(name, parts.reshape(N_DEV, -1, shp[-1]), two(w), two(m), two(v))
        return [r.reshape(shp) for r in res]

    o_win = adam_big("adam_w_in", r_win, w_in, m_w_in, v_w_in)
    o_wpool = adam_big("adam_w_pool", r_wpool, w_pool, m_w_pool, v_w_pool)
    o_wout = adam_big("adam_w_out", r_wout, w_out, m_w_out, v_w_out)
    o_w1 = adam_big("adam_w_mlp_in", r_w1, w_mlp_in, m_w_mlp_in, v_w_mlp_in)
    o_w2 = adam_big("adam_w_mlp_out", r_w2, w_mlp_out, m_w_mlp_out, v_w_mlp_out)

    rep_w = [w_rg_a, w_rg_i, ln_mix_g, ln_mix_b, ln_ffn_g, ln_ffn_b, pool_scale, conv_b]
    rep_m = [m_w_rg_a, m_w_rg_i, m_ln_mix_g, m_ln_mix_b, m_ln_ffn_g, m_ln_ffn_b, m_pool_scale, m_conv_b]
    rep_v = [v_w_rg_a, v_w_rg_i, v_ln_mix_g, v_ln_mix_b, v_ln_ffn_g, v_ln_ffn_b, v_pool_scale, v_conv_b]
    cat = lambda arrs: jnp.concatenate([_rows128(a) for a in arrs], axis=0)
    o_rep = _sum_adamw("adam_replicated", small_all[:, :n_rep, :], cat(rep_w), cat(rep_m), cat(rep_v))

    my_idx = _dev_index(_where_am_i())
    head_parts = lax.dynamic_slice_in_dim(small_all, n_rep + my_idx * SMALL_ROWS, SMALL_ROWS, axis=1)
    o_head = _sum_adamw("adam_head", head_parts, pack_mine, small_pack(m_conv_w, m_b_rg_a, m_b_rg_i, m_rg_lambda),
                        small_pack(v_conv_w, v_b_rg_a, v_b_rg_i, v_rg_lambda))

    def unpack_rep(packed):
        out, r = [], 0
        for wgt, rows in zip(rep_w, rep_rows):
            out.append(packed[r:r + rows].reshape(wgt.shape))
            r += rows
        return out

    def unpack_head(packed):
        return [packed[0:4].reshape(conv_w.shape), packed[4:6].reshape(b_rg_a.shape),
                packed[6:8].reshape(b_rg_i.shape), packed[8:10].reshape(rg_lambda.shape)]

    loss = lax.psum(loss_part[0, 0], ("x", "y", "c"))

    outs = [loss, dx[None]]
    for kind in range(4):
        ra, ri, mg, mb, fg, fb, ps, cb = unpack_rep(o_rep[kind])
        cw, ba, bi, lam = unpack_head(o_head[kind])
        outs += [mg, mb, o_win[kind], o_wpool[kind], ps, cw, cb, ra, ba, ri, bi, lam, o_wout[kind], fg, fb,
                 o_w1[kind], o_w2[kind]]
    return tuple(outs)
```

```python
import functools

import jax
import jax.numpy as jnp
from jax import lax
from jax.experimental import pallas as pl
from jax.experimental.pallas import tpu as pltpu

F32 = jnp.float32
BF16 = jnp.bfloat16
MESH = pl.DeviceIdType.MESH
ANY = pl.BlockSpec(memory_space=pl.ANY)

N_DEV = 8
POOL_WINDOWS = (2, 4, 8, 16)
N_HEADS = 8
RG_C = 8.0
LN_EPS = 1e-5
ALPHA = 2.0 ** 0.25
ADAM_LR = 0.001
ADAM_B1 = 0.9
ADAM_B2 = 0.999
ADAM_EPS = 1e-08
ADAM_WD = 0.01
ADAM_STEP = 10

VMEM_LIMIT = 56 * 1024 * 1024
WIN_HALO = 16
CONV_HALO = 8
SMALL_ROWS = 16


def _params(n_grid):
    return pltpu.CompilerParams(dimension_semantics=("arbitrary",) * n_grid, vmem_limit_bytes=VMEM_LIMIT)


def _shift(v, j):
    n = v.shape[0]
    s = (-j) % n
    return v if s == 0 else pltpu.roll(v, s, 0)


def _expm1(x):
    poly = x * (1.0 + x * (0.5 + x * (1.0 / 6.0 + x * (1.0 / 24.0 + x * (1.0 / 120.0)))))
    return jnp.where(jnp.abs(x) < 0.1, poly, jnp.exp(x) - 1.0)


def _softplus(z):
    e = jnp.exp(-jnp.abs(z))
    u = 1.0 + e
    log1p = jnp.where(u == 1.0, e, jnp.log(u) * (e / jnp.where(u == 1.0, 1.0, u - 1.0)))
    return jnp.maximum(z, 0.0) + log1p


_GELU_C = 0.7978845608028654
_GELU_K = 0.044715


def _gelu_and_grad(x):
    x2 = x * x
    t = jnp.tanh(_GELU_C * (x + _GELU_K * x * x2))
    g = 0.5 * x * (1.0 + t)
    dg = 0.5 * (1.0 + t) + 0.5 * x * (1.0 - t * t) * (_GELU_C * (1.0 + 3.0 * _GELU_K * x2))
    return g, dg


def _ln_fwd(z, g, b):
    mu = jnp.mean(z, axis=-1, keepdims=True)
    zc = z - mu
    var = jnp.mean(zc * zc, axis=-1, keepdims=True)
    rstd = lax.rsqrt(var + LN_EPS)
    xhat = zc * rstd
    return xhat * g + b, xhat, rstd


def _ln_bwd(dy, xhat, rstd, g):
    dxhat = dy * g
    m1 = jnp.mean(dxhat, axis=-1, keepdims=True)
    m2 = jnp.mean(dxhat * xhat, axis=-1, keepdims=True)
    dz = rstd * (dxhat - m1 - xhat * m2)
    dg = jnp.sum(dy * xhat, axis=0, keepdims=True)
    db = jnp.sum(dy, axis=0, keepdims=True)
    return dz, dg, db


def _acc_rows(ref, first, val):
    @pl.when(first)
    def _():
        ref[...] = val

    @pl.when(jnp.logical_not(first))
    def _():
        ref[...] += val


def _matmul(name, a, b, a_spec, b_spec, *, grid, ta=False, tb=False, acc_shape, extras=(), out_shape, out_specs,
            epilogue):
    ni, nj, nk = grid
    n_ex = len(extras)
    n_out = len(out_shape)
    dims = (((0 if ta else 1,), (1 if tb else 0,)), ((), ()))

    def body(a_ref, b_ref, *rest):
        ex_refs = rest[:n_ex]
        out_refs = rest[n_ex:n_ex + n_out]
        i = pl.program_id(0)
        k = pl.program_id(2)
        part = lax.dot_general(a_ref[...].astype(BF16), b_ref[...].astype(BF16), dims, preferred_element_type=F32)
        if nk == 1:
            epilogue(part, i, ex_refs, out_refs)
        else:
            acc_ref = rest[n_ex + n_out]

            @pl.when(k == 0)
            def _():
                acc_ref[...] = part

            @pl.when(k > 0)
            def _():
                acc_ref[...] += part

            @pl.when(k == nk - 1)
            def _():
                epilogue(acc_ref[...], i, ex_refs, out_refs)

    return pl.pallas_call(
        body, name=name, grid=(ni, nj, nk),
        in_specs=[a_spec, b_spec] + [s for _, s in extras],
        out_specs=list(out_specs), out_shape=list(out_shape),
        scratch_shapes=[] if nk == 1 else [pltpu.VMEM(acc_shape, F32)],
        compiler_params=_params(3),
    )(a, b, *[x for x, _ in extras])


def _bs(shape, fn):
    return pl.BlockSpec(shape, fn)


def _where_am_i():
    x, y, c = lax.axis_index("x"), lax.axis_index("y"), lax.axis_index("c")
    return x, y, c


def _dev_index(p):
    return 4 * p[0] + 2 * p[1] + p[2]


def _slab(ref, axis, idx, size):
    sl = [slice(None)] * len(ref.shape)
    sl[axis] = pl.ds(idx * size, size)
    return ref.at[tuple(sl)]


def _all_gather(name, items):
    n = len(items)
    shapes = []
    for shard, axis in items:
        s = list(shard.shape)
        s[axis] *= N_DEV
        shapes.append(jax.ShapeDtypeStruct(tuple(s), shard.dtype))

    def body(*refs):
        in_refs, out_refs = refs[:n], refs[n:2 * n]
        send_sems, recv_sems, local_sems = refs[2 * n:]
        x, y, c = _where_am_i()
        me, sibling = (x, y, c), (x, y, 1 - c)
        chips = [(1 - x, y), (x, 1 - y), (1 - x, 1 - y)]

        def blk(a, p):
            axis = items[a][1]
            return _slab(out_refs[a], axis, _dev_index(p), items[a][0].shape[axis])

        def copy(a, k, block, to, src=None):
            return pltpu.make_async_remote_copy(
                src_ref=blk(a, block) if src is None else src, dst_ref=blk(a, block),
                send_sem=send_sems.at[a, k], recv_sem=recv_sems.at[a, k], device_id=to, device_id_type=MESH)

        mine = [pltpu.make_async_copy(in_refs[a], blk(a, me), local_sems.at[a]) for a in range(n)]
        for cp in mine:
            cp.start()
        first = []
        for a in range(n):
            first.append(copy(a, 0, me, sibling, src=in_refs[a]))
            first += [copy(a, 1 + j, me, (*chip, c), src=in_refs[a]) for j, chip in enumerate(chips)]
        for cp in first:
            cp.start()
        passed = []
        for a in range(n):
            for j, chip in enumerate(chips):
                copy(a, 1 + j, (*chip, c), me).wait_recv()
                fw = copy(a, 4 + j, (*chip, c), sibling)
                fw.start()
                passed.append(fw)
        for a in range(n):
            copy(a, 0, sibling, me).wait_recv()
            for j, chip in enumerate(chips):
                copy(a, 4 + j, (*chip, 1 - c), me).wait_recv()
        for cp in first + passed:
            cp.wait_send()
        for cp in mine:
            cp.wait()

    outs = pl.pallas_call(
        body, name=name, out_shape=shapes, in_specs=[ANY] * n, out_specs=[ANY] * n,
        scratch_shapes=[pltpu.SemaphoreType.DMA((n, 7)), pltpu.SemaphoreType.DMA((n, 7)),
                        pltpu.SemaphoreType.DMA((n,))],
    )(*[s for s, _ in items])
    return list(outs)


HBM = pl.BlockSpec(memory_space=pltpu.HBM)
SEM = pl.BlockSpec(memory_space=pltpu.SEMAPHORE)
DATAFLOW = pltpu.SideEffectType.DATAFLOW_SIDE_EFFECTING


def _in_hbm(a):
    return pltpu.with_memory_space_constraint(a, pltpu.HBM)


def _token_shape():
    return jax.ShapeDtypeStruct((8, 128), F32)


def _split_start(name, n_sems, bufs, issue):
    nb = len(bufs)

    def body(*refs):
        issue(refs[:nb], refs[nb], refs[nb + 1])
        refs[-1][...] = jnp.zeros((8, 128), F32)

    outs = pl.pallas_call(
        body, name=name,
        out_shape=(pltpu.SemaphoreType.DMA((n_sems,)), pltpu.SemaphoreType.DMA((n_sems,)),
                   *[pltpu.HBM(b.shape, b.dtype) for b in bufs], _token_shape()),
        in_specs=[HBM] * nb, out_specs=(SEM, SEM, *[HBM] * nb, pl.BlockSpec(memory_space=pltpu.VMEM)),
        input_output_aliases={i: 2 + i for i in range(nb)},
        compiler_params=pltpu.CompilerParams(has_side_effects=DATAFLOW),
    )(*[_in_hbm(b) for b in bufs])
    return outs[0], outs[1], list(outs[2:2 + nb]), outs[-1]


def _split_relay(name, n_sems, sems, bufs, after, relay):
    nb = len(bufs)

    def body(*refs):
        relay(refs[:nb], refs[nb], refs[nb + 1], refs[nb + 3], refs[nb + 4])
        refs[-1][...] = jnp.zeros((8, 128), F32)

    outs = pl.pallas_call(
        body, name=name,
        out_shape=(pltpu.SemaphoreType.DMA((n_sems,)), pltpu.SemaphoreType.DMA((n_sems,)),
                   *[pltpu.HBM(b.shape, b.dtype) for b in bufs], _token_shape()),
        in_specs=[HBM] * nb + [SEM, SEM, ANY],
        out_specs=(SEM, SEM, *[HBM] * nb, pl.BlockSpec(memory_space=pltpu.VMEM)),
        input_output_aliases={i: 2 + i for i in range(nb)},
        compiler_params=pltpu.CompilerParams(has_side_effects=DATAFLOW),
    )(*bufs, sems[0], sems[1], after)
    return outs[0], outs[1], list(outs[2:2 + nb]), outs[-1]


def _split_wait(name, sems, bufs, after, finish):
    nb = len(bufs)

    def body(*refs):
        finish(refs[:nb], refs[nb], refs[nb + 1])

    outs = pl.pallas_call(
        body, name=name, out_shape=[pltpu.HBM(b.shape, b.dtype) for b in bufs],
        in_specs=[HBM] * nb + [SEM, SEM, ANY], out_specs=[HBM] * nb,
        input_output_aliases={i: i for i in range(nb)},
        compiler_params=pltpu.CompilerParams(has_side_effects=DATAFLOW),
    )(*bufs, sems[0], sems[1], after)
    return list(outs)


def _place(name, items, after):
    n = len(items)
    shapes = []
    for shard, axis in items:
        s = list(shard.shape)
        s[axis] *= N_DEV
        shapes.append(jax.ShapeDtypeStruct(tuple(s), shard.dtype))

    def body(*refs):
        in_refs, out_refs, sems = refs[:n], refs[n + 1:2 * n + 1], refs[-1]
        my_idx = _dev_index(_where_am_i())
        cps = [pltpu.make_async_copy(in_refs[a], _slab(out_refs[a], items[a][1], my_idx, items[a][0].shape[items[a][1]]),
                                     sems.at[a]) for a in range(n)]
        for cp in cps:
            cp.start()
        for cp in cps:
            cp.wait()

    outs = pl.pallas_call(
        body, name=name, out_shape=shapes, in_specs=[ANY] * (n + 1), out_specs=[ANY] * n,
        scratch_shapes=[pltpu.SemaphoreType.DMA((n,))],
    )(*[s for s, _ in items], after)
    return list(outs)


class _SplitGather:
    def __init__(self, name, items, after):
        self.name, self.items, self.n = name, items, len(items)
        fulls = _place(name + "_place", items, after)
        n = self.n

        def issue(refs, send, recv):
            me, sibling, chips, c = self._geometry()
            for a in range(n):
                self._copy1(refs, send, recv, a, 0, me, sibling).start()
                for j, chip in enumerate(chips):
                    self._copy1(refs, send, recv, a, 1 + j, me, (*chip, c)).start()

        self.send, self.recv, self.bufs, self.token = _split_start(
            name + "_start", 4 * n, [s for s, _ in items] + fulls, issue)

    @staticmethod
    def _geometry():
        x, y, c = _where_am_i()
        return (x, y, c), (x, y, 1 - c), [(1 - x, y), (x, 1 - y), (1 - x, 1 - y)], c

    def _blk(self, refs, a, p):
        shard, axis = self.items[a]
        return _slab(refs[self.n + a], axis, _dev_index(p), shard.shape[axis])

    def _copy1(self, refs, send, recv, a, k, owner, to):
        return pltpu.make_async_remote_copy(
            src_ref=refs[a], dst_ref=self._blk(refs, a, owner), send_sem=send.at[4 * a + k],
            recv_sem=recv.at[4 * a + k], device_id=to, device_id_type=MESH)

    def _copy2(self, refs, send, recv, a, j, owner, to):
        return pltpu.make_async_remote_copy(
            src_ref=self._blk(refs, a, owner), dst_ref=self._blk(refs, a, owner), send_sem=send.at[3 * a + j],
            recv_sem=recv.at[3 * a + j], device_id=to, device_id_type=MESH)

    def relay(self, after):
        n = self.n

        def relay(refs, send_in, recv_in, send_out, recv_out):
            me, sibling, chips, c = self._geometry()
            for a in range(n):
                for j, chip in enumerate(chips):
                    self._copy1(refs, send_in, recv_in, a, 1 + j, (*chip, c), me).wait_recv()
                    self._copy2(refs, send_out, recv_out, a, j, (*chip, c), sibling).start()
            for a in range(n):
                self._copy1(refs, send_in, recv_in, a, 0, sibling, me).wait_recv()
                for k in range(4):
                    self._copy1(refs, send_in, recv_in, a, k, me, sibling).wait_send()

        self.send, self.recv, self.bufs, self.token = _split_relay(
            self.name + "_relay", 3 * n, (self.send, self.recv), self.bufs, after, relay)
        return self.token

    def wait(self, after):
        n = self.n

        def finish(refs, send, recv):
            me, sibling, chips, c = self._geometry()
            for a in range(n):
                for j, chip in enumerate(chips):
                    self._copy2(refs, send, recv, a, j, (*chip, 1 - c), me).wait_recv()
                    self._copy2(refs, send, recv, a, j, (*chip, c), sibling).wait_send()

        bufs = _split_wait(self.name + "_wait", (self.send, self.recv), self.bufs, after, finish)
        return bufs[n:]


class _SplitReduceScatter:
    def __init__(self, name, grads):
        self.name, self.n = name, len(grads)
        n = self.n
        g4 = [g.reshape(4, 2, *g.shape[1:]) for g in grads]
        land = [lax.empty((4, 1, *g.shape[1:]), g.dtype) for g in grads]

        def issue(refs, send, recv):
            for a in range(n):
                self._swap(refs, send, recv, a).start()

        self.send, self.recv, self.bufs, self.token = _split_start(name + "_d2d_start", n, g4 + land, issue)

    def _swap(self, refs, send, recv, a):
        x, y, c = _where_am_i()
        return pltpu.make_async_remote_copy(
            src_ref=refs[a].at[:, pl.ds(1 - c, 1)], dst_ref=refs[self.n + a], send_sem=send.at[a], recv_sem=recv.at[a],
            device_id=(x, y, 1 - c), device_id_type=MESH)

    def _hop(self, refs, send, recv, a, m):
        x, y, c = _where_am_i()
        px = (1 - x) if m & 2 else x
        py = (1 - y) if m & 1 else y
        return pltpu.make_async_remote_copy(
            src_ref=refs[a].at[2 * px + py], dst_ref=refs[self.n + a].at[m - 1], send_sem=send.at[3 * a + m - 1],
            recv_sem=recv.at[3 * a + m - 1], device_id=(px, py, c), device_id_type=MESH)

    def combine_and_send(self, after):
        n = self.n

        def finish(refs, send, recv):
            for a in range(n):
                self._swap(refs, send, recv, a).wait()

        bufs = _split_wait(self.name + "_d2d_wait", (self.send, self.recv), self.bufs, after, finish)
        x, y, c = _where_am_i()
        ids = jnp.stack([c, 2 * x + y]).astype(jnp.int32)
        self.own, sums = [], []
        for a in range(n):
            own, hb = _pair_sum(f"{self.name}_sum{a}", bufs[a], bufs[n + a], ids)
            self.own.append(own)
            sums.append(hb)
        land = [lax.empty((3, *h.shape[1:]), h.dtype) for h in sums]

        def issue(refs, send, recv):
            for a in range(n):
                for m in (1, 2, 3):
                    self._hop(refs, send, recv, a, m).start()

        self.send, self.recv, self.bufs, self.token = _split_start(self.name + "_ici_start", 3 * n, sums + land, issue)
        return self.token

    def wait(self, after):
        n = self.n

        def finish(refs, send, recv):
            for a in range(n):
                for m in (1, 2, 3):
                    self._hop(refs, send, recv, a, m).wait()

        bufs = _split_wait(self.name + "_ici_wait", (self.send, self.recv), self.bufs, after, finish)
        return list(zip(self.own, bufs[n:]))


def _pair_sum(name, g4, land, ids):
    rows, cols = g4.shape[2], g4.shape[3]
    tr = rows
    while tr * cols * 2 > 1024 * 1024 and tr % 32 == 0:
        tr //= 2

    def body(ids_ref, g_ref, l_ref, own_ref, sum_ref):
        h = g_ref[...].astype(F32) + l_ref[...].astype(F32)
        sum_ref[...] = h.astype(sum_ref.dtype)

        @pl.when(pl.program_id(1) == ids_ref[1])
        def _():
            own_ref[...] = h

    return pl.pallas_call(
        body, name=name,
        grid_spec=pltpu.PrefetchScalarGridSpec(
            num_scalar_prefetch=1, grid=(rows // tr, 4),
            in_specs=[_bs((None, None, tr, cols), lambda i, q, ids: (q, ids[0], i, 0)),
                      _bs((None, None, tr, cols), lambda i, q, ids: (q, 0, i, 0))],
            out_specs=[_bs((tr, cols), lambda i, q, ids: (i, 0)), _bs((None, tr, cols), lambda i, q, ids: (q, i, 0))]),
        out_shape=[jax.ShapeDtypeStruct((rows, cols), F32), jax.ShapeDtypeStruct((4, rows, cols), g4.dtype)],
        compiler_params=_params(2),
    )(ids, g4, land)


def _win_sum(ext, w, off):
    s = ext + _shift(ext, -1)
    if w >= 4:
        s = _shift(s, -1) + _shift(s, 1)
    if w >= 8:
        s = _shift(s, -2) + _shift(s, 2)
    if w >= 16:
        s = _shift(s, -4) + _shift(s, 4)
    return _shift(s, off) if off else s


def _inv_count(r0, t, w, seq):
    pos = r0 + lax.broadcasted_iota(jnp.int32, (t, 1), 0)
    cnt = jnp.minimum(pos + w // 2, seq) - jnp.maximum(pos - w // 2, 0)
    return 1.0 / cnt.astype(F32)


def _pool_fwd(p3, w_pool, pool_scale, seq, d_model):
    dp = d_model // 2
    pg = dp // len(POOL_WINDOWS)
    t = min(128, seq)
    n_chunks = seq // t
    h = WIN_HALO

    def body(u_ref, w_ref, sc_ref, d_ref, y_ref, pad_ref):
        g = pl.program_id(0)
        zeros = jnp.zeros((h, pg), F32)
        pad_ref[0:h, :] = zeros
        pad_ref[h + seq:h + seq + h, :] = zeros

        def fill(ci, _):
            r0 = pl.multiple_of(ci * t, t)
            pad_ref[pl.ds(h + r0, t), :] = u_ref[pl.ds(r0, t), :]
            return 0

        lax.fori_loop(0, n_chunks, fill, 0)
        wmat = w_ref[...]
        scale = sc_ref[...]
        for gi, w in enumerate(POOL_WINDOWS):
            @pl.when(g == gi)
            def _(w=w):
                def chunk(ci, _):
                    r0 = pl.multiple_of(ci * t, t)
                    ext = pad_ref[pl.ds(r0, t + 2 * h), :]
                    mean = _win_sum(ext, w, 0)[h:h + t, :] * _inv_count(r0, t, w, seq)
                    d = (mean - ext[h:h + t, :]).astype(BF16)
                    d_ref[pl.ds(r0, t), :] = d
                    q = jnp.dot(d, wmat, preferred_element_type=F32)
                    y_ref[pl.ds(r0, t), :] = (q * scale).astype(BF16)
                    return 0

                lax.fori_loop(0, n_chunks, chunk, 0)

    return pl.pallas_call(
        body, name="pool_fwd", grid=(len(POOL_WINDOWS),),
        in_specs=[_bs((None, seq, pg), lambda g: (0, 0, g)), _bs((None, pg, pg), lambda g: (g, 0, 0)),
                  _bs((1, pg), lambda g: (0, g))],
        out_specs=[_bs((seq, pg), lambda g: (0, g)), _bs((seq, pg), lambda g: (0, g))],
        out_shape=[jax.ShapeDtypeStruct((seq, dp), BF16), jax.ShapeDtypeStruct((seq, d_model), BF16)],
        scratch_shapes=[pltpu.VMEM((seq + 2 * h, pg), F32)],
        compiler_params=_params(1),
    )(p3, w_pool, pool_scale)


def _pool_bwd(d, dy, w_pool, pool_scale, token, seq, d_model):
    dp = d_model // 2
    pg = dp // len(POOL_WINDOWS)
    t = min(128, seq)
    n_chunks = seq // t
    h = WIN_HALO
    tn_dims = (((0,), (0,)), ((), ()))
    nt_dims = (((1,), (1,)), ((), ()))

    def body(d_ref, dy_ref, w_ref, sc_ref, tok_ref, du_ref, dwb_ref, dsc_ref, pad_ref, dd_ref, dw_ref):
        del tok_ref
        g = pl.program_id(0)
        zeros = jnp.zeros((h, pg), F32)
        pad_ref[0:h, :] = zeros
        pad_ref[h + seq:h + seq + h, :] = zeros
        wmat = w_ref[...]
        scale = sc_ref[...]
        for gi, w in enumerate(POOL_WINDOWS):
            @pl.when(g == gi)
            def _(w=w):
                dw_ref[...] = jnp.zeros((pg, pg), F32)

                def first(ci, dsc):
                    r0 = pl.multiple_of(ci * t, t)
                    dv = d_ref[pl.ds(r0, t), :]
                    dyv = dy_ref[pl.ds(r0, t), :]
                    q = jnp.dot(dv, wmat, preferred_element_type=F32)
                    dsc = dsc + jnp.sum(dyv * q, axis=0, keepdims=True)
                    dq = (dyv * scale).astype(BF16)
                    dw_ref[...] += lax.dot_general(dv, dq, tn_dims, preferred_element_type=F32)
                    dd = lax.dot_general(dq, wmat, nt_dims, preferred_element_type=F32)
                    dd_ref[pl.ds(r0, t), :] = dd
                    pad_ref[pl.ds(h + r0, t), :] = dd * _inv_count(r0, t, w, seq)
                    return dsc

                dsc_ref[...] = lax.fori_loop(0, n_chunks, first, jnp.zeros((1, pg), F32))
                dwb_ref[...] = dw_ref[...].reshape(N_DEV, pg // N_DEV, pg).astype(BF16)

                def second(ci, _):
                    r0 = pl.multiple_of(ci * t, t)
                    ext = pad_ref[pl.ds(r0, t + 2 * h), :]
                    back = _win_sum(ext, w, 1)[h:h + t, :]
                    du_ref[pl.ds(r0, t), :] = (back - dd_ref[pl.ds(r0, t), :]).astype(BF16)
                    return 0

                lax.fori_loop(0, n_chunks, second, 0)

    return pl.pallas_call(
        body, name="pool_bwd", grid=(len(POOL_WINDOWS),),
        in_specs=[_bs((seq, pg), lambda g: (0, g)), _bs((seq, pg), lambda g: (0, g)),
                  _bs((None, pg, pg), lambda g: (g, 0, 0)), _bs((1, pg), lambda g: (0, g)),
                  _bs((8, 128), lambda g: (0, 0))],
        out_specs=[_bs((seq, pg), lambda g: (0, g)), _bs((N_DEV, None, pg // N_DEV, pg), lambda g: (0, g, 0, 0)),
                   _bs((1, pg), lambda g: (0, g))],
        out_shape=[jax.ShapeDtypeStruct((seq, 3 * dp), BF16),
                   jax.ShapeDtypeStruct((N_DEV, len(POOL_WINDOWS), pg // N_DEV, pg), BF16),
                   jax.ShapeDtypeStruct((1, dp), F32)],
        scratch_shapes=[pltpu.VMEM((seq + 2 * h, pg), F32), pltpu.VMEM((seq, pg), F32), pltpu.VMEM((pg, pg), F32)],
        compiler_params=_params(1),
    )(d, dy, w_pool, pool_scale, token)


def _segment_scan(seg_len, loads, hs_refs, ps_refs, stores):
    lanes = hs_refs[0].shape[-1]
    row = lax.broadcasted_iota(jnp.int32, (8, lanes), 0)

    def pos_of(n, s):
        return s if n == 0 else seg_len - 1 - s

    def step1(s, carry):
        out = []
        for n in range(2):
            hh, pp = carry[n]
            pos = pos_of(n, s)
            coef, inp = loads[n](pos)
            hh = coef * hh + inp
            pp = coef * pp
            hs_refs[n][pl.ds(pl.multiple_of(pos * 8, 8), 8), :] = hh
            ps_refs[n][pl.ds(pl.multiple_of(pos * 8, 8), 8), :] = pp
            out.append((hh, pp))
        return tuple(out)

    init = tuple((jnp.zeros((8, lanes), F32), jnp.ones((8, lanes), F32)) for _ in range(2))
    ends = lax.fori_loop(0, seg_len, step1, init, unroll=4)

    entry = []
    for n in range(2):
        bb, aa = ends[n]
        for sh in (1, 2, 4):
            if n == 0:
                ok = row >= sh
                ap = jnp.where(ok, pltpu.roll(aa, sh, 0), 1.0)
                bp = jnp.where(ok, pltpu.roll(bb, sh, 0), 0.0)
            else:
                ok = row < 8 - sh
                ap = jnp.where(ok, pltpu.roll(aa, 8 - sh, 0), 1.0)
                bp = jnp.where(ok, pltpu.roll(bb, 8 - sh, 0), 0.0)
            bb = aa * bp + bb
            aa = aa * ap
        if n == 0:
            entry.append(jnp.where(row >= 1, pltpu.roll(bb, 1, 0), 0.0))
        else:
            entry.append(jnp.where(row < 7, pltpu.roll(bb, 7, 0), 0.0))

    def step2(s, _):
        for n in range(2):
            at = pl.ds(pl.multiple_of(s * 8, 8), 8)
            stores[n](s, hs_refs[n][at, :] + ps_refs[n][at, :] * entry[n])
        return 0

    lax.fori_loop(0, seg_len, step2, 0, unroll=4)


def _gates(xc, n, wa_ref, wi_ref, pk_ref, sp):
    xcb = xc.astype(BF16)
    r = jax.nn.sigmoid(jnp.dot(xcb, wa_ref[n], preferred_element_type=F32) + pk_ref[pl.ds(4 + n, 1), :])
    i = jax.nn.sigmoid(jnp.dot(xcb, wi_ref[n], preferred_element_type=F32) + pk_ref[pl.ds(6 + n, 1), :])
    log_a = (-RG_C * r) * sp[n]
    a = jnp.exp(log_a)
    m = jnp.sqrt(-_expm1(2.0 * log_a))
    return xcb, r, i, a, m


def _conv_chunk(upad_ref, pk_ref, cb, r0, t):
    ext = upad_ref[pl.ds(r0, t + 2 * CONV_HALO), :]
    acc = pk_ref[pl.ds(1, 1), :] * ext
    for k in (0, 2, 3):
        acc = acc + pk_ref[pl.ds(k, 1), :] * _shift(ext, k - 1)
    return acc[CONV_HALO:CONV_HALO + t, :] + cb, ext


def _lru_fwd(p3, y_in, pack, conv_b, wa, wi, seq, d_model):
    dl = d_model // 2
    lh = dl // N_HEADS
    t = min(128, seq)
    n_chunks = seq // t
    seg = seq // 8
    hal = CONV_HALO
    first_rec_block = (d_model - dl) // lh

    def body(ur_ref, ug_ref, pk_ref, cb_ref, wa_ref, wi_ref, yin_ref, y_ref, h0_ref, h1_ref,
             upad, a_scr, b_scr, hs0, hs1, ps0, ps1):
        del yin_ref
        zeros = jnp.zeros((hal, lh), F32)
        upad[0:hal, :] = zeros
        upad[hal + seq:hal + seq + hal, :] = zeros
        for ref in (h0_ref, h1_ref):
            ref[0:hal, :] = zeros
            ref[hal + seq:hal + seq + hal, :] = zeros

        def fill(ci, _):
            r0 = pl.multiple_of(ci * t, t)
            upad[pl.ds(hal + r0, t), :] = ur_ref[pl.ds(r0, t), :]
            return 0

        lax.fori_loop(0, n_chunks, fill, 0)
        cb = cb_ref[...]
        sp = [_softplus(-pk_ref[pl.ds(8 + n, 1), :]) for n in range(2)]

        def chunk(ci, _):
            r0 = pl.multiple_of(ci * t, t)
            xc, _ext = _conv_chunk(upad, pk_ref, cb, r0, t)
            for n in range(2):
                _, _, i, a, m = _gates(xc, n, wa_ref, wi_ref, pk_ref, sp)
                a_scr[n, pl.ds(r0, t), :] = a
                b_scr[n, pl.ds(r0, t), :] = (m * i) * xc
            return 0

        lax.fori_loop(0, n_chunks, chunk, 0)

        def load(n):
            return lambda pos: (a_scr[n, pl.ds(pos, 8, stride=seg), :], b_scr[n, pl.ds(pos, 8, stride=seg), :])

        def store(n, ref):
            def put(s, v):
                ref[pl.ds(hal + s, 8, stride=seg), :] = v
            return put

        _segment_scan(seg, [load(0), load(1)], [hs0, hs1], [ps0, ps1], [store(0, h0_ref), store(1, h1_ref)])

        def out(ci, _):
            r0 = pl.multiple_of(ci * t, t)
            hsum = h0_ref[pl.ds(hal + r0, t), :] + h1_ref[pl.ds(hal + r0, t), :]
            gl, _dg = _gelu_and_grad(ug_ref[pl.ds(r0, t), :])
            y_ref[pl.ds(r0, t), :] = (hsum * gl).astype(BF16)
            return 0

        lax.fori_loop(0, n_chunks, out, 0)

    return pl.pallas_call(
        body, name="lru_fwd", grid=(N_HEADS,),
        in_specs=[_bs((None, seq, lh), lambda h: (1, 0, h)), _bs((None, seq, lh), lambda h: (2, 0, h)),
                  _bs((None, SMALL_ROWS, lh), lambda h: (h, 0, 0)), _bs((1, lh), lambda h: (0, h)),
                  _bs((2, None, lh, lh), lambda h: (0, h, 0, 0)), _bs((2, None, lh, lh), lambda h: (0, h, 0, 0)),
                  ANY],
        out_specs=[_bs((seq, lh), lambda h: (0, first_rec_block + h)),
                   _bs((seq + 2 * hal, lh), lambda h: (0, h)), _bs((seq + 2 * hal, lh), lambda h: (0, h))],
        out_shape=[jax.ShapeDtypeStruct((seq, d_model), BF16), jax.ShapeDtypeStruct((seq + 2 * hal, dl), F32),
                   jax.ShapeDtypeStruct((seq + 2 * hal, dl), F32)],
        scratch_shapes=[pltpu.VMEM((seq + 2 * hal, lh), F32), pltpu.VMEM((2, seq, lh), F32),
                        pltpu.VMEM((2, seq, lh), F32)] + [pltpu.VMEM((seq, lh), F32)] * 4,
        input_output_aliases={6: 0},
        compiler_params=_params(1),
    )(p3, p3, pack, conv_b, wa, wi, y_in)


def _lru_bwd(p3, dy, h0p, h1p, dproj_in, pack, conv_b, wa, wi, token, seq, d_model):
    dl = d_model // 2
    lh = dl // N_HEADS
    t = min(128, seq)
    n_chunks = seq // t
    seg = seq // 8
    hal = CONV_HALO
    first_rec_block = (d_model - dl) // lh
    tn_dims = (((0,), (0,)), ((), ()))
    nt_dims = (((1,), (1,)), ((), ()))

    def body(ur_ref, ug_ref, dy_ref, h0_ref, h1_ref, pk_ref, cb_ref, wa_ref, wi_ref, tok_ref, din_ref,
             dproj_ref, dpk_ref, dcb_ref, dwa_ref, dwi_ref,
             upad, a_scr, dh_scr, g_scr, dxc_pad, hs0, hs1, ps0, ps1, dpr_ref, out_sems):
        del din_ref, tok_ref
        zeros = jnp.zeros((hal, lh), F32)
        for ref in (upad, dxc_pad):
            ref[0:hal, :] = zeros
            ref[hal + seq:hal + seq + hal, :] = zeros
        for n in range(2):
            a_scr[n, 0:hal, :] = zeros
            a_scr[n, hal + seq:hal + seq + hal, :] = zeros

        def fill(ci, _):
            r0 = pl.multiple_of(ci * t, t)
            upad[pl.ds(hal + r0, t), :] = ur_ref[pl.ds(r0, t), :]
            return 0

        lax.fori_loop(0, n_chunks, fill, 0)
        cb = cb_ref[...]
        lam = [pk_ref[pl.ds(8 + n, 1), :] for n in range(2)]
        sp = [_softplus(-lam[n]) for n in range(2)]

        def chunk1(ci, _):
            r0 = pl.multiple_of(ci * t, t)
            xc, _ext = _conv_chunk(upad, pk_ref, cb, r0, t)
            for n in range(2):
                _, _, _, a, _ = _gates(xc, n, wa_ref, wi_ref, pk_ref, sp)
                a_scr[n, pl.ds(hal + r0, t), :] = a
            hsum = h0_ref[pl.ds(hal + r0, t), :] + h1_ref[pl.ds(hal + r0, t), :]
            gl, dgl = _gelu_and_grad(ug_ref[pl.ds(r0, t), :])
            dyv = dy_ref[pl.ds(r0, t), :]
            dh_scr[pl.ds(r0, t), :] = dyv * gl
            dpr_ref[1, pl.ds(r0, t), :] = ((dyv * hsum) * dgl).astype(BF16)
            return 0

        lax.fori_loop(0, n_chunks, chunk1, 0)

        def load(n):
            def get(pos):
                coef = a_scr[n, pl.ds(hal + pos + (1 if n == 0 else -1), 8, stride=seg), :]
                return coef, dh_scr[pl.ds(pos, 8, stride=seg), :]
            return get

        def store(n):
            def put(s, v):
                g_scr[n, pl.ds(s, 8, stride=seg), :] = v
            return put

        _segment_scan(seg, [load(1), load(0)], [hs0, hs1], [ps0, ps1], [store(1), store(0)])

        dwa_ref[...] = jnp.zeros((2, lh, lh), F32)
        dwi_ref[...] = jnp.zeros((2, lh, lh), F32)

        def chunk3(ci, carry):
            dba, dbi, dlam, dcb = carry
            r0 = pl.multiple_of(ci * t, t)
            xc, _ext = _conv_chunk(upad, pk_ref, cb, r0, t)
            dxc = jnp.zeros((t, lh), F32)
            dba, dbi, dlam = list(dba), list(dbi), list(dlam)
            for n in range(2):
                xcb, r, i, a, m = _gates(xc, n, wa_ref, wi_ref, pk_ref, sp)
                hext = (h0_ref if n == 0 else h1_ref)[pl.ds(r0, t + 2 * hal), :]
                hprev = _shift(hext, -1 if n == 0 else 1)[hal:hal + t, :]
                gb = g_scr[n, pl.ds(r0, t), :]
                da = gb * hprev
                dm = gb * i * xc
                di = gb * m * xc
                dxc = dxc + gb * (m * i)
                dlog_a = da * a - dm * (a * a) / m
                dr = dlog_a * (-RG_C * sp[n])
                dlam[n] = dlam[n] + jnp.sum(dlog_a * r, axis=0, keepdims=True)
                dpr = dr * r * (1.0 - r)
                dpi = di * i * (1.0 - i)
                dba[n] = dba[n] + jnp.sum(dpr, axis=0, keepdims=True)
                dbi[n] = dbi[n] + jnp.sum(dpi, axis=0, keepdims=True)
                dprb, dpib = dpr.astype(BF16), dpi.astype(BF16)
                dwa_ref[n] += lax.dot_general(xcb, dprb, tn_dims, preferred_element_type=F32)
                dwi_ref[n] += lax.dot_general(xcb, dpib, tn_dims, preferred_element_type=F32)
                dxc = dxc + lax.dot_general(dprb, wa_ref[n], nt_dims, preferred_element_type=F32)
                dxc = dxc + lax.dot_general(dpib, wi_ref[n], nt_dims, preferred_element_type=F32)
            dxc_pad[pl.ds(hal + r0, t), :] = dxc
            dcb = dcb + jnp.sum(dxc, axis=0, keepdims=True)
            return tuple(dba), tuple(dbi), tuple(dlam), dcb

        zr = jnp.zeros((1, lh), F32)
        dba, dbi, dlam, dcb = lax.fori_loop(0, n_chunks, chunk3, ((zr, zr), (zr, zr), (zr, zr), zr))
        dcb_ref[...] = dcb
        for n in range(2):
            dpk_ref[pl.ds(4 + n, 1), :] = dba[n]
            dpk_ref[pl.ds(6 + n, 1), :] = dbi[n]
            dpk_ref[pl.ds(8 + n, 1), :] = dlam[n] * (RG_C * jax.nn.sigmoid(-lam[n]))
        dpk_ref[pl.ds(10, SMALL_ROWS - 10), :] = jnp.zeros((SMALL_ROWS - 10, lh), F32)

        def chunk4(ci, dtap):
            r0 = pl.multiple_of(ci * t, t)
            gext = dxc_pad[pl.ds(r0, t + 2 * hal), :]
            uext = upad[pl.ds(r0, t + 2 * hal), :]
            gmid = gext[hal:hal + t, :]
            du = pk_ref[pl.ds(1, 1), :] * gext
            for k in (0, 2, 3):
                du = du + pk_ref[pl.ds(k, 1), :] * _shift(gext, 1 - k)
            dpr_ref[0, pl.ds(r0, t), :] = du[hal:hal + t, :].astype(BF16)
            out = []
            for k in range(4):
                usl = _shift(uext, k - 1)[hal:hal + t, :]
                out.append(dtap[k] + jnp.sum(gmid * usl, axis=0, keepdims=True))
            return tuple(out)

        dtap = lax.fori_loop(0, n_chunks, chunk4, (zr, zr, zr, zr))
        for k in range(4):
            dpk_ref[pl.ds(k, 1), :] = dtap[k]

        head = pl.program_id(0)
        outs = [pltpu.make_async_copy(
            dpr_ref.at[b], dproj_ref.at[:, pl.ds(pl.multiple_of((1 + b) * dl + head * lh, lh), lh)], out_sems.at[b])
            for b in range(2)]
        for cp in outs:
            cp.start()
        for cp in outs:
            cp.wait()

    return pl.pallas_call(
        body, name="lru_bwd", grid=(N_HEADS,),
        in_specs=[_bs((None, seq, lh), lambda h: (1, 0, h)), _bs((None, seq, lh), lambda h: (2, 0, h)),
                  _bs((seq, lh), lambda h: (0, first_rec_block + h)),
                  _bs((seq + 2 * hal, lh), lambda h: (0, h)), _bs((seq + 2 * hal, lh), lambda h: (0, h)),
                  _bs((None, SMALL_ROWS, lh), lambda h: (h, 0, 0)), _bs((1, lh), lambda h: (0, h)),
                  _bs((2, None, lh, lh), lambda h: (0, h, 0, 0)), _bs((2, None, lh, lh), lambda h: (0, h, 0, 0)),
                  _bs((8, 128), lambda h: (0, 0)), ANY],
        out_specs=[ANY, _bs((None, SMALL_ROWS, lh), lambda h: (h, 0, 0)),
                   _bs((1, lh), lambda h: (0, h)),
                   _bs((2, None, lh, lh), lambda h: (0, h, 0, 0)), _bs((2, None, lh, lh), lambda h: (0, h, 0, 0))],
        out_shape=[jax.ShapeDtypeStruct((seq, 3 * dl), BF16), jax.ShapeDtypeStruct((N_HEADS, SMALL_ROWS, lh), F32),
                   jax.ShapeDtypeStruct((1, dl), F32),
                   jax.ShapeDtypeStruct((2, N_HEADS, lh, lh), F32), jax.ShapeDtypeStruct((2, N_HEADS, lh, lh), F32)],
        scratch_shapes=[pltpu.VMEM((seq + 2 * hal, lh), F32), pltpu.VMEM((2, seq + 2 * hal, lh), F32),
                        pltpu.VMEM((seq, lh), F32), pltpu.VMEM((2, seq, lh), F32),
                        pltpu.VMEM((seq + 2 * hal, lh), F32)] + [pltpu.VMEM((seq, lh), F32)] * 4
                       + [pltpu.VMEM((2, seq, lh), BF16), pltpu.SemaphoreType.DMA((2,))],
        input_output_aliases={10: 0},
        compiler_params=_params(1),
    )(p3, p3, dy, h0p, h1p, pack, conv_b, wa, wi, token, dproj_in)


def _adamw_values(w, g, m, v):
    m = ADAM_B1 * m + (1.0 - ADAM_B1) * g
    v = ADAM_B2 * v + (1.0 - ADAM_B2) * (g * g)
    m_hat = m / (1.0 - ADAM_B1 ** ADAM_STEP)
    v_hat = v / (1.0 - ADAM_B2 ** ADAM_STEP)
    delta = -ADAM_LR * (m_hat / (jnp.sqrt(v_hat) + ADAM_EPS) + ADAM_WD * w)
    return delta, m, v


def _sum_adamw(name, own, parts, w, m, v):
    rows, cols = w.shape
    n_parts = parts.shape[0]
    tr = rows
    min_rows = 8 if parts.dtype == F32 else 16
    while tr * cols * 4 > 1024 * 1024 and tr % (2 * min_rows) == 0:
        tr //= 2

    def body(*refs):
        if own is None:
            p_ref, w_ref, m_ref, v_ref, g_ref, d_ref, mo_ref, vo_ref = refs
            g = p_ref[0].astype(F32)
            rest = range(1, n_parts)
        else:
            o_ref, p_ref, w_ref, m_ref, v_ref, g_ref, d_ref, mo_ref, vo_ref = refs
            g = o_ref[...]
            rest = range(n_parts)
        for s in rest:
            g = g + p_ref[s].astype(F32)
        delta, mn, vn = _adamw_values(w_ref[...], g, m_ref[...], v_ref[...])
        g_ref[...] = g
        d_ref[...] = delta
        mo_ref[...] = mn
        vo_ref[...] = vn

    spec = _bs((tr, cols), lambda i: (i, 0))
    lead = [] if own is None else [own]
    return pl.pallas_call(
        body, name=name, grid=(rows // tr,),
        in_specs=[spec] * len(lead) + [_bs((n_parts, tr, cols), lambda i: (0, i, 0)), spec, spec, spec],
        out_specs=[spec] * 4, out_shape=[jax.ShapeDtypeStruct((rows, cols), F32)] * 4,
        compiler_params=_params(1),
    )(*lead, parts, w, m, v)


def _rows128(a):
    return a.reshape(-1, 128)


def kernel(x, ln_mix_g, ln_mix_b, w_in, w_pool, pool_scale, conv_w, conv_b, w_rg_a, b_rg_a, w_rg_i, b_rg_i, rg_lambda, w_out, ln_ffn_g, ln_ffn_b, w_mlp_in, w_mlp_out, loss_target, m_ln_mix_g, m_ln_mix_b, m_w_in, m_w_pool, m_pool_scale, m_conv_w, m_conv_b, m_w_rg_a, m_b_rg_a, m_w_rg_i, m_b_rg_i, m_rg_lambda, m_w_out, m_ln_ffn_g, m_ln_ffn_b, m_w_mlp_in, m_w_mlp_out, v_ln_mix_g, v_ln_mix_b, v_w_in, v_w_pool, v_pool_scale, v_conv_w, v_conv_b, v_w_rg_a, v_b_rg_a, v_w_rg_i, v_b_rg_i, v_rg_lambda, v_w_out, v_ln_ffn_g, v_ln_ffn_b, v_w_mlp_in, v_w_mlp_out):
    seq, d_model = x.shape[1], x.shape[2]
    dh = d_model // 2
    lh = dh // N_HEADS
    pg = dh // len(POOL_WINDOWS)
    d_ff = w_mlp_in.shape[2] * N_DEV
    assert lh == 128 and conv_w.shape[3] == lh and w_pool.shape[2] * N_DEV == pg
    tm = min(512, seq)
    ni = seq // tm

    xs = x[0]
    tgt = loss_target[0]

    def small_pack(cw, ba, bi, lam):
        return jnp.concatenate([cw.reshape(4, lh), ba.reshape(2, lh), bi.reshape(2, lh), lam.reshape(2, lh),
                                jnp.zeros((SMALL_ROWS - 10, lh), F32)], axis=0)

    pack_mine = small_pack(conv_w, b_rg_a, b_rg_i, rg_lambda)
    win_full, wpool_full, wout_full, pack_full = _all_gather("gather_mixer", [
        (w_in[0].astype(BF16), 1), (w_pool[0].astype(BF16), 1), (w_out[0].astype(BF16), 0),
        (pack_mine[None], 0)])
    mlp_gather = _SplitGather("gather_mlp", [(w_mlp_in[0].astype(BF16), 1), (w_mlp_out[0].astype(BF16), 0)],
                              after=pack_full)
    wa_b = w_rg_a[0].astype(BF16)
    wi_b = w_rg_i[0].astype(BF16)
    vec = lambda i, j, k: (0, 0)
    row_full = lambda i, j, k: (i, 0)

    def after(token):
        return (token, _bs((8, 128), vec))

    def proj_epi(acc, i, ex, out):
        out[0][...] = acc

    (p3,) = _matmul(
        "proj", xs, win_full, _bs((tm, d_model), lambda i, j, k: (i, 0)), _bs((d_model, dh), lambda i, j, k: (0, j)),
        grid=(ni, 3, 1), acc_shape=None, extras=[after(mlp_gather.token)],
        out_shape=[jax.ShapeDtypeStruct((3, seq, dh), F32)],
        out_specs=[_bs((None, tm, dh), lambda i, j, k: (j, i, 0))], epilogue=proj_epi)

    d_pool, y_half = _pool_fwd(p3, wpool_full, pool_scale, seq, d_model)
    y, h0p, h1p = _lru_fwd(p3, y_half, pack_full, conv_b, wa_b, wi_b, seq, d_model)
    relay_token = mlp_gather.relay(after=y)

    tm2 = min(256, seq)

    def mix_epi(acc, i, ex, out):
        x_ref, g_ref, b_ref = ex[:3]
        z = ALPHA * x_ref[...] + acc
        x1, _, _ = _ln_fwd(z, g_ref[...], b_ref[...])
        out[0][...] = z
        out[1][...] = x1
        out[2][...] = x1.astype(BF16)

    z1, x1, x1b = _matmul(
        "mix_out", y, wout_full, _bs((tm2, d_model), row_full), _bs((d_model, d_model), vec),
        grid=(seq // tm2, 1, 1), acc_shape=None,
        extras=[(xs, _bs((tm2, d_model), row_full)), (ln_mix_g, _bs((1, d_model), vec)),
                (ln_mix_b, _bs((1, d_model), vec)), after(relay_token)],
        out_shape=[jax.ShapeDtypeStruct((seq, d_model), F32), jax.ShapeDtypeStruct((seq, d_model), F32),
                   jax.ShapeDtypeStruct((seq, d_model), BF16)],
        out_specs=[_bs((tm2, d_model), row_full)] * 3, epilogue=mix_epi)
    w1_full, w2_full = mlp_gather.wait(after=x1b)

    tn = min(1024, d_ff)

    def mlp_in_epi(acc, i, ex, out):
        out[0][...] = acc
        h = jnp.maximum(acc, 0.0)
        out[1][...] = (h * h).astype(BF16)

    pre, hmid = _matmul(
        "mlp_in", x1b, w1_full, _bs((tm, d_model), lambda i, j, k: (i, 0)), _bs((d_model, tn), lambda i, j, k: (0, j)),
        grid=(ni, d_ff // tn, 1), acc_shape=None,
        out_shape=[jax.ShapeDtypeStruct((seq, d_ff), F32), jax.ShapeDtypeStruct((seq, d_ff), BF16)],
        out_specs=[_bs((tm, tn), lambda i, j, k: (i, j))] * 2, epilogue=mlp_in_epi)

    tk = min(1024, d_ff)

    def mlp_out_epi(acc, i, ex, out):
        x1_ref, t_ref, g_ref, b_ref = ex
        z = ALPHA * x1_ref[...] + acc
        g = g_ref[...]
        x2, xhat, rstd = _ln_fwd(z, g, b_ref[...])
        err = x2 - t_ref[...]
        part = 0.5 * jnp.sum(jnp.mean(err * err, axis=-1, keepdims=True), axis=0, keepdims=True)
        dz, dg, db = _ln_bwd(err * (1.0 / d_model), xhat, rstd, g)
        out[0][...] = dz
        out[1][...] = dz.astype(BF16)
        _acc_rows(out[2], i == 0, dg)
        _acc_rows(out[3], i == 0, db)
        _acc_rows(out[4], i == 0, jnp.broadcast_to(part, (8, 128)))

    dz2, dz2b, g_ffn_g, g_ffn_b, loss_part = _matmul(
        "mlp_out", hmid, w2_full, _bs((tm2, tk), lambda i, j, k: (i, k)), _bs((tk, d_model), lambda i, j, k: (k, 0)),
        grid=(seq // tm2, 1, d_ff // tk), acc_shape=(tm2, d_model),
        extras=[(x1, _bs((tm2, d_model), row_full)), (tgt, _bs((tm2, d_model), row_full)),
                (ln_ffn_g, _bs((1, d_model), vec)), (ln_ffn_b, _bs((1, d_model), vec))],
        out_shape=[jax.ShapeDtypeStruct((seq, d_model), F32), jax.ShapeDtypeStruct((seq, d_model), BF16),
                   jax.ShapeDtypeStruct((1, d_model), F32), jax.ShapeDtypeStruct((1, d_model), F32),
                   jax.ShapeDtypeStruct((8, 128), F32)],
        out_specs=[_bs((tm2, d_model), row_full)] * 2 + [_bs((1, d_model), vec)] * 2 + [_bs((8, 128), vec)],
        epilogue=mlp_out_epi)

    def dpre_epi(acc, i, ex, out):
        out[0][...] = (acc * (2.0 * jnp.maximum(ex[0][...], 0.0))).astype(BF16)

    (dpre,) = _matmul(
        "mlp_dpre", dz2b, w2_full, _bs((tm, d_model), lambda i, j, k: (i, 0)), _bs((tn, d_model), lambda i, j, k: (j, 0)),
        grid=(ni, d_ff // tn, 1), tb=True, acc_shape=None,
        extras=[(pre, _bs((tm, tn), lambda i, j, k: (i, j)))],
        out_shape=[jax.ShapeDtypeStruct((seq, d_ff), BF16)], out_specs=[_bs((tm, tn), lambda i, j, k: (i, j))],
        epilogue=dpre_epi)

    def dx1_epi(acc, i, ex, out):
        dz2_ref, z1_ref, g_ref, b_ref = ex
        g = g_ref[...]
        _, xhat, rstd = _ln_fwd(z1_ref[...], g, b_ref[...])
        dz, dg, db = _ln_bwd(ALPHA * dz2_ref[...] + acc, xhat, rstd, g)
        out[0][...] = dz
        out[1][...] = dz.astype(BF16)
        _acc_rows(out[2], i == 0, dg)
        _acc_rows(out[3], i == 0, db)

    dz1, dz1b, g_mix_g, g_mix_b = _matmul(
        "mlp_dx", dpre, w1_full, _bs((tm2, tk), lambda i, j, k: (i, k)), _bs((d_model, tk), lambda i, j, k: (0, k)),
        grid=(seq // tm2, 1, d_ff // tk), tb=True, acc_shape=(tm2, d_model),
        extras=[(dz2, _bs((tm2, d_model), row_full)), (z1, _bs((tm2, d_model), row_full)),
                (ln_mix_g, _bs((1, d_model), vec)), (ln_mix_b, _bs((1, d_model), vec))],
        out_shape=[jax.ShapeDtypeStruct((seq, d_model), F32), jax.ShapeDtypeStruct((seq, d_model), BF16),
                   jax.ShapeDtypeStruct((1, d_model), F32), jax.ShapeDtypeStruct((1, d_model), F32)],
        out_specs=[_bs((tm2, d_model), row_full)] * 2 + [_bs((1, d_model), vec)] * 2, epilogue=dx1_epi)

    def plain_epi(acc, i, ex, out):
        out[0][...] = acc

    def bf16_epi(acc, i, ex, out):
        out[0][...] = acc.astype(BF16)

    tks = min(512, seq)
    nks = seq // tks
    twm = 512
    (g_w2,) = _matmul(
        "grad_w_mlp_out", hmid, dz2b, _bs((tks, twm), lambda i, j, k: (k, i)), _bs((tks, d_model), lambda i, j, k: (k, 0)),
        grid=(d_ff // twm, 1, nks), ta=True, acc_shape=(twm, d_model),
        out_shape=[jax.ShapeDtypeStruct((d_ff, d_model), BF16)], out_specs=[_bs((twm, d_model), lambda i, j, k: (i, 0))],
        epilogue=bf16_epi)
    g_w2 = g_w2.reshape(N_DEV, d_ff // N_DEV, d_model)

    def pair_epi(width):
        def epi(acc, i, ex, out):
            out[0][0] = acc[:, :width].astype(BF16)
            out[0][1] = acc[:, width:].astype(BF16)
        return epi

    fs = d_ff // N_DEV
    (g_w1,) = _matmul(
        "grad_w_mlp_in", x1b, dpre, _bs((tks, twm), lambda i, j, k: (k, i)), _bs((tks, 2 * fs), lambda i, j, k: (k, j)),
        grid=(d_model // twm, N_DEV // 2, nks), ta=True, acc_shape=(twm, 2 * fs),
        out_shape=[jax.ShapeDtypeStruct((N_DEV, d_model, fs), BF16)],
        out_specs=[_bs((2, twm, fs), lambda i, j, k: (j, i, 0))], epilogue=pair_epi(fs))

    (dy,) = _matmul(
        "mix_dy", dz1b, wout_full, _bs((tm, d_model), lambda i, j, k: (i, 0)), _bs((dh, d_model), lambda i, j, k: (j, 0)),
        grid=(ni, 2, 1), tb=True, acc_shape=None,
        out_shape=[jax.ShapeDtypeStruct((seq, d_model), F32)], out_specs=[_bs((tm, dh), lambda i, j, k: (i, j))],
        epilogue=plain_epi)
    (g_wout,) = _matmul(
        "grad_w_out", y, dz1b, _bs((tks, twm), lambda i, j, k: (k, i)), _bs((tks, d_model), lambda i, j, k: (k, 0)),
        grid=(d_model // twm, 1, nks), ta=True, acc_shape=(twm, d_model),
        out_shape=[jax.ShapeDtypeStruct((d_model, d_model), BF16)], out_specs=[_bs((twm, d_model), lambda i, j, k: (i, 0))],
        epilogue=bf16_epi)
    g_wout = g_wout.reshape(N_DEV, d_model // N_DEV, d_model)

    scatter_a = _SplitReduceScatter("scatter_a", [g_w1, g_w2, g_wout])

    dproj_pool, g_wpool, g_pscale = _pool_bwd(d_pool, dy, wpool_full, pool_scale, scatter_a.token, seq, d_model)
    token_a = scatter_a.combine_and_send(after=dproj_pool)
    dproj, g_pack, g_convb, g_wa, g_wi = _lru_bwd(p3, dy, h0p, h1p, dproj_pool, pack_full, conv_b, wa_b, wi_b,
                                                  token_a, seq, d_model)

    rep_parts = [_rows128(g_wa), _rows128(g_wi), _rows128(g_mix_g), _rows128(g_mix_b), _rows128(g_ffn_g),
                 _rows128(g_ffn_b), _rows128(g_pscale), _rows128(g_convb)]
    rep_rows = [p.shape[0] for p in rep_parts]
    n_rep = sum(rep_rows)
    small = jnp.concatenate(rep_parts + [_rows128(g_pack)], axis=0)
    small_gather = _SplitGather("gather_small_grads", [(small[None], 0)], after=small)

    ws = 3 * dh // N_DEV
    (g_win,) = _matmul(
        "grad_w_in", xs, dproj, _bs((tks, twm), lambda i, j, k: (k, i)), _bs((tks, 2 * ws), lambda i, j, k: (k, j)),
        grid=(d_model // twm, N_DEV // 2, nks), ta=True, acc_shape=(twm, 2 * ws),
        extras=[after(small_gather.token)],
        out_shape=[jax.ShapeDtypeStruct((N_DEV, d_model, ws), BF16)],
        out_specs=[_bs((2, twm, ws), lambda i, j, k: (j, i, 0))], epilogue=pair_epi(ws))
    scatter_b = _SplitReduceScatter("scatter_b", [g_win, g_wpool.reshape(N_DEV, pg // N_DEV * len(POOL_WINDOWS), pg)])

    def dx_epi(acc, i, ex, out):
        out[0][...] = ALPHA * ex[0][...] + acc

    (dx,) = _matmul(
        "grad_x", dproj, win_full, _bs((tm, dh), lambda i, j, k: (i, k)), _bs((dh, dh), lambda i, j, k: (j, k)),
        grid=(ni, 2, 3), tb=True, acc_shape=(tm, dh),
        extras=[(dz1, _bs((tm, dh), lambda i, j, k: (i, j))), after(scatter_b.token)],
        out_shape=[jax.ShapeDtypeStruct((seq, d_model), F32)], out_specs=[_bs((tm, dh), lambda i, j, k: (i, j))],
        epilogue=dx_epi)
    token_b = scatter_b.combine_and_send(after=dx)

    def adam_big(name, own_landed, w, m, v):
        own, landed = own_landed
        shp = w.shape
        two = lambda a: a.reshape(-1, shp[-1])
        res = _sum_adamw(name, own, landed, two(w), two(m), two(v))
        return [r.reshape(shp) for r in res]

    r_w1, r_w2, r_wout = scatter_a.wait(after=token_b)
    o_w1 = adam_big("adam_w_mlp_in", r_w1, w_mlp_in, m_w_mlp_in, v_w_mlp_in)
    o_w2 = adam_big("adam_w_mlp_out", r_w2, w_mlp_out, m_w_mlp_out, v_w_mlp_out)
    o_wout = adam_big("adam_w_out", r_wout, w_out, m_w_out, v_w_out)
    r_win, r_wpool = scatter_b.wait(after=o_wout[0])
    o_win = adam_big("adam_w_in", r_win, w_in, m_w_in, v_w_in)
    o_wpool = adam_big("adam_w_pool", r_wpool, w_pool, m_w_pool, v_w_pool)

    small_gather.relay(after=o_win[0])
    (small_all,) = small_gather.wait(after=o_wpool[0])

    rep_w = [w_rg_a, w_rg_i, ln_mix_g, ln_mix_b, ln_ffn_g, ln_ffn_b, pool_scale, conv_b]
    rep_m = [m_w_rg_a, m_w_rg_i, m_ln_mix_g, m_ln_mix_b, m_ln_ffn_g, m_ln_ffn_b, m_pool_scale, m_conv_b]
    rep_v = [v_w_rg_a, v_w_rg_i, v_ln_mix_g, v_ln_mix_b, v_ln_ffn_g, v_ln_ffn_b, v_pool_scale, v_conv_b]
    cat = lambda arrs: jnp.concatenate([_rows128(a) for a in arrs], axis=0)
    o_rep = _sum_adamw("adam_replicated", None, small_all[:, :n_rep, :], cat(rep_w), cat(rep_m), cat(rep_v))

    my_idx = _dev_index(_where_am_i())
    head_parts = lax.dynamic_slice_in_dim(small_all, n_rep + my_idx * SMALL_ROWS, SMALL_ROWS, axis=1)
    o_head = _sum_adamw("adam_head", None, head_parts, pack_mine,
                        small_pack(m_conv_w, m_b_rg_a, m_b_rg_i, m_rg_lambda),
                        small_pack(v_conv_w, v_b_rg_a, v_b_rg_i, v_rg_lambda))

    def unpack_rep(packed):
        out, r = [], 0
        for wgt, rows in zip(rep_w, rep_rows):
            out.append(packed[r:r + rows].reshape(wgt.shape))
            r += rows
        return out

    def unpack_head(packed):
        return [packed[0:4].reshape(conv_w.shape), packed[4:6].reshape(b_rg_a.shape),
                packed[6:8].reshape(b_rg_i.shape), packed[8:10].reshape(rg_lambda.shape)]

    loss = lax.psum(loss_part[0, 0], ("x", "y", "c"))

    outs = [loss, dx[None]]
    for kind in range(4):
        ra, ri, mg, mb, fg, fb, ps, cb = unpack_rep(o_rep[kind])
        cw, ba, bi, lam = unpack_head(o_head[kind])
        outs += [mg, mb, o_win[kind], o_wpool[kind], ps, cw, cb, ra, ba, ri, bi, lam, o_wout[kind], fg, fb,
                 o_w1[kind], o_w2[kind]]
    return tuple(outs)
```

```python
import functools

import jax
import jax.numpy as jnp
from jax import lax
from jax.experimental import pallas as pl
from jax.experimental.pallas import tpu as pltpu

F32 = jnp.float32
BF16 = jnp.bfloat16
MESH = pl.DeviceIdType.MESH
ANY = pl.BlockSpec(memory_space=pl.ANY)

N_DEV = 8
POOL_WINDOWS = (2, 4, 8, 16)
N_HEADS = 8
RG_C = 8.0
LN_EPS = 1e-5
ALPHA = 2.0 ** 0.25
ADAM_LR = 0.001
ADAM_B1 = 0.9
ADAM_B2 = 0.999
ADAM_EPS = 1e-08
ADAM_WD = 0.01
ADAM_STEP = 10

VMEM_LIMIT = 56 * 1024 * 1024
WIN_HALO = 16
CONV_HALO = 8
SMALL_ROWS = 16


def _params(n_grid):
    return pltpu.CompilerParams(dimension_semantics=("arbitrary",) * n_grid, vmem_limit_bytes=VMEM_LIMIT)


def _shift(v, j):
    n = v.shape[0]
    s = (-j) % n
    return v if s == 0 else pltpu.roll(v, s, 0)


def _expm1(x):
    poly = x * (1.0 + x * (0.5 + x * (1.0 / 6.0 + x * (1.0 / 24.0 + x * (1.0 / 120.0)))))
    return jnp.where(jnp.abs(x) < 0.1, poly, jnp.exp(x) - 1.0)


def _softplus(z):
    e = jnp.exp(-jnp.abs(z))
    u = 1.0 + e
    log1p = jnp.where(u == 1.0, e, jnp.log(u) * (e / jnp.where(u == 1.0, 1.0, u - 1.0)))
    return jnp.maximum(z, 0.0) + log1p


_GELU_C = 0.7978845608028654
_GELU_K = 0.044715


def _gelu_and_grad(x):
    x2 = x * x
    t = jnp.tanh(_GELU_C * (x + _GELU_K * x * x2))
    g = 0.5 * x * (1.0 + t)
    dg = 0.5 * (1.0 + t) + 0.5 * x * (1.0 - t * t) * (_GELU_C * (1.0 + 3.0 * _GELU_K * x2))
    return g, dg


def _ln_fwd(z, g, b):
    mu = jnp.mean(z, axis=-1, keepdims=True)
    zc = z - mu
    var = jnp.mean(zc * zc, axis=-1, keepdims=True)
    rstd = lax.rsqrt(var + LN_EPS)
    xhat = zc * rstd
    return xhat * g + b, xhat, rstd


def _ln_bwd(dy, xhat, rstd, g):
    dxhat = dy * g
    m1 = jnp.mean(dxhat, axis=-1, keepdims=True)
    m2 = jnp.mean(dxhat * xhat, axis=-1, keepdims=True)
    dz = rstd * (dxhat - m1 - xhat * m2)
    dg = jnp.sum(dy * xhat, axis=0, keepdims=True)
    db = jnp.sum(dy, axis=0, keepdims=True)
    return dz, dg, db


def _acc_rows(ref, first, val):
    @pl.when(first)
    def _():
        ref[...] = val

    @pl.when(jnp.logical_not(first))
    def _():
        ref[...] += val


def _sp(shape, fn, single=False):
    return shape, fn, single


def _matmul(name, a, b, a_spec, b_spec, *, grid, j_outer=False, ta=False, tb=False, extras=(), out_shape, out_specs,
            epilogue=None):
    ni, nj, nk = grid
    n_ex = len(extras)
    dims = (((0 if ta else 1,), (1 if tb else 0,)), ((), ()))

    def mk(spec):
        shape, fn, single = spec
        index = (lambda g0, g1, g2: fn(g1, g0, g2)) if j_outer else fn
        return pl.BlockSpec(shape, index, pipeline_mode=pl.Buffered(1)) if single else pl.BlockSpec(shape, index)

    def body(a_ref, b_ref, *rest):
        ex_refs = rest[:n_ex]
        out_refs = rest[n_ex:]
        i = pl.program_id(1 if j_outer else 0)
        part = lax.dot_general(a_ref[...].astype(BF16), b_ref[...].astype(BF16), dims, preferred_element_type=F32)
        if nk == 1:
            epilogue(part, i, ex_refs, out_refs)
        else:
            @pl.when(pl.program_id(2) == 0)
            def _():
                out_refs[0][...] = jnp.zeros(out_refs[0].shape, F32)

            out_refs[0][...] += part

    return pl.pallas_call(
        body, name=name, grid=(nj, ni, nk) if j_outer else (ni, nj, nk),
        in_specs=[mk(a_spec), mk(b_spec)] + [mk(s) for _, s in extras],
        out_specs=[mk(s) for s in out_specs], out_shape=list(out_shape),
        compiler_params=_params(3),
    )(a, b, *[x for x, _ in extras])


def _bs(shape, fn):
    return pl.BlockSpec(shape, fn)


def _where_am_i():
    x, y, c = lax.axis_index("x"), lax.axis_index("y"), lax.axis_index("c")
    return x, y, c


def _dev_index(p):
    return 4 * p[0] + 2 * p[1] + p[2]


def _slab(ref, axis, idx, size):
    sl = [slice(None)] * len(ref.shape)
    sl[axis] = pl.ds(idx * size, size)
    return ref.at[tuple(sl)]


def _all_gather(name, items):
    n = len(items)
    shapes = []
    for shard, axis in items:
        s = list(shard.shape)
        s[axis] *= N_DEV
        shapes.append(jax.ShapeDtypeStruct(tuple(s), shard.dtype))

    def body(*refs):
        in_refs, out_refs = refs[:n], refs[n:2 * n]
        send_sems, recv_sems, local_sems = refs[2 * n:]
        x, y, c = _where_am_i()
        me, sibling = (x, y, c), (x, y, 1 - c)
        chips = [(1 - x, y), (x, 1 - y), (1 - x, 1 - y)]

        def blk(a, p):
            axis = items[a][1]
            return _slab(out_refs[a], axis, _dev_index(p), items[a][0].shape[axis])

        def copy(a, k, block, to, src=None):
            return pltpu.make_async_remote_copy(
                src_ref=blk(a, block) if src is None else src, dst_ref=blk(a, block),
                send_sem=send_sems.at[a, k], recv_sem=recv_sems.at[a, k], device_id=to, device_id_type=MESH)

        mine = [pltpu.make_async_copy(in_refs[a], blk(a, me), local_sems.at[a]) for a in range(n)]
        for cp in mine:
            cp.start()
        first = []
        for a in range(n):
            first.append(copy(a, 0, me, sibling, src=in_refs[a]))
            first += [copy(a, 1 + j, me, (*chip, c), src=in_refs[a]) for j, chip in enumerate(chips)]
        for cp in first:
            cp.start()
        passed = []
        for a in range(n):
            for j, chip in enumerate(chips):
                copy(a, 1 + j, (*chip, c), me).wait_recv()
                fw = copy(a, 4 + j, (*chip, c), sibling)
                fw.start()
                passed.append(fw)
        for a in range(n):
            copy(a, 0, sibling, me).wait_recv()
            for j, chip in enumerate(chips):
                copy(a, 4 + j, (*chip, 1 - c), me).wait_recv()
        for cp in first + passed:
            cp.wait_send()
        for cp in mine:
            cp.wait()

    outs = pl.pallas_call(
        body, name=name, out_shape=shapes, in_specs=[ANY] * n, out_specs=[ANY] * n,
        scratch_shapes=[pltpu.SemaphoreType.DMA((n, 7)), pltpu.SemaphoreType.DMA((n, 7)),
                        pltpu.SemaphoreType.DMA((n,))],
    )(*[s for s, _ in items])
    return list(outs)


HBM = pl.BlockSpec(memory_space=pltpu.HBM)
SEM = pl.BlockSpec(memory_space=pltpu.SEMAPHORE)
DATAFLOW = pltpu.SideEffectType.DATAFLOW_SIDE_EFFECTING


def _in_hbm(a):
    return pltpu.with_memory_space_constraint(a, pltpu.HBM)


def _token_shape():
    return jax.ShapeDtypeStruct((8, 128), F32)


def _split_start(name, n_sems, bufs, issue):
    nb = len(bufs)

    def body(*refs):
        issue(refs[:nb], refs[nb], refs[nb + 1])
        refs[-1][...] = jnp.zeros((8, 128), F32)

    outs = pl.pallas_call(
        body, name=name,
        out_shape=(pltpu.SemaphoreType.DMA((n_sems,)), pltpu.SemaphoreType.DMA((n_sems,)),
                   *[pltpu.HBM(b.shape, b.dtype) for b in bufs], _token_shape()),
        in_specs=[HBM] * nb, out_specs=(SEM, SEM, *[HBM] * nb, pl.BlockSpec(memory_space=pltpu.VMEM)),
        input_output_aliases={i: 2 + i for i in range(nb)},
        compiler_params=pltpu.CompilerParams(has_side_effects=DATAFLOW),
    )(*[_in_hbm(b) for b in bufs])
    return outs[0], outs[1], list(outs[2:2 + nb]), outs[-1]


def _split_relay(name, n_sems, sems, bufs, after, relay):
    nb = len(bufs)

    def body(*refs):
        relay(refs[:nb], refs[nb], refs[nb + 1], refs[nb + 3], refs[nb + 4])
        refs[-1][...] = jnp.zeros((8, 128), F32)

    outs = pl.pallas_call(
        body, name=name,
        out_shape=(pltpu.SemaphoreType.DMA((n_sems,)), pltpu.SemaphoreType.DMA((n_sems,)),
                   *[pltpu.HBM(b.shape, b.dtype) for b in bufs], _token_shape()),
        in_specs=[HBM] * nb + [SEM, SEM, ANY],
        out_specs=(SEM, SEM, *[HBM] * nb, pl.BlockSpec(memory_space=pltpu.VMEM)),
        input_output_aliases={i: 2 + i for i in range(nb)},
        compiler_params=pltpu.CompilerParams(has_side_effects=DATAFLOW),
    )(*bufs, sems[0], sems[1], after)
    return outs[0], outs[1], list(outs[2:2 + nb]), outs[-1]


def _split_wait(name, sems, bufs, after, finish):
    nb = len(bufs)

    def body(*refs):
        finish(refs[:nb], refs[nb], refs[nb + 1])

    outs = pl.pallas_call(
        body, name=name, out_shape=[pltpu.HBM(b.shape, b.dtype) for b in bufs],
        in_specs=[HBM] * nb + [SEM, SEM, ANY], out_specs=[HBM] * nb,
        input_output_aliases={i: i for i in range(nb)},
        compiler_params=pltpu.CompilerParams(has_side_effects=DATAFLOW),
    )(*bufs, sems[0], sems[1], after)
    return list(outs)


def _place(name, items, after):
    ids = jnp.reshape(_dev_index(_where_am_i()), (1,)).astype(jnp.int32)
    outs = []
    for a, (shard, axis) in enumerate(items):
        rows, cols = shard.shape[-2], shard.shape[-1]
        tr = rows
        while tr * cols * shard.dtype.itemsize > 2 * 1024 * 1024 and tr % 32 == 0:
            tr //= 2
        nt = rows // tr
        full = list(shard.shape)
        full[axis] *= N_DEV
        if shard.ndim == 2 and axis == 0:
            in_spec = _bs((tr, cols), lambda i, ids: (i, 0))
            out_spec = _bs((tr, cols), lambda i, ids, nt=nt: (ids[0] * nt + i, 0))
        elif shard.ndim == 2 and axis == 1:
            in_spec = _bs((tr, cols), lambda i, ids: (i, 0))
            out_spec = _bs((tr, cols), lambda i, ids: (i, ids[0]))
        else:
            assert shard.ndim == 3 and axis == 0 and shard.shape[0] == 1
            in_spec = _bs((None, tr, cols), lambda i, ids: (0, i, 0))
            out_spec = _bs((None, tr, cols), lambda i, ids: (ids[0], i, 0))

        def body(ids_ref, in_ref, after_ref, out_ref):
            del ids_ref, after_ref
            out_ref[...] = in_ref[...]

        outs.append(pl.pallas_call(
            body, name=f"{name}{a}",
            grid_spec=pltpu.PrefetchScalarGridSpec(
                num_scalar_prefetch=1, grid=(nt,), in_specs=[in_spec, ANY], out_specs=out_spec),
            out_shape=jax.ShapeDtypeStruct(tuple(full), shard.dtype), compiler_params=_params(1),
        )(ids, shard, after))
    return outs


class _SplitGather:
    def __init__(self, name, items, after):
        self.name, self.items, self.n = name, items, len(items)
        fulls = _place(name + "_place", items, after)
        n = self.n

        def issue(refs, send, recv):
            me, sibling, chips, c = self._geometry()
            for a in range(n):
                self._copy1(refs, send, recv, a, 0, me, sibling).start()
                for j, chip in enumerate(chips):
                    self._copy1(refs, send, recv, a, 1 + j, me, (*chip, c)).start()

        self.send, self.recv, self.bufs, self.token = _split_start(
            name + "_start", 4 * n, [s for s, _ in items] + fulls, issue)

    @staticmethod
    def _geometry():
        x, y, c = _where_am_i()
        return (x, y, c), (x, y, 1 - c), [(1 - x, y), (x, 1 - y), (1 - x, 1 - y)], c

    def _blk(self, refs, a, p):
        shard, axis = self.items[a]
        return _slab(refs[self.n + a], axis, _dev_index(p), shard.shape[axis])

    def _copy1(self, refs, send, recv, a, k, owner, to):
        return pltpu.make_async_remote_copy(
            src_ref=refs[a], dst_ref=self._blk(refs, a, owner), send_sem=send.at[4 * a + k],
            recv_sem=recv.at[4 * a + k], device_id=to, device_id_type=MESH)

    def _copy2(self, refs, send, recv, a, j, owner, to):
        return pltpu.make_async_remote_copy(
            src_ref=self._blk(refs, a, owner), dst_ref=self._blk(refs, a, owner), send_sem=send.at[3 * a + j],
            recv_sem=recv.at[3 * a + j], device_id=to, device_id_type=MESH)

    def relay(self, after):
        n = self.n

        def relay(refs, send_in, recv_in, send_out, recv_out):
            me, sibling, chips, c = self._geometry()
            for a in range(n):
                for j, chip in enumerate(chips):
                    self._copy1(refs, send_in, recv_in, a, 1 + j, (*chip, c), me).wait_recv()
                    self._copy2(refs, send_out, recv_out, a, j, (*chip, c), sibling).start()
            for a in range(n):
                self._copy1(refs, send_in, recv_in, a, 0, sibling, me).wait_recv()
                for k in range(4):
                    self._copy1(refs, send_in, recv_in, a, k, me, sibling).wait_send()

        self.send, self.recv, self.bufs, self.token = _split_relay(
            self.name + "_relay", 3 * n, (self.send, self.recv), self.bufs, after, relay)
        return self.token

    def wait(self, after):
        n = self.n

        def finish(refs, send, recv):
            me, sibling, chips, c = self._geometry()
            for a in range(n):
                for j, chip in enumerate(chips):
                    self._copy2(refs, send, recv, a, j, (*chip, 1 - c), me).wait_recv()
                    self._copy2(refs, send, recv, a, j, (*chip, c), sibling).wait_send()

        bufs = _split_wait(self.name + "_wait", (self.send, self.recv), self.bufs, after, finish)
        return bufs[n:]


class _SplitReduceScatter:
    def __init__(self, name, grads):
        self.name, self.n = name, len(grads)
        n = self.n
        g4 = [g.reshape(4, 2, *g.shape[1:]) for g in grads]
        land = [lax.empty((4, 1, *g.shape[1:]), g.dtype) for g in grads]

        def issue(refs, send, recv):
            for a in range(n):
                self._swap(refs, send, recv, a).start()

        self.send, self.recv, self.bufs, self.token = _split_start(name + "_d2d_start", n, g4 + land, issue)

    def _swap(self, refs, send, recv, a):
        x, y, c = _where_am_i()
        return pltpu.make_async_remote_copy(
            src_ref=refs[a].at[:, pl.ds(1 - c, 1)], dst_ref=refs[self.n + a], send_sem=send.at[a], recv_sem=recv.at[a],
            device_id=(x, y, 1 - c), device_id_type=MESH)

    def _hop(self, refs, send, recv, a, m):
        x, y, c = _where_am_i()
        px = (1 - x) if m & 2 else x
        py = (1 - y) if m & 1 else y
        return pltpu.make_async_remote_copy(
            src_ref=refs[a].at[2 * px + py], dst_ref=refs[self.n + a].at[m - 1], send_sem=send.at[3 * a + m - 1],
            recv_sem=recv.at[3 * a + m - 1], device_id=(px, py, c), device_id_type=MESH)

    def combine_and_send(self, after):
        n = self.n

        def finish(refs, send, recv):
            for a in range(n):
                self._swap(refs, send, recv, a).wait()

        bufs = _split_wait(self.name + "_d2d_wait", (self.send, self.recv), self.bufs, after, finish)
        x, y, c = _where_am_i()
        ids = jnp.stack([c, 2 * x + y]).astype(jnp.int32)
        self.own, sums = [], []
        for a in range(n):
            own, hb = _pair_sum(f"{self.name}_sum{a}", bufs[a], bufs[n + a], ids)
            self.own.append(own)
            sums.append(hb)
        land = [lax.empty((3, *h.shape[1:]), h.dtype) for h in sums]

        def issue(refs, send, recv):
            for a in range(n):
                for m in (1, 2, 3):
                    self._hop(refs, send, recv, a, m).start()

        self.send, self.recv, self.bufs, self.token = _split_start(self.name + "_ici_start", 3 * n, sums + land, issue)
        return self.token

    def wait(self, after):
        n = self.n

        def finish(refs, send, recv):
            for a in range(n):
                for m in (1, 2, 3):
                    self._hop(refs, send, recv, a, m).wait()

        bufs = _split_wait(self.name + "_ici_wait", (self.send, self.recv), self.bufs, after, finish)
        return list(zip(self.own, bufs[n:]))


def _pair_sum(name, g4, land, ids):
    rows, cols = g4.shape[2], g4.shape[3]
    tr = rows
    while tr * cols * 2 > 1024 * 1024 and tr % 32 == 0:
        tr //= 2

    def body(ids_ref, g_ref, l_ref, own_ref, sum_ref):
        h = g_ref[...].astype(F32) + l_ref[...].astype(F32)
        sum_ref[...] = h.astype(sum_ref.dtype)

        @pl.when(pl.program_id(1) == ids_ref[1])
        def _():
            own_ref[...] = h

    return pl.pallas_call(
        body, name=name,
        grid_spec=pltpu.PrefetchScalarGridSpec(
            num_scalar_prefetch=1, grid=(rows // tr, 4),
            in_specs=[_bs((None, None, tr, cols), lambda i, q, ids: (q, ids[0], i, 0)),
                      _bs((None, None, tr, cols), lambda i, q, ids: (q, 0, i, 0))],
            out_specs=[_bs((tr, cols), lambda i, q, ids: (i, 0)), _bs((None, tr, cols), lambda i, q, ids: (q, i, 0))]),
        out_shape=[jax.ShapeDtypeStruct((rows, cols), F32), jax.ShapeDtypeStruct((4, rows, cols), g4.dtype)],
        compiler_params=_params(2),
    )(ids, g4, land)


def _win_sum(ext, w, off):
    s = ext + _shift(ext, -1)
    if w >= 4:
        s = _shift(s, -1) + _shift(s, 1)
    if w >= 8:
        s = _shift(s, -2) + _shift(s, 2)
    if w >= 16:
        s = _shift(s, -4) + _shift(s, 4)
    return _shift(s, off) if off else s


def _inv_count(r0, t, w, seq):
    pos = r0 + lax.broadcasted_iota(jnp.int32, (t, 1), 0)
    cnt = jnp.minimum(pos + w // 2, seq) - jnp.maximum(pos - w // 2, 0)
    return 1.0 / cnt.astype(F32)


def _pool_fwd(p3, w_pool, pool_scale, seq, d_model):
    dp = d_model // 2
    pg = dp // len(POOL_WINDOWS)
    t = min(128, seq)
    n_chunks = seq // t
    h = WIN_HALO

    def body(u_ref, w_ref, sc_ref, d_ref, y_ref, pad_ref):
        g = pl.program_id(0)
        zeros = jnp.zeros((h, pg), F32)
        pad_ref[0:h, :] = zeros
        pad_ref[h + seq:h + seq + h, :] = zeros

        def fill(ci, _):
            r0 = pl.multiple_of(ci * t, t)
            pad_ref[pl.ds(h + r0, t), :] = u_ref[pl.ds(r0, t), :]
            return 0

        lax.fori_loop(0, n_chunks, fill, 0)
        wmat = w_ref[...]
        scale = sc_ref[...]
        for gi, w in enumerate(POOL_WINDOWS):
            @pl.when(g == gi)
            def _(w=w):
                def chunk(ci, _):
                    r0 = pl.multiple_of(ci * t, t)
                    ext = pad_ref[pl.ds(r0, t + 2 * h), :]
                    mean = _win_sum(ext, w, 0)[h:h + t, :] * _inv_count(r0, t, w, seq)
                    d = (mean - ext[h:h + t, :]).astype(BF16)
                    d_ref[pl.ds(r0, t), :] = d
                    q = jnp.dot(d, wmat, preferred_element_type=F32)
                    y_ref[pl.ds(r0, t), :] = (q * scale).astype(BF16)
                    return 0

                lax.fori_loop(0, n_chunks, chunk, 0)

    return pl.pallas_call(
        body, name="pool_fwd", grid=(len(POOL_WINDOWS),),
        in_specs=[_bs((None, seq, pg), lambda g: (0, 0, g)), _bs((None, pg, pg), lambda g: (g, 0, 0)),
                  _bs((1, pg), lambda g: (0, g))],
        out_specs=[_bs((seq, pg), lambda g: (0, g)), _bs((seq, pg), lambda g: (0, g))],
        out_shape=[jax.ShapeDtypeStruct((seq, dp), BF16), jax.ShapeDtypeStruct((seq, d_model), BF16)],
        scratch_shapes=[pltpu.VMEM((seq + 2 * h, pg), F32)],
        compiler_params=_params(1),
    )(p3, w_pool, pool_scale)


def _pool_bwd(d, dy, w_pool, pool_scale, token, seq, d_model):
    dp = d_model // 2
    pg = dp // len(POOL_WINDOWS)
    t = min(128, seq)
    n_chunks = seq // t
    h = WIN_HALO
    tn_dims = (((0,), (0,)), ((), ()))
    nt_dims = (((1,), (1,)), ((), ()))

    def body(d_ref, dy_ref, w_ref, sc_ref, tok_ref, du_ref, dwb_ref, dsc_ref, pad_ref, dd_ref, dw_ref):
        del tok_ref
        g = pl.program_id(0)
        zeros = jnp.zeros((h, pg), F32)
        pad_ref[0:h, :] = zeros
        pad_ref[h + seq:h + seq + h, :] = zeros
        wmat = w_ref[...]
        scale = sc_ref[...]
        for gi, w in enumerate(POOL_WINDOWS):
            @pl.when(g == gi)
            def _(w=w):
                dw_ref[...] = jnp.zeros((pg, pg), F32)

                def first(ci, dsc):
                    r0 = pl.multiple_of(ci * t, t)
                    dv = d_ref[pl.ds(r0, t), :]
                    dyv = dy_ref[pl.ds(r0, t), :]
                    q = jnp.dot(dv, wmat, preferred_element_type=F32)
                    dsc = dsc + jnp.sum(dyv * q, axis=0, keepdims=True)
                    dq = (dyv * scale).astype(BF16)
                    dw_ref[...] += lax.dot_general(dv, dq, tn_dims, preferred_element_type=F32)
                    dd = lax.dot_general(dq, wmat, nt_dims, preferred_element_type=F32)
                    dd_ref[pl.ds(r0, t), :] = dd
                    pad_ref[pl.ds(h + r0, t), :] = dd * _inv_count(r0, t, w, seq)
                    return dsc

                dsc_ref[...] = lax.fori_loop(0, n_chunks, first, jnp.zeros((1, pg), F32))
                dwb_ref[...] = dw_ref[...].reshape(N_DEV, pg // N_DEV, pg).astype(BF16)

                def second(ci, _):
                    r0 = pl.multiple_of(ci * t, t)
                    ext = pad_ref[pl.ds(r0, t + 2 * h), :]
                    back = _win_sum(ext, w, 1)[h:h + t, :]
                    du_ref[pl.ds(r0, t), :] = (back - dd_ref[pl.ds(r0, t), :]).astype(BF16)
                    return 0

                lax.fori_loop(0, n_chunks, second, 0)

    return pl.pallas_call(
        body, name="pool_bwd", grid=(len(POOL_WINDOWS),),
        in_specs=[_bs((seq, pg), lambda g: (0, g)), _bs((seq, pg), lambda g: (0, g)),
                  _bs((None, pg, pg), lambda g: (g, 0, 0)), _bs((1, pg), lambda g: (0, g)),
                  _bs((8, 128), lambda g: (0, 0))],
        out_specs=[_bs((seq, pg), lambda g: (0, g)), _bs((N_DEV, None, pg // N_DEV, pg), lambda g: (0, g, 0, 0)),
                   _bs((1, pg), lambda g: (0, g))],
        out_shape=[jax.ShapeDtypeStruct((seq, 3 * dp), BF16),
                   jax.ShapeDtypeStruct((N_DEV, len(POOL_WINDOWS), pg // N_DEV, pg), BF16),
                   jax.ShapeDtypeStruct((1, dp), F32)],
        scratch_shapes=[pltpu.VMEM((seq + 2 * h, pg), F32), pltpu.VMEM((seq, pg), F32), pltpu.VMEM((pg, pg), F32)],
        compiler_params=_params(1),
    )(d, dy, w_pool, pool_scale, token)


def _segment_scan(seg_len, loads, hs_refs, ps_refs, stores):
    lanes = hs_refs[0].shape[-1]
    row = lax.broadcasted_iota(jnp.int32, (8, lanes), 0)

    def pos_of(n, s):
        return s if n == 0 else seg_len - 1 - s

    def step1(s, carry):
        out = []
        for n in range(2):
            hh, pp = carry[n]
            pos = pos_of(n, s)
            coef, inp = loads[n](pos)
            hh = coef * hh + inp
            pp = coef * pp
            hs_refs[n][pl.ds(pl.multiple_of(pos * 8, 8), 8), :] = hh
            ps_refs[n][pl.ds(pl.multiple_of(pos * 8, 8), 8), :] = pp
            out.append((hh, pp))
        return tuple(out)

    init = tuple((jnp.zeros((8, lanes), F32), jnp.ones((8, lanes), F32)) for _ in range(2))
    ends = lax.fori_loop(0, seg_len, step1, init, unroll=4)

    entry = []
    for n in range(2):
        bb, aa = ends[n]
        for sh in (1, 2, 4):
            if n == 0:
                ok = row >= sh
                ap = jnp.where(ok, pltpu.roll(aa, sh, 0), 1.0)
                bp = jnp.where(ok, pltpu.roll(bb, sh, 0), 0.0)
            else:
                ok = row < 8 - sh
                ap = jnp.where(ok, pltpu.roll(aa, 8 - sh, 0), 1.0)
                bp = jnp.where(ok, pltpu.roll(bb, 8 - sh, 0), 0.0)
            bb = aa * bp + bb
            aa = aa * ap
        if n == 0:
            entry.append(jnp.where(row >= 1, pltpu.roll(bb, 1, 0), 0.0))
        else:
            entry.append(jnp.where(row < 7, pltpu.roll(bb, 7, 0), 0.0))

    def step2(s, _):
        for n in range(2):
            at = pl.ds(pl.multiple_of(s * 8, 8), 8)
            stores[n](s, hs_refs[n][at, :] + ps_refs[n][at, :] * entry[n])
        return 0

    lax.fori_loop(0, seg_len, step2, 0, unroll=4)


def _gates(xc, n, wa_ref, wi_ref, pk_ref, sp):
    xcb = xc.astype(BF16)
    r = jax.nn.sigmoid(jnp.dot(xcb, wa_ref[n], preferred_element_type=F32) + pk_ref[pl.ds(4 + n, 1), :])
    i = jax.nn.sigmoid(jnp.dot(xcb, wi_ref[n], preferred_element_type=F32) + pk_ref[pl.ds(6 + n, 1), :])
    log_a = (-RG_C * r) * sp[n]
    a = jnp.exp(log_a)
    m = jnp.sqrt(-_expm1(2.0 * log_a))
    return xcb, r, i, a, m


def _conv_chunk(upad_ref, pk_ref, cb, r0, t):
    ext = upad_ref[pl.ds(r0, t + 2 * CONV_HALO), :]
    acc = pk_ref[pl.ds(1, 1), :] * ext
    for k in (0, 2, 3):
        acc = acc + pk_ref[pl.ds(k, 1), :] * _shift(ext, k - 1)
    return acc[CONV_HALO:CONV_HALO + t, :] + cb, ext


def _lru_fwd(p3, y_in, pack, conv_b, wa, wi, seq, d_model):
    dl = d_model // 2
    lh = dl // N_HEADS
    t = min(128, seq)
    n_chunks = seq // t
    seg = seq // 8
    hal = CONV_HALO
    first_rec_block = (d_model - dl) // lh

    def body(ur_ref, ug_ref, pk_ref, cb_ref, wa_ref, wi_ref, yin_ref, y_ref, h0_ref, h1_ref,
             upad, a_scr, b_scr, hs0, hs1, ps0, ps1):
        del yin_ref
        zeros = jnp.zeros((hal, lh), F32)
        upad[0:hal, :] = zeros
        upad[hal + seq:hal + seq + hal, :] = zeros
        for ref in (h0_ref, h1_ref):
            ref[0:hal, :] = zeros
            ref[hal + seq:hal + seq + hal, :] = zeros

        def fill(ci, _):
            r0 = pl.multiple_of(ci * t, t)
            upad[pl.ds(hal + r0, t), :] = ur_ref[pl.ds(r0, t), :]
            return 0

        lax.fori_loop(0, n_chunks, fill, 0)
        cb = cb_ref[...]
        sp = [_softplus(-pk_ref[pl.ds(8 + n, 1), :]) for n in range(2)]

        def chunk(ci, _):
            r0 = pl.multiple_of(ci * t, t)
            xc, _ext = _conv_chunk(upad, pk_ref, cb, r0, t)
            for n in range(2):
                _, _, i, a, m = _gates(xc, n, wa_ref, wi_ref, pk_ref, sp)
                a_scr[n, pl.ds(r0, t), :] = a
                b_scr[n, pl.ds(r0, t), :] = (m * i) * xc
            return 0

        lax.fori_loop(0, n_chunks, chunk, 0)

        def load(n):
            return lambda pos: (a_scr[n, pl.ds(pos, 8, stride=seg), :], b_scr[n, pl.ds(pos, 8, stride=seg), :])

        def store(n, ref):
            def put(s, v):
                ref[pl.ds(hal + s, 8, stride=seg), :] = v
            return put

        _segment_scan(seg, [load(0), load(1)], [hs0, hs1], [ps0, ps1], [store(0, h0_ref), store(1, h1_ref)])

        def out(ci, _):
            r0 = pl.multiple_of(ci * t, t)
            hsum = h0_ref[pl.ds(hal + r0, t), :] + h1_ref[pl.ds(hal + r0, t), :]
            gl, _dg = _gelu_and_grad(ug_ref[pl.ds(r0, t), :])
            y_ref[pl.ds(r0, t), :] = (hsum * gl).astype(BF16)
            return 0

        lax.fori_loop(0, n_chunks, out, 0)

    return pl.pallas_call(
        body, name="lru_fwd", grid=(N_HEADS,),
        in_specs=[_bs((None, seq, lh), lambda h: (1, 0, h)), _bs((None, seq, lh), lambda h: (2, 0, h)),
                  _bs((None, SMALL_ROWS, lh), lambda h: (h, 0, 0)), _bs((1, lh), lambda h: (0, h)),
                  _bs((2, None, lh, lh), lambda h: (0, h, 0, 0)), _bs((2, None, lh, lh), lambda h: (0, h, 0, 0)),
                  ANY],
        out_specs=[_bs((seq, lh), lambda h: (0, first_rec_block + h)),
                   _bs((seq + 2 * hal, lh), lambda h: (0, h)), _bs((seq + 2 * hal, lh), lambda h: (0, h))],
        out_shape=[jax.ShapeDtypeStruct((seq, d_model), BF16), jax.ShapeDtypeStruct((seq + 2 * hal, dl), F32),
                   jax.ShapeDtypeStruct((seq + 2 * hal, dl), F32)],
        scratch_shapes=[pltpu.VMEM((seq + 2 * hal, lh), F32), pltpu.VMEM((2, seq, lh), F32),
                        pltpu.VMEM((2, seq, lh), F32)] + [pltpu.VMEM((seq, lh), F32)] * 4,
        input_output_aliases={6: 0},
        compiler_params=_params(1),
    )(p3, p3, pack, conv_b, wa, wi, y_in)


def _lru_bwd(p3, dy, h0p, h1p, dproj_in, pack, conv_b, wa, wi, token, seq, d_model):
    dl = d_model // 2
    lh = dl // N_HEADS
    t = min(128, seq)
    n_chunks = seq // t
    seg = seq // 8
    hal = CONV_HALO
    first_rec_block = (d_model - dl) // lh
    tn_dims = (((0,), (0,)), ((), ()))
    nt_dims = (((1,), (1,)), ((), ()))

    def body(ur_ref, ug_ref, dy_ref, h0_ref, h1_ref, pk_ref, cb_ref, wa_ref, wi_ref, tok_ref, din_ref,
             dproj_ref, dpk_ref, dcb_ref, dwa_ref, dwi_ref,
             upad, a_scr, dh_scr, g_scr, dxc_pad, hs0, hs1, ps0, ps1, dpr_ref, out_sems):
        del din_ref, tok_ref
        zeros = jnp.zeros((hal, lh), F32)
        for ref in (upad, dxc_pad):
            ref[0:hal, :] = zeros
            ref[hal + seq:hal + seq + hal, :] = zeros
        for n in range(2):
            a_scr[n, 0:hal, :] = zeros
            a_scr[n, hal + seq:hal + seq + hal, :] = zeros

        def fill(ci, _):
            r0 = pl.multiple_of(ci * t, t)
            upad[pl.ds(hal + r0, t), :] = ur_ref[pl.ds(r0, t), :]
            return 0

        lax.fori_loop(0, n_chunks, fill, 0)
        cb = cb_ref[...]
        lam = [pk_ref[pl.ds(8 + n, 1), :] for n in range(2)]
        sp = [_softplus(-lam[n]) for n in range(2)]

        def chunk1(ci, _):
            r0 = pl.multiple_of(ci * t, t)
            xc, _ext = _conv_chunk(upad, pk_ref, cb, r0, t)
            for n in range(2):
                _, _, _, a, _ = _gates(xc, n, wa_ref, wi_ref, pk_ref, sp)
                a_scr[n, pl.ds(hal + r0, t), :] = a
            hsum = h0_ref[pl.ds(hal + r0, t), :] + h1_ref[pl.ds(hal + r0, t), :]
            gl, dgl = _gelu_and_grad(ug_ref[pl.ds(r0, t), :])
            dyv = dy_ref[pl.ds(r0, t), :]
            dh_scr[pl.ds(r0, t), :] = dyv * gl
            dpr_ref[1, pl.ds(r0, t), :] = ((dyv * hsum) * dgl).astype(BF16)
            return 0

        lax.fori_loop(0, n_chunks, chunk1, 0)

        def load(n):
            def get(pos):
                coef = a_scr[n, pl.ds(hal + pos + (1 if n == 0 else -1), 8, stride=seg), :]
                return coef, dh_scr[pl.ds(pos, 8, stride=seg), :]
            return get

        def store(n):
            def put(s, v):
                g_scr[n, pl.ds(s, 8, stride=seg), :] = v
            return put

        _segment_scan(seg, [load(1), load(0)], [hs0, hs1], [ps0, ps1], [store(1), store(0)])

        dwa_ref[...] = jnp.zeros((2, lh, lh), F32)
        dwi_ref[...] = jnp.zeros((2, lh, lh), F32)

        def chunk3(ci, carry):
            dba, dbi, dlam, dcb = carry
            r0 = pl.multiple_of(ci * t, t)
            xc, _ext = _conv_chunk(upad, pk_ref, cb, r0, t)
            dxc = jnp.zeros((t, lh), F32)
            dba, dbi, dlam = list(dba), list(dbi), list(dlam)
            for n in range(2):
                xcb, r, i, a, m = _gates(xc, n, wa_ref, wi_ref, pk_ref, sp)
                hext = (h0_ref if n == 0 else h1_ref)[pl.ds(r0, t + 2 * hal), :]
                hprev = _shift(hext, -1 if n == 0 else 1)[hal:hal + t, :]
                gb = g_scr[n, pl.ds(r0, t), :]
                da = gb * hprev
                dm = gb * i * xc
                di = gb * m * xc
                dxc = dxc + gb * (m * i)
                dlog_a = da * a - dm * (a * a) / m
                dr = dlog_a * (-RG_C * sp[n])
                dlam[n] = dlam[n] + jnp.sum(dlog_a * r, axis=0, keepdims=True)
                dpr = dr * r * (1.0 - r)
                dpi = di * i * (1.0 - i)
                dba[n] = dba[n] + jnp.sum(dpr, axis=0, keepdims=True)
                dbi[n] = dbi[n] + jnp.sum(dpi, axis=0, keepdims=True)
                dprb, dpib = dpr.astype(BF16), dpi.astype(BF16)
                dwa_ref[n] += lax.dot_general(xcb, dprb, tn_dims, preferred_element_type=F32)
                dwi_ref[n] += lax.dot_general(xcb, dpib, tn_dims, preferred_element_type=F32)
                dxc = dxc + lax.dot_general(dprb, wa_ref[n], nt_dims, preferred_element_type=F32)
                dxc = dxc + lax.dot_general(dpib, wi_ref[n], nt_dims, preferred_element_type=F32)
            dxc_pad[pl.ds(hal + r0, t), :] = dxc
            dcb = dcb + jnp.sum(dxc, axis=0, keepdims=True)
            return tuple(dba), tuple(dbi), tuple(dlam), dcb

        zr = jnp.zeros((1, lh), F32)
        dba, dbi, dlam, dcb = lax.fori_loop(0, n_chunks, chunk3, ((zr, zr), (zr, zr), (zr, zr), zr))
        dcb_ref[...] = dcb
        for n in range(2):
            dpk_ref[pl.ds(4 + n, 1), :] = dba[n]
            dpk_ref[pl.ds(6 + n, 1), :] = dbi[n]
            dpk_ref[pl.ds(8 + n, 1), :] = dlam[n] * (RG_C * jax.nn.sigmoid(-lam[n]))
        dpk_ref[pl.ds(10, SMALL_ROWS - 10), :] = jnp.zeros((SMALL_ROWS - 10, lh), F32)

        def chunk4(ci, dtap):
            r0 = pl.multiple_of(ci * t, t)
            gext = dxc_pad[pl.ds(r0, t + 2 * hal), :]
            uext = upad[pl.ds(r0, t + 2 * hal), :]
            gmid = gext[hal:hal + t, :]
            du = pk_ref[pl.ds(1, 1), :] * gext
            for k in (0, 2, 3):
                du = du + pk_ref[pl.ds(k, 1), :] * _shift(gext, 1 - k)
            dpr_ref[0, pl.ds(r0, t), :] = du[hal:hal + t, :].astype(BF16)
            out = []
            for k in range(4):
                usl = _shift(uext, k - 1)[hal:hal + t, :]
                out.append(dtap[k] + jnp.sum(gmid * usl, axis=0, keepdims=True))
            return tuple(out)

        dtap = lax.fori_loop(0, n_chunks, chunk4, (zr, zr, zr, zr))
        for k in range(4):
            dpk_ref[pl.ds(k, 1), :] = dtap[k]

        head = pl.program_id(0)
        outs = [pltpu.make_async_copy(
            dpr_ref.at[b], dproj_ref.at[:, pl.ds(pl.multiple_of((1 + b) * dl + head * lh, lh), lh)], out_sems.at[b])
            for b in range(2)]
        for cp in outs:
            cp.start()
        for cp in outs:
            cp.wait()

    return pl.pallas_call(
        body, name="lru_bwd", grid=(N_HEADS,),
        in_specs=[_bs((None, seq, lh), lambda h: (1, 0, h)), _bs((None, seq, lh), lambda h: (2, 0, h)),
                  _bs((seq, lh), lambda h: (0, first_rec_block + h)),
                  _bs((seq + 2 * hal, lh), lambda h: (0, h)), _bs((seq + 2 * hal, lh), lambda h: (0, h)),
                  _bs((None, SMALL_ROWS, lh), lambda h: (h, 0, 0)), _bs((1, lh), lambda h: (0, h)),
                  _bs((2, None, lh, lh), lambda h: (0, h, 0, 0)), _bs((2, None, lh, lh), lambda h: (0, h, 0, 0)),
                  _bs((8, 128), lambda h: (0, 0)), ANY],
        out_specs=[ANY, _bs((None, SMALL_ROWS, lh), lambda h: (h, 0, 0)),
                   _bs((1, lh), lambda h: (0, h)),
                   _bs((2, None, lh, lh), lambda h: (0, h, 0, 0)), _bs((2, None, lh, lh), lambda h: (0, h, 0, 0))],
        out_shape=[jax.ShapeDtypeStruct((seq, 3 * dl), BF16), jax.ShapeDtypeStruct((N_HEADS, SMALL_ROWS, lh), F32),
                   jax.ShapeDtypeStruct((1, dl), F32),
                   jax.ShapeDtypeStruct((2, N_HEADS, lh, lh), F32), jax.ShapeDtypeStruct((2, N_HEADS, lh, lh), F32)],
        scratch_shapes=[pltpu.VMEM((seq + 2 * hal, lh), F32), pltpu.VMEM((2, seq + 2 * hal, lh), F32),
                        pltpu.VMEM((seq, lh), F32), pltpu.VMEM((2, seq, lh), F32),
                        pltpu.VMEM((seq + 2 * hal, lh), F32)] + [pltpu.VMEM((seq, lh), F32)] * 4
                       + [pltpu.VMEM((2, seq, lh), BF16), pltpu.SemaphoreType.DMA((2,))],
        input_output_aliases={10: 0},
        compiler_params=_params(1),
    )(p3, p3, dy, h0p, h1p, pack, conv_b, wa, wi, token, dproj_in)


class _tiles:
    def __init__(self, seq, d_model, d_ff):
        self.rows = min(1024, seq)
        self.ln_rows = min(256, seq)
        self.ff_cols = min(1024, d_ff)
        self.ff_k = min(1024, d_ff)
        self.grad_rows = 512


def _ln_loss_bwd(ffn, x1, tgt, g, b, tr):
    seq, d = ffn.shape

    def body(f_ref, x_ref, t_ref, g_ref, b_ref, dz_ref, dzb_ref, dg_ref, db_ref, loss_ref):
        i = pl.program_id(0)
        gv = g_ref[...]
        z = ALPHA * x_ref[...] + f_ref[...]
        y, xhat, rstd = _ln_fwd(z, gv, b_ref[...])
        err = y - t_ref[...]
        part = 0.5 * jnp.sum(jnp.mean(err * err, axis=-1, keepdims=True), axis=0, keepdims=True)
        dz, dg, db = _ln_bwd(err * (1.0 / d), xhat, rstd, gv)
        dz_ref[...] = dz
        dzb_ref[...] = dz.astype(BF16)
        _acc_rows(dg_ref, i == 0, dg)
        _acc_rows(db_ref, i == 0, db)
        _acc_rows(loss_ref, i == 0, jnp.broadcast_to(part, (8, 128)))

    row = _bs((tr, d), lambda i: (i, 0))
    vec = _bs((1, d), lambda i: (0, 0))
    return pl.pallas_call(
        body, name="ln_ffn_loss", grid=(seq // tr,), in_specs=[row, row, row, vec, vec],
        out_specs=[row, row, vec, vec, _bs((8, 128), lambda i: (0, 0))],
        out_shape=[jax.ShapeDtypeStruct((seq, d), F32), jax.ShapeDtypeStruct((seq, d), BF16),
                   jax.ShapeDtypeStruct((1, d), F32), jax.ShapeDtypeStruct((1, d), F32),
                   jax.ShapeDtypeStruct((8, 128), F32)],
        compiler_params=_params(1),
    )(ffn, x1, tgt, g, b)


def _ln_bwd_rows(dx_branch, dres, z, g, b, tr):
    seq, d = z.shape

    def body(a_ref, r_ref, z_ref, g_ref, b_ref, dz_ref, dzb_ref, dg_ref, db_ref):
        i = pl.program_id(0)
        gv = g_ref[...]
        _, xhat, rstd = _ln_fwd(z_ref[...], gv, b_ref[...])
        dz, dg, db = _ln_bwd(ALPHA * r_ref[...] + a_ref[...], xhat, rstd, gv)
        dz_ref[...] = dz
        dzb_ref[...] = dz.astype(BF16)
        _acc_rows(dg_ref, i == 0, dg)
        _acc_rows(db_ref, i == 0, db)

    row = _bs((tr, d), lambda i: (i, 0))
    vec = _bs((1, d), lambda i: (0, 0))
    return pl.pallas_call(
        body, name="ln_mix_bwd", grid=(seq // tr,), in_specs=[row, row, row, vec, vec],
        out_specs=[row, row, vec, vec],
        out_shape=[jax.ShapeDtypeStruct((seq, d), F32), jax.ShapeDtypeStruct((seq, d), BF16),
                   jax.ShapeDtypeStruct((1, d), F32), jax.ShapeDtypeStruct((1, d), F32)],
        compiler_params=_params(1),
    )(dx_branch, dres, z, g, b)


def _adamw_values(w, g, m, v):
    m = ADAM_B1 * m + (1.0 - ADAM_B1) * g
    v = ADAM_B2 * v + (1.0 - ADAM_B2) * (g * g)
    m_hat = m / (1.0 - ADAM_B1 ** ADAM_STEP)
    v_hat = v / (1.0 - ADAM_B2 ** ADAM_STEP)
    delta = -ADAM_LR * (m_hat / (jnp.sqrt(v_hat) + ADAM_EPS) + ADAM_WD * w)
    return delta, m, v


def _sum_adamw(name, own, parts, w, m, v):
    rows, cols = w.shape
    n_parts = parts.shape[0]
    tr = rows
    min_rows = 8 if parts.dtype == F32 else 16
    while tr * cols * 4 > 1024 * 1024 and tr % (2 * min_rows) == 0:
        tr //= 2

    def body(*refs):
        if own is None:
            p_ref, w_ref, m_ref, v_ref, g_ref, d_ref, mo_ref, vo_ref = refs
            g = p_ref[0].astype(F32)
            rest = range(1, n_parts)
        else:
            o_ref, p_ref, w_ref, m_ref, v_ref, g_ref, d_ref, mo_ref, vo_ref = refs
            g = o_ref[...]
            rest = range(n_parts)
        for s in rest:
            g = g + p_ref[s].astype(F32)
        delta, mn, vn = _adamw_values(w_ref[...], g, m_ref[...], v_ref[...])
        g_ref[...] = g
        d_ref[...] = delta
        mo_ref[...] = mn
        vo_ref[...] = vn

    spec = _bs((tr, cols), lambda i: (i, 0))
    lead = [] if own is None else [own]
    return pl.pallas_call(
        body, name=name, grid=(rows // tr,),
        in_specs=[spec] * len(lead) + [_bs((n_parts, tr, cols), lambda i: (0, i, 0)), spec, spec, spec],
        out_specs=[spec] * 4, out_shape=[jax.ShapeDtypeStruct((rows, cols), F32)] * 4,
        compiler_params=_params(1),
    )(*lead, parts, w, m, v)


def _rows128(a):
    return a.reshape(-1, 128)


def kernel(x, ln_mix_g, ln_mix_b, w_in, w_pool, pool_scale, conv_w, conv_b, w_rg_a, b_rg_a, w_rg_i, b_rg_i, rg_lambda, w_out, ln_ffn_g, ln_ffn_b, w_mlp_in, w_mlp_out, loss_target, m_ln_mix_g, m_ln_mix_b, m_w_in, m_w_pool, m_pool_scale, m_conv_w, m_conv_b, m_w_rg_a, m_b_rg_a, m_w_rg_i, m_b_rg_i, m_rg_lambda, m_w_out, m_ln_ffn_g, m_ln_ffn_b, m_w_mlp_in, m_w_mlp_out, v_ln_mix_g, v_ln_mix_b, v_w_in, v_w_pool, v_pool_scale, v_conv_w, v_conv_b, v_w_rg_a, v_b_rg_a, v_w_rg_i, v_b_rg_i, v_rg_lambda, v_w_out, v_ln_ffn_g, v_ln_ffn_b, v_w_mlp_in, v_w_mlp_out):
    seq, d_model = x.shape[1], x.shape[2]
    dh = d_model // 2
    lh = dh // N_HEADS
    pg = dh // len(POOL_WINDOWS)
    d_ff = w_mlp_in.shape[2] * N_DEV
    assert lh == 128 and conv_w.shape[3] == lh and w_pool.shape[2] * N_DEV == pg

    xs = x[0]
    tgt = loss_target[0]

    def small_pack(cw, ba, bi, lam):
        return jnp.concatenate([cw.reshape(4, lh), ba.reshape(2, lh), bi.reshape(2, lh), lam.reshape(2, lh),
                                jnp.zeros((SMALL_ROWS - 10, lh), F32)], axis=0)

    pack_mine = small_pack(conv_w, b_rg_a, b_rg_i, rg_lambda)
    win_full, wpool_full, wout_full, pack_full = _all_gather("gather_mixer", [
        (w_in[0].astype(BF16), 1), (w_pool[0].astype(BF16), 1), (w_out[0].astype(BF16), 0),
        (pack_mine[None], 0)])
    mlp_gather = _SplitGather("gather_mlp", [(w_mlp_in[0].astype(BF16), 1), (w_mlp_out[0].astype(BF16), 0)],
                              after=pack_full)
    wa_b = w_rg_a[0].astype(BF16)
    wi_b = w_rg_i[0].astype(BF16)
    vec = lambda i, j, k: (0, 0)
    row_full = lambda i, j, k: (i, 0)

    def after(token):
        return (token, _sp((8, 128), vec))

    def sds(shape, dtype):
        return jax.ShapeDtypeStruct(shape, dtype)

    def plain_epi(acc, i, ex, out):
        out[0][...] = acc

    def bf16_epi(acc, i, ex, out):
        out[0][...] = acc.astype(BF16)

    t = _tiles(seq, d_model, d_ff)

    (p3,) = _matmul(
        "proj", xs, win_full, _sp((t.rows, d_model), lambda i, j, k: (i, 0)), _sp((d_model, dh), lambda i, j, k: (0, j)),
        grid=(seq // t.rows, 3, 1), extras=[after(mlp_gather.token)],
        out_shape=[sds((3, seq, dh), F32)], out_specs=[_sp((None, t.rows, dh), lambda i, j, k: (j, i, 0))],
        epilogue=plain_epi)

    d_pool, y_half = _pool_fwd(p3, wpool_full, pool_scale, seq, d_model)
    y, h0p, h1p = _lru_fwd(p3, y_half, pack_full, conv_b, wa_b, wi_b, seq, d_model)
    relay_token = mlp_gather.relay(after=y)

    def mix_epi(acc, i, ex, out):
        x_ref, g_ref, b_ref = ex[:3]
        z = ALPHA * x_ref[...] + acc
        x1, _, _ = _ln_fwd(z, g_ref[...], b_ref[...])
        out[0][...] = z
        out[1][...] = x1
        out[2][...] = x1.astype(BF16)

    z1, x1, x1b = _matmul(
        "mix_out", y, wout_full, _sp((t.ln_rows, d_model), row_full), _sp((d_model, d_model), vec, single=True),
        grid=(seq // t.ln_rows, 1, 1),
        extras=[(xs, _sp((t.ln_rows, d_model), row_full)), (ln_mix_g, _sp((1, d_model), vec)),
                (ln_mix_b, _sp((1, d_model), vec)), after(relay_token)],
        out_shape=[sds((seq, d_model), F32), sds((seq, d_model), F32), sds((seq, d_model), BF16)],
        out_specs=[_sp((t.ln_rows, d_model), row_full)] * 3, epilogue=mix_epi)
    w1_full, w2_full = mlp_gather.wait(after=x1b)

    def mlp_in_epi(acc, i, ex, out):
        h = jnp.maximum(acc, 0.0)
        out[0][...] = (h * h).astype(BF16)

    (hmid,) = _matmul(
        "mlp_in", x1b, w1_full, _sp((t.rows, d_model), lambda i, j, k: (i, 0)),
        _sp((d_model, t.ff_cols), lambda i, j, k: (0, j)),
        grid=(seq // t.rows, d_ff // t.ff_cols, 1), j_outer=True,
        out_shape=[sds((seq, d_ff), BF16)], out_specs=[_sp((t.rows, t.ff_cols), lambda i, j, k: (i, j))],
        epilogue=mlp_in_epi)

    (ffn,) = _matmul(
        "mlp_out", hmid, w2_full, _sp((t.rows, t.ff_k), lambda i, j, k: (i, k)),
        _sp((t.ff_k, d_model), lambda i, j, k: (k, 0)),
        grid=(seq // t.rows, 1, d_ff // t.ff_k),
        out_shape=[sds((seq, d_model), F32)], out_specs=[_sp((t.rows, d_model), row_full)])
    dz2, dz2b, g_ffn_g, g_ffn_b, loss_part = _ln_loss_bwd(ffn, x1, tgt, ln_ffn_g, ln_ffn_b, t.ln_rows)

    def dpre_epi(acc, i, ex, out):
        out[0][...] = (acc * (2.0 * jnp.sqrt(ex[0][...].astype(F32)))).astype(BF16)

    (dpre,) = _matmul(
        "mlp_dpre", dz2b, w2_full, _sp((t.rows, d_model), lambda i, j, k: (i, 0)),
        _sp((t.ff_cols, d_model), lambda i, j, k: (j, 0)),
        grid=(seq // t.rows, d_ff // t.ff_cols, 1), j_outer=True, tb=True,
        extras=[(hmid, _sp((t.rows, t.ff_cols), lambda i, j, k: (i, j)))],
        out_shape=[sds((seq, d_ff), BF16)], out_specs=[_sp((t.rows, t.ff_cols), lambda i, j, k: (i, j))],
        epilogue=dpre_epi)

    (dx1_mlp,) = _matmul(
        "mlp_dx", dpre, w1_full, _sp((t.rows, t.ff_k), lambda i, j, k: (i, k)),
        _sp((d_model, t.ff_k), lambda i, j, k: (0, k)),
        grid=(seq // t.rows, 1, d_ff // t.ff_k), tb=True,
        out_shape=[sds((seq, d_model), F32)], out_specs=[_sp((t.rows, d_model), row_full)])
    dz1, dz1b, g_mix_g, g_mix_b = _ln_bwd_rows(dx1_mlp, dz2, z1, ln_mix_g, ln_mix_b, t.ln_rows)

    (g_w2,) = _matmul(
        "grad_w_mlp_out", hmid, dz2b, _sp((seq, t.grad_rows), lambda i, j, k: (0, i)),
        _sp((seq, d_model), vec, single=True),
        grid=(d_ff // t.grad_rows, 1, 1), ta=True,
        out_shape=[sds((d_ff, d_model), BF16)], out_specs=[_sp((t.grad_rows, d_model), row_full)],
        epilogue=bf16_epi)
    g_w2 = g_w2.reshape(N_DEV, d_ff // N_DEV, d_model)

    def block_epi(acc, i, ex, out):
        out[0][0] = acc.astype(BF16)

    fs = d_ff // N_DEV
    (g_w1,) = _matmul(
        "grad_w_mlp_in", x1b, dpre, _sp((seq, t.grad_rows), lambda i, j, k: (0, i)),
        _sp((seq, fs), lambda i, j, k: (0, j)),
        grid=(d_model // t.grad_rows, N_DEV, 1), j_outer=True, ta=True,
        out_shape=[sds((N_DEV, d_model, fs), BF16)],
        out_specs=[_sp((1, t.grad_rows, fs), lambda i, j, k: (j, i, 0))], epilogue=block_epi)

    (dy,) = _matmul(
        "mix_dy", dz1b, wout_full, _sp((t.rows, d_model), lambda i, j, k: (i, 0)),
        _sp((dh, d_model), lambda i, j, k: (j, 0)),
        grid=(seq // t.rows, 2, 1), j_outer=True, tb=True,
        out_shape=[sds((seq, d_model), F32)], out_specs=[_sp((t.rows, dh), lambda i, j, k: (i, j))],
        epilogue=plain_epi)
    (g_wout,) = _matmul(
        "grad_w_out", y, dz1b, _sp((seq, t.grad_rows), lambda i, j, k: (0, i)), _sp((seq, d_model), vec, single=True),
        grid=(d_model // t.grad_rows, 1, 1), ta=True,
        out_shape=[sds((d_model, d_model), BF16)], out_specs=[_sp((t.grad_rows, d_model), row_full)],
        epilogue=bf16_epi)
    g_wout = g_wout.reshape(N_DEV, d_model // N_DEV, d_model)

    scatter_a = _SplitReduceScatter("scatter_a", [g_w1, g_w2, g_wout])

    dproj_pool, g_wpool, g_pscale = _pool_bwd(d_pool, dy, wpool_full, pool_scale, scatter_a.token, seq, d_model)
    token_a = scatter_a.combine_and_send(after=dproj_pool)
    dproj, g_pack, g_convb, g_wa, g_wi = _lru_bwd(p3, dy, h0p, h1p, dproj_pool, pack_full, conv_b, wa_b, wi_b,
                                                  token_a, seq, d_model)

    rep_parts = [_rows128(g_wa), _rows128(g_wi), _rows128(g_mix_g), _rows128(g_mix_b), _rows128(g_ffn_g),
                 _rows128(g_ffn_b), _rows128(g_pscale), _rows128(g_convb)]
    rep_rows = [p.shape[0] for p in rep_parts]
    n_rep = sum(rep_rows)
    small = jnp.concatenate(rep_parts + [_rows128(g_pack)], axis=0)
    small_gather = _SplitGather("gather_small_grads", [(small[None], 0)], after=small)

    ws = 3 * dh // N_DEV

    def pair_epi(acc, i, ex, out):
        out[0][0] = acc[:, :ws].astype(BF16)
        out[0][1] = acc[:, ws:].astype(BF16)

    (g_win,) = _matmul(
        "grad_w_in", xs, dproj, _sp((seq, t.grad_rows), lambda i, j, k: (0, i)),
        _sp((seq, 2 * ws), lambda i, j, k: (0, j)),
        grid=(d_model // t.grad_rows, N_DEV // 2, 1), ta=True, extras=[after(small_gather.token)],
        out_shape=[sds((N_DEV, d_model, ws), BF16)],
        out_specs=[_sp((2, t.grad_rows, ws), lambda i, j, k: (j, i, 0))], epilogue=pair_epi)
    scatter_b = _SplitReduceScatter("scatter_b", [g_win, g_wpool.reshape(N_DEV, pg // N_DEV * len(POOL_WINDOWS), pg)])

    def dx_epi(acc, i, ex, out):
        out[0][...] = ALPHA * ex[0][...] + acc

    (dx,) = _matmul(
        "grad_x", dproj, win_full, _sp((t.ln_rows * 2, 3 * dh), lambda i, j, k: (i, 0)),
        _sp((d_model, 3 * dh), vec, single=True),
        grid=(seq // (t.ln_rows * 2), 1, 1), tb=True,
        extras=[(dz1, _sp((t.ln_rows * 2, d_model), row_full)), after(scatter_b.token)],
        out_shape=[sds((seq, d_model), F32)], out_specs=[_sp((t.ln_rows * 2, d_model), row_full)],
        epilogue=dx_epi)
    token_b = scatter_b.combine_and_send(after=dx)

    def adam_big(name, own_landed, w, m, v):
        own, landed = own_landed
        shp = w.shape
        two = lambda a: a.reshape(-1, shp[-1])
        res = _sum_adamw(name, own, landed, two(w), two(m), two(v))
        return [r.reshape(shp) for r in res]

    r_w1, r_w2, r_wout = scatter_a.wait(after=token_b)
    o_w1 = adam_big("adam_w_mlp_in", r_w1, w_mlp_in, m_w_mlp_in, v_w_mlp_in)
    o_w2 = adam_big("adam_w_mlp_out", r_w2, w_mlp_out, m_w_mlp_out, v_w_mlp_out)
    o_wout = adam_big("adam_w_out", r_wout, w_out, m_w_out, v_w_out)
    r_win, r_wpool = scatter_b.wait(after=o_wout[0])
    o_win = adam_big("adam_w_in", r_win, w_in, m_w_in, v_w_in)
    o_wpool = adam_big("adam_w_pool", r_wpool, w_pool, m_w_pool, v_w_pool)

    small_gather.relay(after=o_win[0])
    (small_all,) = small_gather.wait(after=o_wpool[0])

    rep_w = [w_rg_a, w_rg_i, ln_mix_g, ln_mix_b, ln_ffn_g, ln_ffn_b, pool_scale, conv_b]
    rep_m = [m_w_rg_a, m_w_rg_i, m_ln_mix_g, m_ln_mix_b, m_ln_ffn_g, m_ln_ffn_b, m_pool_scale, m_conv_b]
    rep_v = [v_w_rg_a, v_w_rg_i, v_ln_mix_g, v_ln_mix_b, v_ln_ffn_g, v_ln_ffn_b, v_pool_scale, v_conv_b]
    cat = lambda arrs: jnp.concatenate([_rows128(a) for a in arrs], axis=0)
    o_rep = _sum_adamw("adam_replicated", None, small_all[:, :n_rep, :], cat(rep_w), cat(rep_m), cat(rep_v))

    my_idx = _dev_index(_where_am_i())
    head_parts = lax.dynamic_slice_in_dim(small_all, n_rep + my_idx * SMALL_ROWS, SMALL_ROWS, axis=1)
    o_head = _sum_adamw("adam_head", None, head_parts, pack_mine,
                        small_pack(m_conv_w, m_b_rg_a, m_b_rg_i, m_rg_lambda),
                        small_pack(v_conv_w, v_b_rg_a, v_b_rg_i, v_rg_lambda))

    def unpack_rep(packed):
        out, r = [], 0
        for wgt, rows in zip(rep_w, rep_rows):
            out.append(packed[r:r + rows].reshape(wgt.shape))
            r += rows
        return out

    def unpack_head(packed):
        return [packed[0:4].reshape(conv_w.shape), packed[4:6].reshape(b_rg_a.shape),
                packed[6:8].reshape(b_rg_i.shape), packed[8:10].reshape(rg_lambda.shape)]

    loss = lax.psum(loss_part[0, 0], ("x", "y", "c"))

    outs = [loss, dx[None]]
    for kind in range(4):
        ra, ri, mg, mb, fg, fb, ps, cb = unpack_rep(o_rep[kind])
        cw, ba, bi, lam = unpack_head(o_head[kind])
        outs += [mg, mb, o_win[kind], o_wpool[kind], ps, cw, cb, ra, ba, ri, bi, lam, o_wout[kind], fg, fb,
                 o_w1[kind], o_w2[kind]]
    return tuple(outs)
```

```python
import functools

import jax
import jax.numpy as jnp
from jax import lax
from jax.experimental import pallas as pl
from jax.experimental.pallas import tpu as pltpu

F32 = jnp.float32
BF16 = jnp.bfloat16
MESH = pl.DeviceIdType.MESH
ANY = pl.BlockSpec(memory_space=pl.ANY)

N_DEV = 8
POOL_WINDOWS = (2, 4, 8, 16)
N_HEADS = 8
RG_C = 8.0
LN_EPS = 1e-5
ALPHA = 2.0 ** 0.25
ADAM_LR = 0.001
ADAM_B1 = 0.9
ADAM_B2 = 0.999
ADAM_EPS = 1e-08
ADAM_WD = 0.01
ADAM_STEP = 10

VMEM_LIMIT = 56 * 1024 * 1024
WIN_HALO = 16
CONV_HALO = 8
SMALL_ROWS = 16


def _params(n_grid):
    return pltpu.CompilerParams(dimension_semantics=("arbitrary",) * n_grid, vmem_limit_bytes=VMEM_LIMIT)


def _shift(v, j):
    n = v.shape[0]
    s = (-j) % n
    return v if s == 0 else pltpu.roll(v, s, 0)


def _expm1(x):
    poly = x * (1.0 + x * (0.5 + x * (1.0 / 6.0 + x * (1.0 / 24.0 + x * (1.0 / 120.0)))))
    return jnp.where(jnp.abs(x) < 0.1, poly, jnp.exp(x) - 1.0)


def _softplus(z):
    e = jnp.exp(-jnp.abs(z))
    u = 1.0 + e
    log1p = jnp.where(u == 1.0, e, jnp.log(u) * (e / jnp.where(u == 1.0, 1.0, u - 1.0)))
    return jnp.maximum(z, 0.0) + log1p


_GELU_C = 0.7978845608028654
_GELU_K = 0.044715


def _gelu_and_grad(x):
    x2 = x * x
    t = jnp.tanh(_GELU_C * (x + _GELU_K * x * x2))
    g = 0.5 * x * (1.0 + t)
    dg = 0.5 * (1.0 + t) + 0.5 * x * (1.0 - t * t) * (_GELU_C * (1.0 + 3.0 * _GELU_K * x2))
    return g, dg


def _ln_fwd(z, g, b):
    mu = jnp.mean(z, axis=-1, keepdims=True)
    zc = z - mu
    var = jnp.mean(zc * zc, axis=-1, keepdims=True)
    rstd = lax.rsqrt(var + LN_EPS)
    xhat = zc * rstd
    return xhat * g + b, xhat, rstd


def _ln_bwd(dy, xhat, rstd, g):
    dxhat = dy * g
    m1 = jnp.mean(dxhat, axis=-1, keepdims=True)
    m2 = jnp.mean(dxhat * xhat, axis=-1, keepdims=True)
    dz = rstd * (dxhat - m1 - xhat * m2)
    dg = jnp.sum(dy * xhat, axis=0, keepdims=True)
    db = jnp.sum(dy, axis=0, keepdims=True)
    return dz, dg, db


def _acc_rows(ref, first, val):
    @pl.when(first)
    def _():
        ref[...] = val

    @pl.when(jnp.logical_not(first))
    def _():
        ref[...] += val


def _sp(shape, fn, single=False):
    return shape, fn, single


def _matmul(name, a, b, a_spec, b_spec, *, grid, j_outer=False, ta=False, tb=False, extras=(), out_shape, out_specs,
            epilogue=None):
    ni, nj, nk = grid
    n_ex = len(extras)
    dims = (((0 if ta else 1,), (1 if tb else 0,)), ((), ()))

    def mk(spec):
        shape, fn, single = spec
        index = (lambda g0, g1, g2: fn(g1, g0, g2)) if j_outer else fn
        return pl.BlockSpec(shape, index, pipeline_mode=pl.Buffered(1)) if single else pl.BlockSpec(shape, index)

    def body(a_ref, b_ref, *rest):
        ex_refs = rest[:n_ex]
        out_refs = rest[n_ex:]
        i = pl.program_id(1 if j_outer else 0)
        part = lax.dot_general(a_ref[...].astype(BF16), b_ref[...].astype(BF16), dims, preferred_element_type=F32)
        if nk == 1:
            epilogue(part, i, ex_refs, out_refs)
        else:
            @pl.when(pl.program_id(2) == 0)
            def _():
                out_refs[0][...] = jnp.zeros(out_refs[0].shape, F32)

            out_refs[0][...] += part

    return pl.pallas_call(
        body, name=name, grid=(nj, ni, nk) if j_outer else (ni, nj, nk),
        in_specs=[mk(a_spec), mk(b_spec)] + [mk(s) for _, s in extras],
        out_specs=[mk(s) for s in out_specs], out_shape=list(out_shape),
        compiler_params=_params(3),
    )(a, b, *[x for x, _ in extras])


def _bs(shape, fn):
    return pl.BlockSpec(shape, fn)


def _where_am_i():
    x, y, c = lax.axis_index("x"), lax.axis_index("y"), lax.axis_index("c")
    return x, y, c


def _dev_index(p):
    return 4 * p[0] + 2 * p[1] + p[2]


def _slab(ref, axis, idx, size):
    sl = [slice(None)] * len(ref.shape)
    sl[axis] = pl.ds(idx * size, size)
    return ref.at[tuple(sl)]


def _all_gather(name, items):
    n = len(items)
    shapes = []
    for shard, axis in items:
        s = list(shard.shape)
        s[axis] *= N_DEV
        shapes.append(jax.ShapeDtypeStruct(tuple(s), shard.dtype))

    def body(*refs):
        in_refs, out_refs = refs[:n], refs[n:2 * n]
        send_sems, recv_sems, local_sems = refs[2 * n:]
        x, y, c = _where_am_i()
        me, sibling = (x, y, c), (x, y, 1 - c)
        chips = [(1 - x, y), (x, 1 - y), (1 - x, 1 - y)]

        def blk(a, p):
            axis = items[a][1]
            return _slab(out_refs[a], axis, _dev_index(p), items[a][0].shape[axis])

        def copy(a, k, block, to, src=None):
            return pltpu.make_async_remote_copy(
                src_ref=blk(a, block) if src is None else src, dst_ref=blk(a, block),
                send_sem=send_sems.at[a, k], recv_sem=recv_sems.at[a, k], device_id=to, device_id_type=MESH)

        mine = [pltpu.make_async_copy(in_refs[a], blk(a, me), local_sems.at[a]) for a in range(n)]
        for cp in mine:
            cp.start()
        first = []
        for a in range(n):
            first.append(copy(a, 0, me, sibling, src=in_refs[a]))
            first += [copy(a, 1 + j, me, (*chip, c), src=in_refs[a]) for j, chip in enumerate(chips)]
        for cp in first:
            cp.start()
        passed = []
        for a in range(n):
            for j, chip in enumerate(chips):
                copy(a, 1 + j, (*chip, c), me).wait_recv()
                fw = copy(a, 4 + j, (*chip, c), sibling)
                fw.start()
                passed.append(fw)
        for a in range(n):
            copy(a, 0, sibling, me).wait_recv()
            for j, chip in enumerate(chips):
                copy(a, 4 + j, (*chip, 1 - c), me).wait_recv()
        for cp in first + passed:
            cp.wait_send()
        for cp in mine:
            cp.wait()

    outs = pl.pallas_call(
        body, name=name, out_shape=shapes, in_specs=[ANY] * n, out_specs=[ANY] * n,
        scratch_shapes=[pltpu.SemaphoreType.DMA((n, 7)), pltpu.SemaphoreType.DMA((n, 7)),
                        pltpu.SemaphoreType.DMA((n,))],
    )(*[s for s, _ in items])
    return list(outs)


HBM = pl.BlockSpec(memory_space=pltpu.HBM)
SEM = pl.BlockSpec(memory_space=pltpu.SEMAPHORE)
DATAFLOW = pltpu.SideEffectType.DATAFLOW_SIDE_EFFECTING


def _in_hbm(a):
    return pltpu.with_memory_space_constraint(a, pltpu.HBM)


def _token_shape():
    return jax.ShapeDtypeStruct((8, 128), F32)


def _split_start(name, n_sems, bufs, issue):
    nb = len(bufs)

    def body(*refs):
        issue(refs[:nb], refs[nb], refs[nb + 1])
        refs[-1][...] = jnp.zeros((8, 128), F32)

    outs = pl.pallas_call(
        body, name=name,
        out_shape=(pltpu.SemaphoreType.DMA((n_sems,)), pltpu.SemaphoreType.DMA((n_sems,)),
                   *[pltpu.HBM(b.shape, b.dtype) for b in bufs], _token_shape()),
        in_specs=[HBM] * nb, out_specs=(SEM, SEM, *[HBM] * nb, pl.BlockSpec(memory_space=pltpu.VMEM)),
        input_output_aliases={i: 2 + i for i in range(nb)},
        compiler_params=pltpu.CompilerParams(has_side_effects=DATAFLOW),
    )(*[_in_hbm(b) for b in bufs])
    return outs[0], outs[1], list(outs[2:2 + nb]), outs[-1]


def _split_relay(name, n_sems, sems, bufs, after, relay):
    nb = len(bufs)

    def body(*refs):
        relay(refs[:nb], refs[nb], refs[nb + 1], refs[nb + 3], refs[nb + 4])
        refs[-1][...] = jnp.zeros((8, 128), F32)

    outs = pl.pallas_call(
        body, name=name,
        out_shape=(pltpu.SemaphoreType.DMA((n_sems,)), pltpu.SemaphoreType.DMA((n_sems,)),
                   *[pltpu.HBM(b.shape, b.dtype) for b in bufs], _token_shape()),
        in_specs=[HBM] * nb + [SEM, SEM, ANY],
        out_specs=(SEM, SEM, *[HBM] * nb, pl.BlockSpec(memory_space=pltpu.VMEM)),
        input_output_aliases={i: 2 + i for i in range(nb)},
        compiler_params=pltpu.CompilerParams(has_side_effects=DATAFLOW),
    )(*bufs, sems[0], sems[1], after)
    return outs[0], outs[1], list(outs[2:2 + nb]), outs[-1]


def _split_wait(name, sems, bufs, after, finish):
    nb = len(bufs)

    def body(*refs):
        finish(refs[:nb], refs[nb], refs[nb + 1])

    outs = pl.pallas_call(
        body, name=name, out_shape=[pltpu.HBM(b.shape, b.dtype) for b in bufs],
        in_specs=[HBM] * nb + [SEM, SEM, ANY], out_specs=[HBM] * nb,
        input_output_aliases={i: i for i in range(nb)},
        compiler_params=pltpu.CompilerParams(has_side_effects=DATAFLOW),
    )(*bufs, sems[0], sems[1], after)
    return list(outs)


def _place(name, items, after):
    ids = jnp.reshape(_dev_index(_where_am_i()), (1,)).astype(jnp.int32)
    outs = []
    for a, (shard, axis) in enumerate(items):
        rows, cols = shard.shape[-2], shard.shape[-1]
        tr = rows
        while tr * cols * shard.dtype.itemsize > 2 * 1024 * 1024 and tr % 32 == 0:
            tr //= 2
        nt = rows // tr
        full = list(shard.shape)
        full[axis] *= N_DEV
        if shard.ndim == 2 and axis == 0:
            in_spec = _bs((tr, cols), lambda i, ids: (i, 0))
            out_spec = _bs((tr, cols), lambda i, ids, nt=nt: (ids[0] * nt + i, 0))
        elif shard.ndim == 2 and axis == 1:
            in_spec = _bs((tr, cols), lambda i, ids: (i, 0))
            out_spec = _bs((tr, cols), lambda i, ids: (i, ids[0]))
        else:
            assert shard.ndim == 3 and axis == 0 and shard.shape[0] == 1
            in_spec = _bs((None, tr, cols), lambda i, ids: (0, i, 0))
            out_spec = _bs((None, tr, cols), lambda i, ids: (ids[0], i, 0))

        def body(ids_ref, in_ref, after_ref, out_ref):
            del ids_ref, after_ref
            out_ref[...] = in_ref[...]

        outs.append(pl.pallas_call(
            body, name=f"{name}{a}",
            grid_spec=pltpu.PrefetchScalarGridSpec(
                num_scalar_prefetch=1, grid=(nt,), in_specs=[in_spec, ANY], out_specs=out_spec),
            out_shape=jax.ShapeDtypeStruct(tuple(full), shard.dtype), compiler_params=_params(1),
        )(ids, shard, after))
    return outs


class _SplitGather:
    def __init__(self, name, items, after):
        self.name, self.items, self.n = name, items, len(items)
        fulls = _place(name + "_place", items, after)
        n = self.n

        def issue(refs, send, recv):
            me, sibling, chips, c = self._geometry()
            for a in range(n):
                self._copy1(refs, send, recv, a, 0, me, sibling).start()
                for j, chip in enumerate(chips):
                    self._copy1(refs, send, recv, a, 1 + j, me, (*chip, c)).start()

        self.send, self.recv, self.bufs, self.token = _split_start(
            name + "_start", 4 * n, [s for s, _ in items] + fulls, issue)

    @staticmethod
    def _geometry():
        x, y, c = _where_am_i()
        return (x, y, c), (x, y, 1 - c), [(1 - x, y), (x, 1 - y), (1 - x, 1 - y)], c

    def _blk(self, refs, a, p):
        shard, axis = self.items[a]
        return _slab(refs[self.n + a], axis, _dev_index(p), shard.shape[axis])

    def _copy1(self, refs, send, recv, a, k, owner, to):
        return pltpu.make_async_remote_copy(
            src_ref=refs[a], dst_ref=self._blk(refs, a, owner), send_sem=send.at[4 * a + k],
            recv_sem=recv.at[4 * a + k], device_id=to, device_id_type=MESH)

    def _copy2(self, refs, send, recv, a, j, owner, to):
        return pltpu.make_async_remote_copy(
            src_ref=self._blk(refs, a, owner), dst_ref=self._blk(refs, a, owner), send_sem=send.at[3 * a + j],
            recv_sem=recv.at[3 * a + j], device_id=to, device_id_type=MESH)

    def relay(self, after):
        n = self.n

        def relay(refs, send_in, recv_in, send_out, recv_out):
            me, sibling, chips, c = self._geometry()
            for a in range(n):
                for j, chip in enumerate(chips):
                    self._copy1(refs, send_in, recv_in, a, 1 + j, (*chip, c), me).wait_recv()
                    self._copy2(refs, send_out, recv_out, a, j, (*chip, c), sibling).start()
            for a in range(n):
                self._copy1(refs, send_in, recv_in, a, 0, sibling, me).wait_recv()
                for k in range(4):
                    self._copy1(refs, send_in, recv_in, a, k, me, sibling).wait_send()

        self.send, self.recv, self.bufs, self.token = _split_relay(
            self.name + "_relay", 3 * n, (self.send, self.recv), self.bufs, after, relay)
        return self.token

    def wait(self, after):
        n = self.n

        def finish(refs, send, recv):
            me, sibling, chips, c = self._geometry()
            for a in range(n):
                for j, chip in enumerate(chips):
                    self._copy2(refs, send, recv, a, j, (*chip, 1 - c), me).wait_recv()
                    self._copy2(refs, send, recv, a, j, (*chip, c), sibling).wait_send()

        bufs = _split_wait(self.name + "_wait", (self.send, self.recv), self.bufs, after, finish)
        return bufs[n:]


class _SplitReduceScatter:
    def __init__(self, name, grads):
        self.name, self.n = name, len(grads)
        n = self.n
        g4 = [g.reshape(4, 2, *g.shape[1:]) for g in grads]
        land = [lax.empty((4, 1, *g.shape[1:]), g.dtype) for g in grads]

        def issue(refs, send, recv):
            for a in range(n):
                self._swap(refs, send, recv, a).start()

        self.send, self.recv, self.bufs, self.token = _split_start(name + "_d2d_start", n, g4 + land, issue)

    def _swap(self, refs, send, recv, a):
        x, y, c = _where_am_i()
        return pltpu.make_async_remote_copy(
            src_ref=refs[a].at[:, pl.ds(1 - c, 1)], dst_ref=refs[self.n + a], send_sem=send.at[a], recv_sem=recv.at[a],
            device_id=(x, y, 1 - c), device_id_type=MESH)

    def _hop(self, refs, send, recv, a, m):
        x, y, c = _where_am_i()
        px = (1 - x) if m & 2 else x
        py = (1 - y) if m & 1 else y
        return pltpu.make_async_remote_copy(
            src_ref=refs[a].at[2 * px + py], dst_ref=refs[self.n + a].at[m - 1], send_sem=send.at[3 * a + m - 1],
            recv_sem=recv.at[3 * a + m - 1], device_id=(px, py, c), device_id_type=MESH)

    def combine_and_send(self, after):
        n = self.n

        def finish(refs, send, recv):
            for a in range(n):
                self._swap(refs, send, recv, a).wait()

        bufs = _split_wait(self.name + "_d2d_wait", (self.send, self.recv), self.bufs, after, finish)
        x, y, c = _where_am_i()
        ids = jnp.stack([c, 2 * x + y]).astype(jnp.int32)
        self.own, sums = [], []
        for a in range(n):
            own, hb = _pair_sum(f"{self.name}_sum{a}", bufs[a], bufs[n + a], ids)
            self.own.append(own)
            sums.append(hb)
        land = [lax.empty((3, *h.shape[1:]), h.dtype) for h in sums]

        def issue(refs, send, recv):
            for a in range(n):
                for m in (1, 2, 3):
                    self._hop(refs, send, recv, a, m).start()

        self.send, self.recv, self.bufs, self.token = _split_start(self.name + "_ici_start", 3 * n, sums + land, issue)
        return self.token

    def wait(self, after):
        n = self.n

        def finish(refs, send, recv):
            for a in range(n):
                for m in (1, 2, 3):
                    self._hop(refs, send, recv, a, m).wait()

        bufs = _split_wait(self.name + "_ici_wait", (self.send, self.recv), self.bufs, after, finish)
        return list(zip(self.own, bufs[n:]))


def _pair_sum(name, g4, land, ids):
    rows, cols = g4.shape[2], g4.shape[3]
    tr = rows
    while tr * cols * 2 > 1024 * 1024 and tr % 32 == 0:
        tr //= 2

    def body(ids_ref, g_ref, l_ref, own_ref, sum_ref):
        h = g_ref[...].astype(F32) + l_ref[...].astype(F32)
        sum_ref[...] = h.astype(sum_ref.dtype)

        @pl.when(pl.program_id(1) == ids_ref[1])
        def _():
            own_ref[...] = h

    return pl.pallas_call(
        body, name=name,
        grid_spec=pltpu.PrefetchScalarGridSpec(
            num_scalar_prefetch=1, grid=(rows // tr, 4),
            in_specs=[_bs((None, None, tr, cols), lambda i, q, ids: (q, ids[0], i, 0)),
                      _bs((None, None, tr, cols), lambda i, q, ids: (q, 0, i, 0))],
            out_specs=[_bs((tr, cols), lambda i, q, ids: (i, 0)), _bs((None, tr, cols), lambda i, q, ids: (q, i, 0))]),
        out_shape=[jax.ShapeDtypeStruct((rows, cols), F32), jax.ShapeDtypeStruct((4, rows, cols), g4.dtype)],
        compiler_params=_params(2),
    )(ids, g4, land)


def _win_sum(ext, w, off):
    s = ext + _shift(ext, -1)
    if w >= 4:
        s = _shift(s, -1) + _shift(s, 1)
    if w >= 8:
        s = _shift(s, -2) + _shift(s, 2)
    if w >= 16:
        s = _shift(s, -4) + _shift(s, 4)
    return _shift(s, off) if off else s


def _inv_count(r0, t, w, seq):
    pos = r0 + lax.broadcasted_iota(jnp.int32, (t, 1), 0)
    cnt = jnp.minimum(pos + w // 2, seq) - jnp.maximum(pos - w // 2, 0)
    return 1.0 / cnt.astype(F32)


def _pool_fwd(p3, w_pool, pool_scale, seq, d_model):
    dp = d_model // 2
    pg = dp // len(POOL_WINDOWS)
    t = min(128, seq)
    n_chunks = seq // t
    h = WIN_HALO

    def body(u_ref, w_ref, sc_ref, d_ref, y_ref, pad_ref):
        g = pl.program_id(0)
        zeros = jnp.zeros((h, pg), F32)
        pad_ref[0:h, :] = zeros
        pad_ref[h + seq:h + seq + h, :] = zeros

        def fill(ci, _):
            r0 = pl.multiple_of(ci * t, t)
            pad_ref[pl.ds(h + r0, t), :] = u_ref[pl.ds(r0, t), :]
            return 0

        lax.fori_loop(0, n_chunks, fill, 0)
        wmat = w_ref[...]
        scale = sc_ref[...]
        for gi, w in enumerate(POOL_WINDOWS):
            @pl.when(g == gi)
            def _(w=w):
                def chunk(ci, _):
                    r0 = pl.multiple_of(ci * t, t)
                    ext = pad_ref[pl.ds(r0, t + 2 * h), :]
                    mean = _win_sum(ext, w, 0)[h:h + t, :] * _inv_count(r0, t, w, seq)
                    d = (mean - ext[h:h + t, :]).astype(BF16)
                    d_ref[pl.ds(r0, t), :] = d
                    q = jnp.dot(d, wmat, preferred_element_type=F32)
                    y_ref[pl.ds(r0, t), :] = (q * scale).astype(BF16)
                    return 0

                lax.fori_loop(0, n_chunks, chunk, 0)

    return pl.pallas_call(
        body, name="pool_fwd", grid=(len(POOL_WINDOWS),),
        in_specs=[_bs((None, seq, pg), lambda g: (0, 0, g)), _bs((None, pg, pg), lambda g: (g, 0, 0)),
                  _bs((1, pg), lambda g: (0, g))],
        out_specs=[_bs((seq, pg), lambda g: (0, g)), _bs((seq, pg), lambda g: (0, g))],
        out_shape=[jax.ShapeDtypeStruct((seq, dp), BF16), jax.ShapeDtypeStruct((seq, d_model), BF16)],
        scratch_shapes=[pltpu.VMEM((seq + 2 * h, pg), F32)],
        compiler_params=_params(1),
    )(p3, w_pool, pool_scale)


def _pool_bwd(d, dy, w_pool, pool_scale, token, seq, d_model):
    dp = d_model // 2
    pg = dp // len(POOL_WINDOWS)
    t = min(128, seq)
    n_chunks = seq // t
    h = WIN_HALO
    tn_dims = (((0,), (0,)), ((), ()))
    nt_dims = (((1,), (1,)), ((), ()))

    def body(d_ref, dy_ref, w_ref, sc_ref, tok_ref, du_ref, dwb_ref, dsc_ref, pad_ref, dd_ref, dw_ref):
        del tok_ref
        g = pl.program_id(0)
        zeros = jnp.zeros((h, pg), F32)
        pad_ref[0:h, :] = zeros
        pad_ref[h + seq:h + seq + h, :] = zeros
        wmat = w_ref[...]
        scale = sc_ref[...]
        for gi, w in enumerate(POOL_WINDOWS):
            @pl.when(g == gi)
            def _(w=w):
                dw_ref[...] = jnp.zeros((pg, pg), F32)

                def first(ci, dsc):
                    r0 = pl.multiple_of(ci * t, t)
                    dv = d_ref[pl.ds(r0, t), :]
                    dyv = dy_ref[pl.ds(r0, t), :]
                    q = jnp.dot(dv, wmat, preferred_element_type=F32)
                    dsc = dsc + jnp.sum(dyv * q, axis=0, keepdims=True)
                    dq = (dyv * scale).astype(BF16)
                    dw_ref[...] += lax.dot_general(dv, dq, tn_dims, preferred_element_type=F32)
                    dd = lax.dot_general(dq, wmat, nt_dims, preferred_element_type=F32)
                    dd_ref[pl.ds(r0, t), :] = dd
                    pad_ref[pl.ds(h + r0, t), :] = dd * _inv_count(r0, t, w, seq)
                    return dsc

                dsc_ref[...] = lax.fori_loop(0, n_chunks, first, jnp.zeros((1, pg), F32))
                dwb_ref[...] = dw_ref[...].reshape(N_DEV, pg // N_DEV, pg).astype(BF16)

                def second(ci, _):
                    r0 = pl.multiple_of(ci * t, t)
                    ext = pad_ref[pl.ds(r0, t + 2 * h), :]
                    back = _win_sum(ext, w, 1)[h:h + t, :]
                    du_ref[pl.ds(r0, t), :] = (back - dd_ref[pl.ds(r0, t), :]).astype(BF16)
                    return 0

                lax.fori_loop(0, n_chunks, second, 0)

    return pl.pallas_call(
        body, name="pool_bwd", grid=(len(POOL_WINDOWS),),
        in_specs=[_bs((seq, pg), lambda g: (0, g)), _bs((seq, pg), lambda g: (0, g)),
                  _bs((None, pg, pg), lambda g: (g, 0, 0)), _bs((1, pg), lambda g: (0, g)),
                  _bs((8, 128), lambda g: (0, 0))],
        out_specs=[_bs((seq, pg), lambda g: (0, g)), _bs((N_DEV, None, pg // N_DEV, pg), lambda g: (0, g, 0, 0)),
                   _bs((1, pg), lambda g: (0, g))],
        out_shape=[jax.ShapeDtypeStruct((seq, 3 * dp), BF16),
                   jax.ShapeDtypeStruct((N_DEV, len(POOL_WINDOWS), pg // N_DEV, pg), BF16),
                   jax.ShapeDtypeStruct((1, dp), F32)],
        scratch_shapes=[pltpu.VMEM((seq + 2 * h, pg), F32), pltpu.VMEM((seq, pg), F32), pltpu.VMEM((pg, pg), F32)],
        compiler_params=_params(1),
    )(d, dy, w_pool, pool_scale, token)


def _segment_scan(seg_len, loads, hs_refs, ps_refs, stores):
    lanes = hs_refs[0].shape[-1]
    row = lax.broadcasted_iota(jnp.int32, (8, lanes), 0)

    def pos_of(n, s):
        return s if n == 0 else seg_len - 1 - s

    def step1(s, carry):
        out = []
        for n in range(2):
            hh, pp = carry[n]
            pos = pos_of(n, s)
            coef, inp = loads[n](pos)
            hh = coef * hh + inp
            pp = coef * pp
            hs_refs[n][pl.ds(pl.multiple_of(pos * 8, 8), 8), :] = hh
            ps_refs[n][pl.ds(pl.multiple_of(pos * 8, 8), 8), :] = pp
            out.append((hh, pp))
        return tuple(out)

    init = tuple((jnp.zeros((8, lanes), F32), jnp.ones((8, lanes), F32)) for _ in range(2))
    ends = lax.fori_loop(0, seg_len, step1, init, unroll=4)

    entry = []
    for n in range(2):
        bb, aa = ends[n]
        for sh in (1, 2, 4):
            if n == 0:
                ok = row >= sh
                ap = jnp.where(ok, pltpu.roll(aa, sh, 0), 1.0)
                bp = jnp.where(ok, pltpu.roll(bb, sh, 0), 0.0)
            else:
                ok = row < 8 - sh
                ap = jnp.where(ok, pltpu.roll(aa, 8 - sh, 0), 1.0)
                bp = jnp.where(ok, pltpu.roll(bb, 8 - sh, 0), 0.0)
            bb = aa * bp + bb
            aa = aa * ap
        if n == 0:
            entry.append(jnp.where(row >= 1, pltpu.roll(bb, 1, 0), 0.0))
        else:
            entry.append(jnp.where(row < 7, pltpu.roll(bb, 7, 0), 0.0))

    def step2(s, _):
        for n in range(2):
            at = pl.ds(pl.multiple_of(s * 8, 8), 8)
            stores[n](s, hs_refs[n][at, :] + ps_refs[n][at, :] * entry[n])
        return 0

    lax.fori_loop(0, seg_len, step2, 0, unroll=4)


def _gates(xc, n, wa_ref, wi_ref, pk_ref, sp):
    xcb = xc.astype(BF16)
    r = jax.nn.sigmoid(jnp.dot(xcb, wa_ref[n], preferred_element_type=F32) + pk_ref[pl.ds(4 + n, 1), :])
    i = jax.nn.sigmoid(jnp.dot(xcb, wi_ref[n], preferred_element_type=F32) + pk_ref[pl.ds(6 + n, 1), :])
    log_a = (-RG_C * r) * sp[n]
    a = jnp.exp(log_a)
    m = jnp.sqrt(-_expm1(2.0 * log_a))
    return xcb, r, i, a, m


def _conv_chunk(upad_ref, pk_ref, cb, r0, t):
    ext = upad_ref[pl.ds(r0, t + 2 * CONV_HALO), :]
    acc = pk_ref[pl.ds(1, 1), :] * ext
    for k in (0, 2, 3):
        acc = acc + pk_ref[pl.ds(k, 1), :] * _shift(ext, k - 1)
    return acc[CONV_HALO:CONV_HALO + t, :] + cb, ext


def _lru_fwd(p3, y_in, pack, conv_b, wa, wi, token, seq, d_model):
    dl = d_model // 2
    lh = dl // N_HEADS
    t = min(128, seq)
    n_chunks = seq // t
    seg = seq // 8
    hal = CONV_HALO
    first_rec_block = (d_model - dl) // lh

    def body(ur_ref, ug_ref, pk_ref, cb_ref, wa_ref, wi_ref, yin_ref, tok_ref, y_ref, h0_ref, h1_ref,
             upad, a_scr, b_scr, hs0, hs1, ps0, ps1):
        del yin_ref, tok_ref
        zeros = jnp.zeros((hal, lh), F32)
        upad[0:hal, :] = zeros
        upad[hal + seq:hal + seq + hal, :] = zeros
        for ref in (h0_ref, h1_ref):
            ref[0:hal, :] = zeros
            ref[hal + seq:hal + seq + hal, :] = zeros

        def fill(ci, _):
            r0 = pl.multiple_of(ci * t, t)
            upad[pl.ds(hal + r0, t), :] = ur_ref[pl.ds(r0, t), :]
            return 0

        lax.fori_loop(0, n_chunks, fill, 0)
        cb = cb_ref[...]
        sp = [_softplus(-pk_ref[pl.ds(8 + n, 1), :]) for n in range(2)]

        def chunk(ci, _):
            r0 = pl.multiple_of(ci * t, t)
            xc, _ext = _conv_chunk(upad, pk_ref, cb, r0, t)
            for n in range(2):
                _, _, i, a, m = _gates(xc, n, wa_ref, wi_ref, pk_ref, sp)
                a_scr[n, pl.ds(r0, t), :] = a
                b_scr[n, pl.ds(r0, t), :] = (m * i) * xc
            return 0

        lax.fori_loop(0, n_chunks, chunk, 0)

        def load(n):
            return lambda pos: (a_scr[n, pl.ds(pos, 8, stride=seg), :], b_scr[n, pl.ds(pos, 8, stride=seg), :])

        def store(n, ref):
            def put(s, v):
                ref[pl.ds(hal + s, 8, stride=seg), :] = v
            return put

        _segment_scan(seg, [load(0), load(1)], [hs0, hs1], [ps0, ps1], [store(0, h0_ref), store(1, h1_ref)])

        def out(ci, _):
            r0 = pl.multiple_of(ci * t, t)
            hsum = h0_ref[pl.ds(hal + r0, t), :] + h1_ref[pl.ds(hal + r0, t), :]
            gl, _dg = _gelu_and_grad(ug_ref[pl.ds(r0, t), :])
            y_ref[pl.ds(r0, t), :] = (hsum * gl).astype(BF16)
            return 0

        lax.fori_loop(0, n_chunks, out, 0)

    return pl.pallas_call(
        body, name="lru_fwd", grid=(N_HEADS,),
        in_specs=[_bs((None, seq, lh), lambda h: (1, 0, h)), _bs((None, seq, lh), lambda h: (2, 0, h)),
                  _bs((None, SMALL_ROWS, lh), lambda h: (h, 0, 0)), _bs((1, lh), lambda h: (0, h)),
                  _bs((2, None, lh, lh), lambda h: (0, h, 0, 0)), _bs((2, None, lh, lh), lambda h: (0, h, 0, 0)),
                  ANY, _bs((8, 128), lambda h: (0, 0))],
        out_specs=[_bs((seq, lh), lambda h: (0, first_rec_block + h)),
                   _bs((seq + 2 * hal, lh), lambda h: (0, h)), _bs((seq + 2 * hal, lh), lambda h: (0, h))],
        out_shape=[jax.ShapeDtypeStruct((seq, d_model), BF16), jax.ShapeDtypeStruct((seq + 2 * hal, dl), F32),
                   jax.ShapeDtypeStruct((seq + 2 * hal, dl), F32)],
        scratch_shapes=[pltpu.VMEM((seq + 2 * hal, lh), F32), pltpu.VMEM((2, seq, lh), F32),
                        pltpu.VMEM((2, seq, lh), F32)] + [pltpu.VMEM((seq, lh), F32)] * 4,
        input_output_aliases={6: 0},
        compiler_params=_params(1),
    )(p3, p3, pack, conv_b, wa, wi, y_in, token)


def _lru_bwd(p3, dy, h0p, h1p, dproj_in, pack, conv_b, wa, wi, token, seq, d_model):
    dl = d_model // 2
    lh = dl // N_HEADS
    t = min(128, seq)
    n_chunks = seq // t
    seg = seq // 8
    hal = CONV_HALO
    first_rec_block = (d_model - dl) // lh
    tn_dims = (((0,), (0,)), ((), ()))
    nt_dims = (((1,), (1,)), ((), ()))

    def body(ur_ref, ug_ref, dy_ref, h0_ref, h1_ref, pk_ref, cb_ref, wa_ref, wi_ref, tok_ref, din_ref,
             dproj_ref, dpk_ref, dcb_ref, dwa_ref, dwi_ref,
             upad, a_scr, dh_scr, g_scr, dxc_pad, hs0, hs1, ps0, ps1, dpr_ref, out_sems):
        del din_ref, tok_ref
        zeros = jnp.zeros((hal, lh), F32)
        for ref in (upad, dxc_pad):
            ref[0:hal, :] = zeros
            ref[hal + seq:hal + seq + hal, :] = zeros
        for n in range(2):
            a_scr[n, 0:hal, :] = zeros
            a_scr[n, hal + seq:hal + seq + hal, :] = zeros

        def fill(ci, _):
            r0 = pl.multiple_of(ci * t, t)
            upad[pl.ds(hal + r0, t), :] = ur_ref[pl.ds(r0, t), :]
            return 0

        lax.fori_loop(0, n_chunks, fill, 0)
        cb = cb_ref[...]
        lam = [pk_ref[pl.ds(8 + n, 1), :] for n in range(2)]
        sp = [_softplus(-lam[n]) for n in range(2)]

        def chunk1(ci, _):
            r0 = pl.multiple_of(ci * t, t)
            xc, _ext = _conv_chunk(upad, pk_ref, cb, r0, t)
            for n in range(2):
                _, _, _, a, _ = _gates(xc, n, wa_ref, wi_ref, pk_ref, sp)
                a_scr[n, pl.ds(hal + r0, t), :] = a
            hsum = h0_ref[pl.ds(hal + r0, t), :] + h1_ref[pl.ds(hal + r0, t), :]
            gl, dgl = _gelu_and_grad(ug_ref[pl.ds(r0, t), :])
            dyv = dy_ref[pl.ds(r0, t), :]
            dh_scr[pl.ds(r0, t), :] = dyv * gl
            dpr_ref[1, pl.ds(r0, t), :] = ((dyv * hsum) * dgl).astype(BF16)
            return 0

        lax.fori_loop(0, n_chunks, chunk1, 0)

        def load(n):
            def get(pos):
                coef = a_scr[n, pl.ds(hal + pos + (1 if n == 0 else -1), 8, stride=seg), :]
                return coef, dh_scr[pl.ds(pos, 8, stride=seg), :]
            return get

        def store(n):
            def put(s, v):
                g_scr[n, pl.ds(s, 8, stride=seg), :] = v
            return put

        _segment_scan(seg, [load(1), load(0)], [hs0, hs1], [ps0, ps1], [store(1), store(0)])

        dwa_ref[...] = jnp.zeros((2, lh, lh), F32)
        dwi_ref[...] = jnp.zeros((2, lh, lh), F32)

        def chunk3(ci, carry):
            dba, dbi, dlam, dcb = carry
            r0 = pl.multiple_of(ci * t, t)
            xc, _ext = _conv_chunk(upad, pk_ref, cb, r0, t)
            dxc = jnp.zeros((t, lh), F32)
            dba, dbi, dlam = list(dba), list(dbi), list(dlam)
            for n in range(2):
                xcb, r, i, a, m = _gates(xc, n, wa_ref, wi_ref, pk_ref, sp)
                hext = (h0_ref if n == 0 else h1_ref)[pl.ds(r0, t + 2 * hal), :]
                hprev = _shift(hext, -1 if n == 0 else 1)[hal:hal + t, :]
                gb = g_scr[n, pl.ds(r0, t), :]
                da = gb * hprev
                dm = gb * i * xc
                di = gb * m * xc
                dxc = dxc + gb * (m * i)
                dlog_a = da * a - dm * (a * a) / m
                dr = dlog_a * (-RG_C * sp[n])
                dlam[n] = dlam[n] + jnp.sum(dlog_a * r, axis=0, keepdims=True)
                dpr = dr * r * (1.0 - r)
                dpi = di * i * (1.0 - i)
                dba[n] = dba[n] + jnp.sum(dpr, axis=0, keepdims=True)
                dbi[n] = dbi[n] + jnp.sum(dpi, axis=0, keepdims=True)
                dprb, dpib = dpr.astype(BF16), dpi.astype(BF16)
                dwa_ref[n] += lax.dot_general(xcb, dprb, tn_dims, preferred_element_type=F32)
                dwi_ref[n] += lax.dot_general(xcb, dpib, tn_dims, preferred_element_type=F32)
                dxc = dxc + lax.dot_general(dprb, wa_ref[n], nt_dims, preferred_element_type=F32)
                dxc = dxc + lax.dot_general(dpib, wi_ref[n], nt_dims, preferred_element_type=F32)
            dxc_pad[pl.ds(hal + r0, t), :] = dxc
            dcb = dcb + jnp.sum(dxc, axis=0, keepdims=True)
            return tuple(dba), tuple(dbi), tuple(dlam), dcb

        zr = jnp.zeros((1, lh), F32)
        dba, dbi, dlam, dcb = lax.fori_loop(0, n_chunks, chunk3, ((zr, zr), (zr, zr), (zr, zr), zr))
        dcb_ref[...] = dcb
        for n in range(2):
            dpk_ref[pl.ds(4 + n, 1), :] = dba[n]
            dpk_ref[pl.ds(6 + n, 1), :] = dbi[n]
            dpk_ref[pl.ds(8 + n, 1), :] = dlam[n] * (RG_C * jax.nn.sigmoid(-lam[n]))
        dpk_ref[pl.ds(10, SMALL_ROWS - 10), :] = jnp.zeros((SMALL_ROWS - 10, lh), F32)

        def chunk4(ci, dtap):
            r0 = pl.multiple_of(ci * t, t)
            gext = dxc_pad[pl.ds(r0, t + 2 * hal), :]
            uext = upad[pl.ds(r0, t + 2 * hal), :]
            gmid = gext[hal:hal + t, :]
            du = pk_ref[pl.ds(1, 1), :] * gext
            for k in (0, 2, 3):
                du = du + pk_ref[pl.ds(k, 1), :] * _shift(gext, 1 - k)
            dpr_ref[0, pl.ds(r0, t), :] = du[hal:hal + t, :].astype(BF16)
            out = []
            for k in range(4):
                usl = _shift(uext, k - 1)[hal:hal + t, :]
                out.append(dtap[k] + jnp.sum(gmid * usl, axis=0, keepdims=True))
            return tuple(out)

        dtap = lax.fori_loop(0, n_chunks, chunk4, (zr, zr, zr, zr))
        for k in range(4):
            dpk_ref[pl.ds(k, 1), :] = dtap[k]

        head = pl.program_id(0)
        outs = [pltpu.make_async_copy(
            dpr_ref.at[b], dproj_ref.at[:, pl.ds(pl.multiple_of((1 + b) * dl + head * lh, lh), lh)], out_sems.at[b])
            for b in range(2)]
        for cp in outs:
            cp.start()
        for cp in outs:
            cp.wait()

    return pl.pallas_call(
        body, name="lru_bwd", grid=(N_HEADS,),
        in_specs=[_bs((None, seq, lh), lambda h: (1, 0, h)), _bs((None, seq, lh), lambda h: (2, 0, h)),
                  _bs((seq, lh), lambda h: (0, first_rec_block + h)),
                  _bs((seq + 2 * hal, lh), lambda h: (0, h)), _bs((seq + 2 * hal, lh), lambda h: (0, h)),
                  _bs((None, SMALL_ROWS, lh), lambda h: (h, 0, 0)), _bs((1, lh), lambda h: (0, h)),
                  _bs((2, None, lh, lh), lambda h: (0, h, 0, 0)), _bs((2, None, lh, lh), lambda h: (0, h, 0, 0)),
                  _bs((8, 128), lambda h: (0, 0)), ANY],
        out_specs=[ANY, _bs((None, SMALL_ROWS, lh), lambda h: (h, 0, 0)),
                   _bs((1, lh), lambda h: (0, h)),
                   _bs((2, None, lh, lh), lambda h: (0, h, 0, 0)), _bs((2, None, lh, lh), lambda h: (0, h, 0, 0))],
        out_shape=[jax.ShapeDtypeStruct((seq, 3 * dl), BF16), jax.ShapeDtypeStruct((N_HEADS, SMALL_ROWS, lh), F32),
                   jax.ShapeDtypeStruct((1, dl), F32),
                   jax.ShapeDtypeStruct((2, N_HEADS, lh, lh), F32), jax.ShapeDtypeStruct((2, N_HEADS, lh, lh), F32)],
        scratch_shapes=[pltpu.VMEM((seq + 2 * hal, lh), F32), pltpu.VMEM((2, seq + 2 * hal, lh), F32),
                        pltpu.VMEM((seq, lh), F32), pltpu.VMEM((2, seq, lh), F32),
                        pltpu.VMEM((seq + 2 * hal, lh), F32)] + [pltpu.VMEM((seq, lh), F32)] * 4
                       + [pltpu.VMEM((2, seq, lh), BF16), pltpu.SemaphoreType.DMA((2,))],
        input_output_aliases={10: 0},
        compiler_params=_params(1),
    )(p3, p3, dy, h0p, h1p, pack, conv_b, wa, wi, token, dproj_in)


class _tiles:
    def __init__(self, seq, d_model, d_ff):
        self.rows = min(1024, seq)
        self.ln_rows = min(256, seq)
        self.ff_cols = min(1024, d_ff)
        self.ff_k = min(2048, d_ff)
        self.grad_rows = 512


def _ln_loss_bwd(ffn, x1, tgt, g, b, tr):
    seq, d = ffn.shape

    def body(f_ref, x_ref, t_ref, g_ref, b_ref, dz_ref, dzb_ref, dg_ref, db_ref, loss_ref):
        i = pl.program_id(0)
        gv = g_ref[...]
        z = ALPHA * x_ref[...] + f_ref[...]
        y, xhat, rstd = _ln_fwd(z, gv, b_ref[...])
        err = y - t_ref[...]
        part = 0.5 * jnp.sum(jnp.mean(err * err, axis=-1, keepdims=True), axis=0, keepdims=True)
        dz, dg, db = _ln_bwd(err * (1.0 / d), xhat, rstd, gv)
        dz_ref[...] = dz
        dzb_ref[...] = dz.astype(BF16)
        _acc_rows(dg_ref, i == 0, dg)
        _acc_rows(db_ref, i == 0, db)
        _acc_rows(loss_ref, i == 0, jnp.broadcast_to(part, (8, 128)))

    row = _bs((tr, d), lambda i: (i, 0))
    vec = _bs((1, d), lambda i: (0, 0))
    return pl.pallas_call(
        body, name="ln_ffn_loss", grid=(seq // tr,), in_specs=[row, row, row, vec, vec],
        out_specs=[row, row, vec, vec, _bs((8, 128), lambda i: (0, 0))],
        out_shape=[jax.ShapeDtypeStruct((seq, d), F32), jax.ShapeDtypeStruct((seq, d), BF16),
                   jax.ShapeDtypeStruct((1, d), F32), jax.ShapeDtypeStruct((1, d), F32),
                   jax.ShapeDtypeStruct((8, 128), F32)],
        compiler_params=_params(1),
    )(ffn, x1, tgt, g, b)


def _ln_bwd_rows(dx_branch, dres, z, g, b, tr):
    seq, d = z.shape

    def body(a_ref, r_ref, z_ref, g_ref, b_ref, dz_ref, dzb_ref, dg_ref, db_ref):
        i = pl.program_id(0)
        gv = g_ref[...]
        _, xhat, rstd = _ln_fwd(z_ref[...], gv, b_ref[...])
        dz, dg, db = _ln_bwd(ALPHA * r_ref[...] + a_ref[...], xhat, rstd, gv)
        dz_ref[...] = dz
        dzb_ref[...] = dz.astype(BF16)
        _acc_rows(dg_ref, i == 0, dg)
        _acc_rows(db_ref, i == 0, db)

    row = _bs((tr, d), lambda i: (i, 0))
    vec = _bs((1, d), lambda i: (0, 0))
    return pl.pallas_call(
        body, name="ln_mix_bwd", grid=(seq // tr,), in_specs=[row, row, row, vec, vec],
        out_specs=[row, row, vec, vec],
        out_shape=[jax.ShapeDtypeStruct((seq, d), F32), jax.ShapeDtypeStruct((seq, d), BF16),
                   jax.ShapeDtypeStruct((1, d), F32), jax.ShapeDtypeStruct((1, d), F32)],
        compiler_params=_params(1),
    )(dx_branch, dres, z, g, b)


def _adamw_values(w, g, m, v):
    m = ADAM_B1 * m + (1.0 - ADAM_B1) * g
    v = ADAM_B2 * v + (1.0 - ADAM_B2) * (g * g)
    m_hat = m / (1.0 - ADAM_B1 ** ADAM_STEP)
    v_hat = v / (1.0 - ADAM_B2 ** ADAM_STEP)
    delta = -ADAM_LR * (m_hat / (jnp.sqrt(v_hat) + ADAM_EPS) + ADAM_WD * w)
    return delta, m, v


def _sum_adamw(name, own, parts, w, m, v):
    rows, cols = w.shape
    n_parts = parts.shape[0]
    tr = rows
    min_rows = 8 if parts.dtype == F32 else 16
    while tr * cols * 4 > 1024 * 1024 and tr % (2 * min_rows) == 0:
        tr //= 2

    def body(*refs):
        if own is None:
            p_ref, w_ref, m_ref, v_ref, g_ref, d_ref, mo_ref, vo_ref = refs
            g = p_ref[0].astype(F32)
            rest = range(1, n_parts)
        else:
            o_ref, p_ref, w_ref, m_ref, v_ref, g_ref, d_ref, mo_ref, vo_ref = refs
            g = o_ref[...]
            rest = range(n_parts)
        for s in rest:
            g = g + p_ref[s].astype(F32)
        delta, mn, vn = _adamw_values(w_ref[...], g, m_ref[...], v_ref[...])
        g_ref[...] = g
        d_ref[...] = delta
        mo_ref[...] = mn
        vo_ref[...] = vn

    spec = _bs((tr, cols), lambda i: (i, 0))
    lead = [] if own is None else [own]
    return pl.pallas_call(
        body, name=name, grid=(rows // tr,),
        in_specs=[spec] * len(lead) + [_bs((n_parts, tr, cols), lambda i: (0, i, 0)), spec, spec, spec],
        out_specs=[spec] * 4, out_shape=[jax.ShapeDtypeStruct((rows, cols), F32)] * 4,
        compiler_params=_params(1),
    )(*lead, parts, w, m, v)


def _rows128(a):
    return a.reshape(-1, 128)


def kernel(x, ln_mix_g, ln_mix_b, w_in, w_pool, pool_scale, conv_w, conv_b, w_rg_a, b_rg_a, w_rg_i, b_rg_i, rg_lambda, w_out, ln_ffn_g, ln_ffn_b, w_mlp_in, w_mlp_out, loss_target, m_ln_mix_g, m_ln_mix_b, m_w_in, m_w_pool, m_pool_scale, m_conv_w, m_conv_b, m_w_rg_a, m_b_rg_a, m_w_rg_i, m_b_rg_i, m_rg_lambda, m_w_out, m_ln_ffn_g, m_ln_ffn_b, m_w_mlp_in, m_w_mlp_out, v_ln_mix_g, v_ln_mix_b, v_w_in, v_w_pool, v_pool_scale, v_conv_w, v_conv_b, v_w_rg_a, v_b_rg_a, v_w_rg_i, v_b_rg_i, v_rg_lambda, v_w_out, v_ln_ffn_g, v_ln_ffn_b, v_w_mlp_in, v_w_mlp_out):
    seq, d_model = x.shape[1], x.shape[2]
    dh = d_model // 2
    lh = dh // N_HEADS
    pg = dh // len(POOL_WINDOWS)
    d_ff = w_mlp_in.shape[2] * N_DEV
    assert lh == 128 and conv_w.shape[3] == lh and w_pool.shape[2] * N_DEV == pg

    xs = x[0]
    tgt = loss_target[0]

    def small_pack(cw, ba, bi, lam):
        return jnp.concatenate([cw.reshape(4, lh), ba.reshape(2, lh), bi.reshape(2, lh), lam.reshape(2, lh),
                                jnp.zeros((SMALL_ROWS - 10, lh), F32)], axis=0)

    pack_mine = small_pack(conv_w, b_rg_a, b_rg_i, rg_lambda)
    win_full, wpool_full, pack_full = _all_gather("gather_mixer", [
        (w_in[0].astype(BF16), 1), (w_pool[0].astype(BF16), 1), (pack_mine[None], 0)])
    wout_gather = _SplitGather("gather_w_out", [(w_out[0].astype(BF16), 0)], after=pack_full)
    w1_gather = _SplitGather("gather_w_mlp_in", [(w_mlp_in[0].astype(BF16), 1)], after=wout_gather.token)
    w2_gather = _SplitGather("gather_w_mlp_out", [(w_mlp_out[0].astype(BF16), 0)], after=w1_gather.token)
    wa_b = w_rg_a[0].astype(BF16)
    wi_b = w_rg_i[0].astype(BF16)
    vec = lambda i, j, k: (0, 0)
    row_full = lambda i, j, k: (i, 0)

    def after(token):
        return (token, _sp((8, 128), vec))

    def sds(shape, dtype):
        return jax.ShapeDtypeStruct(shape, dtype)

    def plain_epi(acc, i, ex, out):
        out[0][...] = acc

    def bf16_epi(acc, i, ex, out):
        out[0][...] = acc.astype(BF16)

    t = _tiles(seq, d_model, d_ff)

    (p3,) = _matmul(
        "proj", xs, win_full, _sp((t.rows, d_model), lambda i, j, k: (i, 0)), _sp((d_model, dh), lambda i, j, k: (0, j)),
        grid=(seq // t.rows, 3, 1), extras=[after(w2_gather.token)],
        out_shape=[sds((3, seq, dh), F32)], out_specs=[_sp((None, t.rows, dh), lambda i, j, k: (j, i, 0))],
        epilogue=plain_epi)

    d_pool, y_half = _pool_fwd(p3, wpool_full, pool_scale, seq, d_model)
    y, h0p, h1p = _lru_fwd(p3, y_half, pack_full, conv_b, wa_b, wi_b, wout_gather.relay(after=y_half), seq, d_model)
    (wout_full,) = wout_gather.wait(after=y)
    relay_token = w1_gather.relay(after=wout_full)

    def mix_epi(acc, i, ex, out):
        x_ref, g_ref, b_ref = ex[:3]
        z = ALPHA * x_ref[...] + acc
        x1, _, _ = _ln_fwd(z, g_ref[...], b_ref[...])
        out[0][...] = z
        out[1][...] = x1
        out[2][...] = x1.astype(BF16)

    z1, x1, x1b = _matmul(
        "mix_out", y, wout_full, _sp((t.ln_rows, d_model), row_full), _sp((d_model, d_model), vec, single=True),
        grid=(seq // t.ln_rows, 1, 1),
        extras=[(xs, _sp((t.ln_rows, d_model), row_full)), (ln_mix_g, _sp((1, d_model), vec)),
                (ln_mix_b, _sp((1, d_model), vec)), after(relay_token)],
        out_shape=[sds((seq, d_model), F32), sds((seq, d_model), F32), sds((seq, d_model), BF16)],
        out_specs=[_sp((t.ln_rows, d_model), row_full)] * 3, epilogue=mix_epi)
    (w1_full,) = w1_gather.wait(after=x1b)
    relay_token = w2_gather.relay(after=w1_full)

    def mlp_in_epi(acc, i, ex, out):
        h = jnp.maximum(acc, 0.0)
        out[0][...] = (h * h).astype(BF16)

    (hmid,) = _matmul(
        "mlp_in", x1b, w1_full, _sp((t.rows, d_model), lambda i, j, k: (i, 0)),
        _sp((d_model, t.ff_cols), lambda i, j, k: (0, j)),
        grid=(seq // t.rows, d_ff // t.ff_cols, 1), j_outer=True, extras=[after(relay_token)],
        out_shape=[sds((seq, d_ff), BF16)], out_specs=[_sp((t.rows, t.ff_cols), lambda i, j, k: (i, j))],
        epilogue=mlp_in_epi)
    (w2_full,) = w2_gather.wait(after=hmid)

    (ffn,) = _matmul(
        "mlp_out", hmid, w2_full, _sp((t.rows, t.ff_k), lambda i, j, k: (i, k)),
        _sp((t.ff_k, d_model), lambda i, j, k: (k, 0)),
        grid=(seq // t.rows, 1, d_ff // t.ff_k),
        out_shape=[sds((seq, d_model), F32)], out_specs=[_sp((t.rows, d_model), row_full)])
    dz2, dz2b, g_ffn_g, g_ffn_b, loss_part = _ln_loss_bwd(ffn, x1, tgt, ln_ffn_g, ln_ffn_b, t.ln_rows)

    def dpre_epi(acc, i, ex, out):
        out[0][...] = (acc * (2.0 * jnp.sqrt(ex[0][...].astype(F32)))).astype(BF16)

    (dpre,) = _matmul(
        "mlp_dpre", dz2b, w2_full, _sp((t.rows, d_model), lambda i, j, k: (i, 0)),
        _sp((t.ff_cols, d_model), lambda i, j, k: (j, 0)),
        grid=(seq // t.rows, d_ff // t.ff_cols, 1), j_outer=True, tb=True,
        extras=[(hmid, _sp((t.rows, t.ff_cols), lambda i, j, k: (i, j)))],
        out_shape=[sds((seq, d_ff), BF16)], out_specs=[_sp((t.rows, t.ff_cols), lambda i, j, k: (i, j))],
        epilogue=dpre_epi)

    (dx1_mlp,) = _matmul(
        "mlp_dx", dpre, w1_full, _sp((t.rows, t.ff_k), lambda i, j, k: (i, k)),
        _sp((d_model, t.ff_k), lambda i, j, k: (0, k)),
        grid=(seq // t.rows, 1, d_ff // t.ff_k), tb=True,
        out_shape=[sds((seq, d_model), F32)], out_specs=[_sp((t.rows, d_model), row_full)])
    dz1, dz1b, g_mix_g, g_mix_b = _ln_bwd_rows(dx1_mlp, dz2, z1, ln_mix_g, ln_mix_b, t.ln_rows)

    (g_w2,) = _matmul(
        "grad_w_mlp_out", hmid, dz2b, _sp((seq, t.grad_rows), lambda i, j, k: (0, i)),
        _sp((seq, d_model), vec, single=True),
        grid=(d_ff // t.grad_rows, 1, 1), ta=True,
        out_shape=[sds((d_ff, d_model), BF16)], out_specs=[_sp((t.grad_rows, d_model), row_full)],
        epilogue=bf16_epi)
    g_w2 = g_w2.reshape(N_DEV, d_ff // N_DEV, d_model)

    def block_epi(acc, i, ex, out):
        out[0][0] = acc.astype(BF16)

    fs = d_ff // N_DEV
    (g_w1,) = _matmul(
        "grad_w_mlp_in", x1b, dpre, _sp((seq, t.grad_rows), lambda i, j, k: (0, i)),
        _sp((seq, fs), lambda i, j, k: (0, j)),
        grid=(d_model // t.grad_rows, N_DEV, 1), j_outer=True, ta=True,
        out_shape=[sds((N_DEV, d_model, fs), BF16)],
        out_specs=[_sp((1, t.grad_rows, fs), lambda i, j, k: (j, i, 0))], epilogue=block_epi)

    (dy,) = _matmul(
        "mix_dy", dz1b, wout_full, _sp((t.rows, d_model), lambda i, j, k: (i, 0)),
        _sp((dh, d_model), lambda i, j, k: (j, 0)),
        grid=(seq // t.rows, 2, 1), j_outer=True, tb=True,
        out_shape=[sds((seq, d_model), F32)], out_specs=[_sp((t.rows, dh), lambda i, j, k: (i, j))],
        epilogue=plain_epi)
    (g_wout,) = _matmul(
        "grad_w_out", y, dz1b, _sp((seq, t.grad_rows), lambda i, j, k: (0, i)), _sp((seq, d_model), vec, single=True),
        grid=(d_model // t.grad_rows, 1, 1), ta=True,
        out_shape=[sds((d_model, d_model), BF16)], out_specs=[_sp((t.grad_rows, d_model), row_full)],
        epilogue=bf16_epi)
    g_wout = g_wout.reshape(N_DEV, d_model // N_DEV, d_model)

    scatter_a = _SplitReduceScatter("scatter_a", [g_w1, g_w2, g_wout])

    dproj_pool, g_wpool, g_pscale = _pool_bwd(d_pool, dy, wpool_full, pool_scale, scatter_a.token, seq, d_model)
    token_a = scatter_a.combine_and_send(after=dproj_pool)
    dproj, g_pack, g_convb, g_wa, g_wi = _lru_bwd(p3, dy, h0p, h1p, dproj_pool, pack_full, conv_b, wa_b, wi_b,
                                                  token_a, seq, d_model)

    rep_parts = [_rows128(g_wa), _rows128(g_wi), _rows128(g_mix_g), _rows128(g_mix_b), _rows128(g_ffn_g),
                 _rows128(g_ffn_b), _rows128(g_pscale), _rows128(g_convb)]
    rep_rows = [p.shape[0] for p in rep_parts]
    n_rep = sum(rep_rows)
    small = jnp.concatenate(rep_parts + [_rows128(g_pack)], axis=0)
    small_gather = _SplitGather("gather_small_grads", [(small[None], 0)], after=small)

    ws = 3 * dh // N_DEV

    def pair_epi(acc, i, ex, out):
        out[0][0] = acc[:, :ws].astype(BF16)
        out[0][1] = acc[:, ws:].astype(BF16)

    (g_win,) = _matmul(
        "grad_w_in", xs, dproj, _sp((seq, t.grad_rows), lambda i, j, k: (0, i)),
        _sp((seq, 2 * ws), lambda i, j, k: (0, j)),
        grid=(d_model // t.grad_rows, N_DEV // 2, 1), ta=True, extras=[after(small_gather.token)],
        out_shape=[sds((N_DEV, d_model, ws), BF16)],
        out_specs=[_sp((2, t.grad_rows, ws), lambda i, j, k: (j, i, 0))], epilogue=pair_epi)
    scatter_b = _SplitReduceScatter("scatter_b", [g_win, g_wpool.reshape(N_DEV, pg // N_DEV * len(POOL_WINDOWS), pg)])

    def adam_big(name, own_landed, w, m, v):
        own, landed = own_landed
        shp = w.shape
        two = lambda a: a.reshape(-1, shp[-1])
        res = _sum_adamw(name, own, landed, two(w), two(m), two(v))
        return [r.reshape(shp) for r in res]

    r_w1, r_w2, r_wout = scatter_a.wait(after=scatter_b.token)
    o_w1 = adam_big("adam_w_mlp_in", r_w1, w_mlp_in, m_w_mlp_in, v_w_mlp_in)
    token_b = scatter_b.combine_and_send(after=o_w1[0])

    def dx_epi(acc, i, ex, out):
        out[0][...] = ALPHA * ex[0][...] + acc

    (dx,) = _matmul(
        "grad_x", dproj, win_full, _sp((t.ln_rows * 2, 3 * dh), lambda i, j, k: (i, 0)),
        _sp((d_model, 3 * dh), vec, single=True),
        grid=(seq // (t.ln_rows * 2), 1, 1), tb=True,
        extras=[(dz1, _sp((t.ln_rows * 2, d_model), row_full)), after(token_b)],
        out_shape=[sds((seq, d_model), F32)], out_specs=[_sp((t.ln_rows * 2, d_model), row_full)],
        epilogue=dx_epi)
    o_w2 = adam_big("adam_w_mlp_out", r_w2, w_mlp_out, m_w_mlp_out, v_w_mlp_out)
    o_wout = adam_big("adam_w_out", r_wout, w_out, m_w_out, v_w_out)
    r_win, r_wpool = scatter_b.wait(after=dx)
    o_win = adam_big("adam_w_in", r_win, w_in, m_w_in, v_w_in)
    o_wpool = adam_big("adam_w_pool", r_wpool, w_pool, m_w_pool, v_w_pool)

    small_gather.relay(after=o_win[0])
    (small_all,) = small_gather.wait(after=o_wpool[0])

    rep_w = [w_rg_a, w_rg_i, ln_mix_g, ln_mix_b, ln_ffn_g, ln_ffn_b, pool_scale, conv_b]
    rep_m = [m_w_rg_a, m_w_rg_i, m_ln_mix_g, m_ln_mix_b, m_ln_ffn_g, m_ln_ffn_b, m_pool_scale, m_conv_b]
    rep_v = [v_w_rg_a, v_w_rg_i, v_ln_mix_g, v_ln_mix_b, v_ln_ffn_g, v_ln_ffn_b, v_pool_scale, v_conv_b]
    cat = lambda arrs: jnp.concatenate([_rows128(a) for a in arrs], axis=0)
    o_rep = _sum_adamw("adam_replicated", None, small_all[:, :n_rep, :], cat(rep_w), cat(rep_m), cat(rep_v))

    my_idx = _dev_index(_where_am_i())
    head_parts = lax.dynamic_slice_in_dim(small_all, n_rep + my_idx * SMALL_ROWS, SMALL_ROWS, axis=1)
    o_head = _sum_adamw("adam_head", None, head_parts, pack_mine,
                        small_pack(m_conv_w, m_b_rg_a, m_b_rg_i, m_rg_lambda),
                        small_pack(v_conv_w, v_b_rg_a, v_b_rg_i, v_rg_lambda))

    def unpack_rep(packed):
        out, r = [], 0
        for wgt, rows in zip(rep_w, rep_rows):
            out.append(packed[r:r + rows].reshape(wgt.shape))
            r += rows
        return out

    def unpack_head(packed):
        return [packed[0:4].reshape(conv_w.shape), packed[4:6].reshape(b_rg_a.shape),
                packed[6:8].reshape(b_rg_i.shape), packed[8:10].reshape(rg_lambda.shape)]

    loss = lax.psum(loss_part[0, 0], ("x", "y", "c"))

    outs = [loss, dx[None]]
    for kind in range(4):
        ra, ri, mg, mb, fg, fb, ps, cb = unpack_rep(o_rep[kind])
        cw, ba, bi, lam = unpack_head(o_head[kind])
        outs += [mg, mb, o_win[kind], o_wpool[kind], ps, cw, cb, ra, ba, ri, bi, lam, o_wout[kind], fg, fb,
                 o_w1[kind], o_w2[kind]]
    return tuple(outs)
```

```python
import functools

import jax
import jax.numpy as jnp
from jax import lax
from jax.experimental import pallas as pl
from jax.experimental.pallas import tpu as pltpu

F32 = jnp.float32
BF16 = jnp.bfloat16
MESH = pl.DeviceIdType.MESH
ANY = pl.BlockSpec(memory_space=pl.ANY)

N_DEV = 8
POOL_WINDOWS = (2, 4, 8, 16)
N_HEADS = 8
RG_C = 8.0
LN_EPS = 1e-5
ALPHA = 2.0 ** 0.25
ADAM_LR = 0.001
ADAM_B1 = 0.9
ADAM_B2 = 0.999
ADAM_EPS = 1e-08
ADAM_WD = 0.01
ADAM_STEP = 10

VMEM_LIMIT = 56 * 1024 * 1024
WIN_HALO = 16
CONV_HALO = 8
SMALL_ROWS = 16


def _params(n_grid):
    return pltpu.CompilerParams(dimension_semantics=("arbitrary",) * n_grid, vmem_limit_bytes=VMEM_LIMIT)


def _shift(v, j):
    n = v.shape[0]
    s = (-j) % n
    return v if s == 0 else pltpu.roll(v, s, 0)


def _sigmoid(x):
    return 0.5 * jnp.tanh(0.5 * x) + 0.5


def _softplus(z):
    e = jnp.exp(-jnp.abs(z))
    u = 1.0 + e
    log1p = jnp.where(u == 1.0, e, jnp.log(u) * (e / jnp.where(u == 1.0, 1.0, u - 1.0)))
    return jnp.maximum(z, 0.0) + log1p


_GELU_C = 0.7978845608028654
_GELU_K = 0.044715


def _gelu_and_grad(x):
    x2 = x * x
    t = jnp.tanh(_GELU_C * (x + _GELU_K * x * x2))
    g = 0.5 * x * (1.0 + t)
    dg = 0.5 * (1.0 + t) + 0.5 * x * (1.0 - t * t) * (_GELU_C * (1.0 + 3.0 * _GELU_K * x2))
    return g, dg


def _ln_fwd(z, g, b):
    mu = jnp.mean(z, axis=-1, keepdims=True)
    zc = z - mu
    var = jnp.mean(zc * zc, axis=-1, keepdims=True)
    rstd = lax.rsqrt(var + LN_EPS)
    xhat = zc * rstd
    return xhat * g + b, xhat, rstd


def _ln_bwd(dy, xhat, rstd, g):
    dxhat = dy * g
    m1 = jnp.mean(dxhat, axis=-1, keepdims=True)
    m2 = jnp.mean(dxhat * xhat, axis=-1, keepdims=True)
    dz = rstd * (dxhat - m1 - xhat * m2)
    dg = jnp.sum(dy * xhat, axis=0, keepdims=True)
    db = jnp.sum(dy, axis=0, keepdims=True)
    return dz, dg, db


def _acc_rows(ref, first, val):
    @pl.when(first)
    def _():
        ref[...] = val

    @pl.when(jnp.logical_not(first))
    def _():
        ref[...] += val


def _sp(shape, fn, single=False):
    return shape, fn, single


def _matmul(name, a, b, a_spec, b_spec, *, grid, j_outer=False, ta=False, tb=False, extras=(), out_shape, out_specs,
            epilogue=None, n_split=1):
    ni, nj, nk = grid
    n_ex = len(extras)
    dims = (((0 if ta else 1,), (1 if tb else 0,)), ((), ()))

    def mk(spec):
        shape, fn, single = spec
        index = (lambda g0, g1, g2: fn(g1, g0, g2)) if j_outer else fn
        return pl.BlockSpec(shape, index, pipeline_mode=pl.Buffered(1)) if single else pl.BlockSpec(shape, index)

    def body(a_ref, b_ref, *rest):
        ex_refs = rest[:n_ex]
        out_refs = rest[n_ex:]
        i = pl.program_id(1 if j_outer else 0)
        if n_split > 1:
            av = a_ref[...].astype(BF16)
            width = b_ref.shape[0 if tb else 1] // n_split
            for c in range(n_split):
                cols = pl.ds(c * width, width)
                bv = (b_ref[cols, :] if tb else b_ref[:, cols]).astype(BF16)
                epilogue(lax.dot_general(av, bv, dims, preferred_element_type=F32), i, ex_refs, out_refs, cols)
            return
        part = lax.dot_general(a_ref[...].astype(BF16), b_ref[...].astype(BF16), dims, preferred_element_type=F32)
        if nk == 1:
            epilogue(part, i, ex_refs, out_refs)
        else:
            @pl.when(pl.program_id(2) == 0)
            def _():
                out_refs[0][...] = jnp.zeros(out_refs[0].shape, F32)

            out_refs[0][...] += part

    return pl.pallas_call(
        body, name=name, grid=(nj, ni, nk) if j_outer else (ni, nj, nk),
        in_specs=[mk(a_spec), mk(b_spec)] + [mk(s) for _, s in extras],
        out_specs=[mk(s) for s in out_specs], out_shape=list(out_shape),
        compiler_params=_params(3),
    )(a, b, *[x for x, _ in extras])


def _bs(shape, fn):
    return pl.BlockSpec(shape, fn)


def _where_am_i():
    x, y, c = lax.axis_index("x"), lax.axis_index("y"), lax.axis_index("c")
    return x, y, c


def _dev_index(p):
    return 4 * p[0] + 2 * p[1] + p[2]


def _slab(ref, axis, idx, size):
    sl = [slice(None)] * len(ref.shape)
    sl[axis] = pl.ds(idx * size, size)
    return ref.at[tuple(sl)]


def _all_gather(name, items):
    n = len(items)
    shapes = []
    for shard, axis in items:
        s = list(shard.shape)
        s[axis] *= N_DEV
        shapes.append(jax.ShapeDtypeStruct(tuple(s), shard.dtype))

    def body(*refs):
        in_refs, out_refs = refs[:n], refs[n:2 * n]
        send_sems, recv_sems, local_sems = refs[2 * n:]
        x, y, c = _where_am_i()
        me, sibling = (x, y, c), (x, y, 1 - c)
        chips = [(1 - x, y), (x, 1 - y), (1 - x, 1 - y)]

        def blk(a, p):
            axis = items[a][1]
            return _slab(out_refs[a], axis, _dev_index(p), items[a][0].shape[axis])

        def copy(a, k, block, to, src=None):
            return pltpu.make_async_remote_copy(
                src_ref=blk(a, block) if src is None else src, dst_ref=blk(a, block),
                send_sem=send_sems.at[a, k], recv_sem=recv_sems.at[a, k], device_id=to, device_id_type=MESH)

        mine = [pltpu.make_async_copy(in_refs[a], blk(a, me), local_sems.at[a]) for a in range(n)]
        for cp in mine:
            cp.start()
        first = []
        for a in range(n):
            first.append(copy(a, 0, me, sibling, src=in_refs[a]))
            first += [copy(a, 1 + j, me, (*chip, c), src=in_refs[a]) for j, chip in enumerate(chips)]
        for cp in first:
            cp.start()
        passed = []
        for a in range(n):
            for j, chip in enumerate(chips):
                copy(a, 1 + j, (*chip, c), me).wait_recv()
                fw = copy(a, 4 + j, (*chip, c), sibling)
                fw.start()
                passed.append(fw)
        for a in range(n):
            copy(a, 0, sibling, me).wait_recv()
            for j, chip in enumerate(chips):
                copy(a, 4 + j, (*chip, 1 - c), me).wait_recv()
        for cp in first + passed:
            cp.wait_send()
        for cp in mine:
            cp.wait()

    outs = pl.pallas_call(
        body, name=name, out_shape=shapes, in_specs=[ANY] * n, out_specs=[ANY] * n,
        scratch_shapes=[pltpu.SemaphoreType.DMA((n, 7)), pltpu.SemaphoreType.DMA((n, 7)),
                        pltpu.SemaphoreType.DMA((n,))],
    )(*[s for s, _ in items])
    return list(outs)


HBM = pl.BlockSpec(memory_space=pltpu.HBM)
SEM = pl.BlockSpec(memory_space=pltpu.SEMAPHORE)
DATAFLOW = pltpu.SideEffectType.DATAFLOW_SIDE_EFFECTING


def _in_hbm(a):
    return pltpu.with_memory_space_constraint(a, pltpu.HBM)


def _token_shape():
    return jax.ShapeDtypeStruct((8, 128), F32)


def _split_start(name, n_sems, bufs, issue):
    nb = len(bufs)

    def body(*refs):
        issue(refs[:nb], refs[nb], refs[nb + 1])
        refs[-1][...] = jnp.zeros((8, 128), F32)

    outs = pl.pallas_call(
        body, name=name,
        out_shape=(pltpu.SemaphoreType.DMA((n_sems,)), pltpu.SemaphoreType.DMA((n_sems,)),
                   *[pltpu.HBM(b.shape, b.dtype) for b in bufs], _token_shape()),
        in_specs=[HBM] * nb, out_specs=(SEM, SEM, *[HBM] * nb, pl.BlockSpec(memory_space=pltpu.VMEM)),
        input_output_aliases={i: 2 + i for i in range(nb)},
        compiler_params=pltpu.CompilerParams(has_side_effects=DATAFLOW),
    )(*[_in_hbm(b) for b in bufs])
    return outs[0], outs[1], list(outs[2:2 + nb]), outs[-1]


def _split_relay(name, n_sems, sems, bufs, after, relay):
    nb = len(bufs)

    def body(*refs):
        relay(refs[:nb], refs[nb], refs[nb + 1], refs[nb + 3], refs[nb + 4])
        refs[-1][...] = jnp.zeros((8, 128), F32)

    outs = pl.pallas_call(
        body, name=name,
        out_shape=(pltpu.SemaphoreType.DMA((n_sems,)), pltpu.SemaphoreType.DMA((n_sems,)),
                   *[pltpu.HBM(b.shape, b.dtype) for b in bufs], _token_shape()),
        in_specs=[HBM] * nb + [SEM, SEM, ANY],
        out_specs=(SEM, SEM, *[HBM] * nb, pl.BlockSpec(memory_space=pltpu.VMEM)),
        input_output_aliases={i: 2 + i for i in range(nb)},
        compiler_params=pltpu.CompilerParams(has_side_effects=DATAFLOW),
    )(*bufs, sems[0], sems[1], after)
    return outs[0], outs[1], list(outs[2:2 + nb]), outs[-1]


def _split_wait(name, sems, bufs, after, finish):
    nb = len(bufs)

    def body(*refs):
        finish(refs[:nb], refs[nb], refs[nb + 1])

    outs = pl.pallas_call(
        body, name=name, out_shape=[pltpu.HBM(b.shape, b.dtype) for b in bufs],
        in_specs=[HBM] * nb + [SEM, SEM, ANY], out_specs=[HBM] * nb,
        input_output_aliases={i: i for i in range(nb)},
        compiler_params=pltpu.CompilerParams(has_side_effects=DATAFLOW),
    )(*bufs, sems[0], sems[1], after)
    return list(outs)


def _place(name, items, dtype, after):
    ids = jnp.reshape(_dev_index(_where_am_i()), (1,)).astype(jnp.int32)
    outs = []
    for a, (shard, axis) in enumerate(items):
        rows, cols = shard.shape[-2], shard.shape[-1]
        tr = rows
        while tr * cols * shard.dtype.itemsize > 4 * 1024 * 1024 and tr % 32 == 0:
            tr //= 2
        nt = rows // tr
        full = list(shard.shape)
        full[axis] *= N_DEV
        if shard.ndim == 2 and axis == 0:
            in_spec = _bs((tr, cols), lambda i, ids: (i, 0))
            out_spec = _bs((tr, cols), lambda i, ids, nt=nt: (ids[0] * nt + i, 0))
        elif shard.ndim == 2 and axis == 1:
            in_spec = _bs((tr, cols), lambda i, ids: (i, 0))
            out_spec = _bs((tr, cols), lambda i, ids: (i, ids[0]))
        else:
            assert shard.ndim == 3 and axis == 0 and shard.shape[0] == 1
            in_spec = _bs((None, tr, cols), lambda i, ids: (0, i, 0))
            out_spec = _bs((None, tr, cols), lambda i, ids: (ids[0], i, 0))

        def body(ids_ref, in_ref, after_ref, out_ref):
            del ids_ref, after_ref
            out_ref[...] = in_ref[...].astype(out_ref.dtype)

        outs.append(pl.pallas_call(
            body, name=f"{name}{a}",
            grid_spec=pltpu.PrefetchScalarGridSpec(
                num_scalar_prefetch=1, grid=(nt,), in_specs=[in_spec, ANY], out_specs=out_spec),
            out_shape=jax.ShapeDtypeStruct(tuple(full), dtype), compiler_params=_params(1),
        )(ids, shard, after))
    return outs


class _SplitGather:
    def __init__(self, name, items, dtype, after):
        self.name, self.items, self.n = name, items, len(items)
        fulls = _place(name + "_place", items, dtype, after)
        n = self.n

        def issue(refs, send, recv):
            me, sibling, chips, c = self._geometry()
            for a in range(n):
                self._copy1(refs, send, recv, a, 0, me, sibling).start()
                for j, chip in enumerate(chips):
                    self._copy1(refs, send, recv, a, 1 + j, me, (*chip, c)).start()

        self.send, self.recv, self.bufs, self.token = _split_start(name + "_start", 4 * n, fulls, issue)

    @staticmethod
    def _geometry():
        x, y, c = _where_am_i()
        return (x, y, c), (x, y, 1 - c), [(1 - x, y), (x, 1 - y), (1 - x, 1 - y)], c

    def _blk(self, refs, a, p):
        shard, axis = self.items[a]
        return _slab(refs[a], axis, _dev_index(p), shard.shape[axis])

    def _copy1(self, refs, send, recv, a, k, owner, to):
        return pltpu.make_async_remote_copy(
            src_ref=self._blk(refs, a, owner), dst_ref=self._blk(refs, a, owner), send_sem=send.at[4 * a + k],
            recv_sem=recv.at[4 * a + k], device_id=to, device_id_type=MESH)

    def _copy2(self, refs, send, recv, a, j, owner, to):
        return pltpu.make_async_remote_copy(
            src_ref=self._blk(refs, a, owner), dst_ref=self._blk(refs, a, owner), send_sem=send.at[3 * a + j],
            recv_sem=recv.at[3 * a + j], device_id=to, device_id_type=MESH)

    def relay(self, after):
        n = self.n

        def relay(refs, send_in, recv_in, send_out, recv_out):
            me, sibling, chips, c = self._geometry()
            for a in range(n):
                for j, chip in enumerate(chips):
                    self._copy1(refs, send_in, recv_in, a, 1 + j, (*chip, c), me).wait_recv()
                    self._copy2(refs, send_out, recv_out, a, j, (*chip, c), sibling).start()
            for a in range(n):
                self._copy1(refs, send_in, recv_in, a, 0, sibling, me).wait_recv()
                for k in range(4):
                    self._copy1(refs, send_in, recv_in, a, k, me, sibling).wait_send()

        self.send, self.recv, self.bufs, self.token = _split_relay(
            self.name + "_relay", 3 * n, (self.send, self.recv), self.bufs, after, relay)
        return self.token

    def wait(self, after):
        n = self.n

        def finish(refs, send, recv):
            me, sibling, chips, c = self._geometry()
            for a in range(n):
                for j, chip in enumerate(chips):
                    self._copy2(refs, send, recv, a, j, (*chip, 1 - c), me).wait_recv()
                    self._copy2(refs, send, recv, a, j, (*chip, c), sibling).wait_send()

        return _split_wait(self.name + "_wait", (self.send, self.recv), self.bufs, after, finish)


class _SplitReduceScatter:
    def __init__(self, name, grads):
        self.name, self.n = name, len(grads)
        n = self.n
        g4 = [g.reshape(4, 2, *g.shape[1:]) for g in grads]
        land = [lax.empty((4, 1, *g.shape[1:]), g.dtype) for g in grads]

        def issue(refs, send, recv):
            for a in range(n):
                self._swap(refs, send, recv, a).start()

        self.send, self.recv, self.bufs, self.token = _split_start(name + "_d2d_start", n, g4 + land, issue)

    def _swap(self, refs, send, recv, a):
        x, y, c = _where_am_i()
        return pltpu.make_async_remote_copy(
            src_ref=refs[a].at[:, pl.ds(1 - c, 1)], dst_ref=refs[self.n + a], send_sem=send.at[a], recv_sem=recv.at[a],
            device_id=(x, y, 1 - c), device_id_type=MESH)

    def _hop(self, refs, send, recv, a, m):
        x, y, c = _where_am_i()
        px = (1 - x) if m & 2 else x
        py = (1 - y) if m & 1 else y
        return pltpu.make_async_remote_copy(
            src_ref=refs[a].at[2 * px + py], dst_ref=refs[self.n + a].at[m - 1], send_sem=send.at[3 * a + m - 1],
            recv_sem=recv.at[3 * a + m - 1], device_id=(px, py, c), device_id_type=MESH)

    def combine_and_send(self, after):
        n = self.n

        def finish(refs, send, recv):
            for a in range(n):
                self._swap(refs, send, recv, a).wait()

        bufs = _split_wait(self.name + "_d2d_wait", (self.send, self.recv), self.bufs, after, finish)
        x, y, c = _where_am_i()
        ids = jnp.stack([c, 2 * x + y]).astype(jnp.int32)
        self.own, sums = [], []
        for a in range(n):
            own, hb = _pair_sum(f"{self.name}_sum{a}", bufs[a], bufs[n + a], ids)
            self.own.append(own)
            sums.append(hb)
        land = [lax.empty((3, *h.shape[1:]), h.dtype) for h in sums]

        def issue(refs, send, recv):
            for a in range(n):
                for m in (1, 2, 3):
                    self._hop(refs, send, recv, a, m).start()

        self.send, self.recv, self.bufs, self.token = _split_start(self.name + "_ici_start", 3 * n, sums + land, issue)
        return self.token

    def wait(self, after):
        n = self.n

        def finish(refs, send, recv):
            for a in range(n):
                for m in (1, 2, 3):
                    self._hop(refs, send, recv, a, m).wait()

        bufs = _split_wait(self.name + "_ici_wait", (self.send, self.recv), self.bufs, after, finish)
        return list(zip(self.own, bufs[n:]))


def _pair_sum(name, g4, land, ids):
    rows, cols = g4.shape[2], g4.shape[3]
    tr = rows
    while tr * cols * 2 > 1024 * 1024 and tr % 32 == 0:
        tr //= 2

    def body(ids_ref, g_ref, l_ref, own_ref, sum_ref):
        h = g_ref[...].astype(F32) + l_ref[...].astype(F32)
        sum_ref[...] = h.astype(sum_ref.dtype)

        @pl.when(pl.program_id(1) == ids_ref[1])
        def _():
            own_ref[...] = h

    return pl.pallas_call(
        body, name=name,
        grid_spec=pltpu.PrefetchScalarGridSpec(
            num_scalar_prefetch=1, grid=(rows // tr, 4),
            in_specs=[_bs((None, None, tr, cols), lambda i, q, ids: (q, ids[0], i, 0)),
                      _bs((None, None, tr, cols), lambda i, q, ids: (q, 0, i, 0))],
            out_specs=[_bs((tr, cols), lambda i, q, ids: (i, 0)), _bs((None, tr, cols), lambda i, q, ids: (q, i, 0))]),
        out_shape=[jax.ShapeDtypeStruct((rows, cols), F32), jax.ShapeDtypeStruct((4, rows, cols), g4.dtype)],
        compiler_params=_params(2),
    )(ids, g4, land)


def _win_sum(ext, w, off):
    s = ext + _shift(ext, -1)
    if w >= 4:
        s = _shift(s, -1) + _shift(s, 1)
    if w >= 8:
        s = _shift(s, -2) + _shift(s, 2)
    if w >= 16:
        s = _shift(s, -4) + _shift(s, 4)
    return _shift(s, off) if off else s


def _inv_count(r0, t, w, seq):
    pos = r0 + lax.broadcasted_iota(jnp.int32, (t, 1), 0)
    cnt = jnp.minimum(pos + w // 2, seq) - jnp.maximum(pos - w // 2, 0)
    return 1.0 / cnt.astype(F32)


def _pool_fwd(p3, w_pool, pool_scale, seq, d_model):
    dp = d_model // 2
    pg = dp // len(POOL_WINDOWS)
    t = min(128, seq)
    n_chunks = seq // t
    h = WIN_HALO

    def body(u_ref, w_ref, sc_ref, d_ref, y_ref, pad_ref):
        g = pl.program_id(0)
        zeros = jnp.zeros((h, pg), F32)
        pad_ref[0:h, :] = zeros
        pad_ref[h + seq:h + seq + h, :] = zeros

        def fill(ci, _):
            r0 = pl.multiple_of(ci * t, t)
            pad_ref[pl.ds(h + r0, t), :] = u_ref[pl.ds(r0, t), :]
            return 0

        lax.fori_loop(0, n_chunks, fill, 0)
        wmat = w_ref[...]
        scale = sc_ref[...]
        for gi, w in enumerate(POOL_WINDOWS):
            @pl.when(g == gi)
            def _(w=w):
                def chunk(ci, _):
                    r0 = pl.multiple_of(ci * t, t)
                    ext = pad_ref[pl.ds(r0, t + 2 * h), :]
                    mean = _win_sum(ext, w, 0)[h:h + t, :] * _inv_count(r0, t, w, seq)
                    d = (mean - ext[h:h + t, :]).astype(BF16)
                    d_ref[pl.ds(r0, t), :] = d
                    q = jnp.dot(d, wmat, preferred_element_type=F32)
                    y_ref[pl.ds(r0, t), :] = (q * scale).astype(BF16)
                    return 0

                lax.fori_loop(0, n_chunks, chunk, 0)

    return pl.pallas_call(
        body, name="pool_fwd", grid=(len(POOL_WINDOWS),),
        in_specs=[_bs((None, seq, pg), lambda g: (0, 0, g)), _bs((None, pg, pg), lambda g: (g, 0, 0)),
                  _bs((1, pg), lambda g: (0, g))],
        out_specs=[_bs((seq, pg), lambda g: (0, g)), _bs((seq, pg), lambda g: (0, g))],
        out_shape=[jax.ShapeDtypeStruct((seq, dp), BF16), jax.ShapeDtypeStruct((seq, d_model), BF16)],
        scratch_shapes=[pltpu.VMEM((seq + 2 * h, pg), F32)],
        compiler_params=_params(1),
    )(p3, w_pool, pool_scale)


def _pool_bwd(d, dy, w_pool, pool_scale, token, seq, d_model):
    dp = d_model // 2
    pg = dp // len(POOL_WINDOWS)
    t = min(128, seq)
    n_chunks = seq // t
    h = WIN_HALO
    tn_dims = (((0,), (0,)), ((), ()))
    nt_dims = (((1,), (1,)), ((), ()))

    def body(d_ref, dy_ref, w_ref, sc_ref, tok_ref, du_ref, dwb_ref, dsc_ref, pad_ref, dd_ref, dw_ref):
        del tok_ref
        g = pl.program_id(0)
        zeros = jnp.zeros((h, pg), F32)
        pad_ref[0:h, :] = zeros
        pad_ref[h + seq:h + seq + h, :] = zeros
        wmat = w_ref[...]
        scale = sc_ref[...]
        for gi, w in enumerate(POOL_WINDOWS):
            @pl.when(g == gi)
            def _(w=w):
                dw_ref[...] = jnp.zeros((pg, pg), F32)

                def first(ci, dsc):
                    r0 = pl.multiple_of(ci * t, t)
                    dv = d_ref[pl.ds(r0, t), :]
                    dyv = dy_ref[pl.ds(r0, t), :]
                    q = jnp.dot(dv, wmat, preferred_element_type=F32)
                    dsc = dsc + jnp.sum(dyv * q, axis=0, keepdims=True)
                    dq = (dyv * scale).astype(BF16)
                    dw_ref[...] += lax.dot_general(dv, dq, tn_dims, preferred_element_type=F32)
                    dd = lax.dot_general(dq, wmat, nt_dims, preferred_element_type=F32)
                    dd_ref[pl.ds(r0, t), :] = dd
                    pad_ref[pl.ds(h + r0, t), :] = dd * _inv_count(r0, t, w, seq)
                    return dsc

                dsc_ref[...] = lax.fori_loop(0, n_chunks, first, jnp.zeros((1, pg), F32))
                dwb_ref[...] = dw_ref[...].reshape(N_DEV, pg // N_DEV, pg).astype(BF16)

                def second(ci, _):
                    r0 = pl.multiple_of(ci * t, t)
                    ext = pad_ref[pl.ds(r0, t + 2 * h), :]
                    back = _win_sum(ext, w, 1)[h:h + t, :]
                    du_ref[pl.ds(r0, t), :] = (back - dd_ref[pl.ds(r0, t), :]).astype(BF16)
                    return 0

                lax.fori_loop(0, n_chunks, second, 0)

    return pl.pallas_call(
        body, name="pool_bwd", grid=(len(POOL_WINDOWS),),
        in_specs=[_bs((seq, pg), lambda g: (0, g)), _bs((seq, pg), lambda g: (0, g)),
                  _bs((None, pg, pg), lambda g: (g, 0, 0)), _bs((1, pg), lambda g: (0, g)),
                  _bs((8, 128), lambda g: (0, 0))],
        out_specs=[_bs((seq, pg), lambda g: (0, g)), _bs((N_DEV, None, pg // N_DEV, pg), lambda g: (0, g, 0, 0)),
                   _bs((1, pg), lambda g: (0, g))],
        out_shape=[jax.ShapeDtypeStruct((seq, 3 * dp), BF16),
                   jax.ShapeDtypeStruct((N_DEV, len(POOL_WINDOWS), pg // N_DEV, pg), BF16),
                   jax.ShapeDtypeStruct((1, dp), F32)],
        scratch_shapes=[pltpu.VMEM((seq + 2 * h, pg), F32), pltpu.VMEM((seq, pg), F32), pltpu.VMEM((pg, pg), F32)],
        compiler_params=_params(1),
    )(d, dy, w_pool, pool_scale, token)


def _segment_scan(seg_len, loads, hs_refs, ps_refs, stores):
    lanes = hs_refs[0].shape[-1]
    row = lax.broadcasted_iota(jnp.int32, (8, lanes), 0)

    def pos_of(n, s):
        return s if n == 0 else seg_len - 1 - s

    def step1(s, carry):
        out = []
        for n in range(2):
            hh, pp = carry[n]
            pos = pos_of(n, s)
            coef, inp = loads[n](pos)
            hh = coef * hh + inp
            pp = coef * pp
            hs_refs[n][pl.ds(pl.multiple_of(pos * 8, 8), 8), :] = hh
            ps_refs[n][pl.ds(pl.multiple_of(pos * 8, 8), 8), :] = pp
            out.append((hh, pp))
        return tuple(out)

    init = tuple((jnp.zeros((8, lanes), F32), jnp.ones((8, lanes), F32)) for _ in range(2))
    ends = lax.fori_loop(0, seg_len, step1, init, unroll=4)

    entry = []
    for n in range(2):
        bb, aa = ends[n]
        for sh in (1, 2, 4):
            if n == 0:
                ok = row >= sh
                ap = jnp.where(ok, pltpu.roll(aa, sh, 0), 1.0)
                bp = jnp.where(ok, pltpu.roll(bb, sh, 0), 0.0)
            else:
                ok = row < 8 - sh
                ap = jnp.where(ok, pltpu.roll(aa, 8 - sh, 0), 1.0)
                bp = jnp.where(ok, pltpu.roll(bb, 8 - sh, 0), 0.0)
            bb = aa * bp + bb
            aa = aa * ap
        if n == 0:
            entry.append(jnp.where(row >= 1, pltpu.roll(bb, 1, 0), 0.0))
        else:
            entry.append(jnp.where(row < 7, pltpu.roll(bb, 7, 0), 0.0))

    def step2(s, _):
        for n in range(2):
            at = pl.ds(pl.multiple_of(s * 8, 8), 8)
            stores[n](s, hs_refs[n][at, :] + ps_refs[n][at, :] * entry[n])
        return 0

    lax.fori_loop(0, seg_len, step2, 0, unroll=4)


def _gate_preacts(xc, wcat_ref):
    xcb = xc.astype(BF16)
    return xcb, jnp.dot(xcb, wcat_ref[...], preferred_element_type=F32)


def _gates(pre, n, pk_ref, sp):
    lh = pre.shape[1] // 4
    r = _sigmoid(pre[:, (2 * n) * lh:(2 * n + 1) * lh] + pk_ref[pl.ds(4 + n, 1), :])
    i = _sigmoid(pre[:, (2 * n + 1) * lh:(2 * n + 2) * lh] + pk_ref[pl.ds(6 + n, 1), :])
    log_a = (-RG_C * r) * sp[n]
    a = jnp.exp(log_a)
    x = 2.0 * log_a
    one_minus_a2 = jnp.where(x > -0.01, -(x * (1.0 + x * (0.5 + x * (1.0 / 6.0)))), 1.0 - a * a)
    m = jnp.sqrt(one_minus_a2)
    return r, i, a, m


def _conv_chunk(upad_ref, pk_ref, cb, r0, t):
    ext = upad_ref[pl.ds(r0, t + 2 * CONV_HALO), :]
    acc = pk_ref[pl.ds(1, 1), :] * ext
    for k in (0, 2, 3):
        acc = acc + pk_ref[pl.ds(k, 1), :] * _shift(ext, k - 1)
    return acc[CONV_HALO:CONV_HALO + t, :] + cb, ext


def _lru_fwd(p3, y_in, pack, conv_b, wcat, token, seq, d_model):
    dl = d_model // 2
    lh = dl // N_HEADS
    t = min(128, seq)
    n_chunks = seq // t
    seg = seq // 8
    hal = CONV_HALO
    first_rec_block = (d_model - dl) // lh

    def body(ur_ref, ug_ref, pk_ref, cb_ref, wcat_ref, yin_ref, tok_ref, y_ref, h0_ref, h1_ref,
             upad, a_scr, b_scr, hs0, hs1, ps0, ps1):
        del yin_ref, tok_ref
        zeros = jnp.zeros((hal, lh), F32)
        upad[0:hal, :] = zeros
        upad[hal + seq:hal + seq + hal, :] = zeros
        for ref in (h0_ref, h1_ref):
            ref[0:hal, :] = zeros
            ref[hal + seq:hal + seq + hal, :] = zeros

        def fill(ci, _):
            r0 = pl.multiple_of(ci * t, t)
            upad[pl.ds(hal + r0, t), :] = ur_ref[pl.ds(r0, t), :]
            return 0

        lax.fori_loop(0, n_chunks, fill, 0)
        cb = cb_ref[...]
        sp = [_softplus(-pk_ref[pl.ds(8 + n, 1), :]) for n in range(2)]

        def chunk(ci, _):
            r0 = pl.multiple_of(ci * t, t)
            xc, _ext = _conv_chunk(upad, pk_ref, cb, r0, t)
            _, pre = _gate_preacts(xc, wcat_ref)
            for n in range(2):
                _, i, a, m = _gates(pre, n, pk_ref, sp)
                a_scr[n, pl.ds(r0, t), :] = a
                b_scr[n, pl.ds(r0, t), :] = (m * i) * xc
            return 0

        lax.fori_loop(0, n_chunks, chunk, 0, unroll=2)

        def load(n):
            return lambda pos: (a_scr[n, pl.ds(pos, 8, stride=seg), :], b_scr[n, pl.ds(pos, 8, stride=seg), :])

        def store(n, ref):
            def put(s, v):
                ref[pl.ds(hal + s, 8, stride=seg), :] = v
            return put

        _segment_scan(seg, [load(0), load(1)], [hs0, hs1], [ps0, ps1], [store(0, h0_ref), store(1, h1_ref)])

        def out(ci, _):
            r0 = pl.multiple_of(ci * t, t)
            hsum = h0_ref[pl.ds(hal + r0, t), :] + h1_ref[pl.ds(hal + r0, t), :]
            gl, _dg = _gelu_and_grad(ug_ref[pl.ds(r0, t), :])
            y_ref[pl.ds(r0, t), :] = (hsum * gl).astype(BF16)
            return 0

        lax.fori_loop(0, n_chunks, out, 0)

    return pl.pallas_call(
        body, name="lru_fwd", grid=(N_HEADS,),
        in_specs=[_bs((None, seq, lh), lambda h: (1, 0, h)), _bs((None, seq, lh), lambda h: (2, 0, h)),
                  _bs((None, SMALL_ROWS, lh), lambda h: (h, 0, 0)), _bs((1, lh), lambda h: (0, h)),
                  _bs((None, lh, 4 * lh), lambda h: (h, 0, 0)),
                  ANY, _bs((8, 128), lambda h: (0, 0))],
        out_specs=[_bs((seq, lh), lambda h: (0, first_rec_block + h)),
                   _bs((seq + 2 * hal, lh), lambda h: (0, h)), _bs((seq + 2 * hal, lh), lambda h: (0, h))],
        out_shape=[jax.ShapeDtypeStruct((seq, d_model), BF16), jax.ShapeDtypeStruct((seq + 2 * hal, dl), F32),
                   jax.ShapeDtypeStruct((seq + 2 * hal, dl), F32)],
        scratch_shapes=[pltpu.VMEM((seq + 2 * hal, lh), F32), pltpu.VMEM((2, seq, lh), F32),
                        pltpu.VMEM((2, seq, lh), F32)] + [pltpu.VMEM((seq, lh), F32)] * 4,
        input_output_aliases={5: 0},
        compiler_params=_params(1),
    )(p3, p3, pack, conv_b, wcat, y_in, token)


def _lru_bwd(p3, dy, h0p, h1p, dproj_in, pack, conv_b, wcat, token, seq, d_model):
    dl = d_model // 2
    lh = dl // N_HEADS
    t = min(128, seq)
    n_chunks = seq // t
    seg = seq // 8
    hal = CONV_HALO
    first_rec_block = (d_model - dl) // lh
    tn_dims = (((0,), (0,)), ((), ()))
    nt_dims = (((1,), (1,)), ((), ()))

    def body(ur_ref, ug_ref, dy_ref, h0_ref, h1_ref, pk_ref, cb_ref, wcat_ref, tok_ref, din_ref,
             dproj_ref, dpk_ref, dcb_ref, dwcat_ref,
             upad, a_scr, dh_scr, g_scr, dxc_pad, hs0, hs1, ps0, ps1, dpr_ref, out_sems):
        del din_ref, tok_ref
        zeros = jnp.zeros((hal, lh), F32)
        for ref in (upad, dxc_pad):
            ref[0:hal, :] = zeros
            ref[hal + seq:hal + seq + hal, :] = zeros
        for n in range(2):
            a_scr[n, 0:hal, :] = zeros
            a_scr[n, hal + seq:hal + seq + hal, :] = zeros

        def fill(ci, _):
            r0 = pl.multiple_of(ci * t, t)
            upad[pl.ds(hal + r0, t), :] = ur_ref[pl.ds(r0, t), :]
            return 0

        lax.fori_loop(0, n_chunks, fill, 0)
        cb = cb_ref[...]
        lam = [pk_ref[pl.ds(8 + n, 1), :] for n in range(2)]
        sp = [_softplus(-lam[n]) for n in range(2)]

        def chunk1(ci, _):
            r0 = pl.multiple_of(ci * t, t)
            xc, _ext = _conv_chunk(upad, pk_ref, cb, r0, t)
            _, pre = _gate_preacts(xc, wcat_ref)
            for n in range(2):
                _, _, a, _ = _gates(pre, n, pk_ref, sp)
                a_scr[n, pl.ds(hal + r0, t), :] = a
            hsum = h0_ref[pl.ds(hal + r0, t), :] + h1_ref[pl.ds(hal + r0, t), :]
            gl, dgl = _gelu_and_grad(ug_ref[pl.ds(r0, t), :])
            dyv = dy_ref[pl.ds(r0, t), :]
            dh_scr[pl.ds(r0, t), :] = dyv * gl
            dpr_ref[1, pl.ds(r0, t), :] = ((dyv * hsum) * dgl).astype(BF16)
            return 0

        lax.fori_loop(0, n_chunks, chunk1, 0, unroll=2)

        def load(n):
            def get(pos):
                coef = a_scr[n, pl.ds(hal + pos + (1 if n == 0 else -1), 8, stride=seg), :]
                return coef, dh_scr[pl.ds(pos, 8, stride=seg), :]
            return get

        def store(n):
            def put(s, v):
                g_scr[n, pl.ds(s, 8, stride=seg), :] = v
            return put

        _segment_scan(seg, [load(1), load(0)], [hs0, hs1], [ps0, ps1], [store(1), store(0)])

        dwcat_ref[...] = jnp.zeros((lh, 4 * lh), F32)

        def chunk3(ci, carry):
            dba, dbi, dlam, dcb = carry
            r0 = pl.multiple_of(ci * t, t)
            xc, _ext = _conv_chunk(upad, pk_ref, cb, r0, t)
            xcb, pre = _gate_preacts(xc, wcat_ref)
            dxc = jnp.zeros((t, lh), F32)
            dba, dbi, dlam = list(dba), list(dbi), list(dlam)
            dpre = []
            for n in range(2):
                r, i, a, m = _gates(pre, n, pk_ref, sp)
                hext = (h0_ref if n == 0 else h1_ref)[pl.ds(r0, t + 2 * hal), :]
                hprev = _shift(hext, -1 if n == 0 else 1)[hal:hal + t, :]
                gb = g_scr[n, pl.ds(r0, t), :]
                da = gb * hprev
                dm = gb * i * xc
                di = gb * m * xc
                dxc = dxc + gb * (m * i)
                dlog_a = da * a - dm * (a * a) / m
                dr = dlog_a * (-RG_C * sp[n])
                dlam[n] = dlam[n] + jnp.sum(dlog_a * r, axis=0, keepdims=True)
                dpr = dr * r * (1.0 - r)
                dpi = di * i * (1.0 - i)
                dba[n] = dba[n] + jnp.sum(dpr, axis=0, keepdims=True)
                dbi[n] = dbi[n] + jnp.sum(dpi, axis=0, keepdims=True)
                dpre += [dpr.astype(BF16), dpi.astype(BF16)]
            dpre = jnp.concatenate(dpre, axis=1)
            dwcat_ref[...] += lax.dot_general(xcb, dpre, tn_dims, preferred_element_type=F32)
            dxc = dxc + lax.dot_general(dpre, wcat_ref[...], nt_dims, preferred_element_type=F32)
            dxc_pad[pl.ds(hal + r0, t), :] = dxc
            dcb = dcb + jnp.sum(dxc, axis=0, keepdims=True)
            return tuple(dba), tuple(dbi), tuple(dlam), dcb

        zr = jnp.zeros((1, lh), F32)
        def chunk3_pair(cj, carry):
            return chunk3(2 * cj + 1, chunk3(2 * cj, carry))

        dba, dbi, dlam, dcb = lax.fori_loop(0, n_chunks // 2, chunk3_pair, ((zr, zr), (zr, zr), (zr, zr), zr))
        dcb_ref[...] = dcb
        for n in range(2):
            dpk_ref[pl.ds(4 + n, 1), :] = dba[n]
            dpk_ref[pl.ds(6 + n, 1), :] = dbi[n]
            dpk_ref[pl.ds(8 + n, 1), :] = dlam[n] * (RG_C * jax.nn.sigmoid(-lam[n]))
        dpk_ref[pl.ds(10, SMALL_ROWS - 10), :] = jnp.zeros((SMALL_ROWS - 10, lh), F32)

        def chunk4(ci, dtap):
            r0 = pl.multiple_of(ci * t, t)
            gext = dxc_pad[pl.ds(r0, t + 2 * hal), :]
            uext = upad[pl.ds(r0, t + 2 * hal), :]
            gmid = gext[hal:hal + t, :]
            du = pk_ref[pl.ds(1, 1), :] * gext
            for k in (0, 2, 3):
                du = du + pk_ref[pl.ds(k, 1), :] * _shift(gext, 1 - k)
            dpr_ref[0, pl.ds(r0, t), :] = du[hal:hal + t, :].astype(BF16)
            out = []
            for k in range(4):
                usl = _shift(uext, k - 1)[hal:hal + t, :]
                out.append(dtap[k] + jnp.sum(gmid * usl, axis=0, keepdims=True))
            return tuple(out)

        dtap = lax.fori_loop(0, n_chunks, chunk4, (zr, zr, zr, zr))
        for k in range(4):
            dpk_ref[pl.ds(k, 1), :] = dtap[k]

        head = pl.program_id(0)
        outs = [pltpu.make_async_copy(
            dpr_ref.at[b], dproj_ref.at[:, pl.ds(pl.multiple_of((1 + b) * dl + head * lh, lh), lh)], out_sems.at[b])
            for b in range(2)]
        for cp in outs:
            cp.start()
        for cp in outs:
            cp.wait()

    return pl.pallas_call(
        body, name="lru_bwd", grid=(N_HEADS,),
        in_specs=[_bs((None, seq, lh), lambda h: (1, 0, h)), _bs((None, seq, lh), lambda h: (2, 0, h)),
                  _bs((seq, lh), lambda h: (0, first_rec_block + h)),
                  _bs((seq + 2 * hal, lh), lambda h: (0, h)), _bs((seq + 2 * hal, lh), lambda h: (0, h)),
                  _bs((None, SMALL_ROWS, lh), lambda h: (h, 0, 0)), _bs((1, lh), lambda h: (0, h)),
                  _bs((None, lh, 4 * lh), lambda h: (h, 0, 0)),
                  _bs((8, 128), lambda h: (0, 0)), ANY],
        out_specs=[ANY, _bs((None, SMALL_ROWS, lh), lambda h: (h, 0, 0)),
                   _bs((1, lh), lambda h: (0, h)), _bs((None, lh, 4 * lh), lambda h: (h, 0, 0))],
        out_shape=[jax.ShapeDtypeStruct((seq, 3 * dl), BF16), jax.ShapeDtypeStruct((N_HEADS, SMALL_ROWS, lh), F32),
                   jax.ShapeDtypeStruct((1, dl), F32), jax.ShapeDtypeStruct((N_HEADS, lh, 4 * lh), F32)],
        scratch_shapes=[pltpu.VMEM((seq + 2 * hal, lh), F32), pltpu.VMEM((2, seq + 2 * hal, lh), F32),
                        pltpu.VMEM((seq, lh), F32), pltpu.VMEM((2, seq, lh), F32),
                        pltpu.VMEM((seq + 2 * hal, lh), F32)] + [pltpu.VMEM((seq, lh), F32)] * 4
                       + [pltpu.VMEM((2, seq, lh), BF16), pltpu.SemaphoreType.DMA((2,))],
        input_output_aliases={9: 0},
        compiler_params=_params(1),
    )(p3, p3, dy, h0p, h1p, pack, conv_b, wcat, token, dproj_in)


class _tiles:
    def __init__(self, seq, d_model, d_ff):
        self.rows = min(1024, seq)
        self.ln_rows = min(256, seq)
        self.ff_cols = min(1024, d_ff)
        self.ff_split = 4
        self.ff_k = min(2048, d_ff)
        self.grad_rows = 512


def _ln_loss_bwd(ffn, x1, tgt, g, b, tr):
    seq, d = ffn.shape

    def body(f_ref, x_ref, t_ref, g_ref, b_ref, dz_ref, dzb_ref, dg_ref, db_ref, loss_ref):
        i = pl.program_id(0)
        gv = g_ref[...]
        z = ALPHA * x_ref[...] + f_ref[...]
        y, xhat, rstd = _ln_fwd(z, gv, b_ref[...])
        err = y - t_ref[...]
        part = 0.5 * jnp.sum(jnp.mean(err * err, axis=-1, keepdims=True), axis=0, keepdims=True)
        dz, dg, db = _ln_bwd(err * (1.0 / d), xhat, rstd, gv)
        dz_ref[...] = dz
        dzb_ref[...] = dz.astype(BF16)
        _acc_rows(dg_ref, i == 0, dg)
        _acc_rows(db_ref, i == 0, db)
        _acc_rows(loss_ref, i == 0, jnp.broadcast_to(part, (8, 128)))

    row = _bs((tr, d), lambda i: (i, 0))
    vec = _bs((1, d), lambda i: (0, 0))
    return pl.pallas_call(
        body, name="ln_ffn_loss", grid=(seq // tr,), in_specs=[row, row, row, vec, vec],
        out_specs=[row, row, vec, vec, _bs((8, 128), lambda i: (0, 0))],
        out_shape=[jax.ShapeDtypeStruct((seq, d), F32), jax.ShapeDtypeStruct((seq, d), BF16),
                   jax.ShapeDtypeStruct((1, d), F32), jax.ShapeDtypeStruct((1, d), F32),
                   jax.ShapeDtypeStruct((8, 128), F32)],
        compiler_params=_params(1),
    )(ffn, x1, tgt, g, b)


def _ln_bwd_rows(dx_branch, dres, z, g, b, tr):
    seq, d = z.shape

    def body(a_ref, r_ref, z_ref, g_ref, b_ref, dz_ref, dzb_ref, dg_ref, db_ref):
        i = pl.program_id(0)
        gv = g_ref[...]
        _, xhat, rstd = _ln_fwd(z_ref[...], gv, b_ref[...])
        dz, dg, db = _ln_bwd(ALPHA * r_ref[...] + a_ref[...], xhat, rstd, gv)
        dz_ref[...] = dz
        dzb_ref[...] = dz.astype(BF16)
        _acc_rows(dg_ref, i == 0, dg)
        _acc_rows(db_ref, i == 0, db)

    row = _bs((tr, d), lambda i: (i, 0))
    vec = _bs((1, d), lambda i: (0, 0))
    return pl.pallas_call(
        body, name="ln_mix_bwd", grid=(seq // tr,), in_specs=[row, row, row, vec, vec],
        out_specs=[row, row, vec, vec],
        out_shape=[jax.ShapeDtypeStruct((seq, d), F32), jax.ShapeDtypeStruct((seq, d), BF16),
                   jax.ShapeDtypeStruct((1, d), F32), jax.ShapeDtypeStruct((1, d), F32)],
        compiler_params=_params(1),
    )(dx_branch, dres, z, g, b)


def _adamw_values(w, g, m, v):
    m = ADAM_B1 * m + (1.0 - ADAM_B1) * g
    v = ADAM_B2 * v + (1.0 - ADAM_B2) * (g * g)
    m_hat = m / (1.0 - ADAM_B1 ** ADAM_STEP)
    v_hat = v / (1.0 - ADAM_B2 ** ADAM_STEP)
    delta = -ADAM_LR * (m_hat / (jnp.sqrt(v_hat) + ADAM_EPS) + ADAM_WD * w)
    return delta, m, v


def _sum_adamw(name, own, parts, w, m, v):
    rows, cols = w.shape
    n_parts = parts.shape[0]
    tr = rows
    min_rows = 8 if parts.dtype == F32 else 16
    while tr * cols * 4 > 1024 * 1024 and tr % (2 * min_rows) == 0:
        tr //= 2

    def body(*refs):
        if own is None:
            p_ref, w_ref, m_ref, v_ref, g_ref, d_ref, mo_ref, vo_ref = refs
            g = p_ref[0].astype(F32)
            rest = range(1, n_parts)
        else:
            o_ref, p_ref, w_ref, m_ref, v_ref, g_ref, d_ref, mo_ref, vo_ref = refs
            g = o_ref[...]
            rest = range(n_parts)
        for s in rest:
            g = g + p_ref[s].astype(F32)
        delta, mn, vn = _adamw_values(w_ref[...], g, m_ref[...], v_ref[...])
        g_ref[...] = g
        d_ref[...] = delta
        mo_ref[...] = mn
        vo_ref[...] = vn

    spec = _bs((tr, cols), lambda i: (i, 0))
    lead = [] if own is None else [own]
    return pl.pallas_call(
        body, name=name, grid=(rows // tr,),
        in_specs=[spec] * len(lead) + [_bs((n_parts, tr, cols), lambda i: (0, i, 0)), spec, spec, spec],
        out_specs=[spec] * 4, out_shape=[jax.ShapeDtypeStruct((rows, cols), F32)] * 4,
        compiler_params=_params(1),
    )(*lead, parts, w, m, v)


def _rows128(a):
    return a.reshape(-1, 128)


def kernel(x, ln_mix_g, ln_mix_b, w_in, w_pool, pool_scale, conv_w, conv_b, w_rg_a, b_rg_a, w_rg_i, b_rg_i, rg_lambda, w_out, ln_ffn_g, ln_ffn_b, w_mlp_in, w_mlp_out, loss_target, m_ln_mix_g, m_ln_mix_b, m_w_in, m_w_pool, m_pool_scale, m_conv_w, m_conv_b, m_w_rg_a, m_b_rg_a, m_w_rg_i, m_b_rg_i, m_rg_lambda, m_w_out, m_ln_ffn_g, m_ln_ffn_b, m_w_mlp_in, m_w_mlp_out, v_ln_mix_g, v_ln_mix_b, v_w_in, v_w_pool, v_pool_scale, v_conv_w, v_conv_b, v_w_rg_a, v_b_rg_a, v_w_rg_i, v_b_rg_i, v_rg_lambda, v_w_out, v_ln_ffn_g, v_ln_ffn_b, v_w_mlp_in, v_w_mlp_out):
    seq, d_model = x.shape[1], x.shape[2]
    dh = d_model // 2
    lh = dh // N_HEADS
    pg = dh // len(POOL_WINDOWS)
    d_ff = w_mlp_in.shape[2] * N_DEV
    assert lh == 128 and conv_w.shape[3] == lh and w_pool.shape[2] * N_DEV == pg

    xs = x[0]
    tgt = loss_target[0]

    def small_pack(cw, ba, bi, lam):
        return jnp.concatenate([cw.reshape(4, lh), ba.reshape(2, lh), bi.reshape(2, lh), lam.reshape(2, lh),
                                jnp.zeros((SMALL_ROWS - 10, lh), F32)], axis=0)

    pack_mine = small_pack(conv_w, b_rg_a, b_rg_i, rg_lambda)
    win_full, wpool_full, pack_full = _all_gather("gather_mixer", [
        (w_in[0].astype(BF16), 1), (w_pool[0].astype(BF16), 1), (pack_mine[None], 0)])
    wout_gather = _SplitGather("gather_w_out", [(w_out[0], 0)], BF16, after=pack_full)
    w1_gather = _SplitGather("gather_w_mlp_in", [(w_mlp_in[0], 1)], BF16, after=wout_gather.token)
    w2_gather = _SplitGather("gather_w_mlp_out", [(w_mlp_out[0], 0)], BF16, after=w1_gather.token)
    wcat = jnp.concatenate([w_rg_a[0, 0], w_rg_i[0, 0], w_rg_a[0, 1], w_rg_i[0, 1]], axis=-1).astype(BF16)
    vec = lambda i, j, k: (0, 0)
    row_full = lambda i, j, k: (i, 0)

    def after(token):
        return (token, _sp((8, 128), vec))

    def sds(shape, dtype):
        return jax.ShapeDtypeStruct(shape, dtype)

    def plain_epi(acc, i, ex, out):
        out[0][...] = acc

    def bf16_epi(acc, i, ex, out):
        out[0][...] = acc.astype(BF16)

    t = _tiles(seq, d_model, d_ff)

    (p3,) = _matmul(
        "proj", xs, win_full, _sp((t.rows, d_model), lambda i, j, k: (i, 0)), _sp((d_model, dh), lambda i, j, k: (0, j)),
        grid=(seq // t.rows, 3, 1), extras=[after(w2_gather.token)],
        out_shape=[sds((3, seq, dh), F32)], out_specs=[_sp((None, t.rows, dh), lambda i, j, k: (j, i, 0))],
        epilogue=plain_epi)

    d_pool, y_half = _pool_fwd(p3, wpool_full, pool_scale, seq, d_model)
    y, h0p, h1p = _lru_fwd(p3, y_half, pack_full, conv_b, wcat, wout_gather.relay(after=y_half), seq, d_model)
    (wout_full,) = wout_gather.wait(after=y)
    relay_token = w1_gather.relay(after=wout_full)

    def mix_epi(acc, i, ex, out):
        x_ref, g_ref, b_ref = ex[:3]
        z = ALPHA * x_ref[...] + acc
        x1, _, _ = _ln_fwd(z, g_ref[...], b_ref[...])
        out[0][...] = z
        out[1][...] = x1
        out[2][...] = x1.astype(BF16)

    z1, x1, x1b = _matmul(
        "mix_out", y, wout_full, _sp((t.ln_rows, d_model), row_full), _sp((d_model, d_model), vec, single=True),
        grid=(seq // t.ln_rows, 1, 1),
        extras=[(xs, _sp((t.ln_rows, d_model), row_full)), (ln_mix_g, _sp((1, d_model), vec)),
                (ln_mix_b, _sp((1, d_model), vec)), after(relay_token)],
        out_shape=[sds((seq, d_model), F32), sds((seq, d_model), F32), sds((seq, d_model), BF16)],
        out_specs=[_sp((t.ln_rows, d_model), row_full)] * 3, epilogue=mix_epi)
    (w1_full,) = w1_gather.wait(after=x1b)
    relay_token = w2_gather.relay(after=w1_full)

    def mlp_in_epi(acc, i, ex, out, cols):
        h = jnp.maximum(acc, 0.0)
        out[0][:, cols] = (h * h).astype(BF16)

    (hmid,) = _matmul(
        "mlp_in", x1b, w1_full, _sp((t.rows, d_model), lambda i, j, k: (i, 0)),
        _sp((d_model, t.ff_cols), lambda i, j, k: (0, j)),
        grid=(seq // t.rows, d_ff // t.ff_cols, 1), j_outer=True, extras=[after(relay_token)],
        out_shape=[sds((seq, d_ff), BF16)], out_specs=[_sp((t.rows, t.ff_cols), lambda i, j, k: (i, j))],
        epilogue=mlp_in_epi, n_split=t.ff_split)
    (w2_full,) = w2_gather.wait(after=hmid)

    (ffn,) = _matmul(
        "mlp_out", hmid, w2_full, _sp((t.rows, t.ff_k), lambda i, j, k: (i, k)),
        _sp((t.ff_k, d_model), lambda i, j, k: (k, 0)),
        grid=(seq // t.rows, 1, d_ff // t.ff_k),
        out_shape=[sds((seq, d_model), F32)], out_specs=[_sp((t.rows, d_model), row_full)])
    dz2, dz2b, g_ffn_g, g_ffn_b, loss_part = _ln_loss_bwd(ffn, x1, tgt, ln_ffn_g, ln_ffn_b, t.ln_rows)

    def dpre_epi(acc, i, ex, out, cols):
        out[0][:, cols] = (acc * (2.0 * jnp.sqrt(ex[0][:, cols].astype(F32)))).astype(BF16)

    (dpre,) = _matmul(
        "mlp_dpre", dz2b, w2_full, _sp((t.rows, d_model), lambda i, j, k: (i, 0)),
        _sp((t.ff_cols, d_model), lambda i, j, k: (j, 0)),
        grid=(seq // t.rows, d_ff // t.ff_cols, 1), j_outer=True, tb=True,
        extras=[(hmid, _sp((t.rows, t.ff_cols), lambda i, j, k: (i, j)))],
        out_shape=[sds((seq, d_ff), BF16)], out_specs=[_sp((t.rows, t.ff_cols), lambda i, j, k: (i, j))],
        epilogue=dpre_epi, n_split=t.ff_split)

    (dx1_mlp,) = _matmul(
        "mlp_dx", dpre, w1_full, _sp((t.rows, t.ff_k), lambda i, j, k: (i, k)),
        _sp((d_model, t.ff_k), lambda i, j, k: (0, k)),
        grid=(seq // t.rows, 1, d_ff // t.ff_k), tb=True,
        out_shape=[sds((seq, d_model), F32)], out_specs=[_sp((t.rows, d_model), row_full)])
    dz1, dz1b, g_mix_g, g_mix_b = _ln_bwd_rows(dx1_mlp, dz2, z1, ln_mix_g, ln_mix_b, t.ln_rows)

    (g_w2,) = _matmul(
        "grad_w_mlp_out", hmid, dz2b, _sp((seq, t.grad_rows), lambda i, j, k: (0, i)),
        _sp((seq, d_model), vec, single=True),
        grid=(d_ff // t.grad_rows, 1, 1), ta=True,
        out_shape=[sds((d_ff, d_model), BF16)], out_specs=[_sp((t.grad_rows, d_model), row_full)],
        epilogue=bf16_epi)
    g_w2 = g_w2.reshape(N_DEV, d_ff // N_DEV, d_model)

    def block_epi(acc, i, ex, out):
        out[0][0] = acc.astype(BF16)

    fs = d_ff // N_DEV
    (g_w1,) = _matmul(
        "grad_w_mlp_in", x1b, dpre, _sp((seq, t.grad_rows), lambda i, j, k: (0, i)),
        _sp((seq, fs), lambda i, j, k: (0, j)),
        grid=(d_model // t.grad_rows, N_DEV, 1), j_outer=True, ta=True,
        out_shape=[sds((N_DEV, d_model, fs), BF16)],
        out_specs=[_sp((1, t.grad_rows, fs), lambda i, j, k: (j, i, 0))], epilogue=block_epi)

    (dy,) = _matmul(
        "mix_dy", dz1b, wout_full, _sp((t.rows, d_model), lambda i, j, k: (i, 0)),
        _sp((dh, d_model), lambda i, j, k: (j, 0)),
        grid=(seq // t.rows, 2, 1), j_outer=True, tb=True,
        out_shape=[sds((seq, d_model), F32)], out_specs=[_sp((t.rows, dh), lambda i, j, k: (i, j))],
        epilogue=plain_epi)
    (g_wout,) = _matmul(
        "grad_w_out", y, dz1b, _sp((seq, t.grad_rows), lambda i, j, k: (0, i)), _sp((seq, d_model), vec, single=True),
        grid=(d_model // t.grad_rows, 1, 1), ta=True,
        out_shape=[sds((d_model, d_model), BF16)], out_specs=[_sp((t.grad_rows, d_model), row_full)],
        epilogue=bf16_epi)
    g_wout = g_wout.reshape(N_DEV, d_model // N_DEV, d_model)

    scatter_a = _SplitReduceScatter("scatter_a", [g_w1, g_w2, g_wout])

    dproj_pool, g_wpool, g_pscale = _pool_bwd(d_pool, dy, wpool_full, pool_scale, scatter_a.token, seq, d_model)
    token_a = scatter_a.combine_and_send(after=dproj_pool)
    dproj, g_pack, g_convb, g_wcat = _lru_bwd(p3, dy, h0p, h1p, dproj_pool, pack_full, conv_b, wcat,
                                              token_a, seq, d_model)
    g_wa = jnp.stack([g_wcat[:, :, 0:lh], g_wcat[:, :, 2 * lh:3 * lh]])
    g_wi = jnp.stack([g_wcat[:, :, lh:2 * lh], g_wcat[:, :, 3 * lh:4 * lh]])

    rep_parts = [_rows128(g_wa), _rows128(g_wi), _rows128(g_mix_g), _rows128(g_mix_b), _rows128(g_ffn_g),
                 _rows128(g_ffn_b), _rows128(g_pscale), _rows128(g_convb)]
    rep_rows = [p.shape[0] for p in rep_parts]
    n_rep = sum(rep_rows)
    small = jnp.concatenate(rep_parts + [_rows128(g_pack)], axis=0)
    small_gather = _SplitGather("gather_small_grads", [(small[None], 0)], F32, after=small)

    ws = 3 * dh // N_DEV

    def pair_epi(acc, i, ex, out):
        out[0][0] = acc[:, :ws].astype(BF16)
        out[0][1] = acc[:, ws:].astype(BF16)

    (g_win,) = _matmul(
        "grad_w_in", xs, dproj, _sp((seq, t.grad_rows), lambda i, j, k: (0, i)),
        _sp((seq, 2 * ws), lambda i, j, k: (0, j)),
        grid=(d_model // t.grad_rows, N_DEV // 2, 1), ta=True, extras=[after(small_gather.token)],
        out_shape=[sds((N_DEV, d_model, ws), BF16)],
        out_specs=[_sp((2, t.grad_rows, ws), lambda i, j, k: (j, i, 0))], epilogue=pair_epi)
    scatter_b = _SplitReduceScatter("scatter_b", [g_win, g_wpool.reshape(N_DEV, pg // N_DEV * len(POOL_WINDOWS), pg)])

    def adam_big(name, own_landed, w, m, v):
        own, landed = own_landed
        shp = w.shape
        two = lambda a: a.reshape(-1, shp[-1])
        res = _sum_adamw(name, own, landed, two(w), two(m), two(v))
        return [r.reshape(shp) for r in res]

    r_w1, r_w2, r_wout = scatter_a.wait(after=scatter_b.token)
    o_w1 = adam_big("adam_w_mlp_in", r_w1, w_mlp_in, m_w_mlp_in, v_w_mlp_in)
    token_b = scatter_b.combine_and_send(after=o_w1[0])

    def dx_epi(acc, i, ex, out):
        out[0][...] = ALPHA * ex[0][...] + acc

    (dx,) = _matmul(
        "grad_x", dproj, win_full, _sp((t.ln_rows * 2, 3 * dh), lambda i, j, k: (i, 0)),
        _sp((d_model, 3 * dh), vec, single=True),
        grid=(seq // (t.ln_rows * 2), 1, 1), tb=True,
        extras=[(dz1, _sp((t.ln_rows * 2, d_model), row_full)), after(token_b)],
        out_shape=[sds((seq, d_model), F32)], out_specs=[_sp((t.ln_rows * 2, d_model), row_full)],
        epilogue=dx_epi)
    o_w2 = adam_big("adam_w_mlp_out", r_w2, w_mlp_out, m_w_mlp_out, v_w_mlp_out)
    o_wout = adam_big("adam_w_out", r_wout, w_out, m_w_out, v_w_out)
    r_win, r_wpool = scatter_b.wait(after=dx)
    o_win = adam_big("adam_w_in", r_win, w_in, m_w_in, v_w_in)
    o_wpool = adam_big("adam_w_pool", r_wpool, w_pool, m_w_pool, v_w_pool)

    small_gather.relay(after=o_win[0])
    (small_all,) = small_gather.wait(after=o_wpool[0])

    rep_w = [w_rg_a, w_rg_i, ln_mix_g, ln_mix_b, ln_ffn_g, ln_ffn_b, pool_scale, conv_b]
    rep_m = [m_w_rg_a, m_w_rg_i, m_ln_mix_g, m_ln_mix_b, m_ln_ffn_g, m_ln_ffn_b, m_pool_scale, m_conv_b]
    rep_v = [v_w_rg_a, v_w_rg_i, v_ln_mix_g, v_ln_mix_b, v_ln_ffn_g, v_ln_ffn_b, v_pool_scale, v_conv_b]
    cat = lambda arrs: jnp.concatenate([_rows128(a) for a in arrs], axis=0)
    o_rep = _sum_adamw("adam_replicated", None, small_all, cat(rep_w), cat(rep_m), cat(rep_v))

    my_idx = _dev_index(_where_am_i())
    head_parts = lax.dynamic_slice_in_dim(small_all, n_rep + my_idx * SMALL_ROWS, SMALL_ROWS, axis=1)
    o_head = _sum_adamw("adam_head", None, head_parts, pack_mine,
                        small_pack(m_conv_w, m_b_rg_a, m_b_rg_i, m_rg_lambda),
                        small_pack(v_conv_w, v_b_rg_a, v_b_rg_i, v_rg_lambda))

    def unpack_rep(packed):
        out, r = [], 0
        for wgt, rows in zip(rep_w, rep_rows):
            out.append(packed[r:r + rows].reshape(wgt.shape))
            r += rows
        return out

    def unpack_head(packed):
        return [packed[0:4].reshape(conv_w.shape), packed[4:6].reshape(b_rg_a.shape),
                packed[6:8].reshape(b_rg_i.shape), packed[8:10].reshape(rg_lambda.shape)]

    loss = lax.psum(loss_part[0, 0], ("x", "y", "c"))

    outs = [loss, dx[None]]
    for kind in range(4):
        ra, ri, mg, mb, fg, fb, ps, cb = unpack_rep(o_rep[kind])
        cw, ba, bi, lam = unpack_head(o_head[kind])
        outs += [mg, mb, o_win[kind], o_wpool[kind], ps, cw, cb, ra, ba, ri, bi, lam, o_wout[kind], fg, fb,
                 o_w1[kind], o_w2[kind]]
    return tuple(outs)
```

```python
import functools

import jax
import jax.numpy as jnp
from jax import lax
from jax.experimental import pallas as pl
from jax.experimental.pallas import tpu as pltpu

F32 = jnp.float32
BF16 = jnp.bfloat16
MESH = pl.DeviceIdType.MESH
ANY = pl.BlockSpec(memory_space=pl.ANY)

N_DEV = 8
POOL_WINDOWS = (2, 4, 8, 16)
N_HEADS = 8
RG_C = 8.0
LN_EPS = 1e-5
ALPHA = 2.0 ** 0.25
ADAM_LR = 0.001
ADAM_B1 = 0.9
ADAM_B2 = 0.999
ADAM_EPS = 1e-08
ADAM_WD = 0.01
ADAM_STEP = 10

VMEM_LIMIT = 56 * 1024 * 1024
WIN_HALO = 16
CONV_HALO = 8
SMALL_ROWS = 16


def _params(n_grid):
    return pltpu.CompilerParams(dimension_semantics=("arbitrary",) * n_grid, vmem_limit_bytes=VMEM_LIMIT)


def _shift(v, j):
    n = v.shape[0]
    s = (-j) % n
    return v if s == 0 else pltpu.roll(v, s, 0)


def _sigmoid(x):
    return 0.5 * jnp.tanh(0.5 * x) + 0.5


def _softplus(z):
    e = jnp.exp(-jnp.abs(z))
    u = 1.0 + e
    log1p = jnp.where(u == 1.0, e, jnp.log(u) * (e / jnp.where(u == 1.0, 1.0, u - 1.0)))
    return jnp.maximum(z, 0.0) + log1p


_GELU_C = 0.7978845608028654
_GELU_K = 0.044715


def _gelu_and_grad(x):
    x2 = x * x
    t = jnp.tanh(_GELU_C * (x + _GELU_K * x * x2))
    g = 0.5 * x * (1.0 + t)
    dg = 0.5 * (1.0 + t) + 0.5 * x * (1.0 - t * t) * (_GELU_C * (1.0 + 3.0 * _GELU_K * x2))
    return g, dg


def _ln_fwd(z, g, b):
    mu = jnp.mean(z, axis=-1, keepdims=True)
    zc = z - mu
    var = jnp.mean(zc * zc, axis=-1, keepdims=True)
    rstd = lax.rsqrt(var + LN_EPS)
    xhat = zc * rstd
    return xhat * g + b, xhat, rstd


def _ln_bwd(dy, xhat, rstd, g):
    dxhat = dy * g
    m1 = jnp.mean(dxhat, axis=-1, keepdims=True)
    m2 = jnp.mean(dxhat * xhat, axis=-1, keepdims=True)
    dz = rstd * (dxhat - m1 - xhat * m2)
    dg = jnp.sum(dy * xhat, axis=0, keepdims=True)
    db = jnp.sum(dy, axis=0, keepdims=True)
    return dz, dg, db


def _acc_rows(ref, first, val):
    @pl.when(first)
    def _():
        ref[...] = val

    @pl.when(jnp.logical_not(first))
    def _():
        ref[...] += val


def _sp(shape, fn, single=False):
    return shape, fn, single


def _matmul(name, a, b, a_spec, b_spec, *, grid, j_outer=False, ta=False, tb=False, extras=(), out_shape, out_specs,
            epilogue=None, n_split=1):
    ni, nj, nk = grid
    n_ex = len(extras)
    dims = (((0 if ta else 1,), (1 if tb else 0,)), ((), ()))

    def mk(spec):
        shape, fn, single = spec
        index = (lambda g0, g1, g2: fn(g1, g0, g2)) if j_outer else fn
        return pl.BlockSpec(shape, index, pipeline_mode=pl.Buffered(1)) if single else pl.BlockSpec(shape, index)

    def body(a_ref, b_ref, *rest):
        ex_refs = rest[:n_ex]
        out_refs = rest[n_ex:]
        i = pl.program_id(1 if j_outer else 0)
        if n_split > 1:
            av = a_ref[...].astype(BF16)
            width = b_ref.shape[0 if tb else 1] // n_split
            for c in range(n_split):
                cols = pl.ds(c * width, width)
                bv = (b_ref[cols, :] if tb else b_ref[:, cols]).astype(BF16)
                epilogue(lax.dot_general(av, bv, dims, preferred_element_type=F32), i, ex_refs, out_refs, cols)
            return
        part = lax.dot_general(a_ref[...].astype(BF16), b_ref[...].astype(BF16), dims, preferred_element_type=F32)
        if nk == 1:
            epilogue(part, i, ex_refs, out_refs)
        else:
            @pl.when(pl.program_id(2) == 0)
            def _():
                out_refs[0][...] = jnp.zeros(out_refs[0].shape, F32)

            out_refs[0][...] += part

    return pl.pallas_call(
        body, name=name, grid=(nj, ni, nk) if j_outer else (ni, nj, nk),
        in_specs=[mk(a_spec), mk(b_spec)] + [mk(s) for _, s in extras],
        out_specs=[mk(s) for s in out_specs], out_shape=list(out_shape),
        compiler_params=_params(3),
    )(a, b, *[x for x, _ in extras])


def _bs(shape, fn):
    return pl.BlockSpec(shape, fn)


def _where_am_i():
    x, y, c = lax.axis_index("x"), lax.axis_index("y"), lax.axis_index("c")
    return x, y, c


def _dev_index(p):
    return 4 * p[0] + 2 * p[1] + p[2]


def _slab(ref, axis, idx, size):
    sl = [slice(None)] * len(ref.shape)
    sl[axis] = pl.ds(idx * size, size)
    return ref.at[tuple(sl)]


def _all_gather(name, items):
    n = len(items)
    shapes = []
    for shard, axis in items:
        s = list(shard.shape)
        s[axis] *= N_DEV
        shapes.append(jax.ShapeDtypeStruct(tuple(s), shard.dtype))

    def body(*refs):
        in_refs, out_refs = refs[:n], refs[n:2 * n]
        send_sems, recv_sems, local_sems = refs[2 * n:]
        x, y, c = _where_am_i()
        me, sibling = (x, y, c), (x, y, 1 - c)
        chips = [(1 - x, y), (x, 1 - y), (1 - x, 1 - y)]

        def blk(a, p):
            axis = items[a][1]
            return _slab(out_refs[a], axis, _dev_index(p), items[a][0].shape[axis])

        def copy(a, k, block, to, src=None):
            return pltpu.make_async_remote_copy(
                src_ref=blk(a, block) if src is None else src, dst_ref=blk(a, block),
                send_sem=send_sems.at[a, k], recv_sem=recv_sems.at[a, k], device_id=to, device_id_type=MESH)

        mine = [pltpu.make_async_copy(in_refs[a], blk(a, me), local_sems.at[a]) for a in range(n)]
        for cp in mine:
            cp.start()
        first = []
        for a in range(n):
            first.append(copy(a, 0, me, sibling, src=in_refs[a]))
            first += [copy(a, 1 + j, me, (*chip, c), src=in_refs[a]) for j, chip in enumerate(chips)]
        for cp in first:
            cp.start()
        passed = []
        for a in range(n):
            for j, chip in enumerate(chips):
                copy(a, 1 + j, (*chip, c), me).wait_recv()
                fw = copy(a, 4 + j, (*chip, c), sibling)
                fw.start()
                passed.append(fw)
        for a in range(n):
            copy(a, 0, sibling, me).wait_recv()
            for j, chip in enumerate(chips):
                copy(a, 4 + j, (*chip, 1 - c), me).wait_recv()
        for cp in first + passed:
            cp.wait_send()
        for cp in mine:
            cp.wait()

    outs = pl.pallas_call(
        body, name=name, out_shape=shapes, in_specs=[ANY] * n, out_specs=[ANY] * n,
        scratch_shapes=[pltpu.SemaphoreType.DMA((n, 7)), pltpu.SemaphoreType.DMA((n, 7)),
                        pltpu.SemaphoreType.DMA((n,))],
    )(*[s for s, _ in items])
    return list(outs)


HBM = pl.BlockSpec(memory_space=pltpu.HBM)
SEM = pl.BlockSpec(memory_space=pltpu.SEMAPHORE)
DATAFLOW = pltpu.SideEffectType.DATAFLOW_SIDE_EFFECTING


def _in_hbm(a):
    return pltpu.with_memory_space_constraint(a, pltpu.HBM)


def _token_shape():
    return jax.ShapeDtypeStruct((8, 128), F32)


def _split_start(name, n_sems, bufs, issue):
    nb = len(bufs)

    def body(*refs):
        issue(refs[:nb], refs[nb], refs[nb + 1])
        refs[-1][...] = jnp.zeros((8, 128), F32)

    outs = pl.pallas_call(
        body, name=name,
        out_shape=(pltpu.SemaphoreType.DMA((n_sems,)), pltpu.SemaphoreType.DMA((n_sems,)),
                   *[pltpu.HBM(b.shape, b.dtype) for b in bufs], _token_shape()),
        in_specs=[HBM] * nb, out_specs=(SEM, SEM, *[HBM] * nb, pl.BlockSpec(memory_space=pltpu.VMEM)),
        input_output_aliases={i: 2 + i for i in range(nb)},
        compiler_params=pltpu.CompilerParams(has_side_effects=DATAFLOW),
    )(*[_in_hbm(b) for b in bufs])
    return outs[0], outs[1], list(outs[2:2 + nb]), outs[-1]


def _split_relay(name, n_sems, sems, bufs, after, relay):
    nb = len(bufs)

    def body(*refs):
        relay(refs[:nb], refs[nb], refs[nb + 1], refs[nb + 3], refs[nb + 4])
        refs[-1][...] = jnp.zeros((8, 128), F32)

    outs = pl.pallas_call(
        body, name=name,
        out_shape=(pltpu.SemaphoreType.DMA((n_sems,)), pltpu.SemaphoreType.DMA((n_sems,)),
                   *[pltpu.HBM(b.shape, b.dtype) for b in bufs], _token_shape()),
        in_specs=[HBM] * nb + [SEM, SEM, ANY],
        out_specs=(SEM, SEM, *[HBM] * nb, pl.BlockSpec(memory_space=pltpu.VMEM)),
        input_output_aliases={i: 2 + i for i in range(nb)},
        compiler_params=pltpu.CompilerParams(has_side_effects=DATAFLOW),
    )(*bufs, sems[0], sems[1], after)
    return outs[0], outs[1], list(outs[2:2 + nb]), outs[-1]


def _split_wait(name, sems, bufs, after, finish):
    nb = len(bufs)

    def body(*refs):
        finish(refs[:nb], refs[nb], refs[nb + 1])

    outs = pl.pallas_call(
        body, name=name, out_shape=[pltpu.HBM(b.shape, b.dtype) for b in bufs],
        in_specs=[HBM] * nb + [SEM, SEM, ANY], out_specs=[HBM] * nb,
        input_output_aliases={i: i for i in range(nb)},
        compiler_params=pltpu.CompilerParams(has_side_effects=DATAFLOW),
    )(*bufs, sems[0], sems[1], after)
    return list(outs)


def _place(name, items, dtype, after):
    ids = jnp.reshape(_dev_index(_where_am_i()), (1,)).astype(jnp.int32)
    outs = []
    for a, (shard, axis) in enumerate(items):
        rows, cols = shard.shape[-2], shard.shape[-1]
        tr = rows
        while tr * cols * shard.dtype.itemsize > 4 * 1024 * 1024 and tr % 32 == 0:
            tr //= 2
        nt = rows // tr
        full = list(shard.shape)
        full[axis] *= N_DEV
        if shard.ndim == 2 and axis == 0:
            in_spec = _bs((tr, cols), lambda i, ids: (i, 0))
            out_spec = _bs((tr, cols), lambda i, ids, nt=nt: (ids[0] * nt + i, 0))
        elif shard.ndim == 2 and axis == 1:
            in_spec = _bs((tr, cols), lambda i, ids: (i, 0))
            out_spec = _bs((tr, cols), lambda i, ids: (i, ids[0]))
        else:
            assert shard.ndim == 3 and axis == 0 and shard.shape[0] == 1
            in_spec = _bs((None, tr, cols), lambda i, ids: (0, i, 0))
            out_spec = _bs((None, tr, cols), lambda i, ids: (ids[0], i, 0))

        def body(ids_ref, in_ref, after_ref, out_ref):
            del ids_ref, after_ref
            out_ref[...] = in_ref[...].astype(out_ref.dtype)

        outs.append(pl.pallas_call(
            body, name=f"{name}{a}",
            grid_spec=pltpu.PrefetchScalarGridSpec(
                num_scalar_prefetch=1, grid=(nt,), in_specs=[in_spec, ANY], out_specs=out_spec),
            out_shape=jax.ShapeDtypeStruct(tuple(full), dtype), compiler_params=_params(1),
        )(ids, shard, after))
    return outs


class _SplitGather:
    def __init__(self, name, items, dtype, after):
        self.name, self.items, self.n = name, items, len(items)
        fulls = _place(name + "_place", items, dtype, after)
        n = self.n

        def issue(refs, send, recv):
            me, sibling, chips, c = self._geometry()
            for a in range(n):
                self._copy1(refs, send, recv, a, 0, me, sibling).start()
                for j, chip in enumerate(chips):
                    self._copy1(refs, send, recv, a, 1 + j, me, (*chip, c)).start()

        self.send, self.recv, self.bufs, self.token = _split_start(name + "_start", 4 * n, fulls, issue)

    @staticmethod
    def _geometry():
        x, y, c = _where_am_i()
        return (x, y, c), (x, y, 1 - c), [(1 - x, y), (x, 1 - y), (1 - x, 1 - y)], c

    def _blk(self, refs, a, p):
        shard, axis = self.items[a]
        return _slab(refs[a], axis, _dev_index(p), shard.shape[axis])

    def _copy1(self, refs, send, recv, a, k, owner, to):
        return pltpu.make_async_remote_copy(
            src_ref=self._blk(refs, a, owner), dst_ref=self._blk(refs, a, owner), send_sem=send.at[4 * a + k],
            recv_sem=recv.at[4 * a + k], device_id=to, device_id_type=MESH)

    def _copy2(self, refs, send, recv, a, j, owner, to):
        return pltpu.make_async_remote_copy(
            src_ref=self._blk(refs, a, owner), dst_ref=self._blk(refs, a, owner), send_sem=send.at[3 * a + j],
            recv_sem=recv.at[3 * a + j], device_id=to, device_id_type=MESH)

    def relay(self, after):
        n = self.n

        def relay(refs, send_in, recv_in, send_out, recv_out):
            me, sibling, chips, c = self._geometry()
            for a in range(n):
                for j, chip in enumerate(chips):
                    self._copy1(refs, send_in, recv_in, a, 1 + j, (*chip, c), me).wait_recv()
                    self._copy2(refs, send_out, recv_out, a, j, (*chip, c), sibling).start()
            for a in range(n):
                self._copy1(refs, send_in, recv_in, a, 0, sibling, me).wait_recv()
                for k in range(4):
                    self._copy1(refs, send_in, recv_in, a, k, me, sibling).wait_send()

        self.send, self.recv, self.bufs, self.token = _split_relay(
            self.name + "_relay", 3 * n, (self.send, self.recv), self.bufs, after, relay)
        return self.token

    def wait(self, after):
        n = self.n

        def finish(refs, send, recv):
            me, sibling, chips, c = self._geometry()
            for a in range(n):
                for j, chip in enumerate(chips):
                    self._copy2(refs, send, recv, a, j, (*chip, 1 - c), me).wait_recv()
                    self._copy2(refs, send, recv, a, j, (*chip, c), sibling).wait_send()

        return _split_wait(self.name + "_wait", (self.send, self.recv), self.bufs, after, finish)


class _SplitReduceScatter:
    def __init__(self, name, grads):
        self.name, self.n = name, len(grads)
        n = self.n
        g4 = [g.reshape(4, 2, *g.shape[1:]) for g in grads]
        land = [lax.empty((4, 1, *g.shape[1:]), g.dtype) for g in grads]

        def issue(refs, send, recv):
            for a in range(n):
                self._swap(refs, send, recv, a).start()

        self.send, self.recv, self.bufs, self.token = _split_start(name + "_d2d_start", n, g4 + land, issue)

    def _swap(self, refs, send, recv, a):
        x, y, c = _where_am_i()
        return pltpu.make_async_remote_copy(
            src_ref=refs[a].at[:, pl.ds(1 - c, 1)], dst_ref=refs[self.n + a], send_sem=send.at[a], recv_sem=recv.at[a],
            device_id=(x, y, 1 - c), device_id_type=MESH)

    def _hop(self, refs, send, recv, a, m):
        x, y, c = _where_am_i()
        px = (1 - x) if m & 2 else x
        py = (1 - y) if m & 1 else y
        return pltpu.make_async_remote_copy(
            src_ref=refs[a].at[2 * px + py], dst_ref=refs[self.n + a].at[m - 1], send_sem=send.at[3 * a + m - 1],
            recv_sem=recv.at[3 * a + m - 1], device_id=(px, py, c), device_id_type=MESH)

    def combine_and_send(self, after):
        n = self.n

        def finish(refs, send, recv):
            for a in range(n):
                self._swap(refs, send, recv, a).wait()

        bufs = _split_wait(self.name + "_d2d_wait", (self.send, self.recv), self.bufs, after, finish)
        x, y, c = _where_am_i()
        ids = jnp.stack([c, 2 * x + y]).astype(jnp.int32)
        self.own, sums = [], []
        for a in range(n):
            own, hb = _pair_sum(f"{self.name}_sum{a}", bufs[a], bufs[n + a], ids)
            self.own.append(own)
            sums.append(hb)
        land = [lax.empty((3, *h.shape[1:]), h.dtype) for h in sums]

        def issue(refs, send, recv):
            for a in range(n):
                for m in (1, 2, 3):
                    self._hop(refs, send, recv, a, m).start()

        self.send, self.recv, self.bufs, self.token = _split_start(self.name + "_ici_start", 3 * n, sums + land, issue)
        return self.token

    def wait(self, after):
        n = self.n

        def finish(refs, send, recv):
            for a in range(n):
                for m in (1, 2, 3):
                    self._hop(refs, send, recv, a, m).wait()

        bufs = _split_wait(self.name + "_ici_wait", (self.send, self.recv), self.bufs, after, finish)
        return list(zip(self.own, bufs[n:]))


def _pair_sum(name, g4, land, ids):
    rows, cols = g4.shape[2], g4.shape[3]
    tr = rows
    while tr * cols * 2 > 1024 * 1024 and tr % 32 == 0:
        tr //= 2

    def body(ids_ref, g_ref, l_ref, own_ref, sum_ref):
        h = g_ref[...].astype(F32) + l_ref[...].astype(F32)
        sum_ref[...] = h.astype(sum_ref.dtype)

        @pl.when(pl.program_id(1) == ids_ref[1])
        def _():
            own_ref[...] = h

    return pl.pallas_call(
        body, name=name,
        grid_spec=pltpu.PrefetchScalarGridSpec(
            num_scalar_prefetch=1, grid=(rows // tr, 4),
            in_specs=[_bs((None, None, tr, cols), lambda i, q, ids: (q, ids[0], i, 0)),
                      _bs((None, None, tr, cols), lambda i, q, ids: (q, 0, i, 0))],
            out_specs=[_bs((tr, cols), lambda i, q, ids: (i, 0)), _bs((None, tr, cols), lambda i, q, ids: (q, i, 0))]),
        out_shape=[jax.ShapeDtypeStruct((rows, cols), F32), jax.ShapeDtypeStruct((4, rows, cols), g4.dtype)],
        compiler_params=_params(2),
    )(ids, g4, land)


def _win_sum(ext, w, off):
    s = ext + _shift(ext, -1)
    if w >= 4:
        s = _shift(s, -1) + _shift(s, 1)
    if w >= 8:
        s = _shift(s, -2) + _shift(s, 2)
    if w >= 16:
        s = _shift(s, -4) + _shift(s, 4)
    return _shift(s, off) if off else s


def _inv_count(r0, t, w, seq):
    pos = r0 + lax.broadcasted_iota(jnp.int32, (t, 1), 0)
    cnt = jnp.minimum(pos + w // 2, seq) - jnp.maximum(pos - w // 2, 0)
    return 1.0 / cnt.astype(F32)


def _pool_fwd(p3, w_pool, pool_scale, seq, d_model):
    dp = d_model // 2
    pg = dp // len(POOL_WINDOWS)
    t = min(128, seq)
    n_chunks = seq // t
    h = WIN_HALO

    def body(u_ref, w_ref, sc_ref, d_ref, y_ref, pad_ref):
        g = pl.program_id(0)
        zeros = jnp.zeros((h, pg), F32)
        pad_ref[0:h, :] = zeros
        pad_ref[h + seq:h + seq + h, :] = zeros

        def fill(ci, _):
            r0 = pl.multiple_of(ci * t, t)
            pad_ref[pl.ds(h + r0, t), :] = u_ref[pl.ds(r0, t), :]
            return 0

        lax.fori_loop(0, n_chunks, fill, 0)
        wmat = w_ref[...]
        scale = sc_ref[...]
        for gi, w in enumerate(POOL_WINDOWS):
            @pl.when(g == gi)
            def _(w=w):
                def chunk(ci, _):
                    r0 = pl.multiple_of(ci * t, t)
                    ext = pad_ref[pl.ds(r0, t + 2 * h), :]
                    mean = _win_sum(ext, w, 0)[h:h + t, :] * _inv_count(r0, t, w, seq)
                    d = (mean - ext[h:h + t, :]).astype(BF16)
                    d_ref[pl.ds(r0, t), :] = d
                    q = jnp.dot(d, wmat, preferred_element_type=F32)
                    y_ref[pl.ds(r0, t), :] = (q * scale).astype(BF16)
                    return 0

                lax.fori_loop(0, n_chunks, chunk, 0)

    return pl.pallas_call(
        body, name="pool_fwd", grid=(len(POOL_WINDOWS),),
        in_specs=[_bs((None, seq, pg), lambda g: (0, 0, g)), _bs((None, pg, pg), lambda g: (g, 0, 0)),
                  _bs((1, pg), lambda g: (0, g))],
        out_specs=[_bs((seq, pg), lambda g: (0, g)), _bs((seq, pg), lambda g: (0, g))],
        out_shape=[jax.ShapeDtypeStruct((seq, dp), BF16), jax.ShapeDtypeStruct((seq, d_model), BF16)],
        scratch_shapes=[pltpu.VMEM((seq + 2 * h, pg), F32)],
        compiler_params=_params(1),
    )(p3, w_pool, pool_scale)


def _pool_bwd(d, dy, w_pool, pool_scale, token, seq, d_model):
    dp = d_model // 2
    pg = dp // len(POOL_WINDOWS)
    t = min(128, seq)
    n_chunks = seq // t
    h = WIN_HALO
    tn_dims = (((0,), (0,)), ((), ()))
    nt_dims = (((1,), (1,)), ((), ()))

    def body(d_ref, dy_ref, w_ref, sc_ref, tok_ref, du_ref, dwb_ref, dsc_ref, pad_ref, dd_ref, dw_ref):
        del tok_ref
        g = pl.program_id(0)
        zeros = jnp.zeros((h, pg), F32)
        pad_ref[0:h, :] = zeros
        pad_ref[h + seq:h + seq + h, :] = zeros
        wmat = w_ref[...]
        scale = sc_ref[...]
        for gi, w in enumerate(POOL_WINDOWS):
            @pl.when(g == gi)
            def _(w=w):
                dw_ref[...] = jnp.zeros((pg, pg), F32)

                def first(ci, dsc):
                    r0 = pl.multiple_of(ci * t, t)
                    dv = d_ref[pl.ds(r0, t), :]
                    dyv = dy_ref[pl.ds(r0, t), :]
                    q = jnp.dot(dv, wmat, preferred_element_type=F32)
                    dsc = dsc + jnp.sum(dyv * q, axis=0, keepdims=True)
                    dq = (dyv * scale).astype(BF16)
                    dw_ref[...] += lax.dot_general(dv, dq, tn_dims, preferred_element_type=F32)
                    dd = lax.dot_general(dq, wmat, nt_dims, preferred_element_type=F32)
                    dd_ref[pl.ds(r0, t), :] = dd
                    pad_ref[pl.ds(h + r0, t), :] = dd * _inv_count(r0, t, w, seq)
                    return dsc

                dsc_ref[...] = lax.fori_loop(0, n_chunks, first, jnp.zeros((1, pg), F32))
                dwb_ref[...] = dw_ref[...].reshape(N_DEV, pg // N_DEV, pg).astype(BF16)

                def second(ci, _):
                    r0 = pl.multiple_of(ci * t, t)
                    ext = pad_ref[pl.ds(r0, t + 2 * h), :]
                    back = _win_sum(ext, w, 1)[h:h + t, :]
                    du_ref[pl.ds(r0, t), :] = (back - dd_ref[pl.ds(r0, t), :]).astype(BF16)
                    return 0

                lax.fori_loop(0, n_chunks, second, 0)

    return pl.pallas_call(
        body, name="pool_bwd", grid=(len(POOL_WINDOWS),),
        in_specs=[_bs((seq, pg), lambda g: (0, g)), _bs((seq, pg), lambda g: (0, g)),
                  _bs((None, pg, pg), lambda g: (g, 0, 0)), _bs((1, pg), lambda g: (0, g)),
                  _bs((8, 128), lambda g: (0, 0))],
        out_specs=[_bs((seq, pg), lambda g: (0, g)), _bs((N_DEV, None, pg // N_DEV, pg), lambda g: (0, g, 0, 0)),
                   _bs((1, pg), lambda g: (0, g))],
        out_shape=[jax.ShapeDtypeStruct((seq, 3 * dp), BF16),
                   jax.ShapeDtypeStruct((N_DEV, len(POOL_WINDOWS), pg // N_DEV, pg), BF16),
                   jax.ShapeDtypeStruct((1, dp), F32)],
        scratch_shapes=[pltpu.VMEM((seq + 2 * h, pg), F32), pltpu.VMEM((seq, pg), F32), pltpu.VMEM((pg, pg), F32)],
        compiler_params=_params(1),
    )(d, dy, w_pool, pool_scale, token)


def _tile_scan(n_tiles, lanes, loads, stores):
    row = lax.broadcasted_iota(jnp.int32, (8, lanes), 0)
    group = 8

    def local_scan(n, k):
        aa, bb = loads[n](k)
        for sh in (1, 2, 4):
            if n == 0:
                ok = row >= sh
                ap = jnp.where(ok, pltpu.roll(aa, sh, 0), 1.0)
                bp = jnp.where(ok, pltpu.roll(bb, sh, 0), 0.0)
            else:
                ok = row < 8 - sh
                ap = jnp.where(ok, pltpu.roll(aa, 8 - sh, 0), 1.0)
                bp = jnp.where(ok, pltpu.roll(bb, 8 - sh, 0), 0.0)
            bb = aa * bp + bb
            aa = aa * ap
        return aa, bb

    def step(s, carry):
        carry = list(carry)
        for n in range(2):
            tiles = [s * group + u if n == 0 else n_tiles - 1 - (s * group + u) for u in range(group)]
            local = [local_scan(n, k) for k in tiles]
            for k, (aa, bb) in zip(tiles, local):
                hh = bb + aa * carry[n]
                stores[n](k, hh)
                carry[n] = jnp.broadcast_to(hh[7:8, :] if n == 0 else hh[0:1, :], (8, lanes))
        return tuple(carry)

    zeros = jnp.zeros((8, lanes), F32)
    lax.fori_loop(0, n_tiles // group, step, (zeros, zeros))


def _gate_preacts(xc, wcat_ref):
    xcb = xc.astype(BF16)
    return xcb, jnp.dot(xcb, wcat_ref[...], preferred_element_type=F32)


def _gates(pre, n, pk_ref, sp):
    lh = pre.shape[1] // 4
    r = _sigmoid(pre[:, (2 * n) * lh:(2 * n + 1) * lh] + pk_ref[pl.ds(4 + n, 1), :])
    i = _sigmoid(pre[:, (2 * n + 1) * lh:(2 * n + 2) * lh] + pk_ref[pl.ds(6 + n, 1), :])
    log_a = (-RG_C * r) * sp[n]
    a = jnp.exp(log_a)
    x = 2.0 * log_a
    one_minus_a2 = jnp.where(x > -0.01, -(x * (1.0 + x * (0.5 + x * (1.0 / 6.0)))), 1.0 - a * a)
    m = jnp.sqrt(one_minus_a2)
    return r, i, a, m


def _conv_chunk(upad_ref, pk_ref, cb, r0, t):
    ext = upad_ref[pl.ds(r0, t + 2 * CONV_HALO), :]
    acc = pk_ref[pl.ds(1, 1), :] * ext
    for k in (0, 2, 3):
        acc = acc + pk_ref[pl.ds(k, 1), :] * _shift(ext, k - 1)
    return acc[CONV_HALO:CONV_HALO + t, :] + cb, ext


def _lru_fwd(p3, y_in, pack, conv_b, wcat, token, seq, d_model):
    dl = d_model // 2
    lh = dl // N_HEADS
    t = min(128, seq)
    n_chunks = seq // t
    seg = seq // 8
    hal = CONV_HALO
    first_rec_block = (d_model - dl) // lh

    def body(ur_ref, ug_ref, pk_ref, cb_ref, wcat_ref, yin_ref, tok_ref, y_ref, h0_ref, h1_ref,
             upad, a_scr, b_scr):
        del yin_ref, tok_ref
        zeros = jnp.zeros((hal, lh), F32)
        upad[0:hal, :] = zeros
        upad[hal + seq:hal + seq + hal, :] = zeros
        for ref in (h0_ref, h1_ref):
            ref[0:hal, :] = zeros
            ref[hal + seq:hal + seq + hal, :] = zeros

        def fill(ci, _):
            r0 = pl.multiple_of(ci * t, t)
            upad[pl.ds(hal + r0, t), :] = ur_ref[pl.ds(r0, t), :]
            return 0

        lax.fori_loop(0, n_chunks, fill, 0)
        cb = cb_ref[...]
        sp = [_softplus(-pk_ref[pl.ds(8 + n, 1), :]) for n in range(2)]

        def chunk(ci, _):
            r0 = pl.multiple_of(ci * t, t)
            xc, _ext = _conv_chunk(upad, pk_ref, cb, r0, t)
            _, pre = _gate_preacts(xc, wcat_ref)
            for n in range(2):
                _, i, a, m = _gates(pre, n, pk_ref, sp)
                a_scr[n, pl.ds(r0, t), :] = a
                b_scr[n, pl.ds(r0, t), :] = (m * i) * xc
            return 0

        lax.fori_loop(0, n_chunks, chunk, 0, unroll=2)

        def load(n):
            def get(k):
                at = pl.ds(pl.multiple_of(k * 8, 8), 8)
                return a_scr[n, at, :], b_scr[n, at, :]
            return get

        def store(ref):
            def put(k, v):
                ref[pl.ds(pl.multiple_of(hal + k * 8, 8), 8), :] = v
            return put

        _tile_scan(seq // 8, lh, [load(0), load(1)], [store(h0_ref), store(h1_ref)])

        def out(ci, _):
            r0 = pl.multiple_of(ci * t, t)
            hsum = h0_ref[pl.ds(hal + r0, t), :] + h1_ref[pl.ds(hal + r0, t), :]
            gl, _dg = _gelu_and_grad(ug_ref[pl.ds(r0, t), :])
            y_ref[pl.ds(r0, t), :] = (hsum * gl).astype(BF16)
            return 0

        lax.fori_loop(0, n_chunks, out, 0)

    return pl.pallas_call(
        body, name="lru_fwd", grid=(N_HEADS,),
        in_specs=[_bs((None, seq, lh), lambda h: (1, 0, h)), _bs((None, seq, lh), lambda h: (2, 0, h)),
                  _bs((None, SMALL_ROWS, lh), lambda h: (h, 0, 0)), _bs((1, lh), lambda h: (0, h)),
                  _bs((None, lh, 4 * lh), lambda h: (h, 0, 0)),
                  ANY, _bs((8, 128), lambda h: (0, 0))],
        out_specs=[_bs((seq, lh), lambda h: (0, first_rec_block + h)),
                   _bs((seq + 2 * hal, lh), lambda h: (0, h)), _bs((seq + 2 * hal, lh), lambda h: (0, h))],
        out_shape=[jax.ShapeDtypeStruct((seq, d_model), BF16), jax.ShapeDtypeStruct((seq + 2 * hal, dl), F32),
                   jax.ShapeDtypeStruct((seq + 2 * hal, dl), F32)],
        scratch_shapes=[pltpu.VMEM((seq + 2 * hal, lh), F32), pltpu.VMEM((2, seq, lh), F32),
                        pltpu.VMEM((2, seq, lh), F32)],
        input_output_aliases={5: 0},
        compiler_params=_params(1),
    )(p3, p3, pack, conv_b, wcat, y_in, token)


def _lru_bwd(p3, dy, h0p, h1p, dproj_in, pack, conv_b, wcat, token, seq, d_model):
    dl = d_model // 2
    lh = dl // N_HEADS
    t = min(128, seq)
    n_chunks = seq // t
    seg = seq // 8
    hal = CONV_HALO
    first_rec_block = (d_model - dl) // lh
    tn_dims = (((0,), (0,)), ((), ()))
    nt_dims = (((1,), (1,)), ((), ()))

    def body(ur_ref, ug_ref, dy_ref, h0_ref, h1_ref, pk_ref, cb_ref, wcat_ref, tok_ref, din_ref,
             dproj_ref, dpk_ref, dcb_ref, dwcat_ref,
             upad, a_scr, dh_scr, g_scr, dxc_pad, dpr_ref, out_sems):
        del din_ref, tok_ref
        zeros = jnp.zeros((hal, lh), F32)
        for ref in (upad, dxc_pad):
            ref[0:hal, :] = zeros
            ref[hal + seq:hal + seq + hal, :] = zeros
        for n in range(2):
            a_scr[n, 0:hal, :] = zeros
            a_scr[n, hal + seq:hal + seq + hal, :] = zeros

        def fill(ci, _):
            r0 = pl.multiple_of(ci * t, t)
            upad[pl.ds(hal + r0, t), :] = ur_ref[pl.ds(r0, t), :]
            return 0

        lax.fori_loop(0, n_chunks, fill, 0)
        cb = cb_ref[...]
        lam = [pk_ref[pl.ds(8 + n, 1), :] for n in range(2)]
        sp = [_softplus(-lam[n]) for n in range(2)]

        def chunk1(ci, _):
            r0 = pl.multiple_of(ci * t, t)
            xc, _ext = _conv_chunk(upad, pk_ref, cb, r0, t)
            _, pre = _gate_preacts(xc, wcat_ref)
            for n in range(2):
                _, _, a, _ = _gates(pre, n, pk_ref, sp)
                a_scr[n, pl.ds(hal + r0, t), :] = a
            hsum = h0_ref[pl.ds(hal + r0, t), :] + h1_ref[pl.ds(hal + r0, t), :]
            gl, dgl = _gelu_and_grad(ug_ref[pl.ds(r0, t), :])
            dyv = dy_ref[pl.ds(r0, t), :]
            dh_scr[pl.ds(r0, t), :] = dyv * gl
            dpr_ref[1, pl.ds(r0, t), :] = ((dyv * hsum) * dgl).astype(BF16)
            return 0

        lax.fori_loop(0, n_chunks, chunk1, 0, unroll=2)

        def load(n):
            def get(k):
                r0 = pl.multiple_of(k * 8, 8)
                if n == 0:
                    coef = _shift(a_scr[0, pl.ds(pl.multiple_of(hal + r0, 8), 16), :], 1)[0:8, :]
                else:
                    coef = _shift(a_scr[1, pl.ds(pl.multiple_of(hal + r0 - 8, 8), 16), :], -1)[8:16, :]
                return coef, dh_scr[pl.ds(r0, 8), :]
            return get

        def store(n):
            def put(k, v):
                g_scr[n, pl.ds(pl.multiple_of(k * 8, 8), 8), :] = v
            return put

        _tile_scan(seq // 8, lh, [load(1), load(0)], [store(1), store(0)])

        dwcat_ref[...] = jnp.zeros((lh, 4 * lh), F32)

        def chunk3(ci, carry):
            dba, dbi, dlam, dcb = carry
            r0 = pl.multiple_of(ci * t, t)
            xc, _ext = _conv_chunk(upad, pk_ref, cb, r0, t)
            xcb, pre = _gate_preacts(xc, wcat_ref)
            dxc = jnp.zeros((t, lh), F32)
            dba, dbi, dlam = list(dba), list(dbi), list(dlam)
            dpre = []
            for n in range(2):
                r, i, a, m = _gates(pre, n, pk_ref, sp)
                hext = (h0_ref if n == 0 else h1_ref)[pl.ds(r0, t + 2 * hal), :]
                hprev = _shift(hext, -1 if n == 0 else 1)[hal:hal + t, :]
                gb = g_scr[n, pl.ds(r0, t), :]
                da = gb * hprev
                dm = gb * i * xc
                di = gb * m * xc
                dxc = dxc + gb * (m * i)
                dlog_a = da * a - dm * (a * a) / m
                dr = dlog_a * (-RG_C * sp[n])
                dlam[n] = dlam[n] + jnp.sum(dlog_a * r, axis=0, keepdims=True)
                dpr = dr * r * (1.0 - r)
                dpi = di * i * (1.0 - i)
                dba[n] = dba[n] + jnp.sum(dpr, axis=0, keepdims=True)
                dbi[n] = dbi[n] + jnp.sum(dpi, axis=0, keepdims=True)
                dpre += [dpr.astype(BF16), dpi.astype(BF16)]
            dpre = jnp.concatenate(dpre, axis=1)
            dwcat_ref[...] += lax.dot_general(xcb, dpre, tn_dims, preferred_element_type=F32)
            dxc = dxc + lax.dot_general(dpre, wcat_ref[...], nt_dims, preferred_element_type=F32)
            dxc_pad[pl.ds(hal + r0, t), :] = dxc
            dcb = dcb + jnp.sum(dxc, axis=0, keepdims=True)
            return tuple(dba), tuple(dbi), tuple(dlam), dcb

        zr = jnp.zeros((1, lh), F32)
        def chunk3_pair(cj, carry):
            return chunk3(2 * cj + 1, chunk3(2 * cj, carry))

        dba, dbi, dlam, dcb = lax.fori_loop(0, n_chunks // 2, chunk3_pair, ((zr, zr), (zr, zr), (zr, zr), zr))
        dcb_ref[...] = dcb
        for n in range(2):
            dpk_ref[pl.ds(4 + n, 1), :] = dba[n]
            dpk_ref[pl.ds(6 + n, 1), :] = dbi[n]
            dpk_ref[pl.ds(8 + n, 1), :] = dlam[n] * (RG_C * jax.nn.sigmoid(-lam[n]))
        dpk_ref[pl.ds(10, SMALL_ROWS - 10), :] = jnp.zeros((SMALL_ROWS - 10, lh), F32)

        def chunk4(ci, dtap):
            r0 = pl.multiple_of(ci * t, t)
            gext = dxc_pad[pl.ds(r0, t + 2 * hal), :]
            uext = upad[pl.ds(r0, t + 2 * hal), :]
            gmid = gext[hal:hal + t, :]
            du = pk_ref[pl.ds(1, 1), :] * gext
            for k in (0, 2, 3):
                du = du + pk_ref[pl.ds(k, 1), :] * _shift(gext, 1 - k)
            dpr_ref[0, pl.ds(r0, t), :] = du[hal:hal + t, :].astype(BF16)
            out = []
            for k in range(4):
                usl = _shift(uext, k - 1)[hal:hal + t, :]
                out.append(dtap[k] + jnp.sum(gmid * usl, axis=0, keepdims=True))
            return tuple(out)

        dtap = lax.fori_loop(0, n_chunks, chunk4, (zr, zr, zr, zr))
        for k in range(4):
            dpk_ref[pl.ds(k, 1), :] = dtap[k]

        head = pl.program_id(0)
        outs = [pltpu.make_async_copy(
            dpr_ref.at[b], dproj_ref.at[:, pl.ds(pl.multiple_of((1 + b) * dl + head * lh, lh), lh)], out_sems.at[b])
            for b in range(2)]
        for cp in outs:
            cp.start()
        for cp in outs:
            cp.wait()

    return pl.pallas_call(
        body, name="lru_bwd", grid=(N_HEADS,),
        in_specs=[_bs((None, seq, lh), lambda h: (1, 0, h)), _bs((None, seq, lh), lambda h: (2, 0, h)),
                  _bs((seq, lh), lambda h: (0, first_rec_block + h)),
                  _bs((seq + 2 * hal, lh), lambda h: (0, h)), _bs((seq + 2 * hal, lh), lambda h: (0, h)),
                  _bs((None, SMALL_ROWS, lh), lambda h: (h, 0, 0)), _bs((1, lh), lambda h: (0, h)),
                  _bs((None, lh, 4 * lh), lambda h: (h, 0, 0)),
                  _bs((8, 128), lambda h: (0, 0)), ANY],
        out_specs=[ANY, _bs((None, SMALL_ROWS, lh), lambda h: (h, 0, 0)),
                   _bs((1, lh), lambda h: (0, h)), _bs((None, lh, 4 * lh), lambda h: (h, 0, 0))],
        out_shape=[jax.ShapeDtypeStruct((seq, 3 * dl), BF16), jax.ShapeDtypeStruct((N_HEADS, SMALL_ROWS, lh), F32),
                   jax.ShapeDtypeStruct((1, dl), F32), jax.ShapeDtypeStruct((N_HEADS, lh, 4 * lh), F32)],
        scratch_shapes=[pltpu.VMEM((seq + 2 * hal, lh), F32), pltpu.VMEM((2, seq + 2 * hal, lh), F32),
                        pltpu.VMEM((seq, lh), F32), pltpu.VMEM((2, seq, lh), F32),
                        pltpu.VMEM((seq + 2 * hal, lh), F32), pltpu.VMEM((2, seq, lh), BF16),
                        pltpu.SemaphoreType.DMA((2,))],
        input_output_aliases={9: 0},
        compiler_params=_params(1),
    )(p3, p3, dy, h0p, h1p, pack, conv_b, wcat, token, dproj_in)


class _tiles:
    def __init__(self, seq, d_model, d_ff):
        self.rows = min(1024, seq)
        self.ln_rows = min(256, seq)
        self.ff_cols = min(1024, d_ff)
        self.ff_split = 4
        self.ff_k = min(2048, d_ff)
        self.grad_rows = 512


def _ln_loss_bwd(ffn, x1, tgt, g, b, tr):
    seq, d = ffn.shape

    def body(f_ref, x_ref, t_ref, g_ref, b_ref, dz_ref, dzb_ref, dg_ref, db_ref, loss_ref):
        i = pl.program_id(0)
        gv = g_ref[...]
        z = ALPHA * x_ref[...] + f_ref[...]
        y, xhat, rstd = _ln_fwd(z, gv, b_ref[...])
        err = y - t_ref[...]
        part = 0.5 * jnp.sum(jnp.mean(err * err, axis=-1, keepdims=True), axis=0, keepdims=True)
        dz, dg, db = _ln_bwd(err * (1.0 / d), xhat, rstd, gv)
        dz_ref[...] = dz
        dzb_ref[...] = dz.astype(BF16)
        _acc_rows(dg_ref, i == 0, dg)
        _acc_rows(db_ref, i == 0, db)
        _acc_rows(loss_ref, i == 0, jnp.broadcast_to(part, (8, 128)))

    row = _bs((tr, d), lambda i: (i, 0))
    vec = _bs((1, d), lambda i: (0, 0))
    return pl.pallas_call(
        body, name="ln_ffn_loss", grid=(seq // tr,), in_specs=[row, row, row, vec, vec],
        out_specs=[row, row, vec, vec, _bs((8, 128), lambda i: (0, 0))],
        out_shape=[jax.ShapeDtypeStruct((seq, d), F32), jax.ShapeDtypeStruct((seq, d), BF16),
                   jax.ShapeDtypeStruct((1, d), F32), jax.ShapeDtypeStruct((1, d), F32),
                   jax.ShapeDtypeStruct((8, 128), F32)],
        compiler_params=_params(1),
    )(ffn, x1, tgt, g, b)


def _ln_bwd_rows(dx_branch, dres, z, g, b, tr):
    seq, d = z.shape

    def body(a_ref, r_ref, z_ref, g_ref, b_ref, dz_ref, dzb_ref, dg_ref, db_ref):
        i = pl.program_id(0)
        gv = g_ref[...]
        _, xhat, rstd = _ln_fwd(z_ref[...], gv, b_ref[...])
        dz, dg, db = _ln_bwd(ALPHA * r_ref[...] + a_ref[...], xhat, rstd, gv)
        dz_ref[...] = dz
        dzb_ref[...] = dz.astype(BF16)
        _acc_rows(dg_ref, i == 0, dg)
        _acc_rows(db_ref, i == 0, db)

    row = _bs((tr, d), lambda i: (i, 0))
    vec = _bs((1, d), lambda i: (0, 0))
    return pl.pallas_call(
        body, name="ln_mix_bwd", grid=(seq // tr,), in_specs=[row, row, row, vec, vec],
        out_specs=[row, row, vec, vec],
        out_shape=[jax.ShapeDtypeStruct((seq, d), F32), jax.ShapeDtypeStruct((seq, d), BF16),
                   jax.ShapeDtypeStruct((1, d), F32), jax.ShapeDtypeStruct((1, d), F32)],
        compiler_params=_params(1),
    )(dx_branch, dres, z, g, b)


def _adamw_values(w, g, m, v):
    m = ADAM_B1 * m + (1.0 - ADAM_B1) * g
    v = ADAM_B2 * v + (1.0 - ADAM_B2) * (g * g)
    m_hat = m / (1.0 - ADAM_B1 ** ADAM_STEP)
    v_hat = v / (1.0 - ADAM_B2 ** ADAM_STEP)
    delta = -ADAM_LR * (m_hat / (jnp.sqrt(v_hat) + ADAM_EPS) + ADAM_WD * w)
    return delta, m, v


def _sum_adamw(name, own, parts, w, m, v):
    rows, cols = w.shape
    n_parts = parts.shape[0]
    tr = rows
    min_rows = 8 if parts.dtype == F32 else 16
    while tr * cols * 4 > 1024 * 1024 and tr % (2 * min_rows) == 0:
        tr //= 2

    def body(*refs):
        if own is None:
            p_ref, w_ref, m_ref, v_ref, g_ref, d_ref, mo_ref, vo_ref = refs
            g = p_ref[0].astype(F32)
            rest = range(1, n_parts)
        else:
            o_ref, p_ref, w_ref, m_ref, v_ref, g_ref, d_ref, mo_ref, vo_ref = refs
            g = o_ref[...]
            rest = range(n_parts)
        for s in rest:
            g = g + p_ref[s].astype(F32)
        delta, mn, vn = _adamw_values(w_ref[...], g, m_ref[...], v_ref[...])
        g_ref[...] = g
        d_ref[...] = delta
        mo_ref[...] = mn
        vo_ref[...] = vn

    spec = _bs((tr, cols), lambda i: (i, 0))
    lead = [] if own is None else [own]
    return pl.pallas_call(
        body, name=name, grid=(rows // tr,),
        in_specs=[spec] * len(lead) + [_bs((n_parts, tr, cols), lambda i: (0, i, 0)), spec, spec, spec],
        out_specs=[spec] * 4, out_shape=[jax.ShapeDtypeStruct((rows, cols), F32)] * 4,
        compiler_params=_params(1),
    )(*lead, parts, w, m, v)


def _rows128(a):
    return a.reshape(-1, 128)


def kernel(x, ln_mix_g, ln_mix_b, w_in, w_pool, pool_scale, conv_w, conv_b, w_rg_a, b_rg_a, w_rg_i, b_rg_i, rg_lambda, w_out, ln_ffn_g, ln_ffn_b, w_mlp_in, w_mlp_out, loss_target, m_ln_mix_g, m_ln_mix_b, m_w_in, m_w_pool, m_pool_scale, m_conv_w, m_conv_b, m_w_rg_a, m_b_rg_a, m_w_rg_i, m_b_rg_i, m_rg_lambda, m_w_out, m_ln_ffn_g, m_ln_ffn_b, m_w_mlp_in, m_w_mlp_out, v_ln_mix_g, v_ln_mix_b, v_w_in, v_w_pool, v_pool_scale, v_conv_w, v_conv_b, v_w_rg_a, v_b_rg_a, v_w_rg_i, v_b_rg_i, v_rg_lambda, v_w_out, v_ln_ffn_g, v_ln_ffn_b, v_w_mlp_in, v_w_mlp_out):
    seq, d_model = x.shape[1], x.shape[2]
    dh = d_model // 2
    lh = dh // N_HEADS
    pg = dh // len(POOL_WINDOWS)
    d_ff = w_mlp_in.shape[2] * N_DEV
    assert lh == 128 and conv_w.shape[3] == lh and w_pool.shape[2] * N_DEV == pg

    xs = x[0]
    tgt = loss_target[0]

    def small_pack(cw, ba, bi, lam):
        return jnp.concatenate([cw.reshape(4, lh), ba.reshape(2, lh), bi.reshape(2, lh), lam.reshape(2, lh),
                                jnp.zeros((SMALL_ROWS - 10, lh), F32)], axis=0)

    pack_mine = small_pack(conv_w, b_rg_a, b_rg_i, rg_lambda)
    win_full, wpool_full, pack_full = _all_gather("gather_mixer", [
        (w_in[0].astype(BF16), 1), (w_pool[0].astype(BF16), 1), (pack_mine[None], 0)])
    wout_gather = _SplitGather("gather_w_out", [(w_out[0], 0)], BF16, after=pack_full)
    w1_gather = _SplitGather("gather_w_mlp_in", [(w_mlp_in[0], 1)], BF16, after=wout_gather.token)
    w2_gather = _SplitGather("gather_w_mlp_out", [(w_mlp_out[0], 0)], BF16, after=w1_gather.token)
    wcat = jnp.concatenate([w_rg_a[0, 0], w_rg_i[0, 0], w_rg_a[0, 1], w_rg_i[0, 1]], axis=-1).astype(BF16)
    vec = lambda i, j, k: (0, 0)
    row_full = lambda i, j, k: (i, 0)

    def after(token):
        return (token, _sp((8, 128), vec))

    def sds(shape, dtype):
        return jax.ShapeDtypeStruct(shape, dtype)

    def plain_epi(acc, i, ex, out):
        out[0][...] = acc

    def bf16_epi(acc, i, ex, out):
        out[0][...] = acc.astype(BF16)

    t = _tiles(seq, d_model, d_ff)

    (p3,) = _matmul(
        "proj", xs, win_full, _sp((t.rows, d_model), lambda i, j, k: (i, 0)), _sp((d_model, dh), lambda i, j, k: (0, j)),
        grid=(seq // t.rows, 3, 1), extras=[after(w2_gather.token)],
        out_shape=[sds((3, seq, dh), F32)], out_specs=[_sp((None, t.rows, dh), lambda i, j, k: (j, i, 0))],
        epilogue=plain_epi)

    d_pool, y_half = _pool_fwd(p3, wpool_full, pool_scale, seq, d_model)
    y, h0p, h1p = _lru_fwd(p3, y_half, pack_full, conv_b, wcat, wout_gather.relay(after=y_half), seq, d_model)
    (wout_full,) = wout_gather.wait(after=y)
    relay_token = w1_gather.relay(after=wout_full)

    def mix_epi(acc, i, ex, out):
        x_ref, g_ref, b_ref = ex[:3]
        z = ALPHA * x_ref[...] + acc
        x1, _, _ = _ln_fwd(z, g_ref[...], b_ref[...])
        out[0][...] = z
        out[1][...] = x1
        out[2][...] = x1.astype(BF16)

    z1, x1, x1b = _matmul(
        "mix_out", y, wout_full, _sp((t.ln_rows, d_model), row_full), _sp((d_model, d_model), vec, single=True),
        grid=(seq // t.ln_rows, 1, 1),
        extras=[(xs, _sp((t.ln_rows, d_model), row_full)), (ln_mix_g, _sp((1, d_model), vec)),
                (ln_mix_b, _sp((1, d_model), vec)), after(relay_token)],
        out_shape=[sds((seq, d_model), F32), sds((seq, d_model), F32), sds((seq, d_model), BF16)],
        out_specs=[_sp((t.ln_rows, d_model), row_full)] * 3, epilogue=mix_epi)
    (w1_full,) = w1_gather.wait(after=x1b)
    relay_token = w2_gather.relay(after=w1_full)

    def mlp_in_epi(acc, i, ex, out, cols):
        h = jnp.maximum(acc, 0.0)
        out[0][:, cols] = (h * h).astype(BF16)

    (hmid,) = _matmul(
        "mlp_in", x1b, w1_full, _sp((t.rows, d_model), lambda i, j, k: (i, 0)),
        _sp((d_model, t.ff_cols), lambda i, j, k: (0, j)),
        grid=(seq // t.rows, d_ff // t.ff_cols, 1), j_outer=True, extras=[after(relay_token)],
        out_shape=[sds((seq, d_ff), BF16)], out_specs=[_sp((t.rows, t.ff_cols), lambda i, j, k: (i, j))],
        epilogue=mlp_in_epi, n_split=t.ff_split)
    (w2_full,) = w2_gather.wait(after=hmid)

    (ffn,) = _matmul(
        "mlp_out", hmid, w2_full, _sp((t.rows, t.ff_k), lambda i, j, k: (i, k)),
        _sp((t.ff_k, d_model), lambda i, j, k: (k, 0)),
        grid=(seq // t.rows, 1, d_ff // t.ff_k),
        out_shape=[sds((seq, d_model), F32)], out_specs=[_sp((t.rows, d_model), row_full)])
    dz2, dz2b, g_ffn_g, g_ffn_b, loss_part = _ln_loss_bwd(ffn, x1, tgt, ln_ffn_g, ln_ffn_b, t.ln_rows)

    def dpre_epi(acc, i, ex, out, cols):
        out[0][:, cols] = (acc * (2.0 * jnp.sqrt(ex[0][:, cols].astype(F32)))).astype(BF16)

    (dpre,) = _matmul(
        "mlp_dpre", dz2b, w2_full, _sp((t.rows, d_model), lambda i, j, k: (i, 0)),
        _sp((t.ff_cols, d_model), lambda i, j, k: (j, 0)),
        grid=(seq // t.rows, d_ff // t.ff_cols, 1), j_outer=True, tb=True,
        extras=[(hmid, _sp((t.rows, t.ff_cols), lambda i, j, k: (i, j)))],
        out_shape=[sds((seq, d_ff), BF16)], out_specs=[_sp((t.rows, t.ff_cols), lambda i, j, k: (i, j))],
        epilogue=dpre_epi, n_split=t.ff_split)

    (dx1_mlp,) = _matmul(
        "mlp_dx", dpre, w1_full, _sp((t.rows, t.ff_k), lambda i, j, k: (i, k)),
        _sp((d_model, t.ff_k), lambda i, j, k: (0, k)),
        grid=(seq // t.rows, 1, d_ff // t.ff_k), tb=True,
        out_shape=[sds((seq, d_model), F32)], out_specs=[_sp((t.rows, d_model), row_full)])
    dz1, dz1b, g_mix_g, g_mix_b = _ln_bwd_rows(dx1_mlp, dz2, z1, ln_mix_g, ln_mix_b, t.ln_rows)

    (g_w2,) = _matmul(
        "grad_w_mlp_out", hmid, dz2b, _sp((seq, t.grad_rows), lambda i, j, k: (0, i)),
        _sp((seq, d_model), vec, single=True),
        grid=(d_ff // t.grad_rows, 1, 1), ta=True,
        out_shape=[sds((d_ff, d_model), BF16)], out_specs=[_sp((t.grad_rows, d_model), row_full)],
        epilogue=bf16_epi)
    g_w2 = g_w2.reshape(N_DEV, d_ff // N_DEV, d_model)

    def block_epi(acc, i, ex, out):
        out[0][0] = acc.astype(BF16)

    fs = d_ff // N_DEV
    (g_w1,) = _matmul(
        "grad_w_mlp_in", x1b, dpre, _sp((seq, t.grad_rows), lambda i, j, k: (0, i)),
        _sp((seq, fs), lambda i, j, k: (0, j)),
        grid=(d_model // t.grad_rows, N_DEV, 1), j_outer=True, ta=True,
        out_shape=[sds((N_DEV, d_model, fs), BF16)],
        out_specs=[_sp((1, t.grad_rows, fs), lambda i, j, k: (j, i, 0))], epilogue=block_epi)

    (dy,) = _matmul(
        "mix_dy", dz1b, wout_full, _sp((t.rows, d_model), lambda i, j, k: (i, 0)),
        _sp((dh, d_model), lambda i, j, k: (j, 0)),
        grid=(seq // t.rows, 2, 1), j_outer=True, tb=True,
        out_shape=[sds((seq, d_model), F32)], out_specs=[_sp((t.rows, dh), lambda i, j, k: (i, j))],
        epilogue=plain_epi)
    (g_wout,) = _matmul(
        "grad_w_out", y, dz1b, _sp((seq, t.grad_rows), lambda i, j, k: (0, i)), _sp((seq, d_model), vec, single=True),
        grid=(d_model // t.grad_rows, 1, 1), ta=True,
        out_shape=[sds((d_model, d_model), BF16)], out_specs=[_sp((t.grad_rows, d_model), row_full)],
        epilogue=bf16_epi)
    g_wout = g_wout.reshape(N_DEV, d_model // N_DEV, d_model)

    scatter_a = _SplitReduceScatter("scatter_a", [g_w1, g_w2, g_wout])

    dproj_pool, g_wpool, g_pscale = _pool_bwd(d_pool, dy, wpool_full, pool_scale, scatter_a.token, seq, d_model)
    token_a = scatter_a.combine_and_send(after=dproj_pool)
    dproj, g_pack, g_convb, g_wcat = _lru_bwd(p3, dy, h0p, h1p, dproj_pool, pack_full, conv_b, wcat,
                                              token_a, seq, d_model)
    g_wa = jnp.stack([g_wcat[:, :, 0:lh], g_wcat[:, :, 2 * lh:3 * lh]])
    g_wi = jnp.stack([g_wcat[:, :, lh:2 * lh], g_wcat[:, :, 3 * lh:4 * lh]])

    rep_parts = [_rows128(g_wa), _rows128(g_wi), _rows128(g_mix_g), _rows128(g_mix_b), _rows128(g_ffn_g),
                 _rows128(g_ffn_b), _rows128(g_pscale), _rows128(g_convb)]
    rep_rows = [p.shape[0] for p in rep_parts]
    n_rep = sum(rep_rows)
    small = jnp.concatenate(rep_parts + [_rows128(g_pack)], axis=0)
    small_gather = _SplitGather("gather_small_grads", [(small[None], 0)], F32, after=small)

    ws = 3 * dh // N_DEV

    def pair_epi(acc, i, ex, out):
        out[0][0] = acc[:, :ws].astype(BF16)
        out[0][1] = acc[:, ws:].astype(BF16)

    (g_win,) = _matmul(
        "grad_w_in", xs, dproj, _sp((seq, t.grad_rows), lambda i, j, k: (0, i)),
        _sp((seq, 2 * ws), lambda i, j, k: (0, j)),
        grid=(d_model // t.grad_rows, N_DEV // 2, 1), ta=True, extras=[after(small_gather.token)],
        out_shape=[sds((N_DEV, d_model, ws), BF16)],
        out_specs=[_sp((2, t.grad_rows, ws), lambda i, j, k: (j, i, 0))], epilogue=pair_epi)
    scatter_b = _SplitReduceScatter("scatter_b", [g_win, g_wpool.reshape(N_DEV, pg // N_DEV * len(POOL_WINDOWS), pg)])

    def adam_big(name, own_landed, w, m, v):
        own, landed = own_landed
        shp = w.shape
        two = lambda a: a.reshape(-1, shp[-1])
        res = _sum_adamw(name, own, landed, two(w), two(m), two(v))
        return [r.reshape(shp) for r in res]

    r_w1, r_w2, r_wout = scatter_a.wait(after=scatter_b.token)
    o_w1 = adam_big("adam_w_mlp_in", r_w1, w_mlp_in, m_w_mlp_in, v_w_mlp_in)
    token_b = scatter_b.combine_and_send(after=o_w1[0])

    def dx_epi(acc, i, ex, out):
        out[0][...] = ALPHA * ex[0][...] + acc

    (dx,) = _matmul(
        "grad_x", dproj, win_full, _sp((t.ln_rows * 2, 3 * dh), lambda i, j, k: (i, 0)),
        _sp((d_model, 3 * dh), vec, single=True),
        grid=(seq // (t.ln_rows * 2), 1, 1), tb=True,
        extras=[(dz1, _sp((t.ln_rows * 2, d_model), row_full)), after(token_b)],
        out_shape=[sds((seq, d_model), F32)], out_specs=[_sp((t.ln_rows * 2, d_model), row_full)],
        epilogue=dx_epi)
    o_w2 = adam_big("adam_w_mlp_out", r_w2, w_mlp_out, m_w_mlp_out, v_w_mlp_out)
    o_wout = adam_big("adam_w_out", r_wout, w_out, m_w_out, v_w_out)
    r_win, r_wpool = scatter_b.wait(after=dx)
    o_win = adam_big("adam_w_in", r_win, w_in, m_w_in, v_w_in)
    o_wpool = adam_big("adam_w_pool", r_wpool, w_pool, m_w_pool, v_w_pool)

    small_gather.relay(after=o_win[0])
    (small_all,) = small_gather.wait(after=o_wpool[0])

    rep_w = [w_rg_a, w_rg_i, ln_mix_g, ln_mix_b, ln_ffn_g, ln_ffn_b, pool_scale, conv_b]
    rep_m = [m_w_rg_a, m_w_rg_i, m_ln_mix_g, m_ln_mix_b, m_ln_ffn_g, m_ln_ffn_b, m_pool_scale, m_conv_b]
    rep_v = [v_w_rg_a, v_w_rg_i, v_ln_mix_g, v_ln_mix_b, v_ln_ffn_g, v_ln_ffn_b, v_pool_scale, v_conv_b]
    cat = lambda arrs: jnp.concatenate([_rows128(a) for a in arrs], axis=0)
    o_rep = _sum_adamw("adam_replicated", None, small_all, cat(rep_w), cat(rep_m), cat(rep_v))

    my_idx = _dev_index(_where_am_i())
    head_parts = lax.dynamic_slice_in_dim(small_all, n_rep + my_idx * SMALL_ROWS, SMALL_ROWS, axis=1)
    o_head = _sum_adamw("adam_head", None, head_parts, pack_mine,
                        small_pack(m_conv_w, m_b_rg_a, m_b_rg_i, m_rg_lambda),
                        small_pack(v_conv_w, v_b_rg_a, v_b_rg_i, v_rg_lambda))

    def unpack_rep(packed):
        out, r = [], 0
        for wgt, rows in zip(rep_w, rep_rows):
            out.append(packed[r:r + rows].reshape(wgt.shape))
            r += rows
        return out

    def unpack_head(packed):
        return [packed[0:4].reshape(conv_w.shape), packed[4:6].reshape(b_rg_a.shape),
                packed[6:8].reshape(b_rg_i.shape), packed[8:10].reshape(rg_lambda.shape)]

    loss = lax.psum(loss_part[0, 0], ("x", "y", "c"))

    outs = [loss, dx[None]]
    for kind in range(4):
        ra, ri, mg, mb, fg, fb, ps, cb = unpack_rep(o_rep[kind])
        cw, ba, bi, lam = unpack_head(o_head[kind])
        outs += [mg, mb, o_win[kind], o_wpool[kind], ps, cw, cb, ra, ba, ri, bi, lam, o_wout[kind], fg, fb,
                 o_w1[kind], o_w2[kind]]
    return tuple(outs)
```

```python
import functools

import jax
import jax.numpy as jnp
from jax import lax
from jax.experimental import pallas as pl
from jax.experimental.pallas import tpu as pltpu

F32 = jnp.float32
BF16 = jnp.bfloat16
MESH = pl.DeviceIdType.MESH
ANY = pl.BlockSpec(memory_space=pl.ANY)

N_DEV = 8
POOL_WINDOWS = (2, 4, 8, 16)
N_HEADS = 8
RG_C = 8.0
LN_EPS = 1e-5
ALPHA = 2.0 ** 0.25
ADAM_LR = 0.001
ADAM_B1 = 0.9
ADAM_B2 = 0.999
ADAM_EPS = 1e-08
ADAM_WD = 0.01
ADAM_STEP = 10

VMEM_LIMIT = 56 * 1024 * 1024
WIN_HALO = 16
CONV_HALO = 8
SMALL_ROWS = 16


def _params(n_grid):
    return pltpu.CompilerParams(dimension_semantics=("arbitrary",) * n_grid, vmem_limit_bytes=VMEM_LIMIT)


def _shift(v, j):
    n = v.shape[0]
    s = (-j) % n
    return v if s == 0 else pltpu.roll(v, s, 0)


def _sigmoid(x):
    return 0.5 * jnp.tanh(0.5 * x) + 0.5


def _softplus(z):
    e = jnp.exp(-jnp.abs(z))
    u = 1.0 + e
    log1p = jnp.where(u == 1.0, e, jnp.log(u) * (e / jnp.where(u == 1.0, 1.0, u - 1.0)))
    return jnp.maximum(z, 0.0) + log1p


_GELU_C = 0.7978845608028654
_GELU_K = 0.044715


def _gelu_and_grad(x):
    x2 = x * x
    t = jnp.tanh(_GELU_C * (x + _GELU_K * x * x2))
    g = 0.5 * x * (1.0 + t)
    dg = 0.5 * (1.0 + t) + 0.5 * x * (1.0 - t * t) * (_GELU_C * (1.0 + 3.0 * _GELU_K * x2))
    return g, dg


def _ln_fwd(z, g, b):
    mu = jnp.mean(z, axis=-1, keepdims=True)
    zc = z - mu
    var = jnp.mean(zc * zc, axis=-1, keepdims=True)
    rstd = lax.rsqrt(var + LN_EPS)
    xhat = zc * rstd
    return xhat * g + b, xhat, rstd


def _ln_bwd(dy, xhat, rstd, g):
    dxhat = dy * g
    m1 = jnp.mean(dxhat, axis=-1, keepdims=True)
    m2 = jnp.mean(dxhat * xhat, axis=-1, keepdims=True)
    dz = rstd * (dxhat - m1 - xhat * m2)
    dg = jnp.sum(dy * xhat, axis=0, keepdims=True)
    db = jnp.sum(dy, axis=0, keepdims=True)
    return dz, dg, db


def _acc_rows(ref, first, val):
    @pl.when(first)
    def _():
        ref[...] = val

    @pl.when(jnp.logical_not(first))
    def _():
        ref[...] += val


def _sp(shape, fn, single=False):
    return shape, fn, single


def _matmul(name, a, b, a_spec, b_spec, *, grid, j_outer=False, ta=False, tb=False, extras=(), out_shape, out_specs,
            epilogue=None, n_split=1):
    ni, nj, nk = grid
    n_ex = len(extras)
    dims = (((0 if ta else 1,), (1 if tb else 0,)), ((), ()))

    def mk(spec):
        shape, fn, single = spec
        index = (lambda g0, g1, g2: fn(g1, g0, g2)) if j_outer else fn
        return pl.BlockSpec(shape, index, pipeline_mode=pl.Buffered(1)) if single else pl.BlockSpec(shape, index)

    def body(a_ref, b_ref, *rest):
        ex_refs = rest[:n_ex]
        out_refs = rest[n_ex:]
        i = pl.program_id(1 if j_outer else 0)
        if n_split > 1:
            av = a_ref[...].astype(BF16)
            width = b_ref.shape[0 if tb else 1] // n_split
            for c in range(n_split):
                cols = pl.ds(c * width, width)
                bv = (b_ref[cols, :] if tb else b_ref[:, cols]).astype(BF16)
                epilogue(lax.dot_general(av, bv, dims, preferred_element_type=F32), i, ex_refs, out_refs, cols)
            return
        part = lax.dot_general(a_ref[...].astype(BF16), b_ref[...].astype(BF16), dims, preferred_element_type=F32)
        if nk == 1:
            epilogue(part, i, ex_refs, out_refs)
        else:
            @pl.when(pl.program_id(2) == 0)
            def _():
                out_refs[0][...] = jnp.zeros(out_refs[0].shape, F32)

            out_refs[0][...] += part

    return pl.pallas_call(
        body, name=name, grid=(nj, ni, nk) if j_outer else (ni, nj, nk),
        in_specs=[mk(a_spec), mk(b_spec)] + [mk(s) for _, s in extras],
        out_specs=[mk(s) for s in out_specs], out_shape=list(out_shape),
        compiler_params=_params(3),
    )(a, b, *[x for x, _ in extras])


def _bs(shape, fn):
    return pl.BlockSpec(shape, fn)


def _where_am_i():
    x, y, c = lax.axis_index("x"), lax.axis_index("y"), lax.axis_index("c")
    return x, y, c


def _dev_index(p):
    return 4 * p[0] + 2 * p[1] + p[2]


def _slab(ref, axis, idx, size):
    sl = [slice(None)] * len(ref.shape)
    sl[axis] = pl.ds(idx * size, size)
    return ref.at[tuple(sl)]


def _all_gather(name, items):
    n = len(items)
    shapes = []
    for shard, axis in items:
        s = list(shard.shape)
        s[axis] *= N_DEV
        shapes.append(jax.ShapeDtypeStruct(tuple(s), shard.dtype))

    def body(*refs):
        in_refs, out_refs = refs[:n], refs[n:2 * n]
        send_sems, recv_sems, local_sems = refs[2 * n:]
        x, y, c = _where_am_i()
        me, sibling = (x, y, c), (x, y, 1 - c)
        chips = [(1 - x, y), (x, 1 - y), (1 - x, 1 - y)]

        def blk(a, p):
            axis = items[a][1]
            return _slab(out_refs[a], axis, _dev_index(p), items[a][0].shape[axis])

        def copy(a, k, block, to, src=None):
            return pltpu.make_async_remote_copy(
                src_ref=blk(a, block) if src is None else src, dst_ref=blk(a, block),
                send_sem=send_sems.at[a, k], recv_sem=recv_sems.at[a, k], device_id=to, device_id_type=MESH)

        mine = [pltpu.make_async_copy(in_refs[a], blk(a, me), local_sems.at[a]) for a in range(n)]
        for cp in mine:
            cp.start()
        first = []
        for a in range(n):
            first.append(copy(a, 0, me, sibling, src=in_refs[a]))
            first += [copy(a, 1 + j, me, (*chip, c), src=in_refs[a]) for j, chip in enumerate(chips)]
        for cp in first:
            cp.start()
        passed = []
        for a in range(n):
            for j, chip in enumerate(chips):
                copy(a, 1 + j, (*chip, c), me).wait_recv()
                fw = copy(a, 4 + j, (*chip, c), sibling)
                fw.start()
                passed.append(fw)
        for a in range(n):
            copy(a, 0, sibling, me).wait_recv()
            for j, chip in enumerate(chips):
                copy(a, 4 + j, (*chip, 1 - c), me).wait_recv()
        for cp in first + passed:
            cp.wait_send()
        for cp in mine:
            cp.wait()

    outs = pl.pallas_call(
        body, name=name, out_shape=shapes, in_specs=[ANY] * n, out_specs=[ANY] * n,
        scratch_shapes=[pltpu.SemaphoreType.DMA((n, 7)), pltpu.SemaphoreType.DMA((n, 7)),
                        pltpu.SemaphoreType.DMA((n,))],
    )(*[s for s, _ in items])
    return list(outs)


HBM = pl.BlockSpec(memory_space=pltpu.HBM)
SEM = pl.BlockSpec(memory_space=pltpu.SEMAPHORE)
DATAFLOW = pltpu.SideEffectType.DATAFLOW_SIDE_EFFECTING


def _in_hbm(a):
    return pltpu.with_memory_space_constraint(a, pltpu.HBM)


def _token_shape():
    return jax.ShapeDtypeStruct((8, 128), F32)


def _split_start(name, n_sems, bufs, issue):
    nb = len(bufs)

    def body(*refs):
        issue(refs[:nb], refs[nb], refs[nb + 1])
        refs[-1][...] = jnp.zeros((8, 128), F32)

    outs = pl.pallas_call(
        body, name=name,
        out_shape=(pltpu.SemaphoreType.DMA((n_sems,)), pltpu.SemaphoreType.DMA((n_sems,)),
                   *[pltpu.HBM(b.shape, b.dtype) for b in bufs], _token_shape()),
        in_specs=[HBM] * nb, out_specs=(SEM, SEM, *[HBM] * nb, pl.BlockSpec(memory_space=pltpu.VMEM)),
        input_output_aliases={i: 2 + i for i in range(nb)},
        compiler_params=pltpu.CompilerParams(has_side_effects=DATAFLOW),
    )(*[_in_hbm(b) for b in bufs])
    return outs[0], outs[1], list(outs[2:2 + nb]), outs[-1]


def _split_relay(name, n_sems, sems, bufs, after, relay):
    nb = len(bufs)

    def body(*refs):
        relay(refs[:nb], refs[nb], refs[nb + 1], refs[nb + 3], refs[nb + 4])
        refs[-1][...] = jnp.zeros((8, 128), F32)

    outs = pl.pallas_call(
        body, name=name,
        out_shape=(pltpu.SemaphoreType.DMA((n_sems,)), pltpu.SemaphoreType.DMA((n_sems,)),
                   *[pltpu.HBM(b.shape, b.dtype) for b in bufs], _token_shape()),
        in_specs=[HBM] * nb + [SEM, SEM, ANY],
        out_specs=(SEM, SEM, *[HBM] * nb, pl.BlockSpec(memory_space=pltpu.VMEM)),
        input_output_aliases={i: 2 + i for i in range(nb)},
        compiler_params=pltpu.CompilerParams(has_side_effects=DATAFLOW),
    )(*bufs, sems[0], sems[1], after)
    return outs[0], outs[1], list(outs[2:2 + nb]), outs[-1]


def _split_wait(name, sems, bufs, after, finish):
    nb = len(bufs)

    def body(*refs):
        finish(refs[:nb], refs[nb], refs[nb + 1])

    outs = pl.pallas_call(
        body, name=name, out_shape=[pltpu.HBM(b.shape, b.dtype) for b in bufs],
        in_specs=[HBM] * nb + [SEM, SEM, ANY], out_specs=[HBM] * nb,
        input_output_aliases={i: i for i in range(nb)},
        compiler_params=pltpu.CompilerParams(has_side_effects=DATAFLOW),
    )(*bufs, sems[0], sems[1], after)
    return list(outs)


def _place(name, items, dtype, after):
    ids = jnp.reshape(_dev_index(_where_am_i()), (1,)).astype(jnp.int32)
    outs = []
    for a, (shard, axis) in enumerate(items):
        rows, cols = shard.shape[-2], shard.shape[-1]
        tr = rows
        while tr * cols * shard.dtype.itemsize > 4 * 1024 * 1024 and tr % 32 == 0:
            tr //= 2
        nt = rows // tr
        full = list(shard.shape)
        full[axis] *= N_DEV
        if shard.ndim == 2 and axis == 0:
            in_spec = _bs((tr, cols), lambda i, ids: (i, 0))
            out_spec = _bs((tr, cols), lambda i, ids, nt=nt: (ids[0] * nt + i, 0))
        elif shard.ndim == 2 and axis == 1:
            in_spec = _bs((tr, cols), lambda i, ids: (i, 0))
            out_spec = _bs((tr, cols), lambda i, ids: (i, ids[0]))
        else:
            assert shard.ndim == 3 and axis == 0 and shard.shape[0] == 1
            in_spec = _bs((None, tr, cols), lambda i, ids: (0, i, 0))
            out_spec = _bs((None, tr, cols), lambda i, ids: (ids[0], i, 0))

        def body(ids_ref, in_ref, after_ref, out_ref):
            del ids_ref, after_ref
            out_ref[...] = in_ref[...].astype(out_ref.dtype)

        outs.append(pl.pallas_call(
            body, name=f"{name}{a}",
            grid_spec=pltpu.PrefetchScalarGridSpec(
                num_scalar_prefetch=1, grid=(nt,), in_specs=[in_spec, ANY], out_specs=out_spec),
            out_shape=jax.ShapeDtypeStruct(tuple(full), dtype), compiler_params=_params(1),
        )(ids, shard, after))
    return outs


class _SplitGather:
    def __init__(self, name, items, dtype, after):
        self.name, self.items, self.n = name, items, len(items)
        fulls = _place(name + "_place", items, dtype, after)
        n = self.n

        def issue(refs, send, recv):
            me, sibling, chips, c = self._geometry()
            for a in range(n):
                self._copy1(refs, send, recv, a, 0, me, sibling).start()
                for j, chip in enumerate(chips):
                    self._copy1(refs, send, recv, a, 1 + j, me, (*chip, c)).start()

        self.send, self.recv, self.bufs, self.token = _split_start(name + "_start", 4 * n, fulls, issue)

    @staticmethod
    def _geometry():
        x, y, c = _where_am_i()
        return (x, y, c), (x, y, 1 - c), [(1 - x, y), (x, 1 - y), (1 - x, 1 - y)], c

    def _blk(self, refs, a, p):
        shard, axis = self.items[a]
        return _slab(refs[a], axis, _dev_index(p), shard.shape[axis])

    def _copy1(self, refs, send, recv, a, k, owner, to):
        return pltpu.make_async_remote_copy(
            src_ref=self._blk(refs, a, owner), dst_ref=self._blk(refs, a, owner), send_sem=send.at[4 * a + k],
            recv_sem=recv.at[4 * a + k], device_id=to, device_id_type=MESH)

    def _copy2(self, refs, send, recv, a, j, owner, to):
        return pltpu.make_async_remote_copy(
            src_ref=self._blk(refs, a, owner), dst_ref=self._blk(refs, a, owner), send_sem=send.at[3 * a + j],
            recv_sem=recv.at[3 * a + j], device_id=to, device_id_type=MESH)

    def relay(self, after):
        n = self.n

        def relay(refs, send_in, recv_in, send_out, recv_out):
            me, sibling, chips, c = self._geometry()
            for a in range(n):
                for j, chip in enumerate(chips):
                    self._copy1(refs, send_in, recv_in, a, 1 + j, (*chip, c), me).wait_recv()
                    self._copy2(refs, send_out, recv_out, a, j, (*chip, c), sibling).start()
            for a in range(n):
                self._copy1(refs, send_in, recv_in, a, 0, sibling, me).wait_recv()
                for k in range(4):
                    self._copy1(refs, send_in, recv_in, a, k, me, sibling).wait_send()

        self.send, self.recv, self.bufs, self.token = _split_relay(
            self.name + "_relay", 3 * n, (self.send, self.recv), self.bufs, after, relay)
        return self.token

    def wait(self, after):
        n = self.n

        def finish(refs, send, recv):
            me, sibling, chips, c = self._geometry()
            for a in range(n):
                for j, chip in enumerate(chips):
                    self._copy2(refs, send, recv, a, j, (*chip, 1 - c), me).wait_recv()
                    self._copy2(refs, send, recv, a, j, (*chip, c), sibling).wait_send()

        return _split_wait(self.name + "_wait", (self.send, self.recv), self.bufs, after, finish)


class _SplitReduceScatter:
    def __init__(self, name, grads):
        self.name, self.n = name, len(grads)
        n = self.n
        g4 = [g.reshape(4, 2, *g.shape[1:]) for g in grads]
        land = [lax.empty((4, 1, *g.shape[1:]), g.dtype) for g in grads]

        def issue(refs, send, recv):
            for a in range(n):
                self._swap(refs, send, recv, a).start()

        self.send, self.recv, self.bufs, self.token = _split_start(name + "_d2d_start", n, g4 + land, issue)

    def _swap(self, refs, send, recv, a):
        x, y, c = _where_am_i()
        return pltpu.make_async_remote_copy(
            src_ref=refs[a].at[:, pl.ds(1 - c, 1)], dst_ref=refs[self.n + a], send_sem=send.at[a], recv_sem=recv.at[a],
            device_id=(x, y, 1 - c), device_id_type=MESH)

    def _hop(self, refs, send, recv, a, m):
        x, y, c = _where_am_i()
        px = (1 - x) if m & 2 else x
        py = (1 - y) if m & 1 else y
        return pltpu.make_async_remote_copy(
            src_ref=refs[a].at[2 * px + py], dst_ref=refs[self.n + a].at[m - 1], send_sem=send.at[3 * a + m - 1],
            recv_sem=recv.at[3 * a + m - 1], device_id=(px, py, c), device_id_type=MESH)

    def combine_and_send(self, after):
        n = self.n

        def finish(refs, send, recv):
            for a in range(n):
                self._swap(refs, send, recv, a).wait()

        bufs = _split_wait(self.name + "_d2d_wait", (self.send, self.recv), self.bufs, after, finish)
        x, y, c = _where_am_i()
        ids = jnp.stack([c, 2 * x + y]).astype(jnp.int32)
        self.own, sums = [], []
        for a in range(n):
            own, hb = _pair_sum(f"{self.name}_sum{a}", bufs[a], bufs[n + a], ids)
            self.own.append(own)
            sums.append(hb)
        land = [lax.empty((3, *h.shape[1:]), h.dtype) for h in sums]

        def issue(refs, send, recv):
            for a in range(n):
                for m in (1, 2, 3):
                    self._hop(refs, send, recv, a, m).start()

        self.send, self.recv, self.bufs, self.token = _split_start(self.name + "_ici_start", 3 * n, sums + land, issue)
        return self.token

    def wait(self, after):
        n = self.n

        def finish(refs, send, recv):
            for a in range(n):
                for m in (1, 2, 3):
                    self._hop(refs, send, recv, a, m).wait()

        bufs = _split_wait(self.name + "_ici_wait", (self.send, self.recv), self.bufs, after, finish)
        return list(zip(self.own, bufs[n:]))


def _pair_sum(name, g4, land, ids):
    rows, cols = g4.shape[2], g4.shape[3]
    tr = rows
    while tr * cols * 2 > 1024 * 1024 and tr % 32 == 0:
        tr //= 2

    def body(ids_ref, g_ref, l_ref, own_ref, sum_ref):
        h = g_ref[...].astype(F32) + l_ref[...].astype(F32)
        sum_ref[...] = h.astype(sum_ref.dtype)

        @pl.when(pl.program_id(1) == ids_ref[1])
        def _():
            own_ref[...] = h

    return pl.pallas_call(
        body, name=name,
        grid_spec=pltpu.PrefetchScalarGridSpec(
            num_scalar_prefetch=1, grid=(rows // tr, 4),
            in_specs=[_bs((None, None, tr, cols), lambda i, q, ids: (q, ids[0], i, 0)),
                      _bs((None, None, tr, cols), lambda i, q, ids: (q, 0, i, 0))],
            out_specs=[_bs((tr, cols), lambda i, q, ids: (i, 0)), _bs((None, tr, cols), lambda i, q, ids: (q, i, 0))]),
        out_shape=[jax.ShapeDtypeStruct((rows, cols), F32), jax.ShapeDtypeStruct((4, rows, cols), g4.dtype)],
        compiler_params=_params(2),
    )(ids, g4, land)


def _win_sum(ext, w, off):
    s = ext + _shift(ext, -1)
    if w >= 4:
        s = _shift(s, -1) + _shift(s, 1)
    if w >= 8:
        s = _shift(s, -2) + _shift(s, 2)
    if w >= 16:
        s = _shift(s, -4) + _shift(s, 4)
    return _shift(s, off) if off else s


def _inv_count(r0, t, w, seq):
    pos = r0 + lax.broadcasted_iota(jnp.int32, (t, 1), 0)
    cnt = jnp.minimum(pos + w // 2, seq) - jnp.maximum(pos - w // 2, 0)
    return 1.0 / cnt.astype(F32)


def _pool_fwd(p3, w_pool, pool_scale, seq, d_model):
    dp = d_model // 2
    pg = dp // len(POOL_WINDOWS)
    t = min(128, seq)
    n_chunks = seq // t
    h = WIN_HALO

    def body(u_ref, w_ref, sc_ref, d_ref, y_ref, pad_ref):
        g = pl.program_id(0)
        zeros = jnp.zeros((h, pg), F32)
        pad_ref[0:h, :] = zeros
        pad_ref[h + seq:h + seq + h, :] = zeros

        def fill(ci, _):
            r0 = pl.multiple_of(ci * t, t)
            pad_ref[pl.ds(h + r0, t), :] = u_ref[pl.ds(r0, t), :]
            return 0

        lax.fori_loop(0, n_chunks, fill, 0)
        wmat = w_ref[...]
        scale = sc_ref[...]
        for gi, w in enumerate(POOL_WINDOWS):
            @pl.when(g == gi)
            def _(w=w):
                def chunk(ci, _):
                    r0 = pl.multiple_of(ci * t, t)
                    ext = pad_ref[pl.ds(r0, t + 2 * h), :]
                    mean = _win_sum(ext, w, 0)[h:h + t, :] * _inv_count(r0, t, w, seq)
                    d = (mean - ext[h:h + t, :]).astype(BF16)
                    d_ref[pl.ds(r0, t), :] = d
                    q = jnp.dot(d, wmat, preferred_element_type=F32)
                    y_ref[pl.ds(r0, t), :] = (q * scale).astype(BF16)
                    return 0

                lax.fori_loop(0, n_chunks, chunk, 0)

    return pl.pallas_call(
        body, name="pool_fwd", grid=(len(POOL_WINDOWS),),
        in_specs=[_bs((None, seq, pg), lambda g: (0, 0, g)), _bs((None, pg, pg), lambda g: (g, 0, 0)),
                  _bs((1, pg), lambda g: (0, g))],
        out_specs=[_bs((seq, pg), lambda g: (0, g)), _bs((seq, pg), lambda g: (0, g))],
        out_shape=[jax.ShapeDtypeStruct((seq, dp), BF16), jax.ShapeDtypeStruct((seq, d_model), BF16)],
        scratch_shapes=[pltpu.VMEM((seq + 2 * h, pg), F32)],
        compiler_params=_params(1),
    )(p3, w_pool, pool_scale)


def _pool_bwd(d, dy, w_pool, pool_scale, token, seq, d_model):
    dp = d_model // 2
    pg = dp // len(POOL_WINDOWS)
    t = min(128, seq)
    n_chunks = seq // t
    h = WIN_HALO
    tn_dims = (((0,), (0,)), ((), ()))
    nt_dims = (((1,), (1,)), ((), ()))

    def body(d_ref, dy_ref, w_ref, sc_ref, tok_ref, du_ref, dwb_ref, dsc_ref, pad_ref, dd_ref, dw_ref):
        del tok_ref
        g = pl.program_id(0)
        zeros = jnp.zeros((h, pg), F32)
        pad_ref[0:h, :] = zeros
        pad_ref[h + seq:h + seq + h, :] = zeros
        wmat = w_ref[...]
        scale = sc_ref[...]
        for gi, w in enumerate(POOL_WINDOWS):
            @pl.when(g == gi)
            def _(w=w):
                dw_ref[...] = jnp.zeros((pg, pg), F32)

                def first(ci, dsc):
                    r0 = pl.multiple_of(ci * t, t)
                    dv = d_ref[pl.ds(r0, t), :]
                    dyv = dy_ref[pl.ds(r0, t), :]
                    q = jnp.dot(dv, wmat, preferred_element_type=F32)
                    dsc = dsc + jnp.sum(dyv * q, axis=0, keepdims=True)
                    dq = (dyv * scale).astype(BF16)
                    dw_ref[...] += lax.dot_general(dv, dq, tn_dims, preferred_element_type=F32)
                    dd = lax.dot_general(dq, wmat, nt_dims, preferred_element_type=F32)
                    dd_ref[pl.ds(r0, t), :] = dd
                    pad_ref[pl.ds(h + r0, t), :] = dd * _inv_count(r0, t, w, seq)
                    return dsc

                dsc_ref[...] = lax.fori_loop(0, n_chunks, first, jnp.zeros((1, pg), F32))
                dwb_ref[...] = dw_ref[...].reshape(N_DEV, pg // N_DEV, pg).astype(BF16)

                def second(ci, _):
                    r0 = pl.multiple_of(ci * t, t)
                    ext = pad_ref[pl.ds(r0, t + 2 * h), :]
                    back = _win_sum(ext, w, 1)[h:h + t, :]
                    du_ref[pl.ds(r0, t), :] = (back - dd_ref[pl.ds(r0, t), :]).astype(BF16)
                    return 0

                lax.fori_loop(0, n_chunks, second, 0)

    return pl.pallas_call(
        body, name="pool_bwd", grid=(len(POOL_WINDOWS),),
        in_specs=[_bs((seq, pg), lambda g: (0, g)), _bs((seq, pg), lambda g: (0, g)),
                  _bs((None, pg, pg), lambda g: (g, 0, 0)), _bs((1, pg), lambda g: (0, g)),
                  _bs((8, 128), lambda g: (0, 0))],
        out_specs=[_bs((seq, pg), lambda g: (0, g)), _bs((N_DEV, None, pg // N_DEV, pg), lambda g: (0, g, 0, 0)),
                   _bs((1, pg), lambda g: (0, g))],
        out_shape=[jax.ShapeDtypeStruct((seq, 3 * dp), BF16),
                   jax.ShapeDtypeStruct((N_DEV, len(POOL_WINDOWS), pg // N_DEV, pg), BF16),
                   jax.ShapeDtypeStruct((1, dp), F32)],
        scratch_shapes=[pltpu.VMEM((seq + 2 * h, pg), F32), pltpu.VMEM((seq, pg), F32), pltpu.VMEM((pg, pg), F32)],
        compiler_params=_params(1),
    )(d, dy, w_pool, pool_scale, token)


def _tile_scan(n_tiles, lanes, loads, stores):
    row = lax.broadcasted_iota(jnp.int32, (8, lanes), 0)
    group = 8

    def local_scan(n, k):
        aa, bb = loads[n](k)
        for sh in (1, 2, 4):
            if n == 0:
                ok = row >= sh
                ap = jnp.where(ok, pltpu.roll(aa, sh, 0), 1.0)
                bp = jnp.where(ok, pltpu.roll(bb, sh, 0), 0.0)
            else:
                ok = row < 8 - sh
                ap = jnp.where(ok, pltpu.roll(aa, 8 - sh, 0), 1.0)
                bp = jnp.where(ok, pltpu.roll(bb, 8 - sh, 0), 0.0)
            bb = aa * bp + bb
            aa = aa * ap
        return aa, bb

    def step(s, carry):
        carry = list(carry)
        for n in range(2):
            tiles = [s * group + u if n == 0 else n_tiles - 1 - (s * group + u) for u in range(group)]
            local = [local_scan(n, k) for k in tiles]
            for k, (aa, bb) in zip(tiles, local):
                hh = bb + aa * carry[n]
                stores[n](k, hh)
                carry[n] = jnp.broadcast_to(hh[7:8, :] if n == 0 else hh[0:1, :], (8, lanes))
        return tuple(carry)

    zeros = jnp.zeros((8, lanes), F32)
    lax.fori_loop(0, n_tiles // group, step, (zeros, zeros))


def _gate_preacts(xc, wcat_ref):
    xcb = xc.astype(BF16)
    return xcb, jnp.dot(xcb, wcat_ref[...], preferred_element_type=F32)


def _gates(pre, n, pk_ref, sp):
    lh = pre.shape[1] // 4
    r = _sigmoid(pre[:, (2 * n) * lh:(2 * n + 1) * lh] + pk_ref[pl.ds(4 + n, 1), :])
    i = _sigmoid(pre[:, (2 * n + 1) * lh:(2 * n + 2) * lh] + pk_ref[pl.ds(6 + n, 1), :])
    log_a = (-RG_C * r) * sp[n]
    a = jnp.exp(log_a)
    x = 2.0 * log_a
    one_minus_a2 = jnp.where(x > -0.01, -(x * (1.0 + x * (0.5 + x * (1.0 / 6.0)))), 1.0 - a * a)
    m = jnp.sqrt(one_minus_a2)
    return r, i, a, m


def _conv_chunk(upad_ref, pk_ref, cb, r0, t):
    ext = upad_ref[pl.ds(r0, t + 2 * CONV_HALO), :]
    acc = pk_ref[pl.ds(1, 1), :] * ext
    for k in (0, 2, 3):
        acc = acc + pk_ref[pl.ds(k, 1), :] * _shift(ext, k - 1)
    return acc[CONV_HALO:CONV_HALO + t, :] + cb, ext


def _lru_fwd(p3, y_in, pack, conv_b, wcat, token, seq, d_model):
    dl = d_model // 2
    lh = dl // N_HEADS
    t = min(128, seq)
    n_chunks = seq // t
    seg = seq // 8
    hal = CONV_HALO
    first_rec_block = (d_model - dl) // lh

    def body(ur_ref, ug_ref, pk_ref, cb_ref, wcat_ref, yin_ref, tok_ref, y_ref, h0_ref, h1_ref,
             upad, a_scr, b_scr):
        del yin_ref, tok_ref
        zeros = jnp.zeros((hal, lh), F32)
        upad[0:hal, :] = zeros
        upad[hal + seq:hal + seq + hal, :] = zeros
        for ref in (h0_ref, h1_ref):
            ref[0:hal, :] = zeros
            ref[hal + seq:hal + seq + hal, :] = zeros

        def fill(ci, _):
            r0 = pl.multiple_of(ci * t, t)
            upad[pl.ds(hal + r0, t), :] = ur_ref[pl.ds(r0, t), :]
            return 0

        lax.fori_loop(0, n_chunks, fill, 0)
        cb = cb_ref[...]
        sp = [_softplus(-pk_ref[pl.ds(8 + n, 1), :]) for n in range(2)]

        def chunk(ci, _):
            r0 = pl.multiple_of(ci * t, t)
            xc, _ext = _conv_chunk(upad, pk_ref, cb, r0, t)
            _, pre = _gate_preacts(xc, wcat_ref)
            for n in range(2):
                _, i, a, m = _gates(pre, n, pk_ref, sp)
                a_scr[n, pl.ds(r0, t), :] = a
                b_scr[n, pl.ds(r0, t), :] = (m * i) * xc
            return 0

        lax.fori_loop(0, n_chunks, chunk, 0, unroll=2)

        def load(n):
            def get(k):
                at = pl.ds(pl.multiple_of(k * 8, 8), 8)
                return a_scr[n, at, :], b_scr[n, at, :]
            return get

        def store(ref):
            def put(k, v):
                ref[pl.ds(pl.multiple_of(hal + k * 8, 8), 8), :] = v
            return put

        _tile_scan(seq // 8, lh, [load(0), load(1)], [store(h0_ref), store(h1_ref)])

        def out(ci, _):
            r0 = pl.multiple_of(ci * t, t)
            hsum = h0_ref[pl.ds(hal + r0, t), :] + h1_ref[pl.ds(hal + r0, t), :]
            gl, _dg = _gelu_and_grad(ug_ref[pl.ds(r0, t), :])
            y_ref[pl.ds(r0, t), :] = (hsum * gl).astype(BF16)
            return 0

        lax.fori_loop(0, n_chunks, out, 0)

    return pl.pallas_call(
        body, name="lru_fwd", grid=(N_HEADS,),
        in_specs=[_bs((None, seq, lh), lambda h: (1, 0, h)), _bs((None, seq, lh), lambda h: (2, 0, h)),
                  _bs((None, SMALL_ROWS, lh), lambda h: (h, 0, 0)), _bs((1, lh), lambda h: (0, h)),
                  _bs((None, lh, 4 * lh), lambda h: (h, 0, 0)),
                  ANY, _bs((8, 128), lambda h: (0, 0))],
        out_specs=[_bs((seq, lh), lambda h: (0, first_rec_block + h)),
                   _bs((seq + 2 * hal, lh), lambda h: (0, h)), _bs((seq + 2 * hal, lh), lambda h: (0, h))],
        out_shape=[jax.ShapeDtypeStruct((seq, d_model), BF16), jax.ShapeDtypeStruct((seq + 2 * hal, dl), F32),
                   jax.ShapeDtypeStruct((seq + 2 * hal, dl), F32)],
        scratch_shapes=[pltpu.VMEM((seq + 2 * hal, lh), F32), pltpu.VMEM((2, seq, lh), F32),
                        pltpu.VMEM((2, seq, lh), F32)],
        input_output_aliases={5: 0},
        compiler_params=_params(1),
    )(p3, p3, pack, conv_b, wcat, y_in, token)


def _lru_bwd(p3, dy, h0p, h1p, dproj_in, pack, conv_b, wcat, token, seq, d_model):
    dl = d_model // 2
    lh = dl // N_HEADS
    t = min(128, seq)
    n_chunks = seq // t
    seg = seq // 8
    hal = CONV_HALO
    first_rec_block = (d_model - dl) // lh
    tn_dims = (((0,), (0,)), ((), ()))
    nt_dims = (((1,), (1,)), ((), ()))

    def body(ur_ref, ug_ref, dy_ref, h0_ref, h1_ref, pk_ref, cb_ref, wcat_ref, tok_ref, din_ref,
             dproj_ref, dpk_ref, dcb_ref, dwcat_ref,
             upad, a_scr, dh_scr, g_scr, dxc_pad, dpr_ref, out_sems):
        del din_ref, tok_ref
        zeros = jnp.zeros((hal, lh), F32)
        for ref in (upad, dxc_pad):
            ref[0:hal, :] = zeros
            ref[hal + seq:hal + seq + hal, :] = zeros
        for n in range(2):
            a_scr[n, 0:hal, :] = zeros
            a_scr[n, hal + seq:hal + seq + hal, :] = zeros

        def fill(ci, _):
            r0 = pl.multiple_of(ci * t, t)
            upad[pl.ds(hal + r0, t), :] = ur_ref[pl.ds(r0, t), :]
            return 0

        lax.fori_loop(0, n_chunks, fill, 0)
        cb = cb_ref[...]
        lam = [pk_ref[pl.ds(8 + n, 1), :] for n in range(2)]
        sp = [_softplus(-lam[n]) for n in range(2)]

        def chunk1(ci, _):
            r0 = pl.multiple_of(ci * t, t)
            xc, _ext = _conv_chunk(upad, pk_ref, cb, r0, t)
            _, pre = _gate_preacts(xc, wcat_ref)
            for n in range(2):
                _, _, a, _ = _gates(pre, n, pk_ref, sp)
                a_scr[n, pl.ds(hal + r0, t), :] = a
            hsum = h0_ref[pl.ds(hal + r0, t), :] + h1_ref[pl.ds(hal + r0, t), :]
            gl, dgl = _gelu_and_grad(ug_ref[pl.ds(r0, t), :])
            dyv = dy_ref[pl.ds(r0, t), :]
            dh_scr[pl.ds(r0, t), :] = dyv * gl
            dpr_ref[1, pl.ds(r0, t), :] = ((dyv * hsum) * dgl).astype(BF16)
            return 0

        lax.fori_loop(0, n_chunks, chunk1, 0, unroll=2)

        def load(n):
            def get(k):
                r0 = pl.multiple_of(k * 8, 8)
                if n == 0:
                    coef = _shift(a_scr[0, pl.ds(pl.multiple_of(hal + r0, 8), 16), :], 1)[0:8, :]
                else:
                    coef = _shift(a_scr[1, pl.ds(pl.multiple_of(hal + r0 - 8, 8), 16), :], -1)[8:16, :]
                return coef, dh_scr[pl.ds(r0, 8), :]
            return get

        def store(n):
            def put(k, v):
                g_scr[n, pl.ds(pl.multiple_of(k * 8, 8), 8), :] = v
            return put

        _tile_scan(seq // 8, lh, [load(1), load(0)], [store(1), store(0)])

        dwcat_ref[...] = jnp.zeros((lh, 4 * lh), F32)

        def chunk3(ci, carry):
            dba, dbi, dlam, dcb = carry
            r0 = pl.multiple_of(ci * t, t)
            xc, _ext = _conv_chunk(upad, pk_ref, cb, r0, t)
            xcb, pre = _gate_preacts(xc, wcat_ref)
            dxc = jnp.zeros((t, lh), F32)
            dba, dbi, dlam = list(dba), list(dbi), list(dlam)
            dpre = []
            for n in range(2):
                r, i, a, m = _gates(pre, n, pk_ref, sp)
                hext = (h0_ref if n == 0 else h1_ref)[pl.ds(r0, t + 2 * hal), :]
                hprev = _shift(hext, -1 if n == 0 else 1)[hal:hal + t, :]
                gb = g_scr[n, pl.ds(r0, t), :]
                da = gb * hprev
                dm = gb * i * xc
                di = gb * m * xc
                dxc = dxc + gb * (m * i)
                dlog_a = da * a - dm * (a * a) / m
                dr = dlog_a * (-RG_C * sp[n])
                dlam[n] = dlam[n] + jnp.sum(dlog_a * r, axis=0, keepdims=True)
                dpr = dr * r * (1.0 - r)
                dpi = di * i * (1.0 - i)
                dba[n] = dba[n] + jnp.sum(dpr, axis=0, keepdims=True)
                dbi[n] = dbi[n] + jnp.sum(dpi, axis=0, keepdims=True)
                dpre += [dpr.astype(BF16), dpi.astype(BF16)]
            dpre = jnp.concatenate(dpre, axis=1)
            dwcat_ref[...] += lax.dot_general(xcb, dpre, tn_dims, preferred_element_type=F32)
            dxc = dxc + lax.dot_general(dpre, wcat_ref[...], nt_dims, preferred_element_type=F32)
            dxc_pad[pl.ds(hal + r0, t), :] = dxc
            dcb = dcb + jnp.sum(dxc, axis=0, keepdims=True)
            return tuple(dba), tuple(dbi), tuple(dlam), dcb

        zr = jnp.zeros((1, lh), F32)
        def chunk3_pair(cj, carry):
            return chunk3(2 * cj + 1, chunk3(2 * cj, carry))

        dba, dbi, dlam, dcb = lax.fori_loop(0, n_chunks // 2, chunk3_pair, ((zr, zr), (zr, zr), (zr, zr), zr))
        dcb_ref[...] = dcb
        for n in range(2):
            dpk_ref[pl.ds(4 + n, 1), :] = dba[n]
            dpk_ref[pl.ds(6 + n, 1), :] = dbi[n]
            dpk_ref[pl.ds(8 + n, 1), :] = dlam[n] * (RG_C * jax.nn.sigmoid(-lam[n]))
        dpk_ref[pl.ds(10, SMALL_ROWS - 10), :] = jnp.zeros((SMALL_ROWS - 10, lh), F32)

        def chunk4(ci, dtap):
            r0 = pl.multiple_of(ci * t, t)
            gext = dxc_pad[pl.ds(r0, t + 2 * hal), :]
            uext = upad[pl.ds(r0, t + 2 * hal), :]
            gmid = gext[hal:hal + t, :]
            du = pk_ref[pl.ds(1, 1), :] * gext
            for k in (0, 2, 3):
                du = du + pk_ref[pl.ds(k, 1), :] * _shift(gext, 1 - k)
            dpr_ref[0, pl.ds(r0, t), :] = du[hal:hal + t, :].astype(BF16)
            out = []
            for k in range(4):
                usl = _shift(uext, k - 1)[hal:hal + t, :]
                out.append(dtap[k] + jnp.sum(gmid * usl, axis=0, keepdims=True))
            return tuple(out)

        dtap = lax.fori_loop(0, n_chunks, chunk4, (zr, zr, zr, zr))
        for k in range(4):
            dpk_ref[pl.ds(k, 1), :] = dtap[k]

        head = pl.program_id(0)
        outs = [pltpu.make_async_copy(
            dpr_ref.at[b], dproj_ref.at[:, pl.ds(pl.multiple_of((1 + b) * dl + head * lh, lh), lh)], out_sems.at[b])
            for b in range(2)]
        for cp in outs:
            cp.start()
        for cp in outs:
            cp.wait()

    return pl.pallas_call(
        body, name="lru_bwd", grid=(N_HEADS,),
        in_specs=[_bs((None, seq, lh), lambda h: (1, 0, h)), _bs((None, seq, lh), lambda h: (2, 0, h)),
                  _bs((seq, lh), lambda h: (0, first_rec_block + h)),
                  _bs((seq + 2 * hal, lh), lambda h: (0, h)), _bs((seq + 2 * hal, lh), lambda h: (0, h)),
                  _bs((None, SMALL_ROWS, lh), lambda h: (h, 0, 0)), _bs((1, lh), lambda h: (0, h)),
                  _bs((None, lh, 4 * lh), lambda h: (h, 0, 0)),
                  _bs((8, 128), lambda h: (0, 0)), ANY],
        out_specs=[ANY, _bs((None, SMALL_ROWS, lh), lambda h: (h, 0, 0)),
                   _bs((1, lh), lambda h: (0, h)), _bs((None, lh, 4 * lh), lambda h: (h, 0, 0))],
        out_shape=[jax.ShapeDtypeStruct((seq, 3 * dl), BF16), jax.ShapeDtypeStruct((N_HEADS, SMALL_ROWS, lh), F32),
                   jax.ShapeDtypeStruct((1, dl), F32), jax.ShapeDtypeStruct((N_HEADS, lh, 4 * lh), F32)],
        scratch_shapes=[pltpu.VMEM((seq + 2 * hal, lh), F32), pltpu.VMEM((2, seq + 2 * hal, lh), F32),
                        pltpu.VMEM((seq, lh), F32), pltpu.VMEM((2, seq, lh), F32),
                        pltpu.VMEM((seq + 2 * hal, lh), F32), pltpu.VMEM((2, seq, lh), BF16),
                        pltpu.SemaphoreType.DMA((2,))],
        input_output_aliases={9: 0},
        compiler_params=_params(1),
    )(p3, p3, dy, h0p, h1p, pack, conv_b, wcat, token, dproj_in)


class _tiles:
    def __init__(self, seq, d_model, d_ff):
        self.rows = min(1024, seq)
        self.ln_rows = min(256, seq)
        self.ff_cols = min(1024, d_ff)
        self.ff_split = 4
        self.ff_k = min(2048, d_ff)
        self.grad_rows = 512


def _ln_loss_bwd(ffn, x1, tgt, g, b, tr):
    seq, d = ffn.shape

    def body(f_ref, x_ref, t_ref, g_ref, b_ref, dz_ref, dzb_ref, dg_ref, db_ref, loss_ref):
        i = pl.program_id(0)
        gv = g_ref[...]
        z = ALPHA * x_ref[...] + f_ref[...]
        y, xhat, rstd = _ln_fwd(z, gv, b_ref[...])
        err = y - t_ref[...]
        part = 0.5 * jnp.sum(jnp.mean(err * err, axis=-1, keepdims=True), axis=0, keepdims=True)
        dz, dg, db = _ln_bwd(err * (1.0 / d), xhat, rstd, gv)
        dz_ref[...] = dz
        dzb_ref[...] = dz.astype(BF16)
        _acc_rows(dg_ref, i == 0, dg)
        _acc_rows(db_ref, i == 0, db)
        _acc_rows(loss_ref, i == 0, jnp.broadcast_to(part, (8, 128)))

    row = _bs((tr, d), lambda i: (i, 0))
    vec = _bs((1, d), lambda i: (0, 0))
    return pl.pallas_call(
        body, name="ln_ffn_loss", grid=(seq // tr,), in_specs=[row, row, row, vec, vec],
        out_specs=[row, row, vec, vec, _bs((8, 128), lambda i: (0, 0))],
        out_shape=[jax.ShapeDtypeStruct((seq, d), F32), jax.ShapeDtypeStruct((seq, d), BF16),
                   jax.ShapeDtypeStruct((1, d), F32), jax.ShapeDtypeStruct((1, d), F32),
                   jax.ShapeDtypeStruct((8, 128), F32)],
        compiler_params=_params(1),
    )(ffn, x1, tgt, g, b)


def _ln_bwd_rows(dx_branch, dres, z, g, b, tr):
    seq, d = z.shape

    def body(a_ref, r_ref, z_ref, g_ref, b_ref, dz_ref, dzb_ref, dg_ref, db_ref):
        i = pl.program_id(0)
        gv = g_ref[...]
        _, xhat, rstd = _ln_fwd(z_ref[...], gv, b_ref[...])
        dz, dg, db = _ln_bwd(ALPHA * r_ref[...] + a_ref[...], xhat, rstd, gv)
        dz_ref[...] = dz
        dzb_ref[...] = dz.astype(BF16)
        _acc_rows(dg_ref, i == 0, dg)
        _acc_rows(db_ref, i == 0, db)

    row = _bs((tr, d), lambda i: (i, 0))
    vec = _bs((1, d), lambda i: (0, 0))
    return pl.pallas_call(
        body, name="ln_mix_bwd", grid=(seq // tr,), in_specs=[row, row, row, vec, vec],
        out_specs=[row, row, vec, vec],
        out_shape=[jax.ShapeDtypeStruct((seq, d), F32), jax.ShapeDtypeStruct((seq, d), BF16),
                   jax.ShapeDtypeStruct((1, d), F32), jax.ShapeDtypeStruct((1, d), F32)],
        compiler_params=_params(1),
    )(dx_branch, dres, z, g, b)


def _adamw_values(w, g, m, v):
    m = ADAM_B1 * m + (1.0 - ADAM_B1) * g
    v = ADAM_B2 * v + (1.0 - ADAM_B2) * (g * g)
    m_hat = m / (1.0 - ADAM_B1 ** ADAM_STEP)
    v_hat = v / (1.0 - ADAM_B2 ** ADAM_STEP)
    delta = -ADAM_LR * (m_hat / (jnp.sqrt(v_hat) + ADAM_EPS) + ADAM_WD * w)
    return delta, m, v


def _sum_adamw(name, own, parts, w, m, v):
    rows, cols = w.shape
    n_parts = parts.shape[0]
    tr = rows
    min_rows = 8 if parts.dtype == F32 else 16
    while tr * cols * 4 > 1024 * 1024 and tr % (2 * min_rows) == 0:
        tr //= 2

    def body(*refs):
        if own is None:
            p_ref, w_ref, m_ref, v_ref, g_ref, d_ref, mo_ref, vo_ref = refs
            g = p_ref[0].astype(F32)
            rest = range(1, n_parts)
        else:
            o_ref, p_ref, w_ref, m_ref, v_ref, g_ref, d_ref, mo_ref, vo_ref = refs
            g = o_ref[...]
            rest = range(n_parts)
        for s in rest:
            g = g + p_ref[s].astype(F32)
        delta, mn, vn = _adamw_values(w_ref[...], g, m_ref[...], v_ref[...])
        g_ref[...] = g
        d_ref[...] = delta
        mo_ref[...] = mn
        vo_ref[...] = vn

    spec = _bs((tr, cols), lambda i: (i, 0))
    lead = [] if own is None else [own]
    return pl.pallas_call(
        body, name=name, grid=(rows // tr,),
        in_specs=[spec] * len(lead) + [_bs((n_parts, tr, cols), lambda i: (0, i, 0)), spec, spec, spec],
        out_specs=[spec] * 4, out_shape=[jax.ShapeDtypeStruct((rows, cols), F32)] * 4,
        compiler_params=_params(1),
    )(*lead, parts, w, m, v)


def _rows128(a):
    return a.reshape(-1, 128)


def kernel(x, ln_mix_g, ln_mix_b, w_in, w_pool, pool_scale, conv_w, conv_b, w_rg_a, b_rg_a, w_rg_i, b_rg_i, rg_lambda, w_out, ln_ffn_g, ln_ffn_b, w_mlp_in, w_mlp_out, loss_target, m_ln_mix_g, m_ln_mix_b, m_w_in, m_w_pool, m_pool_scale, m_conv_w, m_conv_b, m_w_rg_a, m_b_rg_a, m_w_rg_i, m_b_rg_i, m_rg_lambda, m_w_out, m_ln_ffn_g, m_ln_ffn_b, m_w_mlp_in, m_w_mlp_out, v_ln_mix_g, v_ln_mix_b, v_w_in, v_w_pool, v_pool_scale, v_conv_w, v_conv_b, v_w_rg_a, v_b_rg_a, v_w_rg_i, v_b_rg_i, v_rg_lambda, v_w_out, v_ln_ffn_g, v_ln_ffn_b, v_w_mlp_in, v_w_mlp_out):
    seq, d_model = x.shape[1], x.shape[2]
    dh = d_model // 2
    lh = dh // N_HEADS
    pg = dh // len(POOL_WINDOWS)
    d_ff = w_mlp_in.shape[2] * N_DEV
    assert lh == 128 and conv_w.shape[3] == lh and w_pool.shape[2] * N_DEV == pg

    xs = x[0]
    tgt = loss_target[0]

    def small_pack(cw, ba, bi, lam):
        return jnp.concatenate([cw.reshape(4, lh), ba.reshape(2, lh), bi.reshape(2, lh), lam.reshape(2, lh),
                                jnp.zeros((SMALL_ROWS - 10, lh), F32)], axis=0)

    pack_mine = small_pack(conv_w, b_rg_a, b_rg_i, rg_lambda)
    win_full, wpool_full, pack_full = _all_gather("gather_mixer", [
        (w_in[0].astype(BF16), 1), (w_pool[0].astype(BF16), 1), (pack_mine[None], 0)])
    wout_gather = _SplitGather("gather_w_out", [(w_out[0], 0)], BF16, after=pack_full)
    w1_gather = _SplitGather("gather_w_mlp_in", [(w_mlp_in[0], 1)], BF16, after=wout_gather.token)
    w2_gather = _SplitGather("gather_w_mlp_out", [(w_mlp_out[0], 0)], BF16, after=w1_gather.token)
    wcat = jnp.concatenate([w_rg_a[0, 0], w_rg_i[0, 0], w_rg_a[0, 1], w_rg_i[0, 1]], axis=-1).astype(BF16)
    vec = lambda i, j, k: (0, 0)
    row_full = lambda i, j, k: (i, 0)

    def after(token):
        return (token, _sp((8, 128), vec))

    def sds(shape, dtype):
        return jax.ShapeDtypeStruct(shape, dtype)

    def plain_epi(acc, i, ex, out):
        out[0][...] = acc

    def bf16_epi(acc, i, ex, out):
        out[0][...] = acc.astype(BF16)

    t = _tiles(seq, d_model, d_ff)

    (p3,) = _matmul(
        "proj", xs, win_full, _sp((t.rows, d_model), lambda i, j, k: (i, 0)), _sp((d_model, dh), lambda i, j, k: (0, j)),
        grid=(seq // t.rows, 3, 1), extras=[after(w2_gather.token)],
        out_shape=[sds((3, seq, dh), F32)], out_specs=[_sp((None, t.rows, dh), lambda i, j, k: (j, i, 0))],
        epilogue=plain_epi)

    d_pool, y_half = _pool_fwd(p3, wpool_full, pool_scale, seq, d_model)
    y, h0p, h1p = _lru_fwd(p3, y_half, pack_full, conv_b, wcat, wout_gather.relay(after=y_half), seq, d_model)
    (wout_full,) = wout_gather.wait(after=y)
    relay_token = w1_gather.relay(after=wout_full)

    def mix_epi(acc, i, ex, out):
        x_ref, g_ref, b_ref = ex[:3]
        z = ALPHA * x_ref[...] + acc
        x1, _, _ = _ln_fwd(z, g_ref[...], b_ref[...])
        out[0][...] = z
        out[1][...] = x1
        out[2][...] = x1.astype(BF16)

    z1, x1, x1b = _matmul(
        "mix_out", y, wout_full, _sp((t.ln_rows, d_model), row_full), _sp((d_model, d_model), vec, single=True),
        grid=(seq // t.ln_rows, 1, 1),
        extras=[(xs, _sp((t.ln_rows, d_model), row_full)), (ln_mix_g, _sp((1, d_model), vec)),
                (ln_mix_b, _sp((1, d_model), vec)), after(relay_token)],
        out_shape=[sds((seq, d_model), F32), sds((seq, d_model), F32), sds((seq, d_model), BF16)],
        out_specs=[_sp((t.ln_rows, d_model), row_full)] * 3, epilogue=mix_epi)
    (w1_full,) = w1_gather.wait(after=x1b)

    def mlp_in_epi(acc, i, ex, out, cols):
        h = jnp.maximum(acc, 0.0)
        out[0][:, cols] = (h * h).astype(BF16)

    (hmid,) = _matmul(
        "mlp_in", x1b, w1_full, _sp((t.rows, d_model), lambda i, j, k: (i, 0)),
        _sp((d_model, t.ff_cols), lambda i, j, k: (0, j)),
        grid=(seq // t.rows, d_ff // t.ff_cols, 1), j_outer=True,
        out_shape=[sds((seq, d_ff), BF16)], out_specs=[_sp((t.rows, t.ff_cols), lambda i, j, k: (i, j))],
        epilogue=mlp_in_epi, n_split=t.ff_split)
    (w2_full,) = w2_gather.wait(after=w2_gather.relay(after=hmid))

    (ffn,) = _matmul(
        "mlp_out", hmid, w2_full, _sp((t.rows, t.ff_k), lambda i, j, k: (i, k)),
        _sp((t.ff_k, d_model), lambda i, j, k: (k, 0)),
        grid=(seq // t.rows, 1, d_ff // t.ff_k),
        out_shape=[sds((seq, d_model), F32)], out_specs=[_sp((t.rows, d_model), row_full)])
    dz2, dz2b, g_ffn_g, g_ffn_b, loss_part = _ln_loss_bwd(ffn, x1, tgt, ln_ffn_g, ln_ffn_b, t.ln_rows)

    (g_w2,) = _matmul(
        "grad_w_mlp_out", hmid, dz2b, _sp((seq, t.grad_rows), lambda i, j, k: (0, i)),
        _sp((seq, d_model), vec, single=True),
        grid=(d_ff // t.grad_rows, 1, 1), ta=True,
        out_shape=[sds((d_ff, d_model), BF16)], out_specs=[_sp((t.grad_rows, d_model), row_full)],
        epilogue=bf16_epi)
    scatter_w2 = _SplitReduceScatter("scatter_w_mlp_out", [g_w2.reshape(N_DEV, d_ff // N_DEV, d_model)])

    def dpre_epi(acc, i, ex, out, cols):
        out[0][:, cols] = (acc * (2.0 * jnp.sqrt(ex[0][:, cols].astype(F32)))).astype(BF16)

    (dpre,) = _matmul(
        "mlp_dpre", dz2b, w2_full, _sp((t.rows, d_model), lambda i, j, k: (i, 0)),
        _sp((t.ff_cols, d_model), lambda i, j, k: (j, 0)),
        grid=(seq // t.rows, d_ff // t.ff_cols, 1), j_outer=True, tb=True,
        extras=[(hmid, _sp((t.rows, t.ff_cols), lambda i, j, k: (i, j))), after(scatter_w2.token)],
        out_shape=[sds((seq, d_ff), BF16)], out_specs=[_sp((t.rows, t.ff_cols), lambda i, j, k: (i, j))],
        epilogue=dpre_epi, n_split=t.ff_split)
    token_w2 = scatter_w2.combine_and_send(after=dpre)

    (dx1_mlp,) = _matmul(
        "mlp_dx", dpre, w1_full, _sp((t.rows, t.ff_k), lambda i, j, k: (i, k)),
        _sp((d_model, t.ff_k), lambda i, j, k: (0, k)),
        grid=(seq // t.rows, 1, d_ff // t.ff_k), tb=True, extras=[after(token_w2)],
        out_shape=[sds((seq, d_model), F32)], out_specs=[_sp((t.rows, d_model), row_full)])
    dz1, dz1b, g_mix_g, g_mix_b = _ln_bwd_rows(dx1_mlp, dz2, z1, ln_mix_g, ln_mix_b, t.ln_rows)

    def block_epi(acc, i, ex, out):
        out[0][0] = acc.astype(BF16)

    fs = d_ff // N_DEV
    (g_w1,) = _matmul(
        "grad_w_mlp_in", x1b, dpre, _sp((seq, t.grad_rows), lambda i, j, k: (0, i)),
        _sp((seq, fs), lambda i, j, k: (0, j)),
        grid=(d_model // t.grad_rows, N_DEV, 1), j_outer=True, ta=True,
        out_shape=[sds((N_DEV, d_model, fs), BF16)],
        out_specs=[_sp((1, t.grad_rows, fs), lambda i, j, k: (j, i, 0))], epilogue=block_epi)
    scatter_w1 = _SplitReduceScatter("scatter_w_mlp_in", [g_w1])

    (dy,) = _matmul(
        "mix_dy", dz1b, wout_full, _sp((t.rows, d_model), lambda i, j, k: (i, 0)),
        _sp((dh, d_model), lambda i, j, k: (j, 0)),
        grid=(seq // t.rows, 2, 1), j_outer=True, tb=True, extras=[after(scatter_w1.token)],
        out_shape=[sds((seq, d_model), F32)], out_specs=[_sp((t.rows, dh), lambda i, j, k: (i, j))],
        epilogue=plain_epi)
    token_w1 = scatter_w1.combine_and_send(after=dy)
    (g_wout,) = _matmul(
        "grad_w_out", y, dz1b, _sp((seq, t.grad_rows), lambda i, j, k: (0, i)), _sp((seq, d_model), vec, single=True),
        grid=(d_model // t.grad_rows, 1, 1), ta=True, extras=[after(token_w1)],
        out_shape=[sds((d_model, d_model), BF16)], out_specs=[_sp((t.grad_rows, d_model), row_full)],
        epilogue=bf16_epi)
    g_wout = g_wout.reshape(N_DEV, d_model // N_DEV, d_model)

    dproj_pool, g_wpool, g_pscale = _pool_bwd(d_pool, dy, wpool_full, pool_scale, token_w1, seq, d_model)
    dproj, g_pack, g_convb, g_wcat = _lru_bwd(p3, dy, h0p, h1p, dproj_pool, pack_full, conv_b, wcat,
                                              token_w1, seq, d_model)
    g_wa = jnp.stack([g_wcat[:, :, 0:lh], g_wcat[:, :, 2 * lh:3 * lh]])
    g_wi = jnp.stack([g_wcat[:, :, lh:2 * lh], g_wcat[:, :, 3 * lh:4 * lh]])

    rep_parts = [_rows128(g_wa), _rows128(g_wi), _rows128(g_mix_g), _rows128(g_mix_b), _rows128(g_ffn_g),
                 _rows128(g_ffn_b), _rows128(g_pscale), _rows128(g_convb)]
    rep_rows = [p.shape[0] for p in rep_parts]
    n_rep = sum(rep_rows)
    small = jnp.concatenate(rep_parts + [_rows128(g_pack)], axis=0)
    small_gather = _SplitGather("gather_small_grads", [(small[None], 0)], F32, after=small)

    ws = 3 * dh // N_DEV

    def pair_epi(acc, i, ex, out):
        out[0][0] = acc[:, :ws].astype(BF16)
        out[0][1] = acc[:, ws:].astype(BF16)

    (g_win,) = _matmul(
        "grad_w_in", xs, dproj, _sp((seq, t.grad_rows), lambda i, j, k: (0, i)),
        _sp((seq, 2 * ws), lambda i, j, k: (0, j)),
        grid=(d_model // t.grad_rows, N_DEV // 2, 1), ta=True, extras=[after(small_gather.token)],
        out_shape=[sds((N_DEV, d_model, ws), BF16)],
        out_specs=[_sp((2, t.grad_rows, ws), lambda i, j, k: (j, i, 0))], epilogue=pair_epi)
    scatter_mix = _SplitReduceScatter(
        "scatter_mixer", [g_win, g_wout, g_wpool.reshape(N_DEV, pg // N_DEV * len(POOL_WINDOWS), pg)])

    def adam_big(name, own_landed, w, m, v):
        own, landed = own_landed
        shp = w.shape
        two = lambda a: a.reshape(-1, shp[-1])
        res = _sum_adamw(name, own, landed, two(w), two(m), two(v))
        return [r.reshape(shp) for r in res]

    (r_w2,) = scatter_w2.wait(after=scatter_mix.token)
    o_w2 = adam_big("adam_w_mlp_out", r_w2, w_mlp_out, m_w_mlp_out, v_w_mlp_out)
    token_mix = scatter_mix.combine_and_send(after=o_w2[0])

    def dx_epi(acc, i, ex, out):
        out[0][...] = ALPHA * ex[0][...] + acc

    (dx,) = _matmul(
        "grad_x", dproj, win_full, _sp((t.ln_rows * 2, 3 * dh), lambda i, j, k: (i, 0)),
        _sp((d_model, 3 * dh), vec, single=True),
        grid=(seq // (t.ln_rows * 2), 1, 1), tb=True,
        extras=[(dz1, _sp((t.ln_rows * 2, d_model), row_full)), after(token_mix)],
        out_shape=[sds((seq, d_model), F32)], out_specs=[_sp((t.ln_rows * 2, d_model), row_full)],
        epilogue=dx_epi)
    (r_w1,) = scatter_w1.wait(after=dx)
    o_w1 = adam_big("adam_w_mlp_in", r_w1, w_mlp_in, m_w_mlp_in, v_w_mlp_in)
    r_win, r_wout, r_wpool = scatter_mix.wait(after=o_w1[0])
    o_win = adam_big("adam_w_in", r_win, w_in, m_w_in, v_w_in)
    o_wout = adam_big("adam_w_out", r_wout, w_out, m_w_out, v_w_out)
    o_wpool = adam_big("adam_w_pool", r_wpool, w_pool, m_w_pool, v_w_pool)

    small_gather.relay(after=o_win[0])
    (small_all,) = small_gather.wait(after=o_wpool[0])

    rep_w = [w_rg_a, w_rg_i, ln_mix_g, ln_mix_b, ln_ffn_g, ln_ffn_b, pool_scale, conv_b]
    rep_m = [m_w_rg_a, m_w_rg_i, m_ln_mix_g, m_ln_mix_b, m_ln_ffn_g, m_ln_ffn_b, m_pool_scale, m_conv_b]
    rep_v = [v_w_rg_a, v_w_rg_i, v_ln_mix_g, v_ln_mix_b, v_ln_ffn_g, v_ln_ffn_b, v_pool_scale, v_conv_b]
    cat = lambda arrs: jnp.concatenate([_rows128(a) for a in arrs], axis=0)
    o_rep = _sum_adamw("adam_replicated", None, small_all, cat(rep_w), cat(rep_m), cat(rep_v))

    my_idx = _dev_index(_where_am_i())
    head_parts = lax.dynamic_slice_in_dim(small_all, n_rep + my_idx * SMALL_ROWS, SMALL_ROWS, axis=1)
    o_head = _sum_adamw("adam_head", None, head_parts, pack_mine,
                        small_pack(m_conv_w, m_b_rg_a, m_b_rg_i, m_rg_lambda),
                        small_pack(v_conv_w, v_b_rg_a, v_b_rg_i, v_rg_lambda))

    def unpack_rep(packed):
        out, r = [], 0
        for wgt, rows in zip(rep_w, rep_rows):
            out.append(packed[r:r + rows].reshape(wgt.shape))
            r += rows
        return out

    def unpack_head(packed):
        return [packed[0:4].reshape(conv_w.shape), packed[4:6].reshape(b_rg_a.shape),
                packed[6:8].reshape(b_rg_i.shape), packed[8:10].reshape(rg_lambda.shape)]

    loss = lax.psum(loss_part[0, 0], ("x", "y", "c"))

    outs = [loss, dx[None]]
    for kind in range(4):
        ra, ri, mg, mb, fg, fb, ps, cb = unpack_rep(o_rep[kind])
        cw, ba, bi, lam = unpack_head(o_head[kind])
        outs += [mg, mb, o_win[kind], o_wpool[kind], ps, cw, cb, ra, ba, ri, bi, lam, o_wout[kind], fg, fb,
                 o_w1[kind], o_w2[kind]]
    return tuple(outs)
```

```python
import functools

import jax
import jax.numpy as jnp
from jax import lax
from jax.experimental import pallas as pl
from jax.experimental.pallas import tpu as pltpu

F32 = jnp.float32
BF16 = jnp.bfloat16
MESH = pl.DeviceIdType.MESH
ANY = pl.BlockSpec(memory_space=pl.ANY)

N_DEV = 8
POOL_WINDOWS = (2, 4, 8, 16)
N_HEADS = 8
RG_C = 8.0
LN_EPS = 1e-5
ALPHA = 2.0 ** 0.25
ADAM_LR = 0.001
ADAM_B1 = 0.9
ADAM_B2 = 0.999
ADAM_EPS = 1e-08
ADAM_WD = 0.01
ADAM_STEP = 10

VMEM_LIMIT = 56 * 1024 * 1024
WIN_HALO = 16
CONV_HALO = 8
SMALL_ROWS = 16


def _params(n_grid):
    return pltpu.CompilerParams(dimension_semantics=("arbitrary",) * n_grid, vmem_limit_bytes=VMEM_LIMIT)


def _shift(v, j):
    n = v.shape[0]
    s = (-j) % n
    return v if s == 0 else pltpu.roll(v, s, 0)


def _sigmoid(x):
    return 0.5 * jnp.tanh(0.5 * x) + 0.5


def _softplus(z):
    e = jnp.exp(-jnp.abs(z))
    u = 1.0 + e
    log1p = jnp.where(u == 1.0, e, jnp.log(u) * (e / jnp.where(u == 1.0, 1.0, u - 1.0)))
    return jnp.maximum(z, 0.0) + log1p


_GELU_C = 0.7978845608028654
_GELU_K = 0.044715


def _gelu_and_grad(x):
    x2 = x * x
    t = jnp.tanh(_GELU_C * (x + _GELU_K * x * x2))
    g = 0.5 * x * (1.0 + t)
    dg = 0.5 * (1.0 + t) + 0.5 * x * (1.0 - t * t) * (_GELU_C * (1.0 + 3.0 * _GELU_K * x2))
    return g, dg


def _ln_fwd(z, g, b):
    mu = jnp.mean(z, axis=-1, keepdims=True)
    zc = z - mu
    var = jnp.mean(zc * zc, axis=-1, keepdims=True)
    rstd = lax.rsqrt(var + LN_EPS)
    xhat = zc * rstd
    return xhat * g + b, xhat, rstd


def _ln_bwd(dy, xhat, rstd, g):
    dxhat = dy * g
    m1 = jnp.mean(dxhat, axis=-1, keepdims=True)
    m2 = jnp.mean(dxhat * xhat, axis=-1, keepdims=True)
    dz = rstd * (dxhat - m1 - xhat * m2)
    dg = jnp.sum(dy * xhat, axis=0, keepdims=True)
    db = jnp.sum(dy, axis=0, keepdims=True)
    return dz, dg, db


def _acc_rows(ref, first, val):
    @pl.when(first)
    def _():
        ref[...] = val

    @pl.when(jnp.logical_not(first))
    def _():
        ref[...] += val


def _sp(shape, fn, single=False):
    return shape, fn, single


def _matmul(name, a, b, a_spec, b_spec, *, grid, j_outer=False, ta=False, tb=False, extras=(), out_shape, out_specs,
            epilogue=None, n_split=1):
    ni, nj, nk = grid
    n_ex = len(extras)
    dims = (((0 if ta else 1,), (1 if tb else 0,)), ((), ()))

    def mk(spec):
        shape, fn, single = spec
        index = (lambda g0, g1, g2: fn(g1, g0, g2)) if j_outer else fn
        return pl.BlockSpec(shape, index, pipeline_mode=pl.Buffered(1)) if single else pl.BlockSpec(shape, index)

    def body(a_ref, b_ref, *rest):
        ex_refs = rest[:n_ex]
        out_refs = rest[n_ex:]
        i = pl.program_id(1 if j_outer else 0)
        if n_split > 1:
            av = a_ref[...].astype(BF16)
            width = b_ref.shape[0 if tb else 1] // n_split
            for c in range(n_split):
                cols = pl.ds(c * width, width)
                bv = (b_ref[cols, :] if tb else b_ref[:, cols]).astype(BF16)
                epilogue(lax.dot_general(av, bv, dims, preferred_element_type=F32), i, ex_refs, out_refs, cols)
            return
        part = lax.dot_general(a_ref[...].astype(BF16), b_ref[...].astype(BF16), dims, preferred_element_type=F32)
        if nk == 1:
            epilogue(part, i, ex_refs, out_refs)
        else:
            @pl.when(pl.program_id(2) == 0)
            def _():
                out_refs[0][...] = jnp.zeros(out_refs[0].shape, F32)

            out_refs[0][...] += part

    return pl.pallas_call(
        body, name=name, grid=(nj, ni, nk) if j_outer else (ni, nj, nk),
        in_specs=[mk(a_spec), mk(b_spec)] + [mk(s) for _, s in extras],
        out_specs=[mk(s) for s in out_specs], out_shape=list(out_shape),
        compiler_params=_params(3),
    )(a, b, *[x for x, _ in extras])


def _bs(shape, fn):
    return pl.BlockSpec(shape, fn)


def _where_am_i():
    x, y, c = lax.axis_index("x"), lax.axis_index("y"), lax.axis_index("c")
    return x, y, c


def _dev_index(p):
    return 4 * p[0] + 2 * p[1] + p[2]


def _slab(ref, axis, idx, size):
    sl = [slice(None)] * len(ref.shape)
    sl[axis] = pl.ds(idx * size, size)
    return ref.at[tuple(sl)]


def _all_gather(name, items):
    n = len(items)
    shapes = []
    for shard, axis in items:
        s = list(shard.shape)
        s[axis] *= N_DEV
        shapes.append(jax.ShapeDtypeStruct(tuple(s), shard.dtype))

    def body(*refs):
        in_refs, out_refs = refs[:n], refs[n:2 * n]
        send_sems, recv_sems, local_sems = refs[2 * n:]
        x, y, c = _where_am_i()
        me, sibling = (x, y, c), (x, y, 1 - c)
        chips = [(1 - x, y), (x, 1 - y), (1 - x, 1 - y)]

        def blk(a, p):
            axis = items[a][1]
            return _slab(out_refs[a], axis, _dev_index(p), items[a][0].shape[axis])

        def copy(a, k, block, to, src=None):
            return pltpu.make_async_remote_copy(
                src_ref=blk(a, block) if src is None else src, dst_ref=blk(a, block),
                send_sem=send_sems.at[a, k], recv_sem=recv_sems.at[a, k], device_id=to, device_id_type=MESH)

        mine = [pltpu.make_async_copy(in_refs[a], blk(a, me), local_sems.at[a]) for a in range(n)]
        for cp in mine:
            cp.start()
        first = []
        for a in range(n):
            first.append(copy(a, 0, me, sibling, src=in_refs[a]))
            first += [copy(a, 1 + j, me, (*chip, c), src=in_refs[a]) for j, chip in enumerate(chips)]
        for cp in first:
            cp.start()
        passed = []
        for a in range(n):
            for j, chip in enumerate(chips):
                copy(a, 1 + j, (*chip, c), me).wait_recv()
                fw = copy(a, 4 + j, (*chip, c), sibling)
                fw.start()
                passed.append(fw)
        for a in range(n):
            copy(a, 0, sibling, me).wait_recv()
            for j, chip in enumerate(chips):
                copy(a, 4 + j, (*chip, 1 - c), me).wait_recv()
        for cp in first + passed:
            cp.wait_send()
        for cp in mine:
            cp.wait()

    outs = pl.pallas_call(
        body, name=name, out_shape=shapes, in_specs=[ANY] * n, out_specs=[ANY] * n,
        scratch_shapes=[pltpu.SemaphoreType.DMA((n, 7)), pltpu.SemaphoreType.DMA((n, 7)),
                        pltpu.SemaphoreType.DMA((n,))],
    )(*[s for s, _ in items])
    return list(outs)


HBM = pl.BlockSpec(memory_space=pltpu.HBM)
SEM = pl.BlockSpec(memory_space=pltpu.SEMAPHORE)
DATAFLOW = pltpu.SideEffectType.DATAFLOW_SIDE_EFFECTING


def _in_hbm(a):
    return pltpu.with_memory_space_constraint(a, pltpu.HBM)


def _token_shape():
    return jax.ShapeDtypeStruct((8, 128), F32)


def _split_start(name, n_sems, bufs, issue):
    nb = len(bufs)

    def body(*refs):
        issue(refs[:nb], refs[nb], refs[nb + 1])
        refs[-1][...] = jnp.zeros((8, 128), F32)

    outs = pl.pallas_call(
        body, name=name,
        out_shape=(pltpu.SemaphoreType.DMA((n_sems,)), pltpu.SemaphoreType.DMA((n_sems,)),
                   *[pltpu.HBM(b.shape, b.dtype) for b in bufs], _token_shape()),
        in_specs=[HBM] * nb, out_specs=(SEM, SEM, *[HBM] * nb, pl.BlockSpec(memory_space=pltpu.VMEM)),
        input_output_aliases={i: 2 + i for i in range(nb)},
        compiler_params=pltpu.CompilerParams(has_side_effects=DATAFLOW),
    )(*[_in_hbm(b) for b in bufs])
    return outs[0], outs[1], list(outs[2:2 + nb]), outs[-1]


def _split_relay(name, n_sems, sems, bufs, after, relay):
    nb = len(bufs)

    def body(*refs):
        relay(refs[:nb], refs[nb], refs[nb + 1], refs[nb + 3], refs[nb + 4])
        refs[-1][...] = jnp.zeros((8, 128), F32)

    outs = pl.pallas_call(
        body, name=name,
        out_shape=(pltpu.SemaphoreType.DMA((n_sems,)), pltpu.SemaphoreType.DMA((n_sems,)),
                   *[pltpu.HBM(b.shape, b.dtype) for b in bufs], _token_shape()),
        in_specs=[HBM] * nb + [SEM, SEM, ANY],
        out_specs=(SEM, SEM, *[HBM] * nb, pl.BlockSpec(memory_space=pltpu.VMEM)),
        input_output_aliases={i: 2 + i for i in range(nb)},
        compiler_params=pltpu.CompilerParams(has_side_effects=DATAFLOW),
    )(*bufs, sems[0], sems[1], after)
    return outs[0], outs[1], list(outs[2:2 + nb]), outs[-1]


def _split_wait(name, sems, bufs, after, finish):
    nb = len(bufs)

    def body(*refs):
        finish(refs[:nb], refs[nb], refs[nb + 1])

    outs = pl.pallas_call(
        body, name=name, out_shape=[pltpu.HBM(b.shape, b.dtype) for b in bufs],
        in_specs=[HBM] * nb + [SEM, SEM, ANY], out_specs=[HBM] * nb,
        input_output_aliases={i: i for i in range(nb)},
        compiler_params=pltpu.CompilerParams(has_side_effects=DATAFLOW),
    )(*bufs, sems[0], sems[1], after)
    return list(outs)


def _place(name, items, dtype, after):
    ids = jnp.reshape(_dev_index(_where_am_i()), (1,)).astype(jnp.int32)
    outs = []
    for a, (shard, axis) in enumerate(items):
        rows, cols = shard.shape[-2], shard.shape[-1]
        tr = rows
        while tr * cols * shard.dtype.itemsize > 4 * 1024 * 1024 and tr % 32 == 0:
            tr //= 2
        nt = rows // tr
        full = list(shard.shape)
        full[axis] *= N_DEV
        if shard.ndim == 2 and axis == 0:
            in_spec = _bs((tr, cols), lambda i, ids: (i, 0))
            out_spec = _bs((tr, cols), lambda i, ids, nt=nt: (ids[0] * nt + i, 0))
        elif shard.ndim == 2 and axis == 1:
            in_spec = _bs((tr, cols), lambda i, ids: (i, 0))
            out_spec = _bs((tr, cols), lambda i, ids: (i, ids[0]))
        else:
            assert shard.ndim == 3 and axis == 0 and shard.shape[0] == 1
            in_spec = _bs((None, tr, cols), lambda i, ids: (0, i, 0))
            out_spec = _bs((None, tr, cols), lambda i, ids: (ids[0], i, 0))

        def body(ids_ref, in_ref, after_ref, out_ref):
            del ids_ref, after_ref
            out_ref[...] = in_ref[...].astype(out_ref.dtype)

        outs.append(pl.pallas_call(
            body, name=f"{name}{a}",
            grid_spec=pltpu.PrefetchScalarGridSpec(
                num_scalar_prefetch=1, grid=(nt,), in_specs=[in_spec, ANY], out_specs=out_spec),
            out_shape=jax.ShapeDtypeStruct(tuple(full), dtype), compiler_params=_params(1),
        )(ids, shard, after))
    return outs


class _SplitGather:
    def __init__(self, name, items, dtype, after):
        self.name, self.items, self.n = name, items, len(items)
        fulls = _place(name + "_place", items, dtype, after)
        n = self.n

        def issue(refs, send, recv):
            me, sibling, chips, c = self._geometry()
            for a in range(n):
                self._copy1(refs, send, recv, a, 0, me, sibling).start()
                for j, chip in enumerate(chips):
                    self._copy1(refs, send, recv, a, 1 + j, me, (*chip, c)).start()

        self.send, self.recv, self.bufs, self.token = _split_start(name + "_start", 4 * n, fulls, issue)

    @staticmethod
    def _geometry():
        x, y, c = _where_am_i()
        return (x, y, c), (x, y, 1 - c), [(1 - x, y), (x, 1 - y), (1 - x, 1 - y)], c

    def _blk(self, refs, a, p):
        shard, axis = self.items[a]
        return _slab(refs[a], axis, _dev_index(p), shard.shape[axis])

    def _copy1(self, refs, send, recv, a, k, owner, to):
        return pltpu.make_async_remote_copy(
            src_ref=self._blk(refs, a, owner), dst_ref=self._blk(refs, a, owner), send_sem=send.at[4 * a + k],
            recv_sem=recv.at[4 * a + k], device_id=to, device_id_type=MESH)

    def _copy2(self, refs, send, recv, a, j, owner, to):
        return pltpu.make_async_remote_copy(
            src_ref=self._blk(refs, a, owner), dst_ref=self._blk(refs, a, owner), send_sem=send.at[3 * a + j],
            recv_sem=recv.at[3 * a + j], device_id=to, device_id_type=MESH)

    def relay(self, after):
        n = self.n

        def relay(refs, send_in, recv_in, send_out, recv_out):
            me, sibling, chips, c = self._geometry()
            for a in range(n):
                for j, chip in enumerate(chips):
                    self._copy1(refs, send_in, recv_in, a, 1 + j, (*chip, c), me).wait_recv()
                    self._copy2(refs, send_out, recv_out, a, j, (*chip, c), sibling).start()
            for a in range(n):
                self._copy1(refs, send_in, recv_in, a, 0, sibling, me).wait_recv()
                for k in range(4):
                    self._copy1(refs, send_in, recv_in, a, k, me, sibling).wait_send()

        self.send, self.recv, self.bufs, self.token = _split_relay(
            self.name + "_relay", 3 * n, (self.send, self.recv), self.bufs, after, relay)
        return self.token

    def wait(self, after):
        n = self.n

        def finish(refs, send, recv):
            me, sibling, chips, c = self._geometry()
            for a in range(n):
                for j, chip in enumerate(chips):
                    self._copy2(refs, send, recv, a, j, (*chip, 1 - c), me).wait_recv()
                    self._copy2(refs, send, recv, a, j, (*chip, c), sibling).wait_send()

        return _split_wait(self.name + "_wait", (self.send, self.recv), self.bufs, after, finish)


class _SplitReduceScatter:
    def __init__(self, name, grads):
        self.name, self.n = name, len(grads)
        n = self.n
        g4 = [g.reshape(4, 2, *g.shape[1:]) for g in grads]
        land = [lax.empty((4, 1, *g.shape[1:]), g.dtype) for g in grads]

        def issue(refs, send, recv):
            for a in range(n):
                self._swap(refs, send, recv, a).start()

        self.send, self.recv, self.bufs, self.token = _split_start(name + "_d2d_start", n, g4 + land, issue)

    def _swap(self, refs, send, recv, a):
        x, y, c = _where_am_i()
        return pltpu.make_async_remote_copy(
            src_ref=refs[a].at[:, pl.ds(1 - c, 1)], dst_ref=refs[self.n + a], send_sem=send.at[a], recv_sem=recv.at[a],
            device_id=(x, y, 1 - c), device_id_type=MESH)

    def _hop(self, refs, send, recv, a, m):
        x, y, c = _where_am_i()
        px = (1 - x) if m & 2 else x
        py = (1 - y) if m & 1 else y
        return pltpu.make_async_remote_copy(
            src_ref=refs[a].at[2 * px + py], dst_ref=refs[self.n + a].at[m - 1], send_sem=send.at[3 * a + m - 1],
            recv_sem=recv.at[3 * a + m - 1], device_id=(px, py, c), device_id_type=MESH)

    def combine_and_send(self, after):
        n = self.n

        def finish(refs, send, recv):
            for a in range(n):
                self._swap(refs, send, recv, a).wait()

        bufs = _split_wait(self.name + "_d2d_wait", (self.send, self.recv), self.bufs, after, finish)
        x, y, c = _where_am_i()
        ids = jnp.stack([c, 2 * x + y]).astype(jnp.int32)
        self.own, sums = [], []
        for a in range(n):
            own, hb = _pair_sum(f"{self.name}_sum{a}", bufs[a], bufs[n + a], ids)
            self.own.append(own)
            sums.append(hb)
        land = [lax.empty((3, *h.shape[1:]), h.dtype) for h in sums]

        def issue(refs, send, recv):
            for a in range(n):
                for m in (1, 2, 3):
                    self._hop(refs, send, recv, a, m).start()

        self.send, self.recv, self.bufs, self.token = _split_start(self.name + "_ici_start", 3 * n, sums + land, issue)
        return self.token

    def wait(self, after):
        n = self.n

        def finish(refs, send, recv):
            for a in range(n):
                for m in (1, 2, 3):
                    self._hop(refs, send, recv, a, m).wait()

        bufs = _split_wait(self.name + "_ici_wait", (self.send, self.recv), self.bufs, after, finish)
        return list(zip(self.own, bufs[n:]))


def _pair_sum(name, g4, land, ids):
    rows, cols = g4.shape[2], g4.shape[3]
    tr = rows
    while tr * cols * 2 > 1024 * 1024 and tr % 32 == 0:
        tr //= 2

    def body(ids_ref, g_ref, l_ref, own_ref, sum_ref):
        h = g_ref[...].astype(F32) + l_ref[...].astype(F32)
        sum_ref[...] = h.astype(sum_ref.dtype)

        @pl.when(pl.program_id(1) == ids_ref[1])
        def _():
            own_ref[...] = h

    return pl.pallas_call(
        body, name=name,
        grid_spec=pltpu.PrefetchScalarGridSpec(
            num_scalar_prefetch=1, grid=(rows // tr, 4),
            in_specs=[_bs((None, None, tr, cols), lambda i, q, ids: (q, ids[0], i, 0)),
                      _bs((None, None, tr, cols), lambda i, q, ids: (q, 0, i, 0))],
            out_specs=[_bs((tr, cols), lambda i, q, ids: (i, 0)), _bs((None, tr, cols), lambda i, q, ids: (q, i, 0))]),
        out_shape=[jax.ShapeDtypeStruct((rows, cols), F32), jax.ShapeDtypeStruct((4, rows, cols), g4.dtype)],
        compiler_params=_params(2),
    )(ids, g4, land)


def _win_sum(ext, w, off):
    s = ext + _shift(ext, -1)
    if w >= 4:
        s = _shift(s, -1) + _shift(s, 1)
    if w >= 8:
        s = _shift(s, -2) + _shift(s, 2)
    if w >= 16:
        s = _shift(s, -4) + _shift(s, 4)
    return _shift(s, off) if off else s


def _inv_count(r0, t, w, seq):
    pos = r0 + lax.broadcasted_iota(jnp.int32, (t, 1), 0)
    cnt = jnp.minimum(pos + w // 2, seq) - jnp.maximum(pos - w // 2, 0)
    return 1.0 / cnt.astype(F32)


def _pool_fwd(p3, w_pool, pool_scale, seq, d_model):
    dp = d_model // 2
    pg = dp // len(POOL_WINDOWS)
    t = min(128, seq)
    n_chunks = seq // t
    h = WIN_HALO

    def body(u_ref, w_ref, sc_ref, d_ref, y_ref, pad_ref):
        g = pl.program_id(0)
        zeros = jnp.zeros((h, pg), F32)
        pad_ref[0:h, :] = zeros
        pad_ref[h + seq:h + seq + h, :] = zeros

        def fill(ci, _):
            r0 = pl.multiple_of(ci * t, t)
            pad_ref[pl.ds(h + r0, t), :] = u_ref[pl.ds(r0, t), :]
            return 0

        lax.fori_loop(0, n_chunks, fill, 0)
        wmat = w_ref[...]
        scale = sc_ref[...]
        for gi, w in enumerate(POOL_WINDOWS):
            @pl.when(g == gi)
            def _(w=w):
                def chunk(ci, _):
                    r0 = pl.multiple_of(ci * t, t)
                    ext = pad_ref[pl.ds(r0, t + 2 * h), :]
                    mean = _win_sum(ext, w, 0)[h:h + t, :] * _inv_count(r0, t, w, seq)
                    d = (mean - ext[h:h + t, :]).astype(BF16)
                    d_ref[pl.ds(r0, t), :] = d
                    q = jnp.dot(d, wmat, preferred_element_type=F32)
                    y_ref[pl.ds(r0, t), :] = (q * scale).astype(BF16)
                    return 0

                lax.fori_loop(0, n_chunks, chunk, 0)

    return pl.pallas_call(
        body, name="pool_fwd", grid=(len(POOL_WINDOWS),),
        in_specs=[_bs((None, seq, pg), lambda g: (0, 0, g)), _bs((None, pg, pg), lambda g: (g, 0, 0)),
                  _bs((1, pg), lambda g: (0, g))],
        out_specs=[_bs((seq, pg), lambda g: (0, g)), _bs((seq, pg), lambda g: (0, g))],
        out_shape=[jax.ShapeDtypeStruct((seq, dp), BF16), jax.ShapeDtypeStruct((seq, d_model), BF16)],
        scratch_shapes=[pltpu.VMEM((seq + 2 * h, pg), F32)],
        compiler_params=_params(1),
    )(p3, w_pool, pool_scale)


def _pool_bwd(d, dy, w_pool, pool_scale, token, seq, d_model):
    dp = d_model // 2
    pg = dp // len(POOL_WINDOWS)
    t = min(128, seq)
    n_chunks = seq // t
    h = WIN_HALO
    tn_dims = (((0,), (0,)), ((), ()))
    nt_dims = (((1,), (1,)), ((), ()))

    def body(d_ref, dy_ref, w_ref, sc_ref, tok_ref, du_ref, dwb_ref, dsc_ref, pad_ref, dd_ref, dw_ref):
        del tok_ref
        g = pl.program_id(0)
        zeros = jnp.zeros((h, pg), F32)
        pad_ref[0:h, :] = zeros
        pad_ref[h + seq:h + seq + h, :] = zeros
        wmat = w_ref[...]
        scale = sc_ref[...]
        for gi, w in enumerate(POOL_WINDOWS):
            @pl.when(g == gi)
            def _(w=w):
                dw_ref[...] = jnp.zeros((pg, pg), F32)

                def first(ci, dsc):
                    r0 = pl.multiple_of(ci * t, t)
                    dv = d_ref[pl.ds(r0, t), :]
                    dyv = dy_ref[pl.ds(r0, t), :]
                    q = jnp.dot(dv, wmat, preferred_element_type=F32)
                    dsc = dsc + jnp.sum(dyv * q, axis=0, keepdims=True)
                    dq = (dyv * scale).astype(BF16)
                    dw_ref[...] += lax.dot_general(dv, dq, tn_dims, preferred_element_type=F32)
                    dd = lax.dot_general(dq, wmat, nt_dims, preferred_element_type=F32)
                    dd_ref[pl.ds(r0, t), :] = dd
                    pad_ref[pl.ds(h + r0, t), :] = dd * _inv_count(r0, t, w, seq)
                    return dsc

                dsc_ref[...] = lax.fori_loop(0, n_chunks, first, jnp.zeros((1, pg), F32))
                dwb_ref[...] = dw_ref[...].reshape(N_DEV, pg // N_DEV, pg).astype(BF16)

                def second(ci, _):
                    r0 = pl.multiple_of(ci * t, t)
                    ext = pad_ref[pl.ds(r0, t + 2 * h), :]
                    back = _win_sum(ext, w, 1)[h:h + t, :]
                    du_ref[pl.ds(r0, t), :] = (back - dd_ref[pl.ds(r0, t), :]).astype(BF16)
                    return 0

                lax.fori_loop(0, n_chunks, second, 0)

    return pl.pallas_call(
        body, name="pool_bwd", grid=(len(POOL_WINDOWS),),
        in_specs=[_bs((seq, pg), lambda g: (0, g)), _bs((seq, pg), lambda g: (0, g)),
                  _bs((None, pg, pg), lambda g: (g, 0, 0)), _bs((1, pg), lambda g: (0, g)),
                  _bs((8, 128), lambda g: (0, 0))],
        out_specs=[_bs((seq, pg), lambda g: (0, g)), _bs((N_DEV, None, pg // N_DEV, pg), lambda g: (0, g, 0, 0)),
                   _bs((1, pg), lambda g: (0, g))],
        out_shape=[jax.ShapeDtypeStruct((seq, 3 * dp), BF16),
                   jax.ShapeDtypeStruct((N_DEV, len(POOL_WINDOWS), pg // N_DEV, pg), BF16),
                   jax.ShapeDtypeStruct((1, dp), F32)],
        scratch_shapes=[pltpu.VMEM((seq + 2 * h, pg), F32), pltpu.VMEM((seq, pg), F32), pltpu.VMEM((pg, pg), F32)],
        compiler_params=_params(1),
    )(d, dy, w_pool, pool_scale, token)


def _tile_scan(n_tiles, lanes, loads, stores):
    row = lax.broadcasted_iota(jnp.int32, (8, lanes), 0)
    group = 8

    def local_scan(n, k):
        aa, bb = loads[n](k)
        for sh in (1, 2, 4):
            if n == 0:
                ok = row >= sh
                ap = jnp.where(ok, pltpu.roll(aa, sh, 0), 1.0)
                bp = jnp.where(ok, pltpu.roll(bb, sh, 0), 0.0)
            else:
                ok = row < 8 - sh
                ap = jnp.where(ok, pltpu.roll(aa, 8 - sh, 0), 1.0)
                bp = jnp.where(ok, pltpu.roll(bb, 8 - sh, 0), 0.0)
            bb = aa * bp + bb
            aa = aa * ap
        return aa, bb

    def step(s, carry):
        carry = list(carry)
        for n in range(2):
            tiles = [s * group + u if n == 0 else n_tiles - 1 - (s * group + u) for u in range(group)]
            local = [local_scan(n, k) for k in tiles]
            for k, (aa, bb) in zip(tiles, local):
                hh = bb + aa * carry[n]
                stores[n](k, hh)
                carry[n] = jnp.broadcast_to(hh[7:8, :] if n == 0 else hh[0:1, :], (8, lanes))
        return tuple(carry)

    zeros = jnp.zeros((8, lanes), F32)
    lax.fori_loop(0, n_tiles // group, step, (zeros, zeros))


def _gate_preacts(xc, wcat_ref):
    xcb = xc.astype(BF16)
    return xcb, jnp.dot(xcb, wcat_ref[...], preferred_element_type=F32)


def _gates(pre, n, pk_ref, sp):
    lh = pre.shape[1] // 4
    r = _sigmoid(pre[:, (2 * n) * lh:(2 * n + 1) * lh] + pk_ref[pl.ds(4 + n, 1), :])
    i = _sigmoid(pre[:, (2 * n + 1) * lh:(2 * n + 2) * lh] + pk_ref[pl.ds(6 + n, 1), :])
    log_a = (-RG_C * r) * sp[n]
    a = jnp.exp(log_a)
    x = 2.0 * log_a
    one_minus_a2 = jnp.where(x > -0.01, -(x * (1.0 + x * (0.5 + x * (1.0 / 6.0)))), 1.0 - a * a)
    m = jnp.sqrt(one_minus_a2)
    return r, i, a, m


def _conv_chunk(upad_ref, pk_ref, cb, r0, t):
    ext = upad_ref[pl.ds(r0, t + 2 * CONV_HALO), :]
    acc = pk_ref[pl.ds(1, 1), :] * ext
    for k in (0, 2, 3):
        acc = acc + pk_ref[pl.ds(k, 1), :] * _shift(ext, k - 1)
    return acc[CONV_HALO:CONV_HALO + t, :] + cb, ext


def _lru_fwd(p3, y_in, pack, conv_b, wcat, token, seq, d_model):
    dl = d_model // 2
    lh = dl // N_HEADS
    t = min(128, seq)
    n_chunks = seq // t
    seg = seq // 8
    hal = CONV_HALO
    first_rec_block = (d_model - dl) // lh

    def body(ur_ref, ug_ref, pk_ref, cb_ref, wcat_ref, yin_ref, tok_ref, y_ref, h0_ref, h1_ref,
             upad, a_scr, b_scr):
        del yin_ref, tok_ref
        zeros = jnp.zeros((hal, lh), F32)
        upad[0:hal, :] = zeros
        upad[hal + seq:hal + seq + hal, :] = zeros
        for ref in (h0_ref, h1_ref):
            ref[0:hal, :] = zeros
            ref[hal + seq:hal + seq + hal, :] = zeros

        def fill(ci, _):
            r0 = pl.multiple_of(ci * t, t)
            upad[pl.ds(hal + r0, t), :] = ur_ref[pl.ds(r0, t), :]
            return 0

        lax.fori_loop(0, n_chunks, fill, 0)
        cb = cb_ref[...]
        sp = [_softplus(-pk_ref[pl.ds(8 + n, 1), :]) for n in range(2)]

        def chunk(ci, _):
            r0 = pl.multiple_of(ci * t, t)
            xc, _ext = _conv_chunk(upad, pk_ref, cb, r0, t)
            _, pre = _gate_preacts(xc, wcat_ref)
            for n in range(2):
                _, i, a, m = _gates(pre, n, pk_ref, sp)
                a_scr[n, pl.ds(r0, t), :] = a
                b_scr[n, pl.ds(r0, t), :] = (m * i) * xc
            return 0

        lax.fori_loop(0, n_chunks, chunk, 0, unroll=2)

        def load(n):
            def get(k):
                at = pl.ds(pl.multiple_of(k * 8, 8), 8)
                return a_scr[n, at, :], b_scr[n, at, :]
            return get

        def store(ref):
            def put(k, v):
                ref[pl.ds(pl.multiple_of(hal + k * 8, 8), 8), :] = v
            return put

        _tile_scan(seq // 8, lh, [load(0), load(1)], [store(h0_ref), store(h1_ref)])

        def out(ci, _):
            r0 = pl.multiple_of(ci * t, t)
            hsum = h0_ref[pl.ds(hal + r0, t), :] + h1_ref[pl.ds(hal + r0, t), :]
            gl, _dg = _gelu_and_grad(ug_ref[pl.ds(r0, t), :])
            y_ref[pl.ds(r0, t), :] = (hsum * gl).astype(BF16)
            return 0

        lax.fori_loop(0, n_chunks, out, 0)

    return pl.pallas_call(
        body, name="lru_fwd", grid=(N_HEADS,),
        in_specs=[_bs((None, seq, lh), lambda h: (1, 0, h)), _bs((None, seq, lh), lambda h: (2, 0, h)),
                  _bs((None, SMALL_ROWS, lh), lambda h: (h, 0, 0)), _bs((1, lh), lambda h: (0, h)),
                  _bs((None, lh, 4 * lh), lambda h: (h, 0, 0)),
                  ANY, _bs((8, 128), lambda h: (0, 0))],
        out_specs=[_bs((seq, lh), lambda h: (0, first_rec_block + h)),
                   _bs((seq + 2 * hal, lh), lambda h: (0, h)), _bs((seq + 2 * hal, lh), lambda h: (0, h))],
        out_shape=[jax.ShapeDtypeStruct((seq, d_model), BF16), jax.ShapeDtypeStruct((seq + 2 * hal, dl), F32),
                   jax.ShapeDtypeStruct((seq + 2 * hal, dl), F32)],
        scratch_shapes=[pltpu.VMEM((seq + 2 * hal, lh), F32), pltpu.VMEM((2, seq, lh), F32),
                        pltpu.VMEM((2, seq, lh), F32)],
        input_output_aliases={5: 0},
        compiler_params=_params(1),
    )(p3, p3, pack, conv_b, wcat, y_in, token)


def _lru_bwd(p3, dy, h0p, h1p, dproj_in, pack, conv_b, wcat, token, seq, d_model):
    dl = d_model // 2
    lh = dl // N_HEADS
    t = min(128, seq)
    n_chunks = seq // t
    seg = seq // 8
    hal = CONV_HALO
    first_rec_block = (d_model - dl) // lh
    tn_dims = (((0,), (0,)), ((), ()))
    nt_dims = (((1,), (1,)), ((), ()))

    def body(ur_ref, ug_ref, dy_ref, h0_ref, h1_ref, pk_ref, cb_ref, wcat_ref, tok_ref, din_ref,
             dproj_ref, dpk_ref, dcb_ref, dwcat_ref,
             upad, a_scr, dh_scr, g_scr, dxc_pad, dpr_ref, out_sems, gate_scr):
        del din_ref, tok_ref
        zeros = jnp.zeros((hal, lh), F32)
        for ref in (upad, dxc_pad):
            ref[0:hal, :] = zeros
            ref[hal + seq:hal + seq + hal, :] = zeros
        for n in range(2):
            a_scr[n, 0:hal, :] = zeros
            a_scr[n, hal + seq:hal + seq + hal, :] = zeros

        def fill(ci, _):
            r0 = pl.multiple_of(ci * t, t)
            upad[pl.ds(hal + r0, t), :] = ur_ref[pl.ds(r0, t), :]
            return 0

        lax.fori_loop(0, n_chunks, fill, 0)
        cb = cb_ref[...]
        lam = [pk_ref[pl.ds(8 + n, 1), :] for n in range(2)]
        sp = [_softplus(-lam[n]) for n in range(2)]

        def chunk1(ci, _):
            r0 = pl.multiple_of(ci * t, t)
            xc, _ext = _conv_chunk(upad, pk_ref, cb, r0, t)
            _, pre = _gate_preacts(xc, wcat_ref)
            for n in range(2):
                r, i, a, m = _gates(pre, n, pk_ref, sp)
                a_scr[n, pl.ds(hal + r0, t), :] = a
                for q, v in enumerate((r, i, m)):
                    gate_scr[3 * n + q, pl.ds(r0, t), :] = v
            hsum = h0_ref[pl.ds(hal + r0, t), :] + h1_ref[pl.ds(hal + r0, t), :]
            gl, dgl = _gelu_and_grad(ug_ref[pl.ds(r0, t), :])
            dyv = dy_ref[pl.ds(r0, t), :]
            dh_scr[pl.ds(r0, t), :] = dyv * gl
            dpr_ref[1, pl.ds(r0, t), :] = ((dyv * hsum) * dgl).astype(BF16)
            return 0

        lax.fori_loop(0, n_chunks, chunk1, 0, unroll=2)

        def load(n):
            def get(k):
                r0 = pl.multiple_of(k * 8, 8)
                if n == 0:
                    coef = _shift(a_scr[0, pl.ds(pl.multiple_of(hal + r0, 8), 16), :], 1)[0:8, :]
                else:
                    coef = _shift(a_scr[1, pl.ds(pl.multiple_of(hal + r0 - 8, 8), 16), :], -1)[8:16, :]
                return coef, dh_scr[pl.ds(r0, 8), :]
            return get

        def store(n):
            def put(k, v):
                g_scr[n, pl.ds(pl.multiple_of(k * 8, 8), 8), :] = v
            return put

        _tile_scan(seq // 8, lh, [load(1), load(0)], [store(1), store(0)])

        dwcat_ref[...] = jnp.zeros((lh, 4 * lh), F32)

        def chunk3(ci, carry):
            dba, dbi, dlam, dcb = carry
            r0 = pl.multiple_of(ci * t, t)
            xc, _ext = _conv_chunk(upad, pk_ref, cb, r0, t)
            xcb = xc.astype(BF16)
            dxc = jnp.zeros((t, lh), F32)
            dba, dbi, dlam = list(dba), list(dbi), list(dlam)
            dpre = []
            for n in range(2):
                r, i, m = (gate_scr[3 * n + q, pl.ds(r0, t), :] for q in range(3))
                a = a_scr[n, pl.ds(hal + r0, t), :]
                hext = (h0_ref if n == 0 else h1_ref)[pl.ds(r0, t + 2 * hal), :]
                hprev = _shift(hext, -1 if n == 0 else 1)[hal:hal + t, :]
                gb = g_scr[n, pl.ds(r0, t), :]
                da = gb * hprev
                dm = gb * i * xc
                di = gb * m * xc
                dxc = dxc + gb * (m * i)
                dlog_a = da * a - dm * (a * a) / m
                dr = dlog_a * (-RG_C * sp[n])
                dlam[n] = dlam[n] + jnp.sum(dlog_a * r, axis=0, keepdims=True)
                dpr = dr * r * (1.0 - r)
                dpi = di * i * (1.0 - i)
                dba[n] = dba[n] + jnp.sum(dpr, axis=0, keepdims=True)
                dbi[n] = dbi[n] + jnp.sum(dpi, axis=0, keepdims=True)
                dpre += [dpr.astype(BF16), dpi.astype(BF16)]
            dpre = jnp.concatenate(dpre, axis=1)
            dwcat_ref[...] += lax.dot_general(xcb, dpre, tn_dims, preferred_element_type=F32)
            dxc = dxc + lax.dot_general(dpre, wcat_ref[...], nt_dims, preferred_element_type=F32)
            dxc_pad[pl.ds(hal + r0, t), :] = dxc
            dcb = dcb + jnp.sum(dxc, axis=0, keepdims=True)
            return tuple(dba), tuple(dbi), tuple(dlam), dcb

        zr = jnp.zeros((1, lh), F32)
        def chunk3_pair(cj, carry):
            return chunk3(2 * cj + 1, chunk3(2 * cj, carry))

        dba, dbi, dlam, dcb = lax.fori_loop(0, n_chunks // 2, chunk3_pair, ((zr, zr), (zr, zr), (zr, zr), zr))
        dcb_ref[...] = dcb
        for n in range(2):
            dpk_ref[pl.ds(4 + n, 1), :] = dba[n]
            dpk_ref[pl.ds(6 + n, 1), :] = dbi[n]
            dpk_ref[pl.ds(8 + n, 1), :] = dlam[n] * (RG_C * jax.nn.sigmoid(-lam[n]))
        dpk_ref[pl.ds(10, SMALL_ROWS - 10), :] = jnp.zeros((SMALL_ROWS - 10, lh), F32)

        def chunk4(ci, dtap):
            r0 = pl.multiple_of(ci * t, t)
            gext = dxc_pad[pl.ds(r0, t + 2 * hal), :]
            uext = upad[pl.ds(r0, t + 2 * hal), :]
            gmid = gext[hal:hal + t, :]
            du = pk_ref[pl.ds(1, 1), :] * gext
            for k in (0, 2, 3):
                du = du + pk_ref[pl.ds(k, 1), :] * _shift(gext, 1 - k)
            dpr_ref[0, pl.ds(r0, t), :] = du[hal:hal + t, :].astype(BF16)
            out = []
            for k in range(4):
                usl = _shift(uext, k - 1)[hal:hal + t, :]
                out.append(dtap[k] + jnp.sum(gmid * usl, axis=0, keepdims=True))
            return tuple(out)

        dtap = lax.fori_loop(0, n_chunks, chunk4, (zr, zr, zr, zr))
        for k in range(4):
            dpk_ref[pl.ds(k, 1), :] = dtap[k]

        head = pl.program_id(0)
        outs = [pltpu.make_async_copy(
            dpr_ref.at[b], dproj_ref.at[:, pl.ds(pl.multiple_of((1 + b) * dl + head * lh, lh), lh)], out_sems.at[b])
            for b in range(2)]
        for cp in outs:
            cp.start()
        for cp in outs:
            cp.wait()

    return pl.pallas_call(
        body, name="lru_bwd", grid=(N_HEADS,),
        in_specs=[_bs((None, seq, lh), lambda h: (1, 0, h)), _bs((None, seq, lh), lambda h: (2, 0, h)),
                  _bs((seq, lh), lambda h: (0, first_rec_block + h)),
                  _bs((seq + 2 * hal, lh), lambda h: (0, h)), _bs((seq + 2 * hal, lh), lambda h: (0, h)),
                  _bs((None, SMALL_ROWS, lh), lambda h: (h, 0, 0)), _bs((1, lh), lambda h: (0, h)),
                  _bs((None, lh, 4 * lh), lambda h: (h, 0, 0)),
                  _bs((8, 128), lambda h: (0, 0)), ANY],
        out_specs=[ANY, _bs((None, SMALL_ROWS, lh), lambda h: (h, 0, 0)),
                   _bs((1, lh), lambda h: (0, h)), _bs((None, lh, 4 * lh), lambda h: (h, 0, 0))],
        out_shape=[jax.ShapeDtypeStruct((seq, 3 * dl), BF16), jax.ShapeDtypeStruct((N_HEADS, SMALL_ROWS, lh), F32),
                   jax.ShapeDtypeStruct((1, dl), F32), jax.ShapeDtypeStruct((N_HEADS, lh, 4 * lh), F32)],
        scratch_shapes=[pltpu.VMEM((seq + 2 * hal, lh), F32), pltpu.VMEM((2, seq + 2 * hal, lh), F32),
                        pltpu.VMEM((seq, lh), F32), pltpu.VMEM((2, seq, lh), F32),
                        pltpu.VMEM((seq + 2 * hal, lh), F32), pltpu.VMEM((2, seq, lh), BF16),
                        pltpu.SemaphoreType.DMA((2,)), pltpu.VMEM((6, seq, lh), F32)],
        input_output_aliases={9: 0},
        compiler_params=_params(1),
    )(p3, p3, dy, h0p, h1p, pack, conv_b, wcat, token, dproj_in)


class _tiles:
    def __init__(self, seq, d_model, d_ff):
        self.rows = min(1024, seq)
        self.ln_rows = min(256, seq)
        self.ff_cols = min(1024, d_ff)
        self.ff_split = 4
        self.ff_k = min(2048, d_ff)
        self.grad_rows = 512


def _ln_loss_bwd(ffn, x1, tgt, g, b, tr):
    seq, d = ffn.shape

    def body(f_ref, x_ref, t_ref, g_ref, b_ref, dz_ref, dzb_ref, dg_ref, db_ref, loss_ref):
        i = pl.program_id(0)
        gv = g_ref[...]
        z = ALPHA * x_ref[...] + f_ref[...]
        y, xhat, rstd = _ln_fwd(z, gv, b_ref[...])
        err = y - t_ref[...]
        part = 0.5 * jnp.sum(jnp.mean(err * err, axis=-1, keepdims=True), axis=0, keepdims=True)
        dz, dg, db = _ln_bwd(err * (1.0 / d), xhat, rstd, gv)
        dz_ref[...] = dz
        dzb_ref[...] = dz.astype(BF16)
        _acc_rows(dg_ref, i == 0, dg)
        _acc_rows(db_ref, i == 0, db)
        _acc_rows(loss_ref, i == 0, jnp.broadcast_to(part, (8, 128)))

    row = _bs((tr, d), lambda i: (i, 0))
    vec = _bs((1, d), lambda i: (0, 0))
    return pl.pallas_call(
        body, name="ln_ffn_loss", grid=(seq // tr,), in_specs=[row, row, row, vec, vec],
        out_specs=[row, row, vec, vec, _bs((8, 128), lambda i: (0, 0))],
        out_shape=[jax.ShapeDtypeStruct((seq, d), F32), jax.ShapeDtypeStruct((seq, d), BF16),
                   jax.ShapeDtypeStruct((1, d), F32), jax.ShapeDtypeStruct((1, d), F32),
                   jax.ShapeDtypeStruct((8, 128), F32)],
        compiler_params=_params(1),
    )(ffn, x1, tgt, g, b)


def _ln_bwd_rows(dx_branch, dres, z, g, b, tr):
    seq, d = z.shape

    def body(a_ref, r_ref, z_ref, g_ref, b_ref, dz_ref, dzb_ref, dg_ref, db_ref):
        i = pl.program_id(0)
        gv = g_ref[...]
        _, xhat, rstd = _ln_fwd(z_ref[...], gv, b_ref[...])
        dz, dg, db = _ln_bwd(ALPHA * r_ref[...] + a_ref[...], xhat, rstd, gv)
        dz_ref[...] = dz
        dzb_ref[...] = dz.astype(BF16)
        _acc_rows(dg_ref, i == 0, dg)
        _acc_rows(db_ref, i == 0, db)

    row = _bs((tr, d), lambda i: (i, 0))
    vec = _bs((1, d), lambda i: (0, 0))
    return pl.pallas_call(
        body, name="ln_mix_bwd", grid=(seq // tr,), in_specs=[row, row, row, vec, vec],
        out_specs=[row, row, vec, vec],
        out_shape=[jax.ShapeDtypeStruct((seq, d), F32), jax.ShapeDtypeStruct((seq, d), BF16),
                   jax.ShapeDtypeStruct((1, d), F32), jax.ShapeDtypeStruct((1, d), F32)],
        compiler_params=_params(1),
    )(dx_branch, dres, z, g, b)


def _adamw_values(w, g, m, v):
    m = ADAM_B1 * m + (1.0 - ADAM_B1) * g
    v = ADAM_B2 * v + (1.0 - ADAM_B2) * (g * g)
    m_hat = m / (1.0 - ADAM_B1 ** ADAM_STEP)
    v_hat = v / (1.0 - ADAM_B2 ** ADAM_STEP)
    delta = -ADAM_LR * (m_hat / (jnp.sqrt(v_hat) + ADAM_EPS) + ADAM_WD * w)
    return delta, m, v


def _sum_adamw(name, own, parts, w, m, v):
    rows, cols = w.shape
    n_parts = parts.shape[0]
    tr = rows
    min_rows = 8 if parts.dtype == F32 else 16
    while tr * cols * 4 > 1024 * 1024 and tr % (2 * min_rows) == 0:
        tr //= 2

    def body(*refs):
        if own is None:
            p_ref, w_ref, m_ref, v_ref, g_ref, d_ref, mo_ref, vo_ref = refs
            g = p_ref[0].astype(F32)
            rest = range(1, n_parts)
        else:
            o_ref, p_ref, w_ref, m_ref, v_ref, g_ref, d_ref, mo_ref, vo_ref = refs
            g = o_ref[...]
            rest = range(n_parts)
        for s in rest:
            g = g + p_ref[s].astype(F32)
        delta, mn, vn = _adamw_values(w_ref[...], g, m_ref[...], v_ref[...])
        g_ref[...] = g
        d_ref[...] = delta
        mo_ref[...] = mn
        vo_ref[...] = vn

    spec = _bs((tr, cols), lambda i: (i, 0))
    lead = [] if own is None else [own]
    return pl.pallas_call(
        body, name=name, grid=(rows // tr,),
        in_specs=[spec] * len(lead) + [_bs((n_parts, tr, cols), lambda i: (0, i, 0)), spec, spec, spec],
        out_specs=[spec] * 4, out_shape=[jax.ShapeDtypeStruct((rows, cols), F32)] * 4,
        compiler_params=_params(1),
    )(*lead, parts, w, m, v)


def _rows128(a):
    return a.reshape(-1, 128)


def kernel(x, ln_mix_g, ln_mix_b, w_in, w_pool, pool_scale, conv_w, conv_b, w_rg_a, b_rg_a, w_rg_i, b_rg_i, rg_lambda, w_out, ln_ffn_g, ln_ffn_b, w_mlp_in, w_mlp_out, loss_target, m_ln_mix_g, m_ln_mix_b, m_w_in, m_w_pool, m_pool_scale, m_conv_w, m_conv_b, m_w_rg_a, m_b_rg_a, m_w_rg_i, m_b_rg_i, m_rg_lambda, m_w_out, m_ln_ffn_g, m_ln_ffn_b, m_w_mlp_in, m_w_mlp_out, v_ln_mix_g, v_ln_mix_b, v_w_in, v_w_pool, v_pool_scale, v_conv_w, v_conv_b, v_w_rg_a, v_b_rg_a, v_w_rg_i, v_b_rg_i, v_rg_lambda, v_w_out, v_ln_ffn_g, v_ln_ffn_b, v_w_mlp_in, v_w_mlp_out):
    seq, d_model = x.shape[1], x.shape[2]
    dh = d_model // 2
    lh = dh // N_HEADS
    pg = dh // len(POOL_WINDOWS)
    d_ff = w_mlp_in.shape[2] * N_DEV
    assert lh == 128 and conv_w.shape[3] == lh and w_pool.shape[2] * N_DEV == pg

    xs = x[0]
    tgt = loss_target[0]

    def small_pack(cw, ba, bi, lam):
        return jnp.concatenate([cw.reshape(4, lh), ba.reshape(2, lh), bi.reshape(2, lh), lam.reshape(2, lh),
                                jnp.zeros((SMALL_ROWS - 10, lh), F32)], axis=0)

    pack_mine = small_pack(conv_w, b_rg_a, b_rg_i, rg_lambda)
    win_full, wpool_full, pack_full = _all_gather("gather_mixer", [
        (w_in[0].astype(BF16), 1), (w_pool[0].astype(BF16), 1), (pack_mine[None], 0)])
    wout_gather = _SplitGather("gather_w_out", [(w_out[0], 0)], BF16, after=pack_full)
    w1_gather = _SplitGather("gather_w_mlp_in", [(w_mlp_in[0], 1)], BF16, after=wout_gather.token)
    w2_gather = _SplitGather("gather_w_mlp_out", [(w_mlp_out[0], 0)], BF16, after=w1_gather.token)
    wcat = jnp.concatenate([w_rg_a[0, 0], w_rg_i[0, 0], w_rg_a[0, 1], w_rg_i[0, 1]], axis=-1).astype(BF16)
    vec = lambda i, j, k: (0, 0)
    row_full = lambda i, j, k: (i, 0)

    def after(token):
        return (token, _sp((8, 128), vec))

    def sds(shape, dtype):
        return jax.ShapeDtypeStruct(shape, dtype)

    def plain_epi(acc, i, ex, out):
        out[0][...] = acc

    def bf16_epi(acc, i, ex, out):
        out[0][...] = acc.astype(BF16)

    t = _tiles(seq, d_model, d_ff)

    (p3,) = _matmul(
        "proj", xs, win_full, _sp((t.rows, d_model), lambda i, j, k: (i, 0)), _sp((d_model, dh), lambda i, j, k: (0, j)),
        grid=(seq // t.rows, 3, 1), extras=[after(w2_gather.token)],
        out_shape=[sds((3, seq, dh), F32)], out_specs=[_sp((None, t.rows, dh), lambda i, j, k: (j, i, 0))],
        epilogue=plain_epi)

    d_pool, y_half = _pool_fwd(p3, wpool_full, pool_scale, seq, d_model)
    y, h0p, h1p = _lru_fwd(p3, y_half, pack_full, conv_b, wcat, wout_gather.relay(after=y_half), seq, d_model)
    (wout_full,) = wout_gather.wait(after=y)
    relay_token = w1_gather.relay(after=wout_full)

    def mix_epi(acc, i, ex, out):
        x_ref, g_ref, b_ref = ex[:3]
        z = ALPHA * x_ref[...] + acc
        x1, _, _ = _ln_fwd(z, g_ref[...], b_ref[...])
        out[0][...] = z
        out[1][...] = x1
        out[2][...] = x1.astype(BF16)

    z1, x1, x1b = _matmul(
        "mix_out", y, wout_full, _sp((t.ln_rows, d_model), row_full), _sp((d_model, d_model), vec, single=True),
        grid=(seq // t.ln_rows, 1, 1),
        extras=[(xs, _sp((t.ln_rows, d_model), row_full)), (ln_mix_g, _sp((1, d_model), vec)),
                (ln_mix_b, _sp((1, d_model), vec)), after(relay_token)],
        out_shape=[sds((seq, d_model), F32), sds((seq, d_model), F32), sds((seq, d_model), BF16)],
        out_specs=[_sp((t.ln_rows, d_model), row_full)] * 3, epilogue=mix_epi)
    (w1_full,) = w1_gather.wait(after=x1b)

    def mlp_in_epi(acc, i, ex, out, cols):
        h = jnp.maximum(acc, 0.0)
        out[0][:, cols] = (h * h).astype(BF16)
        out[1][:, cols] = (2.0 * h).astype(BF16)

    hmid, dact = _matmul(
        "mlp_in", x1b, w1_full, _sp((t.rows, d_model), lambda i, j, k: (i, 0)),
        _sp((d_model, t.ff_cols), lambda i, j, k: (0, j)),
        grid=(seq // t.rows, d_ff // t.ff_cols, 1), j_outer=True,
        out_shape=[sds((seq, d_ff), BF16)] * 2, out_specs=[_sp((t.rows, t.ff_cols), lambda i, j, k: (i, j))] * 2,
        epilogue=mlp_in_epi, n_split=t.ff_split)
    (w2_full,) = w2_gather.wait(after=w2_gather.relay(after=hmid))

    (ffn,) = _matmul(
        "mlp_out", hmid, w2_full, _sp((t.rows, t.ff_k), lambda i, j, k: (i, k)),
        _sp((t.ff_k, d_model), lambda i, j, k: (k, 0)),
        grid=(seq // t.rows, 1, d_ff // t.ff_k),
        out_shape=[sds((seq, d_model), F32)], out_specs=[_sp((t.rows, d_model), row_full)])
    dz2, dz2b, g_ffn_g, g_ffn_b, loss_part = _ln_loss_bwd(ffn, x1, tgt, ln_ffn_g, ln_ffn_b, t.ln_rows)

    (g_w2,) = _matmul(
        "grad_w_mlp_out", hmid, dz2b, _sp((seq, t.grad_rows), lambda i, j, k: (0, i)),
        _sp((seq, d_model), vec, single=True),
        grid=(d_ff // t.grad_rows, 1, 1), ta=True,
        out_shape=[sds((d_ff, d_model), BF16)], out_specs=[_sp((t.grad_rows, d_model), row_full)],
        epilogue=bf16_epi)
    scatter_w2 = _SplitReduceScatter("scatter_w_mlp_out", [g_w2.reshape(N_DEV, d_ff // N_DEV, d_model)])

    def dpre_epi(acc, i, ex, out, cols):
        out[0][:, cols] = (acc * ex[0][:, cols].astype(F32)).astype(BF16)

    (dpre,) = _matmul(
        "mlp_dpre", dz2b, w2_full, _sp((t.rows, d_model), lambda i, j, k: (i, 0)),
        _sp((t.ff_cols, d_model), lambda i, j, k: (j, 0)),
        grid=(seq // t.rows, d_ff // t.ff_cols, 1), j_outer=True, tb=True,
        extras=[(dact, _sp((t.rows, t.ff_cols), lambda i, j, k: (i, j))), after(scatter_w2.token)],
        out_shape=[sds((seq, d_ff), BF16)], out_specs=[_sp((t.rows, t.ff_cols), lambda i, j, k: (i, j))],
        epilogue=dpre_epi, n_split=t.ff_split)
    token_w2 = scatter_w2.combine_and_send(after=dpre)

    (dx1_mlp,) = _matmul(
        "mlp_dx", dpre, w1_full, _sp((t.rows, t.ff_k), lambda i, j, k: (i, k)),
        _sp((d_model, t.ff_k), lambda i, j, k: (0, k)),
        grid=(seq // t.rows, 1, d_ff // t.ff_k), tb=True, extras=[after(token_w2)],
        out_shape=[sds((seq, d_model), F32)], out_specs=[_sp((t.rows, d_model), row_full)])
    dz1, dz1b, g_mix_g, g_mix_b = _ln_bwd_rows(dx1_mlp, dz2, z1, ln_mix_g, ln_mix_b, t.ln_rows)

    def block_epi(acc, i, ex, out):
        out[0][0] = acc.astype(BF16)

    fs = d_ff // N_DEV
    (g_w1,) = _matmul(
        "grad_w_mlp_in", x1b, dpre, _sp((seq, t.grad_rows), lambda i, j, k: (0, i)),
        _sp((seq, fs), lambda i, j, k: (0, j)),
        grid=(d_model // t.grad_rows, N_DEV, 1), j_outer=True, ta=True,
        out_shape=[sds((N_DEV, d_model, fs), BF16)],
        out_specs=[_sp((1, t.grad_rows, fs), lambda i, j, k: (j, i, 0))], epilogue=block_epi)
    scatter_w1 = _SplitReduceScatter("scatter_w_mlp_in", [g_w1])

    (dy,) = _matmul(
        "mix_dy", dz1b, wout_full, _sp((t.rows, d_model), lambda i, j, k: (i, 0)),
        _sp((dh, d_model), lambda i, j, k: (j, 0)),
        grid=(seq // t.rows, 2, 1), j_outer=True, tb=True, extras=[after(scatter_w1.token)],
        out_shape=[sds((seq, d_model), F32)], out_specs=[_sp((t.rows, dh), lambda i, j, k: (i, j))],
        epilogue=plain_epi)
    token_w1 = scatter_w1.combine_and_send(after=dy)
    (g_wout,) = _matmul(
        "grad_w_out", y, dz1b, _sp((seq, t.grad_rows), lambda i, j, k: (0, i)), _sp((seq, d_model), vec, single=True),
        grid=(d_model // t.grad_rows, 1, 1), ta=True, extras=[after(token_w1)],
        out_shape=[sds((d_model, d_model), BF16)], out_specs=[_sp((t.grad_rows, d_model), row_full)],
        epilogue=bf16_epi)
    g_wout = g_wout.reshape(N_DEV, d_model // N_DEV, d_model)

    dproj_pool, g_wpool, g_pscale = _pool_bwd(d_pool, dy, wpool_full, pool_scale, token_w1, seq, d_model)
    dproj, g_pack, g_convb, g_wcat = _lru_bwd(p3, dy, h0p, h1p, dproj_pool, pack_full, conv_b, wcat,
                                              token_w1, seq, d_model)
    g_wa = jnp.stack([g_wcat[:, :, 0:lh], g_wcat[:, :, 2 * lh:3 * lh]])
    g_wi = jnp.stack([g_wcat[:, :, lh:2 * lh], g_wcat[:, :, 3 * lh:4 * lh]])

    rep_parts = [_rows128(g_wa), _rows128(g_wi), _rows128(g_mix_g), _rows128(g_mix_b), _rows128(g_ffn_g),
                 _rows128(g_ffn_b), _rows128(g_pscale), _rows128(g_convb)]
    rep_rows = [p.shape[0] for p in rep_parts]
    n_rep = sum(rep_rows)
    small = jnp.concatenate(rep_parts + [_rows128(g_pack)], axis=0)
    small_gather = _SplitGather("gather_small_grads", [(small[None], 0)], F32, after=small)

    ws = 3 * dh // N_DEV

    def pair_epi(acc, i, ex, out):
        out[0][0] = acc[:, :ws].astype(BF16)
        out[0][1] = acc[:, ws:].astype(BF16)

    (g_win,) = _matmul(
        "grad_w_in", xs, dproj, _sp((seq, t.grad_rows), lambda i, j, k: (0, i)),
        _sp((seq, 2 * ws), lambda i, j, k: (0, j)),
        grid=(d_model // t.grad_rows, N_DEV // 2, 1), ta=True, extras=[after(small_gather.token)],
        out_shape=[sds((N_DEV, d_model, ws), BF16)],
        out_specs=[_sp((2, t.grad_rows, ws), lambda i, j, k: (j, i, 0))], epilogue=pair_epi)
    scatter_mix = _SplitReduceScatter(
        "scatter_mixer", [g_win, g_wout, g_wpool.reshape(N_DEV, pg // N_DEV * len(POOL_WINDOWS), pg)])

    def adam_big(name, own_landed, w, m, v):
        own, landed = own_landed
        shp = w.shape
        two = lambda a: a.reshape(-1, shp[-1])
        res = _sum_adamw(name, own, landed, two(w), two(m), two(v))
        return [r.reshape(shp) for r in res]

    (r_w2,) = scatter_w2.wait(after=scatter_mix.token)
    o_w2 = adam_big("adam_w_mlp_out", r_w2, w_mlp_out, m_w_mlp_out, v_w_mlp_out)
    token_mix = scatter_mix.combine_and_send(after=o_w2[0])

    def dx_epi(acc, i, ex, out):
        out[0][...] = ALPHA * ex[0][...] + acc

    (dx,) = _matmul(
        "grad_x", dproj, win_full, _sp((t.ln_rows * 2, 3 * dh), lambda i, j, k: (i, 0)),
        _sp((d_model, 3 * dh), vec, single=True),
        grid=(seq // (t.ln_rows * 2), 1, 1), tb=True,
        extras=[(dz1, _sp((t.ln_rows * 2, d_model), row_full)), after(token_mix)],
        out_shape=[sds((seq, d_model), F32)], out_specs=[_sp((t.ln_rows * 2, d_model), row_full)],
        epilogue=dx_epi)
    (r_w1,) = scatter_w1.wait(after=dx)
    o_w1 = adam_big("adam_w_mlp_in", r_w1, w_mlp_in, m_w_mlp_in, v_w_mlp_in)
    r_win, r_wout, r_wpool = scatter_mix.wait(after=o_w1[0])
    o_win = adam_big("adam_w_in", r_win, w_in, m_w_in, v_w_in)
    o_wout = adam_big("adam_w_out", r_wout, w_out, m_w_out, v_w_out)
    o_wpool = adam_big("adam_w_pool", r_wpool, w_pool, m_w_pool, v_w_pool)

    small_gather.relay(after=o_win[0])
    (small_all,) = small_gather.wait(after=o_wpool[0])

    rep_w = [w_rg_a, w_rg_i, ln_mix_g, ln_mix_b, ln_ffn_g, ln_ffn_b, pool_scale, conv_b]
    rep_m = [m_w_rg_a, m_w_rg_i, m_ln_mix_g, m_ln_mix_b, m_ln_ffn_g, m_ln_ffn_b, m_pool_scale, m_conv_b]
    rep_v = [v_w_rg_a, v_w_rg_i, v_ln_mix_g, v_ln_mix_b, v_ln_ffn_g, v_ln_ffn_b, v_pool_scale, v_conv_b]
    cat = lambda arrs: jnp.concatenate([_rows128(a) for a in arrs], axis=0)
    o_rep = _sum_adamw("adam_replicated", None, small_all, cat(rep_w), cat(rep_m), cat(rep_v))

    my_idx = _dev_index(_where_am_i())
    head_parts = lax.dynamic_slice_in_dim(small_all, n_rep + my_idx * SMALL_ROWS, SMALL_ROWS, axis=1)
    o_head = _sum_adamw("adam_head", None, head_parts, pack_mine,
                        small_pack(m_conv_w, m_b_rg_a, m_b_rg_i, m_rg_lambda),
                        small_pack(v_conv_w, v_b_rg_a, v_b_rg_i, v_rg_lambda))

    def unpack_rep(packed):
        out, r = [], 0
        for wgt, rows in zip(rep_w, rep_rows):
            out.append(packed[r:r + rows].reshape(wgt.shape))
            r += rows
        return out

    def unpack_head(packed):
        return [packed[0:4].reshape(conv_w.shape), packed[4:6].reshape(b_rg_a.shape),
                packed[6:8].reshape(b_rg_i.shape), packed[8:10].reshape(rg_lambda.shape)]

    loss = lax.psum(loss_part[0, 0], ("x", "y", "c"))

    outs = [loss, dx[None]]
    for kind in range(4):
        ra, ri, mg, mb, fg, fb, ps, cb = unpack_rep(o_rep[kind])
        cw, ba, bi, lam = unpack_head(o_head[kind])
        outs += [mg, mb, o_win[kind], o_wpool[kind], ps, cw, cb, ra, ba, ri, bi, lam, o_wout[kind], fg, fb,
                 o_w1[kind], o_w2[kind]]
    return tuple(outs)
```

```python
import functools

import jax
import jax.numpy as jnp
from jax import lax
from jax.experimental import pallas as pl
from jax.experimental.pallas import tpu as pltpu

F32 = jnp.float32
BF16 = jnp.bfloat16
MESH = pl.DeviceIdType.MESH
ANY = pl.BlockSpec(memory_space=pl.ANY)

N_DEV = 8
POOL_WINDOWS = (2, 4, 8, 16)
N_HEADS = 8
RG_C = 8.0
LN_EPS = 1e-5
ALPHA = 2.0 ** 0.25
ADAM_LR = 0.001
ADAM_B1 = 0.9
ADAM_B2 = 0.999
ADAM_EPS = 1e-08
ADAM_WD = 0.01
ADAM_STEP = 10

VMEM_LIMIT = 56 * 1024 * 1024
WIN_HALO = 16
CONV_HALO = 8
SMALL_ROWS = 16


def _params(n_grid):
    return pltpu.CompilerParams(dimension_semantics=("arbitrary",) * n_grid, vmem_limit_bytes=VMEM_LIMIT)


def _shift(v, j):
    n = v.shape[0]
    s = (-j) % n
    return v if s == 0 else pltpu.roll(v, s, 0)


def _sigmoid(x):
    return 0.5 * jnp.tanh(0.5 * x) + 0.5


def _softplus(z):
    e = jnp.exp(-jnp.abs(z))
    u = 1.0 + e
    log1p = jnp.where(u == 1.0, e, jnp.log(u) * (e / jnp.where(u == 1.0, 1.0, u - 1.0)))
    return jnp.maximum(z, 0.0) + log1p


_GELU_C = 0.7978845608028654
_GELU_K = 0.044715


def _gelu_and_grad(x):
    x2 = x * x
    t = jnp.tanh(_GELU_C * (x + _GELU_K * x * x2))
    g = 0.5 * x * (1.0 + t)
    dg = 0.5 * (1.0 + t) + 0.5 * x * (1.0 - t * t) * (_GELU_C * (1.0 + 3.0 * _GELU_K * x2))
    return g, dg


def _ln_fwd(z, g, b):
    mu = jnp.mean(z, axis=-1, keepdims=True)
    zc = z - mu
    var = jnp.mean(zc * zc, axis=-1, keepdims=True)
    rstd = lax.rsqrt(var + LN_EPS)
    xhat = zc * rstd
    return xhat * g + b, xhat, rstd


def _ln_bwd(dy, xhat, rstd, g):
    dxhat = dy * g
    m1 = jnp.mean(dxhat, axis=-1, keepdims=True)
    m2 = jnp.mean(dxhat * xhat, axis=-1, keepdims=True)
    dz = rstd * (dxhat - m1 - xhat * m2)
    dg = jnp.sum(dy * xhat, axis=0, keepdims=True)
    db = jnp.sum(dy, axis=0, keepdims=True)
    return dz, dg, db


def _acc_rows(ref, first, val):
    @pl.when(first)
    def _():
        ref[...] = val

    @pl.when(jnp.logical_not(first))
    def _():
        ref[...] += val


def _sp(shape, fn, single=False):
    return shape, fn, single


def _matmul(name, a, b, a_spec, b_spec, *, grid, j_outer=False, ta=False, tb=False, extras=(), out_shape, out_specs,
            epilogue=None, n_split=1):
    ni, nj, nk = grid
    n_ex = len(extras)
    dims = (((0 if ta else 1,), (1 if tb else 0,)), ((), ()))

    def mk(spec):
        shape, fn, single = spec
        index = (lambda g0, g1, g2: fn(g1, g0, g2)) if j_outer else fn
        return pl.BlockSpec(shape, index, pipeline_mode=pl.Buffered(1)) if single else pl.BlockSpec(shape, index)

    def body(a_ref, b_ref, *rest):
        ex_refs = rest[:n_ex]
        out_refs = rest[n_ex:]
        i = pl.program_id(1 if j_outer else 0)
        if n_split > 1:
            av = a_ref[...].astype(BF16)
            width = b_ref.shape[0 if tb else 1] // n_split
            for c in range(n_split):
                cols = pl.ds(c * width, width)
                bv = (b_ref[cols, :] if tb else b_ref[:, cols]).astype(BF16)
                epilogue(lax.dot_general(av, bv, dims, preferred_element_type=F32), i, ex_refs, out_refs, cols)
            return
        part = lax.dot_general(a_ref[...].astype(BF16), b_ref[...].astype(BF16), dims, preferred_element_type=F32)
        if nk == 1:
            epilogue(part, i, ex_refs, out_refs)
        else:
            @pl.when(pl.program_id(2) == 0)
            def _():
                out_refs[0][...] = jnp.zeros(out_refs[0].shape, F32)

            out_refs[0][...] += part

    return pl.pallas_call(
        body, name=name, grid=(nj, ni, nk) if j_outer else (ni, nj, nk),
        in_specs=[mk(a_spec), mk(b_spec)] + [mk(s) for _, s in extras],
        out_specs=[mk(s) for s in out_specs], out_shape=list(out_shape),
        compiler_params=_params(3),
    )(a, b, *[x for x, _ in extras])


def _bs(shape, fn):
    return pl.BlockSpec(shape, fn)


def _where_am_i():
    x, y, c = lax.axis_index("x"), lax.axis_index("y"), lax.axis_index("c")
    return x, y, c


def _dev_index(p):
    return 4 * p[0] + 2 * p[1] + p[2]


def _slab(ref, axis, idx, size):
    sl = [slice(None)] * len(ref.shape)
    sl[axis] = pl.ds(idx * size, size)
    return ref.at[tuple(sl)]


def _all_gather(name, items):
    n = len(items)
    shapes = []
    for shard, axis in items:
        s = list(shard.shape)
        s[axis] *= N_DEV
        shapes.append(jax.ShapeDtypeStruct(tuple(s), shard.dtype))

    def body(*refs):
        in_refs, out_refs = refs[:n], refs[n:2 * n]
        send_sems, recv_sems, local_sems = refs[2 * n:]
        x, y, c = _where_am_i()
        me, sibling = (x, y, c), (x, y, 1 - c)
        chips = [(1 - x, y), (x, 1 - y), (1 - x, 1 - y)]

        def blk(a, p):
            axis = items[a][1]
            return _slab(out_refs[a], axis, _dev_index(p), items[a][0].shape[axis])

        def copy(a, k, block, to, src=None):
            return pltpu.make_async_remote_copy(
                src_ref=blk(a, block) if src is None else src, dst_ref=blk(a, block),
                send_sem=send_sems.at[a, k], recv_sem=recv_sems.at[a, k], device_id=to, device_id_type=MESH)

        mine = [pltpu.make_async_copy(in_refs[a], blk(a, me), local_sems.at[a]) for a in range(n)]
        for cp in mine:
            cp.start()
        first = []
        for a in range(n):
            first.append(copy(a, 0, me, sibling, src=in_refs[a]))
            first += [copy(a, 1 + j, me, (*chip, c), src=in_refs[a]) for j, chip in enumerate(chips)]
        for cp in first:
            cp.start()
        passed = []
        for a in range(n):
            for j, chip in enumerate(chips):
                copy(a, 1 + j, (*chip, c), me).wait_recv()
                fw = copy(a, 4 + j, (*chip, c), sibling)
                fw.start()
                passed.append(fw)
        for a in range(n):
            copy(a, 0, sibling, me).wait_recv()
            for j, chip in enumerate(chips):
                copy(a, 4 + j, (*chip, 1 - c), me).wait_recv()
        for cp in first + passed:
            cp.wait_send()
        for cp in mine:
            cp.wait()

    outs = pl.pallas_call(
        body, name=name, out_shape=shapes, in_specs=[ANY] * n, out_specs=[ANY] * n,
        scratch_shapes=[pltpu.SemaphoreType.DMA((n, 7)), pltpu.SemaphoreType.DMA((n, 7)),
                        pltpu.SemaphoreType.DMA((n,))],
    )(*[s for s, _ in items])
    return list(outs)


HBM = pl.BlockSpec(memory_space=pltpu.HBM)
SEM = pl.BlockSpec(memory_space=pltpu.SEMAPHORE)
DATAFLOW = pltpu.SideEffectType.DATAFLOW_SIDE_EFFECTING


def _in_hbm(a):
    return pltpu.with_memory_space_constraint(a, pltpu.HBM)


def _token_shape():
    return jax.ShapeDtypeStruct((8, 128), F32)


def _split_start(name, n_sems, bufs, issue):
    nb = len(bufs)

    def body(*refs):
        issue(refs[:nb], refs[nb], refs[nb + 1])
        refs[-1][...] = jnp.zeros((8, 128), F32)

    outs = pl.pallas_call(
        body, name=name,
        out_shape=(pltpu.SemaphoreType.DMA((n_sems,)), pltpu.SemaphoreType.DMA((n_sems,)),
                   *[pltpu.HBM(b.shape, b.dtype) for b in bufs], _token_shape()),
        in_specs=[HBM] * nb, out_specs=(SEM, SEM, *[HBM] * nb, pl.BlockSpec(memory_space=pltpu.VMEM)),
        input_output_aliases={i: 2 + i for i in range(nb)},
        compiler_params=pltpu.CompilerParams(has_side_effects=DATAFLOW),
    )(*[_in_hbm(b) for b in bufs])
    return outs[0], outs[1], list(outs[2:2 + nb]), outs[-1]


def _split_relay(name, n_sems, sems, bufs, after, relay):
    nb = len(bufs)

    def body(*refs):
        relay(refs[:nb], refs[nb], refs[nb + 1], refs[nb + 3], refs[nb + 4])
        refs[-1][...] = jnp.zeros((8, 128), F32)

    outs = pl.pallas_call(
        body, name=name,
        out_shape=(pltpu.SemaphoreType.DMA((n_sems,)), pltpu.SemaphoreType.DMA((n_sems,)),
                   *[pltpu.HBM(b.shape, b.dtype) for b in bufs], _token_shape()),
        in_specs=[HBM] * nb + [SEM, SEM, ANY],
        out_specs=(SEM, SEM, *[HBM] * nb, pl.BlockSpec(memory_space=pltpu.VMEM)),
        input_output_aliases={i: 2 + i for i in range(nb)},
        compiler_params=pltpu.CompilerParams(has_side_effects=DATAFLOW),
    )(*bufs, sems[0], sems[1], after)
    return outs[0], outs[1], list(outs[2:2 + nb]), outs[-1]


def _split_wait(name, sems, bufs, after, finish):
    nb = len(bufs)

    def body(*refs):
        finish(refs[:nb], refs[nb], refs[nb + 1])

    outs = pl.pallas_call(
        body, name=name, out_shape=[pltpu.HBM(b.shape, b.dtype) for b in bufs],
        in_specs=[HBM] * nb + [SEM, SEM, ANY], out_specs=[HBM] * nb,
        input_output_aliases={i: i for i in range(nb)},
        compiler_params=pltpu.CompilerParams(has_side_effects=DATAFLOW),
    )(*bufs, sems[0], sems[1], after)
    return list(outs)


def _place(name, items, dtype, after):
    ids = jnp.reshape(_dev_index(_where_am_i()), (1,)).astype(jnp.int32)
    outs = []
    for a, (shard, axis) in enumerate(items):
        rows, cols = shard.shape[-2], shard.shape[-1]
        tr = rows
        while tr * cols * shard.dtype.itemsize > 4 * 1024 * 1024 and tr % 32 == 0:
            tr //= 2
        nt = rows // tr
        full = list(shard.shape)
        full[axis] *= N_DEV
        if shard.ndim == 2 and axis == 0:
            in_spec = _bs((tr, cols), lambda i, ids: (i, 0))
            out_spec = _bs((tr, cols), lambda i, ids, nt=nt: (ids[0] * nt + i, 0))
        elif shard.ndim == 2 and axis == 1:
            in_spec = _bs((tr, cols), lambda i, ids: (i, 0))
            out_spec = _bs((tr, cols), lambda i, ids: (i, ids[0]))
        else:
            assert shard.ndim == 3 and axis == 0 and shard.shape[0] == 1
            in_spec = _bs((None, tr, cols), lambda i, ids: (0, i, 0))
            out_spec = _bs((None, tr, cols), lambda i, ids: (ids[0], i, 0))

        def body(ids_ref, in_ref, after_ref, out_ref):
            del ids_ref, after_ref
            out_ref[...] = in_ref[...].astype(out_ref.dtype)

        outs.append(pl.pallas_call(
            body, name=f"{name}{a}",
            grid_spec=pltpu.PrefetchScalarGridSpec(
                num_scalar_prefetch=1, grid=(nt,), in_specs=[in_spec, ANY], out_specs=out_spec),
            out_shape=jax.ShapeDtypeStruct(tuple(full), dtype), compiler_params=_params(1),
        )(ids, shard, after))
    return outs


class _SplitGather:
    def __init__(self, name, items, dtype, after):
        self.name, self.items, self.n = name, items, len(items)
        fulls = _place(name + "_place", items, dtype, after)
        n = self.n

        def issue(refs, send, recv):
            me, sibling, chips, c = self._geometry()
            for a in range(n):
                self._copy1(refs, send, recv, a, 0, me, sibling).start()
                for j, chip in enumerate(chips):
                    self._copy1(refs, send, recv, a, 1 + j, me, (*chip, c)).start()

        self.send, self.recv, self.bufs, self.token = _split_start(name + "_start", 4 * n, fulls, issue)

    @staticmethod
    def _geometry():
        x, y, c = _where_am_i()
        return (x, y, c), (x, y, 1 - c), [(1 - x, y), (x, 1 - y), (1 - x, 1 - y)], c

    def _blk(self, refs, a, p):
        shard, axis = self.items[a]
        return _slab(refs[a], axis, _dev_index(p), shard.shape[axis])

    def _copy1(self, refs, send, recv, a, k, owner, to):
        return pltpu.make_async_remote_copy(
            src_ref=self._blk(refs, a, owner), dst_ref=self._blk(refs, a, owner), send_sem=send.at[4 * a + k],
            recv_sem=recv.at[4 * a + k], device_id=to, device_id_type=MESH)

    def _copy2(self, refs, send, recv, a, j, owner, to):
        return pltpu.make_async_remote_copy(
            src_ref=self._blk(refs, a, owner), dst_ref=self._blk(refs, a, owner), send_sem=send.at[3 * a + j],
            recv_sem=recv.at[3 * a + j], device_id=to, device_id_type=MESH)

    def relay(self, after):
        n = self.n

        def relay(refs, send_in, recv_in, send_out, recv_out):
            me, sibling, chips, c = self._geometry()
            for a in range(n):
                for j, chip in enumerate(chips):
                    self._copy1(refs, send_in, recv_in, a, 1 + j, (*chip, c), me).wait_recv()
                    self._copy2(refs, send_out, recv_out, a, j, (*chip, c), sibling).start()
            for a in range(n):
                self._copy1(refs, send_in, recv_in, a, 0, sibling, me).wait_recv()
                for k in range(4):
                    self._copy1(refs, send_in, recv_in, a, k, me, sibling).wait_send()

        self.send, self.recv, self.bufs, self.token = _split_relay(
            self.name + "_relay", 3 * n, (self.send, self.recv), self.bufs, after, relay)
        return self.token

    def wait(self, after):
        n = self.n

        def finish(refs, send, recv):
            me, sibling, chips, c = self._geometry()
            for a in range(n):
                for j, chip in enumerate(chips):
                    self._copy2(refs, send, recv, a, j, (*chip, 1 - c), me).wait_recv()
                    self._copy2(refs, send, recv, a, j, (*chip, c), sibling).wait_send()

        return _split_wait(self.name + "_wait", (self.send, self.recv), self.bufs, after, finish)


class _SplitReduceScatter:
    def __init__(self, name, grads):
        self.name, self.n = name, len(grads)
        n = self.n
        g4 = [g.reshape(4, 2, *g.shape[1:]) for g in grads]
        land = [lax.empty((4, 1, *g.shape[1:]), g.dtype) for g in grads]

        def issue(refs, send, recv):
            for a in range(n):
                self._swap(refs, send, recv, a).start()

        self.send, self.recv, self.bufs, self.token = _split_start(name + "_d2d_start", n, g4 + land, issue)

    def _swap(self, refs, send, recv, a):
        x, y, c = _where_am_i()
        return pltpu.make_async_remote_copy(
            src_ref=refs[a].at[:, pl.ds(1 - c, 1)], dst_ref=refs[self.n + a], send_sem=send.at[a], recv_sem=recv.at[a],
            device_id=(x, y, 1 - c), device_id_type=MESH)

    def _hop(self, refs, send, recv, a, m):
        x, y, c = _where_am_i()
        px = (1 - x) if m & 2 else x
        py = (1 - y) if m & 1 else y
        return pltpu.make_async_remote_copy(
            src_ref=refs[a].at[2 * px + py], dst_ref=refs[self.n + a].at[m - 1], send_sem=send.at[3 * a + m - 1],
            recv_sem=recv.at[3 * a + m - 1], device_id=(px, py, c), device_id_type=MESH)

    def combine_and_send(self, after):
        n = self.n

        def finish(refs, send, recv):
            for a in range(n):
                self._swap(refs, send, recv, a).wait()

        bufs = _split_wait(self.name + "_d2d_wait", (self.send, self.recv), self.bufs, after, finish)
        x, y, c = _where_am_i()
        ids = jnp.stack([c, 2 * x + y]).astype(jnp.int32)
        self.own, sums = [], []
        for a in range(n):
            own, hb = _pair_sum(f"{self.name}_sum{a}", bufs[a], bufs[n + a], ids)
            self.own.append(own)
            sums.append(hb)
        land = [lax.empty((3, *h.shape[1:]), h.dtype) for h in sums]

        def issue(refs, send, recv):
            for a in range(n):
                for m in (1, 2, 3):
                    self._hop(refs, send, recv, a, m).start()

        self.send, self.recv, self.bufs, self.token = _split_start(self.name + "_ici_start", 3 * n, sums + land, issue)
        return self.token

    def wait(self, after):
        n = self.n

        def finish(refs, send, recv):
            for a in range(n):
                for m in (1, 2, 3):
                    self._hop(refs, send, recv, a, m).wait()

        bufs = _split_wait(self.name + "_ici_wait", (self.send, self.recv), self.bufs, after, finish)
        return list(zip(self.own, bufs[n:]))


def _pair_sum(name, g4, land, ids):
    rows, cols = g4.shape[2], g4.shape[3]
    tr = rows
    while tr * cols * 2 > 1024 * 1024 and tr % 32 == 0:
        tr //= 2

    def body(ids_ref, g_ref, l_ref, own_ref, sum_ref):
        h = g_ref[...].astype(F32) + l_ref[...].astype(F32)
        sum_ref[...] = h.astype(sum_ref.dtype)

        @pl.when(pl.program_id(1) == ids_ref[1])
        def _():
            own_ref[...] = h

    return pl.pallas_call(
        body, name=name,
        grid_spec=pltpu.PrefetchScalarGridSpec(
            num_scalar_prefetch=1, grid=(rows // tr, 4),
            in_specs=[_bs((None, None, tr, cols), lambda i, q, ids: (q, ids[0], i, 0)),
                      _bs((None, None, tr, cols), lambda i, q, ids: (q, 0, i, 0))],
            out_specs=[_bs((tr, cols), lambda i, q, ids: (i, 0)), _bs((None, tr, cols), lambda i, q, ids: (q, i, 0))]),
        out_shape=[jax.ShapeDtypeStruct((rows, cols), F32), jax.ShapeDtypeStruct((4, rows, cols), g4.dtype)],
        compiler_params=_params(2),
    )(ids, g4, land)


def _win_sum(ext, w, off):
    s = ext + _shift(ext, -1)
    if w >= 4:
        s = _shift(s, -1) + _shift(s, 1)
    if w >= 8:
        s = _shift(s, -2) + _shift(s, 2)
    if w >= 16:
        s = _shift(s, -4) + _shift(s, 4)
    return _shift(s, off) if off else s


def _inv_count(r0, t, w, seq):
    pos = r0 + lax.broadcasted_iota(jnp.int32, (t, 1), 0)
    cnt = jnp.minimum(pos + w // 2, seq) - jnp.maximum(pos - w // 2, 0)
    return 1.0 / cnt.astype(F32)


def _pool_fwd(p3, w_pool, pool_scale, seq, d_model):
    dp = d_model // 2
    pg = dp // len(POOL_WINDOWS)
    t = min(128, seq)
    n_chunks = seq // t
    h = WIN_HALO

    def body(u_ref, w_ref, sc_ref, d_ref, y_ref, pad_ref):
        g = pl.program_id(0)
        zeros = jnp.zeros((h, pg), F32)
        pad_ref[0:h, :] = zeros
        pad_ref[h + seq:h + seq + h, :] = zeros

        def fill(ci, _):
            r0 = pl.multiple_of(ci * t, t)
            pad_ref[pl.ds(h + r0, t), :] = u_ref[pl.ds(r0, t), :]
            return 0

        lax.fori_loop(0, n_chunks, fill, 0)
        wmat = w_ref[...]
        scale = sc_ref[...]
        for gi, w in enumerate(POOL_WINDOWS):
            @pl.when(g == gi)
            def _(w=w):
                def chunk(ci, _):
                    r0 = pl.multiple_of(ci * t, t)
                    ext = pad_ref[pl.ds(r0, t + 2 * h), :]
                    mean = _win_sum(ext, w, 0)[h:h + t, :] * _inv_count(r0, t, w, seq)
                    d = (mean - ext[h:h + t, :]).astype(BF16)
                    d_ref[pl.ds(r0, t), :] = d
                    q = jnp.dot(d, wmat, preferred_element_type=F32)
                    y_ref[pl.ds(r0, t), :] = (q * scale).astype(BF16)
                    return 0

                lax.fori_loop(0, n_chunks, chunk, 0)

    return pl.pallas_call(
        body, name="pool_fwd", grid=(len(POOL_WINDOWS),),
        in_specs=[_bs((None, seq, pg), lambda g: (0, 0, g)), _bs((None, pg, pg), lambda g: (g, 0, 0)),
                  _bs((1, pg), lambda g: (0, g))],
        out_specs=[_bs((seq, pg), lambda g: (0, g)), _bs((seq, pg), lambda g: (0, g))],
        out_shape=[jax.ShapeDtypeStruct((seq, dp), BF16), jax.ShapeDtypeStruct((seq, d_model), BF16)],
        scratch_shapes=[pltpu.VMEM((seq + 2 * h, pg), F32)],
        compiler_params=_params(1),
    )(p3, w_pool, pool_scale)


def _pool_bwd(d, dy, w_pool, pool_scale, token, seq, d_model):
    dp = d_model // 2
    pg = dp // len(POOL_WINDOWS)
    t = min(128, seq)
    n_chunks = seq // t
    h = WIN_HALO
    tn_dims = (((0,), (0,)), ((), ()))
    nt_dims = (((1,), (1,)), ((), ()))

    def body(d_ref, dy_ref, w_ref, sc_ref, tok_ref, du_ref, dwb_ref, dsc_ref, pad_ref, dd_ref, dw_ref):
        del tok_ref
        g = pl.program_id(0)
        zeros = jnp.zeros((h, pg), F32)
        pad_ref[0:h, :] = zeros
        pad_ref[h + seq:h + seq + h, :] = zeros
        wmat = w_ref[...]
        scale = sc_ref[...]
        for gi, w in enumerate(POOL_WINDOWS):
            @pl.when(g == gi)
            def _(w=w):
                dw_ref[...] = jnp.zeros((pg, pg), F32)

                def first(ci, dsc):
                    r0 = pl.multiple_of(ci * t, t)
                    dv = d_ref[pl.ds(r0, t), :]
                    dyv = dy_ref[pl.ds(r0, t), :]
                    q = jnp.dot(dv, wmat, preferred_element_type=F32)
                    dsc = dsc + jnp.sum(dyv * q, axis=0, keepdims=True)
                    dq = (dyv * scale).astype(BF16)
                    dw_ref[...] += lax.dot_general(dv, dq, tn_dims, preferred_element_type=F32)
                    dd = lax.dot_general(dq, wmat, nt_dims, preferred_element_type=F32)
                    dd_ref[pl.ds(r0, t), :] = dd
                    pad_ref[pl.ds(h + r0, t), :] = dd * _inv_count(r0, t, w, seq)
                    return dsc

                dsc_ref[...] = lax.fori_loop(0, n_chunks, first, jnp.zeros((1, pg), F32))
                dwb_ref[...] = dw_ref[...].reshape(N_DEV, pg // N_DEV, pg).astype(BF16)

                def second(ci, _):
                    r0 = pl.multiple_of(ci * t, t)
                    ext = pad_ref[pl.ds(r0, t + 2 * h), :]
                    back = _win_sum(ext, w, 1)[h:h + t, :]
                    du_ref[pl.ds(r0, t), :] = (back - dd_ref[pl.ds(r0, t), :]).astype(BF16)
                    return 0

                lax.fori_loop(0, n_chunks, second, 0)

    return pl.pallas_call(
        body, name="pool_bwd", grid=(len(POOL_WINDOWS),),
        in_specs=[_bs((seq, pg), lambda g: (0, g)), _bs((seq, pg), lambda g: (0, g)),
                  _bs((None, pg, pg), lambda g: (g, 0, 0)), _bs((1, pg), lambda g: (0, g)),
                  _bs((8, 128), lambda g: (0, 0))],
        out_specs=[_bs((seq, pg), lambda g: (0, g)), _bs((N_DEV, None, pg // N_DEV, pg), lambda g: (0, g, 0, 0)),
                   _bs((1, pg), lambda g: (0, g))],
        out_shape=[jax.ShapeDtypeStruct((seq, 3 * dp), BF16),
                   jax.ShapeDtypeStruct((N_DEV, len(POOL_WINDOWS), pg // N_DEV, pg), BF16),
                   jax.ShapeDtypeStruct((1, dp), F32)],
        scratch_shapes=[pltpu.VMEM((seq + 2 * h, pg), F32), pltpu.VMEM((seq, pg), F32), pltpu.VMEM((pg, pg), F32)],
        compiler_params=_params(1),
    )(d, dy, w_pool, pool_scale, token)


def _tile_scan(n_tiles, lanes, loads, stores):
    row = lax.broadcasted_iota(jnp.int32, (8, lanes), 0)
    group = 8

    def local_scan(n, k):
        aa, bb = loads[n](k)
        for sh in (1, 2, 4):
            if n == 0:
                ok = row >= sh
                ap = jnp.where(ok, pltpu.roll(aa, sh, 0), 1.0)
                bp = jnp.where(ok, pltpu.roll(bb, sh, 0), 0.0)
            else:
                ok = row < 8 - sh
                ap = jnp.where(ok, pltpu.roll(aa, 8 - sh, 0), 1.0)
                bp = jnp.where(ok, pltpu.roll(bb, 8 - sh, 0), 0.0)
            bb = aa * bp + bb
            aa = aa * ap
        return aa, bb

    def step(s, carry):
        carry = list(carry)
        for n in range(2):
            tiles = [s * group + u if n == 0 else n_tiles - 1 - (s * group + u) for u in range(group)]
            local = [local_scan(n, k) for k in tiles]
            for k, (aa, bb) in zip(tiles, local):
                hh = bb + aa * carry[n]
                stores[n](k, hh)
                carry[n] = jnp.broadcast_to(hh[7:8, :] if n == 0 else hh[0:1, :], (8, lanes))
        return tuple(carry)

    zeros = jnp.zeros((8, lanes), F32)
    lax.fori_loop(0, n_tiles // group, step, (zeros, zeros))


def _gate_preacts(xc, wcat_ref):
    xcb = xc.astype(BF16)
    return xcb, jnp.dot(xcb, wcat_ref[...], preferred_element_type=F32)


def _gates(pre, n, pk_ref, sp):
    lh = pre.shape[1] // 4
    r = _sigmoid(pre[:, (2 * n) * lh:(2 * n + 1) * lh] + pk_ref[pl.ds(4 + n, 1), :])
    i = _sigmoid(pre[:, (2 * n + 1) * lh:(2 * n + 2) * lh] + pk_ref[pl.ds(6 + n, 1), :])
    log_a = (-RG_C * r) * sp[n]
    a = jnp.exp(log_a)
    x = 2.0 * log_a
    one_minus_a2 = jnp.where(x > -0.01, -(x * (1.0 + x * (0.5 + x * (1.0 / 6.0)))), 1.0 - a * a)
    m = jnp.sqrt(one_minus_a2)
    return r, i, a, m


def _conv_chunk(upad_ref, pk_ref, cb, r0, t):
    ext = upad_ref[pl.ds(r0, t + 2 * CONV_HALO), :]
    acc = pk_ref[pl.ds(1, 1), :] * ext
    for k in (0, 2, 3):
        acc = acc + pk_ref[pl.ds(k, 1), :] * _shift(ext, k - 1)
    return acc[CONV_HALO:CONV_HALO + t, :] + cb, ext


def _lru_fwd(p3, y_in, pack, conv_b, wcat, token, seq, d_model):
    dl = d_model // 2
    lh = dl // N_HEADS
    t = min(128, seq)
    n_chunks = seq // t
    seg = seq // 8
    hal = CONV_HALO
    first_rec_block = (d_model - dl) // lh

    def body(ur_ref, ug_ref, pk_ref, cb_ref, wcat_ref, yin_ref, tok_ref, y_ref, h0_ref, h1_ref,
             upad, a_scr, b_scr):
        del yin_ref, tok_ref
        zeros = jnp.zeros((hal, lh), F32)
        upad[0:hal, :] = zeros
        upad[hal + seq:hal + seq + hal, :] = zeros
        for ref in (h0_ref, h1_ref):
            ref[0:hal, :] = zeros
            ref[hal + seq:hal + seq + hal, :] = zeros

        def fill(ci, _):
            r0 = pl.multiple_of(ci * t, t)
            upad[pl.ds(hal + r0, t), :] = ur_ref[pl.ds(r0, t), :]
            return 0

        lax.fori_loop(0, n_chunks, fill, 0)
        cb = cb_ref[...]
        sp = [_softplus(-pk_ref[pl.ds(8 + n, 1), :]) for n in range(2)]

        def chunk(ci, _):
            r0 = pl.multiple_of(ci * t, t)
            xc, _ext = _conv_chunk(upad, pk_ref, cb, r0, t)
            _, pre = _gate_preacts(xc, wcat_ref)
            for n in range(2):
                _, i, a, m = _gates(pre, n, pk_ref, sp)
                a_scr[n, pl.ds(r0, t), :] = a
                b_scr[n, pl.ds(r0, t), :] = (m * i) * xc
            return 0

        lax.fori_loop(0, n_chunks, chunk, 0, unroll=2)

        def load(n):
            def get(k):
                at = pl.ds(pl.multiple_of(k * 8, 8), 8)
                return a_scr[n, at, :], b_scr[n, at, :]
            return get

        def store(ref):
            def put(k, v):
                ref[pl.ds(pl.multiple_of(hal + k * 8, 8), 8), :] = v
            return put

        _tile_scan(seq // 8, lh, [load(0), load(1)], [store(h0_ref), store(h1_ref)])

        def out(ci, _):
            r0 = pl.multiple_of(ci * t, t)
            hsum = h0_ref[pl.ds(hal + r0, t), :] + h1_ref[pl.ds(hal + r0, t), :]
            gl, _dg = _gelu_and_grad(ug_ref[pl.ds(r0, t), :])
            y_ref[pl.ds(r0, t), :] = (hsum * gl).astype(BF16)
            return 0

        lax.fori_loop(0, n_chunks, out, 0)

    return pl.pallas_call(
        body, name="lru_fwd", grid=(N_HEADS,),
        in_specs=[_bs((None, seq, lh), lambda h: (1, 0, h)), _bs((None, seq, lh), lambda h: (2, 0, h)),
                  _bs((None, SMALL_ROWS, lh), lambda h: (h, 0, 0)), _bs((1, lh), lambda h: (0, h)),
                  _bs((None, lh, 4 * lh), lambda h: (h, 0, 0)),
                  ANY, _bs((8, 128), lambda h: (0, 0))],
        out_specs=[_bs((seq, lh), lambda h: (0, first_rec_block + h)),
                   _bs((seq + 2 * hal, lh), lambda h: (0, h)), _bs((seq + 2 * hal, lh), lambda h: (0, h))],
        out_shape=[jax.ShapeDtypeStruct((seq, d_model), BF16), jax.ShapeDtypeStruct((seq + 2 * hal, dl), F32),
                   jax.ShapeDtypeStruct((seq + 2 * hal, dl), F32)],
        scratch_shapes=[pltpu.VMEM((seq + 2 * hal, lh), F32), pltpu.VMEM((2, seq, lh), F32),
                        pltpu.VMEM((2, seq, lh), F32)],
        input_output_aliases={5: 0},
        compiler_params=_params(1),
    )(p3, p3, pack, conv_b, wcat, y_in, token)


def _lru_bwd(p3, dy, h0p, h1p, dproj_in, pack, conv_b, wcat, token, seq, d_model):
    dl = d_model // 2
    lh = dl // N_HEADS
    t = min(128, seq)
    n_chunks = seq // t
    seg = seq // 8
    hal = CONV_HALO
    first_rec_block = (d_model - dl) // lh
    tn_dims = (((0,), (0,)), ((), ()))
    nt_dims = (((1,), (1,)), ((), ()))

    def body(ur_ref, ug_ref, dy_ref, h0_ref, h1_ref, pk_ref, cb_ref, wcat_ref, tok_ref, din_ref,
             dproj_ref, dpk_ref, dcb_ref, dwcat_ref,
             upad, a_scr, dh_scr, g_scr, dxc_pad, dpr_ref, out_sems, gate_scr):
        del din_ref, tok_ref
        zeros = jnp.zeros((hal, lh), F32)
        for ref in (upad, dxc_pad):
            ref[0:hal, :] = zeros
            ref[hal + seq:hal + seq + hal, :] = zeros
        for n in range(2):
            a_scr[n, 0:hal, :] = zeros
            a_scr[n, hal + seq:hal + seq + hal, :] = zeros

        def fill(ci, _):
            r0 = pl.multiple_of(ci * t, t)
            upad[pl.ds(hal + r0, t), :] = ur_ref[pl.ds(r0, t), :]
            return 0

        lax.fori_loop(0, n_chunks, fill, 0)
        cb = cb_ref[...]
        lam = [pk_ref[pl.ds(8 + n, 1), :] for n in range(2)]
        sp = [_softplus(-lam[n]) for n in range(2)]

        def chunk1(ci, _):
            r0 = pl.multiple_of(ci * t, t)
            xc, _ext = _conv_chunk(upad, pk_ref, cb, r0, t)
            _, pre = _gate_preacts(xc, wcat_ref)
            for n in range(2):
                r, i, a, m = _gates(pre, n, pk_ref, sp)
                a_scr[n, pl.ds(hal + r0, t), :] = a
                for q, v in enumerate((r, i, m)):
                    gate_scr[3 * n + q, pl.ds(r0, t), :] = v
            hsum = h0_ref[pl.ds(hal + r0, t), :] + h1_ref[pl.ds(hal + r0, t), :]
            gl, dgl = _gelu_and_grad(ug_ref[pl.ds(r0, t), :])
            dyv = dy_ref[pl.ds(r0, t), :]
            dh_scr[pl.ds(r0, t), :] = dyv * gl
            dpr_ref[1, pl.ds(r0, t), :] = ((dyv * hsum) * dgl).astype(BF16)
            return 0

        lax.fori_loop(0, n_chunks, chunk1, 0, unroll=2)

        def load(n):
            def get(k):
                r0 = pl.multiple_of(k * 8, 8)
                if n == 0:
                    coef = _shift(a_scr[0, pl.ds(pl.multiple_of(hal + r0, 8), 16), :], 1)[0:8, :]
                else:
                    coef = _shift(a_scr[1, pl.ds(pl.multiple_of(hal + r0 - 8, 8), 16), :], -1)[8:16, :]
                return coef, dh_scr[pl.ds(r0, 8), :]
            return get

        def store(n):
            def put(k, v):
                g_scr[n, pl.ds(pl.multiple_of(k * 8, 8), 8), :] = v
            return put

        _tile_scan(seq // 8, lh, [load(1), load(0)], [store(1), store(0)])

        dwcat_ref[...] = jnp.zeros((lh, 4 * lh), F32)

        def chunk3(ci, carry):
            dba, dbi, dlam, dcb = carry
            r0 = pl.multiple_of(ci * t, t)
            xc, _ext = _conv_chunk(upad, pk_ref, cb, r0, t)
            xcb = xc.astype(BF16)
            dxc = jnp.zeros((t, lh), F32)
            dba, dbi, dlam = list(dba), list(dbi), list(dlam)
            dpre = []
            for n in range(2):
                r, i, m = (gate_scr[3 * n + q, pl.ds(r0, t), :] for q in range(3))
                a = a_scr[n, pl.ds(hal + r0, t), :]
                hext = (h0_ref if n == 0 else h1_ref)[pl.ds(r0, t + 2 * hal), :]
                hprev = _shift(hext, -1 if n == 0 else 1)[hal:hal + t, :]
                gb = g_scr[n, pl.ds(r0, t), :]
                da = gb * hprev
                dm = gb * i * xc
                di = gb * m * xc
                dxc = dxc + gb * (m * i)
                dlog_a = da * a - dm * (a * a) / m
                dr = dlog_a * (-RG_C * sp[n])
                dlam[n] = dlam[n] + jnp.sum(dlog_a * r, axis=0, keepdims=True)
                dpr = dr * r * (1.0 - r)
                dpi = di * i * (1.0 - i)
                dba[n] = dba[n] + jnp.sum(dpr, axis=0, keepdims=True)
                dbi[n] = dbi[n] + jnp.sum(dpi, axis=0, keepdims=True)
                dpre += [dpr.astype(BF16), dpi.astype(BF16)]
            dpre = jnp.concatenate(dpre, axis=1)
            dwcat_ref[...] += lax.dot_general(xcb, dpre, tn_dims, preferred_element_type=F32)
            dxc = dxc + lax.dot_general(dpre, wcat_ref[...], nt_dims, preferred_element_type=F32)
            dxc_pad[pl.ds(hal + r0, t), :] = dxc
            dcb = dcb + jnp.sum(dxc, axis=0, keepdims=True)
            return tuple(dba), tuple(dbi), tuple(dlam), dcb

        zr = jnp.zeros((1, lh), F32)
        def chunk3_pair(cj, carry):
            return chunk3(2 * cj + 1, chunk3(2 * cj, carry))

        dba, dbi, dlam, dcb = lax.fori_loop(0, n_chunks // 2, chunk3_pair, ((zr, zr), (zr, zr), (zr, zr), zr))
        dcb_ref[...] = dcb
        for n in range(2):
            dpk_ref[pl.ds(4 + n, 1), :] = dba[n]
            dpk_ref[pl.ds(6 + n, 1), :] = dbi[n]
            dpk_ref[pl.ds(8 + n, 1), :] = dlam[n] * (RG_C * jax.nn.sigmoid(-lam[n]))
        dpk_ref[pl.ds(10, SMALL_ROWS - 10), :] = jnp.zeros((SMALL_ROWS - 10, lh), F32)

        def chunk4(ci, dtap):
            r0 = pl.multiple_of(ci * t, t)
            gext = dxc_pad[pl.ds(r0, t + 2 * hal), :]
            uext = upad[pl.ds(r0, t + 2 * hal), :]
            gmid = gext[hal:hal + t, :]
            du = pk_ref[pl.ds(1, 1), :] * gext
            for k in (0, 2, 3):
                du = du + pk_ref[pl.ds(k, 1), :] * _shift(gext, 1 - k)
            dpr_ref[0, pl.ds(r0, t), :] = du[hal:hal + t, :].astype(BF16)
            out = []
            for k in range(4):
                usl = _shift(uext, k - 1)[hal:hal + t, :]
                out.append(dtap[k] + jnp.sum(gmid * usl, axis=0, keepdims=True))
            return tuple(out)

        dtap = lax.fori_loop(0, n_chunks, chunk4, (zr, zr, zr, zr))
        for k in range(4):
            dpk_ref[pl.ds(k, 1), :] = dtap[k]

        head = pl.program_id(0)
        outs = [pltpu.make_async_copy(
            dpr_ref.at[b], dproj_ref.at[:, pl.ds(pl.multiple_of((1 + b) * dl + head * lh, lh), lh)], out_sems.at[b])
            for b in range(2)]
        for cp in outs:
            cp.start()
        for cp in outs:
            cp.wait()

    return pl.pallas_call(
        body, name="lru_bwd", grid=(N_HEADS,),
        in_specs=[_bs((None, seq, lh), lambda h: (1, 0, h)), _bs((None, seq, lh), lambda h: (2, 0, h)),
                  _bs((seq, lh), lambda h: (0, first_rec_block + h)),
                  _bs((seq + 2 * hal, lh), lambda h: (0, h)), _bs((seq + 2 * hal, lh), lambda h: (0, h)),
                  _bs((None, SMALL_ROWS, lh), lambda h: (h, 0, 0)), _bs((1, lh), lambda h: (0, h)),
                  _bs((None, lh, 4 * lh), lambda h: (h, 0, 0)),
                  _bs((8, 128), lambda h: (0, 0)), ANY],
        out_specs=[ANY, _bs((None, SMALL_ROWS, lh), lambda h: (h, 0, 0)),
                   _bs((1, lh), lambda h: (0, h)), _bs((None, lh, 4 * lh), lambda h: (h, 0, 0))],
        out_shape=[jax.ShapeDtypeStruct((seq, 3 * dl), BF16), jax.ShapeDtypeStruct((N_HEADS, SMALL_ROWS, lh), F32),
                   jax.ShapeDtypeStruct((1, dl), F32), jax.ShapeDtypeStruct((N_HEADS, lh, 4 * lh), F32)],
        scratch_shapes=[pltpu.VMEM((seq + 2 * hal, lh), F32), pltpu.VMEM((2, seq + 2 * hal, lh), F32),
                        pltpu.VMEM((seq, lh), F32), pltpu.VMEM((2, seq, lh), F32),
                        pltpu.VMEM((seq + 2 * hal, lh), F32), pltpu.VMEM((2, seq, lh), BF16),
                        pltpu.SemaphoreType.DMA((2,)), pltpu.VMEM((6, seq, lh), F32)],
        input_output_aliases={9: 0},
        compiler_params=_params(1),
    )(p3, p3, dy, h0p, h1p, pack, conv_b, wcat, token, dproj_in)


class _tiles:
    def __init__(self, seq, d_model, d_ff):
        self.rows = min(1024, seq)
        self.ln_rows = min(256, seq)
        self.ff_cols = min(1024, d_ff)
        self.ff_split = 4
        self.ff_k = min(2048, d_ff)
        self.grad_rows = 512


def _ln_loss_bwd(ffn, x1, tgt, g, b, tr):
    seq, d = ffn.shape

    def body(f_ref, x_ref, t_ref, g_ref, b_ref, dz_ref, dzb_ref, dg_ref, db_ref, loss_ref):
        i = pl.program_id(0)
        gv = g_ref[...]
        z = ALPHA * x_ref[...] + f_ref[...]
        y, xhat, rstd = _ln_fwd(z, gv, b_ref[...])
        err = y - t_ref[...]
        part = 0.5 * jnp.sum(jnp.mean(err * err, axis=-1, keepdims=True), axis=0, keepdims=True)
        dz, dg, db = _ln_bwd(err * (1.0 / d), xhat, rstd, gv)
        dz_ref[...] = dz
        dzb_ref[...] = dz.astype(BF16)
        _acc_rows(dg_ref, i == 0, dg)
        _acc_rows(db_ref, i == 0, db)
        _acc_rows(loss_ref, i == 0, jnp.broadcast_to(part, (8, 128)))

    row = _bs((tr, d), lambda i: (i, 0))
    vec = _bs((1, d), lambda i: (0, 0))
    return pl.pallas_call(
        body, name="ln_ffn_loss", grid=(seq // tr,), in_specs=[row, row, row, vec, vec],
        out_specs=[row, row, vec, vec, _bs((8, 128), lambda i: (0, 0))],
        out_shape=[jax.ShapeDtypeStruct((seq, d), F32), jax.ShapeDtypeStruct((seq, d), BF16),
                   jax.ShapeDtypeStruct((1, d), F32), jax.ShapeDtypeStruct((1, d), F32),
                   jax.ShapeDtypeStruct((8, 128), F32)],
        compiler_params=_params(1),
    )(ffn, x1, tgt, g, b)


def _ln_bwd_rows(dx_branch, dres, z, g, b, tr):
    seq, d = z.shape

    def body(a_ref, r_ref, z_ref, g_ref, b_ref, dz_ref, dzb_ref, dg_ref, db_ref):
        i = pl.program_id(0)
        gv = g_ref[...]
        _, xhat, rstd = _ln_fwd(z_ref[...], gv, b_ref[...])
        dz, dg, db = _ln_bwd(ALPHA * r_ref[...] + a_ref[...], xhat, rstd, gv)
        dz_ref[...] = dz
        dzb_ref[...] = dz.astype(BF16)
        _acc_rows(dg_ref, i == 0, dg)
        _acc_rows(db_ref, i == 0, db)

    row = _bs((tr, d), lambda i: (i, 0))
    vec = _bs((1, d), lambda i: (0, 0))
    return pl.pallas_call(
        body, name="ln_mix_bwd", grid=(seq // tr,), in_specs=[row, row, row, vec, vec],
        out_specs=[row, row, vec, vec],
        out_shape=[jax.ShapeDtypeStruct((seq, d), F32), jax.ShapeDtypeStruct((seq, d), BF16),
                   jax.ShapeDtypeStruct((1, d), F32), jax.ShapeDtypeStruct((1, d), F32)],
        compiler_params=_params(1),
    )(dx_branch, dres, z, g, b)


def _sum_blocks(name, parts):
    def body(p_ref, o_ref):
        acc = p_ref[0]
        for s in range(1, parts.shape[0]):
            acc = acc + p_ref[s]
        o_ref[...] = acc

    return pl.pallas_call(body, name=name, out_shape=jax.ShapeDtypeStruct(parts.shape[1:], F32))(parts)


def _adamw_values(w, g, m, v):
    m = ADAM_B1 * m + (1.0 - ADAM_B1) * g
    v = ADAM_B2 * v + (1.0 - ADAM_B2) * (g * g)
    m_hat = m / (1.0 - ADAM_B1 ** ADAM_STEP)
    v_hat = v / (1.0 - ADAM_B2 ** ADAM_STEP)
    delta = -ADAM_LR * (m_hat / (jnp.sqrt(v_hat) + ADAM_EPS) + ADAM_WD * w)
    return delta, m, v


def _sum_adamw(name, own, parts, w, m, v):
    rows, cols = w.shape
    n_parts = parts.shape[0]
    tr = rows
    min_rows = 8 if parts.dtype == F32 else 16
    while tr * cols * 4 > 1024 * 1024 and tr % (2 * min_rows) == 0:
        tr //= 2

    def body(*refs):
        if own is None:
            p_ref, w_ref, m_ref, v_ref, g_ref, d_ref, mo_ref, vo_ref = refs
            g = p_ref[0].astype(F32)
            rest = range(1, n_parts)
        else:
            o_ref, p_ref, w_ref, m_ref, v_ref, g_ref, d_ref, mo_ref, vo_ref = refs
            g = o_ref[...]
            rest = range(n_parts)
        for s in rest:
            g = g + p_ref[s].astype(F32)
        delta, mn, vn = _adamw_values(w_ref[...], g, m_ref[...], v_ref[...])
        g_ref[...] = g
        d_ref[...] = delta
        mo_ref[...] = mn
        vo_ref[...] = vn

    spec = _bs((tr, cols), lambda i: (i, 0))
    lead = [] if own is None else [own]
    return pl.pallas_call(
        body, name=name, grid=(rows // tr,),
        in_specs=[spec] * len(lead) + [_bs((n_parts, tr, cols), lambda i: (0, i, 0)), spec, spec, spec],
        out_specs=[spec] * 4, out_shape=[jax.ShapeDtypeStruct((rows, cols), F32)] * 4,
        compiler_params=_params(1),
    )(*lead, parts, w, m, v)


def _rows128(a):
    return a.reshape(-1, 128)


def kernel(x, ln_mix_g, ln_mix_b, w_in, w_pool, pool_scale, conv_w, conv_b, w_rg_a, b_rg_a, w_rg_i, b_rg_i, rg_lambda, w_out, ln_ffn_g, ln_ffn_b, w_mlp_in, w_mlp_out, loss_target, m_ln_mix_g, m_ln_mix_b, m_w_in, m_w_pool, m_pool_scale, m_conv_w, m_conv_b, m_w_rg_a, m_b_rg_a, m_w_rg_i, m_b_rg_i, m_rg_lambda, m_w_out, m_ln_ffn_g, m_ln_ffn_b, m_w_mlp_in, m_w_mlp_out, v_ln_mix_g, v_ln_mix_b, v_w_in, v_w_pool, v_pool_scale, v_conv_w, v_conv_b, v_w_rg_a, v_b_rg_a, v_w_rg_i, v_b_rg_i, v_rg_lambda, v_w_out, v_ln_ffn_g, v_ln_ffn_b, v_w_mlp_in, v_w_mlp_out):
    seq, d_model = x.shape[1], x.shape[2]
    dh = d_model // 2
    lh = dh // N_HEADS
    pg = dh // len(POOL_WINDOWS)
    d_ff = w_mlp_in.shape[2] * N_DEV
    assert lh == 128 and conv_w.shape[3] == lh and w_pool.shape[2] * N_DEV == pg

    xs = x[0]
    tgt = loss_target[0]

    def small_pack(cw, ba, bi, lam):
        return jnp.concatenate([cw.reshape(4, lh), ba.reshape(2, lh), bi.reshape(2, lh), lam.reshape(2, lh),
                                jnp.zeros((SMALL_ROWS - 10, lh), F32)], axis=0)

    pack_mine = small_pack(conv_w, b_rg_a, b_rg_i, rg_lambda)
    win_full, wpool_full, pack_full = _all_gather("gather_mixer", [
        (w_in[0].astype(BF16), 1), (w_pool[0].astype(BF16), 1), (pack_mine[None], 0)])
    wout_gather = _SplitGather("gather_w_out", [(w_out[0], 0)], BF16, after=pack_full)
    w1_gather = _SplitGather("gather_w_mlp_in", [(w_mlp_in[0], 1)], BF16, after=wout_gather.token)
    w2_gather = _SplitGather("gather_w_mlp_out", [(w_mlp_out[0], 0)], BF16, after=w1_gather.token)
    wcat = jnp.concatenate([w_rg_a[0, 0], w_rg_i[0, 0], w_rg_a[0, 1], w_rg_i[0, 1]], axis=-1).astype(BF16)
    vec = lambda i, j, k: (0, 0)
    row_full = lambda i, j, k: (i, 0)

    def after(token):
        return (token, _sp((8, 128), vec))

    def sds(shape, dtype):
        return jax.ShapeDtypeStruct(shape, dtype)

    def plain_epi(acc, i, ex, out):
        out[0][...] = acc

    def bf16_epi(acc, i, ex, out):
        out[0][...] = acc.astype(BF16)

    t = _tiles(seq, d_model, d_ff)

    (p3,) = _matmul(
        "proj", xs, win_full, _sp((t.rows, d_model), lambda i, j, k: (i, 0)), _sp((d_model, dh), lambda i, j, k: (0, j)),
        grid=(seq // t.rows, 3, 1), extras=[after(w2_gather.token)],
        out_shape=[sds((3, seq, dh), F32)], out_specs=[_sp((None, t.rows, dh), lambda i, j, k: (j, i, 0))],
        epilogue=plain_epi)

    d_pool, y_half = _pool_fwd(p3, wpool_full, pool_scale, seq, d_model)
    y, h0p, h1p = _lru_fwd(p3, y_half, pack_full, conv_b, wcat, wout_gather.relay(after=y_half), seq, d_model)
    (wout_full,) = wout_gather.wait(after=y)
    relay_token = w1_gather.relay(after=wout_full)

    def mix_epi(acc, i, ex, out):
        x_ref, g_ref, b_ref = ex[:3]
        z = ALPHA * x_ref[...] + acc
        x1, _, _ = _ln_fwd(z, g_ref[...], b_ref[...])
        out[0][...] = z
        out[1][...] = x1
        out[2][...] = x1.astype(BF16)

    z1, x1, x1b = _matmul(
        "mix_out", y, wout_full, _sp((t.ln_rows, d_model), row_full), _sp((d_model, d_model), vec, single=True),
        grid=(seq // t.ln_rows, 1, 1),
        extras=[(xs, _sp((t.ln_rows, d_model), row_full)), (ln_mix_g, _sp((1, d_model), vec)),
                (ln_mix_b, _sp((1, d_model), vec)), after(relay_token)],
        out_shape=[sds((seq, d_model), F32), sds((seq, d_model), F32), sds((seq, d_model), BF16)],
        out_specs=[_sp((t.ln_rows, d_model), row_full)] * 3, epilogue=mix_epi)
    (w1_full,) = w1_gather.wait(after=x1b)

    def mlp_in_epi(acc, i, ex, out, cols):
        h = jnp.maximum(acc, 0.0)
        out[0][:, cols] = (h * h).astype(BF16)
        out[1][:, cols] = (2.0 * h).astype(BF16)

    hmid, dact = _matmul(
        "mlp_in", x1b, w1_full, _sp((t.rows, d_model), lambda i, j, k: (i, 0)),
        _sp((d_model, t.ff_cols), lambda i, j, k: (0, j)),
        grid=(seq // t.rows, d_ff // t.ff_cols, 1), j_outer=True,
        out_shape=[sds((seq, d_ff), BF16)] * 2, out_specs=[_sp((t.rows, t.ff_cols), lambda i, j, k: (i, j))] * 2,
        epilogue=mlp_in_epi, n_split=t.ff_split)
    (w2_full,) = w2_gather.wait(after=w2_gather.relay(after=hmid))

    (ffn,) = _matmul(
        "mlp_out", hmid, w2_full, _sp((t.rows, t.ff_k), lambda i, j, k: (i, k)),
        _sp((t.ff_k, d_model), lambda i, j, k: (k, 0)),
        grid=(seq // t.rows, 1, d_ff // t.ff_k),
        out_shape=[sds((seq, d_model), F32)], out_specs=[_sp((t.rows, d_model), row_full)])
    dz2, dz2b, g_ffn_g, g_ffn_b, loss_part = _ln_loss_bwd(ffn, x1, tgt, ln_ffn_g, ln_ffn_b, t.ln_rows)

    (g_w2,) = _matmul(
        "grad_w_mlp_out", hmid, dz2b, _sp((seq, t.grad_rows), lambda i, j, k: (0, i)),
        _sp((seq, d_model), vec, single=True),
        grid=(d_ff // t.grad_rows, 1, 1), ta=True,
        out_shape=[sds((d_ff, d_model), BF16)], out_specs=[_sp((t.grad_rows, d_model), row_full)],
        epilogue=bf16_epi)
    scatter_w2 = _SplitReduceScatter("scatter_w_mlp_out", [g_w2.reshape(N_DEV, d_ff // N_DEV, d_model)])

    def dpre_epi(acc, i, ex, out, cols):
        out[0][:, cols] = (acc * ex[0][:, cols].astype(F32)).astype(BF16)

    (dpre,) = _matmul(
        "mlp_dpre", dz2b, w2_full, _sp((t.rows, d_model), lambda i, j, k: (i, 0)),
        _sp((t.ff_cols, d_model), lambda i, j, k: (j, 0)),
        grid=(seq // t.rows, d_ff // t.ff_cols, 1), j_outer=True, tb=True,
        extras=[(dact, _sp((t.rows, t.ff_cols), lambda i, j, k: (i, j))), after(scatter_w2.token)],
        out_shape=[sds((seq, d_ff), BF16)], out_specs=[_sp((t.rows, t.ff_cols), lambda i, j, k: (i, j))],
        epilogue=dpre_epi, n_split=t.ff_split)
    token_w2 = scatter_w2.combine_and_send(after=dpre)

    (dx1_mlp,) = _matmul(
        "mlp_dx", dpre, w1_full, _sp((t.rows, t.ff_k), lambda i, j, k: (i, k)),
        _sp((d_model, t.ff_k), lambda i, j, k: (0, k)),
        grid=(seq // t.rows, 1, d_ff // t.ff_k), tb=True, extras=[after(token_w2)],
        out_shape=[sds((seq, d_model), F32)], out_specs=[_sp((t.rows, d_model), row_full)])
    dz1, dz1b, g_mix_g, g_mix_b = _ln_bwd_rows(dx1_mlp, dz2, z1, ln_mix_g, ln_mix_b, t.ln_rows)

    def block_epi(acc, i, ex, out):
        out[0][0] = acc.astype(BF16)

    fs = d_ff // N_DEV
    (g_w1,) = _matmul(
        "grad_w_mlp_in", x1b, dpre, _sp((seq, t.grad_rows), lambda i, j, k: (0, i)),
        _sp((seq, fs), lambda i, j, k: (0, j)),
        grid=(d_model // t.grad_rows, N_DEV, 1), j_outer=True, ta=True,
        out_shape=[sds((N_DEV, d_model, fs), BF16)],
        out_specs=[_sp((1, t.grad_rows, fs), lambda i, j, k: (j, i, 0))], epilogue=block_epi)

    (dy,) = _matmul(
        "mix_dy", dz1b, wout_full, _sp((t.rows, d_model), lambda i, j, k: (i, 0)),
        _sp((dh, d_model), lambda i, j, k: (j, 0)),
        grid=(seq // t.rows, 2, 1), j_outer=True, tb=True,
        out_shape=[sds((seq, d_model), F32)], out_specs=[_sp((t.rows, dh), lambda i, j, k: (i, j))],
        epilogue=plain_epi)
    (g_wout,) = _matmul(
        "grad_w_out", y, dz1b, _sp((seq, t.grad_rows), lambda i, j, k: (0, i)), _sp((seq, d_model), vec, single=True),
        grid=(d_model // t.grad_rows, 1, 1), ta=True,
        out_shape=[sds((d_model, d_model), BF16)], out_specs=[_sp((t.grad_rows, d_model), row_full)],
        epilogue=bf16_epi)
    scatter_w1 = _SplitReduceScatter("scatter_w_mlp_in", [g_w1, g_wout.reshape(N_DEV, d_model // N_DEV, d_model)])

    dproj_pool, g_wpool, g_pscale = _pool_bwd(d_pool, dy, wpool_full, pool_scale, scatter_w1.token, seq, d_model)
    token_w1 = scatter_w1.combine_and_send(after=dproj_pool)
    dproj, g_pack, g_convb, g_wcat = _lru_bwd(p3, dy, h0p, h1p, dproj_pool, pack_full, conv_b, wcat,
                                              token_w1, seq, d_model)
    g_wa = jnp.stack([g_wcat[:, :, 0:lh], g_wcat[:, :, 2 * lh:3 * lh]])
    g_wi = jnp.stack([g_wcat[:, :, lh:2 * lh], g_wcat[:, :, 3 * lh:4 * lh]])

    rep_parts = [_rows128(g_wa), _rows128(g_wi), _rows128(g_mix_g), _rows128(g_mix_b), _rows128(g_ffn_g),
                 _rows128(g_ffn_b), _rows128(g_pscale), _rows128(g_convb)]
    rep_rows = [p.shape[0] for p in rep_parts]
    n_rep = sum(rep_rows)
    small = jnp.concatenate(rep_parts + [_rows128(g_pack), loss_part], axis=0)
    small_gather = _SplitGather("gather_small_grads", [(small[None], 0)], F32, after=small)

    ws = 3 * dh // N_DEV

    def pair_epi(acc, i, ex, out):
        out[0][0] = acc[:, :ws].astype(BF16)
        out[0][1] = acc[:, ws:].astype(BF16)

    (g_win,) = _matmul(
        "grad_w_in", xs, dproj, _sp((seq, t.grad_rows), lambda i, j, k: (0, i)),
        _sp((seq, 2 * ws), lambda i, j, k: (0, j)),
        grid=(d_model // t.grad_rows, N_DEV // 2, 1), ta=True, extras=[after(small_gather.token)],
        out_shape=[sds((N_DEV, d_model, ws), BF16)],
        out_specs=[_sp((2, t.grad_rows, ws), lambda i, j, k: (j, i, 0))], epilogue=pair_epi)
    scatter_mix = _SplitReduceScatter(
        "scatter_mixer", [g_win, g_wpool.reshape(N_DEV, pg // N_DEV * len(POOL_WINDOWS), pg)])

    def adam_big(name, own_landed, w, m, v):
        own, landed = own_landed
        shp = w.shape
        two = lambda a: a.reshape(-1, shp[-1])
        res = _sum_adamw(name, own, landed, two(w), two(m), two(v))
        return [r.reshape(shp) for r in res]

    (r_w2,) = scatter_w2.wait(after=scatter_mix.token)
    o_w2 = adam_big("adam_w_mlp_out", r_w2, w_mlp_out, m_w_mlp_out, v_w_mlp_out)
    token_mix = scatter_mix.combine_and_send(after=o_w2[0])

    def dx_epi(acc, i, ex, out):
        out[0][...] = ALPHA * ex[0][...] + acc

    (dx,) = _matmul(
        "grad_x", dproj, win_full, _sp((t.ln_rows * 2, 3 * dh), lambda i, j, k: (i, 0)),
        _sp((d_model, 3 * dh), vec, single=True),
        grid=(seq // (t.ln_rows * 2), 1, 1), tb=True,
        extras=[(dz1, _sp((t.ln_rows * 2, d_model), row_full)), after(token_mix)],
        out_shape=[sds((seq, d_model), F32)], out_specs=[_sp((t.ln_rows * 2, d_model), row_full)],
        epilogue=dx_epi)
    r_w1, r_wout = scatter_w1.wait(after=dx)
    o_w1 = adam_big("adam_w_mlp_in", r_w1, w_mlp_in, m_w_mlp_in, v_w_mlp_in)
    o_wout = adam_big("adam_w_out", r_wout, w_out, m_w_out, v_w_out)
    r_win, r_wpool = scatter_mix.wait(after=o_wout[0])
    o_win = adam_big("adam_w_in", r_win, w_in, m_w_in, v_w_in)
    o_wpool = adam_big("adam_w_pool", r_wpool, w_pool, m_w_pool, v_w_pool)

    small_gather.relay(after=o_win[0])
    (small_all,) = small_gather.wait(after=o_wpool[0])

    rep_w = [w_rg_a, w_rg_i, ln_mix_g, ln_mix_b, ln_ffn_g, ln_ffn_b, pool_scale, conv_b]
    rep_m = [m_w_rg_a, m_w_rg_i, m_ln_mix_g, m_ln_mix_b, m_ln_ffn_g, m_ln_ffn_b, m_pool_scale, m_conv_b]
    rep_v = [v_w_rg_a, v_w_rg_i, v_ln_mix_g, v_ln_mix_b, v_ln_ffn_g, v_ln_ffn_b, v_pool_scale, v_conv_b]
    cat = lambda arrs: jnp.concatenate([_rows128(a) for a in arrs], axis=0)
    o_rep = _sum_adamw("adam_replicated", None, small_all, cat(rep_w), cat(rep_m), cat(rep_v))

    my_idx = _dev_index(_where_am_i())
    head_parts = lax.dynamic_slice_in_dim(small_all, n_rep + my_idx * SMALL_ROWS, SMALL_ROWS, axis=1)
    o_head = _sum_adamw("adam_head", None, head_parts, pack_mine,
                        small_pack(m_conv_w, m_b_rg_a, m_b_rg_i, m_rg_lambda),
                        small_pack(v_conv_w, v_b_rg_a, v_b_rg_i, v_rg_lambda))

    def unpack_rep(packed):
        out, r = [], 0
        for wgt, rows in zip(rep_w, rep_rows):
            out.append(packed[r:r + rows].reshape(wgt.shape))
            r += rows
        return out

    def unpack_head(packed):
        return [packed[0:4].reshape(conv_w.shape), packed[4:6].reshape(b_rg_a.shape),
                packed[6:8].reshape(b_rg_i.shape), packed[8:10].reshape(rg_lambda.shape)]

    loss = _sum_blocks("loss_sum", small_all[:, n_rep + N_HEADS * SMALL_ROWS:, :])[0, 0]

    outs = [loss, dx[None]]
    for kind in range(4):
        ra, ri, mg, mb, fg, fb, ps, cb = unpack_rep(o_rep[kind])
        cw, ba, bi, lam = unpack_head(o_head[kind])
        outs += [mg, mb, o_win[kind], o_wpool[kind], ps, cw, cb, ra, ba, ri, bi, lam, o_wout[kind], fg, fb,
                 o_w1[kind], o_w2[kind]]
    return tuple(outs)
```

```python
import functools

import jax
import jax.numpy as jnp
from jax import lax
from jax.experimental import pallas as pl
from jax.experimental.pallas import tpu as pltpu

F32 = jnp.float32
BF16 = jnp.bfloat16
MESH = pl.DeviceIdType.MESH
ANY = pl.BlockSpec(memory_space=pl.ANY)

N_DEV = 8
POOL_WINDOWS = (2, 4, 8, 16)
N_HEADS = 8
RG_C = 8.0
LN_EPS = 1e-5
ALPHA = 2.0 ** 0.25
ADAM_LR = 0.001
ADAM_B1 = 0.9
ADAM_B2 = 0.999
ADAM_EPS = 1e-08
ADAM_WD = 0.01
ADAM_STEP = 10

VMEM_LIMIT = 56 * 1024 * 1024
WIN_HALO = 16
CONV_HALO = 8
SMALL_ROWS = 16


def _params(n_grid):
    return pltpu.CompilerParams(dimension_semantics=("arbitrary",) * n_grid, vmem_limit_bytes=VMEM_LIMIT)


def _shift(v, j):
    n = v.shape[0]
    s = (-j) % n
    return v if s == 0 else pltpu.roll(v, s, 0)


def _sigmoid(x):
    return 0.5 * jnp.tanh(0.5 * x) + 0.5


def _softplus(z):
    e = jnp.exp(-jnp.abs(z))
    u = 1.0 + e
    log1p = jnp.where(u == 1.0, e, jnp.log(u) * (e / jnp.where(u == 1.0, 1.0, u - 1.0)))
    return jnp.maximum(z, 0.0) + log1p


_GELU_C = 0.7978845608028654
_GELU_K = 0.044715


def _gelu_and_grad(x):
    x2 = x * x
    t = jnp.tanh(_GELU_C * (x + _GELU_K * x * x2))
    g = 0.5 * x * (1.0 + t)
    dg = 0.5 * (1.0 + t) + 0.5 * x * (1.0 - t * t) * (_GELU_C * (1.0 + 3.0 * _GELU_K * x2))
    return g, dg


def _ln_fwd(z, g, b):
    mu = jnp.mean(z, axis=-1, keepdims=True)
    zc = z - mu
    var = jnp.mean(zc * zc, axis=-1, keepdims=True)
    rstd = lax.rsqrt(var + LN_EPS)
    xhat = zc * rstd
    return xhat * g + b, xhat, rstd


def _ln_bwd(dy, xhat, rstd, g):
    dxhat = dy * g
    m1 = jnp.mean(dxhat, axis=-1, keepdims=True)
    m2 = jnp.mean(dxhat * xhat, axis=-1, keepdims=True)
    dz = rstd * (dxhat - m1 - xhat * m2)
    dg = jnp.sum(dy * xhat, axis=0, keepdims=True)
    db = jnp.sum(dy, axis=0, keepdims=True)
    return dz, dg, db


def _acc_rows(ref, first, val):
    @pl.when(first)
    def _():
        ref[...] = val

    @pl.when(jnp.logical_not(first))
    def _():
        ref[...] += val


def _sp(shape, fn, single=False):
    return shape, fn, single


def _matmul(name, a, b, a_spec, b_spec, *, grid, j_outer=False, ta=False, tb=False, extras=(), out_shape, out_specs,
            epilogue=None, n_split=1):
    ni, nj, nk = grid
    n_ex = len(extras)
    dims = (((0 if ta else 1,), (1 if tb else 0,)), ((), ()))

    def mk(spec):
        shape, fn, single = spec
        index = (lambda g0, g1, g2: fn(g1, g0, g2)) if j_outer else fn
        return pl.BlockSpec(shape, index, pipeline_mode=pl.Buffered(1)) if single else pl.BlockSpec(shape, index)

    def body(a_ref, b_ref, *rest):
        ex_refs = rest[:n_ex]
        out_refs = rest[n_ex:]
        i = pl.program_id(1 if j_outer else 0)
        if n_split > 1:
            av = a_ref[...].astype(BF16)
            width = b_ref.shape[0 if tb else 1] // n_split
            for c in range(n_split):
                cols = pl.ds(c * width, width)
                bv = (b_ref[cols, :] if tb else b_ref[:, cols]).astype(BF16)
                epilogue(lax.dot_general(av, bv, dims, preferred_element_type=F32), i, ex_refs, out_refs, cols)
            return
        part = lax.dot_general(a_ref[...].astype(BF16), b_ref[...].astype(BF16), dims, preferred_element_type=F32)
        if nk == 1:
            epilogue(part, i, ex_refs, out_refs)
        else:
            @pl.when(pl.program_id(2) == 0)
            def _():
                out_refs[0][...] = jnp.zeros(out_refs[0].shape, F32)

            out_refs[0][...] += part

    return pl.pallas_call(
        body, name=name, grid=(nj, ni, nk) if j_outer else (ni, nj, nk),
        in_specs=[mk(a_spec), mk(b_spec)] + [mk(s) for _, s in extras],
        out_specs=[mk(s) for s in out_specs], out_shape=list(out_shape),
        compiler_params=_params(3),
    )(a, b, *[x for x, _ in extras])


def _bs(shape, fn):
    return pl.BlockSpec(shape, fn)


def _where_am_i():
    x, y, c = lax.axis_index("x"), lax.axis_index("y"), lax.axis_index("c")
    return x, y, c


def _dev_index(p):
    return 4 * p[0] + 2 * p[1] + p[2]


def _slab(ref, axis, idx, size):
    sl = [slice(None)] * len(ref.shape)
    sl[axis] = pl.ds(idx * size, size)
    return ref.at[tuple(sl)]


def _all_gather(name, items):
    n = len(items)
    shapes = []
    for shard, axis in items:
        s = list(shard.shape)
        s[axis] *= N_DEV
        shapes.append(jax.ShapeDtypeStruct(tuple(s), shard.dtype))

    def body(*refs):
        in_refs, out_refs = refs[:n], refs[n:2 * n]
        send_sems, recv_sems, local_sems = refs[2 * n:]
        x, y, c = _where_am_i()
        me, sibling = (x, y, c), (x, y, 1 - c)
        chips = [(1 - x, y), (x, 1 - y), (1 - x, 1 - y)]

        def blk(a, p):
            axis = items[a][1]
            return _slab(out_refs[a], axis, _dev_index(p), items[a][0].shape[axis])

        def copy(a, k, block, to, src=None):
            return pltpu.make_async_remote_copy(
                src_ref=blk(a, block) if src is None else src, dst_ref=blk(a, block),
                send_sem=send_sems.at[a, k], recv_sem=recv_sems.at[a, k], device_id=to, device_id_type=MESH)

        mine = [pltpu.make_async_copy(in_refs[a], blk(a, me), local_sems.at[a]) for a in range(n)]
        for cp in mine:
            cp.start()
        first = []
        for a in range(n):
            first.append(copy(a, 0, me, sibling, src=in_refs[a]))
            first += [copy(a, 1 + j, me, (*chip, c), src=in_refs[a]) for j, chip in enumerate(chips)]
        for cp in first:
            cp.start()
        passed = []
        for a in range(n):
            for j, chip in enumerate(chips):
                copy(a, 1 + j, (*chip, c), me).wait_recv()
                fw = copy(a, 4 + j, (*chip, c), sibling)
                fw.start()
                passed.append(fw)
        for a in range(n):
            copy(a, 0, sibling, me).wait_recv()
            for j, chip in enumerate(chips):
                copy(a, 4 + j, (*chip, 1 - c), me).wait_recv()
        for cp in first + passed:
            cp.wait_send()
        for cp in mine:
            cp.wait()

    outs = pl.pallas_call(
        body, name=name, out_shape=shapes, in_specs=[ANY] * n, out_specs=[ANY] * n,
        scratch_shapes=[pltpu.SemaphoreType.DMA((n, 7)), pltpu.SemaphoreType.DMA((n, 7)),
                        pltpu.SemaphoreType.DMA((n,))],
    )(*[s for s, _ in items])
    return list(outs)


HBM = pl.BlockSpec(memory_space=pltpu.HBM)
SEM = pl.BlockSpec(memory_space=pltpu.SEMAPHORE)
DATAFLOW = pltpu.SideEffectType.DATAFLOW_SIDE_EFFECTING


def _in_hbm(a):
    return pltpu.with_memory_space_constraint(a, pltpu.HBM)


def _token_shape():
    return jax.ShapeDtypeStruct((8, 128), F32)


def _split_start(name, n_sems, bufs, issue):
    nb = len(bufs)

    def body(*refs):
        issue(refs[:nb], refs[nb], refs[nb + 1])
        refs[-1][...] = jnp.zeros((8, 128), F32)

    outs = pl.pallas_call(
        body, name=name,
        out_shape=(pltpu.SemaphoreType.DMA((n_sems,)), pltpu.SemaphoreType.DMA((n_sems,)),
                   *[pltpu.HBM(b.shape, b.dtype) for b in bufs], _token_shape()),
        in_specs=[HBM] * nb, out_specs=(SEM, SEM, *[HBM] * nb, pl.BlockSpec(memory_space=pltpu.VMEM)),
        input_output_aliases={i: 2 + i for i in range(nb)},
        compiler_params=pltpu.CompilerParams(has_side_effects=DATAFLOW),
    )(*[_in_hbm(b) for b in bufs])
    return outs[0], outs[1], list(outs[2:2 + nb]), outs[-1]


def _split_relay(name, n_sems, sems, bufs, after, relay):
    nb = len(bufs)

    def body(*refs):
        relay(refs[:nb], refs[nb], refs[nb + 1], refs[nb + 3], refs[nb + 4])
        refs[-1][...] = jnp.zeros((8, 128), F32)

    outs = pl.pallas_call(
        body, name=name,
        out_shape=(pltpu.SemaphoreType.DMA((n_sems,)), pltpu.SemaphoreType.DMA((n_sems,)),
                   *[pltpu.HBM(b.shape, b.dtype) for b in bufs], _token_shape()),
        in_specs=[HBM] * nb + [SEM, SEM, ANY],
        out_specs=(SEM, SEM, *[HBM] * nb, pl.BlockSpec(memory_space=pltpu.VMEM)),
        input_output_aliases={i: 2 + i for i in range(nb)},
        compiler_params=pltpu.CompilerParams(has_side_effects=DATAFLOW),
    )(*bufs, sems[0], sems[1], after)
    return outs[0], outs[1], list(outs[2:2 + nb]), outs[-1]


def _split_wait(name, sems, bufs, after, finish):
    nb = len(bufs)

    def body(*refs):
        finish(refs[:nb], refs[nb], refs[nb + 1])

    outs = pl.pallas_call(
        body, name=name, out_shape=[pltpu.HBM(b.shape, b.dtype) for b in bufs],
        in_specs=[HBM] * nb + [SEM, SEM, ANY], out_specs=[HBM] * nb,
        input_output_aliases={i: i for i in range(nb)},
        compiler_params=pltpu.CompilerParams(has_side_effects=DATAFLOW),
    )(*bufs, sems[0], sems[1], after)
    return list(outs)


def _place(name, items, dtype, after):
    ids = jnp.reshape(_dev_index(_where_am_i()), (1,)).astype(jnp.int32)
    outs = []
    for a, (shard, axis) in enumerate(items):
        rows, cols = shard.shape[-2], shard.shape[-1]
        tr = rows
        while tr * cols * shard.dtype.itemsize > 4 * 1024 * 1024 and tr % 32 == 0:
            tr //= 2
        nt = rows // tr
        full = list(shard.shape)
        full[axis] *= N_DEV
        if shard.ndim == 2 and axis == 0:
            in_spec = _bs((tr, cols), lambda i, ids: (i, 0))
            out_spec = _bs((tr, cols), lambda i, ids, nt=nt: (ids[0] * nt + i, 0))
        elif shard.ndim == 2 and axis == 1:
            in_spec = _bs((tr, cols), lambda i, ids: (i, 0))
            out_spec = _bs((tr, cols), lambda i, ids: (i, ids[0]))
        elif shard.ndim == 3 and axis == 1:
            tr, nt = rows, shard.shape[0]
            in_spec = _bs((None, rows, cols), lambda i, ids: (i, 0, 0))
            out_spec = _bs((None, rows, cols), lambda i, ids: (i, ids[0], 0))
        else:
            assert shard.ndim == 3 and axis == 0 and shard.shape[0] == 1
            in_spec = _bs((None, tr, cols), lambda i, ids: (0, i, 0))
            out_spec = _bs((None, tr, cols), lambda i, ids: (ids[0], i, 0))

        def body(ids_ref, in_ref, after_ref, out_ref):
            del ids_ref, after_ref
            out_ref[...] = in_ref[...].astype(out_ref.dtype)

        outs.append(pl.pallas_call(
            body, name=f"{name}{a}",
            grid_spec=pltpu.PrefetchScalarGridSpec(
                num_scalar_prefetch=1, grid=(nt,), in_specs=[in_spec, ANY], out_specs=out_spec),
            out_shape=jax.ShapeDtypeStruct(tuple(full), dtype), compiler_params=_params(1),
        )(ids, shard, after))
    return outs


class _SplitGather:
    def __init__(self, name, items, dtype, after):
        self.name, self.items, self.n = name, items, len(items)
        fulls = _place(name + "_place", items, dtype, after)
        n = self.n

        def issue(refs, send, recv):
            me, sibling, chips, c = self._geometry()
            for a in range(n):
                self._copy1(refs, send, recv, a, 0, me, sibling).start()
                for j, chip in enumerate(chips):
                    self._copy1(refs, send, recv, a, 1 + j, me, (*chip, c)).start()

        self.send, self.recv, self.bufs, self.token = _split_start(name + "_start", 4 * n, fulls, issue)

    @staticmethod
    def _geometry():
        x, y, c = _where_am_i()
        return (x, y, c), (x, y, 1 - c), [(1 - x, y), (x, 1 - y), (1 - x, 1 - y)], c

    def _blk(self, refs, a, p):
        shard, axis = self.items[a]
        return _slab(refs[a], axis, _dev_index(p), shard.shape[axis])

    def _copy1(self, refs, send, recv, a, k, owner, to):
        return pltpu.make_async_remote_copy(
            src_ref=self._blk(refs, a, owner), dst_ref=self._blk(refs, a, owner), send_sem=send.at[4 * a + k],
            recv_sem=recv.at[4 * a + k], device_id=to, device_id_type=MESH)

    def _copy2(self, refs, send, recv, a, j, owner, to):
        return pltpu.make_async_remote_copy(
            src_ref=self._blk(refs, a, owner), dst_ref=self._blk(refs, a, owner), send_sem=send.at[3 * a + j],
            recv_sem=recv.at[3 * a + j], device_id=to, device_id_type=MESH)

    def relay(self, after):
        n = self.n

        def relay(refs, send_in, recv_in, send_out, recv_out):
            me, sibling, chips, c = self._geometry()
            for a in range(n):
                for j, chip in enumerate(chips):
                    self._copy1(refs, send_in, recv_in, a, 1 + j, (*chip, c), me).wait_recv()
                    self._copy2(refs, send_out, recv_out, a, j, (*chip, c), sibling).start()
            for a in range(n):
                self._copy1(refs, send_in, recv_in, a, 0, sibling, me).wait_recv()
                for k in range(4):
                    self._copy1(refs, send_in, recv_in, a, k, me, sibling).wait_send()

        self.send, self.recv, self.bufs, self.token = _split_relay(
            self.name + "_relay", 3 * n, (self.send, self.recv), self.bufs, after, relay)
        return self.token

    def wait(self, after):
        n = self.n

        def finish(refs, send, recv):
            me, sibling, chips, c = self._geometry()
            for a in range(n):
                for j, chip in enumerate(chips):
                    self._copy2(refs, send, recv, a, j, (*chip, 1 - c), me).wait_recv()
                    self._copy2(refs, send, recv, a, j, (*chip, c), sibling).wait_send()

        return _split_wait(self.name + "_wait", (self.send, self.recv), self.bufs, after, finish)


class _SplitReduceScatter:
    def __init__(self, name, grads):
        self.name, self.n = name, len(grads)
        n = self.n
        g4 = [g.reshape(4, 2, *g.shape[1:]) for g in grads]
        land = [lax.empty((4, 1, *g.shape[1:]), g.dtype) for g in grads]

        def issue(refs, send, recv):
            for a in range(n):
                self._swap(refs, send, recv, a).start()

        self.send, self.recv, self.bufs, self.token = _split_start(name + "_d2d_start", n, g4 + land, issue)

    def _swap(self, refs, send, recv, a):
        x, y, c = _where_am_i()
        return pltpu.make_async_remote_copy(
            src_ref=refs[a].at[:, pl.ds(1 - c, 1)], dst_ref=refs[self.n + a], send_sem=send.at[a], recv_sem=recv.at[a],
            device_id=(x, y, 1 - c), device_id_type=MESH)

    def _hop(self, refs, send, recv, a, m):
        x, y, c = _where_am_i()
        px = (1 - x) if m & 2 else x
        py = (1 - y) if m & 1 else y
        return pltpu.make_async_remote_copy(
            src_ref=refs[a].at[2 * px + py], dst_ref=refs[self.n + a].at[m - 1], send_sem=send.at[3 * a + m - 1],
            recv_sem=recv.at[3 * a + m - 1], device_id=(px, py, c), device_id_type=MESH)

    def combine_and_send(self, after):
        n = self.n

        def finish(refs, send, recv):
            for a in range(n):
                self._swap(refs, send, recv, a).wait()

        bufs = _split_wait(self.name + "_d2d_wait", (self.send, self.recv), self.bufs, after, finish)
        x, y, c = _where_am_i()
        ids = jnp.stack([c, 2 * x + y]).astype(jnp.int32)
        self.own, sums = [], []
        for a in range(n):
            own, hb = _pair_sum(f"{self.name}_sum{a}", bufs[a], bufs[n + a], ids)
            self.own.append(own)
            sums.append(hb)
        land = [lax.empty((3, *h.shape[1:]), h.dtype) for h in sums]

        def issue(refs, send, recv):
            for a in range(n):
                for m in (1, 2, 3):
                    self._hop(refs, send, recv, a, m).start()

        self.send, self.recv, self.bufs, self.token = _split_start(self.name + "_ici_start", 3 * n, sums + land, issue)
        return self.token

    def wait(self, after):
        n = self.n

        def finish(refs, send, recv):
            for a in range(n):
                for m in (1, 2, 3):
                    self._hop(refs, send, recv, a, m).wait()

        bufs = _split_wait(self.name + "_ici_wait", (self.send, self.recv), self.bufs, after, finish)
        return list(zip(self.own, bufs[n:]))


def _pair_sum(name, g4, land, ids):
    rows, cols = g4.shape[2], g4.shape[3]
    tr = rows
    while tr * cols * 2 > 1024 * 1024 and tr % 32 == 0:
        tr //= 2

    def body(ids_ref, g_ref, l_ref, own_ref, sum_ref):
        h = g_ref[...].astype(F32) + l_ref[...].astype(F32)
        sum_ref[...] = h.astype(sum_ref.dtype)

        @pl.when(pl.program_id(1) == ids_ref[1])
        def _():
            own_ref[...] = h

    return pl.pallas_call(
        body, name=name,
        grid_spec=pltpu.PrefetchScalarGridSpec(
            num_scalar_prefetch=1, grid=(rows // tr, 4),
            in_specs=[_bs((None, None, tr, cols), lambda i, q, ids: (q, ids[0], i, 0)),
                      _bs((None, None, tr, cols), lambda i, q, ids: (q, 0, i, 0))],
            out_specs=[_bs((tr, cols), lambda i, q, ids: (i, 0)), _bs((None, tr, cols), lambda i, q, ids: (q, i, 0))]),
        out_shape=[jax.ShapeDtypeStruct((rows, cols), F32), jax.ShapeDtypeStruct((4, rows, cols), g4.dtype)],
        compiler_params=_params(2),
    )(ids, g4, land)


def _win_sum(ext, w, off):
    s = ext + _shift(ext, -1)
    if w >= 4:
        s = _shift(s, -1) + _shift(s, 1)
    if w >= 8:
        s = _shift(s, -2) + _shift(s, 2)
    if w >= 16:
        s = _shift(s, -4) + _shift(s, 4)
    return _shift(s, off) if off else s


def _inv_count(r0, t, w, seq):
    pos = r0 + lax.broadcasted_iota(jnp.int32, (t, 1), 0)
    cnt = jnp.minimum(pos + w // 2, seq) - jnp.maximum(pos - w // 2, 0)
    return 1.0 / cnt.astype(F32)


def _pool_fwd(p3, w_pool, pool_scale, seq, d_model):
    dp = d_model // 2
    pg = dp // len(POOL_WINDOWS)
    t = min(128, seq)
    n_chunks = seq // t
    h = WIN_HALO

    def body(u_ref, w_ref, sc_ref, d_ref, y_ref, pad_ref):
        g = pl.program_id(0)
        zeros = jnp.zeros((h, pg), F32)
        pad_ref[0:h, :] = zeros
        pad_ref[h + seq:h + seq + h, :] = zeros

        def fill(ci, _):
            r0 = pl.multiple_of(ci * t, t)
            pad_ref[pl.ds(h + r0, t), :] = u_ref[pl.ds(r0, t), :]
            return 0

        lax.fori_loop(0, n_chunks, fill, 0)
        wmat = w_ref[...]
        scale = sc_ref[...]
        for gi, w in enumerate(POOL_WINDOWS):
            @pl.when(g == gi)
            def _(w=w):
                def chunk(ci, _):
                    r0 = pl.multiple_of(ci * t, t)
                    ext = pad_ref[pl.ds(r0, t + 2 * h), :]
                    mean = _win_sum(ext, w, 0)[h:h + t, :] * _inv_count(r0, t, w, seq)
                    d = (mean - ext[h:h + t, :]).astype(BF16)
                    d_ref[pl.ds(r0, t), :] = d
                    q = jnp.dot(d, wmat, preferred_element_type=F32)
                    y_ref[pl.ds(r0, t), :] = (q * scale).astype(BF16)
                    return 0

                lax.fori_loop(0, n_chunks, chunk, 0)

    return pl.pallas_call(
        body, name="pool_fwd", grid=(len(POOL_WINDOWS),),
        in_specs=[_bs((None, seq, pg), lambda g: (0, 0, g)), _bs((None, pg, pg), lambda g: (g, 0, 0)),
                  _bs((1, pg), lambda g: (0, g))],
        out_specs=[_bs((seq, pg), lambda g: (0, g)), _bs((seq, pg), lambda g: (0, g))],
        out_shape=[jax.ShapeDtypeStruct((seq, dp), BF16), jax.ShapeDtypeStruct((seq, d_model), BF16)],
        scratch_shapes=[pltpu.VMEM((seq + 2 * h, pg), F32)],
        compiler_params=_params(1),
    )(p3, w_pool, pool_scale)


def _pool_bwd(d, dy, w_pool, pool_scale, token, seq, d_model):
    dp = d_model // 2
    pg = dp // len(POOL_WINDOWS)
    t = min(128, seq)
    n_chunks = seq // t
    h = WIN_HALO
    tn_dims = (((0,), (0,)), ((), ()))
    nt_dims = (((1,), (1,)), ((), ()))

    def body(d_ref, dy_ref, w_ref, sc_ref, tok_ref, du_ref, dwb_ref, dsc_ref, pad_ref, dd_ref, dw_ref):
        del tok_ref
        g = pl.program_id(0)
        zeros = jnp.zeros((h, pg), F32)
        pad_ref[0:h, :] = zeros
        pad_ref[h + seq:h + seq + h, :] = zeros
        wmat = w_ref[...]
        scale = sc_ref[...]
        for gi, w in enumerate(POOL_WINDOWS):
            @pl.when(g == gi)
            def _(w=w):
                dw_ref[...] = jnp.zeros((pg, pg), F32)

                def first(ci, dsc):
                    r0 = pl.multiple_of(ci * t, t)
                    dv = d_ref[pl.ds(r0, t), :]
                    dyv = dy_ref[pl.ds(r0, t), :]
                    q = jnp.dot(dv, wmat, preferred_element_type=F32)
                    dsc = dsc + jnp.sum(dyv * q, axis=0, keepdims=True)
                    dq = (dyv * scale).astype(BF16)
                    dw_ref[...] += lax.dot_general(dv, dq, tn_dims, preferred_element_type=F32)
                    dd = lax.dot_general(dq, wmat, nt_dims, preferred_element_type=F32)
                    dd_ref[pl.ds(r0, t), :] = dd
                    pad_ref[pl.ds(h + r0, t), :] = dd * _inv_count(r0, t, w, seq)
                    return dsc

                dsc_ref[...] = lax.fori_loop(0, n_chunks, first, jnp.zeros((1, pg), F32))
                dwb_ref[...] = dw_ref[...].reshape(N_DEV, pg // N_DEV, pg).astype(BF16)

                def second(ci, _):
                    r0 = pl.multiple_of(ci * t, t)
                    ext = pad_ref[pl.ds(r0, t + 2 * h), :]
                    back = _win_sum(ext, w, 1)[h:h + t, :]
                    du_ref[pl.ds(r0, t), :] = (back - dd_ref[pl.ds(r0, t), :]).astype(BF16)
                    return 0

                lax.fori_loop(0, n_chunks, second, 0)

    return pl.pallas_call(
        body, name="pool_bwd", grid=(len(POOL_WINDOWS),),
        in_specs=[_bs((seq, pg), lambda g: (0, g)), _bs((seq, pg), lambda g: (0, g)),
                  _bs((None, pg, pg), lambda g: (g, 0, 0)), _bs((1, pg), lambda g: (0, g)),
                  _bs((8, 128), lambda g: (0, 0))],
        out_specs=[_bs((seq, pg), lambda g: (0, g)), _bs((N_DEV, None, pg // N_DEV, pg), lambda g: (0, g, 0, 0)),
                   _bs((1, pg), lambda g: (0, g))],
        out_shape=[jax.ShapeDtypeStruct((seq, 3 * dp), BF16),
                   jax.ShapeDtypeStruct((N_DEV, len(POOL_WINDOWS), pg // N_DEV, pg), BF16),
                   jax.ShapeDtypeStruct((1, dp), F32)],
        scratch_shapes=[pltpu.VMEM((seq + 2 * h, pg), F32), pltpu.VMEM((seq, pg), F32), pltpu.VMEM((pg, pg), F32)],
        compiler_params=_params(1),
    )(d, dy, w_pool, pool_scale, token)


def _tile_scan(n_tiles, lanes, loads, stores):
    row = lax.broadcasted_iota(jnp.int32, (8, lanes), 0)
    group = 8

    def local_scan(n, k):
        aa, bb = loads[n](k)
        for sh in (1, 2, 4):
            if n == 0:
                ok = row >= sh
                ap = jnp.where(ok, pltpu.roll(aa, sh, 0), 1.0)
                bp = jnp.where(ok, pltpu.roll(bb, sh, 0), 0.0)
            else:
                ok = row < 8 - sh
                ap = jnp.where(ok, pltpu.roll(aa, 8 - sh, 0), 1.0)
                bp = jnp.where(ok, pltpu.roll(bb, 8 - sh, 0), 0.0)
            bb = aa * bp + bb
            aa = aa * ap
        return aa, bb

    def step(s, carry):
        carry = list(carry)
        for n in range(2):
            tiles = [s * group + u if n == 0 else n_tiles - 1 - (s * group + u) for u in range(group)]
            local = [local_scan(n, k) for k in tiles]
            for k, (aa, bb) in zip(tiles, local):
                hh = bb + aa * carry[n]
                stores[n](k, hh)
                carry[n] = jnp.broadcast_to(hh[7:8, :] if n == 0 else hh[0:1, :], (8, lanes))
        return tuple(carry)

    zeros = jnp.zeros((8, lanes), F32)
    lax.fori_loop(0, n_tiles // group, step, (zeros, zeros))


def _gate_preacts(xc, wcat_ref):
    xcb = xc.astype(BF16)
    return xcb, jnp.dot(xcb, wcat_ref[...], preferred_element_type=F32)


def _gates(pre, n, pk_ref, sp):
    lh = pre.shape[1] // 4
    r = _sigmoid(pre[:, (2 * n) * lh:(2 * n + 1) * lh] + pk_ref[pl.ds(4 + n, 1), :])
    i = _sigmoid(pre[:, (2 * n + 1) * lh:(2 * n + 2) * lh] + pk_ref[pl.ds(6 + n, 1), :])
    log_a = (-RG_C * r) * sp[n]
    a = jnp.exp(log_a)
    x = 2.0 * log_a
    one_minus_a2 = jnp.where(x > -0.01, -(x * (1.0 + x * (0.5 + x * (1.0 / 6.0)))), 1.0 - a * a)
    m = jnp.sqrt(one_minus_a2)
    return r, i, a, m


def _conv_chunk(upad_ref, pk_ref, cb, r0, t):
    ext = upad_ref[pl.ds(r0, t + 2 * CONV_HALO), :]
    acc = pk_ref[pl.ds(1, 1), :] * ext
    for k in (0, 2, 3):
        acc = acc + pk_ref[pl.ds(k, 1), :] * _shift(ext, k - 1)
    return acc[CONV_HALO:CONV_HALO + t, :] + cb, ext


def _lru_fwd(p3, y_in, pack, conv_b, wcat, token, seq, d_model):
    dl = d_model // 2
    lh = dl // N_HEADS
    t = min(128, seq)
    n_chunks = seq // t
    seg = seq // 8
    hal = CONV_HALO
    first_rec_block = (d_model - dl) // lh

    def body(ur_ref, ug_ref, pk_ref, cb_ref, wcat_ref, yin_ref, tok_ref, y_ref, h0_ref, h1_ref,
             upad, a_scr, b_scr):
        del yin_ref, tok_ref
        zeros = jnp.zeros((hal, lh), F32)
        upad[0:hal, :] = zeros
        upad[hal + seq:hal + seq + hal, :] = zeros
        for ref in (h0_ref, h1_ref):
            ref[0:hal, :] = zeros
            ref[hal + seq:hal + seq + hal, :] = zeros

        def fill(ci, _):
            r0 = pl.multiple_of(ci * t, t)
            upad[pl.ds(hal + r0, t), :] = ur_ref[pl.ds(r0, t), :]
            return 0

        lax.fori_loop(0, n_chunks, fill, 0)
        cb = cb_ref[...]
        sp = [_softplus(-pk_ref[pl.ds(8 + n, 1), :]) for n in range(2)]

        def chunk(ci, _):
            r0 = pl.multiple_of(ci * t, t)
            xc, _ext = _conv_chunk(upad, pk_ref, cb, r0, t)
            _, pre = _gate_preacts(xc, wcat_ref)
            for n in range(2):
                _, i, a, m = _gates(pre, n, pk_ref, sp)
                a_scr[n, pl.ds(r0, t), :] = a
                b_scr[n, pl.ds(r0, t), :] = (m * i) * xc
            return 0

        lax.fori_loop(0, n_chunks, chunk, 0, unroll=2)

        def load(n):
            def get(k):
                at = pl.ds(pl.multiple_of(k * 8, 8), 8)
                return a_scr[n, at, :], b_scr[n, at, :]
            return get

        def store(ref):
            def put(k, v):
                ref[pl.ds(pl.multiple_of(hal + k * 8, 8), 8), :] = v
            return put

        _tile_scan(seq // 8, lh, [load(0), load(1)], [store(h0_ref), store(h1_ref)])

        def out(ci, _):
            r0 = pl.multiple_of(ci * t, t)
            hsum = h0_ref[pl.ds(hal + r0, t), :] + h1_ref[pl.ds(hal + r0, t), :]
            gl, _dg = _gelu_and_grad(ug_ref[pl.ds(r0, t), :])
            y_ref[pl.ds(r0, t), :] = (hsum * gl).astype(BF16)
            return 0

        lax.fori_loop(0, n_chunks, out, 0)

    return pl.pallas_call(
        body, name="lru_fwd", grid=(N_HEADS,),
        in_specs=[_bs((None, seq, lh), lambda h: (1, 0, h)), _bs((None, seq, lh), lambda h: (2, 0, h)),
                  _bs((None, SMALL_ROWS, lh), lambda h: (h, 0, 0)), _bs((1, lh), lambda h: (0, h)),
                  _bs((None, lh, 4 * lh), lambda h: (h, 0, 0)),
                  ANY, _bs((8, 128), lambda h: (0, 0))],
        out_specs=[_bs((seq, lh), lambda h: (0, first_rec_block + h)),
                   _bs((seq + 2 * hal, lh), lambda h: (0, h)), _bs((seq + 2 * hal, lh), lambda h: (0, h))],
        out_shape=[jax.ShapeDtypeStruct((seq, d_model), BF16), jax.ShapeDtypeStruct((seq + 2 * hal, dl), F32),
                   jax.ShapeDtypeStruct((seq + 2 * hal, dl), F32)],
        scratch_shapes=[pltpu.VMEM((seq + 2 * hal, lh), F32), pltpu.VMEM((2, seq, lh), F32),
                        pltpu.VMEM((2, seq, lh), F32)],
        input_output_aliases={5: 0},
        compiler_params=_params(1),
    )(p3, p3, pack, conv_b, wcat, y_in, token)


def _lru_bwd(p3, dy, h0p, h1p, dproj_in, pack, conv_b, wcat, token, seq, d_model):
    dl = d_model // 2
    lh = dl // N_HEADS
    t = min(128, seq)
    n_chunks = seq // t
    seg = seq // 8
    hal = CONV_HALO
    first_rec_block = (d_model - dl) // lh
    tn_dims = (((0,), (0,)), ((), ()))
    nt_dims = (((1,), (1,)), ((), ()))

    def body(ur_ref, ug_ref, dy_ref, h0_ref, h1_ref, pk_ref, cb_ref, wcat_ref, tok_ref, din_ref,
             dproj_ref, dpk_ref, dcb_ref, dwcat_ref,
             upad, a_scr, dh_scr, g_scr, dxc_pad, dpr_ref, out_sems, gate_scr):
        del din_ref, tok_ref
        zeros = jnp.zeros((hal, lh), F32)
        for ref in (upad, dxc_pad):
            ref[0:hal, :] = zeros
            ref[hal + seq:hal + seq + hal, :] = zeros
        for n in range(2):
            a_scr[n, 0:hal, :] = zeros
            a_scr[n, hal + seq:hal + seq + hal, :] = zeros

        def fill(ci, _):
            r0 = pl.multiple_of(ci * t, t)
            upad[pl.ds(hal + r0, t), :] = ur_ref[pl.ds(r0, t), :]
            return 0

        lax.fori_loop(0, n_chunks, fill, 0)
        cb = cb_ref[...]
        lam = [pk_ref[pl.ds(8 + n, 1), :] for n in range(2)]
        sp = [_softplus(-lam[n]) for n in range(2)]

        def chunk1(ci, _):
            r0 = pl.multiple_of(ci * t, t)
            xc, _ext = _conv_chunk(upad, pk_ref, cb, r0, t)
            _, pre = _gate_preacts(xc, wcat_ref)
            for n in range(2):
                r, i, a, m = _gates(pre, n, pk_ref, sp)
                a_scr[n, pl.ds(hal + r0, t), :] = a
                for q, v in enumerate((r, i, m)):
                    gate_scr[3 * n + q, pl.ds(r0, t), :] = v
            hsum = h0_ref[pl.ds(hal + r0, t), :] + h1_ref[pl.ds(hal + r0, t), :]
            gl, dgl = _gelu_and_grad(ug_ref[pl.ds(r0, t), :])
            dyv = dy_ref[pl.ds(r0, t), :]
            dh_scr[pl.ds(r0, t), :] = dyv * gl
            dpr_ref[1, pl.ds(r0, t), :] = ((dyv * hsum) * dgl).astype(BF16)
            return 0

        lax.fori_loop(0, n_chunks, chunk1, 0, unroll=2)

        def load(n):
            def get(k):
                r0 = pl.multiple_of(k * 8, 8)
                if n == 0:
                    coef = _shift(a_scr[0, pl.ds(pl.multiple_of(hal + r0, 8), 16), :], 1)[0:8, :]
                else:
                    coef = _shift(a_scr[1, pl.ds(pl.multiple_of(hal + r0 - 8, 8), 16), :], -1)[8:16, :]
                return coef, dh_scr[pl.ds(r0, 8), :]
            return get

        def store(n):
            def put(k, v):
                g_scr[n, pl.ds(pl.multiple_of(k * 8, 8), 8), :] = v
            return put

        _tile_scan(seq // 8, lh, [load(1), load(0)], [store(1), store(0)])

        dwcat_ref[...] = jnp.zeros((lh, 4 * lh), F32)

        def chunk3(ci, carry):
            dba, dbi, dlam, dcb = carry
            r0 = pl.multiple_of(ci * t, t)
            xc, _ext = _conv_chunk(upad, pk_ref, cb, r0, t)
            xcb = xc.astype(BF16)
            dxc = jnp.zeros((t, lh), F32)
            dba, dbi, dlam = list(dba), list(dbi), list(dlam)
            dpre = []
            for n in range(2):
                r, i, m = (gate_scr[3 * n + q, pl.ds(r0, t), :] for q in range(3))
                a = a_scr[n, pl.ds(hal + r0, t), :]
                hext = (h0_ref if n == 0 else h1_ref)[pl.ds(r0, t + 2 * hal), :]
                hprev = _shift(hext, -1 if n == 0 else 1)[hal:hal + t, :]
                gb = g_scr[n, pl.ds(r0, t), :]
                da = gb * hprev
                dm = gb * i * xc
                di = gb * m * xc
                dxc = dxc + gb * (m * i)
                dlog_a = da * a - dm * (a * a) / m
                dr = dlog_a * (-RG_C * sp[n])
                dlam[n] = dlam[n] + jnp.sum(dlog_a * r, axis=0, keepdims=True)
                dpr = dr * r * (1.0 - r)
                dpi = di * i * (1.0 - i)
                dba[n] = dba[n] + jnp.sum(dpr, axis=0, keepdims=True)
                dbi[n] = dbi[n] + jnp.sum(dpi, axis=0, keepdims=True)
                dpre += [dpr.astype(BF16), dpi.astype(BF16)]
            dpre = jnp.concatenate(dpre, axis=1)
            dwcat_ref[...] += lax.dot_general(xcb, dpre, tn_dims, preferred_element_type=F32)
            dxc = dxc + lax.dot_general(dpre, wcat_ref[...], nt_dims, preferred_element_type=F32)
            dxc_pad[pl.ds(hal + r0, t), :] = dxc
            dcb = dcb + jnp.sum(dxc, axis=0, keepdims=True)
            return tuple(dba), tuple(dbi), tuple(dlam), dcb

        zr = jnp.zeros((1, lh), F32)
        def chunk3_pair(cj, carry):
            return chunk3(2 * cj + 1, chunk3(2 * cj, carry))

        dba, dbi, dlam, dcb = lax.fori_loop(0, n_chunks // 2, chunk3_pair, ((zr, zr), (zr, zr), (zr, zr), zr))
        dcb_ref[...] = dcb
        for n in range(2):
            dpk_ref[pl.ds(4 + n, 1), :] = dba[n]
            dpk_ref[pl.ds(6 + n, 1), :] = dbi[n]
            dpk_ref[pl.ds(8 + n, 1), :] = dlam[n] * (RG_C * jax.nn.sigmoid(-lam[n]))
        dpk_ref[pl.ds(10, SMALL_ROWS - 10), :] = jnp.zeros((SMALL_ROWS - 10, lh), F32)

        def chunk4(ci, dtap):
            r0 = pl.multiple_of(ci * t, t)
            gext = dxc_pad[pl.ds(r0, t + 2 * hal), :]
            uext = upad[pl.ds(r0, t + 2 * hal), :]
            gmid = gext[hal:hal + t, :]
            du = pk_ref[pl.ds(1, 1), :] * gext
            for k in (0, 2, 3):
                du = du + pk_ref[pl.ds(k, 1), :] * _shift(gext, 1 - k)
            dpr_ref[0, pl.ds(r0, t), :] = du[hal:hal + t, :].astype(BF16)
            out = []
            for k in range(4):
                usl = _shift(uext, k - 1)[hal:hal + t, :]
                out.append(dtap[k] + jnp.sum(gmid * usl, axis=0, keepdims=True))
            return tuple(out)

        dtap = lax.fori_loop(0, n_chunks, chunk4, (zr, zr, zr, zr))
        for k in range(4):
            dpk_ref[pl.ds(k, 1), :] = dtap[k]

        head = pl.program_id(0)
        outs = [pltpu.make_async_copy(
            dpr_ref.at[b], dproj_ref.at[:, pl.ds(pl.multiple_of((1 + b) * dl + head * lh, lh), lh)], out_sems.at[b])
            for b in range(2)]
        for cp in outs:
            cp.start()
        for cp in outs:
            cp.wait()

    return pl.pallas_call(
        body, name="lru_bwd", grid=(N_HEADS,),
        in_specs=[_bs((None, seq, lh), lambda h: (1, 0, h)), _bs((None, seq, lh), lambda h: (2, 0, h)),
                  _bs((seq, lh), lambda h: (0, first_rec_block + h)),
                  _bs((seq + 2 * hal, lh), lambda h: (0, h)), _bs((seq + 2 * hal, lh), lambda h: (0, h)),
                  _bs((None, SMALL_ROWS, lh), lambda h: (h, 0, 0)), _bs((1, lh), lambda h: (0, h)),
                  _bs((None, lh, 4 * lh), lambda h: (h, 0, 0)),
                  _bs((8, 128), lambda h: (0, 0)), ANY],
        out_specs=[ANY, _bs((None, SMALL_ROWS, lh), lambda h: (h, 0, 0)),
                   _bs((1, lh), lambda h: (0, h)), _bs((None, lh, 4 * lh), lambda h: (h, 0, 0))],
        out_shape=[jax.ShapeDtypeStruct((seq, 3 * dl), BF16), jax.ShapeDtypeStruct((N_HEADS, SMALL_ROWS, lh), F32),
                   jax.ShapeDtypeStruct((1, dl), F32), jax.ShapeDtypeStruct((N_HEADS, lh, 4 * lh), F32)],
        scratch_shapes=[pltpu.VMEM((seq + 2 * hal, lh), F32), pltpu.VMEM((2, seq + 2 * hal, lh), F32),
                        pltpu.VMEM((seq, lh), F32), pltpu.VMEM((2, seq, lh), F32),
                        pltpu.VMEM((seq + 2 * hal, lh), F32), pltpu.VMEM((2, seq, lh), BF16),
                        pltpu.SemaphoreType.DMA((2,)), pltpu.VMEM((6, seq, lh), F32)],
        input_output_aliases={9: 0},
        compiler_params=_params(1),
    )(p3, p3, dy, h0p, h1p, pack, conv_b, wcat, token, dproj_in)


class _tiles:
    def __init__(self, seq, d_model, d_ff):
        self.rows = min(1024, seq)
        self.ln_rows = min(256, seq)
        self.ff_cols = min(1024, d_ff)
        self.ff_split = 4
        self.ff_k = min(2048, d_ff)
        self.grad_rows = 512


def _ln_loss_bwd(ffn, x1, tgt, g, b, tr):
    seq, d = ffn.shape

    def body(f_ref, x_ref, t_ref, g_ref, b_ref, dz_ref, dzb_ref, dg_ref, db_ref, loss_ref):
        i = pl.program_id(0)
        gv = g_ref[...]
        z = ALPHA * x_ref[...] + f_ref[...]
        y, xhat, rstd = _ln_fwd(z, gv, b_ref[...])
        err = y - t_ref[...]
        part = 0.5 * jnp.sum(jnp.mean(err * err, axis=-1, keepdims=True), axis=0, keepdims=True)
        dz, dg, db = _ln_bwd(err * (1.0 / d), xhat, rstd, gv)
        dz_ref[...] = dz
        dzb_ref[...] = dz.astype(BF16)
        _acc_rows(dg_ref, i == 0, dg)
        _acc_rows(db_ref, i == 0, db)
        _acc_rows(loss_ref, i == 0, jnp.broadcast_to(part, (8, 128)))

    row = _bs((tr, d), lambda i: (i, 0))
    vec = _bs((1, d), lambda i: (0, 0))
    return pl.pallas_call(
        body, name="ln_ffn_loss", grid=(seq // tr,), in_specs=[row, row, row, vec, vec],
        out_specs=[row, row, vec, vec, _bs((8, 128), lambda i: (0, 0))],
        out_shape=[jax.ShapeDtypeStruct((seq, d), F32), jax.ShapeDtypeStruct((seq, d), BF16),
                   jax.ShapeDtypeStruct((1, d), F32), jax.ShapeDtypeStruct((1, d), F32),
                   jax.ShapeDtypeStruct((8, 128), F32)],
        compiler_params=_params(1),
    )(ffn, x1, tgt, g, b)


def _ln_bwd_rows(dx_branch, dres, z, g, b, tr):
    seq, d = z.shape

    def body(a_ref, r_ref, z_ref, g_ref, b_ref, dz_ref, dzb_ref, dg_ref, db_ref):
        i = pl.program_id(0)
        gv = g_ref[...]
        _, xhat, rstd = _ln_fwd(z_ref[...], gv, b_ref[...])
        dz, dg, db = _ln_bwd(ALPHA * r_ref[...] + a_ref[...], xhat, rstd, gv)
        dz_ref[...] = dz
        dzb_ref[...] = dz.astype(BF16)
        _acc_rows(dg_ref, i == 0, dg)
        _acc_rows(db_ref, i == 0, db)

    row = _bs((tr, d), lambda i: (i, 0))
    vec = _bs((1, d), lambda i: (0, 0))
    return pl.pallas_call(
        body, name="ln_mix_bwd", grid=(seq // tr,), in_specs=[row, row, row, vec, vec],
        out_specs=[row, row, vec, vec],
        out_shape=[jax.ShapeDtypeStruct((seq, d), F32), jax.ShapeDtypeStruct((seq, d), BF16),
                   jax.ShapeDtypeStruct((1, d), F32), jax.ShapeDtypeStruct((1, d), F32)],
        compiler_params=_params(1),
    )(dx_branch, dres, z, g, b)


def _to_bf16(name, a, token):
    rows, cols = a.shape
    tr = min(512, rows)

    def body(a_ref, tok_ref, o_ref):
        del tok_ref
        o_ref[...] = a_ref[...].astype(BF16)

    return pl.pallas_call(
        body, name=name, grid=(rows // tr,),
        in_specs=[_bs((tr, cols), lambda i: (i, 0)), _bs((8, 128), lambda i: (0, 0))],
        out_specs=_bs((tr, cols), lambda i: (i, 0)), out_shape=jax.ShapeDtypeStruct((rows, cols), BF16),
        compiler_params=_params(1),
    )(a, token)


def _sum_blocks(name, parts):
    def body(p_ref, o_ref):
        acc = p_ref[0]
        for s in range(1, parts.shape[0]):
            acc = acc + p_ref[s]
        o_ref[...] = acc

    return pl.pallas_call(body, name=name, out_shape=jax.ShapeDtypeStruct(parts.shape[1:], F32))(parts)


def _adamw_values(w, g, m, v):
    m = ADAM_B1 * m + (1.0 - ADAM_B1) * g
    v = ADAM_B2 * v + (1.0 - ADAM_B2) * (g * g)
    m_hat = m / (1.0 - ADAM_B1 ** ADAM_STEP)
    v_hat = v / (1.0 - ADAM_B2 ** ADAM_STEP)
    delta = -ADAM_LR * (m_hat / (jnp.sqrt(v_hat) + ADAM_EPS) + ADAM_WD * w)
    return delta, m, v


def _sum_adamw(name, own, parts, w, m, v):
    rows, cols = w.shape
    n_parts = parts.shape[0]
    tr = rows
    min_rows = 8 if parts.dtype == F32 else 16
    while tr * cols * 4 > 1024 * 1024 and tr % (2 * min_rows) == 0:
        tr //= 2

    def body(*refs):
        if own is None:
            p_ref, w_ref, m_ref, v_ref, g_ref, d_ref, mo_ref, vo_ref = refs
            g = p_ref[0].astype(F32)
            rest = range(1, n_parts)
        else:
            o_ref, p_ref, w_ref, m_ref, v_ref, g_ref, d_ref, mo_ref, vo_ref = refs
            g = o_ref[...]
            rest = range(n_parts)
        for s in rest:
            g = g + p_ref[s].astype(F32)
        delta, mn, vn = _adamw_values(w_ref[...], g, m_ref[...], v_ref[...])
        g_ref[...] = g
        d_ref[...] = delta
        mo_ref[...] = mn
        vo_ref[...] = vn

    spec = _bs((tr, cols), lambda i: (i, 0))
    lead = [] if own is None else [own]
    return pl.pallas_call(
        body, name=name, grid=(rows // tr,),
        in_specs=[spec] * len(lead) + [_bs((n_parts, tr, cols), lambda i: (0, i, 0)), spec, spec, spec],
        out_specs=[spec] * 4, out_shape=[jax.ShapeDtypeStruct((rows, cols), F32)] * 4,
        compiler_params=_params(1),
    )(*lead, parts, w, m, v)


def _rows128(a):
    return a.reshape(-1, 128)


def kernel(x, ln_mix_g, ln_mix_b, w_in, w_pool, pool_scale, conv_w, conv_b, w_rg_a, b_rg_a, w_rg_i, b_rg_i, rg_lambda, w_out, ln_ffn_g, ln_ffn_b, w_mlp_in, w_mlp_out, loss_target, m_ln_mix_g, m_ln_mix_b, m_w_in, m_w_pool, m_pool_scale, m_conv_w, m_conv_b, m_w_rg_a, m_b_rg_a, m_w_rg_i, m_b_rg_i, m_rg_lambda, m_w_out, m_ln_ffn_g, m_ln_ffn_b, m_w_mlp_in, m_w_mlp_out, v_ln_mix_g, v_ln_mix_b, v_w_in, v_w_pool, v_pool_scale, v_conv_w, v_conv_b, v_w_rg_a, v_b_rg_a, v_w_rg_i, v_b_rg_i, v_rg_lambda, v_w_out, v_ln_ffn_g, v_ln_ffn_b, v_w_mlp_in, v_w_mlp_out):
    seq, d_model = x.shape[1], x.shape[2]
    dh = d_model // 2
    lh = dh // N_HEADS
    pg = dh // len(POOL_WINDOWS)
    d_ff = w_mlp_in.shape[2] * N_DEV
    assert lh == 128 and conv_w.shape[3] == lh and w_pool.shape[2] * N_DEV == pg

    xs = x[0]
    tgt = loss_target[0]

    def small_pack(cw, ba, bi, lam):
        return jnp.concatenate([cw.reshape(4, lh), ba.reshape(2, lh), bi.reshape(2, lh), lam.reshape(2, lh),
                                jnp.zeros((SMALL_ROWS - 10, lh), F32)], axis=0)

    pack_mine = small_pack(conv_w, b_rg_a, b_rg_i, rg_lambda)
    pack_bits = lax.bitcast_convert_type(pack_mine, BF16).reshape(1, SMALL_ROWS, 2 * lh)
    win_gather = _SplitGather("gather_w_in", [(w_in[0], 1), (w_pool[0], 1), (pack_bits, 0)], BF16, after=pack_mine)
    wout_gather = _SplitGather("gather_w_out", [(w_out[0], 0)], BF16, after=win_gather.token)
    w1_gather = _SplitGather("gather_w_mlp_in", [(w_mlp_in[0], 1)], BF16, after=wout_gather.token)
    w2_gather = _SplitGather("gather_w_mlp_out", [(w_mlp_out[0], 0)], BF16, after=w1_gather.token)
    xb = _to_bf16("x_bf16", x[0], w2_gather.token)
    win_full, wpool_full, pack_bits_full = win_gather.wait(after=win_gather.relay(after=xb))
    pack_full = lax.bitcast_convert_type(pack_bits_full.reshape(N_DEV, SMALL_ROWS, lh, 2), F32)
    wcat = jnp.concatenate([w_rg_a[0, 0], w_rg_i[0, 0], w_rg_a[0, 1], w_rg_i[0, 1]], axis=-1).astype(BF16)
    vec = lambda i, j, k: (0, 0)
    row_full = lambda i, j, k: (i, 0)

    def after(token):
        return (token, _sp((8, 128), vec))

    def sds(shape, dtype):
        return jax.ShapeDtypeStruct(shape, dtype)

    def plain_epi(acc, i, ex, out):
        out[0][...] = acc

    def bf16_epi(acc, i, ex, out):
        out[0][...] = acc.astype(BF16)

    t = _tiles(seq, d_model, d_ff)

    (p3,) = _matmul(
        "proj", xb, win_full, _sp((t.rows, d_model), lambda i, j, k: (i, 0)), _sp((d_model, dh), lambda i, j, k: (0, j)),
        grid=(seq // t.rows, 3, 1),
        out_shape=[sds((3, seq, dh), F32)], out_specs=[_sp((None, t.rows, dh), lambda i, j, k: (j, i, 0))],
        epilogue=plain_epi)

    d_pool, y_half = _pool_fwd(p3, wpool_full, pool_scale, seq, d_model)
    y, h0p, h1p = _lru_fwd(p3, y_half, pack_full, conv_b, wcat, wout_gather.relay(after=y_half), seq, d_model)
    (wout_full,) = wout_gather.wait(after=y)
    relay_token = w1_gather.relay(after=wout_full)

    mix_rows = 2 * t.ln_rows

    def mix_epi(acc, i, ex, out):
        x_ref, g_ref, b_ref = ex[:3]
        for part in range(2):
            rows = pl.ds(part * t.ln_rows, t.ln_rows)
            z = ALPHA * x_ref[rows, :] + acc[part * t.ln_rows:(part + 1) * t.ln_rows, :]
            x1, _, _ = _ln_fwd(z, g_ref[...], b_ref[...])
            out[0][rows, :] = z
            out[1][rows, :] = x1
            out[2][rows, :] = x1.astype(BF16)

    z1, x1, x1b = _matmul(
        "mix_out", y, wout_full, _sp((mix_rows, d_model), row_full), _sp((d_model, d_model), vec, single=True),
        grid=(seq // mix_rows, 1, 1),
        extras=[(xs, _sp((mix_rows, d_model), row_full)), (ln_mix_g, _sp((1, d_model), vec)),
                (ln_mix_b, _sp((1, d_model), vec)), after(relay_token)],
        out_shape=[sds((seq, d_model), F32), sds((seq, d_model), F32), sds((seq, d_model), BF16)],
        out_specs=[_sp((mix_rows, d_model), row_full)] * 3, epilogue=mix_epi)
    (w1_full,) = w1_gather.wait(after=x1b)

    def mlp_in_epi(acc, i, ex, out, cols):
        h = jnp.maximum(acc, 0.0)
        out[0][:, cols] = (h * h).astype(BF16)
        out[1][:, cols] = (2.0 * h).astype(BF16)

    hmid, dact = _matmul(
        "mlp_in", x1b, w1_full, _sp((t.rows, d_model), lambda i, j, k: (i, 0)),
        _sp((d_model, t.ff_cols), lambda i, j, k: (0, j)),
        grid=(seq // t.rows, d_ff // t.ff_cols, 1), j_outer=True,
        out_shape=[sds((seq, d_ff), BF16)] * 2, out_specs=[_sp((t.rows, t.ff_cols), lambda i, j, k: (i, j))] * 2,
        epilogue=mlp_in_epi, n_split=t.ff_split)
    (w2_full,) = w2_gather.wait(after=w2_gather.relay(after=hmid))

    (ffn,) = _matmul(
        "mlp_out", hmid, w2_full, _sp((t.rows, t.ff_k), lambda i, j, k: (i, k)),
        _sp((t.ff_k, d_model), lambda i, j, k: (k, 0)),
        grid=(seq // t.rows, 1, d_ff // t.ff_k),
        out_shape=[sds((seq, d_model), F32)], out_specs=[_sp((t.rows, d_model), row_full)])
    dz2, dz2b, g_ffn_g, g_ffn_b, loss_part = _ln_loss_bwd(ffn, x1, tgt, ln_ffn_g, ln_ffn_b, t.ln_rows)

    (g_w2,) = _matmul(
        "grad_w_mlp_out", hmid, dz2b, _sp((seq, t.grad_rows), lambda i, j, k: (0, i)),
        _sp((seq, d_model), vec, single=True),
        grid=(d_ff // t.grad_rows, 1, 1), ta=True,
        out_shape=[sds((d_ff, d_model), BF16)], out_specs=[_sp((t.grad_rows, d_model), row_full)],
        epilogue=bf16_epi)
    scatter_w2 = _SplitReduceScatter("scatter_w_mlp_out", [g_w2.reshape(N_DEV, d_ff // N_DEV, d_model)])

    def dpre_epi(acc, i, ex, out, cols):
        out[0][:, cols] = (acc * ex[0][:, cols].astype(F32)).astype(BF16)

    (dpre,) = _matmul(
        "mlp_dpre", dz2b, w2_full, _sp((t.rows, d_model), lambda i, j, k: (i, 0)),
        _sp((t.ff_cols, d_model), lambda i, j, k: (j, 0)),
        grid=(seq // t.rows, d_ff // t.ff_cols, 1), j_outer=True, tb=True,
        extras=[(dact, _sp((t.rows, t.ff_cols), lambda i, j, k: (i, j))), after(scatter_w2.token)],
        out_shape=[sds((seq, d_ff), BF16)], out_specs=[_sp((t.rows, t.ff_cols), lambda i, j, k: (i, j))],
        epilogue=dpre_epi, n_split=t.ff_split)
    token_w2 = scatter_w2.combine_and_send(after=dpre)

    (dx1_mlp,) = _matmul(
        "mlp_dx", dpre, w1_full, _sp((t.rows, t.ff_k), lambda i, j, k: (i, k)),
        _sp((d_model, t.ff_k), lambda i, j, k: (0, k)),
        grid=(seq // t.rows, 1, d_ff // t.ff_k), tb=True, extras=[after(token_w2)],
        out_shape=[sds((seq, d_model), F32)], out_specs=[_sp((t.rows, d_model), row_full)])
    dz1, dz1b, g_mix_g, g_mix_b = _ln_bwd_rows(dx1_mlp, dz2, z1, ln_mix_g, ln_mix_b, t.ln_rows)

    def block_epi(acc, i, ex, out):
        out[0][0] = acc.astype(BF16)

    fs = d_ff // N_DEV
    (g_w1,) = _matmul(
        "grad_w_mlp_in", x1b, dpre, _sp((seq, t.grad_rows), lambda i, j, k: (0, i)),
        _sp((seq, fs), lambda i, j, k: (0, j)),
        grid=(d_model // t.grad_rows, N_DEV, 1), j_outer=True, ta=True,
        out_shape=[sds((N_DEV, d_model, fs), BF16)],
        out_specs=[_sp((1, t.grad_rows, fs), lambda i, j, k: (j, i, 0))], epilogue=block_epi)

    (dy,) = _matmul(
        "mix_dy", dz1b, wout_full, _sp((t.rows, d_model), lambda i, j, k: (i, 0)),
        _sp((dh, d_model), lambda i, j, k: (j, 0)),
        grid=(seq // t.rows, 2, 1), j_outer=True, tb=True,
        out_shape=[sds((seq, d_model), F32)], out_specs=[_sp((t.rows, dh), lambda i, j, k: (i, j))],
        epilogue=plain_epi)
    (g_wout,) = _matmul(
        "grad_w_out", y, dz1b, _sp((seq, t.grad_rows), lambda i, j, k: (0, i)), _sp((seq, d_model), vec, single=True),
        grid=(d_model // t.grad_rows, 1, 1), ta=True,
        out_shape=[sds((d_model, d_model), BF16)], out_specs=[_sp((t.grad_rows, d_model), row_full)],
        epilogue=bf16_epi)
    scatter_w1 = _SplitReduceScatter("scatter_w_mlp_in", [g_w1, g_wout.reshape(N_DEV, d_model // N_DEV, d_model)])

    dproj_pool, g_wpool, g_pscale = _pool_bwd(d_pool, dy, wpool_full, pool_scale, scatter_w1.token, seq, d_model)
    token_w1 = scatter_w1.combine_and_send(after=dproj_pool)
    dproj, g_pack, g_convb, g_wcat = _lru_bwd(p3, dy, h0p, h1p, dproj_pool, pack_full, conv_b, wcat,
                                              token_w1, seq, d_model)
    g_wa = jnp.stack([g_wcat[:, :, 0:lh], g_wcat[:, :, 2 * lh:3 * lh]])
    g_wi = jnp.stack([g_wcat[:, :, lh:2 * lh], g_wcat[:, :, 3 * lh:4 * lh]])

    rep_parts = [_rows128(g_wa), _rows128(g_wi), _rows128(g_mix_g), _rows128(g_mix_b), _rows128(g_ffn_g),
                 _rows128(g_ffn_b), _rows128(g_pscale), _rows128(g_convb)]
    rep_rows = [p.shape[0] for p in rep_parts]
    n_rep = sum(rep_rows)
    small = jnp.concatenate(rep_parts + [_rows128(g_pack), loss_part], axis=0)
    small_gather = _SplitGather("gather_small_grads", [(small[None], 0)], F32, after=small)

    ws = 3 * dh // N_DEV

    def pair_epi(acc, i, ex, out):
        out[0][0] = acc[:, :ws].astype(BF16)
        out[0][1] = acc[:, ws:].astype(BF16)

    (g_win,) = _matmul(
        "grad_w_in", xb, dproj, _sp((seq, t.grad_rows), lambda i, j, k: (0, i)),
        _sp((seq, 2 * ws), lambda i, j, k: (0, j)),
        grid=(d_model // t.grad_rows, N_DEV // 2, 1), ta=True, extras=[after(small_gather.token)],
        out_shape=[sds((N_DEV, d_model, ws), BF16)],
        out_specs=[_sp((2, t.grad_rows, ws), lambda i, j, k: (j, i, 0))], epilogue=pair_epi)
    scatter_mix = _SplitReduceScatter(
        "scatter_mixer", [g_win, g_wpool.reshape(N_DEV, pg // N_DEV * len(POOL_WINDOWS), pg)])

    def adam_big(name, own_landed, w, m, v):
        own, landed = own_landed
        shp = w.shape
        two = lambda a: a.reshape(-1, shp[-1])
        res = _sum_adamw(name, own, landed, two(w), two(m), two(v))
        return [r.reshape(shp) for r in res]

    (r_w2,) = scatter_w2.wait(after=scatter_mix.token)
    o_w2 = adam_big("adam_w_mlp_out", r_w2, w_mlp_out, m_w_mlp_out, v_w_mlp_out)
    token_mix = scatter_mix.combine_and_send(after=o_w2[0])

    def dx_epi(acc, i, ex, out):
        out[0][...] = ALPHA * ex[0][...] + acc

    (dx,) = _matmul(
        "grad_x", dproj, win_full, _sp((t.ln_rows * 2, 3 * dh), lambda i, j, k: (i, 0)),
        _sp((d_model, 3 * dh), vec, single=True),
        grid=(seq // (t.ln_rows * 2), 1, 1), tb=True,
        extras=[(dz1, _sp((t.ln_rows * 2, d_model), row_full)), after(token_mix)],
        out_shape=[sds((seq, d_model), F32)], out_specs=[_sp((t.ln_rows * 2, d_model), row_full)],
        epilogue=dx_epi)
    r_w1, r_wout = scatter_w1.wait(after=dx)
    o_w1 = adam_big("adam_w_mlp_in", r_w1, w_mlp_in, m_w_mlp_in, v_w_mlp_in)
    o_wout = adam_big("adam_w_out", r_wout, w_out, m_w_out, v_w_out)
    r_win, r_wpool = scatter_mix.wait(after=o_wout[0])
    o_win = adam_big("adam_w_in", r_win, w_in, m_w_in, v_w_in)
    o_wpool = adam_big("adam_w_pool", r_wpool, w_pool, m_w_pool, v_w_pool)

    small_gather.relay(after=o_win[0])
    (small_all,) = small_gather.wait(after=o_wpool[0])

    rep_w = [w_rg_a, w_rg_i, ln_mix_g, ln_mix_b, ln_ffn_g, ln_ffn_b, pool_scale, conv_b]
    rep_m = [m_w_rg_a, m_w_rg_i, m_ln_mix_g, m_ln_mix_b, m_ln_ffn_g, m_ln_ffn_b, m_pool_scale, m_conv_b]
    rep_v = [v_w_rg_a, v_w_rg_i, v_ln_mix_g, v_ln_mix_b, v_ln_ffn_g, v_ln_ffn_b, v_pool_scale, v_conv_b]
    cat = lambda arrs: jnp.concatenate([_rows128(a) for a in arrs], axis=0)
    o_rep = _sum_adamw("adam_replicated", None, small_all, cat(rep_w), cat(rep_m), cat(rep_v))

    my_idx = _dev_index(_where_am_i())
    head_parts = lax.dynamic_slice_in_dim(small_all, n_rep + my_idx * SMALL_ROWS, SMALL_ROWS, axis=1)
    o_head = _sum_adamw("adam_head", None, head_parts, pack_mine,
                        small_pack(m_conv_w, m_b_rg_a, m_b_rg_i, m_rg_lambda),
                        small_pack(v_conv_w, v_b_rg_a, v_b_rg_i, v_rg_lambda))

    def unpack_rep(packed):
        out, r = [], 0
        for wgt, rows in zip(rep_w, rep_rows):
            out.append(packed[r:r + rows].reshape(wgt.shape))
            r += rows
        return out

    def unpack_head(packed):
        return [packed[0:4].reshape(conv_w.shape), packed[4:6].reshape(b_rg_a.shape),
                packed[6:8].reshape(b_rg_i.shape), packed[8:10].reshape(rg_lambda.shape)]

    loss = _sum_blocks("loss_sum", small_all[:, n_rep + N_HEADS * SMALL_ROWS:, :])[0, 0]

    outs = [loss, dx[None]]
    for kind in range(4):
        ra, ri, mg, mb, fg, fb, ps, cb = unpack_rep(o_rep[kind])
        cw, ba, bi, lam = unpack_head(o_head[kind])
        outs += [mg, mb, o_win[kind], o_wpool[kind], ps, cw, cb, ra, ba, ri, bi, lam, o_wout[kind], fg, fb,
                 o_w1[kind], o_w2[kind]]
    return tuple(outs)
```

```python
import functools

import jax
import jax.numpy as jnp
from jax import lax
from jax.experimental import pallas as pl
from jax.experimental.pallas import tpu as pltpu

F32 = jnp.float32
BF16 = jnp.bfloat16
MESH = pl.DeviceIdType.MESH
ANY = pl.BlockSpec(memory_space=pl.ANY)

N_DEV = 8
POOL_WINDOWS = (2, 4, 8, 16)
N_HEADS = 8
RG_C = 8.0
LN_EPS = 1e-5
ALPHA = 2.0 ** 0.25
ADAM_LR = 0.001
ADAM_B1 = 0.9
ADAM_B2 = 0.999
ADAM_EPS = 1e-08
ADAM_WD = 0.01
ADAM_STEP = 10

VMEM_LIMIT = 56 * 1024 * 1024
WIN_HALO = 16
CONV_HALO = 8
SMALL_ROWS = 16


def _params(n_grid):
    return pltpu.CompilerParams(dimension_semantics=("arbitrary",) * n_grid, vmem_limit_bytes=VMEM_LIMIT)


def _shift(v, j):
    n = v.shape[0]
    s = (-j) % n
    return v if s == 0 else pltpu.roll(v, s, 0)


def _sigmoid(x):
    return 0.5 * jnp.tanh(0.5 * x) + 0.5


def _softplus(z):
    e = jnp.exp(-jnp.abs(z))
    u = 1.0 + e
    log1p = jnp.where(u == 1.0, e, jnp.log(u) * (e / jnp.where(u == 1.0, 1.0, u - 1.0)))
    return jnp.maximum(z, 0.0) + log1p


_GELU_C = 0.7978845608028654
_GELU_K = 0.044715


def _gelu_and_grad(x):
    x2 = x * x
    t = jnp.tanh(_GELU_C * (x + _GELU_K * x * x2))
    g = 0.5 * x * (1.0 + t)
    dg = 0.5 * (1.0 + t) + 0.5 * x * (1.0 - t * t) * (_GELU_C * (1.0 + 3.0 * _GELU_K * x2))
    return g, dg


def _ln_fwd(z, g, b):
    mu = jnp.mean(z, axis=-1, keepdims=True)
    zc = z - mu
    var = jnp.mean(zc * zc, axis=-1, keepdims=True)
    rstd = lax.rsqrt(var + LN_EPS)
    xhat = zc * rstd
    return xhat * g + b, xhat, rstd


def _ln_bwd(dy, xhat, rstd, g):
    dxhat = dy * g
    m1 = jnp.mean(dxhat, axis=-1, keepdims=True)
    m2 = jnp.mean(dxhat * xhat, axis=-1, keepdims=True)
    dz = rstd * (dxhat - m1 - xhat * m2)
    dg = jnp.sum(dy * xhat, axis=0, keepdims=True)
    db = jnp.sum(dy, axis=0, keepdims=True)
    return dz, dg, db


def _acc_rows(ref, first, val):
    @pl.when(first)
    def _():
        ref[...] = val

    @pl.when(jnp.logical_not(first))
    def _():
        ref[...] += val


def _sp(shape, fn, single=False):
    return shape, fn, single


def _matmul(name, a, b, a_spec, b_spec, *, grid, j_outer=False, ta=False, tb=False, extras=(), out_shape, out_specs,
            epilogue=None, n_split=1):
    ni, nj, nk = grid
    n_ex = len(extras)
    dims = (((0 if ta else 1,), (1 if tb else 0,)), ((), ()))

    def mk(spec):
        shape, fn, single = spec
        index = (lambda g0, g1, g2: fn(g1, g0, g2)) if j_outer else fn
        return pl.BlockSpec(shape, index, pipeline_mode=pl.Buffered(1)) if single else pl.BlockSpec(shape, index)

    def body(a_ref, b_ref, *rest):
        ex_refs = rest[:n_ex]
        out_refs = rest[n_ex:]
        i = pl.program_id(1 if j_outer else 0)
        if n_split > 1:
            av = a_ref[...].astype(BF16)
            width = b_ref.shape[0 if tb else 1] // n_split
            for c in range(n_split):
                cols = pl.ds(c * width, width)
                bv = (b_ref[cols, :] if tb else b_ref[:, cols]).astype(BF16)
                epilogue(lax.dot_general(av, bv, dims, preferred_element_type=F32), i, ex_refs, out_refs, cols)
            return
        part = lax.dot_general(a_ref[...].astype(BF16), b_ref[...].astype(BF16), dims, preferred_element_type=F32)
        if nk == 1:
            epilogue(part, i, ex_refs, out_refs)
        else:
            @pl.when(pl.program_id(2) == 0)
            def _():
                out_refs[0][...] = part

            @pl.when(pl.program_id(2) > 0)
            def _():
                out_refs[0][...] += part

    return pl.pallas_call(
        body, name=name, grid=(nj, ni, nk) if j_outer else (ni, nj, nk),
        in_specs=[mk(a_spec), mk(b_spec)] + [mk(s) for _, s in extras],
        out_specs=[mk(s) for s in out_specs], out_shape=list(out_shape),
        compiler_params=_params(3),
    )(a, b, *[x for x, _ in extras])


def _bs(shape, fn):
    return pl.BlockSpec(shape, fn)


def _where_am_i():
    x, y, c = lax.axis_index("x"), lax.axis_index("y"), lax.axis_index("c")
    return x, y, c


def _dev_index(p):
    return 4 * p[0] + 2 * p[1] + p[2]


def _slab(ref, axis, idx, size):
    sl = [slice(None)] * len(ref.shape)
    sl[axis] = pl.ds(idx * size, size)
    return ref.at[tuple(sl)]


def _all_gather(name, items):
    n = len(items)
    shapes = []
    for shard, axis in items:
        s = list(shard.shape)
        s[axis] *= N_DEV
        shapes.append(jax.ShapeDtypeStruct(tuple(s), shard.dtype))

    def body(*refs):
        in_refs, out_refs = refs[:n], refs[n:2 * n]
        send_sems, recv_sems, local_sems = refs[2 * n:]
        x, y, c = _where_am_i()
        me, sibling = (x, y, c), (x, y, 1 - c)
        chips = [(1 - x, y), (x, 1 - y), (1 - x, 1 - y)]

        def blk(a, p):
            axis = items[a][1]
            return _slab(out_refs[a], axis, _dev_index(p), items[a][0].shape[axis])

        def copy(a, k, block, to, src=None):
            return pltpu.make_async_remote_copy(
                src_ref=blk(a, block) if src is None else src, dst_ref=blk(a, block),
                send_sem=send_sems.at[a, k], recv_sem=recv_sems.at[a, k], device_id=to, device_id_type=MESH)

        mine = [pltpu.make_async_copy(in_refs[a], blk(a, me), local_sems.at[a]) for a in range(n)]
        for cp in mine:
            cp.start()
        first = []
        for a in range(n):
            first.append(copy(a, 0, me, sibling, src=in_refs[a]))
            first += [copy(a, 1 + j, me, (*chip, c), src=in_refs[a]) for j, chip in enumerate(chips)]
        for cp in first:
            cp.start()
        passed = []
        for a in range(n):
            for j, chip in enumerate(chips):
                copy(a, 1 + j, (*chip, c), me).wait_recv()
                fw = copy(a, 4 + j, (*chip, c), sibling)
                fw.start()
                passed.append(fw)
        for a in range(n):
            copy(a, 0, sibling, me).wait_recv()
            for j, chip in enumerate(chips):
                copy(a, 4 + j, (*chip, 1 - c), me).wait_recv()
        for cp in first + passed:
            cp.wait_send()
        for cp in mine:
            cp.wait()

    outs = pl.pallas_call(
        body, name=name, out_shape=shapes, in_specs=[ANY] * n, out_specs=[ANY] * n,
        scratch_shapes=[pltpu.SemaphoreType.DMA((n, 7)), pltpu.SemaphoreType.DMA((n, 7)),
                        pltpu.SemaphoreType.DMA((n,))],
    )(*[s for s, _ in items])
    return list(outs)


HBM = pl.BlockSpec(memory_space=pltpu.HBM)
SEM = pl.BlockSpec(memory_space=pltpu.SEMAPHORE)
DATAFLOW = pltpu.SideEffectType.DATAFLOW_SIDE_EFFECTING


def _in_hbm(a):
    return pltpu.with_memory_space_constraint(a, pltpu.HBM)


def _token_shape():
    return jax.ShapeDtypeStruct((8, 128), F32)


def _split_start(name, n_sems, bufs, issue):
    nb = len(bufs)

    def body(*refs):
        issue(refs[:nb], refs[nb], refs[nb + 1])
        refs[-1][...] = jnp.zeros((8, 128), F32)

    outs = pl.pallas_call(
        body, name=name,
        out_shape=(pltpu.SemaphoreType.DMA((n_sems,)), pltpu.SemaphoreType.DMA((n_sems,)),
                   *[pltpu.HBM(b.shape, b.dtype) for b in bufs], _token_shape()),
        in_specs=[HBM] * nb, out_specs=(SEM, SEM, *[HBM] * nb, pl.BlockSpec(memory_space=pltpu.VMEM)),
        input_output_aliases={i: 2 + i for i in range(nb)},
        compiler_params=pltpu.CompilerParams(has_side_effects=DATAFLOW),
    )(*[_in_hbm(b) for b in bufs])
    return outs[0], outs[1], list(outs[2:2 + nb]), outs[-1]


def _split_relay(name, n_sems, sems, bufs, after, relay):
    nb = len(bufs)

    def body(*refs):
        relay(refs[:nb], refs[nb], refs[nb + 1], refs[nb + 3], refs[nb + 4])
        refs[-1][...] = jnp.zeros((8, 128), F32)

    outs = pl.pallas_call(
        body, name=name,
        out_shape=(pltpu.SemaphoreType.DMA((n_sems,)), pltpu.SemaphoreType.DMA((n_sems,)),
                   *[pltpu.HBM(b.shape, b.dtype) for b in bufs], _token_shape()),
        in_specs=[HBM] * nb + [SEM, SEM, ANY],
        out_specs=(SEM, SEM, *[HBM] * nb, pl.BlockSpec(memory_space=pltpu.VMEM)),
        input_output_aliases={i: 2 + i for i in range(nb)},
        compiler_params=pltpu.CompilerParams(has_side_effects=DATAFLOW),
    )(*bufs, sems[0], sems[1], after)
    return outs[0], outs[1], list(outs[2:2 + nb]), outs[-1]


def _split_wait(name, sems, bufs, after, finish):
    nb = len(bufs)

    def body(*refs):
        finish(refs[:nb], refs[nb], refs[nb + 1])

    outs = pl.pallas_call(
        body, name=name, out_shape=[pltpu.HBM(b.shape, b.dtype) for b in bufs],
        in_specs=[HBM] * nb + [SEM, SEM, ANY], out_specs=[HBM] * nb,
        input_output_aliases={i: i for i in range(nb)},
        compiler_params=pltpu.CompilerParams(has_side_effects=DATAFLOW),
    )(*bufs, sems[0], sems[1], after)
    return list(outs)


def _place(name, items, dtype, after):
    ids = jnp.reshape(_dev_index(_where_am_i()), (1,)).astype(jnp.int32)
    outs = []
    for a, (shard, axis) in enumerate(items):
        rows, cols = shard.shape[-2], shard.shape[-1]
        tr = rows
        while tr * cols * shard.dtype.itemsize > 4 * 1024 * 1024 and tr % 32 == 0:
            tr //= 2
        nt = rows // tr
        full = list(shard.shape)
        full[axis] *= N_DEV
        if shard.ndim == 2 and axis == 0:
            in_spec = _bs((tr, cols), lambda i, ids: (i, 0))
            out_spec = _bs((tr, cols), lambda i, ids, nt=nt: (ids[0] * nt + i, 0))
        elif shard.ndim == 2 and axis == 1:
            in_spec = _bs((tr, cols), lambda i, ids: (i, 0))
            out_spec = _bs((tr, cols), lambda i, ids: (i, ids[0]))
        elif shard.ndim == 3 and axis == 1:
            tr, nt = rows, shard.shape[0]
            in_spec = _bs((None, rows, cols), lambda i, ids: (i, 0, 0))
            out_spec = _bs((None, rows, cols), lambda i, ids: (i, ids[0], 0))
        else:
            assert shard.ndim == 3 and axis == 0 and shard.shape[0] == 1
            in_spec = _bs((None, tr, cols), lambda i, ids: (0, i, 0))
            out_spec = _bs((None, tr, cols), lambda i, ids: (ids[0], i, 0))

        def body(ids_ref, in_ref, after_ref, out_ref):
            del ids_ref, after_ref
            out_ref[...] = in_ref[...].astype(out_ref.dtype)

        outs.append(pl.pallas_call(
            body, name=f"{name}{a}",
            grid_spec=pltpu.PrefetchScalarGridSpec(
                num_scalar_prefetch=1, grid=(nt,), in_specs=[in_spec, ANY], out_specs=out_spec),
            out_shape=jax.ShapeDtypeStruct(tuple(full), dtype), compiler_params=_params(1),
        )(ids, shard, after))
    return outs


class _SplitGather:
    def __init__(self, name, items, dtype, after):
        self.name, self.items, self.n = name, items, len(items)
        fulls = _place(name + "_place", items, dtype, after)
        n = self.n

        def issue(refs, send, recv):
            me, sibling, chips, c = self._geometry()
            for a in range(n):
                self._copy1(refs, send, recv, a, 0, me, sibling).start()
                for j, chip in enumerate(chips):
                    self._copy1(refs, send, recv, a, 1 + j, me, (*chip, c)).start()

        self.send, self.recv, self.bufs, self.token = _split_start(name + "_start", 4 * n, fulls, issue)

    @staticmethod
    def _geometry():
        x, y, c = _where_am_i()
        return (x, y, c), (x, y, 1 - c), [(1 - x, y), (x, 1 - y), (1 - x, 1 - y)], c

    def _blk(self, refs, a, p):
        shard, axis = self.items[a]
        return _slab(refs[a], axis, _dev_index(p), shard.shape[axis])

    def _copy1(self, refs, send, recv, a, k, owner, to):
        return pltpu.make_async_remote_copy(
            src_ref=self._blk(refs, a, owner), dst_ref=self._blk(refs, a, owner), send_sem=send.at[4 * a + k],
            recv_sem=recv.at[4 * a + k], device_id=to, device_id_type=MESH)

    def _copy2(self, refs, send, recv, a, j, owner, to):
        return pltpu.make_async_remote_copy(
            src_ref=self._blk(refs, a, owner), dst_ref=self._blk(refs, a, owner), send_sem=send.at[3 * a + j],
            recv_sem=recv.at[3 * a + j], device_id=to, device_id_type=MESH)

    def relay(self, after):
        n = self.n

        def relay(refs, send_in, recv_in, send_out, recv_out):
            me, sibling, chips, c = self._geometry()
            for a in range(n):
                for j, chip in enumerate(chips):
                    self._copy1(refs, send_in, recv_in, a, 1 + j, (*chip, c), me).wait_recv()
                    self._copy2(refs, send_out, recv_out, a, j, (*chip, c), sibling).start()
            for a in range(n):
                self._copy1(refs, send_in, recv_in, a, 0, sibling, me).wait_recv()
                for k in range(4):
                    self._copy1(refs, send_in, recv_in, a, k, me, sibling).wait_send()

        self.send, self.recv, self.bufs, self.token = _split_relay(
            self.name + "_relay", 3 * n, (self.send, self.recv), self.bufs, after, relay)
        return self.token

    def wait(self, after):
        n = self.n

        def finish(refs, send, recv):
            me, sibling, chips, c = self._geometry()
            for a in range(n):
                for j, chip in enumerate(chips):
                    self._copy2(refs, send, recv, a, j, (*chip, 1 - c), me).wait_recv()
                    self._copy2(refs, send, recv, a, j, (*chip, c), sibling).wait_send()

        return _split_wait(self.name + "_wait", (self.send, self.recv), self.bufs, after, finish)


class _SplitReduceScatter:
    def __init__(self, name, grads):
        self.name, self.n = name, len(grads)
        n = self.n
        g4 = [g.reshape(4, 2, *g.shape[1:]) for g in grads]
        land = [lax.empty((4, 1, *g.shape[1:]), g.dtype) for g in grads]

        def issue(refs, send, recv):
            for a in range(n):
                self._swap(refs, send, recv, a).start()

        self.send, self.recv, self.bufs, self.token = _split_start(name + "_d2d_start", n, g4 + land, issue)

    def _swap(self, refs, send, recv, a):
        x, y, c = _where_am_i()
        return pltpu.make_async_remote_copy(
            src_ref=refs[a].at[:, pl.ds(1 - c, 1)], dst_ref=refs[self.n + a], send_sem=send.at[a], recv_sem=recv.at[a],
            device_id=(x, y, 1 - c), device_id_type=MESH)

    def _hop(self, refs, send, recv, a, m):
        x, y, c = _where_am_i()
        px = (1 - x) if m & 2 else x
        py = (1 - y) if m & 1 else y
        return pltpu.make_async_remote_copy(
            src_ref=refs[a].at[2 * px + py], dst_ref=refs[self.n + a].at[m - 1], send_sem=send.at[3 * a + m - 1],
            recv_sem=recv.at[3 * a + m - 1], device_id=(px, py, c), device_id_type=MESH)

    def combine_and_send(self, after):
        n = self.n

        def finish(refs, send, recv):
            for a in range(n):
                self._swap(refs, send, recv, a).wait()

        bufs = _split_wait(self.name + "_d2d_wait", (self.send, self.recv), self.bufs, after, finish)
        x, y, c = _where_am_i()
        ids = jnp.stack([c, 2 * x + y]).astype(jnp.int32)
        self.own, sums = [], []
        for a in range(n):
            own, hb = _pair_sum(f"{self.name}_sum{a}", bufs[a], bufs[n + a], ids)
            self.own.append(own)
            sums.append(hb)
        land = [lax.empty((3, *h.shape[1:]), h.dtype) for h in sums]

        def issue(refs, send, recv):
            for a in range(n):
                for m in (1, 2, 3):
                    self._hop(refs, send, recv, a, m).start()

        self.send, self.recv, self.bufs, self.token = _split_start(self.name + "_ici_start", 3 * n, sums + land, issue)
        return self.token

    def wait(self, after):
        n = self.n

        def finish(refs, send, recv):
            for a in range(n):
                for m in (1, 2, 3):
                    self._hop(refs, send, recv, a, m).wait()

        bufs = _split_wait(self.name + "_ici_wait", (self.send, self.recv), self.bufs, after, finish)
        return list(zip(self.own, bufs[n:]))


def _pair_sum(name, g4, land, ids):
    rows, cols = g4.shape[2], g4.shape[3]
    tr = rows
    while tr * cols * 2 > 1024 * 1024 and tr % 32 == 0:
        tr //= 2

    def body(ids_ref, g_ref, l_ref, own_ref, sum_ref):
        h = g_ref[...].astype(F32) + l_ref[...].astype(F32)
        sum_ref[...] = h.astype(sum_ref.dtype)

        @pl.when(pl.program_id(1) == ids_ref[1])
        def _():
            own_ref[...] = h

    return pl.pallas_call(
        body, name=name,
        grid_spec=pltpu.PrefetchScalarGridSpec(
            num_scalar_prefetch=1, grid=(rows // tr, 4),
            in_specs=[_bs((None, None, tr, cols), lambda i, q, ids: (q, ids[0], i, 0)),
                      _bs((None, None, tr, cols), lambda i, q, ids: (q, 0, i, 0))],
            out_specs=[_bs((tr, cols), lambda i, q, ids: (i, 0)), _bs((None, tr, cols), lambda i, q, ids: (q, i, 0))]),
        out_shape=[jax.ShapeDtypeStruct((rows, cols), F32), jax.ShapeDtypeStruct((4, rows, cols), g4.dtype)],
        compiler_params=_params(2),
    )(ids, g4, land)


def _win_sum(ext, w, off):
    s = ext + _shift(ext, -1)
    if w >= 4:
        s = _shift(s, -1) + _shift(s, 1)
    if w >= 8:
        s = _shift(s, -2) + _shift(s, 2)
    if w >= 16:
        s = _shift(s, -4) + _shift(s, 4)
    return _shift(s, off) if off else s


def _inv_count(r0, t, w, seq):
    pos = r0 + lax.broadcasted_iota(jnp.int32, (t, 1), 0)
    cnt = jnp.minimum(pos + w // 2, seq) - jnp.maximum(pos - w // 2, 0)
    return 1.0 / cnt.astype(F32)


def _pool_fwd(p3, w_pool, pool_scale, seq, d_model):
    dp = d_model // 2
    pg = dp // len(POOL_WINDOWS)
    t = min(128, seq)
    n_chunks = seq // t
    h = WIN_HALO

    def body(u_ref, w_ref, sc_ref, d_ref, y_ref, pad_ref):
        g = pl.program_id(0)
        zeros = jnp.zeros((h, pg), F32)
        pad_ref[0:h, :] = zeros
        pad_ref[h + seq:h + seq + h, :] = zeros

        def fill(ci, _):
            r0 = pl.multiple_of(ci * t, t)
            pad_ref[pl.ds(h + r0, t), :] = u_ref[pl.ds(r0, t), :]
            return 0

        lax.fori_loop(0, n_chunks, fill, 0)
        wmat = w_ref[...]
        scale = sc_ref[...]
        for gi, w in enumerate(POOL_WINDOWS):
            @pl.when(g == gi)
            def _(w=w):
                def chunk(ci, _):
                    r0 = pl.multiple_of(ci * t, t)
                    ext = pad_ref[pl.ds(r0, t + 2 * h), :]
                    mean = _win_sum(ext, w, 0)[h:h + t, :] * _inv_count(r0, t, w, seq)
                    d = (mean - ext[h:h + t, :]).astype(BF16)
                    d_ref[pl.ds(r0, t), :] = d
                    q = jnp.dot(d, wmat, preferred_element_type=F32)
                    y_ref[pl.ds(r0, t), :] = (q * scale).astype(BF16)
                    return 0

                lax.fori_loop(0, n_chunks, chunk, 0, unroll=2)

    return pl.pallas_call(
        body, name="pool_fwd", grid=(len(POOL_WINDOWS),),
        in_specs=[_bs((None, seq, pg), lambda g: (0, 0, g)), _bs((None, pg, pg), lambda g: (g, 0, 0)),
                  _bs((1, pg), lambda g: (0, g))],
        out_specs=[_bs((seq, pg), lambda g: (0, g)), _bs((seq, pg), lambda g: (0, g))],
        out_shape=[jax.ShapeDtypeStruct((seq, dp), BF16), jax.ShapeDtypeStruct((seq, d_model), BF16)],
        scratch_shapes=[pltpu.VMEM((seq + 2 * h, pg), F32)],
        compiler_params=_params(1),
    )(p3, w_pool, pool_scale)


def _pool_bwd(d, dy, w_pool, pool_scale, token, seq, d_model):
    dp = d_model // 2
    pg = dp // len(POOL_WINDOWS)
    t = min(128, seq)
    n_chunks = seq // t
    h = WIN_HALO
    tn_dims = (((0,), (0,)), ((), ()))
    nt_dims = (((1,), (1,)), ((), ()))

    def body(d_ref, dy_ref, w_ref, sc_ref, tok_ref, du_ref, dwb_ref, dsc_ref, pad_ref, dd_ref, dw_ref):
        del tok_ref
        g = pl.program_id(0)
        zeros = jnp.zeros((h, pg), F32)
        pad_ref[0:h, :] = zeros
        pad_ref[h + seq:h + seq + h, :] = zeros
        wmat = w_ref[...]
        scale = sc_ref[...]
        for gi, w in enumerate(POOL_WINDOWS):
            @pl.when(g == gi)
            def _(w=w):
                dw_ref[...] = jnp.zeros((pg, pg), F32)

                def first(ci, dsc):
                    r0 = pl.multiple_of(ci * t, t)
                    dv = d_ref[pl.ds(r0, t), :]
                    dyv = dy_ref[pl.ds(r0, t), :]
                    q = jnp.dot(dv, wmat, preferred_element_type=F32)
                    dsc = dsc + jnp.sum(dyv * q, axis=0, keepdims=True)
                    dq = (dyv * scale).astype(BF16)
                    dw_ref[...] += lax.dot_general(dv, dq, tn_dims, preferred_element_type=F32)
                    dd = lax.dot_general(dq, wmat, nt_dims, preferred_element_type=F32)
                    dd_ref[pl.ds(r0, t), :] = dd
                    pad_ref[pl.ds(h + r0, t), :] = dd * _inv_count(r0, t, w, seq)
                    return dsc

                def first_pair(cj, dsc):
                    return first(2 * cj + 1, first(2 * cj, dsc))

                dsc_ref[...] = lax.fori_loop(0, n_chunks // 2, first_pair, jnp.zeros((1, pg), F32))
                dwb_ref[...] = dw_ref[...].reshape(N_DEV, pg // N_DEV, pg).astype(BF16)

                def second(ci, _):
                    r0 = pl.multiple_of(ci * t, t)
                    ext = pad_ref[pl.ds(r0, t + 2 * h), :]
                    back = _win_sum(ext, w, 1)[h:h + t, :]
                    du_ref[pl.ds(r0, t), :] = (back - dd_ref[pl.ds(r0, t), :]).astype(BF16)
                    return 0

                lax.fori_loop(0, n_chunks, second, 0, unroll=2)

    return pl.pallas_call(
        body, name="pool_bwd", grid=(len(POOL_WINDOWS),),
        in_specs=[_bs((seq, pg), lambda g: (0, g)), _bs((seq, pg), lambda g: (0, g)),
                  _bs((None, pg, pg), lambda g: (g, 0, 0)), _bs((1, pg), lambda g: (0, g)),
                  _bs((8, 128), lambda g: (0, 0))],
        out_specs=[_bs((seq, pg), lambda g: (0, g)), _bs((N_DEV, None, pg // N_DEV, pg), lambda g: (0, g, 0, 0)),
                   _bs((1, pg), lambda g: (0, g))],
        out_shape=[jax.ShapeDtypeStruct((seq, 3 * dp), BF16),
                   jax.ShapeDtypeStruct((N_DEV, len(POOL_WINDOWS), pg // N_DEV, pg), BF16),
                   jax.ShapeDtypeStruct((1, dp), F32)],
        scratch_shapes=[pltpu.VMEM((seq + 2 * h, pg), F32), pltpu.VMEM((seq, pg), F32), pltpu.VMEM((pg, pg), F32)],
        compiler_params=_params(1),
    )(d, dy, w_pool, pool_scale, token)


def _tile_scan(n_tiles, lanes, loads, stores):
    row = lax.broadcasted_iota(jnp.int32, (8, lanes), 0)
    group = 8

    def local_scan(n, k):
        aa, bb = loads[n](k)
        for sh in (1, 2, 4):
            if n == 0:
                ok = row >= sh
                ap = jnp.where(ok, pltpu.roll(aa, sh, 0), 1.0)
                bp = jnp.where(ok, pltpu.roll(bb, sh, 0), 0.0)
            else:
                ok = row < 8 - sh
                ap = jnp.where(ok, pltpu.roll(aa, 8 - sh, 0), 1.0)
                bp = jnp.where(ok, pltpu.roll(bb, 8 - sh, 0), 0.0)
            bb = aa * bp + bb
            aa = aa * ap
        return aa, bb

    def step(s, carry):
        carry = list(carry)
        for n in range(2):
            tiles = [s * group + u if n == 0 else n_tiles - 1 - (s * group + u) for u in range(group)]
            local = [local_scan(n, k) for k in tiles]
            for k, (aa, bb) in zip(tiles, local):
                hh = bb + aa * carry[n]
                stores[n](k, hh)
                carry[n] = jnp.broadcast_to(hh[7:8, :] if n == 0 else hh[0:1, :], (8, lanes))
        return tuple(carry)

    zeros = jnp.zeros((8, lanes), F32)
    lax.fori_loop(0, n_tiles // group, step, (zeros, zeros))


def _gate_preacts(xc, wcat_ref):
    xcb = xc.astype(BF16)
    return xcb, jnp.dot(xcb, wcat_ref[...], preferred_element_type=F32)


def _gates(pre, n, pk_ref, sp):
    lh = pre.shape[1] // 4
    r = _sigmoid(pre[:, (2 * n) * lh:(2 * n + 1) * lh] + pk_ref[pl.ds(4 + n, 1), :])
    i = _sigmoid(pre[:, (2 * n + 1) * lh:(2 * n + 2) * lh] + pk_ref[pl.ds(6 + n, 1), :])
    log_a = (-RG_C * r) * sp[n]
    a = jnp.exp(log_a)
    x = 2.0 * log_a
    one_minus_a2 = jnp.where(x > -0.01, -(x * (1.0 + x * (0.5 + x * (1.0 / 6.0)))), 1.0 - a * a)
    m = jnp.sqrt(one_minus_a2)
    return r, i, a, m


def _conv_chunk(upad_ref, pk_ref, cb, r0, t):
    ext = upad_ref[pl.ds(r0, t + 2 * CONV_HALO), :]
    acc = pk_ref[pl.ds(1, 1), :] * ext
    for k in (0, 2, 3):
        acc = acc + pk_ref[pl.ds(k, 1), :] * _shift(ext, k - 1)
    return acc[CONV_HALO:CONV_HALO + t, :] + cb, ext


def _lru_fwd(p3, y_in, pack, conv_b, wcat, token, seq, d_model):
    dl = d_model // 2
    lh = dl // N_HEADS
    t = min(128, seq)
    n_chunks = seq // t
    seg = seq // 8
    hal = CONV_HALO
    first_rec_block = (d_model - dl) // lh

    def body(ur_ref, ug_ref, pk_ref, cb_ref, wcat_ref, yin_ref, tok_ref, y_ref, h0_ref, h1_ref,
             upad, a_scr, b_scr):
        del yin_ref, tok_ref
        zeros = jnp.zeros((hal, lh), F32)
        upad[0:hal, :] = zeros
        upad[hal + seq:hal + seq + hal, :] = zeros
        for ref in (h0_ref, h1_ref):
            ref[0:hal, :] = zeros
            ref[hal + seq:hal + seq + hal, :] = zeros

        def fill(ci, _):
            r0 = pl.multiple_of(ci * t, t)
            upad[pl.ds(hal + r0, t), :] = ur_ref[pl.ds(r0, t), :]
            return 0

        lax.fori_loop(0, n_chunks, fill, 0)
        cb = cb_ref[...]
        sp = [_softplus(-pk_ref[pl.ds(8 + n, 1), :]) for n in range(2)]

        def chunk(ci, _):
            r0 = pl.multiple_of(ci * t, t)
            xc, _ext = _conv_chunk(upad, pk_ref, cb, r0, t)
            _, pre = _gate_preacts(xc, wcat_ref)
            for n in range(2):
                _, i, a, m = _gates(pre, n, pk_ref, sp)
                a_scr[n, pl.ds(r0, t), :] = a
                b_scr[n, pl.ds(r0, t), :] = (m * i) * xc
            return 0

        lax.fori_loop(0, n_chunks, chunk, 0, unroll=2)

        def load(n):
            def get(k):
                at = pl.ds(pl.multiple_of(k * 8, 8), 8)
                return a_scr[n, at, :], b_scr[n, at, :]
            return get

        def store(ref):
            def put(k, v):
                ref[pl.ds(pl.multiple_of(hal + k * 8, 8), 8), :] = v
            return put

        _tile_scan(seq // 8, lh, [load(0), load(1)], [store(h0_ref), store(h1_ref)])

        def out(ci, _):
            r0 = pl.multiple_of(ci * t, t)
            hsum = h0_ref[pl.ds(hal + r0, t), :] + h1_ref[pl.ds(hal + r0, t), :]
            gl, _dg = _gelu_and_grad(ug_ref[pl.ds(r0, t), :])
            y_ref[pl.ds(r0, t), :] = (hsum * gl).astype(BF16)
            return 0

        lax.fori_loop(0, n_chunks, out, 0)

    return pl.pallas_call(
        body, name="lru_fwd", grid=(N_HEADS,),
        in_specs=[_bs((None, seq, lh), lambda h: (1, 0, h)), _bs((None, seq, lh), lambda h: (2, 0, h)),
                  _bs((None, SMALL_ROWS, lh), lambda h: (h, 0, 0)), _bs((1, lh), lambda h: (0, h)),
                  _bs((None, lh, 4 * lh), lambda h: (h, 0, 0)),
                  ANY, _bs((8, 128), lambda h: (0, 0))],
        out_specs=[_bs((seq, lh), lambda h: (0, first_rec_block + h)),
                   _bs((seq + 2 * hal, lh), lambda h: (0, h)), _bs((seq + 2 * hal, lh), lambda h: (0, h))],
        out_shape=[jax.ShapeDtypeStruct((seq, d_model), BF16), jax.ShapeDtypeStruct((seq + 2 * hal, dl), F32),
                   jax.ShapeDtypeStruct((seq + 2 * hal, dl), F32)],
        scratch_shapes=[pltpu.VMEM((seq + 2 * hal, lh), F32), pltpu.VMEM((2, seq, lh), F32),
                        pltpu.VMEM((2, seq, lh), F32)],
        input_output_aliases={5: 0},
        compiler_params=_params(1),
    )(p3, p3, pack, conv_b, wcat, y_in, token)


def _lru_bwd(p3, dy, h0p, h1p, dproj_in, pack, conv_b, wcat, token, seq, d_model):
    dl = d_model // 2
    lh = dl // N_HEADS
    t = min(128, seq)
    n_chunks = seq // t
    seg = seq // 8
    hal = CONV_HALO
    first_rec_block = (d_model - dl) // lh
    tn_dims = (((0,), (0,)), ((), ()))
    nt_dims = (((1,), (1,)), ((), ()))

    def body(ur_ref, ug_ref, dy_ref, h0_ref, h1_ref, pk_ref, cb_ref, wcat_ref, tok_ref, din_ref,
             dproj_ref, dpk_ref, dcb_ref, dwcat_ref,
             upad, a_scr, dh_scr, g_scr, dxc_pad, dpr_ref, out_sems, gate_scr):
        del din_ref, tok_ref
        zeros = jnp.zeros((hal, lh), F32)
        for ref in (upad, dxc_pad):
            ref[0:hal, :] = zeros
            ref[hal + seq:hal + seq + hal, :] = zeros
        for n in range(2):
            a_scr[n, 0:hal, :] = zeros
            a_scr[n, hal + seq:hal + seq + hal, :] = zeros

        def fill(ci, _):
            r0 = pl.multiple_of(ci * t, t)
            upad[pl.ds(hal + r0, t), :] = ur_ref[pl.ds(r0, t), :]
            return 0

        lax.fori_loop(0, n_chunks, fill, 0)
        cb = cb_ref[...]
        lam = [pk_ref[pl.ds(8 + n, 1), :] for n in range(2)]
        sp = [_softplus(-lam[n]) for n in range(2)]

        def chunk1(ci, _):
            r0 = pl.multiple_of(ci * t, t)
            xc, _ext = _conv_chunk(upad, pk_ref, cb, r0, t)
            _, pre = _gate_preacts(xc, wcat_ref)
            for n in range(2):
                r, i, a, m = _gates(pre, n, pk_ref, sp)
                a_scr[n, pl.ds(hal + r0, t), :] = a
                for q, v in enumerate((r, i, m)):
                    gate_scr[3 * n + q, pl.ds(r0, t), :] = v
            hsum = h0_ref[pl.ds(hal + r0, t), :] + h1_ref[pl.ds(hal + r0, t), :]
            gl, dgl = _gelu_and_grad(ug_ref[pl.ds(r0, t), :])
            dyv = dy_ref[pl.ds(r0, t), :]
            dh_scr[pl.ds(r0, t), :] = dyv * gl
            dpr_ref[1, pl.ds(r0, t), :] = ((dyv * hsum) * dgl).astype(BF16)
            return 0

        lax.fori_loop(0, n_chunks, chunk1, 0, unroll=2)

        def load(n):
            def get(k):
                r0 = pl.multiple_of(k * 8, 8)
                if n == 0:
                    coef = _shift(a_scr[0, pl.ds(pl.multiple_of(hal + r0, 8), 16), :], 1)[0:8, :]
                else:
                    coef = _shift(a_scr[1, pl.ds(pl.multiple_of(hal + r0 - 8, 8), 16), :], -1)[8:16, :]
                return coef, dh_scr[pl.ds(r0, 8), :]
            return get

        def store(n):
            def put(k, v):
                g_scr[n, pl.ds(pl.multiple_of(k * 8, 8), 8), :] = v
            return put

        _tile_scan(seq // 8, lh, [load(1), load(0)], [store(1), store(0)])

        dwcat_ref[...] = jnp.zeros((lh, 4 * lh), F32)

        def chunk3(ci, carry):
            dba, dbi, dlam, dcb = carry
            r0 = pl.multiple_of(ci * t, t)
            xc, _ext = _conv_chunk(upad, pk_ref, cb, r0, t)
            xcb = xc.astype(BF16)
            dxc = jnp.zeros((t, lh), F32)
            dba, dbi, dlam = list(dba), list(dbi), list(dlam)
            dpre = []
            for n in range(2):
                r, i, m = (gate_scr[3 * n + q, pl.ds(r0, t), :] for q in range(3))
                a = a_scr[n, pl.ds(hal + r0, t), :]
                hext = (h0_ref if n == 0 else h1_ref)[pl.ds(r0, t + 2 * hal), :]
                hprev = _shift(hext, -1 if n == 0 else 1)[hal:hal + t, :]
                gb = g_scr[n, pl.ds(r0, t), :]
                da = gb * hprev
                dm = gb * i * xc
                di = gb * m * xc
                dxc = dxc + gb * (m * i)
                dlog_a = da * a - dm * (a * a) / m
                dr = dlog_a * (-RG_C * sp[n])
                dlam[n] = dlam[n] + jnp.sum(dlog_a * r, axis=0, keepdims=True)
                dpr = dr * r * (1.0 - r)
                dpi = di * i * (1.0 - i)
                dba[n] = dba[n] + jnp.sum(dpr, axis=0, keepdims=True)
                dbi[n] = dbi[n] + jnp.sum(dpi, axis=0, keepdims=True)
                dpre += [dpr.astype(BF16), dpi.astype(BF16)]
            dpre = jnp.concatenate(dpre, axis=1)
            dwcat_ref[...] += lax.dot_general(xcb, dpre, tn_dims, preferred_element_type=F32)
            dxc = dxc + lax.dot_general(dpre, wcat_ref[...], nt_dims, preferred_element_type=F32)
            dxc_pad[pl.ds(hal + r0, t), :] = dxc
            dcb = dcb + jnp.sum(dxc, axis=0, keepdims=True)
            return tuple(dba), tuple(dbi), tuple(dlam), dcb

        zr = jnp.zeros((1, lh), F32)
        def chunk3_pair(cj, carry):
            return chunk3(2 * cj + 1, chunk3(2 * cj, carry))

        dba, dbi, dlam, dcb = lax.fori_loop(0, n_chunks // 2, chunk3_pair, ((zr, zr), (zr, zr), (zr, zr), zr))
        dcb_ref[...] = dcb
        for n in range(2):
            dpk_ref[pl.ds(4 + n, 1), :] = dba[n]
            dpk_ref[pl.ds(6 + n, 1), :] = dbi[n]
            dpk_ref[pl.ds(8 + n, 1), :] = dlam[n] * (RG_C * jax.nn.sigmoid(-lam[n]))
        dpk_ref[pl.ds(10, SMALL_ROWS - 10), :] = jnp.zeros((SMALL_ROWS - 10, lh), F32)

        def chunk4(ci, dtap):
            r0 = pl.multiple_of(ci * t, t)
            gext = dxc_pad[pl.ds(r0, t + 2 * hal), :]
            uext = upad[pl.ds(r0, t + 2 * hal), :]
            gmid = gext[hal:hal + t, :]
            du = pk_ref[pl.ds(1, 1), :] * gext
            for k in (0, 2, 3):
                du = du + pk_ref[pl.ds(k, 1), :] * _shift(gext, 1 - k)
            dpr_ref[0, pl.ds(r0, t), :] = du[hal:hal + t, :].astype(BF16)
            out = []
            for k in range(4):
                usl = _shift(uext, k - 1)[hal:hal + t, :]
                out.append(dtap[k] + jnp.sum(gmid * usl, axis=0, keepdims=True))
            return tuple(out)

        dtap = lax.fori_loop(0, n_chunks, chunk4, (zr, zr, zr, zr))
        for k in range(4):
            dpk_ref[pl.ds(k, 1), :] = dtap[k]

        head = pl.program_id(0)
        outs = [pltpu.make_async_copy(
            dpr_ref.at[b], dproj_ref.at[:, pl.ds(pl.multiple_of((1 + b) * dl + head * lh, lh), lh)], out_sems.at[b])
            for b in range(2)]
        for cp in outs:
            cp.start()
        for cp in outs:
            cp.wait()

    return pl.pallas_call(
        body, name="lru_bwd", grid=(N_HEADS,),
        in_specs=[_bs((None, seq, lh), lambda h: (1, 0, h)), _bs((None, seq, lh), lambda h: (2, 0, h)),
                  _bs((seq, lh), lambda h: (0, first_rec_block + h)),
                  _bs((seq + 2 * hal, lh), lambda h: (0, h)), _bs((seq + 2 * hal, lh), lambda h: (0, h)),
                  _bs((None, SMALL_ROWS, lh), lambda h: (h, 0, 0)), _bs((1, lh), lambda h: (0, h)),
                  _bs((None, lh, 4 * lh), lambda h: (h, 0, 0)),
                  _bs((8, 128), lambda h: (0, 0)), ANY],
        out_specs=[ANY, _bs((None, SMALL_ROWS, lh), lambda h: (h, 0, 0)),
                   _bs((1, lh), lambda h: (0, h)), _bs((None, lh, 4 * lh), lambda h: (h, 0, 0))],
        out_shape=[jax.ShapeDtypeStruct((seq, 3 * dl), BF16), jax.ShapeDtypeStruct((N_HEADS, SMALL_ROWS, lh), F32),
                   jax.ShapeDtypeStruct((1, dl), F32), jax.ShapeDtypeStruct((N_HEADS, lh, 4 * lh), F32)],
        scratch_shapes=[pltpu.VMEM((seq + 2 * hal, lh), F32), pltpu.VMEM((2, seq + 2 * hal, lh), F32),
                        pltpu.VMEM((seq, lh), F32), pltpu.VMEM((2, seq, lh), F32),
                        pltpu.VMEM((seq + 2 * hal, lh), F32), pltpu.VMEM((2, seq, lh), BF16),
                        pltpu.SemaphoreType.DMA((2,)), pltpu.VMEM((6, seq, lh), F32)],
        input_output_aliases={9: 0},
        compiler_params=_params(1),
    )(p3, p3, dy, h0p, h1p, pack, conv_b, wcat, token, dproj_in)


class _tiles:
    def __init__(self, seq, d_model, d_ff):
        self.rows = min(1024, seq)
        self.ln_rows = min(256, seq)
        self.ff_cols = min(1024, d_ff)
        self.ff_split = 4
        self.ff_k = min(2048, d_ff)
        self.grad_rows = 512


def _ln_loss_bwd(ffn, x1, tgt, g, b, tr):
    seq, d = ffn.shape

    def body(f_ref, x_ref, t_ref, g_ref, b_ref, dz_ref, dzb_ref, dg_ref, db_ref, loss_ref):
        i = pl.program_id(0)
        gv = g_ref[...]
        z = ALPHA * x_ref[...] + f_ref[...]
        y, xhat, rstd = _ln_fwd(z, gv, b_ref[...])
        err = y - t_ref[...]
        part = 0.5 * jnp.sum(jnp.mean(err * err, axis=-1, keepdims=True), axis=0, keepdims=True)
        dz, dg, db = _ln_bwd(err * (1.0 / d), xhat, rstd, gv)
        dz_ref[...] = dz
        dzb_ref[...] = dz.astype(BF16)
        _acc_rows(dg_ref, i == 0, dg)
        _acc_rows(db_ref, i == 0, db)
        _acc_rows(loss_ref, i == 0, jnp.broadcast_to(part, (8, 128)))

    row = _bs((tr, d), lambda i: (i, 0))
    vec = _bs((1, d), lambda i: (0, 0))
    return pl.pallas_call(
        body, name="ln_ffn_loss", grid=(seq // tr,), in_specs=[row, row, row, vec, vec],
        out_specs=[row, row, vec, vec, _bs((8, 128), lambda i: (0, 0))],
        out_shape=[jax.ShapeDtypeStruct((seq, d), F32), jax.ShapeDtypeStruct((seq, d), BF16),
                   jax.ShapeDtypeStruct((1, d), F32), jax.ShapeDtypeStruct((1, d), F32),
                   jax.ShapeDtypeStruct((8, 128), F32)],
        compiler_params=_params(1),
    )(ffn, x1, tgt, g, b)


def _ln_bwd_rows(dx_branch, dres, z, g, b, tr):
    seq, d = z.shape

    def body(a_ref, r_ref, z_ref, g_ref, b_ref, dz_ref, dzb_ref, dg_ref, db_ref):
        i = pl.program_id(0)
        gv = g_ref[...]
        _, xhat, rstd = _ln_fwd(z_ref[...], gv, b_ref[...])
        dz, dg, db = _ln_bwd(ALPHA * r_ref[...] + a_ref[...], xhat, rstd, gv)
        dz_ref[...] = dz
        dzb_ref[...] = dz.astype(BF16)
        _acc_rows(dg_ref, i == 0, dg)
        _acc_rows(db_ref, i == 0, db)

    row = _bs((tr, d), lambda i: (i, 0))
    vec = _bs((1, d), lambda i: (0, 0))
    return pl.pallas_call(
        body, name="ln_mix_bwd", grid=(seq // tr,), in_specs=[row, row, row, vec, vec],
        out_specs=[row, row, vec, vec],
        out_shape=[jax.ShapeDtypeStruct((seq, d), F32), jax.ShapeDtypeStruct((seq, d), BF16),
                   jax.ShapeDtypeStruct((1, d), F32), jax.ShapeDtypeStruct((1, d), F32)],
        compiler_params=_params(1),
    )(dx_branch, dres, z, g, b)


def _to_bf16(name, a, token):
    rows, cols = a.shape
    tr = min(512, rows)

    def body(a_ref, tok_ref, o_ref):
        del tok_ref
        o_ref[...] = a_ref[...].astype(BF16)

    return pl.pallas_call(
        body, name=name, grid=(rows // tr,),
        in_specs=[_bs((tr, cols), lambda i: (i, 0)), _bs((8, 128), lambda i: (0, 0))],
        out_specs=_bs((tr, cols), lambda i: (i, 0)), out_shape=jax.ShapeDtypeStruct((rows, cols), BF16),
        compiler_params=_params(1),
    )(a, token)


def _sum_blocks(name, parts):
    def body(p_ref, o_ref):
        acc = p_ref[0]
        for s in range(1, parts.shape[0]):
            acc = acc + p_ref[s]
        o_ref[...] = acc

    return pl.pallas_call(body, name=name, out_shape=jax.ShapeDtypeStruct(parts.shape[1:], F32))(parts)


def _adamw_values(w, g, m, v):
    m = ADAM_B1 * m + (1.0 - ADAM_B1) * g
    v = ADAM_B2 * v + (1.0 - ADAM_B2) * (g * g)
    m_hat = m / (1.0 - ADAM_B1 ** ADAM_STEP)
    v_hat = v / (1.0 - ADAM_B2 ** ADAM_STEP)
    delta = -ADAM_LR * (m_hat / (jnp.sqrt(v_hat) + ADAM_EPS) + ADAM_WD * w)
    return delta, m, v


def _sum_adamw(name, own, parts, w, m, v):
    rows, cols = w.shape
    n_parts = parts.shape[0]
    tr = rows
    min_rows = 8 if parts.dtype == F32 else 16
    while tr * cols * 4 > 1024 * 1024 and tr % (2 * min_rows) == 0:
        tr //= 2

    def body(*refs):
        if own is None:
            p_ref, w_ref, m_ref, v_ref, g_ref, d_ref, mo_ref, vo_ref = refs
            g = p_ref[0].astype(F32)
            rest = range(1, n_parts)
        else:
            o_ref, p_ref, w_ref, m_ref, v_ref, g_ref, d_ref, mo_ref, vo_ref = refs
            g = o_ref[...]
            rest = range(n_parts)
        for s in rest:
            g = g + p_ref[s].astype(F32)
        delta, mn, vn = _adamw_values(w_ref[...], g, m_ref[...], v_ref[...])
        g_ref[...] = g
        d_ref[...] = delta
        mo_ref[...] = mn
        vo_ref[...] = vn

    spec = _bs((tr, cols), lambda i: (i, 0))
    lead = [] if own is None else [own]
    return pl.pallas_call(
        body, name=name, grid=(rows // tr,),
        in_specs=[spec] * len(lead) + [_bs((n_parts, tr, cols), lambda i: (0, i, 0)), spec, spec, spec],
        out_specs=[spec] * 4, out_shape=[jax.ShapeDtypeStruct((rows, cols), F32)] * 4,
        compiler_params=_params(1),
    )(*lead, parts, w, m, v)


def _rows128(a):
    return a.reshape(-1, 128)


def kernel(x, ln_mix_g, ln_mix_b, w_in, w_pool, pool_scale, conv_w, conv_b, w_rg_a, b_rg_a, w_rg_i, b_rg_i, rg_lambda, w_out, ln_ffn_g, ln_ffn_b, w_mlp_in, w_mlp_out, loss_target, m_ln_mix_g, m_ln_mix_b, m_w_in, m_w_pool, m_pool_scale, m_conv_w, m_conv_b, m_w_rg_a, m_b_rg_a, m_w_rg_i, m_b_rg_i, m_rg_lambda, m_w_out, m_ln_ffn_g, m_ln_ffn_b, m_w_mlp_in, m_w_mlp_out, v_ln_mix_g, v_ln_mix_b, v_w_in, v_w_pool, v_pool_scale, v_conv_w, v_conv_b, v_w_rg_a, v_b_rg_a, v_w_rg_i, v_b_rg_i, v_rg_lambda, v_w_out, v_ln_ffn_g, v_ln_ffn_b, v_w_mlp_in, v_w_mlp_out):
    seq, d_model = x.shape[1], x.shape[2]
    dh = d_model // 2
    lh = dh // N_HEADS
    pg = dh // len(POOL_WINDOWS)
    d_ff = w_mlp_in.shape[2] * N_DEV
    assert lh == 128 and conv_w.shape[3] == lh and w_pool.shape[2] * N_DEV == pg

    xs = x[0]
    tgt = loss_target[0]

    def small_pack(cw, ba, bi, lam):
        return jnp.concatenate([cw.reshape(4, lh), ba.reshape(2, lh), bi.reshape(2, lh), lam.reshape(2, lh),
                                jnp.zeros((SMALL_ROWS - 10, lh), F32)], axis=0)

    pack_mine = small_pack(conv_w, b_rg_a, b_rg_i, rg_lambda)
    pack_bits = lax.bitcast_convert_type(pack_mine, BF16).reshape(1, SMALL_ROWS, 2 * lh)
    win_gather = _SplitGather("gather_w_in", [(w_in[0], 1), (w_pool[0], 1), (pack_bits, 0)], BF16, after=pack_mine)
    wout_gather = _SplitGather("gather_w_out", [(w_out[0], 0)], BF16, after=win_gather.token)
    w1_gather = _SplitGather("gather_w_mlp_in", [(w_mlp_in[0], 1)], BF16, after=wout_gather.token)
    w2_gather = _SplitGather("gather_w_mlp_out", [(w_mlp_out[0], 0)], BF16, after=w1_gather.token)
    xb = _to_bf16("x_bf16", x[0], w2_gather.token)
    win_full, wpool_full, pack_bits_full = win_gather.wait(after=win_gather.relay(after=xb))
    pack_full = lax.bitcast_convert_type(pack_bits_full.reshape(N_DEV, SMALL_ROWS, lh, 2), F32)
    wcat = jnp.concatenate([w_rg_a[0, 0], w_rg_i[0, 0], w_rg_a[0, 1], w_rg_i[0, 1]], axis=-1).astype(BF16)
    vec = lambda i, j, k: (0, 0)
    row_full = lambda i, j, k: (i, 0)

    def after(token):
        return (token, _sp((8, 128), vec))

    def sds(shape, dtype):
        return jax.ShapeDtypeStruct(shape, dtype)

    def plain_epi(acc, i, ex, out):
        out[0][...] = acc

    def bf16_epi(acc, i, ex, out):
        out[0][...] = acc.astype(BF16)

    t = _tiles(seq, d_model, d_ff)

    (p3,) = _matmul(
        "proj", xb, win_full, _sp((t.rows, d_model), lambda i, j, k: (i, 0)), _sp((d_model, dh), lambda i, j, k: (0, j)),
        grid=(seq // t.rows, 3, 1),
        out_shape=[sds((3, seq, dh), F32)], out_specs=[_sp((None, t.rows, dh), lambda i, j, k: (j, i, 0))],
        epilogue=plain_epi)

    d_pool, y_half = _pool_fwd(p3, wpool_full, pool_scale, seq, d_model)
    y, h0p, h1p = _lru_fwd(p3, y_half, pack_full, conv_b, wcat, wout_gather.relay(after=y_half), seq, d_model)
    (wout_full,) = wout_gather.wait(after=y)
    relay_token = w1_gather.relay(after=wout_full)

    mix_rows = 2 * t.ln_rows

    def mix_epi(acc, i, ex, out):
        x_ref, g_ref, b_ref = ex[:3]
        for part in range(2):
            rows = pl.ds(part * t.ln_rows, t.ln_rows)
            z = ALPHA * x_ref[rows, :] + acc[part * t.ln_rows:(part + 1) * t.ln_rows, :]
            x1, _, _ = _ln_fwd(z, g_ref[...], b_ref[...])
            out[0][rows, :] = z
            out[1][rows, :] = x1
            out[2][rows, :] = x1.astype(BF16)

    z1, x1, x1b = _matmul(
        "mix_out", y, wout_full, _sp((mix_rows, d_model), row_full), _sp((d_model, d_model), vec, single=True),
        grid=(seq // mix_rows, 1, 1),
        extras=[(xs, _sp((mix_rows, d_model), row_full)), (ln_mix_g, _sp((1, d_model), vec)),
                (ln_mix_b, _sp((1, d_model), vec)), after(relay_token)],
        out_shape=[sds((seq, d_model), F32), sds((seq, d_model), F32), sds((seq, d_model), BF16)],
        out_specs=[_sp((mix_rows, d_model), row_full)] * 3, epilogue=mix_epi)
    (w1_full,) = w1_gather.wait(after=x1b)

    def mlp_in_epi(acc, i, ex, out, cols):
        h = jnp.maximum(acc, 0.0)
        out[0][:, cols] = (h * h).astype(BF16)
        out[1][:, cols] = (2.0 * h).astype(BF16)

    hmid, dact = _matmul(
        "mlp_in", x1b, w1_full, _sp((t.rows, d_model), lambda i, j, k: (i, 0)),
        _sp((d_model, t.ff_cols), lambda i, j, k: (0, j)),
        grid=(seq // t.rows, d_ff // t.ff_cols, 1), j_outer=True,
        out_shape=[sds((seq, d_ff), BF16)] * 2, out_specs=[_sp((t.rows, t.ff_cols), lambda i, j, k: (i, j))] * 2,
        epilogue=mlp_in_epi, n_split=t.ff_split)
    (w2_full,) = w2_gather.wait(after=w2_gather.relay(after=hmid))

    (ffn,) = _matmul(
        "mlp_out", hmid, w2_full, _sp((t.rows, t.ff_k), lambda i, j, k: (i, k)),
        _sp((t.ff_k, d_model), lambda i, j, k: (k, 0)),
        grid=(seq // t.rows, 1, d_ff // t.ff_k),
        out_shape=[sds((seq, d_model), F32)], out_specs=[_sp((t.rows, d_model), row_full)])
    dz2, dz2b, g_ffn_g, g_ffn_b, loss_part = _ln_loss_bwd(ffn, x1, tgt, ln_ffn_g, ln_ffn_b, t.ln_rows)

    (g_w2,) = _matmul(
        "grad_w_mlp_out", hmid, dz2b, _sp((seq, t.grad_rows), lambda i, j, k: (0, i)),
        _sp((seq, d_model), vec, single=True),
        grid=(d_ff // t.grad_rows, 1, 1), ta=True,
        out_shape=[sds((d_ff, d_model), BF16)], out_specs=[_sp((t.grad_rows, d_model), row_full)],
        epilogue=bf16_epi)
    scatter_w2 = _SplitReduceScatter("scatter_w_mlp_out", [g_w2.reshape(N_DEV, d_ff // N_DEV, d_model)])

    def dpre_epi(acc, i, ex, out, cols):
        out[0][:, cols] = (acc * ex[0][:, cols].astype(F32)).astype(BF16)

    (dpre,) = _matmul(
        "mlp_dpre", dz2b, w2_full, _sp((t.rows, d_model), lambda i, j, k: (i, 0)),
        _sp((t.ff_cols, d_model), lambda i, j, k: (j, 0)),
        grid=(seq // t.rows, d_ff // t.ff_cols, 1), j_outer=True, tb=True,
        extras=[(dact, _sp((t.rows, t.ff_cols), lambda i, j, k: (i, j))), after(scatter_w2.token)],
        out_shape=[sds((seq, d_ff), BF16)], out_specs=[_sp((t.rows, t.ff_cols), lambda i, j, k: (i, j))],
        epilogue=dpre_epi, n_split=t.ff_split)
    token_w2 = scatter_w2.combine_and_send(after=dpre)

    (dx1_mlp,) = _matmul(
        "mlp_dx", dpre, w1_full, _sp((t.rows, t.ff_k), lambda i, j, k: (i, k)),
        _sp((d_model, t.ff_k), lambda i, j, k: (0, k)),
        grid=(seq // t.rows, 1, d_ff // t.ff_k), tb=True, extras=[after(token_w2)],
        out_shape=[sds((seq, d_model), F32)], out_specs=[_sp((t.rows, d_model), row_full)])
    dz1, dz1b, g_mix_g, g_mix_b = _ln_bwd_rows(dx1_mlp, dz2, z1, ln_mix_g, ln_mix_b, t.ln_rows)

    def block_epi(acc, i, ex, out):
        out[0][0] = acc.astype(BF16)

    fs = d_ff // N_DEV
    (g_w1,) = _matmul(
        "grad_w_mlp_in", x1b, dpre, _sp((seq, t.grad_rows), lambda i, j, k: (0, i)),
        _sp((seq, fs), lambda i, j, k: (0, j)),
        grid=(d_model // t.grad_rows, N_DEV, 1), j_outer=True, ta=True,
        out_shape=[sds((N_DEV, d_model, fs), BF16)],
        out_specs=[_sp((1, t.grad_rows, fs), lambda i, j, k: (j, i, 0))], epilogue=block_epi)

    (dy,) = _matmul(
        "mix_dy", dz1b, wout_full, _sp((t.rows, d_model), lambda i, j, k: (i, 0)),
        _sp((dh, d_model), lambda i, j, k: (j, 0)),
        grid=(seq // t.rows, 2, 1), j_outer=True, tb=True,
        out_shape=[sds((seq, d_model), F32)], out_specs=[_sp((t.rows, dh), lambda i, j, k: (i, j))],
        epilogue=plain_epi)
    (g_wout,) = _matmul(
        "grad_w_out", y, dz1b, _sp((seq, t.grad_rows), lambda i, j, k: (0, i)), _sp((seq, d_model), vec, single=True),
        grid=(d_model // t.grad_rows, 1, 1), ta=True,
        out_shape=[sds((d_model, d_model), BF16)], out_specs=[_sp((t.grad_rows, d_model), row_full)],
        epilogue=bf16_epi)
    scatter_w1 = _SplitReduceScatter("scatter_w_mlp_in", [g_w1, g_wout.reshape(N_DEV, d_model // N_DEV, d_model)])

    dproj_pool, g_wpool, g_pscale = _pool_bwd(d_pool, dy, wpool_full, pool_scale, scatter_w1.token, seq, d_model)
    token_w1 = scatter_w1.combine_and_send(after=dproj_pool)
    dproj, g_pack, g_convb, g_wcat = _lru_bwd(p3, dy, h0p, h1p, dproj_pool, pack_full, conv_b, wcat,
                                              token_w1, seq, d_model)
    g_wa = jnp.stack([g_wcat[:, :, 0:lh], g_wcat[:, :, 2 * lh:3 * lh]])
    g_wi = jnp.stack([g_wcat[:, :, lh:2 * lh], g_wcat[:, :, 3 * lh:4 * lh]])

    rep_parts = [_rows128(g_wa), _rows128(g_wi), _rows128(g_mix_g), _rows128(g_mix_b), _rows128(g_ffn_g),
                 _rows128(g_ffn_b), _rows128(g_pscale), _rows128(g_convb)]
    rep_rows = [p.shape[0] for p in rep_parts]
    n_rep = sum(rep_rows)
    small = jnp.concatenate(rep_parts + [_rows128(g_pack), loss_part], axis=0)
    small_gather = _SplitGather("gather_small_grads", [(small[None], 0)], F32, after=small)

    ws = 3 * dh // N_DEV

    def pair_epi(acc, i, ex, out):
        out[0][0] = acc[:, :ws].astype(BF16)
        out[0][1] = acc[:, ws:].astype(BF16)

    (g_win,) = _matmul(
        "grad_w_in", xb, dproj, _sp((seq, t.grad_rows), lambda i, j, k: (0, i)),
        _sp((seq, 2 * ws), lambda i, j, k: (0, j)),
        grid=(d_model // t.grad_rows, N_DEV // 2, 1), ta=True, extras=[after(small_gather.token)],
        out_shape=[sds((N_DEV, d_model, ws), BF16)],
        out_specs=[_sp((2, t.grad_rows, ws), lambda i, j, k: (j, i, 0))], epilogue=pair_epi)
    scatter_mix = _SplitReduceScatter(
        "scatter_mixer", [g_win, g_wpool.reshape(N_DEV, pg // N_DEV * len(POOL_WINDOWS), pg)])

    def adam_big(name, own_landed, w, m, v):
        own, landed = own_landed
        shp = w.shape
        two = lambda a: a.reshape(-1, shp[-1])
        res = _sum_adamw(name, own, landed, two(w), two(m), two(v))
        return [r.reshape(shp) for r in res]

    (r_w2,) = scatter_w2.wait(after=scatter_mix.token)
    o_w2 = adam_big("adam_w_mlp_out", r_w2, w_mlp_out, m_w_mlp_out, v_w_mlp_out)
    token_mix = scatter_mix.combine_and_send(after=o_w2[0])

    def dx_epi(acc, i, ex, out):
        out[0][...] = ALPHA * ex[0][...] + acc

    (dx,) = _matmul(
        "grad_x", dproj, win_full, _sp((t.ln_rows * 2, 3 * dh), lambda i, j, k: (i, 0)),
        _sp((d_model, 3 * dh), vec, single=True),
        grid=(seq // (t.ln_rows * 2), 1, 1), tb=True,
        extras=[(dz1, _sp((t.ln_rows * 2, d_model), row_full)), after(token_mix)],
        out_shape=[sds((seq, d_model), F32)], out_specs=[_sp((t.ln_rows * 2, d_model), row_full)],
        epilogue=dx_epi)
    r_w1, r_wout = scatter_w1.wait(after=dx)
    o_w1 = adam_big("adam_w_mlp_in", r_w1, w_mlp_in, m_w_mlp_in, v_w_mlp_in)
    o_wout = adam_big("adam_w_out", r_wout, w_out, m_w_out, v_w_out)
    r_win, r_wpool = scatter_mix.wait(after=o_wout[0])
    o_win = adam_big("adam_w_in", r_win, w_in, m_w_in, v_w_in)
    o_wpool = adam_big("adam_w_pool", r_wpool, w_pool, m_w_pool, v_w_pool)

    small_gather.relay(after=o_win[0])
    (small_all,) = small_gather.wait(after=o_wpool[0])

    rep_w = [w_rg_a, w_rg_i, ln_mix_g, ln_mix_b, ln_ffn_g, ln_ffn_b, pool_scale, conv_b]
    rep_m = [m_w_rg_a, m_w_rg_i, m_ln_mix_g, m_ln_mix_b, m_ln_ffn_g, m_ln_ffn_b, m_pool_scale, m_conv_b]
    rep_v = [v_w_rg_a, v_w_rg_i, v_ln_mix_g, v_ln_mix_b, v_ln_ffn_g, v_ln_ffn_b, v_pool_scale, v_conv_b]
    cat = lambda arrs: jnp.concatenate([_rows128(a) for a in arrs], axis=0)
    o_rep = _sum_adamw("adam_replicated", None, small_all, cat(rep_w), cat(rep_m), cat(rep_v))

    my_idx = _dev_index(_where_am_i())
    head_parts = lax.dynamic_slice_in_dim(small_all, n_rep + my_idx * SMALL_ROWS, SMALL_ROWS, axis=1)
    o_head = _sum_adamw("adam_head", None, head_parts, pack_mine,
                        small_pack(m_conv_w, m_b_rg_a, m_b_rg_i, m_rg_lambda),
                        small_pack(v_conv_w, v_b_rg_a, v_b_rg_i, v_rg_lambda))

    def unpack_rep(packed):
        out, r = [], 0
        for wgt, rows in zip(rep_w, rep_rows):
            out.append(packed[r:r + rows].reshape(wgt.shape))
            r += rows
        return out

    def unpack_head(packed):
        return [packed[0:4].reshape(conv_w.shape), packed[4:6].reshape(b_rg_a.shape),
                packed[6:8].reshape(b_rg_i.shape), packed[8:10].reshape(rg_lambda.shape)]

    loss = _sum_blocks("loss_sum", small_all[:, n_rep + N_HEADS * SMALL_ROWS:, :])[0, 0]

    outs = [loss, dx[None]]
    for kind in range(4):
        ra, ri, mg, mb, fg, fb, ps, cb = unpack_rep(o_rep[kind])
        cw, ba, bi, lam = unpack_head(o_head[kind])
        outs += [mg, mb, o_win[kind], o_wpool[kind], ps, cw, cb, ra, ba, ri, bi, lam, o_wout[kind], fg, fb,
                 o_w1[kind], o_w2[kind]]
    return tuple(outs)
```

```python
import functools

import jax
import jax.numpy as jnp
from jax import lax
from jax.experimental import pallas as pl
from jax.experimental.pallas import tpu as pltpu

F32 = jnp.float32
BF16 = jnp.bfloat16
MESH = pl.DeviceIdType.MESH
ANY = pl.BlockSpec(memory_space=pl.ANY)

N_DEV = 8
POOL_WINDOWS = (2, 4, 8, 16)
N_HEADS = 8
RG_C = 8.0
LN_EPS = 1e-5
ALPHA = 2.0 ** 0.25
ADAM_LR = 0.001
ADAM_B1 = 0.9
ADAM_B2 = 0.999
ADAM_EPS = 1e-08
ADAM_WD = 0.01
ADAM_STEP = 10

VMEM_LIMIT = 56 * 1024 * 1024
WIN_HALO = 16
CONV_HALO = 8
SMALL_ROWS = 16


def _params(n_grid):
    return pltpu.CompilerParams(dimension_semantics=("arbitrary",) * n_grid, vmem_limit_bytes=VMEM_LIMIT)


def _shift(v, j):
    n = v.shape[0]
    s = (-j) % n
    return v if s == 0 else pltpu.roll(v, s, 0)


def _sigmoid(x):
    return 0.5 * jnp.tanh(0.5 * x) + 0.5


def _softplus(z):
    e = jnp.exp(-jnp.abs(z))
    u = 1.0 + e
    log1p = jnp.where(u == 1.0, e, jnp.log(u) * (e / jnp.where(u == 1.0, 1.0, u - 1.0)))
    return jnp.maximum(z, 0.0) + log1p


_GELU_C = 0.7978845608028654
_GELU_K = 0.044715


def _gelu_and_grad(x):
    x2 = x * x
    t = jnp.tanh(_GELU_C * (x + _GELU_K * x * x2))
    g = 0.5 * x * (1.0 + t)
    dg = 0.5 * (1.0 + t) + 0.5 * x * (1.0 - t * t) * (_GELU_C * (1.0 + 3.0 * _GELU_K * x2))
    return g, dg


def _ln_fwd(z, g, b):
    mu = jnp.mean(z, axis=-1, keepdims=True)
    zc = z - mu
    var = jnp.mean(zc * zc, axis=-1, keepdims=True)
    rstd = lax.rsqrt(var + LN_EPS)
    xhat = zc * rstd
    return xhat * g + b, xhat, rstd


def _ln_bwd(dy, xhat, rstd, g):
    dxhat = dy * g
    m1 = jnp.mean(dxhat, axis=-1, keepdims=True)
    m2 = jnp.mean(dxhat * xhat, axis=-1, keepdims=True)
    dz = rstd * (dxhat - m1 - xhat * m2)
    dg = jnp.sum(dy * xhat, axis=0, keepdims=True)
    db = jnp.sum(dy, axis=0, keepdims=True)
    return dz, dg, db


def _acc_rows(ref, first, val):
    @pl.when(first)
    def _():
        ref[...] = val

    @pl.when(jnp.logical_not(first))
    def _():
        ref[...] += val


def _sp(shape, fn, single=False):
    return shape, fn, single


def _matmul(name, a, b, a_spec, b_spec, *, grid, j_outer=False, ta=False, tb=False, extras=(), out_shape, out_specs,
            epilogue=None, n_split=1, side=None):
    ni, nj, nk = grid
    n_ex = len(extras)
    dims = (((0 if ta else 1,), (1 if tb else 0,)), ((), ()))
    side_in, side_shape, side_out, side_fn = side if side is not None else ((), (), (), None)
    n_main_out = len(out_shape)
    inner = ni if j_outer else nj

    def mk(spec):
        shape, fn, single = spec
        index = (lambda g0, g1, g2: fn(g1, g0, g2)) if j_outer else fn
        return pl.BlockSpec(shape, index, pipeline_mode=pl.Buffered(1)) if single else pl.BlockSpec(shape, index)

    def mk_side(block, fn):
        return pl.BlockSpec(block, lambda g0, g1, g2: fn(g0 * inner + g1))

    def body(a_ref, b_ref, *rest):
        ex_refs = rest[:n_ex]
        out_refs = rest[n_ex + len(side_in):n_ex + len(side_in) + n_main_out]
        if side_fn is not None:
            side_fn(pl.program_id(0) * inner + pl.program_id(1), rest[n_ex:n_ex + len(side_in)],
                    rest[n_ex + len(side_in) + n_main_out:])
        i = pl.program_id(1 if j_outer else 0)
        if n_split > 1:
            av = a_ref[...].astype(BF16)
            width = b_ref.shape[0 if tb else 1] // n_split
            for c in range(n_split):
                cols = pl.ds(c * width, width)
                bv = (b_ref[cols, :] if tb else b_ref[:, cols]).astype(BF16)
                epilogue(lax.dot_general(av, bv, dims, preferred_element_type=F32), i, ex_refs, out_refs, cols)
            return
        part = lax.dot_general(a_ref[...].astype(BF16), b_ref[...].astype(BF16), dims, preferred_element_type=F32)
        if nk == 1:
            epilogue(part, i, ex_refs, out_refs)
        else:
            @pl.when(pl.program_id(2) == 0)
            def _():
                out_refs[0][...] = part

            @pl.when(pl.program_id(2) > 0)
            def _():
                out_refs[0][...] += part

    return pl.pallas_call(
        body, name=name, grid=(nj, ni, nk) if j_outer else (ni, nj, nk),
        in_specs=[mk(a_spec), mk(b_spec)] + [mk(s) for _, s in extras] + [mk_side(blk, fn) for _, blk, fn in side_in],
        out_specs=[mk(s) for s in out_specs] + [mk_side(blk, fn) for blk, fn in side_out],
        out_shape=list(out_shape) + list(side_shape),
        compiler_params=_params(3),
    )(a, b, *[x for x, _ in extras], *[x for x, _, _ in side_in])


def _bs(shape, fn):
    return pl.BlockSpec(shape, fn)


def _where_am_i():
    x, y, c = lax.axis_index("x"), lax.axis_index("y"), lax.axis_index("c")
    return x, y, c


def _dev_index(p):
    return 4 * p[0] + 2 * p[1] + p[2]


def _slab(ref, axis, idx, size):
    sl = [slice(None)] * len(ref.shape)
    sl[axis] = pl.ds(idx * size, size)
    return ref.at[tuple(sl)]


def _all_gather(name, items):
    n = len(items)
    shapes = []
    for shard, axis in items:
        s = list(shard.shape)
        s[axis] *= N_DEV
        shapes.append(jax.ShapeDtypeStruct(tuple(s), shard.dtype))

    def body(*refs):
        in_refs, out_refs = refs[:n], refs[n:2 * n]
        send_sems, recv_sems, local_sems = refs[2 * n:]
        x, y, c = _where_am_i()
        me, sibling = (x, y, c), (x, y, 1 - c)
        chips = [(1 - x, y), (x, 1 - y), (1 - x, 1 - y)]

        def blk(a, p):
            axis = items[a][1]
            return _slab(out_refs[a], axis, _dev_index(p), items[a][0].shape[axis])

        def copy(a, k, block, to, src=None):
            return pltpu.make_async_remote_copy(
                src_ref=blk(a, block) if src is None else src, dst_ref=blk(a, block),
                send_sem=send_sems.at[a, k], recv_sem=recv_sems.at[a, k], device_id=to, device_id_type=MESH)

        mine = [pltpu.make_async_copy(in_refs[a], blk(a, me), local_sems.at[a]) for a in range(n)]
        for cp in mine:
            cp.start()
        first = []
        for a in range(n):
            first.append(copy(a, 0, me, sibling, src=in_refs[a]))
            first += [copy(a, 1 + j, me, (*chip, c), src=in_refs[a]) for j, chip in enumerate(chips)]
        for cp in first:
            cp.start()
        passed = []
        for a in range(n):
            for j, chip in enumerate(chips):
                copy(a, 1 + j, (*chip, c), me).wait_recv()
                fw = copy(a, 4 + j, (*chip, c), sibling)
                fw.start()
                passed.append(fw)
        for a in range(n):
            copy(a, 0, sibling, me).wait_recv()
            for j, chip in enumerate(chips):
                copy(a, 4 + j, (*chip, 1 - c), me).wait_recv()
        for cp in first + passed:
            cp.wait_send()
        for cp in mine:
            cp.wait()

    outs = pl.pallas_call(
        body, name=name, out_shape=shapes, in_specs=[ANY] * n, out_specs=[ANY] * n,
        scratch_shapes=[pltpu.SemaphoreType.DMA((n, 7)), pltpu.SemaphoreType.DMA((n, 7)),
                        pltpu.SemaphoreType.DMA((n,))],
    )(*[s for s, _ in items])
    return list(outs)


HBM = pl.BlockSpec(memory_space=pltpu.HBM)
SEM = pl.BlockSpec(memory_space=pltpu.SEMAPHORE)
DATAFLOW = pltpu.SideEffectType.DATAFLOW_SIDE_EFFECTING


def _in_hbm(a):
    return pltpu.with_memory_space_constraint(a, pltpu.HBM)


def _token_shape():
    return jax.ShapeDtypeStruct((8, 128), F32)


def _split_start(name, n_sems, bufs, issue):
    nb = len(bufs)

    def body(*refs):
        issue(refs[:nb], refs[nb], refs[nb + 1])
        refs[-1][...] = jnp.zeros((8, 128), F32)

    outs = pl.pallas_call(
        body, name=name,
        out_shape=(pltpu.SemaphoreType.DMA((n_sems,)), pltpu.SemaphoreType.DMA((n_sems,)),
                   *[pltpu.HBM(b.shape, b.dtype) for b in bufs], _token_shape()),
        in_specs=[HBM] * nb, out_specs=(SEM, SEM, *[HBM] * nb, pl.BlockSpec(memory_space=pltpu.VMEM)),
        input_output_aliases={i: 2 + i for i in range(nb)},
        compiler_params=pltpu.CompilerParams(has_side_effects=DATAFLOW),
    )(*[_in_hbm(b) for b in bufs])
    return outs[0], outs[1], list(outs[2:2 + nb]), outs[-1]


def _split_relay(name, n_sems, sems, bufs, after, relay):
    nb = len(bufs)

    def body(*refs):
        relay(refs[:nb], refs[nb], refs[nb + 1], refs[nb + 3], refs[nb + 4])
        refs[-1][...] = jnp.zeros((8, 128), F32)

    outs = pl.pallas_call(
        body, name=name,
        out_shape=(pltpu.SemaphoreType.DMA((n_sems,)), pltpu.SemaphoreType.DMA((n_sems,)),
                   *[pltpu.HBM(b.shape, b.dtype) for b in bufs], _token_shape()),
        in_specs=[HBM] * nb + [SEM, SEM, ANY],
        out_specs=(SEM, SEM, *[HBM] * nb, pl.BlockSpec(memory_space=pltpu.VMEM)),
        input_output_aliases={i: 2 + i for i in range(nb)},
        compiler_params=pltpu.CompilerParams(has_side_effects=DATAFLOW),
    )(*bufs, sems[0], sems[1], after)
    return outs[0], outs[1], list(outs[2:2 + nb]), outs[-1]


def _split_wait(name, sems, bufs, after, finish):
    nb = len(bufs)

    def body(*refs):
        finish(refs[:nb], refs[nb], refs[nb + 1])

    outs = pl.pallas_call(
        body, name=name, out_shape=[pltpu.HBM(b.shape, b.dtype) for b in bufs],
        in_specs=[HBM] * nb + [SEM, SEM, ANY], out_specs=[HBM] * nb,
        input_output_aliases={i: i for i in range(nb)},
        compiler_params=pltpu.CompilerParams(has_side_effects=DATAFLOW),
    )(*bufs, sems[0], sems[1], after)
    return list(outs)


def _place(name, items, dtype, after):
    ids = jnp.reshape(_dev_index(_where_am_i()), (1,)).astype(jnp.int32)
    outs = []
    for a, (shard, axis) in enumerate(items):
        rows, cols = shard.shape[-2], shard.shape[-1]
        tr = rows
        while tr * cols * shard.dtype.itemsize > 4 * 1024 * 1024 and tr % 32 == 0:
            tr //= 2
        nt = rows // tr
        full = list(shard.shape)
        full[axis] *= N_DEV
        if shard.ndim == 2 and axis == 0:
            in_spec = _bs((tr, cols), lambda i, ids: (i, 0))
            out_spec = _bs((tr, cols), lambda i, ids, nt=nt: (ids[0] * nt + i, 0))
        elif shard.ndim == 2 and axis == 1:
            in_spec = _bs((tr, cols), lambda i, ids: (i, 0))
            out_spec = _bs((tr, cols), lambda i, ids: (i, ids[0]))
        elif shard.ndim == 3 and axis == 1:
            tr, nt = rows, shard.shape[0]
            in_spec = _bs((None, rows, cols), lambda i, ids: (i, 0, 0))
            out_spec = _bs((None, rows, cols), lambda i, ids: (i, ids[0], 0))
        else:
            assert shard.ndim == 3 and axis == 0 and shard.shape[0] == 1
            in_spec = _bs((None, tr, cols), lambda i, ids: (0, i, 0))
            out_spec = _bs((None, tr, cols), lambda i, ids: (ids[0], i, 0))

        def body(ids_ref, in_ref, after_ref, out_ref):
            del ids_ref, after_ref
            out_ref[...] = in_ref[...].astype(out_ref.dtype)

        outs.append(pl.pallas_call(
            body, name=f"{name}{a}",
            grid_spec=pltpu.PrefetchScalarGridSpec(
                num_scalar_prefetch=1, grid=(nt,), in_specs=[in_spec, ANY], out_specs=out_spec),
            out_shape=jax.ShapeDtypeStruct(tuple(full), dtype), compiler_params=_params(1),
        )(ids, shard, after))
    return outs


class _SplitGather:
    def __init__(self, name, items, dtype, after):
        self.name, self.items, self.n = name, items, len(items)
        fulls = _place(name + "_place", items, dtype, after)
        n = self.n

        def issue(refs, send, recv):
            me, sibling, chips, c = self._geometry()
            for a in range(n):
                self._copy1(refs, send, recv, a, 0, me, sibling).start()
                for j, chip in enumerate(chips):
                    self._copy1(refs, send, recv, a, 1 + j, me, (*chip, c)).start()

        self.send, self.recv, self.bufs, self.token = _split_start(name + "_start", 4 * n, fulls, issue)

    @staticmethod
    def _geometry():
        x, y, c = _where_am_i()
        return (x, y, c), (x, y, 1 - c), [(1 - x, y), (x, 1 - y), (1 - x, 1 - y)], c

    def _blk(self, refs, a, p):
        shard, axis = self.items[a]
        return _slab(refs[a], axis, _dev_index(p), shard.shape[axis])

    def _copy1(self, refs, send, recv, a, k, owner, to):
        return pltpu.make_async_remote_copy(
            src_ref=self._blk(refs, a, owner), dst_ref=self._blk(refs, a, owner), send_sem=send.at[4 * a + k],
            recv_sem=recv.at[4 * a + k], device_id=to, device_id_type=MESH)

    def _copy2(self, refs, send, recv, a, j, owner, to):
        return pltpu.make_async_remote_copy(
            src_ref=self._blk(refs, a, owner), dst_ref=self._blk(refs, a, owner), send_sem=send.at[3 * a + j],
            recv_sem=recv.at[3 * a + j], device_id=to, device_id_type=MESH)

    def relay(self, after):
        n = self.n

        def relay(refs, send_in, recv_in, send_out, recv_out):
            me, sibling, chips, c = self._geometry()
            for a in range(n):
                for j, chip in enumerate(chips):
                    self._copy1(refs, send_in, recv_in, a, 1 + j, (*chip, c), me).wait_recv()
                    self._copy2(refs, send_out, recv_out, a, j, (*chip, c), sibling).start()
            for a in range(n):
                self._copy1(refs, send_in, recv_in, a, 0, sibling, me).wait_recv()
                for k in range(4):
                    self._copy1(refs, send_in, recv_in, a, k, me, sibling).wait_send()

        self.send, self.recv, self.bufs, self.token = _split_relay(
            self.name + "_relay", 3 * n, (self.send, self.recv), self.bufs, after, relay)
        return self.token

    def wait(self, after):
        n = self.n

        def finish(refs, send, recv):
            me, sibling, chips, c = self._geometry()
            for a in range(n):
                for j, chip in enumerate(chips):
                    self._copy2(refs, send, recv, a, j, (*chip, 1 - c), me).wait_recv()
                    self._copy2(refs, send, recv, a, j, (*chip, c), sibling).wait_send()

        return _split_wait(self.name + "_wait", (self.send, self.recv), self.bufs, after, finish)


class _SplitReduceScatter:
    def __init__(self, name, grads):
        self.name, self.n = name, len(grads)
        n = self.n
        g4 = [g.reshape(4, 2, *g.shape[1:]) for g in grads]
        land = [lax.empty((4, 1, *g.shape[1:]), g.dtype) for g in grads]

        def issue(refs, send, recv):
            for a in range(n):
                self._swap(refs, send, recv, a).start()

        self.send, self.recv, self.bufs, self.token = _split_start(name + "_d2d_start", n, g4 + land, issue)

    def _swap(self, refs, send, recv, a):
        x, y, c = _where_am_i()
        return pltpu.make_async_remote_copy(
            src_ref=refs[a].at[:, pl.ds(1 - c, 1)], dst_ref=refs[self.n + a], send_sem=send.at[a], recv_sem=recv.at[a],
            device_id=(x, y, 1 - c), device_id_type=MESH)

    def _hop(self, refs, send, recv, a, m):
        x, y, c = _where_am_i()
        px = (1 - x) if m & 2 else x
        py = (1 - y) if m & 1 else y
        return pltpu.make_async_remote_copy(
            src_ref=refs[a].at[2 * px + py], dst_ref=refs[self.n + a].at[m - 1], send_sem=send.at[3 * a + m - 1],
            recv_sem=recv.at[3 * a + m - 1], device_id=(px, py, c), device_id_type=MESH)

    def combine_and_send(self, after):
        n = self.n

        def finish(refs, send, recv):
            for a in range(n):
                self._swap(refs, send, recv, a).wait()

        bufs = _split_wait(self.name + "_d2d_wait", (self.send, self.recv), self.bufs, after, finish)
        x, y, c = _where_am_i()
        ids = jnp.stack([c, 2 * x + y]).astype(jnp.int32)
        self.own, sums = [], []
        for a in range(n):
            own, hb = _pair_sum(f"{self.name}_sum{a}", bufs[a], bufs[n + a], ids)
            self.own.append(own)
            sums.append(hb)
        land = [lax.empty((3, *h.shape[1:]), h.dtype) for h in sums]

        def issue(refs, send, recv):
            for a in range(n):
                for m in (1, 2, 3):
                    self._hop(refs, send, recv, a, m).start()

        self.send, self.recv, self.bufs, self.token = _split_start(self.name + "_ici_start", 3 * n, sums + land, issue)
        return self.token

    def wait(self, after):
        n = self.n

        def finish(refs, send, recv):
            for a in range(n):
                for m in (1, 2, 3):
                    self._hop(refs, send, recv, a, m).wait()

        bufs = _split_wait(self.name + "_ici_wait", (self.send, self.recv), self.bufs, after, finish)
        return list(zip(self.own, bufs[n:]))


def _pair_sum(name, g4, land, ids):
    rows, cols = g4.shape[2], g4.shape[3]
    tr = rows
    while tr * cols * 2 > 1024 * 1024 and tr % 32 == 0:
        tr //= 2

    def body(ids_ref, g_ref, l_ref, own_ref, sum_ref):
        h = g_ref[...].astype(F32) + l_ref[...].astype(F32)
        sum_ref[...] = h.astype(sum_ref.dtype)

        @pl.when(pl.program_id(1) == ids_ref[1])
        def _():
            own_ref[...] = h

    return pl.pallas_call(
        body, name=name,
        grid_spec=pltpu.PrefetchScalarGridSpec(
            num_scalar_prefetch=1, grid=(rows // tr, 4),
            in_specs=[_bs((None, None, tr, cols), lambda i, q, ids: (q, ids[0], i, 0)),
                      _bs((None, None, tr, cols), lambda i, q, ids: (q, 0, i, 0))],
            out_specs=[_bs((tr, cols), lambda i, q, ids: (i, 0)), _bs((None, tr, cols), lambda i, q, ids: (q, i, 0))]),
        out_shape=[jax.ShapeDtypeStruct((rows, cols), F32), jax.ShapeDtypeStruct((4, rows, cols), g4.dtype)],
        compiler_params=_params(2),
    )(ids, g4, land)


def _win_sum(ext, w, off):
    s = ext + _shift(ext, -1)
    if w >= 4:
        s = _shift(s, -1) + _shift(s, 1)
    if w >= 8:
        s = _shift(s, -2) + _shift(s, 2)
    if w >= 16:
        s = _shift(s, -4) + _shift(s, 4)
    return _shift(s, off) if off else s


def _inv_count(r0, t, w, seq):
    pos = r0 + lax.broadcasted_iota(jnp.int32, (t, 1), 0)
    cnt = jnp.minimum(pos + w // 2, seq) - jnp.maximum(pos - w // 2, 0)
    return 1.0 / cnt.astype(F32)


def _pool_fwd(p3, w_pool, pool_scale, seq, d_model):
    dp = d_model // 2
    pg = dp // len(POOL_WINDOWS)
    t = min(128, seq)
    n_chunks = seq // t
    h = WIN_HALO

    def body(u_ref, w_ref, sc_ref, d_ref, y_ref, pad_ref):
        g = pl.program_id(0)
        zeros = jnp.zeros((h, pg), F32)
        pad_ref[0:h, :] = zeros
        pad_ref[h + seq:h + seq + h, :] = zeros

        def fill(ci, _):
            r0 = pl.multiple_of(ci * t, t)
            pad_ref[pl.ds(h + r0, t), :] = u_ref[pl.ds(r0, t), :]
            return 0

        lax.fori_loop(0, n_chunks, fill, 0)
        wmat = w_ref[...]
        scale = sc_ref[...]
        for gi, w in enumerate(POOL_WINDOWS):
            @pl.when(g == gi)
            def _(w=w):
                def chunk(ci, _):
                    r0 = pl.multiple_of(ci * t, t)
                    ext = pad_ref[pl.ds(r0, t + 2 * h), :]
                    mean = _win_sum(ext, w, 0)[h:h + t, :] * _inv_count(r0, t, w, seq)
                    d = (mean - ext[h:h + t, :]).astype(BF16)
                    d_ref[pl.ds(r0, t), :] = d
                    q = jnp.dot(d, wmat, preferred_element_type=F32)
                    y_ref[pl.ds(r0, t), :] = (q * scale).astype(BF16)
                    return 0

                lax.fori_loop(0, n_chunks, chunk, 0, unroll=2)

    return pl.pallas_call(
        body, name="pool_fwd", grid=(len(POOL_WINDOWS),),
        in_specs=[_bs((None, seq, pg), lambda g: (0, 0, g)), _bs((None, pg, pg), lambda g: (g, 0, 0)),
                  _bs((1, pg), lambda g: (0, g))],
        out_specs=[_bs((seq, pg), lambda g: (0, g)), _bs((seq, pg), lambda g: (0, g))],
        out_shape=[jax.ShapeDtypeStruct((seq, dp), BF16), jax.ShapeDtypeStruct((seq, d_model), BF16)],
        scratch_shapes=[pltpu.VMEM((seq + 2 * h, pg), F32)],
        compiler_params=_params(1),
    )(p3, w_pool, pool_scale)


def _pool_bwd(d, dy, w_pool, pool_scale, token, seq, d_model):
    dp = d_model // 2
    pg = dp // len(POOL_WINDOWS)
    t = min(128, seq)
    n_chunks = seq // t
    h = WIN_HALO
    tn_dims = (((0,), (0,)), ((), ()))
    nt_dims = (((1,), (1,)), ((), ()))

    def body(d_ref, dy_ref, w_ref, sc_ref, tok_ref, du_ref, dwb_ref, dsc_ref, pad_ref, dd_ref, dw_ref):
        del tok_ref
        g = pl.program_id(0)
        zeros = jnp.zeros((h, pg), F32)
        pad_ref[0:h, :] = zeros
        pad_ref[h + seq:h + seq + h, :] = zeros
        wmat = w_ref[...]
        scale = sc_ref[...]
        for gi, w in enumerate(POOL_WINDOWS):
            @pl.when(g == gi)
            def _(w=w):
                dw_ref[...] = jnp.zeros((pg, pg), F32)

                def first(ci, dsc):
                    r0 = pl.multiple_of(ci * t, t)
                    dv = d_ref[pl.ds(r0, t), :]
                    dyv = dy_ref[pl.ds(r0, t), :]
                    q = jnp.dot(dv, wmat, preferred_element_type=F32)
                    dsc = dsc + jnp.sum(dyv * q, axis=0, keepdims=True)
                    dq = (dyv * scale).astype(BF16)
                    dw_ref[...] += lax.dot_general(dv, dq, tn_dims, preferred_element_type=F32)
                    dd = lax.dot_general(dq, wmat, nt_dims, preferred_element_type=F32)
                    dd_ref[pl.ds(r0, t), :] = dd
                    pad_ref[pl.ds(h + r0, t), :] = dd * _inv_count(r0, t, w, seq)
                    return dsc

                def first_pair(cj, dsc):
                    return first(2 * cj + 1, first(2 * cj, dsc))

                dsc_ref[...] = lax.fori_loop(0, n_chunks // 2, first_pair, jnp.zeros((1, pg), F32))
                dwb_ref[...] = dw_ref[...].reshape(N_DEV, pg // N_DEV, pg).astype(BF16)

                def second(ci, _):
                    r0 = pl.multiple_of(ci * t, t)
                    ext = pad_ref[pl.ds(r0, t + 2 * h), :]
                    back = _win_sum(ext, w, 1)[h:h + t, :]
                    du_ref[pl.ds(r0, t), :] = (back - dd_ref[pl.ds(r0, t), :]).astype(BF16)
                    return 0

                lax.fori_loop(0, n_chunks, second, 0, unroll=2)

    return pl.pallas_call(
        body, name="pool_bwd", grid=(len(POOL_WINDOWS),),
        in_specs=[_bs((seq, pg), lambda g: (0, g)), _bs((seq, pg), lambda g: (0, g)),
                  _bs((None, pg, pg), lambda g: (g, 0, 0)), _bs((1, pg), lambda g: (0, g)),
                  _bs((8, 128), lambda g: (0, 0))],
        out_specs=[_bs((seq, pg), lambda g: (0, g)), _bs((N_DEV, None, pg // N_DEV, pg), lambda g: (0, g, 0, 0)),
                   _bs((1, pg), lambda g: (0, g))],
        out_shape=[jax.ShapeDtypeStruct((seq, 3 * dp), BF16),
                   jax.ShapeDtypeStruct((N_DEV, len(POOL_WINDOWS), pg // N_DEV, pg), BF16),
                   jax.ShapeDtypeStruct((1, dp), F32)],
        scratch_shapes=[pltpu.VMEM((seq + 2 * h, pg), F32), pltpu.VMEM((seq, pg), F32), pltpu.VMEM((pg, pg), F32)],
        compiler_params=_params(1),
    )(d, dy, w_pool, pool_scale, token)


def _tile_scan(n_tiles, lanes, loads, stores):
    row = lax.broadcasted_iota(jnp.int32, (8, lanes), 0)
    group = 8

    def local_scan(n, k):
        aa, bb = loads[n](k)
        for sh in (1, 2, 4):
            if n == 0:
                ok = row >= sh
                ap = jnp.where(ok, pltpu.roll(aa, sh, 0), 1.0)
                bp = jnp.where(ok, pltpu.roll(bb, sh, 0), 0.0)
            else:
                ok = row < 8 - sh
                ap = jnp.where(ok, pltpu.roll(aa, 8 - sh, 0), 1.0)
                bp = jnp.where(ok, pltpu.roll(bb, 8 - sh, 0), 0.0)
            bb = aa * bp + bb
            aa = aa * ap
        return aa, bb

    def step(s, carry):
        carry = list(carry)
        for n in range(2):
            tiles = [s * group + u if n == 0 else n_tiles - 1 - (s * group + u) for u in range(group)]
            local = [local_scan(n, k) for k in tiles]
            for k, (aa, bb) in zip(tiles, local):
                hh = bb + aa * carry[n]
                stores[n](k, hh)
                carry[n] = jnp.broadcast_to(hh[7:8, :] if n == 0 else hh[0:1, :], (8, lanes))
        return tuple(carry)

    zeros = jnp.zeros((8, lanes), F32)
    lax.fori_loop(0, n_tiles // group, step, (zeros, zeros))


def _gate_preacts(xc, wcat_ref):
    xcb = xc.astype(BF16)
    return xcb, jnp.dot(xcb, wcat_ref[...], preferred_element_type=F32)


def _gates(pre, n, pk_ref, sp):
    lh = pre.shape[1] // 4
    r = _sigmoid(pre[:, (2 * n) * lh:(2 * n + 1) * lh] + pk_ref[pl.ds(4 + n, 1), :])
    i = _sigmoid(pre[:, (2 * n + 1) * lh:(2 * n + 2) * lh] + pk_ref[pl.ds(6 + n, 1), :])
    log_a = (-RG_C * r) * sp[n]
    a = jnp.exp(log_a)
    x = 2.0 * log_a
    one_minus_a2 = jnp.where(x > -0.01, -(x * (1.0 + x * (0.5 + x * (1.0 / 6.0)))), 1.0 - a * a)
    m = jnp.sqrt(one_minus_a2)
    return r, i, a, m


def _conv_chunk(upad_ref, pk_ref, cb, r0, t):
    ext = upad_ref[pl.ds(r0, t + 2 * CONV_HALO), :]
    acc = pk_ref[pl.ds(1, 1), :] * ext
    for k in (0, 2, 3):
        acc = acc + pk_ref[pl.ds(k, 1), :] * _shift(ext, k - 1)
    return acc[CONV_HALO:CONV_HALO + t, :] + cb, ext


def _lru_fwd(p3, y_in, pack, conv_b, wcat, token, seq, d_model):
    dl = d_model // 2
    lh = dl // N_HEADS
    t = min(128, seq)
    n_chunks = seq // t
    seg = seq // 8
    hal = CONV_HALO
    first_rec_block = (d_model - dl) // lh

    def body(ur_ref, ug_ref, pk_ref, cb_ref, wcat_ref, yin_ref, tok_ref, y_ref, h0_ref, h1_ref,
             upad, a_scr, b_scr):
        del yin_ref, tok_ref
        zeros = jnp.zeros((hal, lh), F32)
        upad[0:hal, :] = zeros
        upad[hal + seq:hal + seq + hal, :] = zeros
        for ref in (h0_ref, h1_ref):
            ref[0:hal, :] = zeros
            ref[hal + seq:hal + seq + hal, :] = zeros

        def fill(ci, _):
            r0 = pl.multiple_of(ci * t, t)
            upad[pl.ds(hal + r0, t), :] = ur_ref[pl.ds(r0, t), :]
            return 0

        lax.fori_loop(0, n_chunks, fill, 0)
        cb = cb_ref[...]
        sp = [_softplus(-pk_ref[pl.ds(8 + n, 1), :]) for n in range(2)]

        def chunk(ci, _):
            r0 = pl.multiple_of(ci * t, t)
            xc, _ext = _conv_chunk(upad, pk_ref, cb, r0, t)
            _, pre = _gate_preacts(xc, wcat_ref)
            for n in range(2):
                _, i, a, m = _gates(pre, n, pk_ref, sp)
                a_scr[n, pl.ds(r0, t), :] = a
                b_scr[n, pl.ds(r0, t), :] = (m * i) * xc
            return 0

        lax.fori_loop(0, n_chunks, chunk, 0, unroll=2)

        def load(n):
            def get(k):
                at = pl.ds(pl.multiple_of(k * 8, 8), 8)
                return a_scr[n, at, :], b_scr[n, at, :]
            return get

        def store(ref):
            def put(k, v):
                ref[pl.ds(pl.multiple_of(hal + k * 8, 8), 8), :] = v
            return put

        _tile_scan(seq // 8, lh, [load(0), load(1)], [store(h0_ref), store(h1_ref)])

        def out(ci, _):
            r0 = pl.multiple_of(ci * t, t)
            hsum = h0_ref[pl.ds(hal + r0, t), :] + h1_ref[pl.ds(hal + r0, t), :]
            gl, _dg = _gelu_and_grad(ug_ref[pl.ds(r0, t), :])
            y_ref[pl.ds(r0, t), :] = (hsum * gl).astype(BF16)
            return 0

        lax.fori_loop(0, n_chunks, out, 0)

    return pl.pallas_call(
        body, name="lru_fwd", grid=(N_HEADS,),
        in_specs=[_bs((None, seq, lh), lambda h: (1, 0, h)), _bs((None, seq, lh), lambda h: (2, 0, h)),
                  _bs((None, SMALL_ROWS, lh), lambda h: (h, 0, 0)), _bs((1, lh), lambda h: (0, h)),
                  _bs((None, lh, 4 * lh), lambda h: (h, 0, 0)),
                  ANY, _bs((8, 128), lambda h: (0, 0))],
        out_specs=[_bs((seq, lh), lambda h: (0, first_rec_block + h)),
                   _bs((seq + 2 * hal, lh), lambda h: (0, h)), _bs((seq + 2 * hal, lh), lambda h: (0, h))],
        out_shape=[jax.ShapeDtypeStruct((seq, d_model), BF16), jax.ShapeDtypeStruct((seq + 2 * hal, dl), F32),
                   jax.ShapeDtypeStruct((seq + 2 * hal, dl), F32)],
        scratch_shapes=[pltpu.VMEM((seq + 2 * hal, lh), F32), pltpu.VMEM((2, seq, lh), F32),
                        pltpu.VMEM((2, seq, lh), F32)],
        input_output_aliases={5: 0},
        compiler_params=_params(1),
    )(p3, p3, pack, conv_b, wcat, y_in, token)


def _lru_bwd(p3, dy, h0p, h1p, dproj_in, pack, conv_b, wcat, token, seq, d_model):
    dl = d_model // 2
    lh = dl // N_HEADS
    t = min(128, seq)
    n_chunks = seq // t
    seg = seq // 8
    hal = CONV_HALO
    first_rec_block = (d_model - dl) // lh
    tn_dims = (((0,), (0,)), ((), ()))
    nt_dims = (((1,), (1,)), ((), ()))

    def body(ur_ref, ug_ref, dy_ref, h0_ref, h1_ref, pk_ref, cb_ref, wcat_ref, tok_ref, din_ref,
             dproj_ref, dpk_ref, dcb_ref, dwcat_ref,
             upad, a_scr, dh_scr, g_scr, dxc_pad, dpr_ref, out_sems, gate_scr):
        del din_ref, tok_ref
        zeros = jnp.zeros((hal, lh), F32)
        for ref in (upad, dxc_pad):
            ref[0:hal, :] = zeros
            ref[hal + seq:hal + seq + hal, :] = zeros
        for n in range(2):
            a_scr[n, 0:hal, :] = zeros
            a_scr[n, hal + seq:hal + seq + hal, :] = zeros

        def fill(ci, _):
            r0 = pl.multiple_of(ci * t, t)
            upad[pl.ds(hal + r0, t), :] = ur_ref[pl.ds(r0, t), :]
            return 0

        lax.fori_loop(0, n_chunks, fill, 0)
        cb = cb_ref[...]
        lam = [pk_ref[pl.ds(8 + n, 1), :] for n in range(2)]
        sp = [_softplus(-lam[n]) for n in range(2)]

        def chunk1(ci, _):
            r0 = pl.multiple_of(ci * t, t)
            xc, _ext = _conv_chunk(upad, pk_ref, cb, r0, t)
            _, pre = _gate_preacts(xc, wcat_ref)
            for n in range(2):
                r, i, a, m = _gates(pre, n, pk_ref, sp)
                a_scr[n, pl.ds(hal + r0, t), :] = a
                for q, v in enumerate((r, i, m)):
                    gate_scr[3 * n + q, pl.ds(r0, t), :] = v
            hsum = h0_ref[pl.ds(hal + r0, t), :] + h1_ref[pl.ds(hal + r0, t), :]
            gl, dgl = _gelu_and_grad(ug_ref[pl.ds(r0, t), :])
            dyv = dy_ref[pl.ds(r0, t), :]
            dh_scr[pl.ds(r0, t), :] = dyv * gl
            dpr_ref[1, pl.ds(r0, t), :] = ((dyv * hsum) * dgl).astype(BF16)
            return 0

        lax.fori_loop(0, n_chunks, chunk1, 0, unroll=2)

        def load(n):
            def get(k):
                r0 = pl.multiple_of(k * 8, 8)
                if n == 0:
                    coef = _shift(a_scr[0, pl.ds(pl.multiple_of(hal + r0, 8), 16), :], 1)[0:8, :]
                else:
                    coef = _shift(a_scr[1, pl.ds(pl.multiple_of(hal + r0 - 8, 8), 16), :], -1)[8:16, :]
                return coef, dh_scr[pl.ds(r0, 8), :]
            return get

        def store(n):
            def put(k, v):
                g_scr[n, pl.ds(pl.multiple_of(k * 8, 8), 8), :] = v
            return put

        _tile_scan(seq // 8, lh, [load(1), load(0)], [store(1), store(0)])

        dwcat_ref[...] = jnp.zeros((lh, 4 * lh), F32)

        def chunk3(ci, carry):
            dba, dbi, dlam, dcb = carry
            r0 = pl.multiple_of(ci * t, t)
            xc, _ext = _conv_chunk(upad, pk_ref, cb, r0, t)
            xcb = xc.astype(BF16)
            dxc = jnp.zeros((t, lh), F32)
            dba, dbi, dlam = list(dba), list(dbi), list(dlam)
            dpre = []
            for n in range(2):
                r, i, m = (gate_scr[3 * n + q, pl.ds(r0, t), :] for q in range(3))
                a = a_scr[n, pl.ds(hal + r0, t), :]
                hext = (h0_ref if n == 0 else h1_ref)[pl.ds(r0, t + 2 * hal), :]
                hprev = _shift(hext, -1 if n == 0 else 1)[hal:hal + t, :]
                gb = g_scr[n, pl.ds(r0, t), :]
                da = gb * hprev
                dm = gb * i * xc
                di = gb * m * xc
                dxc = dxc + gb * (m * i)
                dlog_a = da * a - dm * (a * a) / m
                dr = dlog_a * (-RG_C * sp[n])
                dlam[n] = dlam[n] + jnp.sum(dlog_a * r, axis=0, keepdims=True)
                dpr = dr * r * (1.0 - r)
                dpi = di * i * (1.0 - i)
                dba[n] = dba[n] + jnp.sum(dpr, axis=0, keepdims=True)
                dbi[n] = dbi[n] + jnp.sum(dpi, axis=0, keepdims=True)
                dpre += [dpr.astype(BF16), dpi.astype(BF16)]
            dpre = jnp.concatenate(dpre, axis=1)
            dwcat_ref[...] += lax.dot_general(xcb, dpre, tn_dims, preferred_element_type=F32)
            dxc = dxc + lax.dot_general(dpre, wcat_ref[...], nt_dims, preferred_element_type=F32)
            dxc_pad[pl.ds(hal + r0, t), :] = dxc
            dcb = dcb + jnp.sum(dxc, axis=0, keepdims=True)
            return tuple(dba), tuple(dbi), tuple(dlam), dcb

        zr = jnp.zeros((1, lh), F32)
        def chunk3_pair(cj, carry):
            return chunk3(2 * cj + 1, chunk3(2 * cj, carry))

        dba, dbi, dlam, dcb = lax.fori_loop(0, n_chunks // 2, chunk3_pair, ((zr, zr), (zr, zr), (zr, zr), zr))
        dcb_ref[...] = dcb
        for n in range(2):
            dpk_ref[pl.ds(4 + n, 1), :] = dba[n]
            dpk_ref[pl.ds(6 + n, 1), :] = dbi[n]
            dpk_ref[pl.ds(8 + n, 1), :] = dlam[n] * (RG_C * jax.nn.sigmoid(-lam[n]))
        dpk_ref[pl.ds(10, SMALL_ROWS - 10), :] = jnp.zeros((SMALL_ROWS - 10, lh), F32)

        def chunk4(ci, dtap):
            r0 = pl.multiple_of(ci * t, t)
            gext = dxc_pad[pl.ds(r0, t + 2 * hal), :]
            uext = upad[pl.ds(r0, t + 2 * hal), :]
            gmid = gext[hal:hal + t, :]
            du = pk_ref[pl.ds(1, 1), :] * gext
            for k in (0, 2, 3):
                du = du + pk_ref[pl.ds(k, 1), :] * _shift(gext, 1 - k)
            dpr_ref[0, pl.ds(r0, t), :] = du[hal:hal + t, :].astype(BF16)
            out = []
            for k in range(4):
                usl = _shift(uext, k - 1)[hal:hal + t, :]
                out.append(dtap[k] + jnp.sum(gmid * usl, axis=0, keepdims=True))
            return tuple(out)

        dtap = lax.fori_loop(0, n_chunks, chunk4, (zr, zr, zr, zr))
        for k in range(4):
            dpk_ref[pl.ds(k, 1), :] = dtap[k]

        head = pl.program_id(0)
        outs = [pltpu.make_async_copy(
            dpr_ref.at[b], dproj_ref.at[:, pl.ds(pl.multiple_of((1 + b) * dl + head * lh, lh), lh)], out_sems.at[b])
            for b in range(2)]
        for cp in outs:
            cp.start()
        for cp in outs:
            cp.wait()

    return pl.pallas_call(
        body, name="lru_bwd", grid=(N_HEADS,),
        in_specs=[_bs((None, seq, lh), lambda h: (1, 0, h)), _bs((None, seq, lh), lambda h: (2, 0, h)),
                  _bs((seq, lh), lambda h: (0, first_rec_block + h)),
                  _bs((seq + 2 * hal, lh), lambda h: (0, h)), _bs((seq + 2 * hal, lh), lambda h: (0, h)),
                  _bs((None, SMALL_ROWS, lh), lambda h: (h, 0, 0)), _bs((1, lh), lambda h: (0, h)),
                  _bs((None, lh, 4 * lh), lambda h: (h, 0, 0)),
                  _bs((8, 128), lambda h: (0, 0)), ANY],
        out_specs=[ANY, _bs((None, SMALL_ROWS, lh), lambda h: (h, 0, 0)),
                   _bs((1, lh), lambda h: (0, h)), _bs((None, lh, 4 * lh), lambda h: (h, 0, 0))],
        out_shape=[jax.ShapeDtypeStruct((seq, 3 * dl), BF16), jax.ShapeDtypeStruct((N_HEADS, SMALL_ROWS, lh), F32),
                   jax.ShapeDtypeStruct((1, dl), F32), jax.ShapeDtypeStruct((N_HEADS, lh, 4 * lh), F32)],
        scratch_shapes=[pltpu.VMEM((seq + 2 * hal, lh), F32), pltpu.VMEM((2, seq + 2 * hal, lh), F32),
                        pltpu.VMEM((seq, lh), F32), pltpu.VMEM((2, seq, lh), F32),
                        pltpu.VMEM((seq + 2 * hal, lh), F32), pltpu.VMEM((2, seq, lh), BF16),
                        pltpu.SemaphoreType.DMA((2,)), pltpu.VMEM((6, seq, lh), F32)],
        input_output_aliases={9: 0},
        compiler_params=_params(1),
    )(p3, p3, dy, h0p, h1p, pack, conv_b, wcat, token, dproj_in)


class _tiles:
    def __init__(self, seq, d_model, d_ff):
        self.rows = min(1024, seq)
        self.ln_rows = min(256, seq)
        self.ff_cols = min(1024, d_ff)
        self.ff_split = 4
        self.ff_k = min(2048, d_ff)
        self.grad_rows = 512


def _ln_loss_bwd(ffn, x1, tgt, g, b, tr):
    seq, d = ffn.shape

    def body(f_ref, x_ref, t_ref, g_ref, b_ref, dz_ref, dzb_ref, dg_ref, db_ref, loss_ref):
        i = pl.program_id(0)
        gv = g_ref[...]
        z = ALPHA * x_ref[...] + f_ref[...]
        y, xhat, rstd = _ln_fwd(z, gv, b_ref[...])
        err = y - t_ref[...]
        part = 0.5 * jnp.sum(jnp.mean(err * err, axis=-1, keepdims=True), axis=0, keepdims=True)
        dz, dg, db = _ln_bwd(err * (1.0 / d), xhat, rstd, gv)
        dz_ref[...] = dz
        dzb_ref[...] = dz.astype(BF16)
        _acc_rows(dg_ref, i == 0, dg)
        _acc_rows(db_ref, i == 0, db)
        _acc_rows(loss_ref, i == 0, jnp.broadcast_to(part, (8, 128)))

    row = _bs((tr, d), lambda i: (i, 0))
    vec = _bs((1, d), lambda i: (0, 0))
    return pl.pallas_call(
        body, name="ln_ffn_loss", grid=(seq // tr,), in_specs=[row, row, row, vec, vec],
        out_specs=[row, row, vec, vec, _bs((8, 128), lambda i: (0, 0))],
        out_shape=[jax.ShapeDtypeStruct((seq, d), F32), jax.ShapeDtypeStruct((seq, d), BF16),
                   jax.ShapeDtypeStruct((1, d), F32), jax.ShapeDtypeStruct((1, d), F32),
                   jax.ShapeDtypeStruct((8, 128), F32)],
        compiler_params=_params(1),
    )(ffn, x1, tgt, g, b)


def _ln_bwd_side(dx_branch, dres, z, g, b, n_steps):
    seq, d = z.shape
    tr = seq // n_steps

    def fn(step, ins, outs):
        a_ref, r_ref, z_ref, g_ref, b_ref = ins
        dz_ref, dzb_ref, dg_ref, db_ref = outs
        gv = g_ref[...]
        _, xhat, rstd = _ln_fwd(z_ref[...], gv, b_ref[...])
        dz, dg, db = _ln_bwd(ALPHA * r_ref[...] + a_ref[...], xhat, rstd, gv)
        dz_ref[...] = dz
        dzb_ref[...] = dz.astype(BF16)
        _acc_rows(dg_ref, step == 0, dg)
        _acc_rows(db_ref, step == 0, db)

    row = ((tr, d), lambda s: (s, 0))
    vec = ((1, d), lambda s: (0, 0))
    shapes = [jax.ShapeDtypeStruct((seq, d), F32), jax.ShapeDtypeStruct((seq, d), BF16),
              jax.ShapeDtypeStruct((1, d), F32), jax.ShapeDtypeStruct((1, d), F32)]
    return [(dx_branch, *row), (dres, *row), (z, *row), (g, *vec), (b, *vec)], shapes, [row, row, vec, vec], fn


def _to_bf16(name, a, token):
    rows, cols = a.shape
    tr = min(512, rows)

    def body(a_ref, tok_ref, o_ref):
        del tok_ref
        o_ref[...] = a_ref[...].astype(BF16)

    return pl.pallas_call(
        body, name=name, grid=(rows // tr,),
        in_specs=[_bs((tr, cols), lambda i: (i, 0)), _bs((8, 128), lambda i: (0, 0))],
        out_specs=_bs((tr, cols), lambda i: (i, 0)), out_shape=jax.ShapeDtypeStruct((rows, cols), BF16),
        compiler_params=_params(1),
    )(a, token)


def _sum_blocks(name, parts):
    def body(p_ref, o_ref):
        acc = p_ref[0]
        for s in range(1, parts.shape[0]):
            acc = acc + p_ref[s]
        o_ref[...] = acc

    return pl.pallas_call(body, name=name, out_shape=jax.ShapeDtypeStruct(parts.shape[1:], F32))(parts)


def _adamw_values(w, g, m, v):
    m = ADAM_B1 * m + (1.0 - ADAM_B1) * g
    v = ADAM_B2 * v + (1.0 - ADAM_B2) * (g * g)
    m_hat = m / (1.0 - ADAM_B1 ** ADAM_STEP)
    v_hat = v / (1.0 - ADAM_B2 ** ADAM_STEP)
    delta = -ADAM_LR * (m_hat / (jnp.sqrt(v_hat) + ADAM_EPS) + ADAM_WD * w)
    return delta, m, v


def _sum_adamw(name, own, parts, w, m, v):
    rows, cols = w.shape
    n_parts = parts.shape[0]
    tr = rows
    min_rows = 8 if parts.dtype == F32 else 16
    while tr * cols * 4 > 1024 * 1024 and tr % (2 * min_rows) == 0:
        tr //= 2

    def body(*refs):
        if own is None:
            p_ref, w_ref, m_ref, v_ref, g_ref, d_ref, mo_ref, vo_ref = refs
            g = p_ref[0].astype(F32)
            rest = range(1, n_parts)
        else:
            o_ref, p_ref, w_ref, m_ref, v_ref, g_ref, d_ref, mo_ref, vo_ref = refs
            g = o_ref[...]
            rest = range(n_parts)
        for s in rest:
            g = g + p_ref[s].astype(F32)
        delta, mn, vn = _adamw_values(w_ref[...], g, m_ref[...], v_ref[...])
        g_ref[...] = g
        d_ref[...] = delta
        mo_ref[...] = mn
        vo_ref[...] = vn

    spec = _bs((tr, cols), lambda i: (i, 0))
    lead = [] if own is None else [own]
    return pl.pallas_call(
        body, name=name, grid=(rows // tr,),
        in_specs=[spec] * len(lead) + [_bs((n_parts, tr, cols), lambda i: (0, i, 0)), spec, spec, spec],
        out_specs=[spec] * 4, out_shape=[jax.ShapeDtypeStruct((rows, cols), F32)] * 4,
        compiler_params=_params(1),
    )(*lead, parts, w, m, v)


def _rows128(a):
    return a.reshape(-1, 128)


def kernel(x, ln_mix_g, ln_mix_b, w_in, w_pool, pool_scale, conv_w, conv_b, w_rg_a, b_rg_a, w_rg_i, b_rg_i, rg_lambda, w_out, ln_ffn_g, ln_ffn_b, w_mlp_in, w_mlp_out, loss_target, m_ln_mix_g, m_ln_mix_b, m_w_in, m_w_pool, m_pool_scale, m_conv_w, m_conv_b, m_w_rg_a, m_b_rg_a, m_w_rg_i, m_b_rg_i, m_rg_lambda, m_w_out, m_ln_ffn_g, m_ln_ffn_b, m_w_mlp_in, m_w_mlp_out, v_ln_mix_g, v_ln_mix_b, v_w_in, v_w_pool, v_pool_scale, v_conv_w, v_conv_b, v_w_rg_a, v_b_rg_a, v_w_rg_i, v_b_rg_i, v_rg_lambda, v_w_out, v_ln_ffn_g, v_ln_ffn_b, v_w_mlp_in, v_w_mlp_out):
    seq, d_model = x.shape[1], x.shape[2]
    dh = d_model // 2
    lh = dh // N_HEADS
    pg = dh // len(POOL_WINDOWS)
    d_ff = w_mlp_in.shape[2] * N_DEV
    assert lh == 128 and conv_w.shape[3] == lh and w_pool.shape[2] * N_DEV == pg

    xs = x[0]
    tgt = loss_target[0]

    def small_pack(cw, ba, bi, lam):
        return jnp.concatenate([cw.reshape(4, lh), ba.reshape(2, lh), bi.reshape(2, lh), lam.reshape(2, lh),
                                jnp.zeros((SMALL_ROWS - 10, lh), F32)], axis=0)

    pack_mine = small_pack(conv_w, b_rg_a, b_rg_i, rg_lambda)
    pack_bits = lax.bitcast_convert_type(pack_mine, BF16).reshape(1, SMALL_ROWS, 2 * lh)
    win_gather = _SplitGather("gather_w_in", [(w_in[0], 1), (w_pool[0], 1), (pack_bits, 0)], BF16, after=pack_mine)
    wout_gather = _SplitGather("gather_w_out", [(w_out[0], 0)], BF16, after=win_gather.token)
    w1_gather = _SplitGather("gather_w_mlp_in", [(w_mlp_in[0], 1)], BF16, after=wout_gather.token)
    w2_gather = _SplitGather("gather_w_mlp_out", [(w_mlp_out[0], 0)], BF16, after=w1_gather.token)
    xb = _to_bf16("x_bf16", x[0], w2_gather.token)
    win_full, wpool_full, pack_bits_full = win_gather.wait(after=win_gather.relay(after=xb))
    pack_full = lax.bitcast_convert_type(pack_bits_full.reshape(N_DEV, SMALL_ROWS, lh, 2), F32)
    wcat = jnp.concatenate([w_rg_a[0, 0], w_rg_i[0, 0], w_rg_a[0, 1], w_rg_i[0, 1]], axis=-1).astype(BF16)
    vec = lambda i, j, k: (0, 0)
    row_full = lambda i, j, k: (i, 0)

    def after(token):
        return (token, _sp((8, 128), vec))

    def sds(shape, dtype):
        return jax.ShapeDtypeStruct(shape, dtype)

    def plain_epi(acc, i, ex, out):
        out[0][...] = acc

    def bf16_epi(acc, i, ex, out):
        out[0][...] = acc.astype(BF16)

    t = _tiles(seq, d_model, d_ff)

    (p3,) = _matmul(
        "proj", xb, win_full, _sp((t.rows, d_model), lambda i, j, k: (i, 0)), _sp((d_model, dh), lambda i, j, k: (0, j)),
        grid=(seq // t.rows, 3, 1),
        out_shape=[sds((3, seq, dh), F32)], out_specs=[_sp((None, t.rows, dh), lambda i, j, k: (j, i, 0))],
        epilogue=plain_epi)

    d_pool, y_half = _pool_fwd(p3, wpool_full, pool_scale, seq, d_model)
    y, h0p, h1p = _lru_fwd(p3, y_half, pack_full, conv_b, wcat, wout_gather.relay(after=y_half), seq, d_model)
    (wout_full,) = wout_gather.wait(after=y)
    relay_token = w1_gather.relay(after=wout_full)

    mix_rows = 2 * t.ln_rows

    def mix_epi(acc, i, ex, out):
        x_ref, g_ref, b_ref = ex[:3]
        for part in range(2):
            rows = pl.ds(part * t.ln_rows, t.ln_rows)
            z = ALPHA * x_ref[rows, :] + acc[part * t.ln_rows:(part + 1) * t.ln_rows, :]
            x1, _, _ = _ln_fwd(z, g_ref[...], b_ref[...])
            out[0][rows, :] = z
            out[1][rows, :] = x1
            out[2][rows, :] = x1.astype(BF16)

    z1, x1, x1b = _matmul(
        "mix_out", y, wout_full, _sp((mix_rows, d_model), row_full), _sp((d_model, d_model), vec, single=True),
        grid=(seq // mix_rows, 1, 1),
        extras=[(xs, _sp((mix_rows, d_model), row_full)), (ln_mix_g, _sp((1, d_model), vec)),
                (ln_mix_b, _sp((1, d_model), vec)), after(relay_token)],
        out_shape=[sds((seq, d_model), F32), sds((seq, d_model), F32), sds((seq, d_model), BF16)],
        out_specs=[_sp((mix_rows, d_model), row_full)] * 3, epilogue=mix_epi)
    (w1_full,) = w1_gather.wait(after=x1b)

    def mlp_in_epi(acc, i, ex, out, cols):
        h = jnp.maximum(acc, 0.0)
        out[0][:, cols] = (h * h).astype(BF16)
        out[1][:, cols] = (2.0 * h).astype(BF16)

    hmid, dact = _matmul(
        "mlp_in", x1b, w1_full, _sp((t.rows, d_model), lambda i, j, k: (i, 0)),
        _sp((d_model, t.ff_cols), lambda i, j, k: (0, j)),
        grid=(seq // t.rows, d_ff // t.ff_cols, 1), j_outer=True,
        out_shape=[sds((seq, d_ff), BF16)] * 2, out_specs=[_sp((t.rows, t.ff_cols), lambda i, j, k: (i, j))] * 2,
        epilogue=mlp_in_epi, n_split=t.ff_split)
    (w2_full,) = w2_gather.wait(after=w2_gather.relay(after=hmid))

    (ffn,) = _matmul(
        "mlp_out", hmid, w2_full, _sp((t.rows, t.ff_k), lambda i, j, k: (i, k)),
        _sp((t.ff_k, d_model), lambda i, j, k: (k, 0)),
        grid=(seq // t.rows, 1, d_ff // t.ff_k),
        out_shape=[sds((seq, d_model), F32)], out_specs=[_sp((t.rows, d_model), row_full)])
    dz2, dz2b, g_ffn_g, g_ffn_b, loss_part = _ln_loss_bwd(ffn, x1, tgt, ln_ffn_g, ln_ffn_b, t.ln_rows)

    (g_w2,) = _matmul(
        "grad_w_mlp_out", hmid, dz2b, _sp((seq, t.grad_rows), lambda i, j, k: (0, i)),
        _sp((seq, d_model), vec, single=True),
        grid=(d_ff // t.grad_rows, 1, 1), ta=True,
        out_shape=[sds((d_ff, d_model), BF16)], out_specs=[_sp((t.grad_rows, d_model), row_full)],
        epilogue=bf16_epi)
    scatter_w2 = _SplitReduceScatter("scatter_w_mlp_out", [g_w2.reshape(N_DEV, d_ff // N_DEV, d_model)])

    def dpre_epi(acc, i, ex, out, cols):
        out[0][:, cols] = (acc * ex[0][:, cols].astype(F32)).astype(BF16)

    (dpre,) = _matmul(
        "mlp_dpre", dz2b, w2_full, _sp((t.rows, d_model), lambda i, j, k: (i, 0)),
        _sp((t.ff_cols, d_model), lambda i, j, k: (j, 0)),
        grid=(seq // t.rows, d_ff // t.ff_cols, 1), j_outer=True, tb=True,
        extras=[(dact, _sp((t.rows, t.ff_cols), lambda i, j, k: (i, j))), after(scatter_w2.token)],
        out_shape=[sds((seq, d_ff), BF16)], out_specs=[_sp((t.rows, t.ff_cols), lambda i, j, k: (i, j))],
        epilogue=dpre_epi, n_split=t.ff_split)
    token_w2 = scatter_w2.combine_and_send(after=dpre)

    (dx1_mlp,) = _matmul(
        "mlp_dx", dpre, w1_full, _sp((t.rows, t.ff_k), lambda i, j, k: (i, k)),
        _sp((d_model, t.ff_k), lambda i, j, k: (0, k)),
        grid=(seq // t.rows, 1, d_ff // t.ff_k), tb=True, extras=[after(token_w2)],
        out_shape=[sds((seq, d_model), F32)], out_specs=[_sp((t.rows, d_model), row_full)])
    def block_epi(acc, i, ex, out):
        out[0][0] = acc.astype(BF16)

    fs = d_ff // N_DEV
    g_w1, dz1, dz1b, g_mix_g, g_mix_b = _matmul(
        "grad_w_mlp_in", x1b, dpre, _sp((seq, t.grad_rows), lambda i, j, k: (0, i)),
        _sp((seq, fs), lambda i, j, k: (0, j)),
        grid=(d_model // t.grad_rows, N_DEV, 1), j_outer=True, ta=True,
        out_shape=[sds((N_DEV, d_model, fs), BF16)],
        out_specs=[_sp((1, t.grad_rows, fs), lambda i, j, k: (j, i, 0))], epilogue=block_epi,
        side=_ln_bwd_side(dx1_mlp, dz2, z1, ln_mix_g, ln_mix_b, d_model // t.grad_rows * N_DEV))

    (dy,) = _matmul(
        "mix_dy", dz1b, wout_full, _sp((t.rows, d_model), lambda i, j, k: (i, 0)),
        _sp((dh, d_model), lambda i, j, k: (j, 0)),
        grid=(seq // t.rows, 2, 1), j_outer=True, tb=True,
        out_shape=[sds((seq, d_model), F32)], out_specs=[_sp((t.rows, dh), lambda i, j, k: (i, j))],
        epilogue=plain_epi)
    (g_wout,) = _matmul(
        "grad_w_out", y, dz1b, _sp((seq, t.grad_rows), lambda i, j, k: (0, i)), _sp((seq, d_model), vec, single=True),
        grid=(d_model // t.grad_rows, 1, 1), ta=True,
        out_shape=[sds((d_model, d_model), BF16)], out_specs=[_sp((t.grad_rows, d_model), row_full)],
        epilogue=bf16_epi)
    scatter_w1 = _SplitReduceScatter("scatter_w_mlp_in", [g_w1, g_wout.reshape(N_DEV, d_model // N_DEV, d_model)])

    dproj_pool, g_wpool, g_pscale = _pool_bwd(d_pool, dy, wpool_full, pool_scale, scatter_w1.token, seq, d_model)
    token_w1 = scatter_w1.combine_and_send(after=dproj_pool)
    dproj, g_pack, g_convb, g_wcat = _lru_bwd(p3, dy, h0p, h1p, dproj_pool, pack_full, conv_b, wcat,
                                              token_w1, seq, d_model)
    g_wa = jnp.stack([g_wcat[:, :, 0:lh], g_wcat[:, :, 2 * lh:3 * lh]])
    g_wi = jnp.stack([g_wcat[:, :, lh:2 * lh], g_wcat[:, :, 3 * lh:4 * lh]])

    rep_parts = [_rows128(g_wa), _rows128(g_wi), _rows128(g_mix_g), _rows128(g_mix_b), _rows128(g_ffn_g),
                 _rows128(g_ffn_b), _rows128(g_pscale), _rows128(g_convb)]
    rep_rows = [p.shape[0] for p in rep_parts]
    n_rep = sum(rep_rows)
    small = jnp.concatenate(rep_parts + [_rows128(g_pack), loss_part], axis=0)
    small_gather = _SplitGather("gather_small_grads", [(small[None], 0)], F32, after=small)

    ws = 3 * dh // N_DEV

    def pair_epi(acc, i, ex, out):
        out[0][0] = acc[:, :ws].astype(BF16)
        out[0][1] = acc[:, ws:].astype(BF16)

    (g_win,) = _matmul(
        "grad_w_in", xb, dproj, _sp((seq, t.grad_rows), lambda i, j, k: (0, i)),
        _sp((seq, 2 * ws), lambda i, j, k: (0, j)),
        grid=(d_model // t.grad_rows, N_DEV // 2, 1), ta=True, extras=[after(small_gather.token)],
        out_shape=[sds((N_DEV, d_model, ws), BF16)],
        out_specs=[_sp((2, t.grad_rows, ws), lambda i, j, k: (j, i, 0))], epilogue=pair_epi)
    scatter_mix = _SplitReduceScatter(
        "scatter_mixer", [g_win, g_wpool.reshape(N_DEV, pg // N_DEV * len(POOL_WINDOWS), pg)])

    def adam_big(name, own_landed, w, m, v):
        own, landed = own_landed
        shp = w.shape
        two = lambda a: a.reshape(-1, shp[-1])
        res = _sum_adamw(name, own, landed, two(w), two(m), two(v))
        return [r.reshape(shp) for r in res]

    (r_w2,) = scatter_w2.wait(after=scatter_mix.token)
    o_w2 = adam_big("adam_w_mlp_out", r_w2, w_mlp_out, m_w_mlp_out, v_w_mlp_out)
    token_mix = scatter_mix.combine_and_send(after=o_w2[0])

    def dx_epi(acc, i, ex, out):
        out[0][...] = ALPHA * ex[0][...] + acc

    (dx,) = _matmul(
        "grad_x", dproj, win_full, _sp((t.ln_rows * 2, 3 * dh), lambda i, j, k: (i, 0)),
        _sp((d_model, 3 * dh), vec, single=True),
        grid=(seq // (t.ln_rows * 2), 1, 1), tb=True,
        extras=[(dz1, _sp((t.ln_rows * 2, d_model), row_full)), after(token_mix)],
        out_shape=[sds((seq, d_model), F32)], out_specs=[_sp((t.ln_rows * 2, d_model), row_full)],
        epilogue=dx_epi)
    r_w1, r_wout = scatter_w1.wait(after=dx)
    o_w1 = adam_big("adam_w_mlp_in", r_w1, w_mlp_in, m_w_mlp_in, v_w_mlp_in)
    o_wout = adam_big("adam_w_out", r_wout, w_out, m_w_out, v_w_out)
    r_win, r_wpool = scatter_mix.wait(after=o_wout[0])
    o_win = adam_big("adam_w_in", r_win, w_in, m_w_in, v_w_in)
    o_wpool = adam_big("adam_w_pool", r_wpool, w_pool, m_w_pool, v_w_pool)

    small_gather.relay(after=o_win[0])
    (small_all,) = small_gather.wait(after=o_wpool[0])

    rep_w = [w_rg_a, w_rg_i, ln_mix_g, ln_mix_b, ln_ffn_g, ln_ffn_b, pool_scale, conv_b]
    rep_m = [m_w_rg_a, m_w_rg_i, m_ln_mix_g, m_ln_mix_b, m_ln_ffn_g, m_ln_ffn_b, m_pool_scale, m_conv_b]
    rep_v = [v_w_rg_a, v_w_rg_i, v_ln_mix_g, v_ln_mix_b, v_ln_ffn_g, v_ln_ffn_b, v_pool_scale, v_conv_b]
    cat = lambda arrs: jnp.concatenate([_rows128(a) for a in arrs], axis=0)
    o_rep = _sum_adamw("adam_replicated", None, small_all, cat(rep_w), cat(rep_m), cat(rep_v))

    my_idx = _dev_index(_where_am_i())
    head_parts = lax.dynamic_slice_in_dim(small_all, n_rep + my_idx * SMALL_ROWS, SMALL_ROWS, axis=1)
    o_head = _sum_adamw("adam_head", None, head_parts, pack_mine,
                        small_pack(m_conv_w, m_b_rg_a, m_b_rg_i, m_rg_lambda),
                        small_pack(v_conv_w, v_b_rg_a, v_b_rg_i, v_rg_lambda))

    def unpack_rep(packed):
        out, r = [], 0
        for wgt, rows in zip(rep_w, rep_rows):
            out.append(packed[r:r + rows].reshape(wgt.shape))
            r += rows
        return out

    def unpack_head(packed):
        return [packed[0:4].reshape(conv_w.shape), packed[4:6].reshape(b_rg_a.shape),
                packed[6:8].reshape(b_rg_i.shape), packed[8:10].reshape(rg_lambda.shape)]

    loss = _sum_blocks("loss_sum", small_all[:, n_rep + N_HEADS * SMALL_ROWS:, :])[0, 0]

    outs = [loss, dx[None]]
    for kind in range(4):
        ra, ri, mg, mb, fg, fb, ps, cb = unpack_rep(o_rep[kind])
        cw, ba, bi, lam = unpack_head(o_head[kind])
        outs += [mg, mb, o_win[kind], o_wpool[kind], ps, cw, cb, ra, ba, ri, bi, lam, o_wout[kind], fg, fb,
                 o_w1[kind], o_w2[kind]]
    return tuple(outs)
```

```python
import functools

import jax
import jax.numpy as jnp
from jax import lax
from jax.experimental import pallas as pl
from jax.experimental.pallas import tpu as pltpu

F32 = jnp.float32
BF16 = jnp.bfloat16
MESH = pl.DeviceIdType.MESH
ANY = pl.BlockSpec(memory_space=pl.ANY)

N_DEV = 8
POOL_WINDOWS = (2, 4, 8, 16)
N_HEADS = 8
RG_C = 8.0
LN_EPS = 1e-5
ALPHA = 2.0 ** 0.25
ADAM_LR = 0.001
ADAM_B1 = 0.9
ADAM_B2 = 0.999
ADAM_EPS = 1e-08
ADAM_WD = 0.01
ADAM_STEP = 10

VMEM_LIMIT = 56 * 1024 * 1024
WIN_HALO = 16
CONV_HALO = 8
SMALL_ROWS = 16


def _params(n_grid):
    return pltpu.CompilerParams(dimension_semantics=("arbitrary",) * n_grid, vmem_limit_bytes=VMEM_LIMIT)


def _shift(v, j):
    n = v.shape[0]
    s = (-j) % n
    return v if s == 0 else pltpu.roll(v, s, 0)


def _sigmoid(x):
    return 0.5 * jnp.tanh(0.5 * x) + 0.5


def _softplus(z):
    e = jnp.exp(-jnp.abs(z))
    u = 1.0 + e
    log1p = jnp.where(u == 1.0, e, jnp.log(u) * (e / jnp.where(u == 1.0, 1.0, u - 1.0)))
    return jnp.maximum(z, 0.0) + log1p


_GELU_C = 0.7978845608028654
_GELU_K = 0.044715


def _gelu_and_grad(x):
    x2 = x * x
    t = jnp.tanh(_GELU_C * (x + _GELU_K * x * x2))
    g = 0.5 * x * (1.0 + t)
    dg = 0.5 * (1.0 + t) + 0.5 * x * (1.0 - t * t) * (_GELU_C * (1.0 + 3.0 * _GELU_K * x2))
    return g, dg


def _ln_fwd(z, g, b):
    mu = jnp.mean(z, axis=-1, keepdims=True)
    zc = z - mu
    var = jnp.mean(zc * zc, axis=-1, keepdims=True)
    rstd = lax.rsqrt(var + LN_EPS)
    xhat = zc * rstd
    return xhat * g + b, xhat, rstd


def _ln_bwd(dy, xhat, rstd, g):
    dxhat = dy * g
    m1 = jnp.mean(dxhat, axis=-1, keepdims=True)
    m2 = jnp.mean(dxhat * xhat, axis=-1, keepdims=True)
    dz = rstd * (dxhat - m1 - xhat * m2)
    dg = jnp.sum(dy * xhat, axis=0, keepdims=True)
    db = jnp.sum(dy, axis=0, keepdims=True)
    return dz, dg, db


def _acc_rows(ref, first, val):
    @pl.when(first)
    def _():
        ref[...] = val

    @pl.when(jnp.logical_not(first))
    def _():
        ref[...] += val


def _sp(shape, fn, single=False):
    return shape, fn, single


def _matmul(name, a, b, a_spec, b_spec, *, grid, j_outer=False, ta=False, tb=False, extras=(), out_shape, out_specs,
            epilogue=None, n_split=1, side=None, a_tile=None, b_tile=None):
    ni, nj, nk = grid
    n_ex = len(extras)
    dims = (((0 if ta else 1,), (1 if tb else 0,)), ((), ()))
    side_in, side_shape, side_out, side_fn = side if side is not None else ((), (), (), None)
    n_main_out = len(out_shape)
    inner = ni if j_outer else nj

    def mk(spec):
        shape, fn, single = spec
        index = (lambda g0, g1, g2: fn(g1, g0, g2)) if j_outer else fn
        return pl.BlockSpec(shape, index, pipeline_mode=pl.Buffered(1)) if single else pl.BlockSpec(shape, index)

    def mk_side(block, fn):
        return pl.BlockSpec(block, lambda g0, g1, g2: fn(g0 * inner + g1))

    def body(a_ref, b_ref, *rest):
        ex_refs = rest[:n_ex]
        out_refs = rest[n_ex + len(side_in):n_ex + len(side_in) + n_main_out]
        if side_fn is not None:
            side_fn(pl.program_id(0) * inner + pl.program_id(1), rest[n_ex:n_ex + len(side_in)],
                    rest[n_ex + len(side_in) + n_main_out:])
        i = pl.program_id(1 if j_outer else 0)
        if a_tile is not None:
            at = pl.ds(pl.multiple_of(i * a_tile, a_tile), a_tile)
            a_ref = a_ref.at[:, at] if ta else a_ref.at[at, :]
        if b_tile is not None:
            j = pl.program_id(0 if j_outer else 1)
            bt = pl.ds(pl.multiple_of(j * b_tile, b_tile), b_tile)
            b_ref = b_ref.at[bt, :] if tb else b_ref.at[:, bt]
        if n_split > 1:
            av = a_ref[...].astype(BF16)
            width = b_ref.shape[0 if tb else 1] // n_split
            for c in range(n_split):
                cols = pl.ds(c * width, width)
                bv = (b_ref[cols, :] if tb else b_ref[:, cols]).astype(BF16)
                epilogue(lax.dot_general(av, bv, dims, preferred_element_type=F32), i, ex_refs, out_refs, cols)
            return
        part = lax.dot_general(a_ref[...].astype(BF16), b_ref[...].astype(BF16), dims, preferred_element_type=F32)
        if nk == 1:
            epilogue(part, i, ex_refs, out_refs)
        else:
            @pl.when(pl.program_id(2) == 0)
            def _():
                out_refs[0][...] = part

            @pl.when(pl.program_id(2) > 0)
            def _():
                out_refs[0][...] += part

    return pl.pallas_call(
        body, name=name, grid=(nj, ni, nk) if j_outer else (ni, nj, nk),
        in_specs=[mk(a_spec), mk(b_spec)] + [mk(s) for _, s in extras] + [mk_side(blk, fn) for _, blk, fn in side_in],
        out_specs=[mk(s) for s in out_specs] + [mk_side(blk, fn) for blk, fn in side_out],
        out_shape=list(out_shape) + list(side_shape),
        compiler_params=_params(3),
    )(a, b, *[x for x, _ in extras], *[x for x, _, _ in side_in])


def _bs(shape, fn):
    return pl.BlockSpec(shape, fn)


def _where_am_i():
    x, y, c = lax.axis_index("x"), lax.axis_index("y"), lax.axis_index("c")
    return x, y, c


def _dev_index(p):
    return 4 * p[0] + 2 * p[1] + p[2]


def _slab(ref, axis, idx, size):
    sl = [slice(None)] * len(ref.shape)
    sl[axis] = pl.ds(idx * size, size)
    return ref.at[tuple(sl)]


def _all_gather(name, items):
    n = len(items)
    shapes = []
    for shard, axis in items:
        s = list(shard.shape)
        s[axis] *= N_DEV
        shapes.append(jax.ShapeDtypeStruct(tuple(s), shard.dtype))

    def body(*refs):
        in_refs, out_refs = refs[:n], refs[n:2 * n]
        send_sems, recv_sems, local_sems = refs[2 * n:]
        x, y, c = _where_am_i()
        me, sibling = (x, y, c), (x, y, 1 - c)
        chips = [(1 - x, y), (x, 1 - y), (1 - x, 1 - y)]

        def blk(a, p):
            axis = items[a][1]
            return _slab(out_refs[a], axis, _dev_index(p), items[a][0].shape[axis])

        def copy(a, k, block, to, src=None):
            return pltpu.make_async_remote_copy(
                src_ref=blk(a, block) if src is None else src, dst_ref=blk(a, block),
                send_sem=send_sems.at[a, k], recv_sem=recv_sems.at[a, k], device_id=to, device_id_type=MESH)

        mine = [pltpu.make_async_copy(in_refs[a], blk(a, me), local_sems.at[a]) for a in range(n)]
        for cp in mine:
            cp.start()
        first = []
        for a in range(n):
            first.append(copy(a, 0, me, sibling, src=in_refs[a]))
            first += [copy(a, 1 + j, me, (*chip, c), src=in_refs[a]) for j, chip in enumerate(chips)]
        for cp in first:
            cp.start()
        passed = []
        for a in range(n):
            for j, chip in enumerate(chips):
                copy(a, 1 + j, (*chip, c), me).wait_recv()
                fw = copy(a, 4 + j, (*chip, c), sibling)
                fw.start()
                passed.append(fw)
        for a in range(n):
            copy(a, 0, sibling, me).wait_recv()
            for j, chip in enumerate(chips):
                copy(a, 4 + j, (*chip, 1 - c), me).wait_recv()
        for cp in first + passed:
            cp.wait_send()
        for cp in mine:
            cp.wait()

    outs = pl.pallas_call(
        body, name=name, out_shape=shapes, in_specs=[ANY] * n, out_specs=[ANY] * n,
        scratch_shapes=[pltpu.SemaphoreType.DMA((n, 7)), pltpu.SemaphoreType.DMA((n, 7)),
                        pltpu.SemaphoreType.DMA((n,))],
    )(*[s for s, _ in items])
    return list(outs)


HBM = pl.BlockSpec(memory_space=pltpu.HBM)
SEM = pl.BlockSpec(memory_space=pltpu.SEMAPHORE)
DATAFLOW = pltpu.SideEffectType.DATAFLOW_SIDE_EFFECTING


def _in_hbm(a):
    return pltpu.with_memory_space_constraint(a, pltpu.HBM)


def _token_shape():
    return jax.ShapeDtypeStruct((8, 128), F32)


def _split_start(name, n_sems, bufs, issue):
    nb = len(bufs)

    def body(*refs):
        issue(refs[:nb], refs[nb], refs[nb + 1])
        refs[-1][...] = jnp.zeros((8, 128), F32)

    outs = pl.pallas_call(
        body, name=name,
        out_shape=(pltpu.SemaphoreType.DMA((n_sems,)), pltpu.SemaphoreType.DMA((n_sems,)),
                   *[pltpu.HBM(b.shape, b.dtype) for b in bufs], _token_shape()),
        in_specs=[HBM] * nb, out_specs=(SEM, SEM, *[HBM] * nb, pl.BlockSpec(memory_space=pltpu.VMEM)),
        input_output_aliases={i: 2 + i for i in range(nb)},
        compiler_params=pltpu.CompilerParams(has_side_effects=DATAFLOW),
    )(*[_in_hbm(b) for b in bufs])
    return outs[0], outs[1], list(outs[2:2 + nb]), outs[-1]


def _split_relay(name, n_sems, sems, bufs, after, relay):
    nb = len(bufs)

    def body(*refs):
        relay(refs[:nb], refs[nb], refs[nb + 1], refs[nb + 3], refs[nb + 4])
        refs[-1][...] = jnp.zeros((8, 128), F32)

    outs = pl.pallas_call(
        body, name=name,
        out_shape=(pltpu.SemaphoreType.DMA((n_sems,)), pltpu.SemaphoreType.DMA((n_sems,)),
                   *[pltpu.HBM(b.shape, b.dtype) for b in bufs], _token_shape()),
        in_specs=[HBM] * nb + [SEM, SEM, ANY],
        out_specs=(SEM, SEM, *[HBM] * nb, pl.BlockSpec(memory_space=pltpu.VMEM)),
        input_output_aliases={i: 2 + i for i in range(nb)},
        compiler_params=pltpu.CompilerParams(has_side_effects=DATAFLOW),
    )(*bufs, sems[0], sems[1], after)
    return outs[0], outs[1], list(outs[2:2 + nb]), outs[-1]


def _split_wait(name, sems, bufs, after, finish):
    nb = len(bufs)

    def body(*refs):
        finish(refs[:nb], refs[nb], refs[nb + 1])

    outs = pl.pallas_call(
        body, name=name, out_shape=[pltpu.HBM(b.shape, b.dtype) for b in bufs],
        in_specs=[HBM] * nb + [SEM, SEM, ANY], out_specs=[HBM] * nb,
        input_output_aliases={i: i for i in range(nb)},
        compiler_params=pltpu.CompilerParams(has_side_effects=DATAFLOW),
    )(*bufs, sems[0], sems[1], after)
    return list(outs)


def _place(name, items, dtype, after):
    ids = jnp.reshape(_dev_index(_where_am_i()), (1,)).astype(jnp.int32)
    outs = []
    for a, (shard, axis) in enumerate(items):
        rows, cols = shard.shape[-2], shard.shape[-1]
        tr = rows
        while tr * cols * shard.dtype.itemsize > 4 * 1024 * 1024 and tr % 32 == 0:
            tr //= 2
        nt = rows // tr
        full = list(shard.shape)
        full[axis] *= N_DEV
        if shard.ndim == 2 and axis == 0:
            in_spec = _bs((tr, cols), lambda i, ids: (i, 0))
            out_spec = _bs((tr, cols), lambda i, ids, nt=nt: (ids[0] * nt + i, 0))
        elif shard.ndim == 2 and axis == 1:
            in_spec = _bs((tr, cols), lambda i, ids: (i, 0))
            out_spec = _bs((tr, cols), lambda i, ids: (i, ids[0]))
        elif shard.ndim == 3 and axis == 1:
            tr, nt = rows, shard.shape[0]
            in_spec = _bs((None, rows, cols), lambda i, ids: (i, 0, 0))
            out_spec = _bs((None, rows, cols), lambda i, ids: (i, ids[0], 0))
        else:
            assert shard.ndim == 3 and axis == 0 and shard.shape[0] == 1
            in_spec = _bs((None, tr, cols), lambda i, ids: (0, i, 0))
            out_spec = _bs((None, tr, cols), lambda i, ids: (ids[0], i, 0))

        def body(ids_ref, in_ref, after_ref, out_ref):
            del ids_ref, after_ref
            out_ref[...] = in_ref[...].astype(out_ref.dtype)

        outs.append(pl.pallas_call(
            body, name=f"{name}{a}",
            grid_spec=pltpu.PrefetchScalarGridSpec(
                num_scalar_prefetch=1, grid=(nt,), in_specs=[in_spec, ANY], out_specs=out_spec),
            out_shape=jax.ShapeDtypeStruct(tuple(full), dtype), compiler_params=_params(1),
        )(ids, shard, after))
    return outs


class _SplitGather:
    def __init__(self, name, items, dtype, after):
        self.name, self.items, self.n = name, items, len(items)
        fulls = _place(name + "_place", items, dtype, after)
        n = self.n

        def issue(refs, send, recv):
            me, sibling, chips, c = self._geometry()
            for a in range(n):
                self._copy1(refs, send, recv, a, 0, me, sibling).start()
                for j, chip in enumerate(chips):
                    self._copy1(refs, send, recv, a, 1 + j, me, (*chip, c)).start()

        self.send, self.recv, self.bufs, self.token = _split_start(name + "_start", 4 * n, fulls, issue)

    @staticmethod
    def _geometry():
        x, y, c = _where_am_i()
        return (x, y, c), (x, y, 1 - c), [(1 - x, y), (x, 1 - y), (1 - x, 1 - y)], c

    def _blk(self, refs, a, p):
        shard, axis = self.items[a]
        return _slab(refs[a], axis, _dev_index(p), shard.shape[axis])

    def _copy1(self, refs, send, recv, a, k, owner, to):
        return pltpu.make_async_remote_copy(
            src_ref=self._blk(refs, a, owner), dst_ref=self._blk(refs, a, owner), send_sem=send.at[4 * a + k],
            recv_sem=recv.at[4 * a + k], device_id=to, device_id_type=MESH)

    def _copy2(self, refs, send, recv, a, j, owner, to):
        return pltpu.make_async_remote_copy(
            src_ref=self._blk(refs, a, owner), dst_ref=self._blk(refs, a, owner), send_sem=send.at[3 * a + j],
            recv_sem=recv.at[3 * a + j], device_id=to, device_id_type=MESH)

    def relay(self, after):
        n = self.n

        def relay(refs, send_in, recv_in, send_out, recv_out):
            me, sibling, chips, c = self._geometry()
            for a in range(n):
                for j, chip in enumerate(chips):
                    self._copy1(refs, send_in, recv_in, a, 1 + j, (*chip, c), me).wait_recv()
                    self._copy2(refs, send_out, recv_out, a, j, (*chip, c), sibling).start()
            for a in range(n):
                self._copy1(refs, send_in, recv_in, a, 0, sibling, me).wait_recv()
                for k in range(4):
                    self._copy1(refs, send_in, recv_in, a, k, me, sibling).wait_send()

        self.send, self.recv, self.bufs, self.token = _split_relay(
            self.name + "_relay", 3 * n, (self.send, self.recv), self.bufs, after, relay)
        return self.token

    def wait(self, after):
        n = self.n

        def finish(refs, send, recv):
            me, sibling, chips, c = self._geometry()
            for a in range(n):
                for j, chip in enumerate(chips):
                    self._copy2(refs, send, recv, a, j, (*chip, 1 - c), me).wait_recv()
                    self._copy2(refs, send, recv, a, j, (*chip, c), sibling).wait_send()

        return _split_wait(self.name + "_wait", (self.send, self.recv), self.bufs, after, finish)


class _SplitReduceScatter:
    def __init__(self, name, grads):
        self.name, self.n = name, len(grads)
        n = self.n
        g4 = [g.reshape(4, 2, *g.shape[1:]) for g in grads]
        land = [lax.empty((4, 1, *g.shape[1:]), g.dtype) for g in grads]

        def issue(refs, send, recv):
            for a in range(n):
                self._swap(refs, send, recv, a).start()

        self.send, self.recv, self.bufs, self.token = _split_start(name + "_d2d_start", n, g4 + land, issue)

    def _swap(self, refs, send, recv, a):
        x, y, c = _where_am_i()
        return pltpu.make_async_remote_copy(
            src_ref=refs[a].at[:, pl.ds(1 - c, 1)], dst_ref=refs[self.n + a], send_sem=send.at[a], recv_sem=recv.at[a],
            device_id=(x, y, 1 - c), device_id_type=MESH)

    def _hop(self, refs, send, recv, a, m):
        x, y, c = _where_am_i()
        px = (1 - x) if m & 2 else x
        py = (1 - y) if m & 1 else y
        return pltpu.make_async_remote_copy(
            src_ref=refs[a].at[2 * px + py], dst_ref=refs[self.n + a].at[m - 1], send_sem=send.at[3 * a + m - 1],
            recv_sem=recv.at[3 * a + m - 1], device_id=(px, py, c), device_id_type=MESH)

    def combine_and_send(self, after):
        n = self.n

        def finish(refs, send, recv):
            for a in range(n):
                self._swap(refs, send, recv, a).wait()

        bufs = _split_wait(self.name + "_d2d_wait", (self.send, self.recv), self.bufs, after, finish)
        x, y, c = _where_am_i()
        ids = jnp.stack([c, 2 * x + y]).astype(jnp.int32)
        self.own, sums = [], []
        for a in range(n):
            own, hb = _pair_sum(f"{self.name}_sum{a}", bufs[a], bufs[n + a], ids)
            self.own.append(own)
            sums.append(hb)
        land = [lax.empty((3, *h.shape[1:]), h.dtype) for h in sums]

        def issue(refs, send, recv):
            for a in range(n):
                for m in (1, 2, 3):
                    self._hop(refs, send, recv, a, m).start()

        self.send, self.recv, self.bufs, self.token = _split_start(self.name + "_ici_start", 3 * n, sums + land, issue)
        return self.token

    def wait(self, after):
        n = self.n

        def finish(refs, send, recv):
            for a in range(n):
                for m in (1, 2, 3):
                    self._hop(refs, send, recv, a, m).wait()

        bufs = _split_wait(self.name + "_ici_wait", (self.send, self.recv), self.bufs, after, finish)
        return list(zip(self.own, bufs[n:]))


def _pair_sum(name, g4, land, ids):
    rows, cols = g4.shape[2], g4.shape[3]
    tr = rows
    while tr * cols * 2 > 1024 * 1024 and tr % 32 == 0:
        tr //= 2

    def body(ids_ref, g_ref, l_ref, own_ref, sum_ref):
        h = g_ref[...].astype(F32) + l_ref[...].astype(F32)
        sum_ref[...] = h.astype(sum_ref.dtype)

        @pl.when(pl.program_id(1) == ids_ref[1])
        def _():
            own_ref[...] = h

    return pl.pallas_call(
        body, name=name,
        grid_spec=pltpu.PrefetchScalarGridSpec(
            num_scalar_prefetch=1, grid=(rows // tr, 4),
            in_specs=[_bs((None, None, tr, cols), lambda i, q, ids: (q, ids[0], i, 0)),
                      _bs((None, None, tr, cols), lambda i, q, ids: (q, 0, i, 0))],
            out_specs=[_bs((tr, cols), lambda i, q, ids: (i, 0)), _bs((None, tr, cols), lambda i, q, ids: (q, i, 0))]),
        out_shape=[jax.ShapeDtypeStruct((rows, cols), F32), jax.ShapeDtypeStruct((4, rows, cols), g4.dtype)],
        compiler_params=_params(2),
    )(ids, g4, land)


def _win_sum(ext, w, off):
    s = ext + _shift(ext, -1)
    if w >= 4:
        s = _shift(s, -1) + _shift(s, 1)
    if w >= 8:
        s = _shift(s, -2) + _shift(s, 2)
    if w >= 16:
        s = _shift(s, -4) + _shift(s, 4)
    return _shift(s, off) if off else s


def _inv_count(r0, t, w, seq):
    pos = r0 + lax.broadcasted_iota(jnp.int32, (t, 1), 0)
    cnt = jnp.minimum(pos + w // 2, seq) - jnp.maximum(pos - w // 2, 0)
    return 1.0 / cnt.astype(F32)


def _pool_fwd(p3, w_pool, pool_scale, seq, d_model):
    dp = d_model // 2
    pg = dp // len(POOL_WINDOWS)
    t = min(128, seq)
    n_chunks = seq // t
    h = WIN_HALO

    def body(u_ref, w_ref, sc_ref, d_ref, y_ref, pad_ref):
        g = pl.program_id(0)
        zeros = jnp.zeros((h, pg), F32)
        pad_ref[0:h, :] = zeros
        pad_ref[h + seq:h + seq + h, :] = zeros

        def fill(ci, _):
            r0 = pl.multiple_of(ci * t, t)
            pad_ref[pl.ds(h + r0, t), :] = u_ref[pl.ds(r0, t), :]
            return 0

        lax.fori_loop(0, n_chunks, fill, 0)
        wmat = w_ref[...]
        scale = sc_ref[...]
        for gi, w in enumerate(POOL_WINDOWS):
            @pl.when(g == gi)
            def _(w=w):
                def chunk(ci, _):
                    r0 = pl.multiple_of(ci * t, t)
                    ext = pad_ref[pl.ds(r0, t + 2 * h), :]
                    mean = _win_sum(ext, w, 0)[h:h + t, :] * _inv_count(r0, t, w, seq)
                    d = (mean - ext[h:h + t, :]).astype(BF16)
                    d_ref[pl.ds(r0, t), :] = d
                    q = jnp.dot(d, wmat, preferred_element_type=F32)
                    y_ref[pl.ds(r0, t), :] = (q * scale).astype(BF16)
                    return 0

                lax.fori_loop(0, n_chunks, chunk, 0, unroll=2)

    return pl.pallas_call(
        body, name="pool_fwd", grid=(len(POOL_WINDOWS),),
        in_specs=[_bs((None, seq, pg), lambda g: (0, 0, g)), _bs((None, pg, pg), lambda g: (g, 0, 0)),
                  _bs((1, pg), lambda g: (0, g))],
        out_specs=[_bs((seq, pg), lambda g: (0, g)), _bs((seq, pg), lambda g: (0, g))],
        out_shape=[jax.ShapeDtypeStruct((seq, dp), BF16), jax.ShapeDtypeStruct((seq, d_model), BF16)],
        scratch_shapes=[pltpu.VMEM((seq + 2 * h, pg), F32)],
        compiler_params=_params(1),
    )(p3, w_pool, pool_scale)


def _pool_bwd(d, dy, w_pool, pool_scale, token, seq, d_model):
    dp = d_model // 2
    pg = dp // len(POOL_WINDOWS)
    t = min(128, seq)
    n_chunks = seq // t
    h = WIN_HALO
    tn_dims = (((0,), (0,)), ((), ()))
    nt_dims = (((1,), (1,)), ((), ()))

    def body(d_ref, dy_ref, w_ref, sc_ref, tok_ref, du_ref, dwb_ref, dsc_ref, pad_ref, dd_ref, dw_ref):
        del tok_ref
        g = pl.program_id(0)
        zeros = jnp.zeros((h, pg), F32)
        pad_ref[0:h, :] = zeros
        pad_ref[h + seq:h + seq + h, :] = zeros
        wmat = w_ref[...]
        scale = sc_ref[...]
        for gi, w in enumerate(POOL_WINDOWS):
            @pl.when(g == gi)
            def _(w=w):
                dw_ref[...] = jnp.zeros((pg, pg), F32)

                def first(ci, dsc):
                    r0 = pl.multiple_of(ci * t, t)
                    dv = d_ref[pl.ds(r0, t), :]
                    dyv = dy_ref[pl.ds(r0, t), :]
                    q = jnp.dot(dv, wmat, preferred_element_type=F32)
                    dsc = dsc + jnp.sum(dyv * q, axis=0, keepdims=True)
                    dq = (dyv * scale).astype(BF16)
                    dw_ref[...] += lax.dot_general(dv, dq, tn_dims, preferred_element_type=F32)
                    dd = lax.dot_general(dq, wmat, nt_dims, preferred_element_type=F32)
                    dd_ref[pl.ds(r0, t), :] = dd
                    pad_ref[pl.ds(h + r0, t), :] = dd * _inv_count(r0, t, w, seq)
                    return dsc

                def first_pair(cj, dsc):
                    return first(2 * cj + 1, first(2 * cj, dsc))

                dsc_ref[...] = lax.fori_loop(0, n_chunks // 2, first_pair, jnp.zeros((1, pg), F32))
                dwb_ref[...] = dw_ref[...].reshape(N_DEV, pg // N_DEV, pg).astype(BF16)

                def second(ci, _):
                    r0 = pl.multiple_of(ci * t, t)
                    ext = pad_ref[pl.ds(r0, t + 2 * h), :]
                    back = _win_sum(ext, w, 1)[h:h + t, :]
                    du_ref[pl.ds(r0, t), :] = (back - dd_ref[pl.ds(r0, t), :]).astype(BF16)
                    return 0

                lax.fori_loop(0, n_chunks, second, 0, unroll=2)

    return pl.pallas_call(
        body, name="pool_bwd", grid=(len(POOL_WINDOWS),),
        in_specs=[_bs((seq, pg), lambda g: (0, g)), _bs((seq, pg), lambda g: (0, g)),
                  _bs((None, pg, pg), lambda g: (g, 0, 0)), _bs((1, pg), lambda g: (0, g)),
                  _bs((8, 128), lambda g: (0, 0))],
        out_specs=[_bs((seq, pg), lambda g: (0, g)), _bs((N_DEV, None, pg // N_DEV, pg), lambda g: (0, g, 0, 0)),
                   _bs((1, pg), lambda g: (0, g))],
        out_shape=[jax.ShapeDtypeStruct((seq, 3 * dp), BF16),
                   jax.ShapeDtypeStruct((N_DEV, len(POOL_WINDOWS), pg // N_DEV, pg), BF16),
                   jax.ShapeDtypeStruct((1, dp), F32)],
        scratch_shapes=[pltpu.VMEM((seq + 2 * h, pg), F32), pltpu.VMEM((seq, pg), F32), pltpu.VMEM((pg, pg), F32)],
        compiler_params=_params(1),
    )(d, dy, w_pool, pool_scale, token)


def _tile_scan(n_tiles, lanes, loads, stores):
    row = lax.broadcasted_iota(jnp.int32, (8, lanes), 0)
    group = 8

    def local_scan(n, k):
        aa, bb = loads[n](k)
        for sh in (1, 2, 4):
            if n == 0:
                ok = row >= sh
                ap = jnp.where(ok, pltpu.roll(aa, sh, 0), 1.0)
                bp = jnp.where(ok, pltpu.roll(bb, sh, 0), 0.0)
            else:
                ok = row < 8 - sh
                ap = jnp.where(ok, pltpu.roll(aa, 8 - sh, 0), 1.0)
                bp = jnp.where(ok, pltpu.roll(bb, 8 - sh, 0), 0.0)
            bb = aa * bp + bb
            aa = aa * ap
        return aa, bb

    def step(s, carry):
        carry = list(carry)
        for n in range(2):
            tiles = [s * group + u if n == 0 else n_tiles - 1 - (s * group + u) for u in range(group)]
            local = [local_scan(n, k) for k in tiles]
            for k, (aa, bb) in zip(tiles, local):
                hh = bb + aa * carry[n]
                stores[n](k, hh)
                carry[n] = jnp.broadcast_to(hh[7:8, :] if n == 0 else hh[0:1, :], (8, lanes))
        return tuple(carry)

    zeros = jnp.zeros((8, lanes), F32)
    lax.fori_loop(0, n_tiles // group, step, (zeros, zeros))


def _gate_preacts(xc, wcat_ref):
    xcb = xc.astype(BF16)
    return xcb, jnp.dot(xcb, wcat_ref[...], preferred_element_type=F32)


def _gates(pre, n, pk_ref, sp):
    lh = pre.shape[1] // 4
    r = _sigmoid(pre[:, (2 * n) * lh:(2 * n + 1) * lh] + pk_ref[pl.ds(4 + n, 1), :])
    i = _sigmoid(pre[:, (2 * n + 1) * lh:(2 * n + 2) * lh] + pk_ref[pl.ds(6 + n, 1), :])
    log_a = (-RG_C * r) * sp[n]
    a = jnp.exp(log_a)
    x = 2.0 * log_a
    one_minus_a2 = jnp.where(x > -0.01, -(x * (1.0 + x * (0.5 + x * (1.0 / 6.0)))), 1.0 - a * a)
    m = jnp.sqrt(one_minus_a2)
    return r, i, a, m


def _conv_chunk(upad_ref, pk_ref, cb, r0, t):
    ext = upad_ref[pl.ds(r0, t + 2 * CONV_HALO), :]
    acc = pk_ref[pl.ds(1, 1), :] * ext
    for k in (0, 2, 3):
        acc = acc + pk_ref[pl.ds(k, 1), :] * _shift(ext, k - 1)
    return acc[CONV_HALO:CONV_HALO + t, :] + cb, ext


def _lru_fwd(p3, y_in, pack, conv_b, wcat, token, seq, d_model):
    dl = d_model // 2
    lh = dl // N_HEADS
    t = min(128, seq)
    n_chunks = seq // t
    seg = seq // 8
    hal = CONV_HALO
    first_rec_block = (d_model - dl) // lh

    def body(ur_ref, ug_ref, pk_ref, cb_ref, wcat_ref, yin_ref, tok_ref, y_ref, h0_ref, h1_ref,
             upad, a_scr, b_scr):
        del yin_ref, tok_ref
        zeros = jnp.zeros((hal, lh), F32)
        upad[0:hal, :] = zeros
        upad[hal + seq:hal + seq + hal, :] = zeros
        for ref in (h0_ref, h1_ref):
            ref[0:hal, :] = zeros
            ref[hal + seq:hal + seq + hal, :] = zeros

        def fill(ci, _):
            r0 = pl.multiple_of(ci * t, t)
            upad[pl.ds(hal + r0, t), :] = ur_ref[pl.ds(r0, t), :]
            return 0

        lax.fori_loop(0, n_chunks, fill, 0)
        cb = cb_ref[...]
        sp = [_softplus(-pk_ref[pl.ds(8 + n, 1), :]) for n in range(2)]

        def chunk(ci, _):
            r0 = pl.multiple_of(ci * t, t)
            xc, _ext = _conv_chunk(upad, pk_ref, cb, r0, t)
            _, pre = _gate_preacts(xc, wcat_ref)
            for n in range(2):
                _, i, a, m = _gates(pre, n, pk_ref, sp)
                a_scr[n, pl.ds(r0, t), :] = a
                b_scr[n, pl.ds(r0, t), :] = (m * i) * xc
            return 0

        lax.fori_loop(0, n_chunks, chunk, 0, unroll=2)

        def load(n):
            def get(k):
                at = pl.ds(pl.multiple_of(k * 8, 8), 8)
                return a_scr[n, at, :], b_scr[n, at, :]
            return get

        def store(ref):
            def put(k, v):
                ref[pl.ds(pl.multiple_of(hal + k * 8, 8), 8), :] = v
            return put

        _tile_scan(seq // 8, lh, [load(0), load(1)], [store(h0_ref), store(h1_ref)])

        def out(ci, _):
            r0 = pl.multiple_of(ci * t, t)
            hsum = h0_ref[pl.ds(hal + r0, t), :] + h1_ref[pl.ds(hal + r0, t), :]
            gl, _dg = _gelu_and_grad(ug_ref[pl.ds(r0, t), :])
            y_ref[pl.ds(r0, t), :] = (hsum * gl).astype(BF16)
            return 0

        lax.fori_loop(0, n_chunks, out, 0)

    return pl.pallas_call(
        body, name="lru_fwd", grid=(N_HEADS,),
        in_specs=[_bs((None, seq, lh), lambda h: (1, 0, h)), _bs((None, seq, lh), lambda h: (2, 0, h)),
                  _bs((None, SMALL_ROWS, lh), lambda h: (h, 0, 0)), _bs((1, lh), lambda h: (0, h)),
                  _bs((None, lh, 4 * lh), lambda h: (h, 0, 0)),
                  ANY, _bs((8, 128), lambda h: (0, 0))],
        out_specs=[_bs((seq, lh), lambda h: (0, first_rec_block + h)),
                   _bs((seq + 2 * hal, lh), lambda h: (0, h)), _bs((seq + 2 * hal, lh), lambda h: (0, h))],
        out_shape=[jax.ShapeDtypeStruct((seq, d_model), BF16), jax.ShapeDtypeStruct((seq + 2 * hal, dl), F32),
                   jax.ShapeDtypeStruct((seq + 2 * hal, dl), F32)],
        scratch_shapes=[pltpu.VMEM((seq + 2 * hal, lh), F32), pltpu.VMEM((2, seq, lh), F32),
                        pltpu.VMEM((2, seq, lh), F32)],
        input_output_aliases={5: 0},
        compiler_params=_params(1),
    )(p3, p3, pack, conv_b, wcat, y_in, token)


def _lru_bwd(p3, dy, h0p, h1p, dproj_in, pack, conv_b, wcat, token, seq, d_model):
    dl = d_model // 2
    lh = dl // N_HEADS
    t = min(128, seq)
    n_chunks = seq // t
    seg = seq // 8
    hal = CONV_HALO
    first_rec_block = (d_model - dl) // lh
    tn_dims = (((0,), (0,)), ((), ()))
    nt_dims = (((1,), (1,)), ((), ()))

    def body(ur_ref, ug_ref, dy_ref, h0_ref, h1_ref, pk_ref, cb_ref, wcat_ref, tok_ref, din_ref,
             dproj_ref, dpk_ref, dcb_ref, dwcat_ref,
             upad, a_scr, dh_scr, g_scr, dxc_pad, dpr_ref, out_sems, gate_scr):
        del din_ref, tok_ref
        zeros = jnp.zeros((hal, lh), F32)
        for ref in (upad, dxc_pad):
            ref[0:hal, :] = zeros
            ref[hal + seq:hal + seq + hal, :] = zeros
        for n in range(2):
            a_scr[n, 0:hal, :] = zeros
            a_scr[n, hal + seq:hal + seq + hal, :] = zeros

        def fill(ci, _):
            r0 = pl.multiple_of(ci * t, t)
            upad[pl.ds(hal + r0, t), :] = ur_ref[pl.ds(r0, t), :]
            return 0

        lax.fori_loop(0, n_chunks, fill, 0)
        cb = cb_ref[...]
        lam = [pk_ref[pl.ds(8 + n, 1), :] for n in range(2)]
        sp = [_softplus(-lam[n]) for n in range(2)]

        def chunk1(ci, _):
            r0 = pl.multiple_of(ci * t, t)
            xc, _ext = _conv_chunk(upad, pk_ref, cb, r0, t)
            _, pre = _gate_preacts(xc, wcat_ref)
            for n in range(2):
                r, i, a, m = _gates(pre, n, pk_ref, sp)
                a_scr[n, pl.ds(hal + r0, t), :] = a
                for q, v in enumerate((r, i, m)):
                    gate_scr[3 * n + q, pl.ds(r0, t), :] = v
            hsum = h0_ref[pl.ds(hal + r0, t), :] + h1_ref[pl.ds(hal + r0, t), :]
            gl, dgl = _gelu_and_grad(ug_ref[pl.ds(r0, t), :])
            dyv = dy_ref[pl.ds(r0, t), :]
            dh_scr[pl.ds(r0, t), :] = dyv * gl
            dpr_ref[1, pl.ds(r0, t), :] = ((dyv * hsum) * dgl).astype(BF16)
            return 0

        lax.fori_loop(0, n_chunks, chunk1, 0, unroll=2)

        def load(n):
            def get(k):
                r0 = pl.multiple_of(k * 8, 8)
                if n == 0:
                    coef = _shift(a_scr[0, pl.ds(pl.multiple_of(hal + r0, 8), 16), :], 1)[0:8, :]
                else:
                    coef = _shift(a_scr[1, pl.ds(pl.multiple_of(hal + r0 - 8, 8), 16), :], -1)[8:16, :]
                return coef, dh_scr[pl.ds(r0, 8), :]
            return get

        def store(n):
            def put(k, v):
                g_scr[n, pl.ds(pl.multiple_of(k * 8, 8), 8), :] = v
            return put

        _tile_scan(seq // 8, lh, [load(1), load(0)], [store(1), store(0)])

        dwcat_ref[...] = jnp.zeros((lh, 4 * lh), F32)

        def chunk3(ci, carry):
            dba, dbi, dlam, dcb = carry
            r0 = pl.multiple_of(ci * t, t)
            xc, _ext = _conv_chunk(upad, pk_ref, cb, r0, t)
            xcb = xc.astype(BF16)
            dxc = jnp.zeros((t, lh), F32)
            dba, dbi, dlam = list(dba), list(dbi), list(dlam)
            dpre = []
            for n in range(2):
                r, i, m = (gate_scr[3 * n + q, pl.ds(r0, t), :] for q in range(3))
                a = a_scr[n, pl.ds(hal + r0, t), :]
                hext = (h0_ref if n == 0 else h1_ref)[pl.ds(r0, t + 2 * hal), :]
                hprev = _shift(hext, -1 if n == 0 else 1)[hal:hal + t, :]
                gb = g_scr[n, pl.ds(r0, t), :]
                da = gb * hprev
                dm = gb * i * xc
                di = gb * m * xc
                dxc = dxc + gb * (m * i)
                dlog_a = da * a - dm * (a * a) / m
                dr = dlog_a * (-RG_C * sp[n])
                dlam[n] = dlam[n] + jnp.sum(dlog_a * r, axis=0, keepdims=True)
                dpr = dr * r * (1.0 - r)
                dpi = di * i * (1.0 - i)
                dba[n] = dba[n] + jnp.sum(dpr, axis=0, keepdims=True)
                dbi[n] = dbi[n] + jnp.sum(dpi, axis=0, keepdims=True)
                dpre += [dpr.astype(BF16), dpi.astype(BF16)]
            dpre = jnp.concatenate(dpre, axis=1)
            dwcat_ref[...] += lax.dot_general(xcb, dpre, tn_dims, preferred_element_type=F32)
            dxc = dxc + lax.dot_general(dpre, wcat_ref[...], nt_dims, preferred_element_type=F32)
            dxc_pad[pl.ds(hal + r0, t), :] = dxc
            dcb = dcb + jnp.sum(dxc, axis=0, keepdims=True)
            return tuple(dba), tuple(dbi), tuple(dlam), dcb

        zr = jnp.zeros((1, lh), F32)
        def chunk3_pair(cj, carry):
            return chunk3(2 * cj + 1, chunk3(2 * cj, carry))

        dba, dbi, dlam, dcb = lax.fori_loop(0, n_chunks // 2, chunk3_pair, ((zr, zr), (zr, zr), (zr, zr), zr))
        dcb_ref[...] = dcb
        for n in range(2):
            dpk_ref[pl.ds(4 + n, 1), :] = dba[n]
            dpk_ref[pl.ds(6 + n, 1), :] = dbi[n]
            dpk_ref[pl.ds(8 + n, 1), :] = dlam[n] * (RG_C * jax.nn.sigmoid(-lam[n]))
        dpk_ref[pl.ds(10, SMALL_ROWS - 10), :] = jnp.zeros((SMALL_ROWS - 10, lh), F32)

        def chunk4(ci, dtap):
            r0 = pl.multiple_of(ci * t, t)
            gext = dxc_pad[pl.ds(r0, t + 2 * hal), :]
            uext = upad[pl.ds(r0, t + 2 * hal), :]
            gmid = gext[hal:hal + t, :]
            du = pk_ref[pl.ds(1, 1), :] * gext
            for k in (0, 2, 3):
                du = du + pk_ref[pl.ds(k, 1), :] * _shift(gext, 1 - k)
            dpr_ref[0, pl.ds(r0, t), :] = du[hal:hal + t, :].astype(BF16)
            out = []
            for k in range(4):
                usl = _shift(uext, k - 1)[hal:hal + t, :]
                out.append(dtap[k] + jnp.sum(gmid * usl, axis=0, keepdims=True))
            return tuple(out)

        dtap = lax.fori_loop(0, n_chunks, chunk4, (zr, zr, zr, zr))
        for k in range(4):
            dpk_ref[pl.ds(k, 1), :] = dtap[k]

        head = pl.program_id(0)
        outs = [pltpu.make_async_copy(
            dpr_ref.at[b], dproj_ref.at[:, pl.ds(pl.multiple_of((1 + b) * dl + head * lh, lh), lh)], out_sems.at[b])
            for b in range(2)]
        for cp in outs:
            cp.start()
        for cp in outs:
            cp.wait()

    return pl.pallas_call(
        body, name="lru_bwd", grid=(N_HEADS,),
        in_specs=[_bs((None, seq, lh), lambda h: (1, 0, h)), _bs((None, seq, lh), lambda h: (2, 0, h)),
                  _bs((seq, lh), lambda h: (0, first_rec_block + h)),
                  _bs((seq + 2 * hal, lh), lambda h: (0, h)), _bs((seq + 2 * hal, lh), lambda h: (0, h)),
                  _bs((None, SMALL_ROWS, lh), lambda h: (h, 0, 0)), _bs((1, lh), lambda h: (0, h)),
                  _bs((None, lh, 4 * lh), lambda h: (h, 0, 0)),
                  _bs((8, 128), lambda h: (0, 0)), ANY],
        out_specs=[ANY, _bs((None, SMALL_ROWS, lh), lambda h: (h, 0, 0)),
                   _bs((1, lh), lambda h: (0, h)), _bs((None, lh, 4 * lh), lambda h: (h, 0, 0))],
        out_shape=[jax.ShapeDtypeStruct((seq, 3 * dl), BF16), jax.ShapeDtypeStruct((N_HEADS, SMALL_ROWS, lh), F32),
                   jax.ShapeDtypeStruct((1, dl), F32), jax.ShapeDtypeStruct((N_HEADS, lh, 4 * lh), F32)],
        scratch_shapes=[pltpu.VMEM((seq + 2 * hal, lh), F32), pltpu.VMEM((2, seq + 2 * hal, lh), F32),
                        pltpu.VMEM((seq, lh), F32), pltpu.VMEM((2, seq, lh), F32),
                        pltpu.VMEM((seq + 2 * hal, lh), F32), pltpu.VMEM((2, seq, lh), BF16),
                        pltpu.SemaphoreType.DMA((2,)), pltpu.VMEM((6, seq, lh), F32)],
        input_output_aliases={9: 0},
        compiler_params=_params(1),
    )(p3, p3, dy, h0p, h1p, pack, conv_b, wcat, token, dproj_in)


class _tiles:
    def __init__(self, seq, d_model, d_ff):
        self.rows = min(1024, seq)
        self.ln_rows = min(256, seq)
        self.ff_cols = min(1024, d_ff)
        self.ff_split = 4
        self.ff_k = min(2048, d_ff)
        self.grad_rows = 512


def _ln_loss_bwd(ffn, x1, tgt, g, b, tr):
    seq, d = ffn.shape

    def body(f_ref, x_ref, t_ref, g_ref, b_ref, dz_ref, dzb_ref, dg_ref, db_ref, loss_ref):
        i = pl.program_id(0)
        gv = g_ref[...]
        z = ALPHA * x_ref[...] + f_ref[...]
        y, xhat, rstd = _ln_fwd(z, gv, b_ref[...])
        err = y - t_ref[...]
        part = 0.5 * jnp.sum(jnp.mean(err * err, axis=-1, keepdims=True), axis=0, keepdims=True)
        dz, dg, db = _ln_bwd(err * (1.0 / d), xhat, rstd, gv)
        dz_ref[...] = dz
        dzb_ref[...] = dz.astype(BF16)
        _acc_rows(dg_ref, i == 0, dg)
        _acc_rows(db_ref, i == 0, db)
        _acc_rows(loss_ref, i == 0, jnp.broadcast_to(part, (8, 128)))

    row = _bs((tr, d), lambda i: (i, 0))
    vec = _bs((1, d), lambda i: (0, 0))
    return pl.pallas_call(
        body, name="ln_ffn_loss", grid=(seq // tr,), in_specs=[row, row, row, vec, vec],
        out_specs=[row, row, vec, vec, _bs((8, 128), lambda i: (0, 0))],
        out_shape=[jax.ShapeDtypeStruct((seq, d), F32), jax.ShapeDtypeStruct((seq, d), BF16),
                   jax.ShapeDtypeStruct((1, d), F32), jax.ShapeDtypeStruct((1, d), F32),
                   jax.ShapeDtypeStruct((8, 128), F32)],
        compiler_params=_params(1),
    )(ffn, x1, tgt, g, b)


def _ln_bwd_side(dx_branch, dres, z, g, b, n_steps):
    seq, d = z.shape
    tr = seq // n_steps

    def fn(step, ins, outs):
        a_ref, r_ref, z_ref, g_ref, b_ref = ins
        dz_ref, dzb_ref, dg_ref, db_ref = outs
        gv = g_ref[...]
        _, xhat, rstd = _ln_fwd(z_ref[...], gv, b_ref[...])
        dz, dg, db = _ln_bwd(ALPHA * r_ref[...] + a_ref[...], xhat, rstd, gv)
        dz_ref[...] = dz
        dzb_ref[...] = dz.astype(BF16)
        _acc_rows(dg_ref, step == 0, dg)
        _acc_rows(db_ref, step == 0, db)

    row = ((tr, d), lambda s: (s, 0))
    vec = ((1, d), lambda s: (0, 0))
    shapes = [jax.ShapeDtypeStruct((seq, d), F32), jax.ShapeDtypeStruct((seq, d), BF16),
              jax.ShapeDtypeStruct((1, d), F32), jax.ShapeDtypeStruct((1, d), F32)]
    return [(dx_branch, *row), (dres, *row), (z, *row), (g, *vec), (b, *vec)], shapes, [row, row, vec, vec], fn


def _to_bf16(name, a, token):
    rows, cols = a.shape
    tr = min(512, rows)

    def body(a_ref, tok_ref, o_ref):
        del tok_ref
        o_ref[...] = a_ref[...].astype(BF16)

    return pl.pallas_call(
        body, name=name, grid=(rows // tr,),
        in_specs=[_bs((tr, cols), lambda i: (i, 0)), _bs((8, 128), lambda i: (0, 0))],
        out_specs=_bs((tr, cols), lambda i: (i, 0)), out_shape=jax.ShapeDtypeStruct((rows, cols), BF16),
        compiler_params=_params(1),
    )(a, token)


def _sum_blocks(name, parts):
    def body(p_ref, o_ref):
        acc = p_ref[0]
        for s in range(1, parts.shape[0]):
            acc = acc + p_ref[s]
        o_ref[...] = acc

    return pl.pallas_call(body, name=name, out_shape=jax.ShapeDtypeStruct(parts.shape[1:], F32))(parts)


def _adamw_values(w, g, m, v):
    m = ADAM_B1 * m + (1.0 - ADAM_B1) * g
    v = ADAM_B2 * v + (1.0 - ADAM_B2) * (g * g)
    m_hat = m / (1.0 - ADAM_B1 ** ADAM_STEP)
    v_hat = v / (1.0 - ADAM_B2 ** ADAM_STEP)
    delta = -ADAM_LR * (m_hat / (jnp.sqrt(v_hat) + ADAM_EPS) + ADAM_WD * w)
    return delta, m, v


def _sum_adamw(name, own, parts, w, m, v):
    rows, cols = w.shape
    n_parts = parts.shape[0]
    tr = rows
    min_rows = 8 if parts.dtype == F32 else 16
    while tr * cols * 4 > 1024 * 1024 and tr % (2 * min_rows) == 0:
        tr //= 2

    def body(*refs):
        if own is None:
            p_ref, w_ref, m_ref, v_ref, g_ref, d_ref, mo_ref, vo_ref = refs
            g = p_ref[0].astype(F32)
            rest = range(1, n_parts)
        else:
            o_ref, p_ref, w_ref, m_ref, v_ref, g_ref, d_ref, mo_ref, vo_ref = refs
            g = o_ref[...]
            rest = range(n_parts)
        for s in rest:
            g = g + p_ref[s].astype(F32)
        delta, mn, vn = _adamw_values(w_ref[...], g, m_ref[...], v_ref[...])
        g_ref[...] = g
        d_ref[...] = delta
        mo_ref[...] = mn
        vo_ref[...] = vn

    spec = _bs((tr, cols), lambda i: (i, 0))
    lead = [] if own is None else [own]
    return pl.pallas_call(
        body, name=name, grid=(rows // tr,),
        in_specs=[spec] * len(lead) + [_bs((n_parts, tr, cols), lambda i: (0, i, 0)), spec, spec, spec],
        out_specs=[spec] * 4, out_shape=[jax.ShapeDtypeStruct((rows, cols), F32)] * 4,
        compiler_params=_params(1),
    )(*lead, parts, w, m, v)


def _rows128(a):
    return a.reshape(-1, 128)


def kernel(x, ln_mix_g, ln_mix_b, w_in, w_pool, pool_scale, conv_w, conv_b, w_rg_a, b_rg_a, w_rg_i, b_rg_i, rg_lambda, w_out, ln_ffn_g, ln_ffn_b, w_mlp_in, w_mlp_out, loss_target, m_ln_mix_g, m_ln_mix_b, m_w_in, m_w_pool, m_pool_scale, m_conv_w, m_conv_b, m_w_rg_a, m_b_rg_a, m_w_rg_i, m_b_rg_i, m_rg_lambda, m_w_out, m_ln_ffn_g, m_ln_ffn_b, m_w_mlp_in, m_w_mlp_out, v_ln_mix_g, v_ln_mix_b, v_w_in, v_w_pool, v_pool_scale, v_conv_w, v_conv_b, v_w_rg_a, v_b_rg_a, v_w_rg_i, v_b_rg_i, v_rg_lambda, v_w_out, v_ln_ffn_g, v_ln_ffn_b, v_w_mlp_in, v_w_mlp_out):
    seq, d_model = x.shape[1], x.shape[2]
    dh = d_model // 2
    lh = dh // N_HEADS
    pg = dh // len(POOL_WINDOWS)
    d_ff = w_mlp_in.shape[2] * N_DEV
    assert lh == 128 and conv_w.shape[3] == lh and w_pool.shape[2] * N_DEV == pg

    xs = x[0]
    tgt = loss_target[0]

    def small_pack(cw, ba, bi, lam):
        return jnp.concatenate([cw.reshape(4, lh), ba.reshape(2, lh), bi.reshape(2, lh), lam.reshape(2, lh),
                                jnp.zeros((SMALL_ROWS - 10, lh), F32)], axis=0)

    pack_mine = small_pack(conv_w, b_rg_a, b_rg_i, rg_lambda)
    pack_bits = lax.bitcast_convert_type(pack_mine, BF16).reshape(1, SMALL_ROWS, 2 * lh)
    win_gather = _SplitGather("gather_w_in", [(w_in[0], 1), (w_pool[0], 1), (pack_bits, 0)], BF16, after=pack_mine)
    wout_gather = _SplitGather("gather_w_out", [(w_out[0], 0)], BF16, after=win_gather.token)
    w1_gather = _SplitGather("gather_w_mlp_in", [(w_mlp_in[0], 1)], BF16, after=wout_gather.token)
    w2_gather = _SplitGather("gather_w_mlp_out", [(w_mlp_out[0], 0)], BF16, after=w1_gather.token)
    xb = _to_bf16("x_bf16", x[0], w2_gather.token)
    win_full, wpool_full, pack_bits_full = win_gather.wait(after=win_gather.relay(after=xb))
    pack_full = lax.bitcast_convert_type(pack_bits_full.reshape(N_DEV, SMALL_ROWS, lh, 2), F32)
    wcat = jnp.concatenate([w_rg_a[0, 0], w_rg_i[0, 0], w_rg_a[0, 1], w_rg_i[0, 1]], axis=-1).astype(BF16)
    vec = lambda i, j, k: (0, 0)
    row_full = lambda i, j, k: (i, 0)

    def after(token):
        return (token, _sp((8, 128), vec))

    def sds(shape, dtype):
        return jax.ShapeDtypeStruct(shape, dtype)

    def plain_epi(acc, i, ex, out):
        out[0][...] = acc

    def bf16_epi(acc, i, ex, out):
        out[0][...] = acc.astype(BF16)

    t = _tiles(seq, d_model, d_ff)

    (p3,) = _matmul(
        "proj", xb, win_full, _sp((t.rows, d_model), lambda i, j, k: (i, 0)), _sp((d_model, dh), lambda i, j, k: (0, j)),
        grid=(seq // t.rows, 3, 1),
        out_shape=[sds((3, seq, dh), F32)], out_specs=[_sp((None, t.rows, dh), lambda i, j, k: (j, i, 0))],
        epilogue=plain_epi)

    d_pool, y_half = _pool_fwd(p3, wpool_full, pool_scale, seq, d_model)
    y, h0p, h1p = _lru_fwd(p3, y_half, pack_full, conv_b, wcat, wout_gather.relay(after=y_half), seq, d_model)
    (wout_full,) = wout_gather.wait(after=y)
    relay_token = w1_gather.relay(after=wout_full)

    mix_rows = 2 * t.ln_rows

    def mix_epi(acc, i, ex, out):
        x_ref, g_ref, b_ref = ex[:3]
        for part in range(2):
            rows = pl.ds(part * t.ln_rows, t.ln_rows)
            z = ALPHA * x_ref[rows, :] + acc[part * t.ln_rows:(part + 1) * t.ln_rows, :]
            x1, _, _ = _ln_fwd(z, g_ref[...], b_ref[...])
            out[0][rows, :] = z
            out[1][rows, :] = x1
            out[2][rows, :] = x1.astype(BF16)

    z1, x1, x1b = _matmul(
        "mix_out", y, wout_full, _sp((mix_rows, d_model), row_full), _sp((d_model, d_model), vec, single=True),
        grid=(seq // mix_rows, 1, 1),
        extras=[(xs, _sp((mix_rows, d_model), row_full)), (ln_mix_g, _sp((1, d_model), vec)),
                (ln_mix_b, _sp((1, d_model), vec)), after(relay_token)],
        out_shape=[sds((seq, d_model), F32), sds((seq, d_model), F32), sds((seq, d_model), BF16)],
        out_specs=[_sp((mix_rows, d_model), row_full)] * 3, epilogue=mix_epi)
    (w1_full,) = w1_gather.wait(after=x1b)

    def mlp_in_epi(acc, i, ex, out, cols):
        h = jnp.maximum(acc, 0.0)
        out[0][:, cols] = (h * h).astype(BF16)
        out[1][:, cols] = (2.0 * h).astype(BF16)

    hmid, dact = _matmul(
        "mlp_in", x1b, w1_full, _sp((seq, d_model), vec, single=True), _sp((d_model, t.ff_cols), lambda i, j, k: (0, j)),
        grid=(seq // t.rows, d_ff // t.ff_cols, 1), j_outer=True, a_tile=t.rows,
        out_shape=[sds((seq, d_ff), BF16)] * 2, out_specs=[_sp((t.rows, t.ff_cols), lambda i, j, k: (i, j))] * 2,
        epilogue=mlp_in_epi, n_split=t.ff_split)
    (w2_full,) = w2_gather.wait(after=w2_gather.relay(after=hmid))

    (ffn,) = _matmul(
        "mlp_out", hmid, w2_full, _sp((t.rows, t.ff_k), lambda i, j, k: (i, k)),
        _sp((t.ff_k, d_model), lambda i, j, k: (k, 0)),
        grid=(seq // t.rows, 1, d_ff // t.ff_k),
        out_shape=[sds((seq, d_model), F32)], out_specs=[_sp((t.rows, d_model), row_full)])
    dz2, dz2b, g_ffn_g, g_ffn_b, loss_part = _ln_loss_bwd(ffn, x1, tgt, ln_ffn_g, ln_ffn_b, t.ln_rows)

    (g_w2,) = _matmul(
        "grad_w_mlp_out", hmid, dz2b, _sp((seq, t.grad_rows), lambda i, j, k: (0, i)),
        _sp((seq, d_model), vec, single=True),
        grid=(d_ff // t.grad_rows, 1, 1), ta=True,
        out_shape=[sds((d_ff, d_model), BF16)], out_specs=[_sp((t.grad_rows, d_model), row_full)],
        epilogue=bf16_epi)
    scatter_w2 = _SplitReduceScatter("scatter_w_mlp_out", [g_w2.reshape(N_DEV, d_ff // N_DEV, d_model)])

    def dpre_epi(acc, i, ex, out, cols):
        out[0][:, cols] = (acc * ex[0][:, cols].astype(F32)).astype(BF16)

    (dpre,) = _matmul(
        "mlp_dpre", dz2b, w2_full, _sp((seq, d_model), vec, single=True),
        _sp((t.ff_cols, d_model), lambda i, j, k: (j, 0)),
        grid=(seq // t.rows, d_ff // t.ff_cols, 1), j_outer=True, tb=True, a_tile=t.rows,
        extras=[(dact, _sp((t.rows, t.ff_cols), lambda i, j, k: (i, j))), after(scatter_w2.token)],
        out_shape=[sds((seq, d_ff), BF16)], out_specs=[_sp((t.rows, t.ff_cols), lambda i, j, k: (i, j))],
        epilogue=dpre_epi, n_split=t.ff_split)
    token_w2 = scatter_w2.combine_and_send(after=dpre)

    (dx1_mlp,) = _matmul(
        "mlp_dx", dpre, w1_full, _sp((t.rows, t.ff_k), lambda i, j, k: (i, k)),
        _sp((d_model, t.ff_k), lambda i, j, k: (0, k)),
        grid=(seq // t.rows, 1, d_ff // t.ff_k), tb=True, extras=[after(token_w2)],
        out_shape=[sds((seq, d_model), F32)], out_specs=[_sp((t.rows, d_model), row_full)])
    def block_epi(acc, i, ex, out):
        out[0][0] = acc.astype(BF16)

    fs = d_ff // N_DEV
    g_w1, dz1, dz1b, g_mix_g, g_mix_b = _matmul(
        "grad_w_mlp_in", x1b, dpre, _sp((seq, d_model), vec, single=True),
        _sp((seq, fs), lambda i, j, k: (0, j)),
        grid=(d_model // t.grad_rows, N_DEV, 1), j_outer=True, ta=True, a_tile=t.grad_rows,
        out_shape=[sds((N_DEV, d_model, fs), BF16)],
        out_specs=[_sp((1, t.grad_rows, fs), lambda i, j, k: (j, i, 0))], epilogue=block_epi,
        side=_ln_bwd_side(dx1_mlp, dz2, z1, ln_mix_g, ln_mix_b, d_model // t.grad_rows * N_DEV))

    (dy,) = _matmul(
        "mix_dy", dz1b, wout_full, _sp((seq, d_model), vec, single=True),
        _sp((dh, d_model), lambda i, j, k: (j, 0)),
        grid=(seq // t.rows, 2, 1), j_outer=True, tb=True, a_tile=t.rows,
        out_shape=[sds((seq, d_model), F32)], out_specs=[_sp((t.rows, dh), lambda i, j, k: (i, j))],
        epilogue=plain_epi)
    (g_wout,) = _matmul(
        "grad_w_out", y, dz1b, _sp((seq, t.grad_rows), lambda i, j, k: (0, i)), _sp((seq, d_model), vec, single=True),
        grid=(d_model // t.grad_rows, 1, 1), ta=True,
        out_shape=[sds((d_model, d_model), BF16)], out_specs=[_sp((t.grad_rows, d_model), row_full)],
        epilogue=bf16_epi)
    scatter_w1 = _SplitReduceScatter("scatter_w_mlp_in", [g_w1, g_wout.reshape(N_DEV, d_model // N_DEV, d_model)])

    dproj_pool, g_wpool, g_pscale = _pool_bwd(d_pool, dy, wpool_full, pool_scale, scatter_w1.token, seq, d_model)
    token_w1 = scatter_w1.combine_and_send(after=dproj_pool)
    dproj, g_pack, g_convb, g_wcat = _lru_bwd(p3, dy, h0p, h1p, dproj_pool, pack_full, conv_b, wcat,
                                              token_w1, seq, d_model)
    g_wa = jnp.stack([g_wcat[:, :, 0:lh], g_wcat[:, :, 2 * lh:3 * lh]])
    g_wi = jnp.stack([g_wcat[:, :, lh:2 * lh], g_wcat[:, :, 3 * lh:4 * lh]])

    rep_parts = [_rows128(g_wa), _rows128(g_wi), _rows128(g_mix_g), _rows128(g_mix_b), _rows128(g_ffn_g),
                 _rows128(g_ffn_b), _rows128(g_pscale), _rows128(g_convb)]
    rep_rows = [p.shape[0] for p in rep_parts]
    n_rep = sum(rep_rows)
    small = jnp.concatenate(rep_parts + [_rows128(g_pack), loss_part], axis=0)
    small_gather = _SplitGather("gather_small_grads", [(small[None], 0)], F32, after=small)

    ws = 3 * dh // N_DEV

    def pair_epi(acc, i, ex, out):
        out[0][0] = acc[:, :ws].astype(BF16)
        out[0][1] = acc[:, ws:].astype(BF16)

    (g_win,) = _matmul(
        "grad_w_in", xb, dproj, _sp((seq, t.grad_rows), lambda i, j, k: (0, i)),
        _sp((seq, 3 * dh), vec, single=True),
        grid=(d_model // t.grad_rows, N_DEV // 2, 1), ta=True, b_tile=2 * ws, extras=[after(small_gather.token)],
        out_shape=[sds((N_DEV, d_model, ws), BF16)],
        out_specs=[_sp((2, t.grad_rows, ws), lambda i, j, k: (j, i, 0))], epilogue=pair_epi)
    scatter_mix = _SplitReduceScatter(
        "scatter_mixer", [g_win, g_wpool.reshape(N_DEV, pg // N_DEV * len(POOL_WINDOWS), pg)])

    def adam_big(name, own_landed, w, m, v):
        own, landed = own_landed
        shp = w.shape
        two = lambda a: a.reshape(-1, shp[-1])
        res = _sum_adamw(name, own, landed, two(w), two(m), two(v))
        return [r.reshape(shp) for r in res]

    (r_w2,) = scatter_w2.wait(after=scatter_mix.token)
    o_w2 = adam_big("adam_w_mlp_out", r_w2, w_mlp_out, m_w_mlp_out, v_w_mlp_out)
    token_mix = scatter_mix.combine_and_send(after=o_w2[0])

    def dx_epi(acc, i, ex, out):
        out[0][...] = ALPHA * ex[0][...] + acc

    (dx,) = _matmul(
        "grad_x", dproj, win_full, _sp((t.ln_rows * 2, 3 * dh), lambda i, j, k: (i, 0)),
        _sp((d_model, 3 * dh), vec, single=True),
        grid=(seq // (t.ln_rows * 2), 1, 1), tb=True,
        extras=[(dz1, _sp((t.ln_rows * 2, d_model), row_full)), after(token_mix)],
        out_shape=[sds((seq, d_model), F32)], out_specs=[_sp((t.ln_rows * 2, d_model), row_full)],
        epilogue=dx_epi)
    r_w1, r_wout = scatter_w1.wait(after=dx)
    o_w1 = adam_big("adam_w_mlp_in", r_w1, w_mlp_in, m_w_mlp_in, v_w_mlp_in)
    o_wout = adam_big("adam_w_out", r_wout, w_out, m_w_out, v_w_out)
    r_win, r_wpool = scatter_mix.wait(after=o_wout[0])
    o_win = adam_big("adam_w_in", r_win, w_in, m_w_in, v_w_in)
    o_wpool = adam_big("adam_w_pool", r_wpool, w_pool, m_w_pool, v_w_pool)

    small_gather.relay(after=o_win[0])
    (small_all,) = small_gather.wait(after=o_wpool[0])

    rep_w = [w_rg_a, w_rg_i, ln_mix_g, ln_mix_b, ln_ffn_g, ln_ffn_b, pool_scale, conv_b]
    rep_m = [m_w_rg_a, m_w_rg_i, m_ln_mix_g, m_ln_mix_b, m_ln_ffn_g, m_ln_ffn_b, m_pool_scale, m_conv_b]
    rep_v = [v_w_rg_a, v_w_rg_i, v_ln_mix_g, v_ln_mix_b, v_ln_ffn_g, v_ln_ffn_b, v_pool_scale, v_conv_b]
    cat = lambda arrs: jnp.concatenate([_rows128(a) for a in arrs], axis=0)
    o_rep = _sum_adamw("adam_replicated", None, small_all, cat(rep_w), cat(rep_m), cat(rep_v))

    my_idx = _dev_index(_where_am_i())
    head_parts = lax.dynamic_slice_in_dim(small_all, n_rep + my_idx * SMALL_ROWS, SMALL_ROWS, axis=1)
    o_head = _sum_adamw("adam_head", None, head_parts, pack_mine,
                        small_pack(m_conv_w, m_b_rg_a, m_b_rg_i, m_rg_lambda),
                        small_pack(v_conv_w, v_b_rg_a, v_b_rg_i, v_rg_lambda))

    def unpack_rep(packed):
        out, r = [], 0
        for wgt, rows in zip(rep_w, rep_rows):
            out.append(packed[r:r + rows].reshape(wgt.shape))
            r += rows
        return out

    def unpack_head(packed):
        return [packed[0:4].reshape(conv_w.shape), packed[4:6].reshape(b_rg_a.shape),
                packed[6:8].reshape(b_rg_i.shape), packed[8:10].reshape(rg_lambda.shape)]

    loss = _sum_blocks("loss_sum", small_all[:, n_rep + N_HEADS * SMALL_ROWS:, :])[0, 0]

    outs = [loss, dx[None]]
    for kind in range(4):
        ra, ri, mg, mb, fg, fb, ps, cb = unpack_rep(o_rep[kind])
        cw, ba, bi, lam = unpack_head(o_head[kind])
        outs += [mg, mb, o_win[kind], o_wpool[kind], ps, cw, cb, ra, ba, ri, bi, lam, o_wout[kind], fg, fb,
                 o_w1[kind], o_w2[kind]]
    return tuple(outs)
```

```python
import functools

import jax
import jax.numpy as jnp
from jax import lax
from jax.experimental import pallas as pl
from jax.experimental.pallas import tpu as pltpu

F32 = jnp.float32
BF16 = jnp.bfloat16
MESH = pl.DeviceIdType.MESH
ANY = pl.BlockSpec(memory_space=pl.ANY)

N_DEV = 8
POOL_WINDOWS = (2, 4, 8, 16)
N_HEADS = 8
RG_C = 8.0
LN_EPS = 1e-5
ALPHA = 2.0 ** 0.25
ADAM_LR = 0.001
ADAM_B1 = 0.9
ADAM_B2 = 0.999
ADAM_EPS = 1e-08
ADAM_WD = 0.01
ADAM_STEP = 10

VMEM_LIMIT = 56 * 1024 * 1024
WIN_HALO = 16
CONV_HALO = 8
SMALL_ROWS = 16


def _params(n_grid):
    return pltpu.CompilerParams(dimension_semantics=("arbitrary",) * n_grid, vmem_limit_bytes=VMEM_LIMIT)


def _shift(v, j):
    n = v.shape[0]
    s = (-j) % n
    return v if s == 0 else pltpu.roll(v, s, 0)


def _sigmoid(x):
    return 0.5 * jnp.tanh(0.5 * x) + 0.5


def _softplus(z):
    e = jnp.exp(-jnp.abs(z))
    u = 1.0 + e
    log1p = jnp.where(u == 1.0, e, jnp.log(u) * (e / jnp.where(u == 1.0, 1.0, u - 1.0)))
    return jnp.maximum(z, 0.0) + log1p


_GELU_C = 0.7978845608028654
_GELU_K = 0.044715


def _gelu_and_grad(x):
    x2 = x * x
    t = jnp.tanh(_GELU_C * (x + _GELU_K * x * x2))
    g = 0.5 * x * (1.0 + t)
    dg = 0.5 * (1.0 + t) + 0.5 * x * (1.0 - t * t) * (_GELU_C * (1.0 + 3.0 * _GELU_K * x2))
    return g, dg


def _ln_fwd(z, g, b):
    mu = jnp.mean(z, axis=-1, keepdims=True)
    zc = z - mu
    var = jnp.mean(zc * zc, axis=-1, keepdims=True)
    rstd = lax.rsqrt(var + LN_EPS)
    xhat = zc * rstd
    return xhat * g + b, xhat, rstd


def _ln_bwd(dy, xhat, rstd, g):
    dxhat = dy * g
    m1 = jnp.mean(dxhat, axis=-1, keepdims=True)
    m2 = jnp.mean(dxhat * xhat, axis=-1, keepdims=True)
    dz = rstd * (dxhat - m1 - xhat * m2)
    dg = jnp.sum(dy * xhat, axis=0, keepdims=True)
    db = jnp.sum(dy, axis=0, keepdims=True)
    return dz, dg, db


def _acc_rows(ref, first, val):
    @pl.when(first)
    def _():
        ref[...] = val

    @pl.when(jnp.logical_not(first))
    def _():
        ref[...] += val


def _sp(shape, fn, single=False):
    return shape, fn, single


def _matmul(name, a, b, a_spec, b_spec, *, grid, j_outer=False, ta=False, tb=False, extras=(), out_shape, out_specs,
            epilogue=None, n_split=1, side=None):
    ni, nj, nk = grid
    n_ex = len(extras)
    dims = (((0 if ta else 1,), (1 if tb else 0,)), ((), ()))
    side_in, side_shape, side_out, side_fn = side if side is not None else ((), (), (), None)
    n_main_out = len(out_shape)
    inner = ni if j_outer else nj

    def mk(spec):
        shape, fn, single = spec
        index = (lambda g0, g1, g2: fn(g1, g0, g2)) if j_outer else fn
        return pl.BlockSpec(shape, index, pipeline_mode=pl.Buffered(1)) if single else pl.BlockSpec(shape, index)

    def mk_side(block, fn):
        return pl.BlockSpec(block, lambda g0, g1, g2: fn(g0 * inner + g1))

    def body(a_ref, b_ref, *rest):
        ex_refs = rest[:n_ex]
        out_refs = rest[n_ex + len(side_in):n_ex + len(side_in) + n_main_out]
        if side_fn is not None:
            side_fn(pl.program_id(0) * inner + pl.program_id(1), rest[n_ex:n_ex + len(side_in)],
                    rest[n_ex + len(side_in) + n_main_out:])
        i = pl.program_id(1 if j_outer else 0)
        if n_split > 1:
            av = a_ref[...].astype(BF16)
            width = b_ref.shape[0 if tb else 1] // n_split
            for c in range(n_split):
                cols = pl.ds(c * width, width)
                bv = (b_ref[cols, :] if tb else b_ref[:, cols]).astype(BF16)
                epilogue(lax.dot_general(av, bv, dims, preferred_element_type=F32), i, ex_refs, out_refs, cols)
            return
        part = lax.dot_general(a_ref[...].astype(BF16), b_ref[...].astype(BF16), dims, preferred_element_type=F32)
        if nk == 1:
            epilogue(part, i, ex_refs, out_refs)
        else:
            @pl.when(pl.program_id(2) == 0)
            def _():
                out_refs[0][...] = part

            @pl.when(pl.program_id(2) > 0)
            def _():
                out_refs[0][...] += part

    return pl.pallas_call(
        body, name=name, grid=(nj, ni, nk) if j_outer else (ni, nj, nk),
        in_specs=[mk(a_spec), mk(b_spec)] + [mk(s) for _, s in extras] + [mk_side(blk, fn) for _, blk, fn in side_in],
        out_specs=[mk(s) for s in out_specs] + [mk_side(blk, fn) for blk, fn in side_out],
        out_shape=list(out_shape) + list(side_shape),
        compiler_params=_params(3),
    )(a, b, *[x for x, _ in extras], *[x for x, _, _ in side_in])


def _bs(shape, fn):
    return pl.BlockSpec(shape, fn)


def _where_am_i():
    x, y, c = lax.axis_index("x"), lax.axis_index("y"), lax.axis_index("c")
    return x, y, c


def _dev_index(p):
    return 4 * p[0] + 2 * p[1] + p[2]


def _slab(ref, axis, idx, size):
    sl = [slice(None)] * len(ref.shape)
    sl[axis] = pl.ds(idx * size, size)
    return ref.at[tuple(sl)]


HBM = pl.BlockSpec(memory_space=pltpu.HBM)
SEM = pl.BlockSpec(memory_space=pltpu.SEMAPHORE)
DATAFLOW = pltpu.SideEffectType.DATAFLOW_SIDE_EFFECTING


def _in_hbm(a):
    return pltpu.with_memory_space_constraint(a, pltpu.HBM)


def _token_shape():
    return jax.ShapeDtypeStruct((8, 128), F32)


def _split_start(name, n_sems, bufs, issue):
    nb = len(bufs)

    def body(*refs):
        issue(refs[:nb], refs[nb], refs[nb + 1])
        refs[-1][...] = jnp.zeros((8, 128), F32)

    outs = pl.pallas_call(
        body, name=name,
        out_shape=(pltpu.SemaphoreType.DMA((n_sems,)), pltpu.SemaphoreType.DMA((n_sems,)),
                   *[pltpu.HBM(b.shape, b.dtype) for b in bufs], _token_shape()),
        in_specs=[HBM] * nb, out_specs=(SEM, SEM, *[HBM] * nb, pl.BlockSpec(memory_space=pltpu.VMEM)),
        input_output_aliases={i: 2 + i for i in range(nb)},
        compiler_params=pltpu.CompilerParams(has_side_effects=DATAFLOW),
    )(*[_in_hbm(b) for b in bufs])
    return outs[0], outs[1], list(outs[2:2 + nb]), outs[-1]


def _split_relay(name, n_sems, sems, bufs, after, relay):
    nb = len(bufs)

    def body(*refs):
        relay(refs[:nb], refs[nb], refs[nb + 1], refs[nb + 3], refs[nb + 4])
        refs[-1][...] = jnp.zeros((8, 128), F32)

    outs = pl.pallas_call(
        body, name=name,
        out_shape=(pltpu.SemaphoreType.DMA((n_sems,)), pltpu.SemaphoreType.DMA((n_sems,)),
                   *[pltpu.HBM(b.shape, b.dtype) for b in bufs], _token_shape()),
        in_specs=[HBM] * nb + [SEM, SEM, ANY],
        out_specs=(SEM, SEM, *[HBM] * nb, pl.BlockSpec(memory_space=pltpu.VMEM)),
        input_output_aliases={i: 2 + i for i in range(nb)},
        compiler_params=pltpu.CompilerParams(has_side_effects=DATAFLOW),
    )(*bufs, sems[0], sems[1], after)
    return outs[0], outs[1], list(outs[2:2 + nb]), outs[-1]


def _split_wait(name, sems, bufs, after, finish):
    nb = len(bufs)

    def body(*refs):
        finish(refs[:nb], refs[nb], refs[nb + 1])

    outs = pl.pallas_call(
        body, name=name, out_shape=[pltpu.HBM(b.shape, b.dtype) for b in bufs],
        in_specs=[HBM] * nb + [SEM, SEM, ANY], out_specs=[HBM] * nb,
        input_output_aliases={i: i for i in range(nb)},
        compiler_params=pltpu.CompilerParams(has_side_effects=DATAFLOW),
    )(*bufs, sems[0], sems[1], after)
    return list(outs)


def _place(name, items, dtype, after):
    ids = jnp.reshape(_dev_index(_where_am_i()), (1,)).astype(jnp.int32)
    outs = []
    for a, (shard, axis) in enumerate(items):
        rows, cols = shard.shape[-2], shard.shape[-1]
        tr = rows
        while tr * cols * shard.dtype.itemsize > 4 * 1024 * 1024 and tr % 32 == 0:
            tr //= 2
        nt = rows // tr
        full = list(shard.shape)
        full[axis] *= N_DEV
        if shard.ndim == 2 and axis == 0:
            in_spec = _bs((tr, cols), lambda i, ids: (i, 0))
            out_spec = _bs((tr, cols), lambda i, ids, nt=nt: (ids[0] * nt + i, 0))
        elif shard.ndim == 2 and axis == 1:
            in_spec = _bs((tr, cols), lambda i, ids: (i, 0))
            out_spec = _bs((tr, cols), lambda i, ids: (i, ids[0]))
        elif shard.ndim == 3 and axis == 1:
            tr, nt = rows, shard.shape[0]
            in_spec = _bs((None, rows, cols), lambda i, ids: (i, 0, 0))
            out_spec = _bs((None, rows, cols), lambda i, ids: (i, ids[0], 0))
        else:
            assert shard.ndim == 3 and axis == 0 and shard.shape[0] == 1
            in_spec = _bs((None, tr, cols), lambda i, ids: (0, i, 0))
            out_spec = _bs((None, tr, cols), lambda i, ids: (ids[0], i, 0))

        def body(ids_ref, in_ref, after_ref, out_ref):
            del ids_ref, after_ref
            out_ref[...] = in_ref[...].astype(out_ref.dtype)

        outs.append(pl.pallas_call(
            body, name=f"{name}{a}",
            grid_spec=pltpu.PrefetchScalarGridSpec(
                num_scalar_prefetch=1, grid=(nt,), in_specs=[in_spec, ANY], out_specs=out_spec),
            out_shape=jax.ShapeDtypeStruct(tuple(full), dtype), compiler_params=_params(1),
        )(ids, shard, after))
    return outs


class _SplitGather:
    def __init__(self, name, items, dtype, after):
        self.name, self.items, self.n = name, items, len(items)
        fulls = _place(name + "_place", items, dtype, after)
        n = self.n

        def issue(refs, send, recv):
            me, sibling, chips, c = self._geometry()
            for a in range(n):
                self._copy1(refs, send, recv, a, 0, me, sibling).start()
                for j, chip in enumerate(chips):
                    self._copy1(refs, send, recv, a, 1 + j, me, (*chip, c)).start()

        self.send, self.recv, self.bufs, self.token = _split_start(name + "_start", 4 * n, fulls, issue)

    @staticmethod
    def _geometry():
        x, y, c = _where_am_i()
        return (x, y, c), (x, y, 1 - c), [(1 - x, y), (x, 1 - y), (1 - x, 1 - y)], c

    def _blk(self, refs, a, p):
        shard, axis = self.items[a]
        return _slab(refs[a], axis, _dev_index(p), shard.shape[axis])

    def _copy1(self, refs, send, recv, a, k, owner, to):
        return pltpu.make_async_remote_copy(
            src_ref=self._blk(refs, a, owner), dst_ref=self._blk(refs, a, owner), send_sem=send.at[4 * a + k],
            recv_sem=recv.at[4 * a + k], device_id=to, device_id_type=MESH)

    def _copy2(self, refs, send, recv, a, j, owner, to):
        return pltpu.make_async_remote_copy(
            src_ref=self._blk(refs, a, owner), dst_ref=self._blk(refs, a, owner), send_sem=send.at[3 * a + j],
            recv_sem=recv.at[3 * a + j], device_id=to, device_id_type=MESH)

    def relay(self, after):
        n = self.n

        def relay(refs, send_in, recv_in, send_out, recv_out):
            me, sibling, chips, c = self._geometry()
            for a in range(n):
                for j, chip in enumerate(chips):
                    self._copy1(refs, send_in, recv_in, a, 1 + j, (*chip, c), me).wait_recv()
                    self._copy2(refs, send_out, recv_out, a, j, (*chip, c), sibling).start()
            for a in range(n):
                self._copy1(refs, send_in, recv_in, a, 0, sibling, me).wait_recv()
                for k in range(4):
                    self._copy1(refs, send_in, recv_in, a, k, me, sibling).wait_send()

        self.send, self.recv, self.bufs, self.token = _split_relay(
            self.name + "_relay", 3 * n, (self.send, self.recv), self.bufs, after, relay)
        return self.token

    def wait(self, after):
        n = self.n

        def finish(refs, send, recv):
            me, sibling, chips, c = self._geometry()
            for a in range(n):
                for j, chip in enumerate(chips):
                    self._copy2(refs, send, recv, a, j, (*chip, 1 - c), me).wait_recv()
                    self._copy2(refs, send, recv, a, j, (*chip, c), sibling).wait_send()

        return _split_wait(self.name + "_wait", (self.send, self.recv), self.bufs, after, finish)


class _SplitReduceScatter:
    def __init__(self, name, grads):
        self.name, self.n = name, len(grads)
        n = self.n
        g4 = [g.reshape(4, 2, *g.shape[1:]) for g in grads]
        land = [lax.empty((4, 1, *g.shape[1:]), g.dtype) for g in grads]

        def issue(refs, send, recv):
            for a in range(n):
                self._swap(refs, send, recv, a).start()

        self.send, self.recv, self.bufs, self.token = _split_start(name + "_d2d_start", n, g4 + land, issue)

    def _swap(self, refs, send, recv, a):
        x, y, c = _where_am_i()
        return pltpu.make_async_remote_copy(
            src_ref=refs[a].at[:, pl.ds(1 - c, 1)], dst_ref=refs[self.n + a], send_sem=send.at[a], recv_sem=recv.at[a],
            device_id=(x, y, 1 - c), device_id_type=MESH)

    def _hop(self, refs, send, recv, a, m):
        x, y, c = _where_am_i()
        px = (1 - x) if m & 2 else x
        py = (1 - y) if m & 1 else y
        return pltpu.make_async_remote_copy(
            src_ref=refs[a].at[2 * px + py], dst_ref=refs[self.n + a].at[m - 1], send_sem=send.at[3 * a + m - 1],
            recv_sem=recv.at[3 * a + m - 1], device_id=(px, py, c), device_id_type=MESH)

    def combine_and_send(self, after):
        n = self.n

        def finish(refs, send, recv):
            for a in range(n):
                self._swap(refs, send, recv, a).wait()

        bufs = _split_wait(self.name + "_d2d_wait", (self.send, self.recv), self.bufs, after, finish)
        x, y, c = _where_am_i()
        ids = jnp.stack([c, 2 * x + y]).astype(jnp.int32)
        self.own, sums = [], []
        for a in range(n):
            own, hb = _pair_sum(f"{self.name}_sum{a}", bufs[a], bufs[n + a], ids)
            self.own.append(own)
            sums.append(hb)
        land = [lax.empty((3, *h.shape[1:]), h.dtype) for h in sums]

        def issue(refs, send, recv):
            for a in range(n):
                for m in (1, 2, 3):
                    self._hop(refs, send, recv, a, m).start()

        self.send, self.recv, self.bufs, self.token = _split_start(self.name + "_ici_start", 3 * n, sums + land, issue)
        return self.token

    def wait(self, after):
        n = self.n

        def finish(refs, send, recv):
            for a in range(n):
                for m in (1, 2, 3):
                    self._hop(refs, send, recv, a, m).wait()

        bufs = _split_wait(self.name + "_ici_wait", (self.send, self.recv), self.bufs, after, finish)
        return list(zip(self.own, bufs[n:]))


def _pair_sum(name, g4, land, ids):
    rows, cols = g4.shape[2], g4.shape[3]
    tr = rows
    while tr * cols * 2 > 1024 * 1024 and tr % 32 == 0:
        tr //= 2

    def body(ids_ref, g_ref, l_ref, own_ref, sum_ref):
        h = g_ref[...].astype(F32) + l_ref[...].astype(F32)
        sum_ref[...] = h.astype(sum_ref.dtype)

        @pl.when(pl.program_id(1) == ids_ref[1])
        def _():
            own_ref[...] = h

    return pl.pallas_call(
        body, name=name,
        grid_spec=pltpu.PrefetchScalarGridSpec(
            num_scalar_prefetch=1, grid=(rows // tr, 4),
            in_specs=[_bs((None, None, tr, cols), lambda i, q, ids: (q, ids[0], i, 0)),
                      _bs((None, None, tr, cols), lambda i, q, ids: (q, 0, i, 0))],
            out_specs=[_bs((tr, cols), lambda i, q, ids: (i, 0)), _bs((None, tr, cols), lambda i, q, ids: (q, i, 0))]),
        out_shape=[jax.ShapeDtypeStruct((rows, cols), F32), jax.ShapeDtypeStruct((4, rows, cols), g4.dtype)],
        compiler_params=_params(2),
    )(ids, g4, land)


def _win_sum(ext, w, off):
    s = ext + _shift(ext, -1)
    if w >= 4:
        s = _shift(s, -1) + _shift(s, 1)
    if w >= 8:
        s = _shift(s, -2) + _shift(s, 2)
    if w >= 16:
        s = _shift(s, -4) + _shift(s, 4)
    return _shift(s, off) if off else s


def _inv_count(r0, t, w, seq):
    pos = r0 + lax.broadcasted_iota(jnp.int32, (t, 1), 0)
    cnt = jnp.minimum(pos + w // 2, seq) - jnp.maximum(pos - w // 2, 0)
    return 1.0 / cnt.astype(F32)


def _pool_fwd(p3, w_pool, pool_scale, seq, d_model):
    dp = d_model // 2
    pg = dp // len(POOL_WINDOWS)
    t = min(128, seq)
    n_chunks = seq // t
    h = WIN_HALO

    def body(u_ref, w_ref, sc_ref, d_ref, y_ref, pad_ref):
        g = pl.program_id(0)
        zeros = jnp.zeros((h, pg), F32)
        pad_ref[0:h, :] = zeros
        pad_ref[h + seq:h + seq + h, :] = zeros

        def fill(ci, _):
            r0 = pl.multiple_of(ci * t, t)
            pad_ref[pl.ds(h + r0, t), :] = u_ref[pl.ds(r0, t), :]
            return 0

        lax.fori_loop(0, n_chunks, fill, 0)
        wmat = w_ref[...]
        scale = sc_ref[...]
        for gi, w in enumerate(POOL_WINDOWS):
            @pl.when(g == gi)
            def _(w=w):
                def chunk(ci, _):
                    r0 = pl.multiple_of(ci * t, t)
                    ext = pad_ref[pl.ds(r0, t + 2 * h), :]
                    mean = _win_sum(ext, w, 0)[h:h + t, :] * _inv_count(r0, t, w, seq)
                    d = (mean - ext[h:h + t, :]).astype(BF16)
                    d_ref[pl.ds(r0, t), :] = d
                    q = jnp.dot(d, wmat, preferred_element_type=F32)
                    y_ref[pl.ds(r0, t), :] = (q * scale).astype(BF16)
                    return 0

                lax.fori_loop(0, n_chunks, chunk, 0, unroll=2)

    return pl.pallas_call(
        body, name="pool_fwd", grid=(len(POOL_WINDOWS),),
        in_specs=[_bs((None, seq, pg), lambda g: (0, 0, g)), _bs((None, pg, pg), lambda g: (g, 0, 0)),
                  _bs((1, pg), lambda g: (0, g))],
        out_specs=[_bs((seq, pg), lambda g: (0, g)), _bs((seq, pg), lambda g: (0, g))],
        out_shape=[jax.ShapeDtypeStruct((seq, dp), BF16), jax.ShapeDtypeStruct((seq, d_model), BF16)],
        scratch_shapes=[pltpu.VMEM((seq + 2 * h, pg), F32)],
        compiler_params=_params(1),
    )(p3, w_pool, pool_scale)


def _pool_bwd(d, dy, w_pool, pool_scale, token, seq, d_model):
    dp = d_model // 2
    pg = dp // len(POOL_WINDOWS)
    t = min(128, seq)
    n_chunks = seq // t
    h = WIN_HALO
    tn_dims = (((0,), (0,)), ((), ()))
    nt_dims = (((1,), (1,)), ((), ()))

    def body(d_ref, dy_ref, w_ref, sc_ref, tok_ref, du_ref, dwb_ref, dsc_ref, pad_ref, dd_ref, dw_ref):
        del tok_ref
        g = pl.program_id(0)
        zeros = jnp.zeros((h, pg), F32)
        pad_ref[0:h, :] = zeros
        pad_ref[h + seq:h + seq + h, :] = zeros
        wmat = w_ref[...]
        scale = sc_ref[...]
        for gi, w in enumerate(POOL_WINDOWS):
            @pl.when(g == gi)
            def _(w=w):
                dw_ref[...] = jnp.zeros((pg, pg), F32)

                def first(ci, dsc):
                    r0 = pl.multiple_of(ci * t, t)
                    dv = d_ref[pl.ds(r0, t), :]
                    dyv = dy_ref[pl.ds(r0, t), :]
                    q = jnp.dot(dv, wmat, preferred_element_type=F32)
                    dsc = dsc + jnp.sum(dyv * q, axis=0, keepdims=True)
                    dq = (dyv * scale).astype(BF16)
                    dw_ref[...] += lax.dot_general(dv, dq, tn_dims, preferred_element_type=F32)
                    dd = lax.dot_general(dq, wmat, nt_dims, preferred_element_type=F32)
                    dd_ref[pl.ds(r0, t), :] = dd
                    pad_ref[pl.ds(h + r0, t), :] = dd * _inv_count(r0, t, w, seq)
                    return dsc

                def first_pair(cj, dsc):
                    return first(2 * cj + 1, first(2 * cj, dsc))

                dsc_ref[...] = lax.fori_loop(0, n_chunks // 2, first_pair, jnp.zeros((1, pg), F32))
                dwb_ref[...] = dw_ref[...].reshape(N_DEV, pg // N_DEV, pg).astype(BF16)

                def second(ci, _):
                    r0 = pl.multiple_of(ci * t, t)
                    ext = pad_ref[pl.ds(r0, t + 2 * h), :]
                    back = _win_sum(ext, w, 1)[h:h + t, :]
                    du_ref[pl.ds(r0, t), :] = (back - dd_ref[pl.ds(r0, t), :]).astype(BF16)
                    return 0

                lax.fori_loop(0, n_chunks, second, 0, unroll=2)

    return pl.pallas_call(
        body, name="pool_bwd", grid=(len(POOL_WINDOWS),),
        in_specs=[_bs((seq, pg), lambda g: (0, g)), _bs((seq, pg), lambda g: (0, g)),
                  _bs((None, pg, pg), lambda g: (g, 0, 0)), _bs((1, pg), lambda g: (0, g)),
                  _bs((8, 128), lambda g: (0, 0))],
        out_specs=[_bs((seq, pg), lambda g: (0, g)), _bs((N_DEV, None, pg // N_DEV, pg), lambda g: (0, g, 0, 0)),
                   _bs((1, pg), lambda g: (0, g))],
        out_shape=[jax.ShapeDtypeStruct((seq, 3 * dp), BF16),
                   jax.ShapeDtypeStruct((N_DEV, len(POOL_WINDOWS), pg // N_DEV, pg), BF16),
                   jax.ShapeDtypeStruct((1, dp), F32)],
        scratch_shapes=[pltpu.VMEM((seq + 2 * h, pg), F32), pltpu.VMEM((seq, pg), F32), pltpu.VMEM((pg, pg), F32)],
        compiler_params=_params(1),
    )(d, dy, w_pool, pool_scale, token)


def _tile_scan(n_tiles, lanes, loads, stores):
    row = lax.broadcasted_iota(jnp.int32, (8, lanes), 0)
    group = 8

    def local_scan(n, k):
        aa, bb = loads[n](k)
        for sh in (1, 2, 4):
            if n == 0:
                ok = row >= sh
                ap = jnp.where(ok, pltpu.roll(aa, sh, 0), 1.0)
                bp = jnp.where(ok, pltpu.roll(bb, sh, 0), 0.0)
            else:
                ok = row < 8 - sh
                ap = jnp.where(ok, pltpu.roll(aa, 8 - sh, 0), 1.0)
                bp = jnp.where(ok, pltpu.roll(bb, 8 - sh, 0), 0.0)
            bb = aa * bp + bb
            aa = aa * ap
        return aa, bb

    def step(s, carry):
        carry = list(carry)
        for n in range(2):
            tiles = [s * group + u if n == 0 else n_tiles - 1 - (s * group + u) for u in range(group)]
            local = [local_scan(n, k) for k in tiles]
            for k, (aa, bb) in zip(tiles, local):
                hh = bb + aa * carry[n]
                stores[n](k, hh)
                carry[n] = jnp.broadcast_to(hh[7:8, :] if n == 0 else hh[0:1, :], (8, lanes))
        return tuple(carry)

    zeros = jnp.zeros((8, lanes), F32)
    lax.fori_loop(0, n_tiles // group, step, (zeros, zeros))


def _gate_preacts(xc, wcat_ref):
    xcb = xc.astype(BF16)
    return xcb, jnp.dot(xcb, wcat_ref[...], preferred_element_type=F32)


def _gates(pre, n, pk_ref, sp):
    lh = pre.shape[1] // 4
    r = _sigmoid(pre[:, (2 * n) * lh:(2 * n + 1) * lh] + pk_ref[pl.ds(4 + n, 1), :])
    i = _sigmoid(pre[:, (2 * n + 1) * lh:(2 * n + 2) * lh] + pk_ref[pl.ds(6 + n, 1), :])
    log_a = (-RG_C * r) * sp[n]
    a = jnp.exp(log_a)
    x = 2.0 * log_a
    one_minus_a2 = jnp.where(x > -0.01, -(x * (1.0 + x * (0.5 + x * (1.0 / 6.0)))), 1.0 - a * a)
    m = jnp.sqrt(one_minus_a2)
    return r, i, a, m


def _conv_chunk(upad_ref, pk_ref, cb, r0, t):
    ext = upad_ref[pl.ds(r0, t + 2 * CONV_HALO), :]
    acc = pk_ref[pl.ds(1, 1), :] * ext
    for k in (0, 2, 3):
        acc = acc + pk_ref[pl.ds(k, 1), :] * _shift(ext, k - 1)
    return acc[CONV_HALO:CONV_HALO + t, :] + cb, ext


def _lru_fwd(p3, y_in, pack, conv_b, wcat, token, seq, d_model):
    dl = d_model // 2
    lh = dl // N_HEADS
    t = min(128, seq)
    n_chunks = seq // t
    seg = seq // 8
    hal = CONV_HALO
    first_rec_block = (d_model - dl) // lh

    def body(ur_ref, ug_ref, pk_ref, cb_ref, wcat_ref, yin_ref, tok_ref, y_ref, h0_ref, h1_ref,
             upad, a_scr, b_scr):
        del yin_ref, tok_ref
        zeros = jnp.zeros((hal, lh), F32)
        upad[0:hal, :] = zeros
        upad[hal + seq:hal + seq + hal, :] = zeros
        for ref in (h0_ref, h1_ref):
            ref[0:hal, :] = zeros
            ref[hal + seq:hal + seq + hal, :] = zeros

        def fill(ci, _):
            r0 = pl.multiple_of(ci * t, t)
            upad[pl.ds(hal + r0, t), :] = ur_ref[pl.ds(r0, t), :]
            return 0

        lax.fori_loop(0, n_chunks, fill, 0)
        cb = cb_ref[...]
        sp = [_softplus(-pk_ref[pl.ds(8 + n, 1), :]) for n in range(2)]

        def chunk(ci, _):
            r0 = pl.multiple_of(ci * t, t)
            xc, _ext = _conv_chunk(upad, pk_ref, cb, r0, t)
            _, pre = _gate_preacts(xc, wcat_ref)
            for n in range(2):
                _, i, a, m = _gates(pre, n, pk_ref, sp)
                a_scr[n, pl.ds(r0, t), :] = a
                b_scr[n, pl.ds(r0, t), :] = (m * i) * xc
            return 0

        lax.fori_loop(0, n_chunks, chunk, 0, unroll=2)

        def load(n):
            def get(k):
                at = pl.ds(pl.multiple_of(k * 8, 8), 8)
                return a_scr[n, at, :], b_scr[n, at, :]
            return get

        def store(ref):
            def put(k, v):
                ref[pl.ds(pl.multiple_of(hal + k * 8, 8), 8), :] = v
            return put

        _tile_scan(seq // 8, lh, [load(0), load(1)], [store(h0_ref), store(h1_ref)])

        def out(ci, _):
            r0 = pl.multiple_of(ci * t, t)
            hsum = h0_ref[pl.ds(hal + r0, t), :] + h1_ref[pl.ds(hal + r0, t), :]
            gl, _dg = _gelu_and_grad(ug_ref[pl.ds(r0, t), :])
            y_ref[pl.ds(r0, t), :] = (hsum * gl).astype(BF16)
            return 0

        lax.fori_loop(0, n_chunks, out, 0)

    return pl.pallas_call(
        body, name="lru_fwd", grid=(N_HEADS,),
        in_specs=[_bs((None, seq, lh), lambda h: (1, 0, h)), _bs((None, seq, lh), lambda h: (2, 0, h)),
                  _bs((None, SMALL_ROWS, lh), lambda h: (h, 0, 0)), _bs((1, lh), lambda h: (0, h)),
                  _bs((None, lh, 4 * lh), lambda h: (h, 0, 0)),
                  ANY, _bs((8, 128), lambda h: (0, 0))],
        out_specs=[_bs((seq, lh), lambda h: (0, first_rec_block + h)),
                   _bs((seq + 2 * hal, lh), lambda h: (0, h)), _bs((seq + 2 * hal, lh), lambda h: (0, h))],
        out_shape=[jax.ShapeDtypeStruct((seq, d_model), BF16), jax.ShapeDtypeStruct((seq + 2 * hal, dl), F32),
                   jax.ShapeDtypeStruct((seq + 2 * hal, dl), F32)],
        scratch_shapes=[pltpu.VMEM((seq + 2 * hal, lh), F32), pltpu.VMEM((2, seq, lh), F32),
                        pltpu.VMEM((2, seq, lh), F32)],
        input_output_aliases={5: 0},
        compiler_params=_params(1),
    )(p3, p3, pack, conv_b, wcat, y_in, token)


def _lru_bwd(p3, dy, h0p, h1p, dproj_in, pack, conv_b, wcat, token, seq, d_model):
    dl = d_model // 2
    lh = dl // N_HEADS
    t = min(128, seq)
    n_chunks = seq // t
    seg = seq // 8
    hal = CONV_HALO
    first_rec_block = (d_model - dl) // lh
    tn_dims = (((0,), (0,)), ((), ()))
    nt_dims = (((1,), (1,)), ((), ()))

    def body(ur_ref, ug_ref, dy_ref, h0_ref, h1_ref, pk_ref, cb_ref, wcat_ref, tok_ref, din_ref,
             dproj_ref, dpk_ref, dcb_ref, dwcat_ref,
             upad, a_scr, dh_scr, g_scr, dxc_pad, dpr_ref, out_sems, gate_scr):
        del din_ref, tok_ref
        zeros = jnp.zeros((hal, lh), F32)
        for ref in (upad, dxc_pad):
            ref[0:hal, :] = zeros
            ref[hal + seq:hal + seq + hal, :] = zeros
        for n in range(2):
            a_scr[n, 0:hal, :] = zeros
            a_scr[n, hal + seq:hal + seq + hal, :] = zeros

        def fill(ci, _):
            r0 = pl.multiple_of(ci * t, t)
            upad[pl.ds(hal + r0, t), :] = ur_ref[pl.ds(r0, t), :]
            return 0

        lax.fori_loop(0, n_chunks, fill, 0)
        cb = cb_ref[...]
        lam = [pk_ref[pl.ds(8 + n, 1), :] for n in range(2)]
        sp = [_softplus(-lam[n]) for n in range(2)]

        def chunk1(ci, _):
            r0 = pl.multiple_of(ci * t, t)
            xc, _ext = _conv_chunk(upad, pk_ref, cb, r0, t)
            _, pre = _gate_preacts(xc, wcat_ref)
            for n in range(2):
                r, i, a, m = _gates(pre, n, pk_ref, sp)
                a_scr[n, pl.ds(hal + r0, t), :] = a
                for q, v in enumerate((r, i, m)):
                    gate_scr[3 * n + q, pl.ds(r0, t), :] = v
            hsum = h0_ref[pl.ds(hal + r0, t), :] + h1_ref[pl.ds(hal + r0, t), :]
            gl, dgl = _gelu_and_grad(ug_ref[pl.ds(r0, t), :])
            dyv = dy_ref[pl.ds(r0, t), :]
            dh_scr[pl.ds(r0, t), :] = dyv * gl
            dpr_ref[1, pl.ds(r0, t), :] = ((dyv * hsum) * dgl).astype(BF16)
            return 0

        lax.fori_loop(0, n_chunks, chunk1, 0, unroll=2)

        def load(n):
            def get(k):
                r0 = pl.multiple_of(k * 8, 8)
                if n == 0:
                    coef = _shift(a_scr[0, pl.ds(pl.multiple_of(hal + r0, 8), 16), :], 1)[0:8, :]
                else:
                    coef = _shift(a_scr[1, pl.ds(pl.multiple_of(hal + r0 - 8, 8), 16), :], -1)[8:16, :]
                return coef, dh_scr[pl.ds(r0, 8), :]
            return get

        def store(n):
            def put(k, v):
                g_scr[n, pl.ds(pl.multiple_of(k * 8, 8), 8), :] = v
            return put

        _tile_scan(seq // 8, lh, [load(1), load(0)], [store(1), store(0)])

        dwcat_ref[...] = jnp.zeros((lh, 4 * lh), F32)

        def chunk3(ci, carry):
            dba, dbi, dlam, dcb = carry
            r0 = pl.multiple_of(ci * t, t)
            xc, _ext = _conv_chunk(upad, pk_ref, cb, r0, t)
            xcb = xc.astype(BF16)
            dxc = jnp.zeros((t, lh), F32)
            dba, dbi, dlam = list(dba), list(dbi), list(dlam)
            dpre = []
            for n in range(2):
                r, i, m = (gate_scr[3 * n + q, pl.ds(r0, t), :] for q in range(3))
                a = a_scr[n, pl.ds(hal + r0, t), :]
                hext = (h0_ref if n == 0 else h1_ref)[pl.ds(r0, t + 2 * hal), :]
                hprev = _shift(hext, -1 if n == 0 else 1)[hal:hal + t, :]
                gb = g_scr[n, pl.ds(r0, t), :]
                da = gb * hprev
                dm = gb * i * xc
                di = gb * m * xc
                dxc = dxc + gb * (m * i)
                dlog_a = da * a - dm * (a * a) / m
                dr = dlog_a * (-RG_C * sp[n])
                dlam[n] = dlam[n] + jnp.sum(dlog_a * r, axis=0, keepdims=True)
                dpr = dr * r * (1.0 - r)
                dpi = di * i * (1.0 - i)
                dba[n] = dba[n] + jnp.sum(dpr, axis=0, keepdims=True)
                dbi[n] = dbi[n] + jnp.sum(dpi, axis=0, keepdims=True)
                dpre += [dpr.astype(BF16), dpi.astype(BF16)]
            dpre = jnp.concatenate(dpre, axis=1)
            dwcat_ref[...] += lax.dot_general(xcb, dpre, tn_dims, preferred_element_type=F32)
            dxc = dxc + lax.dot_general(dpre, wcat_ref[...], nt_dims, preferred_element_type=F32)
            dxc_pad[pl.ds(hal + r0, t), :] = dxc
            dcb = dcb + jnp.sum(dxc, axis=0, keepdims=True)
            return tuple(dba), tuple(dbi), tuple(dlam), dcb

        zr = jnp.zeros((1, lh), F32)
        def chunk3_pair(cj, carry):
            return chunk3(2 * cj + 1, chunk3(2 * cj, carry))

        dba, dbi, dlam, dcb = lax.fori_loop(0, n_chunks // 2, chunk3_pair, ((zr, zr), (zr, zr), (zr, zr), zr))
        dcb_ref[...] = dcb
        for n in range(2):
            dpk_ref[pl.ds(4 + n, 1), :] = dba[n]
            dpk_ref[pl.ds(6 + n, 1), :] = dbi[n]
            dpk_ref[pl.ds(8 + n, 1), :] = dlam[n] * (RG_C * jax.nn.sigmoid(-lam[n]))
        dpk_ref[pl.ds(10, SMALL_ROWS - 10), :] = jnp.zeros((SMALL_ROWS - 10, lh), F32)

        def chunk4(ci, dtap):
            r0 = pl.multiple_of(ci * t, t)
            gext = dxc_pad[pl.ds(r0, t + 2 * hal), :]
            uext = upad[pl.ds(r0, t + 2 * hal), :]
            gmid = gext[hal:hal + t, :]
            du = pk_ref[pl.ds(1, 1), :] * gext
            for k in (0, 2, 3):
                du = du + pk_ref[pl.ds(k, 1), :] * _shift(gext, 1 - k)
            dpr_ref[0, pl.ds(r0, t), :] = du[hal:hal + t, :].astype(BF16)
            out = []
            for k in range(4):
                usl = _shift(uext, k - 1)[hal:hal + t, :]
                out.append(dtap[k] + jnp.sum(gmid * usl, axis=0, keepdims=True))
            return tuple(out)

        dtap = lax.fori_loop(0, n_chunks, chunk4, (zr, zr, zr, zr))
        for k in range(4):
            dpk_ref[pl.ds(k, 1), :] = dtap[k]

        head = pl.program_id(0)
        outs = [pltpu.make_async_copy(
            dpr_ref.at[b], dproj_ref.at[:, pl.ds(pl.multiple_of((1 + b) * dl + head * lh, lh), lh)], out_sems.at[b])
            for b in range(2)]
        for cp in outs:
            cp.start()
        for cp in outs:
            cp.wait()

    return pl.pallas_call(
        body, name="lru_bwd", grid=(N_HEADS,),
        in_specs=[_bs((None, seq, lh), lambda h: (1, 0, h)), _bs((None, seq, lh), lambda h: (2, 0, h)),
                  _bs((seq, lh), lambda h: (0, first_rec_block + h)),
                  _bs((seq + 2 * hal, lh), lambda h: (0, h)), _bs((seq + 2 * hal, lh), lambda h: (0, h)),
                  _bs((None, SMALL_ROWS, lh), lambda h: (h, 0, 0)), _bs((1, lh), lambda h: (0, h)),
                  _bs((None, lh, 4 * lh), lambda h: (h, 0, 0)),
                  _bs((8, 128), lambda h: (0, 0)), ANY],
        out_specs=[ANY, _bs((None, SMALL_ROWS, lh), lambda h: (h, 0, 0)),
                   _bs((1, lh), lambda h: (0, h)), _bs((None, lh, 4 * lh), lambda h: (h, 0, 0))],
        out_shape=[jax.ShapeDtypeStruct((seq, 3 * dl), BF16), jax.ShapeDtypeStruct((N_HEADS, SMALL_ROWS, lh), F32),
                   jax.ShapeDtypeStruct((1, dl), F32), jax.ShapeDtypeStruct((N_HEADS, lh, 4 * lh), F32)],
        scratch_shapes=[pltpu.VMEM((seq + 2 * hal, lh), F32), pltpu.VMEM((2, seq + 2 * hal, lh), F32),
                        pltpu.VMEM((seq, lh), F32), pltpu.VMEM((2, seq, lh), F32),
                        pltpu.VMEM((seq + 2 * hal, lh), F32), pltpu.VMEM((2, seq, lh), BF16),
                        pltpu.SemaphoreType.DMA((2,)), pltpu.VMEM((6, seq, lh), F32)],
        input_output_aliases={9: 0},
        compiler_params=_params(1),
    )(p3, p3, dy, h0p, h1p, pack, conv_b, wcat, token, dproj_in)


class _tiles:
    def __init__(self, seq, d_model, d_ff):
        self.rows = min(1024, seq)
        self.ln_rows = min(256, seq)
        self.ff_cols = min(1024, d_ff)
        self.ff_split = 4
        self.ff_k = min(2048, d_ff)
        self.grad_rows = 512


def _ln_loss_bwd(ffn, x1, tgt, g, b, tr):
    seq, d = ffn.shape

    def body(f_ref, x_ref, t_ref, g_ref, b_ref, dz_ref, dzb_ref, dg_ref, db_ref, loss_ref):
        i = pl.program_id(0)
        gv = g_ref[...]
        z = ALPHA * x_ref[...] + f_ref[...]
        y, xhat, rstd = _ln_fwd(z, gv, b_ref[...])
        err = y - t_ref[...]
        part = 0.5 * jnp.sum(jnp.mean(err * err, axis=-1, keepdims=True), axis=0, keepdims=True)
        dz, dg, db = _ln_bwd(err * (1.0 / d), xhat, rstd, gv)
        dz_ref[...] = dz
        dzb_ref[...] = dz.astype(BF16)
        _acc_rows(dg_ref, i == 0, dg)
        _acc_rows(db_ref, i == 0, db)
        _acc_rows(loss_ref, i == 0, jnp.broadcast_to(part, (8, 128)))

    row = _bs((tr, d), lambda i: (i, 0))
    vec = _bs((1, d), lambda i: (0, 0))
    return pl.pallas_call(
        body, name="ln_ffn_loss", grid=(seq // tr,), in_specs=[row, row, row, vec, vec],
        out_specs=[row, row, vec, vec, _bs((8, 128), lambda i: (0, 0))],
        out_shape=[jax.ShapeDtypeStruct((seq, d), F32), jax.ShapeDtypeStruct((seq, d), BF16),
                   jax.ShapeDtypeStruct((1, d), F32), jax.ShapeDtypeStruct((1, d), F32),
                   jax.ShapeDtypeStruct((8, 128), F32)],
        compiler_params=_params(1),
    )(ffn, x1, tgt, g, b)


def _ln_bwd_side(dx_branch, dres, z, g, b, n_steps):
    seq, d = z.shape
    tr = seq // n_steps

    def fn(step, ins, outs):
        a_ref, r_ref, z_ref, g_ref, b_ref = ins
        dz_ref, dzb_ref, dg_ref, db_ref = outs
        gv = g_ref[...]
        _, xhat, rstd = _ln_fwd(z_ref[...], gv, b_ref[...])
        dz, dg, db = _ln_bwd(ALPHA * r_ref[...] + a_ref[...], xhat, rstd, gv)
        dz_ref[...] = dz
        dzb_ref[...] = dz.astype(BF16)
        _acc_rows(dg_ref, step == 0, dg)
        _acc_rows(db_ref, step == 0, db)

    row = ((tr, d), lambda s: (s, 0))
    vec = ((1, d), lambda s: (0, 0))
    shapes = [jax.ShapeDtypeStruct((seq, d), F32), jax.ShapeDtypeStruct((seq, d), BF16),
              jax.ShapeDtypeStruct((1, d), F32), jax.ShapeDtypeStruct((1, d), F32)]
    return [(dx_branch, *row), (dres, *row), (z, *row), (g, *vec), (b, *vec)], shapes, [row, row, vec, vec], fn


def _to_bf16(name, a, token):
    rows, cols = a.shape
    tr = min(512, rows)

    def body(a_ref, tok_ref, o_ref):
        del tok_ref
        o_ref[...] = a_ref[...].astype(BF16)

    return pl.pallas_call(
        body, name=name, grid=(rows // tr,),
        in_specs=[_bs((tr, cols), lambda i: (i, 0)), _bs((8, 128), lambda i: (0, 0))],
        out_specs=_bs((tr, cols), lambda i: (i, 0)), out_shape=jax.ShapeDtypeStruct((rows, cols), BF16),
        compiler_params=_params(1),
    )(a, token)


def _sum_blocks(name, parts):
    def body(p_ref, o_ref):
        acc = p_ref[0]
        for s in range(1, parts.shape[0]):
            acc = acc + p_ref[s]
        o_ref[...] = acc

    return pl.pallas_call(body, name=name, out_shape=jax.ShapeDtypeStruct(parts.shape[1:], F32))(parts)


def _adamw_values(w, g, m, v):
    m = ADAM_B1 * m + (1.0 - ADAM_B1) * g
    v = ADAM_B2 * v + (1.0 - ADAM_B2) * (g * g)
    m_hat = m / (1.0 - ADAM_B1 ** ADAM_STEP)
    v_hat = v / (1.0 - ADAM_B2 ** ADAM_STEP)
    delta = -ADAM_LR * (m_hat / (jnp.sqrt(v_hat) + ADAM_EPS) + ADAM_WD * w)
    return delta, m, v


def _adamw_side(own, parts, w, m, v, n_steps):
    rows, cols = w.shape
    tr = rows // n_steps

    def fn(step, ins, outs):
        o_ref, p_ref, w_ref, m_ref, v_ref = ins
        g = o_ref[...]
        for s in range(parts.shape[0]):
            g = g + p_ref[s].astype(F32)
        delta, mn, vn = _adamw_values(w_ref[...], g, m_ref[...], v_ref[...])
        for ref, val in zip(outs, (g, delta, mn, vn)):
            ref[...] = val

    row = ((tr, cols), lambda s: (s, 0))
    stack = ((parts.shape[0], tr, cols), lambda s: (0, s, 0))
    shapes = [jax.ShapeDtypeStruct((rows, cols), F32)] * 4
    return [(own, *row), (parts, *stack), (w, *row), (m, *row), (v, *row)], shapes, [row] * 4, fn


def _sum_adamw(name, own, parts, w, m, v):
    rows, cols = w.shape
    n_parts = parts.shape[0]
    tr = rows
    min_rows = 8 if parts.dtype == F32 else 16
    while tr * cols * 4 > 1024 * 1024 and tr % (2 * min_rows) == 0:
        tr //= 2

    def body(*refs):
        if own is None:
            p_ref, w_ref, m_ref, v_ref, g_ref, d_ref, mo_ref, vo_ref = refs
            g = p_ref[0].astype(F32)
            rest = range(1, n_parts)
        else:
            o_ref, p_ref, w_ref, m_ref, v_ref, g_ref, d_ref, mo_ref, vo_ref = refs
            g = o_ref[...]
            rest = range(n_parts)
        for s in rest:
            g = g + p_ref[s].astype(F32)
        delta, mn, vn = _adamw_values(w_ref[...], g, m_ref[...], v_ref[...])
        g_ref[...] = g
        d_ref[...] = delta
        mo_ref[...] = mn
        vo_ref[...] = vn

    spec = _bs((tr, cols), lambda i: (i, 0))
    lead = [] if own is None else [own]
    return pl.pallas_call(
        body, name=name, grid=(rows // tr,),
        in_specs=[spec] * len(lead) + [_bs((n_parts, tr, cols), lambda i: (0, i, 0)), spec, spec, spec],
        out_specs=[spec] * 4, out_shape=[jax.ShapeDtypeStruct((rows, cols), F32)] * 4,
        compiler_params=_params(1),
    )(*lead, parts, w, m, v)


def _rows128(a):
    return a.reshape(-1, 128)


def kernel(x, ln_mix_g, ln_mix_b, w_in, w_pool, pool_scale, conv_w, conv_b, w_rg_a, b_rg_a, w_rg_i, b_rg_i, rg_lambda, w_out, ln_ffn_g, ln_ffn_b, w_mlp_in, w_mlp_out, loss_target, m_ln_mix_g, m_ln_mix_b, m_w_in, m_w_pool, m_pool_scale, m_conv_w, m_conv_b, m_w_rg_a, m_b_rg_a, m_w_rg_i, m_b_rg_i, m_rg_lambda, m_w_out, m_ln_ffn_g, m_ln_ffn_b, m_w_mlp_in, m_w_mlp_out, v_ln_mix_g, v_ln_mix_b, v_w_in, v_w_pool, v_pool_scale, v_conv_w, v_conv_b, v_w_rg_a, v_b_rg_a, v_w_rg_i, v_b_rg_i, v_rg_lambda, v_w_out, v_ln_ffn_g, v_ln_ffn_b, v_w_mlp_in, v_w_mlp_out):
    seq, d_model = x.shape[1], x.shape[2]
    dh = d_model // 2
    lh = dh // N_HEADS
    pg = dh // len(POOL_WINDOWS)
    d_ff = w_mlp_in.shape[2] * N_DEV
    assert lh == 128 and conv_w.shape[3] == lh and w_pool.shape[2] * N_DEV == pg

    xs = x[0]
    tgt = loss_target[0]

    def small_pack(cw, ba, bi, lam):
        return jnp.concatenate([cw.reshape(4, lh), ba.reshape(2, lh), bi.reshape(2, lh), lam.reshape(2, lh),
                                jnp.zeros((SMALL_ROWS - 10, lh), F32)], axis=0)

    pack_mine = small_pack(conv_w, b_rg_a, b_rg_i, rg_lambda)
    pack_bits = lax.bitcast_convert_type(pack_mine, BF16).reshape(1, SMALL_ROWS, 2 * lh)
    win_gather = _SplitGather("gather_w_in", [(w_in[0], 1), (w_pool[0], 1), (pack_bits, 0)], BF16, after=pack_mine)
    wout_gather = _SplitGather("gather_w_out", [(w_out[0], 0)], BF16, after=win_gather.token)
    w1_gather = _SplitGather("gather_w_mlp_in", [(w_mlp_in[0], 1)], BF16, after=wout_gather.token)
    w2_gather = _SplitGather("gather_w_mlp_out", [(w_mlp_out[0], 0)], BF16, after=w1_gather.token)
    xb = _to_bf16("x_bf16", x[0], w2_gather.token)
    win_full, wpool_full, pack_bits_full = win_gather.wait(after=win_gather.relay(after=xb))
    pack_full = lax.bitcast_convert_type(pack_bits_full.reshape(N_DEV, SMALL_ROWS, lh, 2), F32)
    wcat = jnp.concatenate([w_rg_a[0, 0], w_rg_i[0, 0], w_rg_a[0, 1], w_rg_i[0, 1]], axis=-1).astype(BF16)
    vec = lambda i, j, k: (0, 0)
    row_full = lambda i, j, k: (i, 0)

    def after(token):
        return (token, _sp((8, 128), vec))

    def sds(shape, dtype):
        return jax.ShapeDtypeStruct(shape, dtype)

    def plain_epi(acc, i, ex, out):
        out[0][...] = acc

    def bf16_epi(acc, i, ex, out):
        out[0][...] = acc.astype(BF16)

    t = _tiles(seq, d_model, d_ff)

    (p3,) = _matmul(
        "proj", xb, win_full, _sp((t.rows, d_model), lambda i, j, k: (i, 0)), _sp((d_model, dh), lambda i, j, k: (0, j)),
        grid=(seq // t.rows, 3, 1),
        out_shape=[sds((3, seq, dh), F32)], out_specs=[_sp((None, t.rows, dh), lambda i, j, k: (j, i, 0))],
        epilogue=plain_epi)

    d_pool, y_half = _pool_fwd(p3, wpool_full, pool_scale, seq, d_model)
    y, h0p, h1p = _lru_fwd(p3, y_half, pack_full, conv_b, wcat, wout_gather.relay(after=y_half), seq, d_model)
    (wout_full,) = wout_gather.wait(after=y)
    relay_token = w1_gather.relay(after=wout_full)

    mix_rows = 2 * t.ln_rows

    def mix_epi(acc, i, ex, out):
        x_ref, g_ref, b_ref = ex[:3]
        for part in range(2):
            rows = pl.ds(part * t.ln_rows, t.ln_rows)
            z = ALPHA * x_ref[rows, :] + acc[part * t.ln_rows:(part + 1) * t.ln_rows, :]
            x1, _, _ = _ln_fwd(z, g_ref[...], b_ref[...])
            out[0][rows, :] = z
            out[1][rows, :] = x1
            out[2][rows, :] = x1.astype(BF16)

    z1, x1, x1b = _matmul(
        "mix_out", y, wout_full, _sp((mix_rows, d_model), row_full), _sp((d_model, d_model), vec, single=True),
        grid=(seq // mix_rows, 1, 1),
        extras=[(xs, _sp((mix_rows, d_model), row_full)), (ln_mix_g, _sp((1, d_model), vec)),
                (ln_mix_b, _sp((1, d_model), vec)), after(relay_token)],
        out_shape=[sds((seq, d_model), F32), sds((seq, d_model), F32), sds((seq, d_model), BF16)],
        out_specs=[_sp((mix_rows, d_model), row_full)] * 3, epilogue=mix_epi)
    (w1_full,) = w1_gather.wait(after=x1b)

    def mlp_in_epi(acc, i, ex, out, cols):
        h = jnp.maximum(acc, 0.0)
        out[0][:, cols] = (h * h).astype(BF16)
        out[1][:, cols] = (2.0 * h).astype(BF16)

    hmid, dact = _matmul(
        "mlp_in", x1b, w1_full, _sp((t.rows, d_model), lambda i, j, k: (i, 0)),
        _sp((d_model, t.ff_cols), lambda i, j, k: (0, j)),
        grid=(seq // t.rows, d_ff // t.ff_cols, 1), j_outer=True,
        out_shape=[sds((seq, d_ff), BF16)] * 2, out_specs=[_sp((t.rows, t.ff_cols), lambda i, j, k: (i, j))] * 2,
        epilogue=mlp_in_epi, n_split=t.ff_split)
    (w2_full,) = w2_gather.wait(after=w2_gather.relay(after=hmid))

    (ffn,) = _matmul(
        "mlp_out", hmid, w2_full, _sp((t.rows, t.ff_k), lambda i, j, k: (i, k)),
        _sp((t.ff_k, d_model), lambda i, j, k: (k, 0)),
        grid=(seq // t.rows, 1, d_ff // t.ff_k),
        out_shape=[sds((seq, d_model), F32)], out_specs=[_sp((t.rows, d_model), row_full)])
    dz2, dz2b, g_ffn_g, g_ffn_b, loss_part = _ln_loss_bwd(ffn, x1, tgt, ln_ffn_g, ln_ffn_b, t.ln_rows)

    (g_w2,) = _matmul(
        "grad_w_mlp_out", hmid, dz2b, _sp((seq, t.grad_rows), lambda i, j, k: (0, i)),
        _sp((seq, d_model), vec, single=True),
        grid=(d_ff // t.grad_rows, 1, 1), ta=True,
        out_shape=[sds((d_ff, d_model), BF16)], out_specs=[_sp((t.grad_rows, d_model), row_full)],
        epilogue=bf16_epi)
    scatter_w2 = _SplitReduceScatter("scatter_w_mlp_out", [g_w2.reshape(N_DEV, d_ff // N_DEV, d_model)])

    def dpre_epi(acc, i, ex, out, cols):
        out[0][:, cols] = (acc * ex[0][:, cols].astype(F32)).astype(BF16)

    (dpre,) = _matmul(
        "mlp_dpre", dz2b, w2_full, _sp((t.rows, d_model), lambda i, j, k: (i, 0)),
        _sp((t.ff_cols, d_model), lambda i, j, k: (j, 0)),
        grid=(seq // t.rows, d_ff // t.ff_cols, 1), j_outer=True, tb=True,
        extras=[(dact, _sp((t.rows, t.ff_cols), lambda i, j, k: (i, j))), after(scatter_w2.token)],
        out_shape=[sds((seq, d_ff), BF16)], out_specs=[_sp((t.rows, t.ff_cols), lambda i, j, k: (i, j))],
        epilogue=dpre_epi, n_split=t.ff_split)
    token_w2 = scatter_w2.combine_and_send(after=dpre)

    (dx1_mlp,) = _matmul(
        "mlp_dx", dpre, w1_full, _sp((t.rows, t.ff_k), lambda i, j, k: (i, k)),
        _sp((d_model, t.ff_k), lambda i, j, k: (0, k)),
        grid=(seq // t.rows, 1, d_ff // t.ff_k), tb=True, extras=[after(token_w2)],
        out_shape=[sds((seq, d_model), F32)], out_specs=[_sp((t.rows, d_model), row_full)])
    def block_epi(acc, i, ex, out):
        out[0][0] = acc.astype(BF16)

    fs = d_ff // N_DEV
    g_w1, dz1, dz1b, g_mix_g, g_mix_b = _matmul(
        "grad_w_mlp_in", x1b, dpre, _sp((seq, t.grad_rows), lambda i, j, k: (0, i)),
        _sp((seq, fs), lambda i, j, k: (0, j)),
        grid=(d_model // t.grad_rows, N_DEV, 1), j_outer=True, ta=True,
        out_shape=[sds((N_DEV, d_model, fs), BF16)],
        out_specs=[_sp((1, t.grad_rows, fs), lambda i, j, k: (j, i, 0))], epilogue=block_epi,
        side=_ln_bwd_side(dx1_mlp, dz2, z1, ln_mix_g, ln_mix_b, d_model // t.grad_rows * N_DEV))

    (dy,) = _matmul(
        "mix_dy", dz1b, wout_full, _sp((t.rows, d_model), lambda i, j, k: (i, 0)),
        _sp((dh, d_model), lambda i, j, k: (j, 0)),
        grid=(seq // t.rows, 2, 1), j_outer=True, tb=True,
        out_shape=[sds((seq, d_model), F32)], out_specs=[_sp((t.rows, dh), lambda i, j, k: (i, j))],
        epilogue=plain_epi)
    (g_wout,) = _matmul(
        "grad_w_out", y, dz1b, _sp((seq, t.grad_rows), lambda i, j, k: (0, i)), _sp((seq, d_model), vec, single=True),
        grid=(d_model // t.grad_rows, 1, 1), ta=True,
        out_shape=[sds((d_model, d_model), BF16)], out_specs=[_sp((t.grad_rows, d_model), row_full)],
        epilogue=bf16_epi)
    scatter_w1 = _SplitReduceScatter("scatter_w_mlp_in", [g_w1, g_wout.reshape(N_DEV, d_model // N_DEV, d_model)])

    dproj_pool, g_wpool, g_pscale = _pool_bwd(d_pool, dy, wpool_full, pool_scale, scatter_w1.token, seq, d_model)
    token_w1 = scatter_w1.combine_and_send(after=dproj_pool)
    dproj, g_pack, g_convb, g_wcat = _lru_bwd(p3, dy, h0p, h1p, dproj_pool, pack_full, conv_b, wcat,
                                              token_w1, seq, d_model)
    g_wa = jnp.stack([g_wcat[:, :, 0:lh], g_wcat[:, :, 2 * lh:3 * lh]])
    g_wi = jnp.stack([g_wcat[:, :, lh:2 * lh], g_wcat[:, :, 3 * lh:4 * lh]])

    rep_parts = [_rows128(g_wa), _rows128(g_wi), _rows128(g_mix_g), _rows128(g_mix_b), _rows128(g_ffn_g),
                 _rows128(g_ffn_b), _rows128(g_pscale), _rows128(g_convb)]
    rep_rows = [p.shape[0] for p in rep_parts]
    n_rep = sum(rep_rows)
    small = jnp.concatenate(rep_parts + [_rows128(g_pack), loss_part], axis=0)
    small_gather = _SplitGather("gather_small_grads", [(small[None], 0)], F32, after=small)

    ws = 3 * dh // N_DEV

    def pair_epi(acc, i, ex, out):
        out[0][0] = acc[:, :ws].astype(BF16)
        out[0][1] = acc[:, ws:].astype(BF16)

    def adam_big(name, own_landed, w, m, v):
        own, landed = own_landed
        shp = w.shape
        two = lambda a: a.reshape(-1, shp[-1])
        res = _sum_adamw(name, own, landed, two(w), two(m), two(v))
        return [r.reshape(shp) for r in res]

    (r_w2,) = scatter_w2.wait(after=small_gather.token)
    n_steps = d_model // t.grad_rows * (N_DEV // 2)
    g_win, *o_w2 = _matmul(
        "grad_w_in", xb, dproj, _sp((seq, t.grad_rows), lambda i, j, k: (0, i)),
        _sp((seq, 2 * ws), lambda i, j, k: (0, j)),
        grid=(d_model // t.grad_rows, N_DEV // 2, 1), ta=True,
        out_shape=[sds((N_DEV, d_model, ws), BF16)],
        out_specs=[_sp((2, t.grad_rows, ws), lambda i, j, k: (j, i, 0))], epilogue=pair_epi,
        side=_adamw_side(r_w2[0], r_w2[1], w_mlp_out[0], m_w_mlp_out[0], v_w_mlp_out[0], n_steps))
    o_w2 = [r.reshape(w_mlp_out.shape) for r in o_w2]
    scatter_mix = _SplitReduceScatter(
        "scatter_mixer", [g_win, g_wpool.reshape(N_DEV, pg // N_DEV * len(POOL_WINDOWS), pg)])

    r_w1, r_wout = scatter_w1.wait(after=scatter_mix.token)
    o_wout = adam_big("adam_w_out", r_wout, w_out, m_w_out, v_w_out)
    o_w1 = adam_big("adam_w_mlp_in", r_w1, w_mlp_in, m_w_mlp_in, v_w_mlp_in)
    token_mix = scatter_mix.combine_and_send(after=o_w1[0])

    def dx_epi(acc, i, ex, out):
        out[0][...] = ALPHA * ex[0][...] + acc

    (dx,) = _matmul(
        "grad_x", dproj, win_full, _sp((t.ln_rows * 2, 3 * dh), lambda i, j, k: (i, 0)),
        _sp((d_model, 3 * dh), vec, single=True),
        grid=(seq // (t.ln_rows * 2), 1, 1), tb=True,
        extras=[(dz1, _sp((t.ln_rows * 2, d_model), row_full)), after(token_mix)],
        out_shape=[sds((seq, d_model), F32)], out_specs=[_sp((t.ln_rows * 2, d_model), row_full)],
        epilogue=dx_epi)
    r_win, r_wpool = scatter_mix.wait(after=dx)
    o_win = adam_big("adam_w_in", r_win, w_in, m_w_in, v_w_in)
    o_wpool = adam_big("adam_w_pool", r_wpool, w_pool, m_w_pool, v_w_pool)

    small_gather.relay(after=o_win[0])
    (small_all,) = small_gather.wait(after=o_wpool[0])

    rep_w = [w_rg_a, w_rg_i, ln_mix_g, ln_mix_b, ln_ffn_g, ln_ffn_b, pool_scale, conv_b]
    rep_m = [m_w_rg_a, m_w_rg_i, m_ln_mix_g, m_ln_mix_b, m_ln_ffn_g, m_ln_ffn_b, m_pool_scale, m_conv_b]
    rep_v = [v_w_rg_a, v_w_rg_i, v_ln_mix_g, v_ln_mix_b, v_ln_ffn_g, v_ln_ffn_b, v_pool_scale, v_conv_b]
    cat = lambda arrs: jnp.concatenate([_rows128(a) for a in arrs], axis=0)
    o_rep = _sum_adamw("adam_replicated", None, small_all, cat(rep_w), cat(rep_m), cat(rep_v))

    my_idx = _dev_index(_where_am_i())
    head_parts = lax.dynamic_slice_in_dim(small_all, n_rep + my_idx * SMALL_ROWS, SMALL_ROWS, axis=1)
    o_head = _sum_adamw("adam_head", None, head_parts, pack_mine,
                        small_pack(m_conv_w, m_b_rg_a, m_b_rg_i, m_rg_lambda),
                        small_pack(v_conv_w, v_b_rg_a, v_b_rg_i, v_rg_lambda))

    def unpack_rep(packed):
        out, r = [], 0
        for wgt, rows in zip(rep_w, rep_rows):
            out.append(packed[r:r + rows].reshape(wgt.shape))
            r += rows
        return out

    def unpack_head(packed):
        return [packed[0:4].reshape(conv_w.shape), packed[4:6].reshape(b_rg_a.shape),
                packed[6:8].reshape(b_rg_i.shape), packed[8:10].reshape(rg_lambda.shape)]

    loss = _sum_blocks("loss_sum", small_all[:, n_rep + N_HEADS * SMALL_ROWS:, :])[0, 0]

    outs = [loss, dx[None]]
    for kind in range(4):
        ra, ri, mg, mb, fg, fb, ps, cb = unpack_rep(o_rep[kind])
        cw, ba, bi, lam = unpack_head(o_head[kind])
        outs += [mg, mb, o_win[kind], o_wpool[kind], ps, cw, cb, ra, ba, ri, bi, lam, o_wout[kind], fg, fb,
                 o_w1[kind], o_w2[kind]]
    return tuple(outs)
```

```python
import functools

import jax
import jax.numpy as jnp
from jax import lax
from jax.experimental import pallas as pl
from jax.experimental.pallas import tpu as pltpu

F32 = jnp.float32
BF16 = jnp.bfloat16
MESH = pl.DeviceIdType.MESH
ANY = pl.BlockSpec(memory_space=pl.ANY)

N_DEV = 8
POOL_WINDOWS = (2, 4, 8, 16)
N_HEADS = 8
RG_C = 8.0
LN_EPS = 1e-5
ALPHA = 2.0 ** 0.25
ADAM_LR = 0.001
ADAM_B1 = 0.9
ADAM_B2 = 0.999
ADAM_EPS = 1e-08
ADAM_WD = 0.01
ADAM_STEP = 10

VMEM_LIMIT = 56 * 1024 * 1024
SEQ_CHUNK = 128
WIN_HALO = 16
CONV_HALO = 8
SMALL_ROWS = 16


def _params(n_grid):
    return pltpu.CompilerParams(dimension_semantics=("arbitrary",) * n_grid, vmem_limit_bytes=VMEM_LIMIT)


def _shift(v, j):
    n = v.shape[0]
    s = (-j) % n
    return v if s == 0 else pltpu.roll(v, s, 0)


def _sigmoid(x):
    return 0.5 * jnp.tanh(0.5 * x) + 0.5


def _softplus(z):
    e = jnp.exp(-jnp.abs(z))
    u = 1.0 + e
    log1p = jnp.where(u == 1.0, e, jnp.log(u) * (e / jnp.where(u == 1.0, 1.0, u - 1.0)))
    return jnp.maximum(z, 0.0) + log1p


_GELU_C = 0.7978845608028654
_GELU_K = 0.044715


def _gelu_and_grad(x):
    x2 = x * x
    t = jnp.tanh(_GELU_C * (x + _GELU_K * x * x2))
    g = 0.5 * x * (1.0 + t)
    dg = 0.5 * (1.0 + t) + 0.5 * x * (1.0 - t * t) * (_GELU_C * (1.0 + 3.0 * _GELU_K * x2))
    return g, dg


def _ln_fwd(z, g, b):
    mu = jnp.mean(z, axis=-1, keepdims=True)
    zc = z - mu
    var = jnp.mean(zc * zc, axis=-1, keepdims=True)
    rstd = lax.rsqrt(var + LN_EPS)
    xhat = zc * rstd
    return xhat * g + b, xhat, rstd


def _ln_bwd(dy, xhat, rstd, g):
    dxhat = dy * g
    m1 = jnp.mean(dxhat, axis=-1, keepdims=True)
    m2 = jnp.mean(dxhat * xhat, axis=-1, keepdims=True)
    dz = rstd * (dxhat - m1 - xhat * m2)
    dg = jnp.sum(dy * xhat, axis=0, keepdims=True)
    db = jnp.sum(dy, axis=0, keepdims=True)
    return dz, dg, db


def _acc_rows(ref, first, val):
    @pl.when(first)
    def _():
        ref[...] = val

    @pl.when(jnp.logical_not(first))
    def _():
        ref[...] += val


def _sp(shape, fn, single=False):
    return shape, fn, single


def _matmul(name, a, b, a_spec, b_spec, *, grid, j_outer=False, ta=False, tb=False, extras=(), out_shape, out_specs,
            epilogue=None, n_split=1, side=None):
    ni, nj, nk = grid
    n_ex = len(extras)
    dims = (((0 if ta else 1,), (1 if tb else 0,)), ((), ()))
    side_in, side_shape, side_out, side_fn = side if side is not None else ((), (), (), None)
    n_main_out = len(out_shape)
    inner = ni if j_outer else nj

    def mk(spec):
        shape, fn, single = spec
        index = (lambda g0, g1, g2: fn(g1, g0, g2)) if j_outer else fn
        return pl.BlockSpec(shape, index, pipeline_mode=pl.Buffered(1)) if single else pl.BlockSpec(shape, index)

    def mk_side(block, fn):
        return pl.BlockSpec(block, lambda g0, g1, g2: fn(g0 * inner + g1))

    def body(a_ref, b_ref, *rest):
        ex_refs = rest[:n_ex]
        out_refs = rest[n_ex + len(side_in):n_ex + len(side_in) + n_main_out]
        if side_fn is not None:
            side_fn(pl.program_id(0) * inner + pl.program_id(1), rest[n_ex:n_ex + len(side_in)],
                    rest[n_ex + len(side_in) + n_main_out:])
        i = pl.program_id(1 if j_outer else 0)
        if n_split > 1:
            av = a_ref[...].astype(BF16)
            width = b_ref.shape[0 if tb else 1] // n_split
            for c in range(n_split):
                cols = pl.ds(c * width, width)
                bv = (b_ref[cols, :] if tb else b_ref[:, cols]).astype(BF16)
                epilogue(lax.dot_general(av, bv, dims, preferred_element_type=F32), i, ex_refs, out_refs, cols)
            return
        part = lax.dot_general(a_ref[...].astype(BF16), b_ref[...].astype(BF16), dims, preferred_element_type=F32)
        if nk == 1:
            epilogue(part, i, ex_refs, out_refs)
        else:
            @pl.when(pl.program_id(2) == 0)
            def _():
                out_refs[0][...] = part

            @pl.when(pl.program_id(2) > 0)
            def _():
                out_refs[0][...] += part

    return pl.pallas_call(
        body, name=name, grid=(nj, ni, nk) if j_outer else (ni, nj, nk),
        in_specs=[mk(a_spec), mk(b_spec)] + [mk(s) for _, s in extras] + [mk_side(blk, fn) for _, blk, fn in side_in],
        out_specs=[mk(s) for s in out_specs] + [mk_side(blk, fn) for blk, fn in side_out],
        out_shape=list(out_shape) + list(side_shape),
        compiler_params=_params(3),
    )(a, b, *[x for x, _ in extras], *[x for x, _, _ in side_in])


def _bs(shape, fn):
    return pl.BlockSpec(shape, fn)


def _where_am_i():
    x, y, c = lax.axis_index("x"), lax.axis_index("y"), lax.axis_index("c")
    return x, y, c


def _dev_index(p):
    return 4 * p[0] + 2 * p[1] + p[2]


def _slab(ref, axis, idx, size):
    sl = [slice(None)] * len(ref.shape)
    sl[axis] = pl.ds(idx * size, size)
    return ref.at[tuple(sl)]


HBM = pl.BlockSpec(memory_space=pltpu.HBM)
SEM = pl.BlockSpec(memory_space=pltpu.SEMAPHORE)
DATAFLOW = pltpu.SideEffectType.DATAFLOW_SIDE_EFFECTING


def _in_hbm(a):
    return pltpu.with_memory_space_constraint(a, pltpu.HBM)


def _token_shape():
    return jax.ShapeDtypeStruct((8, 128), F32)


def _split_start(name, n_sems, bufs, issue):
    nb = len(bufs)

    def body(*refs):
        issue(refs[:nb], refs[nb], refs[nb + 1])
        refs[-1][...] = jnp.zeros((8, 128), F32)

    outs = pl.pallas_call(
        body, name=name,
        out_shape=(pltpu.SemaphoreType.DMA((n_sems,)), pltpu.SemaphoreType.DMA((n_sems,)),
                   *[pltpu.HBM(b.shape, b.dtype) for b in bufs], _token_shape()),
        in_specs=[HBM] * nb, out_specs=(SEM, SEM, *[HBM] * nb, pl.BlockSpec(memory_space=pltpu.VMEM)),
        input_output_aliases={i: 2 + i for i in range(nb)},
        compiler_params=pltpu.CompilerParams(has_side_effects=DATAFLOW),
    )(*[_in_hbm(b) for b in bufs])
    return outs[0], outs[1], list(outs[2:2 + nb]), outs[-1]


def _split_relay(name, n_sems, sems, bufs, after, relay):
    nb = len(bufs)

    def body(*refs):
        relay(refs[:nb], refs[nb], refs[nb + 1], refs[nb + 3], refs[nb + 4])
        refs[-1][...] = jnp.zeros((8, 128), F32)

    outs = pl.pallas_call(
        body, name=name,
        out_shape=(pltpu.SemaphoreType.DMA((n_sems,)), pltpu.SemaphoreType.DMA((n_sems,)),
                   *[pltpu.HBM(b.shape, b.dtype) for b in bufs], _token_shape()),
        in_specs=[HBM] * nb + [SEM, SEM, ANY],
        out_specs=(SEM, SEM, *[HBM] * nb, pl.BlockSpec(memory_space=pltpu.VMEM)),
        input_output_aliases={i: 2 + i for i in range(nb)},
        compiler_params=pltpu.CompilerParams(has_side_effects=DATAFLOW),
    )(*bufs, sems[0], sems[1], after)
    return outs[0], outs[1], list(outs[2:2 + nb]), outs[-1]


def _split_wait(name, sems, bufs, after, finish):
    nb = len(bufs)

    def body(*refs):
        finish(refs[:nb], refs[nb], refs[nb + 1])

    outs = pl.pallas_call(
        body, name=name, out_shape=[pltpu.HBM(b.shape, b.dtype) for b in bufs],
        in_specs=[HBM] * nb + [SEM, SEM, ANY], out_specs=[HBM] * nb,
        input_output_aliases={i: i for i in range(nb)},
        compiler_params=pltpu.CompilerParams(has_side_effects=DATAFLOW),
    )(*bufs, sems[0], sems[1], after)
    return list(outs)


def _place(name, items, dtype, after):
    ids = jnp.reshape(_dev_index(_where_am_i()), (1,)).astype(jnp.int32)
    outs = []
    for a, (shard, axis) in enumerate(items):
        rows, cols = shard.shape[-2], shard.shape[-1]
        tr = rows
        while tr * cols * shard.dtype.itemsize > 4 * 1024 * 1024 and tr % 32 == 0:
            tr //= 2
        nt = rows // tr
        full = list(shard.shape)
        full[axis] *= N_DEV
        if shard.ndim == 2 and axis == 0:
            in_spec = _bs((tr, cols), lambda i, ids: (i, 0))
            out_spec = _bs((tr, cols), lambda i, ids, nt=nt: (ids[0] * nt + i, 0))
        elif shard.ndim == 2 and axis == 1:
            in_spec = _bs((tr, cols), lambda i, ids: (i, 0))
            out_spec = _bs((tr, cols), lambda i, ids: (i, ids[0]))
        elif shard.ndim == 3 and axis == 1:
            tr, nt = rows, shard.shape[0]
            in_spec = _bs((None, rows, cols), lambda i, ids: (i, 0, 0))
            out_spec = _bs((None, rows, cols), lambda i, ids: (i, ids[0], 0))
        else:
            assert shard.ndim == 3 and axis == 0 and shard.shape[0] == 1
            in_spec = _bs((None, tr, cols), lambda i, ids: (0, i, 0))
            out_spec = _bs((None, tr, cols), lambda i, ids: (ids[0], i, 0))

        def body(ids_ref, in_ref, after_ref, out_ref):
            del ids_ref, after_ref
            out_ref[...] = in_ref[...].astype(out_ref.dtype)

        outs.append(pl.pallas_call(
            body, name=f"{name}{a}",
            grid_spec=pltpu.PrefetchScalarGridSpec(
                num_scalar_prefetch=1, grid=(nt,), in_specs=[in_spec, ANY], out_specs=out_spec),
            out_shape=jax.ShapeDtypeStruct(tuple(full), dtype), compiler_params=_params(1),
        )(ids, shard, after))
    return outs


class _SplitGather:
    def __init__(self, name, items, dtype, after):
        self.name, self.items, self.n = name, items, len(items)
        fulls = _place(name + "_place", items, dtype, after)
        n = self.n

        def issue(refs, send, recv):
            me, sibling, chips, c = self._geometry()
            for a in range(n):
                self._copy1(refs, send, recv, a, 0, me, sibling).start()
                for j, chip in enumerate(chips):
                    self._copy1(refs, send, recv, a, 1 + j, me, (*chip, c)).start()

        self.send, self.recv, self.bufs, self.token = _split_start(name + "_start", 4 * n, fulls, issue)

    @staticmethod
    def _geometry():
        x, y, c = _where_am_i()
        return (x, y, c), (x, y, 1 - c), [(1 - x, y), (x, 1 - y), (1 - x, 1 - y)], c

    def _blk(self, refs, a, p):
        shard, axis = self.items[a]
        return _slab(refs[a], axis, _dev_index(p), shard.shape[axis])

    def _copy1(self, refs, send, recv, a, k, owner, to):
        return pltpu.make_async_remote_copy(
            src_ref=self._blk(refs, a, owner), dst_ref=self._blk(refs, a, owner), send_sem=send.at[4 * a + k],
            recv_sem=recv.at[4 * a + k], device_id=to, device_id_type=MESH)

    def _copy2(self, refs, send, recv, a, j, owner, to):
        return pltpu.make_async_remote_copy(
            src_ref=self._blk(refs, a, owner), dst_ref=self._blk(refs, a, owner), send_sem=send.at[3 * a + j],
            recv_sem=recv.at[3 * a + j], device_id=to, device_id_type=MESH)

    def relay(self, after):
        n = self.n

        def relay(refs, send_in, recv_in, send_out, recv_out):
            me, sibling, chips, c = self._geometry()
            for a in range(n):
                for j, chip in enumerate(chips):
                    self._copy1(refs, send_in, recv_in, a, 1 + j, (*chip, c), me).wait_recv()
                    self._copy2(refs, send_out, recv_out, a, j, (*chip, c), sibling).start()
            for a in range(n):
                self._copy1(refs, send_in, recv_in, a, 0, sibling, me).wait_recv()
                for k in range(4):
                    self._copy1(refs, send_in, recv_in, a, k, me, sibling).wait_send()

        self.send, self.recv, self.bufs, self.token = _split_relay(
            self.name + "_relay", 3 * n, (self.send, self.recv), self.bufs, after, relay)
        return self.token

    def wait(self, after):
        n = self.n

        def finish(refs, send, recv):
            me, sibling, chips, c = self._geometry()
            for a in range(n):
                for j, chip in enumerate(chips):
                    self._copy2(refs, send, recv, a, j, (*chip, 1 - c), me).wait_recv()
                    self._copy2(refs, send, recv, a, j, (*chip, c), sibling).wait_send()

        return _split_wait(self.name + "_wait", (self.send, self.recv), self.bufs, after, finish)


class _SplitReduceScatter:
    def __init__(self, name, grads):
        self.name, self.n = name, len(grads)
        n = self.n
        g4 = [g.reshape(4, 2, *g.shape[1:]) for g in grads]
        land = [lax.empty((4, 1, *g.shape[1:]), g.dtype) for g in grads]

        def issue(refs, send, recv):
            for a in range(n):
                self._swap(refs, send, recv, a).start()

        self.send, self.recv, self.bufs, self.token = _split_start(name + "_d2d_start", n, g4 + land, issue)

    def _swap(self, refs, send, recv, a):
        x, y, c = _where_am_i()
        return pltpu.make_async_remote_copy(
            src_ref=refs[a].at[:, pl.ds(1 - c, 1)], dst_ref=refs[self.n + a], send_sem=send.at[a], recv_sem=recv.at[a],
            device_id=(x, y, 1 - c), device_id_type=MESH)

    def _hop(self, refs, send, recv, a, m):
        x, y, c = _where_am_i()
        px = (1 - x) if m & 2 else x
        py = (1 - y) if m & 1 else y
        return pltpu.make_async_remote_copy(
            src_ref=refs[a].at[2 * px + py], dst_ref=refs[self.n + a].at[m - 1], send_sem=send.at[3 * a + m - 1],
            recv_sem=recv.at[3 * a + m - 1], device_id=(px, py, c), device_id_type=MESH)

    def combine_and_send(self, after):
        n = self.n

        def finish(refs, send, recv):
            for a in range(n):
                self._swap(refs, send, recv, a).wait()

        bufs = _split_wait(self.name + "_d2d_wait", (self.send, self.recv), self.bufs, after, finish)
        x, y, c = _where_am_i()
        ids = jnp.stack([c, 2 * x + y]).astype(jnp.int32)
        self.own, sums = [], []
        for a in range(n):
            own, hb = _pair_sum(f"{self.name}_sum{a}", bufs[a], bufs[n + a], ids)
            self.own.append(own)
            sums.append(hb)
        land = [lax.empty((3, *h.shape[1:]), h.dtype) for h in sums]

        def issue(refs, send, recv):
            for a in range(n):
                for m in (1, 2, 3):
                    self._hop(refs, send, recv, a, m).start()

        self.send, self.recv, self.bufs, self.token = _split_start(self.name + "_ici_start", 3 * n, sums + land, issue)
        return self.token

    def wait(self, after):
        n = self.n

        def finish(refs, send, recv):
            for a in range(n):
                for m in (1, 2, 3):
                    self._hop(refs, send, recv, a, m).wait()

        bufs = _split_wait(self.name + "_ici_wait", (self.send, self.recv), self.bufs, after, finish)
        return list(zip(self.own, bufs[n:]))


def _pair_sum(name, g4, land, ids):
    rows, cols = g4.shape[2], g4.shape[3]
    tr = rows
    while tr * cols * 2 > 1024 * 1024 and tr % 32 == 0:
        tr //= 2

    def body(ids_ref, g_ref, l_ref, own_ref, sum_ref):
        h = g_ref[...].astype(F32) + l_ref[...].astype(F32)
        sum_ref[...] = h.astype(sum_ref.dtype)

        @pl.when(pl.program_id(1) == ids_ref[1])
        def _():
            own_ref[...] = h

    return pl.pallas_call(
        body, name=name,
        grid_spec=pltpu.PrefetchScalarGridSpec(
            num_scalar_prefetch=1, grid=(rows // tr, 4),
            in_specs=[_bs((None, None, tr, cols), lambda i, q, ids: (q, ids[0], i, 0)),
                      _bs((None, None, tr, cols), lambda i, q, ids: (q, 0, i, 0))],
            out_specs=[_bs((tr, cols), lambda i, q, ids: (i, 0)), _bs((None, tr, cols), lambda i, q, ids: (q, i, 0))]),
        out_shape=[jax.ShapeDtypeStruct((rows, cols), F32), jax.ShapeDtypeStruct((4, rows, cols), g4.dtype)],
        compiler_params=_params(2),
    )(ids, g4, land)


def _win_sum(ext, w, off):
    s = ext + _shift(ext, -1)
    if w >= 4:
        s = _shift(s, -1) + _shift(s, 1)
    if w >= 8:
        s = _shift(s, -2) + _shift(s, 2)
    if w >= 16:
        s = _shift(s, -4) + _shift(s, 4)
    return _shift(s, off) if off else s


def _inv_count(r0, t, w, seq):
    pos = r0 + lax.broadcasted_iota(jnp.int32, (t, 1), 0)
    cnt = jnp.minimum(pos + w // 2, seq) - jnp.maximum(pos - w // 2, 0)
    return 1.0 / cnt.astype(F32)


def _pool_fwd(p3, w_pool, pool_scale, seq, d_model):
    dp = d_model // 2
    pg = dp // len(POOL_WINDOWS)
    t = min(SEQ_CHUNK, seq)
    n_chunks = seq // t
    h = WIN_HALO

    def body(u_ref, w_ref, sc_ref, d_ref, y_ref, pad_ref):
        g = pl.program_id(0)
        zeros = jnp.zeros((h, pg), F32)
        pad_ref[0:h, :] = zeros
        pad_ref[h + seq:h + seq + h, :] = zeros

        def fill(ci, _):
            r0 = pl.multiple_of(ci * t, t)
            pad_ref[pl.ds(h + r0, t), :] = u_ref[pl.ds(r0, t), :]
            return 0

        lax.fori_loop(0, n_chunks, fill, 0)
        wmat = w_ref[...]
        scale = sc_ref[...]
        for gi, w in enumerate(POOL_WINDOWS):
            @pl.when(g == gi)
            def _(w=w):
                def chunk(ci, _):
                    r0 = pl.multiple_of(ci * t, t)
                    ext = pad_ref[pl.ds(r0, t + 2 * h), :]
                    mean = _win_sum(ext, w, 0)[h:h + t, :] * _inv_count(r0, t, w, seq)
                    d = (mean - ext[h:h + t, :]).astype(BF16)
                    d_ref[pl.ds(r0, t), :] = d
                    q = jnp.dot(d, wmat, preferred_element_type=F32)
                    y_ref[pl.ds(r0, t), :] = (q * scale).astype(BF16)
                    return 0

                lax.fori_loop(0, n_chunks, chunk, 0, unroll=2)

    return pl.pallas_call(
        body, name="pool_fwd", grid=(len(POOL_WINDOWS),),
        in_specs=[_bs((None, seq, pg), lambda g: (0, 0, g)), _bs((None, pg, pg), lambda g: (g, 0, 0)),
                  _bs((1, pg), lambda g: (0, g))],
        out_specs=[_bs((seq, pg), lambda g: (0, g)), _bs((seq, pg), lambda g: (0, g))],
        out_shape=[jax.ShapeDtypeStruct((seq, dp), BF16), jax.ShapeDtypeStruct((seq, d_model), BF16)],
        scratch_shapes=[pltpu.VMEM((seq + 2 * h, pg), F32)],
        compiler_params=_params(1),
    )(p3, w_pool, pool_scale)


def _pool_bwd(d, dy, w_pool, pool_scale, token, seq, d_model):
    dp = d_model // 2
    pg = dp // len(POOL_WINDOWS)
    t = min(SEQ_CHUNK, seq)
    n_chunks = seq // t
    h = WIN_HALO
    tn_dims = (((0,), (0,)), ((), ()))
    nt_dims = (((1,), (1,)), ((), ()))

    def body(d_ref, dy_ref, w_ref, sc_ref, tok_ref, du_ref, dwb_ref, dsc_ref, pad_ref, dd_ref, dw_ref):
        del tok_ref
        g = pl.program_id(0)
        zeros = jnp.zeros((h, pg), F32)
        pad_ref[0:h, :] = zeros
        pad_ref[h + seq:h + seq + h, :] = zeros
        wmat = w_ref[...]
        scale = sc_ref[...]
        for gi, w in enumerate(POOL_WINDOWS):
            @pl.when(g == gi)
            def _(w=w):
                dw_ref[...] = jnp.zeros((pg, pg), F32)

                def first(ci, dsc):
                    r0 = pl.multiple_of(ci * t, t)
                    dv = d_ref[pl.ds(r0, t), :]
                    dyv = dy_ref[pl.ds(r0, t), :]
                    q = jnp.dot(dv, wmat, preferred_element_type=F32)
                    dsc = dsc + jnp.sum(dyv * q, axis=0, keepdims=True)
                    dq = (dyv * scale).astype(BF16)
                    dw_ref[...] += lax.dot_general(dv, dq, tn_dims, preferred_element_type=F32)
                    dd = lax.dot_general(dq, wmat, nt_dims, preferred_element_type=F32)
                    dd_ref[pl.ds(r0, t), :] = dd
                    pad_ref[pl.ds(h + r0, t), :] = dd * _inv_count(r0, t, w, seq)
                    return dsc

                def first_pair(cj, dsc):
                    return first(2 * cj + 1, first(2 * cj, dsc))

                dsc_ref[...] = lax.fori_loop(0, n_chunks // 2, first_pair, jnp.zeros((1, pg), F32))
                dwb_ref[...] = dw_ref[...].reshape(N_DEV, pg // N_DEV, pg).astype(BF16)

                def second(ci, _):
                    r0 = pl.multiple_of(ci * t, t)
                    ext = pad_ref[pl.ds(r0, t + 2 * h), :]
                    back = _win_sum(ext, w, 1)[h:h + t, :]
                    du_ref[pl.ds(r0, t), :] = (back - dd_ref[pl.ds(r0, t), :]).astype(BF16)
                    return 0

                lax.fori_loop(0, n_chunks, second, 0, unroll=2)

    return pl.pallas_call(
        body, name="pool_bwd", grid=(len(POOL_WINDOWS),),
        in_specs=[_bs((seq, pg), lambda g: (0, g)), _bs((seq, pg), lambda g: (0, g)),
                  _bs((None, pg, pg), lambda g: (g, 0, 0)), _bs((1, pg), lambda g: (0, g)),
                  _bs((8, 128), lambda g: (0, 0))],
        out_specs=[_bs((seq, pg), lambda g: (0, g)), _bs((N_DEV, None, pg // N_DEV, pg), lambda g: (0, g, 0, 0)),
                   _bs((1, pg), lambda g: (0, g))],
        out_shape=[jax.ShapeDtypeStruct((seq, 3 * dp), BF16),
                   jax.ShapeDtypeStruct((N_DEV, len(POOL_WINDOWS), pg // N_DEV, pg), BF16),
                   jax.ShapeDtypeStruct((1, dp), F32)],
        scratch_shapes=[pltpu.VMEM((seq + 2 * h, pg), F32), pltpu.VMEM((seq, pg), F32), pltpu.VMEM((pg, pg), F32)],
        compiler_params=_params(1),
    )(d, dy, w_pool, pool_scale, token)


def _tile_scan(n_tiles, lanes, loads, stores):
    row = lax.broadcasted_iota(jnp.int32, (8, lanes), 0)
    group = 8

    def local_scan(n, k):
        aa, bb = loads[n](k)
        for sh in (1, 2, 4):
            if n == 0:
                ok = row >= sh
                ap = jnp.where(ok, pltpu.roll(aa, sh, 0), 1.0)
                bp = jnp.where(ok, pltpu.roll(bb, sh, 0), 0.0)
            else:
                ok = row < 8 - sh
                ap = jnp.where(ok, pltpu.roll(aa, 8 - sh, 0), 1.0)
                bp = jnp.where(ok, pltpu.roll(bb, 8 - sh, 0), 0.0)
            bb = aa * bp + bb
            aa = aa * ap
        return aa, bb

    def step(s, carry):
        carry = list(carry)
        for n in range(2):
            tiles = [s * group + u if n == 0 else n_tiles - 1 - (s * group + u) for u in range(group)]
            local = [local_scan(n, k) for k in tiles]
            for k, (aa, bb) in zip(tiles, local):
                hh = bb + aa * carry[n]
                stores[n](k, hh)
                carry[n] = jnp.broadcast_to(hh[7:8, :] if n == 0 else hh[0:1, :], (8, lanes))
        return tuple(carry)

    zeros = jnp.zeros((8, lanes), F32)
    lax.fori_loop(0, n_tiles // group, step, (zeros, zeros))


def _gate_preacts(xc, wcat_ref):
    xcb = xc.astype(BF16)
    return xcb, jnp.dot(xcb, wcat_ref[...], preferred_element_type=F32)


def _gates(pre, n, pk_ref, sp):
    lh = pre.shape[1] // 4
    r = _sigmoid(pre[:, (2 * n) * lh:(2 * n + 1) * lh] + pk_ref[pl.ds(4 + n, 1), :])
    i = _sigmoid(pre[:, (2 * n + 1) * lh:(2 * n + 2) * lh] + pk_ref[pl.ds(6 + n, 1), :])
    log_a = (-RG_C * r) * sp[n]
    a = jnp.exp(log_a)
    x = 2.0 * log_a
    one_minus_a2 = jnp.where(x > -0.01, -(x * (1.0 + x * (0.5 + x * (1.0 / 6.0)))), 1.0 - a * a)
    m = jnp.sqrt(one_minus_a2)
    return r, i, a, m


def _conv_chunk(upad_ref, pk_ref, cb, r0, t):
    ext = upad_ref[pl.ds(r0, t + 2 * CONV_HALO), :]
    acc = pk_ref[pl.ds(1, 1), :] * ext
    for k in (0, 2, 3):
        acc = acc + pk_ref[pl.ds(k, 1), :] * _shift(ext, k - 1)
    return acc[CONV_HALO:CONV_HALO + t, :] + cb, ext


def _lru_fwd(p3, y_in, pack, conv_b, wcat, token, seq, d_model):
    dl = d_model // 2
    lh = dl // N_HEADS
    t = min(SEQ_CHUNK, seq)
    n_chunks = seq // t
    hal = CONV_HALO
    first_rec_block = (d_model - dl) // lh

    def body(ur_ref, ug_ref, pk_ref, cb_ref, wcat_ref, yin_ref, tok_ref, y_ref, h0_ref, h1_ref,
             upad, a_scr, b_scr):
        del yin_ref, tok_ref
        zeros = jnp.zeros((hal, lh), F32)
        upad[0:hal, :] = zeros
        upad[hal + seq:hal + seq + hal, :] = zeros
        for ref in (h0_ref, h1_ref):
            ref[0:hal, :] = zeros
            ref[hal + seq:hal + seq + hal, :] = zeros

        def fill(ci, _):
            r0 = pl.multiple_of(ci * t, t)
            upad[pl.ds(hal + r0, t), :] = ur_ref[pl.ds(r0, t), :]
            return 0

        lax.fori_loop(0, n_chunks, fill, 0)
        cb = cb_ref[...]
        sp = [_softplus(-pk_ref[pl.ds(8 + n, 1), :]) for n in range(2)]

        def chunk(ci, _):
            r0 = pl.multiple_of(ci * t, t)
            xc, _ext = _conv_chunk(upad, pk_ref, cb, r0, t)
            _, pre = _gate_preacts(xc, wcat_ref)
            for n in range(2):
                _, i, a, m = _gates(pre, n, pk_ref, sp)
                a_scr[n, pl.ds(r0, t), :] = a
                b_scr[n, pl.ds(r0, t), :] = (m * i) * xc
            return 0

        lax.fori_loop(0, n_chunks, chunk, 0, unroll=2)

        def load(n):
            def get(k):
                at = pl.ds(pl.multiple_of(k * 8, 8), 8)
                return a_scr[n, at, :], b_scr[n, at, :]
            return get

        def store(ref):
            def put(k, v):
                ref[pl.ds(pl.multiple_of(hal + k * 8, 8), 8), :] = v
            return put

        _tile_scan(seq // 8, lh, [load(0), load(1)], [store(h0_ref), store(h1_ref)])

        def out(ci, _):
            r0 = pl.multiple_of(ci * t, t)
            hsum = h0_ref[pl.ds(hal + r0, t), :] + h1_ref[pl.ds(hal + r0, t), :]
            gl, _dg = _gelu_and_grad(ug_ref[pl.ds(r0, t), :])
            y_ref[pl.ds(r0, t), :] = (hsum * gl).astype(BF16)
            return 0

        lax.fori_loop(0, n_chunks, out, 0)

    return pl.pallas_call(
        body, name="lru_fwd", grid=(N_HEADS,),
        in_specs=[_bs((None, seq, lh), lambda h: (1, 0, h)), _bs((None, seq, lh), lambda h: (2, 0, h)),
                  _bs((None, SMALL_ROWS, lh), lambda h: (h, 0, 0)), _bs((1, lh), lambda h: (0, h)),
                  _bs((None, lh, 4 * lh), lambda h: (h, 0, 0)),
                  ANY, _bs((8, 128), lambda h: (0, 0))],
        out_specs=[_bs((seq, lh), lambda h: (0, first_rec_block + h)),
                   _bs((seq + 2 * hal, lh), lambda h: (0, h)), _bs((seq + 2 * hal, lh), lambda h: (0, h))],
        out_shape=[jax.ShapeDtypeStruct((seq, d_model), BF16), jax.ShapeDtypeStruct((seq + 2 * hal, dl), F32),
                   jax.ShapeDtypeStruct((seq + 2 * hal, dl), F32)],
        scratch_shapes=[pltpu.VMEM((seq + 2 * hal, lh), F32), pltpu.VMEM((2, seq, lh), F32),
                        pltpu.VMEM((2, seq, lh), F32)],
        input_output_aliases={5: 0},
        compiler_params=_params(1),
    )(p3, p3, pack, conv_b, wcat, y_in, token)


def _lru_bwd(p3, dy, h0p, h1p, dproj_in, pack, conv_b, wcat, token, seq, d_model):
    dl = d_model // 2
    lh = dl // N_HEADS
    t = min(SEQ_CHUNK, seq)
    n_chunks = seq // t
    hal = CONV_HALO
    first_rec_block = (d_model - dl) // lh
    tn_dims = (((0,), (0,)), ((), ()))
    nt_dims = (((1,), (1,)), ((), ()))

    def body(ur_ref, ug_ref, dy_ref, h0_ref, h1_ref, pk_ref, cb_ref, wcat_ref, tok_ref, din_ref,
             dproj_ref, dpk_ref, dcb_ref, dwcat_ref,
             upad, a_scr, dh_scr, g_scr, dxc_pad, dpr_ref, out_sems, gate_scr):
        del din_ref, tok_ref
        zeros = jnp.zeros((hal, lh), F32)
        for ref in (upad, dxc_pad):
            ref[0:hal, :] = zeros
            ref[hal + seq:hal + seq + hal, :] = zeros
        for n in range(2):
            a_scr[n, 0:hal, :] = zeros
            a_scr[n, hal + seq:hal + seq + hal, :] = zeros

        def fill(ci, _):
            r0 = pl.multiple_of(ci * t, t)
            upad[pl.ds(hal + r0, t), :] = ur_ref[pl.ds(r0, t), :]
            return 0

        lax.fori_loop(0, n_chunks, fill, 0)
        cb = cb_ref[...]
        lam = [pk_ref[pl.ds(8 + n, 1), :] for n in range(2)]
        sp = [_softplus(-lam[n]) for n in range(2)]

        def chunk1(ci, _):
            r0 = pl.multiple_of(ci * t, t)
            xc, _ext = _conv_chunk(upad, pk_ref, cb, r0, t)
            _, pre = _gate_preacts(xc, wcat_ref)
            for n in range(2):
                r, i, a, m = _gates(pre, n, pk_ref, sp)
                a_scr[n, pl.ds(hal + r0, t), :] = a
                for q, v in enumerate((r, i, m)):
                    gate_scr[3 * n + q, pl.ds(r0, t), :] = v
            hsum = h0_ref[pl.ds(hal + r0, t), :] + h1_ref[pl.ds(hal + r0, t), :]
            gl, dgl = _gelu_and_grad(ug_ref[pl.ds(r0, t), :])
            dyv = dy_ref[pl.ds(r0, t), :]
            dh_scr[pl.ds(r0, t), :] = dyv * gl
            dpr_ref[1, pl.ds(r0, t), :] = ((dyv * hsum) * dgl).astype(BF16)
            return 0

        lax.fori_loop(0, n_chunks, chunk1, 0, unroll=2)

        def load(n):
            def get(k):
                r0 = pl.multiple_of(k * 8, 8)
                if n == 0:
                    coef = _shift(a_scr[0, pl.ds(pl.multiple_of(hal + r0, 8), 16), :], 1)[0:8, :]
                else:
                    coef = _shift(a_scr[1, pl.ds(pl.multiple_of(hal + r0 - 8, 8), 16), :], -1)[8:16, :]
                return coef, dh_scr[pl.ds(r0, 8), :]
            return get

        def store(n):
            def put(k, v):
                g_scr[n, pl.ds(pl.multiple_of(k * 8, 8), 8), :] = v
            return put

        _tile_scan(seq // 8, lh, [load(1), load(0)], [store(1), store(0)])

        dwcat_ref[...] = jnp.zeros((lh, 4 * lh), F32)

        def chunk3(ci, carry):
            dba, dbi, dlam, dcb = carry
            r0 = pl.multiple_of(ci * t, t)
            xc, _ext = _conv_chunk(upad, pk_ref, cb, r0, t)
            xcb = xc.astype(BF16)
            dxc = jnp.zeros((t, lh), F32)
            dba, dbi, dlam = list(dba), list(dbi), list(dlam)
            dpre = []
            for n in range(2):
                r, i, m = (gate_scr[3 * n + q, pl.ds(r0, t), :] for q in range(3))
                a = a_scr[n, pl.ds(hal + r0, t), :]
                hext = (h0_ref if n == 0 else h1_ref)[pl.ds(r0, t + 2 * hal), :]
                hprev = _shift(hext, -1 if n == 0 else 1)[hal:hal + t, :]
                gb = g_scr[n, pl.ds(r0, t), :]
                da = gb * hprev
                dm = gb * i * xc
                di = gb * m * xc
                dxc = dxc + gb * (m * i)
                dlog_a = da * a - dm * (a * a) / m
                dr = dlog_a * (-RG_C * sp[n])
                dlam[n] = dlam[n] + jnp.sum(dlog_a * r, axis=0, keepdims=True)
                dpr = dr * r * (1.0 - r)
                dpi = di * i * (1.0 - i)
                dba[n] = dba[n] + jnp.sum(dpr, axis=0, keepdims=True)
                dbi[n] = dbi[n] + jnp.sum(dpi, axis=0, keepdims=True)
                dpre += [dpr.astype(BF16), dpi.astype(BF16)]
            dpre = jnp.concatenate(dpre, axis=1)
            dwcat_ref[...] += lax.dot_general(xcb, dpre, tn_dims, preferred_element_type=F32)
            dxc = dxc + lax.dot_general(dpre, wcat_ref[...], nt_dims, preferred_element_type=F32)
            dxc_pad[pl.ds(hal + r0, t), :] = dxc
            dcb = dcb + jnp.sum(dxc, axis=0, keepdims=True)
            return tuple(dba), tuple(dbi), tuple(dlam), dcb

        zr = jnp.zeros((1, lh), F32)
        def chunk3_pair(cj, carry):
            return chunk3(2 * cj + 1, chunk3(2 * cj, carry))

        dba, dbi, dlam, dcb = lax.fori_loop(0, n_chunks // 2, chunk3_pair, ((zr, zr), (zr, zr), (zr, zr), zr))
        dcb_ref[...] = dcb
        for n in range(2):
            dpk_ref[pl.ds(4 + n, 1), :] = dba[n]
            dpk_ref[pl.ds(6 + n, 1), :] = dbi[n]
            dpk_ref[pl.ds(8 + n, 1), :] = dlam[n] * (RG_C * jax.nn.sigmoid(-lam[n]))
        dpk_ref[pl.ds(10, SMALL_ROWS - 10), :] = jnp.zeros((SMALL_ROWS - 10, lh), F32)

        def chunk4(ci, dtap):
            r0 = pl.multiple_of(ci * t, t)
            gext = dxc_pad[pl.ds(r0, t + 2 * hal), :]
            uext = upad[pl.ds(r0, t + 2 * hal), :]
            gmid = gext[hal:hal + t, :]
            du = pk_ref[pl.ds(1, 1), :] * gext
            for k in (0, 2, 3):
                du = du + pk_ref[pl.ds(k, 1), :] * _shift(gext, 1 - k)
            dpr_ref[0, pl.ds(r0, t), :] = du[hal:hal + t, :].astype(BF16)
            out = []
            for k in range(4):
                usl = _shift(uext, k - 1)[hal:hal + t, :]
                out.append(dtap[k] + jnp.sum(gmid * usl, axis=0, keepdims=True))
            return tuple(out)

        dtap = lax.fori_loop(0, n_chunks, chunk4, (zr, zr, zr, zr))
        for k in range(4):
            dpk_ref[pl.ds(k, 1), :] = dtap[k]

        head = pl.program_id(0)
        outs = [pltpu.make_async_copy(
            dpr_ref.at[b], dproj_ref.at[:, pl.ds(pl.multiple_of((1 + b) * dl + head * lh, lh), lh)], out_sems.at[b])
            for b in range(2)]
        for cp in outs:
            cp.start()
        for cp in outs:
            cp.wait()

    return pl.pallas_call(
        body, name="lru_bwd", grid=(N_HEADS,),
        in_specs=[_bs((None, seq, lh), lambda h: (1, 0, h)), _bs((None, seq, lh), lambda h: (2, 0, h)),
                  _bs((seq, lh), lambda h: (0, first_rec_block + h)),
                  _bs((seq + 2 * hal, lh), lambda h: (0, h)), _bs((seq + 2 * hal, lh), lambda h: (0, h)),
                  _bs((None, SMALL_ROWS, lh), lambda h: (h, 0, 0)), _bs((1, lh), lambda h: (0, h)),
                  _bs((None, lh, 4 * lh), lambda h: (h, 0, 0)),
                  _bs((8, 128), lambda h: (0, 0)), ANY],
        out_specs=[ANY, _bs((None, SMALL_ROWS, lh), lambda h: (h, 0, 0)),
                   _bs((1, lh), lambda h: (0, h)), _bs((None, lh, 4 * lh), lambda h: (h, 0, 0))],
        out_shape=[jax.ShapeDtypeStruct((seq, 3 * dl), BF16), jax.ShapeDtypeStruct((N_HEADS, SMALL_ROWS, lh), F32),
                   jax.ShapeDtypeStruct((1, dl), F32), jax.ShapeDtypeStruct((N_HEADS, lh, 4 * lh), F32)],
        scratch_shapes=[pltpu.VMEM((seq + 2 * hal, lh), F32), pltpu.VMEM((2, seq + 2 * hal, lh), F32),
                        pltpu.VMEM((seq, lh), F32), pltpu.VMEM((2, seq, lh), F32),
                        pltpu.VMEM((seq + 2 * hal, lh), F32), pltpu.VMEM((2, seq, lh), BF16),
                        pltpu.SemaphoreType.DMA((2,)), pltpu.VMEM((6, seq, lh), F32)],
        input_output_aliases={9: 0},
        compiler_params=_params(1),
    )(p3, p3, dy, h0p, h1p, pack, conv_b, wcat, token, dproj_in)


class _tiles:
    def __init__(self, seq, d_model, d_ff):
        self.rows = min(1024, seq)
        self.ln_rows = min(256, seq)
        self.ff_cols = min(1024, d_ff)
        self.ff_split = 4
        self.ff_k = min(2048, d_ff)
        self.grad_rows = 512


def _ln_loss_bwd(ffn, x1, tgt, g, b, tr):
    seq, d = ffn.shape

    def body(f_ref, x_ref, t_ref, g_ref, b_ref, dz_ref, dzb_ref, dg_ref, db_ref, loss_ref):
        i = pl.program_id(0)
        gv = g_ref[...]
        z = ALPHA * x_ref[...] + f_ref[...]
        y, xhat, rstd = _ln_fwd(z, gv, b_ref[...])
        err = y - t_ref[...]
        part = 0.5 * jnp.sum(jnp.mean(err * err, axis=-1, keepdims=True), axis=0, keepdims=True)
        dz, dg, db = _ln_bwd(err * (1.0 / d), xhat, rstd, gv)
        dz_ref[...] = dz
        dzb_ref[...] = dz.astype(BF16)
        _acc_rows(dg_ref, i == 0, dg)
        _acc_rows(db_ref, i == 0, db)
        _acc_rows(loss_ref, i == 0, jnp.broadcast_to(part, (8, 128)))

    row = _bs((tr, d), lambda i: (i, 0))
    vec = _bs((1, d), lambda i: (0, 0))
    return pl.pallas_call(
        body, name="ln_ffn_loss", grid=(seq // tr,), in_specs=[row, row, row, vec, vec],
        out_specs=[row, row, vec, vec, _bs((8, 128), lambda i: (0, 0))],
        out_shape=[jax.ShapeDtypeStruct((seq, d), F32), jax.ShapeDtypeStruct((seq, d), BF16),
                   jax.ShapeDtypeStruct((1, d), F32), jax.ShapeDtypeStruct((1, d), F32),
                   jax.ShapeDtypeStruct((8, 128), F32)],
        compiler_params=_params(1),
    )(ffn, x1, tgt, g, b)


def _ln_bwd_side(dx_branch, dres, z, g, b, n_steps):
    seq, d = z.shape
    tr = seq // n_steps

    def fn(step, ins, outs):
        a_ref, r_ref, z_ref, g_ref, b_ref = ins
        dz_ref, dzb_ref, dg_ref, db_ref = outs
        gv = g_ref[...]
        _, xhat, rstd = _ln_fwd(z_ref[...], gv, b_ref[...])
        dz, dg, db = _ln_bwd(ALPHA * r_ref[...] + a_ref[...], xhat, rstd, gv)
        dz_ref[...] = dz
        dzb_ref[...] = dz.astype(BF16)
        _acc_rows(dg_ref, step == 0, dg)
        _acc_rows(db_ref, step == 0, db)

    row = ((tr, d), lambda s: (s, 0))
    vec = ((1, d), lambda s: (0, 0))
    shapes = [jax.ShapeDtypeStruct((seq, d), F32), jax.ShapeDtypeStruct((seq, d), BF16),
              jax.ShapeDtypeStruct((1, d), F32), jax.ShapeDtypeStruct((1, d), F32)]
    return [(dx_branch, *row), (dres, *row), (z, *row), (g, *vec), (b, *vec)], shapes, [row, row, vec, vec], fn


def _to_bf16(name, a, token):
    rows, cols = a.shape
    tr = min(512, rows)

    def body(a_ref, tok_ref, o_ref):
        del tok_ref
        o_ref[...] = a_ref[...].astype(BF16)

    return pl.pallas_call(
        body, name=name, grid=(rows // tr,),
        in_specs=[_bs((tr, cols), lambda i: (i, 0)), _bs((8, 128), lambda i: (0, 0))],
        out_specs=_bs((tr, cols), lambda i: (i, 0)), out_shape=jax.ShapeDtypeStruct((rows, cols), BF16),
        compiler_params=_params(1),
    )(a, token)


def _sum_blocks(name, parts):
    def body(p_ref, o_ref):
        acc = p_ref[0]
        for s in range(1, parts.shape[0]):
            acc = acc + p_ref[s]
        o_ref[...] = acc

    return pl.pallas_call(body, name=name, out_shape=jax.ShapeDtypeStruct(parts.shape[1:], F32))(parts)


def _adamw_values(w, g, m, v):
    m = ADAM_B1 * m + (1.0 - ADAM_B1) * g
    v = ADAM_B2 * v + (1.0 - ADAM_B2) * (g * g)
    m_hat = m / (1.0 - ADAM_B1 ** ADAM_STEP)
    v_hat = v / (1.0 - ADAM_B2 ** ADAM_STEP)
    delta = -ADAM_LR * (m_hat / (jnp.sqrt(v_hat) + ADAM_EPS) + ADAM_WD * w)
    return delta, m, v


def _adamw_side(own, parts, w, m, v, n_steps):
    rows, cols = w.shape
    tr = rows // n_steps

    def fn(step, ins, outs):
        o_ref, p_ref, w_ref, m_ref, v_ref = ins
        g = o_ref[...]
        for s in range(parts.shape[0]):
            g = g + p_ref[s].astype(F32)
        delta, mn, vn = _adamw_values(w_ref[...], g, m_ref[...], v_ref[...])
        for ref, val in zip(outs, (g, delta, mn, vn)):
            ref[...] = val

    row = ((tr, cols), lambda s: (s, 0))
    stack = ((parts.shape[0], tr, cols), lambda s: (0, s, 0))
    shapes = [jax.ShapeDtypeStruct((rows, cols), F32)] * 4
    return [(own, *row), (parts, *stack), (w, *row), (m, *row), (v, *row)], shapes, [row] * 4, fn


def _sum_adamw(name, own, parts, w, m, v):
    rows, cols = w.shape
    n_parts = parts.shape[0]
    tr = rows
    min_rows = 8 if parts.dtype == F32 else 16
    while tr * cols * 4 > 1024 * 1024 and tr % (2 * min_rows) == 0:
        tr //= 2

    def body(*refs):
        if own is None:
            p_ref, w_ref, m_ref, v_ref, g_ref, d_ref, mo_ref, vo_ref = refs
            g = p_ref[0].astype(F32)
            rest = range(1, n_parts)
        else:
            o_ref, p_ref, w_ref, m_ref, v_ref, g_ref, d_ref, mo_ref, vo_ref = refs
            g = o_ref[...]
            rest = range(n_parts)
        for s in rest:
            g = g + p_ref[s].astype(F32)
        delta, mn, vn = _adamw_values(w_ref[...], g, m_ref[...], v_ref[...])
        g_ref[...] = g
        d_ref[...] = delta
        mo_ref[...] = mn
        vo_ref[...] = vn

    spec = _bs((tr, cols), lambda i: (i, 0))
    lead = [] if own is None else [own]
    return pl.pallas_call(
        body, name=name, grid=(rows // tr,),
        in_specs=[spec] * len(lead) + [_bs((n_parts, tr, cols), lambda i: (0, i, 0)), spec, spec, spec],
        out_specs=[spec] * 4, out_shape=[jax.ShapeDtypeStruct((rows, cols), F32)] * 4,
        compiler_params=_params(1),
    )(*lead, parts, w, m, v)


def _rows128(a):
    return a.reshape(-1, 128)


def kernel(x, ln_mix_g, ln_mix_b, w_in, w_pool, pool_scale, conv_w, conv_b, w_rg_a, b_rg_a, w_rg_i, b_rg_i, rg_lambda, w_out, ln_ffn_g, ln_ffn_b, w_mlp_in, w_mlp_out, loss_target, m_ln_mix_g, m_ln_mix_b, m_w_in, m_w_pool, m_pool_scale, m_conv_w, m_conv_b, m_w_rg_a, m_b_rg_a, m_w_rg_i, m_b_rg_i, m_rg_lambda, m_w_out, m_ln_ffn_g, m_ln_ffn_b, m_w_mlp_in, m_w_mlp_out, v_ln_mix_g, v_ln_mix_b, v_w_in, v_w_pool, v_pool_scale, v_conv_w, v_conv_b, v_w_rg_a, v_b_rg_a, v_w_rg_i, v_b_rg_i, v_rg_lambda, v_w_out, v_ln_ffn_g, v_ln_ffn_b, v_w_mlp_in, v_w_mlp_out):
    seq, d_model = x.shape[1], x.shape[2]
    dh = d_model // 2
    lh = dh // N_HEADS
    pg = dh // len(POOL_WINDOWS)
    d_ff = w_mlp_in.shape[2] * N_DEV
    assert lh == 128 and conv_w.shape[3] == lh and w_pool.shape[2] * N_DEV == pg

    xs = x[0]
    tgt = loss_target[0]

    def small_pack(cw, ba, bi, lam):
        return jnp.concatenate([cw.reshape(4, lh), ba.reshape(2, lh), bi.reshape(2, lh), lam.reshape(2, lh),
                                jnp.zeros((SMALL_ROWS - 10, lh), F32)], axis=0)

    pack_mine = small_pack(conv_w, b_rg_a, b_rg_i, rg_lambda)
    pack_bits = lax.bitcast_convert_type(pack_mine, BF16).reshape(1, SMALL_ROWS, 2 * lh)
    win_gather = _SplitGather("gather_w_in", [(w_in[0], 1), (w_pool[0], 1), (pack_bits, 0)], BF16, after=pack_mine)
    wout_gather = _SplitGather("gather_w_out", [(w_out[0], 0)], BF16, after=win_gather.token)
    w1_gather = _SplitGather("gather_w_mlp_in", [(w_mlp_in[0], 1)], BF16, after=wout_gather.token)
    w2_gather = _SplitGather("gather_w_mlp_out", [(w_mlp_out[0], 0)], BF16, after=w1_gather.token)
    xb = _to_bf16("x_bf16", x[0], w2_gather.token)
    win_full, wpool_full, pack_bits_full = win_gather.wait(after=win_gather.relay(after=xb))
    pack_full = lax.bitcast_convert_type(pack_bits_full.reshape(N_DEV, SMALL_ROWS, lh, 2), F32)
    wcat = jnp.concatenate([w_rg_a[0, 0], w_rg_i[0, 0], w_rg_a[0, 1], w_rg_i[0, 1]], axis=-1).astype(BF16)
    vec = lambda i, j, k: (0, 0)
    row_full = lambda i, j, k: (i, 0)

    def after(token):
        return (token, _sp((8, 128), vec))

    def sds(shape, dtype):
        return jax.ShapeDtypeStruct(shape, dtype)

    def plain_epi(acc, i, ex, out):
        out[0][...] = acc

    def bf16_epi(acc, i, ex, out):
        out[0][...] = acc.astype(BF16)

    t = _tiles(seq, d_model, d_ff)

    (p3,) = _matmul(
        "proj", xb, win_full, _sp((t.rows, d_model), lambda i, j, k: (i, 0)), _sp((d_model, dh), lambda i, j, k: (0, j)),
        grid=(seq // t.rows, 3, 1),
        out_shape=[sds((3, seq, dh), F32)], out_specs=[_sp((None, t.rows, dh), lambda i, j, k: (j, i, 0))],
        epilogue=plain_epi)

    d_pool, y_half = _pool_fwd(p3, wpool_full, pool_scale, seq, d_model)
    y, h0p, h1p = _lru_fwd(p3, y_half, pack_full, conv_b, wcat, wout_gather.relay(after=y_half), seq, d_model)
    (wout_full,) = wout_gather.wait(after=y)
    relay_token = w1_gather.relay(after=wout_full)

    mix_rows = 2 * t.ln_rows

    def mix_epi(acc, i, ex, out):
        x_ref, g_ref, b_ref = ex[:3]
        for part in range(2):
            rows = pl.ds(part * t.ln_rows, t.ln_rows)
            z = ALPHA * x_ref[rows, :] + acc[part * t.ln_rows:(part + 1) * t.ln_rows, :]
            x1, _, _ = _ln_fwd(z, g_ref[...], b_ref[...])
            out[0][rows, :] = z
            out[1][rows, :] = x1
            out[2][rows, :] = x1.astype(BF16)

    z1, x1, x1b = _matmul(
        "mix_out", y, wout_full, _sp((mix_rows, d_model), row_full), _sp((d_model, d_model), vec, single=True),
        grid=(seq // mix_rows, 1, 1),
        extras=[(xs, _sp((mix_rows, d_model), row_full)), (ln_mix_g, _sp((1, d_model), vec)),
                (ln_mix_b, _sp((1, d_model), vec)), after(relay_token)],
        out_shape=[sds((seq, d_model), F32), sds((seq, d_model), F32), sds((seq, d_model), BF16)],
        out_specs=[_sp((mix_rows, d_model), row_full)] * 3, epilogue=mix_epi)
    (w1_full,) = w1_gather.wait(after=x1b)

    def mlp_in_epi(acc, i, ex, out, cols):
        h = jnp.maximum(acc, 0.0)
        out[0][:, cols] = (h * h).astype(BF16)
        out[1][:, cols] = (2.0 * h).astype(BF16)

    hmid, dact = _matmul(
        "mlp_in", x1b, w1_full, _sp((t.rows, d_model), lambda i, j, k: (i, 0)),
        _sp((d_model, t.ff_cols), lambda i, j, k: (0, j)),
        grid=(seq // t.rows, d_ff // t.ff_cols, 1), j_outer=True,
        out_shape=[sds((seq, d_ff), BF16)] * 2, out_specs=[_sp((t.rows, t.ff_cols), lambda i, j, k: (i, j))] * 2,
        epilogue=mlp_in_epi, n_split=t.ff_split)
    (w2_full,) = w2_gather.wait(after=w2_gather.relay(after=hmid))

    (ffn,) = _matmul(
        "mlp_out", hmid, w2_full, _sp((t.rows, t.ff_k), lambda i, j, k: (i, k)),
        _sp((t.ff_k, d_model), lambda i, j, k: (k, 0)),
        grid=(seq // t.rows, 1, d_ff // t.ff_k),
        out_shape=[sds((seq, d_model), F32)], out_specs=[_sp((t.rows, d_model), row_full)])
    dz2, dz2b, g_ffn_g, g_ffn_b, loss_part = _ln_loss_bwd(ffn, x1, tgt, ln_ffn_g, ln_ffn_b, t.ln_rows)

    (g_w2,) = _matmul(
        "grad_w_mlp_out", hmid, dz2b, _sp((seq, t.grad_rows), lambda i, j, k: (0, i)),
        _sp((seq, d_model), vec, single=True),
        grid=(d_ff // t.grad_rows, 1, 1), ta=True,
        out_shape=[sds((d_ff, d_model), BF16)], out_specs=[_sp((t.grad_rows, d_model), row_full)],
        epilogue=bf16_epi)
    scatter_w2 = _SplitReduceScatter("scatter_w_mlp_out", [g_w2.reshape(N_DEV, d_ff // N_DEV, d_model)])

    def dpre_epi(acc, i, ex, out, cols):
        out[0][:, cols] = (acc * ex[0][:, cols].astype(F32)).astype(BF16)

    (dpre,) = _matmul(
        "mlp_dpre", dz2b, w2_full, _sp((t.rows, d_model), lambda i, j, k: (i, 0)),
        _sp((t.ff_cols, d_model), lambda i, j, k: (j, 0)),
        grid=(seq // t.rows, d_ff // t.ff_cols, 1), j_outer=True, tb=True,
        extras=[(dact, _sp((t.rows, t.ff_cols), lambda i, j, k: (i, j))), after(scatter_w2.token)],
        out_shape=[sds((seq, d_ff), BF16)], out_specs=[_sp((t.rows, t.ff_cols), lambda i, j, k: (i, j))],
        epilogue=dpre_epi, n_split=t.ff_split)
    token_w2 = scatter_w2.combine_and_send(after=dpre)

    (dx1_mlp,) = _matmul(
        "mlp_dx", dpre, w1_full, _sp((t.rows, t.ff_k), lambda i, j, k: (i, k)),
        _sp((d_model, t.ff_k), lambda i, j, k: (0, k)),
        grid=(seq // t.rows, 1, d_ff // t.ff_k), tb=True, extras=[after(token_w2)],
        out_shape=[sds((seq, d_model), F32)], out_specs=[_sp((t.rows, d_model), row_full)])
    def block_epi(acc, i, ex, out):
        out[0][0] = acc.astype(BF16)

    fs = d_ff // N_DEV
    g_w1, dz1, dz1b, g_mix_g, g_mix_b = _matmul(
        "grad_w_mlp_in", x1b, dpre, _sp((seq, t.grad_rows), lambda i, j, k: (0, i)),
        _sp((seq, fs), lambda i, j, k: (0, j)),
        grid=(d_model // t.grad_rows, N_DEV, 1), j_outer=True, ta=True,
        out_shape=[sds((N_DEV, d_model, fs), BF16)],
        out_specs=[_sp((1, t.grad_rows, fs), lambda i, j, k: (j, i, 0))], epilogue=block_epi,
        side=_ln_bwd_side(dx1_mlp, dz2, z1, ln_mix_g, ln_mix_b, d_model // t.grad_rows * N_DEV))

    (dy,) = _matmul(
        "mix_dy", dz1b, wout_full, _sp((t.rows, d_model), lambda i, j, k: (i, 0)),
        _sp((dh, d_model), lambda i, j, k: (j, 0)),
        grid=(seq // t.rows, 2, 1), j_outer=True, tb=True,
        out_shape=[sds((seq, d_model), F32)], out_specs=[_sp((t.rows, dh), lambda i, j, k: (i, j))],
        epilogue=plain_epi)
    (g_wout,) = _matmul(
        "grad_w_out", y, dz1b, _sp((seq, t.grad_rows), lambda i, j, k: (0, i)), _sp((seq, d_model), vec, single=True),
        grid=(d_model // t.grad_rows, 1, 1), ta=True,
        out_shape=[sds((d_model, d_model), BF16)], out_specs=[_sp((t.grad_rows, d_model), row_full)],
        epilogue=bf16_epi)
    scatter_w1 = _SplitReduceScatter("scatter_w_mlp_in", [g_w1, g_wout.reshape(N_DEV, d_model // N_DEV, d_model)])

    dproj_pool, g_wpool, g_pscale = _pool_bwd(d_pool, dy, wpool_full, pool_scale, scatter_w1.token, seq, d_model)
    token_w1 = scatter_w1.combine_and_send(after=dproj_pool)
    dproj, g_pack, g_convb, g_wcat = _lru_bwd(p3, dy, h0p, h1p, dproj_pool, pack_full, conv_b, wcat,
                                              token_w1, seq, d_model)
    g_wa = jnp.stack([g_wcat[:, :, 0:lh], g_wcat[:, :, 2 * lh:3 * lh]])
    g_wi = jnp.stack([g_wcat[:, :, lh:2 * lh], g_wcat[:, :, 3 * lh:4 * lh]])

    rep_parts = [_rows128(g_wa), _rows128(g_wi), _rows128(g_mix_g), _rows128(g_mix_b), _rows128(g_ffn_g),
                 _rows128(g_ffn_b), _rows128(g_pscale), _rows128(g_convb)]
    rep_rows = [p.shape[0] for p in rep_parts]
    n_rep = sum(rep_rows)
    small = jnp.concatenate(rep_parts + [_rows128(g_pack), loss_part], axis=0)
    small_gather = _SplitGather("gather_small_grads", [(small[None], 0)], F32, after=small)

    ws = 3 * dh // N_DEV

    def pair_epi(acc, i, ex, out):
        out[0][0] = acc[:, :ws].astype(BF16)
        out[0][1] = acc[:, ws:].astype(BF16)

    def adam_big(name, own_landed, w, m, v):
        own, landed = own_landed
        shp = w.shape
        two = lambda a: a.reshape(-1, shp[-1])
        res = _sum_adamw(name, own, landed, two(w), two(m), two(v))
        return [r.reshape(shp) for r in res]

    (r_w2,) = scatter_w2.wait(after=small_gather.token)
    n_steps = d_model // t.grad_rows * (N_DEV // 2)
    g_win, *o_w2 = _matmul(
        "grad_w_in", xb, dproj, _sp((seq, t.grad_rows), lambda i, j, k: (0, i)),
        _sp((seq, 2 * ws), lambda i, j, k: (0, j)),
        grid=(d_model // t.grad_rows, N_DEV // 2, 1), ta=True,
        out_shape=[sds((N_DEV, d_model, ws), BF16)],
        out_specs=[_sp((2, t.grad_rows, ws), lambda i, j, k: (j, i, 0))], epilogue=pair_epi,
        side=_adamw_side(r_w2[0], r_w2[1], w_mlp_out[0], m_w_mlp_out[0], v_w_mlp_out[0], n_steps))
    o_w2 = [r.reshape(w_mlp_out.shape) for r in o_w2]
    scatter_mix = _SplitReduceScatter(
        "scatter_mixer", [g_win, g_wpool.reshape(N_DEV, pg // N_DEV * len(POOL_WINDOWS), pg)])

    r_w1, r_wout = scatter_w1.wait(after=scatter_mix.token)
    o_w1 = adam_big("adam_w_mlp_in", r_w1, w_mlp_in, m_w_mlp_in, v_w_mlp_in)
    token_mix = scatter_mix.combine_and_send(after=o_w1[0])

    def dx_epi(acc, i, ex, out):
        out[0][...] = ALPHA * ex[0][...] + acc

    dx_rows = t.ln_rows * 2
    dx, *o_wout = _matmul(
        "grad_x", dproj, win_full, _sp((dx_rows, 3 * dh), lambda i, j, k: (i, 0)),
        _sp((d_model, 3 * dh), vec, single=True),
        grid=(seq // dx_rows, 1, 1), tb=True,
        extras=[(dz1, _sp((dx_rows, d_model), row_full)), after(token_mix)],
        out_shape=[sds((seq, d_model), F32)], out_specs=[_sp((dx_rows, d_model), row_full)],
        epilogue=dx_epi,
        side=_adamw_side(r_wout[0], r_wout[1], w_out[0], m_w_out[0], v_w_out[0], seq // dx_rows))
    o_wout = [r.reshape(w_out.shape) for r in o_wout]
    r_win, r_wpool = scatter_mix.wait(after=dx)
    o_win = adam_big("adam_w_in", r_win, w_in, m_w_in, v_w_in)
    o_wpool = adam_big("adam_w_pool", r_wpool, w_pool, m_w_pool, v_w_pool)

    small_gather.relay(after=o_win[0])
    (small_all,) = small_gather.wait(after=o_wpool[0])

    rep_w = [w_rg_a, w_rg_i, ln_mix_g, ln_mix_b, ln_ffn_g, ln_ffn_b, pool_scale, conv_b]
    rep_m = [m_w_rg_a, m_w_rg_i, m_ln_mix_g, m_ln_mix_b, m_ln_ffn_g, m_ln_ffn_b, m_pool_scale, m_conv_b]
    rep_v = [v_w_rg_a, v_w_rg_i, v_ln_mix_g, v_ln_mix_b, v_ln_ffn_g, v_ln_ffn_b, v_pool_scale, v_conv_b]
    cat = lambda arrs: jnp.concatenate([_rows128(a) for a in arrs], axis=0)
    o_rep = _sum_adamw("adam_replicated", None, small_all, cat(rep_w), cat(rep_m), cat(rep_v))

    my_idx = _dev_index(_where_am_i())
    head_parts = lax.dynamic_slice_in_dim(small_all, n_rep + my_idx * SMALL_ROWS, SMALL_ROWS, axis=1)
    o_head = _sum_adamw("adam_head", None, head_parts, pack_mine,
                        small_pack(m_conv_w, m_b_rg_a, m_b_rg_i, m_rg_lambda),
                        small_pack(v_conv_w, v_b_rg_a, v_b_rg_i, v_rg_lambda))

    def unpack_rep(packed):
        out, r = [], 0
        for wgt, rows in zip(rep_w, rep_rows):
            out.append(packed[r:r + rows].reshape(wgt.shape))
            r += rows
        return out

    def unpack_head(packed):
        return [packed[0:4].reshape(conv_w.shape), packed[4:6].reshape(b_rg_a.shape),
                packed[6:8].reshape(b_rg_i.shape), packed[8:10].reshape(rg_lambda.shape)]

    loss = _sum_blocks("loss_sum", small_all[:, n_rep + N_HEADS * SMALL_ROWS:, :])[0, 0]

    outs = [loss, dx[None]]
    for kind in range(4):
        ra, ri, mg, mb, fg, fb, ps, cb = unpack_rep(o_rep[kind])
        cw, ba, bi, lam = unpack_head(o_head[kind])
        outs += [mg, mb, o_win[kind], o_wpool[kind], ps, cw, cb, ra, ba, ri, bi, lam, o_wout[kind], fg, fb,
                 o_w1[kind], o_w2[kind]]
    return tuple(outs)
```

```python
import functools

import jax
import jax.numpy as jnp
from jax import lax
from jax.experimental import pallas as pl
from jax.experimental.pallas import tpu as pltpu

F32 = jnp.float32
BF16 = jnp.bfloat16
MESH = pl.DeviceIdType.MESH
ANY = pl.BlockSpec(memory_space=pl.ANY)

N_DEV = 8
POOL_WINDOWS = (2, 4, 8, 16)
N_HEADS = 8
RG_C = 8.0
LN_EPS = 1e-5
ALPHA = 2.0 ** 0.25
ADAM_LR = 0.001
ADAM_B1 = 0.9
ADAM_B2 = 0.999
ADAM_EPS = 1e-08
ADAM_WD = 0.01
ADAM_STEP = 10

VMEM_LIMIT = 56 * 1024 * 1024
SEQ_CHUNK = 128
WIN_HALO = 16
CONV_HALO = 8
SMALL_ROWS = 16


def _params(n_grid):
    return pltpu.CompilerParams(dimension_semantics=("arbitrary",) * n_grid, vmem_limit_bytes=VMEM_LIMIT)


def _shift(v, j):
    n = v.shape[0]
    s = (-j) % n
    return v if s == 0 else pltpu.roll(v, s, 0)


def _sigmoid(x):
    return 0.5 * jnp.tanh(0.5 * x) + 0.5


def _softplus(z):
    e = jnp.exp(-jnp.abs(z))
    u = 1.0 + e
    log1p = jnp.where(u == 1.0, e, jnp.log(u) * (e / jnp.where(u == 1.0, 1.0, u - 1.0)))
    return jnp.maximum(z, 0.0) + log1p


_GELU_C = 0.7978845608028654
_GELU_K = 0.044715


def _gelu_and_grad(x):
    x2 = x * x
    t = jnp.tanh(_GELU_C * (x + _GELU_K * x * x2))
    g = 0.5 * x * (1.0 + t)
    dg = 0.5 * (1.0 + t) + 0.5 * x * (1.0 - t * t) * (_GELU_C * (1.0 + 3.0 * _GELU_K * x2))
    return g, dg


def _ln_fwd(z, g, b):
    mu = jnp.mean(z, axis=-1, keepdims=True)
    zc = z - mu
    var = jnp.mean(zc * zc, axis=-1, keepdims=True)
    rstd = lax.rsqrt(var + LN_EPS)
    xhat = zc * rstd
    return xhat * g + b, xhat, rstd


def _ln_bwd(dy, xhat, rstd, g):
    dxhat = dy * g
    m1 = jnp.mean(dxhat, axis=-1, keepdims=True)
    m2 = jnp.mean(dxhat * xhat, axis=-1, keepdims=True)
    dz = rstd * (dxhat - m1 - xhat * m2)
    dg = jnp.sum(dy * xhat, axis=0, keepdims=True)
    db = jnp.sum(dy, axis=0, keepdims=True)
    return dz, dg, db


def _acc_rows(ref, first, val):
    @pl.when(first)
    def _():
        ref[...] = val

    @pl.when(jnp.logical_not(first))
    def _():
        ref[...] += val


def _sp(shape, fn, single=False):
    return shape, fn, single


def _matmul(name, a, b, a_spec, b_spec, *, grid, j_outer=False, ta=False, tb=False, extras=(), out_shape, out_specs,
            epilogue=None, n_split=1, side=None):
    ni, nj, nk = grid
    n_ex = len(extras)
    dims = (((0 if ta else 1,), (1 if tb else 0,)), ((), ()))
    side_in, side_shape, side_out, side_fn = side if side is not None else ((), (), (), None)
    n_main_out = len(out_shape)
    inner = ni if j_outer else nj

    def mk(spec):
        shape, fn, single = spec
        index = (lambda g0, g1, g2: fn(g1, g0, g2)) if j_outer else fn
        return pl.BlockSpec(shape, index, pipeline_mode=pl.Buffered(1)) if single else pl.BlockSpec(shape, index)

    def mk_side(block, fn):
        return pl.BlockSpec(block, lambda g0, g1, g2: fn(g0 * inner + g1))

    def body(a_ref, b_ref, *rest):
        ex_refs = rest[:n_ex]
        out_refs = rest[n_ex + len(side_in):n_ex + len(side_in) + n_main_out]
        if side_fn is not None:
            side_fn(pl.program_id(0) * inner + pl.program_id(1), rest[n_ex:n_ex + len(side_in)],
                    rest[n_ex + len(side_in) + n_main_out:])
        i = pl.program_id(1 if j_outer else 0)
        if n_split > 1:
            av = a_ref[...].astype(BF16)
            width = b_ref.shape[0 if tb else 1] // n_split
            for c in range(n_split):
                cols = pl.ds(c * width, width)
                bv = (b_ref[cols, :] if tb else b_ref[:, cols]).astype(BF16)
                epilogue(lax.dot_general(av, bv, dims, preferred_element_type=F32), i, ex_refs, out_refs, cols)
            return
        part = lax.dot_general(a_ref[...].astype(BF16), b_ref[...].astype(BF16), dims, preferred_element_type=F32)
        if nk == 1:
            epilogue(part, i, ex_refs, out_refs)
        else:
            @pl.when(pl.program_id(2) == 0)
            def _():
                out_refs[0][...] = part

            @pl.when(pl.program_id(2) > 0)
            def _():
                out_refs[0][...] += part

    return pl.pallas_call(
        body, name=name, grid=(nj, ni, nk) if j_outer else (ni, nj, nk),
        in_specs=[mk(a_spec), mk(b_spec)] + [mk(s) for _, s in extras] + [mk_side(blk, fn) for _, blk, fn in side_in],
        out_specs=[mk(s) for s in out_specs] + [mk_side(blk, fn) for blk, fn in side_out],
        out_shape=list(out_shape) + list(side_shape),
        compiler_params=_params(3),
    )(a, b, *[x for x, _ in extras], *[x for x, _, _ in side_in])


def _bs(shape, fn):
    return pl.BlockSpec(shape, fn)


def _where_am_i():
    x, y, c = lax.axis_index("x"), lax.axis_index("y"), lax.axis_index("c")
    return x, y, c


def _dev_index(p):
    return 4 * p[0] + 2 * p[1] + p[2]


def _slab(ref, axis, idx, size):
    sl = [slice(None)] * len(ref.shape)
    sl[axis] = pl.ds(idx * size, size)
    return ref.at[tuple(sl)]


HBM = pl.BlockSpec(memory_space=pltpu.HBM)
SEM = pl.BlockSpec(memory_space=pltpu.SEMAPHORE)
DATAFLOW = pltpu.SideEffectType.DATAFLOW_SIDE_EFFECTING


def _in_hbm(a):
    return pltpu.with_memory_space_constraint(a, pltpu.HBM)


def _token_shape():
    return jax.ShapeDtypeStruct((8, 128), F32)


def _split_start(name, n_sems, bufs, issue):
    nb = len(bufs)

    def body(*refs):
        issue(refs[:nb], refs[nb], refs[nb + 1])
        refs[-1][...] = jnp.zeros((8, 128), F32)

    outs = pl.pallas_call(
        body, name=name,
        out_shape=(pltpu.SemaphoreType.DMA((n_sems,)), pltpu.SemaphoreType.DMA((n_sems,)),
                   *[pltpu.HBM(b.shape, b.dtype) for b in bufs], _token_shape()),
        in_specs=[HBM] * nb, out_specs=(SEM, SEM, *[HBM] * nb, pl.BlockSpec(memory_space=pltpu.VMEM)),
        input_output_aliases={i: 2 + i for i in range(nb)},
        compiler_params=pltpu.CompilerParams(has_side_effects=DATAFLOW),
    )(*[_in_hbm(b) for b in bufs])
    return outs[0], outs[1], list(outs[2:2 + nb]), outs[-1]


def _split_relay(name, n_sems, sems, bufs, after, relay):
    nb = len(bufs)

    def body(*refs):
        relay(refs[:nb], refs[nb], refs[nb + 1], refs[nb + 3], refs[nb + 4])
        refs[-1][...] = jnp.zeros((8, 128), F32)

    outs = pl.pallas_call(
        body, name=name,
        out_shape=(pltpu.SemaphoreType.DMA((n_sems,)), pltpu.SemaphoreType.DMA((n_sems,)),
                   *[pltpu.HBM(b.shape, b.dtype) for b in bufs], _token_shape()),
        in_specs=[HBM] * nb + [SEM, SEM, ANY],
        out_specs=(SEM, SEM, *[HBM] * nb, pl.BlockSpec(memory_space=pltpu.VMEM)),
        input_output_aliases={i: 2 + i for i in range(nb)},
        compiler_params=pltpu.CompilerParams(has_side_effects=DATAFLOW),
    )(*bufs, sems[0], sems[1], after)
    return outs[0], outs[1], list(outs[2:2 + nb]), outs[-1]


def _split_wait(name, sems, bufs, after, finish):
    nb = len(bufs)

    def body(*refs):
        finish(refs[:nb], refs[nb], refs[nb + 1])

    outs = pl.pallas_call(
        body, name=name, out_shape=[pltpu.HBM(b.shape, b.dtype) for b in bufs],
        in_specs=[HBM] * nb + [SEM, SEM, ANY], out_specs=[HBM] * nb,
        input_output_aliases={i: i for i in range(nb)},
        compiler_params=pltpu.CompilerParams(has_side_effects=DATAFLOW),
    )(*bufs, sems[0], sems[1], after)
    return list(outs)


def _place(name, items, dtype, after):
    ids = jnp.reshape(_dev_index(_where_am_i()), (1,)).astype(jnp.int32)
    outs = []
    for a, (shard, axis) in enumerate(items):
        rows, cols = shard.shape[-2], shard.shape[-1]
        tr = rows
        while tr * cols * shard.dtype.itemsize > 4 * 1024 * 1024 and tr % 32 == 0:
            tr //= 2
        nt = rows // tr
        full = list(shard.shape)
        full[axis] *= N_DEV
        if shard.ndim == 2 and axis == 0:
            in_spec = _bs((tr, cols), lambda i, ids: (i, 0))
            out_spec = _bs((tr, cols), lambda i, ids, nt=nt: (ids[0] * nt + i, 0))
        elif shard.ndim == 2 and axis == 1:
            in_spec = _bs((tr, cols), lambda i, ids: (i, 0))
            out_spec = _bs((tr, cols), lambda i, ids: (i, ids[0]))
        elif shard.ndim == 3 and axis == 1:
            tr, nt = rows, shard.shape[0]
            in_spec = _bs((None, rows, cols), lambda i, ids: (i, 0, 0))
            out_spec = _bs((None, rows, cols), lambda i, ids: (i, ids[0], 0))
        else:
            assert shard.ndim == 3 and axis == 0 and shard.shape[0] == 1
            in_spec = _bs((None, tr, cols), lambda i, ids: (0, i, 0))
            out_spec = _bs((None, tr, cols), lambda i, ids: (ids[0], i, 0))

        def body(ids_ref, in_ref, after_ref, out_ref):
            del ids_ref, after_ref
            out_ref[...] = in_ref[...].astype(out_ref.dtype)

        outs.append(pl.pallas_call(
            body, name=f"{name}{a}",
            grid_spec=pltpu.PrefetchScalarGridSpec(
                num_scalar_prefetch=1, grid=(nt,), in_specs=[in_spec, ANY], out_specs=out_spec),
            out_shape=jax.ShapeDtypeStruct(tuple(full), dtype), compiler_params=_params(1),
        )(ids, shard, after))
    return outs


class _SplitGather:
    def __init__(self, name, items, dtype, after):
        self.name, self.items, self.n = name, items, len(items)
        fulls = _place(name + "_place", items, dtype, after)
        n = self.n

        def issue(refs, send, recv):
            me, sibling, chips, c = self._geometry()
            for a in range(n):
                self._copy1(refs, send, recv, a, 0, me, sibling).start()
                for j, chip in enumerate(chips):
                    self._copy1(refs, send, recv, a, 1 + j, me, (*chip, c)).start()

        self.send, self.recv, self.bufs, self.token = _split_start(name + "_start", 4 * n, fulls, issue)

    @staticmethod
    def _geometry():
        x, y, c = _where_am_i()
        return (x, y, c), (x, y, 1 - c), [(1 - x, y), (x, 1 - y), (1 - x, 1 - y)], c

    def _blk(self, refs, a, p):
        shard, axis = self.items[a]
        return _slab(refs[a], axis, _dev_index(p), shard.shape[axis])

    def _copy1(self, refs, send, recv, a, k, owner, to):
        return pltpu.make_async_remote_copy(
            src_ref=self._blk(refs, a, owner), dst_ref=self._blk(refs, a, owner), send_sem=send.at[4 * a + k],
            recv_sem=recv.at[4 * a + k], device_id=to, device_id_type=MESH)

    def _copy2(self, refs, send, recv, a, j, owner, to):
        return pltpu.make_async_remote_copy(
            src_ref=self._blk(refs, a, owner), dst_ref=self._blk(refs, a, owner), send_sem=send.at[3 * a + j],
            recv_sem=recv.at[3 * a + j], device_id=to, device_id_type=MESH)

    def relay(self, after):
        n = self.n

        def relay(refs, send_in, recv_in, send_out, recv_out):
            me, sibling, chips, c = self._geometry()
            for a in range(n):
                for j, chip in enumerate(chips):
                    self._copy1(refs, send_in, recv_in, a, 1 + j, (*chip, c), me).wait_recv()
                    self._copy2(refs, send_out, recv_out, a, j, (*chip, c), sibling).start()
            for a in range(n):
                self._copy1(refs, send_in, recv_in, a, 0, sibling, me).wait_recv()
                for k in range(4):
                    self._copy1(refs, send_in, recv_in, a, k, me, sibling).wait_send()

        self.send, self.recv, self.bufs, self.token = _split_relay(
            self.name + "_relay", 3 * n, (self.send, self.recv), self.bufs, after, relay)
        return self.token

    def wait(self, after):
        n = self.n

        def finish(refs, send, recv):
            me, sibling, chips, c = self._geometry()
            for a in range(n):
                for j, chip in enumerate(chips):
                    self._copy2(refs, send, recv, a, j, (*chip, 1 - c), me).wait_recv()
                    self._copy2(refs, send, recv, a, j, (*chip, c), sibling).wait_send()

        return _split_wait(self.name + "_wait", (self.send, self.recv), self.bufs, after, finish)


class _SplitReduceScatter:
    def __init__(self, name, grads):
        self.name, self.n = name, len(grads)
        n = self.n
        g4 = [g.reshape(4, 2, *g.shape[1:]) for g in grads]
        land = [lax.empty((4, 1, *g.shape[1:]), g.dtype) for g in grads]

        def issue(refs, send, recv):
            for a in range(n):
                self._swap(refs, send, recv, a).start()

        self.send, self.recv, self.bufs, self.token = _split_start(name + "_d2d_start", n, g4 + land, issue)

    def _swap(self, refs, send, recv, a):
        x, y, c = _where_am_i()
        return pltpu.make_async_remote_copy(
            src_ref=refs[a].at[:, pl.ds(1 - c, 1)], dst_ref=refs[self.n + a], send_sem=send.at[a], recv_sem=recv.at[a],
            device_id=(x, y, 1 - c), device_id_type=MESH)

    def _hop(self, refs, send, recv, a, m):
        x, y, c = _where_am_i()
        px = (1 - x) if m & 2 else x
        py = (1 - y) if m & 1 else y
        return pltpu.make_async_remote_copy(
            src_ref=refs[a].at[2 * px + py], dst_ref=refs[self.n + a].at[m - 1], send_sem=send.at[3 * a + m - 1],
            recv_sem=recv.at[3 * a + m - 1], device_id=(px, py, c), device_id_type=MESH)

    def combine_and_send(self, after):
        n = self.n

        def finish(refs, send, recv):
            for a in range(n):
                self._swap(refs, send, recv, a).wait()

        bufs = _split_wait(self.name + "_d2d_wait", (self.send, self.recv), self.bufs, after, finish)
        x, y, c = _where_am_i()
        ids = jnp.stack([c, 2 * x + y]).astype(jnp.int32)
        self.own, sums = [], []
        for a in range(n):
            own, hb = _pair_sum(f"{self.name}_sum{a}", bufs[a], bufs[n + a], ids)
            self.own.append(own)
            sums.append(hb)
        land = [lax.empty((3, *h.shape[1:]), h.dtype) for h in sums]

        def issue(refs, send, recv):
            for a in range(n):
                for m in (1, 2, 3):
                    self._hop(refs, send, recv, a, m).start()

        self.send, self.recv, self.bufs, self.token = _split_start(self.name + "_ici_start", 3 * n, sums + land, issue)
        return self.token

    def wait(self, after):
        n = self.n

        def finish(refs, send, recv):
            for a in range(n):
                for m in (1, 2, 3):
                    self._hop(refs, send, recv, a, m).wait()

        bufs = _split_wait(self.name + "_ici_wait", (self.send, self.recv), self.bufs, after, finish)
        return list(zip(self.own, bufs[n:]))


def _pair_sum(name, g4, land, ids):
    rows, cols = g4.shape[2], g4.shape[3]
    tr = rows
    while tr * cols * 2 > 1024 * 1024 and tr % 32 == 0:
        tr //= 2

    def body(ids_ref, g_ref, l_ref, own_ref, sum_ref):
        h = g_ref[...].astype(F32) + l_ref[...].astype(F32)
        sum_ref[...] = h.astype(sum_ref.dtype)

        @pl.when(pl.program_id(1) == ids_ref[1])
        def _():
            own_ref[...] = h

    return pl.pallas_call(
        body, name=name,
        grid_spec=pltpu.PrefetchScalarGridSpec(
            num_scalar_prefetch=1, grid=(rows // tr, 4),
            in_specs=[_bs((None, None, tr, cols), lambda i, q, ids: (q, ids[0], i, 0)),
                      _bs((None, None, tr, cols), lambda i, q, ids: (q, 0, i, 0))],
            out_specs=[_bs((tr, cols), lambda i, q, ids: (i, 0)), _bs((None, tr, cols), lambda i, q, ids: (q, i, 0))]),
        out_shape=[jax.ShapeDtypeStruct((rows, cols), F32), jax.ShapeDtypeStruct((4, rows, cols), g4.dtype)],
        compiler_params=_params(2),
    )(ids, g4, land)


def _win_sum(ext, w, off):
    s = ext + _shift(ext, -1)
    if w >= 4:
        s = _shift(s, -1) + _shift(s, 1)
    if w >= 8:
        s = _shift(s, -2) + _shift(s, 2)
    if w >= 16:
        s = _shift(s, -4) + _shift(s, 4)
    return _shift(s, off) if off else s


def _inv_count(r0, t, w, seq):
    pos = r0 + lax.broadcasted_iota(jnp.int32, (t, 1), 0)
    cnt = jnp.minimum(pos + w // 2, seq) - jnp.maximum(pos - w // 2, 0)
    return 1.0 / cnt.astype(F32)


def _pool_fwd(p3, w_pool, pool_scale, seq, d_model):
    dp = d_model // 2
    pg = dp // len(POOL_WINDOWS)
    t = min(SEQ_CHUNK, seq)
    n_chunks = seq // t
    h = WIN_HALO

    def body(u_ref, w_ref, sc_ref, d_ref, y_ref, pad_ref):
        g = pl.program_id(0)
        zeros = jnp.zeros((h, pg), F32)
        pad_ref[0:h, :] = zeros
        pad_ref[h + seq:h + seq + h, :] = zeros

        def fill(ci, _):
            r0 = pl.multiple_of(ci * t, t)
            pad_ref[pl.ds(h + r0, t), :] = u_ref[pl.ds(r0, t), :]
            return 0

        lax.fori_loop(0, n_chunks, fill, 0)
        wmat = w_ref[...]
        scale = sc_ref[...]
        for gi, w in enumerate(POOL_WINDOWS):
            @pl.when(g == gi)
            def _(w=w):
                def chunk(ci, _):
                    r0 = pl.multiple_of(ci * t, t)
                    ext = pad_ref[pl.ds(r0, t + 2 * h), :]
                    mean = _win_sum(ext, w, 0)[h:h + t, :] * _inv_count(r0, t, w, seq)
                    d = (mean - ext[h:h + t, :]).astype(BF16)
                    d_ref[pl.ds(r0, t), :] = d
                    q = jnp.dot(d, wmat, preferred_element_type=F32)
                    y_ref[pl.ds(r0, t), :] = (q * scale).astype(BF16)
                    return 0

                lax.fori_loop(0, n_chunks, chunk, 0, unroll=2)

    return pl.pallas_call(
        body, name="pool_fwd", grid=(len(POOL_WINDOWS),),
        in_specs=[_bs((None, seq, pg), lambda g: (0, 0, g)), _bs((None, pg, pg), lambda g: (g, 0, 0)),
                  _bs((1, pg), lambda g: (0, g))],
        out_specs=[_bs((seq, pg), lambda g: (0, g)), _bs((seq, pg), lambda g: (0, g))],
        out_shape=[jax.ShapeDtypeStruct((seq, dp), BF16), jax.ShapeDtypeStruct((seq, d_model), BF16)],
        scratch_shapes=[pltpu.VMEM((seq + 2 * h, pg), F32)],
        compiler_params=_params(1),
    )(p3, w_pool, pool_scale)


def _pool_bwd(d, dy, w_pool, pool_scale, token, seq, d_model):
    dp = d_model // 2
    pg = dp // len(POOL_WINDOWS)
    t = min(SEQ_CHUNK, seq)
    n_chunks = seq // t
    h = WIN_HALO
    tn_dims = (((0,), (0,)), ((), ()))
    nt_dims = (((1,), (1,)), ((), ()))

    def body(d_ref, dy_ref, w_ref, sc_ref, tok_ref, du_ref, dwb_ref, dsc_ref, pad_ref, dd_ref, dw_ref):
        del tok_ref
        g = pl.program_id(0)
        zeros = jnp.zeros((h, pg), F32)
        pad_ref[0:h, :] = zeros
        pad_ref[h + seq:h + seq + h, :] = zeros
        wmat = w_ref[...]
        scale = sc_ref[...]
        for gi, w in enumerate(POOL_WINDOWS):
            @pl.when(g == gi)
            def _(w=w):
                dw_ref[...] = jnp.zeros((pg, pg), F32)

                def first(ci, dsc):
                    r0 = pl.multiple_of(ci * t, t)
                    dv = d_ref[pl.ds(r0, t), :]
                    dyv = dy_ref[pl.ds(r0, t), :]
                    q = jnp.dot(dv, wmat, preferred_element_type=F32)
                    dsc = dsc + jnp.sum(dyv * q, axis=0, keepdims=True)
                    dq = (dyv * scale).astype(BF16)
                    dw_ref[...] += lax.dot_general(dv, dq, tn_dims, preferred_element_type=F32)
                    dd = lax.dot_general(dq, wmat, nt_dims, preferred_element_type=F32)
                    dd_ref[pl.ds(r0, t), :] = dd
                    pad_ref[pl.ds(h + r0, t), :] = dd * _inv_count(r0, t, w, seq)
                    return dsc

                def first_pair(cj, dsc):
                    return first(2 * cj + 1, first(2 * cj, dsc))

                dsc_ref[...] = lax.fori_loop(0, n_chunks // 2, first_pair, jnp.zeros((1, pg), F32))
                dwb_ref[...] = dw_ref[...].reshape(N_DEV, pg // N_DEV, pg).astype(BF16)

                def second(ci, _):
                    r0 = pl.multiple_of(ci * t, t)
                    ext = pad_ref[pl.ds(r0, t + 2 * h), :]
                    back = _win_sum(ext, w, 1)[h:h + t, :]
                    du_ref[pl.ds(r0, t), :] = (back - dd_ref[pl.ds(r0, t), :]).astype(BF16)
                    return 0

                lax.fori_loop(0, n_chunks, second, 0, unroll=2)

    return pl.pallas_call(
        body, name="pool_bwd", grid=(len(POOL_WINDOWS),),
        in_specs=[_bs((seq, pg), lambda g: (0, g)), _bs((seq, pg), lambda g: (0, g)),
                  _bs((None, pg, pg), lambda g: (g, 0, 0)), _bs((1, pg), lambda g: (0, g)),
                  _bs((8, 128), lambda g: (0, 0))],
        out_specs=[_bs((seq, pg), lambda g: (0, g)), _bs((N_DEV, None, pg // N_DEV, pg), lambda g: (0, g, 0, 0)),
                   _bs((1, pg), lambda g: (0, g))],
        out_shape=[jax.ShapeDtypeStruct((seq, 3 * dp), BF16),
                   jax.ShapeDtypeStruct((N_DEV, len(POOL_WINDOWS), pg // N_DEV, pg), BF16),
                   jax.ShapeDtypeStruct((1, dp), F32)],
        scratch_shapes=[pltpu.VMEM((seq + 2 * h, pg), F32), pltpu.VMEM((seq, pg), F32), pltpu.VMEM((pg, pg), F32)],
        compiler_params=_params(1),
    )(d, dy, w_pool, pool_scale, token)


def _tile_scan(n_tiles, lanes, loads, stores):
    row = lax.broadcasted_iota(jnp.int32, (8, lanes), 0)
    group = 8

    def local_scan(n, k):
        aa, bb = loads[n](k)
        for sh in (1, 2, 4):
            if n == 0:
                ok = row >= sh
                ap = jnp.where(ok, pltpu.roll(aa, sh, 0), 1.0)
                bp = jnp.where(ok, pltpu.roll(bb, sh, 0), 0.0)
            else:
                ok = row < 8 - sh
                ap = jnp.where(ok, pltpu.roll(aa, 8 - sh, 0), 1.0)
                bp = jnp.where(ok, pltpu.roll(bb, 8 - sh, 0), 0.0)
            bb = aa * bp + bb
            aa = aa * ap
        return aa, bb

    def step(s, carry):
        carry = list(carry)
        for n in range(2):
            tiles = [s * group + u if n == 0 else n_tiles - 1 - (s * group + u) for u in range(group)]
            local = [local_scan(n, k) for k in tiles]
            for k, (aa, bb) in zip(tiles, local):
                hh = bb + aa * carry[n]
                stores[n](k, hh)
                carry[n] = jnp.broadcast_to(hh[7:8, :] if n == 0 else hh[0:1, :], (8, lanes))
        return tuple(carry)

    zeros = jnp.zeros((8, lanes), F32)
    lax.fori_loop(0, n_tiles // group, step, (zeros, zeros))


def _gate_preacts(xc, wcat_ref):
    xcb = xc.astype(BF16)
    return xcb, jnp.dot(xcb, wcat_ref[...], preferred_element_type=F32)


def _gates(pre, n, pk_ref, sp):
    lh = pre.shape[1] // 4
    r = _sigmoid(pre[:, (2 * n) * lh:(2 * n + 1) * lh] + pk_ref[pl.ds(4 + n, 1), :])
    i = _sigmoid(pre[:, (2 * n + 1) * lh:(2 * n + 2) * lh] + pk_ref[pl.ds(6 + n, 1), :])
    log_a = (-RG_C * r) * sp[n]
    a = jnp.exp(log_a)
    x = 2.0 * log_a
    one_minus_a2 = jnp.where(x > -0.01, -(x * (1.0 + x * (0.5 + x * (1.0 / 6.0)))), 1.0 - a * a)
    m = jnp.sqrt(one_minus_a2)
    return r, i, a, m


def _conv_chunk(upad_ref, pk_ref, cb, r0, t):
    ext = upad_ref[pl.ds(r0, t + 2 * CONV_HALO), :]
    acc = pk_ref[pl.ds(1, 1), :] * ext
    for k in (0, 2, 3):
        acc = acc + pk_ref[pl.ds(k, 1), :] * _shift(ext, k - 1)
    return acc[CONV_HALO:CONV_HALO + t, :] + cb, ext


def _lru_fwd(p3, y_in, pack, conv_b, wcat, token, seq, d_model):
    dl = d_model // 2
    lh = dl // N_HEADS
    t = min(SEQ_CHUNK, seq)
    n_chunks = seq // t
    hal = CONV_HALO
    first_rec_block = (d_model - dl) // lh

    def body(ur_ref, ug_ref, pk_ref, cb_ref, wcat_ref, yin_ref, tok_ref, y_ref, h0_ref, h1_ref,
             upad, a_scr, b_scr):
        del yin_ref, tok_ref
        zeros = jnp.zeros((hal, lh), F32)
        upad[0:hal, :] = zeros
        upad[hal + seq:hal + seq + hal, :] = zeros
        for ref in (h0_ref, h1_ref):
            ref[0:hal, :] = zeros
            ref[hal + seq:hal + seq + hal, :] = zeros

        def fill(ci, _):
            r0 = pl.multiple_of(ci * t, t)
            upad[pl.ds(hal + r0, t), :] = ur_ref[pl.ds(r0, t), :]
            return 0

        lax.fori_loop(0, n_chunks, fill, 0)
        cb = cb_ref[...]
        sp = [_softplus(-pk_ref[pl.ds(8 + n, 1), :]) for n in range(2)]

        def chunk(ci, _):
            r0 = pl.multiple_of(ci * t, t)
            xc, _ext = _conv_chunk(upad, pk_ref, cb, r0, t)
            _, pre = _gate_preacts(xc, wcat_ref)
            for n in range(2):
                _, i, a, m = _gates(pre, n, pk_ref, sp)
                a_scr[n, pl.ds(r0, t), :] = a
                b_scr[n, pl.ds(r0, t), :] = (m * i) * xc
            return 0

        lax.fori_loop(0, n_chunks, chunk, 0, unroll=2)

        def load(n):
            def get(k):
                at = pl.ds(pl.multiple_of(k * 8, 8), 8)
                return a_scr[n, at, :], b_scr[n, at, :]
            return get

        def store(ref):
            def put(k, v):
                ref[pl.ds(pl.multiple_of(hal + k * 8, 8), 8), :] = v
            return put

        _tile_scan(seq // 8, lh, [load(0), load(1)], [store(h0_ref), store(h1_ref)])

        def out(ci, _):
            r0 = pl.multiple_of(ci * t, t)
            hsum = h0_ref[pl.ds(hal + r0, t), :] + h1_ref[pl.ds(hal + r0, t), :]
            gl, _dg = _gelu_and_grad(ug_ref[pl.ds(r0, t), :])
            y_ref[pl.ds(r0, t), :] = (hsum * gl).astype(BF16)
            return 0

        lax.fori_loop(0, n_chunks, out, 0)

    return pl.pallas_call(
        body, name="lru_fwd", grid=(N_HEADS,),
        in_specs=[_bs((None, seq, lh), lambda h: (1, 0, h)), _bs((None, seq, lh), lambda h: (2, 0, h)),
                  _bs((None, SMALL_ROWS, lh), lambda h: (h, 0, 0)), _bs((1, lh), lambda h: (0, h)),
                  _bs((None, lh, 4 * lh), lambda h: (h, 0, 0)),
                  ANY, _bs((8, 128), lambda h: (0, 0))],
        out_specs=[_bs((seq, lh), lambda h: (0, first_rec_block + h)),
                   _bs((seq + 2 * hal, lh), lambda h: (0, h)), _bs((seq + 2 * hal, lh), lambda h: (0, h))],
        out_shape=[jax.ShapeDtypeStruct((seq, d_model), BF16), jax.ShapeDtypeStruct((seq + 2 * hal, dl), F32),
                   jax.ShapeDtypeStruct((seq + 2 * hal, dl), F32)],
        scratch_shapes=[pltpu.VMEM((seq + 2 * hal, lh), F32), pltpu.VMEM((2, seq, lh), F32),
                        pltpu.VMEM((2, seq, lh), F32)],
        input_output_aliases={5: 0},
        compiler_params=_params(1),
    )(p3, p3, pack, conv_b, wcat, y_in, token)


def _lru_bwd(p3, dy, h0p, h1p, dproj_in, pack, conv_b, wcat, token, seq, d_model):
    dl = d_model // 2
    lh = dl // N_HEADS
    t = min(SEQ_CHUNK, seq)
    n_chunks = seq // t
    hal = CONV_HALO
    first_rec_block = (d_model - dl) // lh
    tn_dims = (((0,), (0,)), ((), ()))
    nt_dims = (((1,), (1,)), ((), ()))

    def body(ur_ref, ug_ref, dy_ref, h0_ref, h1_ref, pk_ref, cb_ref, wcat_ref, tok_ref, din_ref,
             dproj_ref, dpk_ref, dcb_ref, dwcat_ref,
             upad, a_scr, dh_scr, g_scr, dxc_pad, dpr_ref, out_sems, gate_scr):
        del din_ref, tok_ref
        zeros = jnp.zeros((hal, lh), F32)
        for ref in (upad, dxc_pad):
            ref[0:hal, :] = zeros
            ref[hal + seq:hal + seq + hal, :] = zeros
        for n in range(2):
            a_scr[n, 0:hal, :] = zeros
            a_scr[n, hal + seq:hal + seq + hal, :] = zeros

        def fill(ci, _):
            r0 = pl.multiple_of(ci * t, t)
            upad[pl.ds(hal + r0, t), :] = ur_ref[pl.ds(r0, t), :]
            return 0

        lax.fori_loop(0, n_chunks, fill, 0)
        cb = cb_ref[...]
        lam = [pk_ref[pl.ds(8 + n, 1), :] for n in range(2)]
        sp = [_softplus(-lam[n]) for n in range(2)]

        def chunk1(ci, _):
            r0 = pl.multiple_of(ci * t, t)
            xc, _ext = _conv_chunk(upad, pk_ref, cb, r0, t)
            _, pre = _gate_preacts(xc, wcat_ref)
            for n in range(2):
                r, i, a, m = _gates(pre, n, pk_ref, sp)
                a_scr[n, pl.ds(hal + r0, t), :] = a
                for q, v in enumerate((r, i, m)):
                    gate_scr[3 * n + q, pl.ds(r0, t), :] = v
            hsum = h0_ref[pl.ds(hal + r0, t), :] + h1_ref[pl.ds(hal + r0, t), :]
            gl, dgl = _gelu_and_grad(ug_ref[pl.ds(r0, t), :])
            dyv = dy_ref[pl.ds(r0, t), :]
            dh_scr[pl.ds(r0, t), :] = dyv * gl
            dpr_ref[1, pl.ds(r0, t), :] = ((dyv * hsum) * dgl).astype(BF16)
            return 0

        lax.fori_loop(0, n_chunks, chunk1, 0, unroll=2)

        def load(n):
            def get(k):
                r0 = pl.multiple_of(k * 8, 8)
                if n == 0:
                    coef = _shift(a_scr[0, pl.ds(pl.multiple_of(hal + r0, 8), 16), :], 1)[0:8, :]
                else:
                    coef = _shift(a_scr[1, pl.ds(pl.multiple_of(hal + r0 - 8, 8), 16), :], -1)[8:16, :]
                return coef, dh_scr[pl.ds(r0, 8), :]
            return get

        def store(n):
            def put(k, v):
                g_scr[n, pl.ds(pl.multiple_of(k * 8, 8), 8), :] = v
            return put

        _tile_scan(seq // 8, lh, [load(1), load(0)], [store(1), store(0)])

        dwcat_ref[...] = jnp.zeros((lh, 4 * lh), F32)

        def chunk3(ci, carry):
            dba, dbi, dlam, dcb = carry
            r0 = pl.multiple_of(ci * t, t)
            xc, _ext = _conv_chunk(upad, pk_ref, cb, r0, t)
            xcb = xc.astype(BF16)
            dxc = jnp.zeros((t, lh), F32)
            dba, dbi, dlam = list(dba), list(dbi), list(dlam)
            dpre = []
            for n in range(2):
                r, i, m = (gate_scr[3 * n + q, pl.ds(r0, t), :] for q in range(3))
                a = a_scr[n, pl.ds(hal + r0, t), :]
                hext = (h0_ref if n == 0 else h1_ref)[pl.ds(r0, t + 2 * hal), :]
                hprev = _shift(hext, -1 if n == 0 else 1)[hal:hal + t, :]
                gb = g_scr[n, pl.ds(r0, t), :]
                da = gb * hprev
                dm = gb * i * xc
                di = gb * m * xc
                dxc = dxc + gb * (m * i)
                dlog_a = da * a - dm * (a * a) / m
                dr = dlog_a * (-RG_C * sp[n])
                dlam[n] = dlam[n] + jnp.sum(dlog_a * r, axis=0, keepdims=True)
                dpr = dr * r * (1.0 - r)
                dpi = di * i * (1.0 - i)
                dba[n] = dba[n] + jnp.sum(dpr, axis=0, keepdims=True)
                dbi[n] = dbi[n] + jnp.sum(dpi, axis=0, keepdims=True)
                dpre += [dpr.astype(BF16), dpi.astype(BF16)]
            dpre = jnp.concatenate(dpre, axis=1)
            dwcat_ref[...] += lax.dot_general(xcb, dpre, tn_dims, preferred_element_type=F32)
            dxc = dxc + lax.dot_general(dpre, wcat_ref[...], nt_dims, preferred_element_type=F32)
            dxc_pad[pl.ds(hal + r0, t), :] = dxc
            dcb = dcb + jnp.sum(dxc, axis=0, keepdims=True)
            return tuple(dba), tuple(dbi), tuple(dlam), dcb

        zr = jnp.zeros((1, lh), F32)
        def chunk3_pair(cj, carry):
            return chunk3(2 * cj + 1, chunk3(2 * cj, carry))

        dba, dbi, dlam, dcb = lax.fori_loop(0, n_chunks // 2, chunk3_pair, ((zr, zr), (zr, zr), (zr, zr), zr))
        dcb_ref[...] = dcb
        for n in range(2):
            dpk_ref[pl.ds(4 + n, 1), :] = dba[n]
            dpk_ref[pl.ds(6 + n, 1), :] = dbi[n]
            dpk_ref[pl.ds(8 + n, 1), :] = dlam[n] * (RG_C * jax.nn.sigmoid(-lam[n]))
        dpk_ref[pl.ds(10, SMALL_ROWS - 10), :] = jnp.zeros((SMALL_ROWS - 10, lh), F32)

        def chunk4(ci, dtap):
            r0 = pl.multiple_of(ci * t, t)
            gext = dxc_pad[pl.ds(r0, t + 2 * hal), :]
            uext = upad[pl.ds(r0, t + 2 * hal), :]
            gmid = gext[hal:hal + t, :]
            du = pk_ref[pl.ds(1, 1), :] * gext
            for k in (0, 2, 3):
                du = du + pk_ref[pl.ds(k, 1), :] * _shift(gext, 1 - k)
            dpr_ref[0, pl.ds(r0, t), :] = du[hal:hal + t, :].astype(BF16)
            out = []
            for k in range(4):
                usl = _shift(uext, k - 1)[hal:hal + t, :]
                out.append(dtap[k] + jnp.sum(gmid * usl, axis=0, keepdims=True))
            return tuple(out)

        dtap = lax.fori_loop(0, n_chunks, chunk4, (zr, zr, zr, zr))
        for k in range(4):
            dpk_ref[pl.ds(k, 1), :] = dtap[k]

        head = pl.program_id(0)
        outs = [pltpu.make_async_copy(
            dpr_ref.at[b], dproj_ref.at[:, pl.ds(pl.multiple_of((1 + b) * dl + head * lh, lh), lh)], out_sems.at[b])
            for b in range(2)]
        for cp in outs:
            cp.start()
        for cp in outs:
            cp.wait()

    return pl.pallas_call(
        body, name="lru_bwd", grid=(N_HEADS,),
        in_specs=[_bs((None, seq, lh), lambda h: (1, 0, h)), _bs((None, seq, lh), lambda h: (2, 0, h)),
                  _bs((seq, lh), lambda h: (0, first_rec_block + h)),
                  _bs((seq + 2 * hal, lh), lambda h: (0, h)), _bs((seq + 2 * hal, lh), lambda h: (0, h)),
                  _bs((None, SMALL_ROWS, lh), lambda h: (h, 0, 0)), _bs((1, lh), lambda h: (0, h)),
                  _bs((None, lh, 4 * lh), lambda h: (h, 0, 0)),
                  _bs((8, 128), lambda h: (0, 0)), ANY],
        out_specs=[ANY, _bs((None, SMALL_ROWS, lh), lambda h: (h, 0, 0)),
                   _bs((1, lh), lambda h: (0, h)), _bs((None, lh, 4 * lh), lambda h: (h, 0, 0))],
        out_shape=[jax.ShapeDtypeStruct((seq, 3 * dl), BF16), jax.ShapeDtypeStruct((N_HEADS, SMALL_ROWS, lh), F32),
                   jax.ShapeDtypeStruct((1, dl), F32), jax.ShapeDtypeStruct((N_HEADS, lh, 4 * lh), F32)],
        scratch_shapes=[pltpu.VMEM((seq + 2 * hal, lh), F32), pltpu.VMEM((2, seq + 2 * hal, lh), F32),
                        pltpu.VMEM((seq, lh), F32), pltpu.VMEM((2, seq, lh), F32),
                        pltpu.VMEM((seq + 2 * hal, lh), F32), pltpu.VMEM((2, seq, lh), BF16),
                        pltpu.SemaphoreType.DMA((2,)), pltpu.VMEM((6, seq, lh), F32)],
        input_output_aliases={9: 0},
        compiler_params=_params(1),
    )(p3, p3, dy, h0p, h1p, pack, conv_b, wcat, token, dproj_in)


class _tiles:
    def __init__(self, seq, d_model, d_ff):
        self.rows = min(1024, seq)
        self.ln_rows = min(256, seq)
        self.ff_cols = min(2048, d_ff)
        self.ff_split = 4
        self.ff_k = min(2048, d_ff)
        self.grad_rows = 512


def _ln_loss_bwd(ffn, x1, tgt, g, b, tr):
    seq, d = ffn.shape

    def body(f_ref, x_ref, t_ref, g_ref, b_ref, dz_ref, dzb_ref, dg_ref, db_ref, loss_ref):
        i = pl.program_id(0)
        gv = g_ref[...]
        z = ALPHA * x_ref[...] + f_ref[...]
        y, xhat, rstd = _ln_fwd(z, gv, b_ref[...])
        err = y - t_ref[...]
        part = 0.5 * jnp.sum(jnp.mean(err * err, axis=-1, keepdims=True), axis=0, keepdims=True)
        dz, dg, db = _ln_bwd(err * (1.0 / d), xhat, rstd, gv)
        dz_ref[...] = dz
        dzb_ref[...] = dz.astype(BF16)
        _acc_rows(dg_ref, i == 0, dg)
        _acc_rows(db_ref, i == 0, db)
        _acc_rows(loss_ref, i == 0, jnp.broadcast_to(part, (8, 128)))

    row = _bs((tr, d), lambda i: (i, 0))
    vec = _bs((1, d), lambda i: (0, 0))
    return pl.pallas_call(
        body, name="ln_ffn_loss", grid=(seq // tr,), in_specs=[row, row, row, vec, vec],
        out_specs=[row, row, vec, vec, _bs((8, 128), lambda i: (0, 0))],
        out_shape=[jax.ShapeDtypeStruct((seq, d), F32), jax.ShapeDtypeStruct((seq, d), BF16),
                   jax.ShapeDtypeStruct((1, d), F32), jax.ShapeDtypeStruct((1, d), F32),
                   jax.ShapeDtypeStruct((8, 128), F32)],
        compiler_params=_params(1),
    )(ffn, x1, tgt, g, b)


def _ln_bwd_side(dx_branch, dres, z, g, b, n_steps):
    seq, d = z.shape
    tr = seq // n_steps

    def fn(step, ins, outs):
        a_ref, r_ref, z_ref, g_ref, b_ref = ins
        dz_ref, dzb_ref, dg_ref, db_ref = outs
        gv = g_ref[...]
        _, xhat, rstd = _ln_fwd(z_ref[...], gv, b_ref[...])
        dz, dg, db = _ln_bwd(ALPHA * r_ref[...] + a_ref[...], xhat, rstd, gv)
        dz_ref[...] = dz
        dzb_ref[...] = dz.astype(BF16)
        _acc_rows(dg_ref, step == 0, dg)
        _acc_rows(db_ref, step == 0, db)

    row = ((tr, d), lambda s: (s, 0))
    vec = ((1, d), lambda s: (0, 0))
    shapes = [jax.ShapeDtypeStruct((seq, d), F32), jax.ShapeDtypeStruct((seq, d), BF16),
              jax.ShapeDtypeStruct((1, d), F32), jax.ShapeDtypeStruct((1, d), F32)]
    return [(dx_branch, *row), (dres, *row), (z, *row), (g, *vec), (b, *vec)], shapes, [row, row, vec, vec], fn


def _to_bf16(name, a, token):
    rows, cols = a.shape
    tr = min(512, rows)

    def body(a_ref, tok_ref, o_ref):
        del tok_ref
        o_ref[...] = a_ref[...].astype(BF16)

    return pl.pallas_call(
        body, name=name, grid=(rows // tr,),
        in_specs=[_bs((tr, cols), lambda i: (i, 0)), _bs((8, 128), lambda i: (0, 0))],
        out_specs=_bs((tr, cols), lambda i: (i, 0)), out_shape=jax.ShapeDtypeStruct((rows, cols), BF16),
        compiler_params=_params(1),
    )(a, token)


def _sum_blocks(name, parts):
    def body(p_ref, o_ref):
        acc = p_ref[0]
        for s in range(1, parts.shape[0]):
            acc = acc + p_ref[s]
        o_ref[...] = acc

    return pl.pallas_call(body, name=name, out_shape=jax.ShapeDtypeStruct(parts.shape[1:], F32))(parts)


def _adamw_values(w, g, m, v):
    m = ADAM_B1 * m + (1.0 - ADAM_B1) * g
    v = ADAM_B2 * v + (1.0 - ADAM_B2) * (g * g)
    m_hat = m / (1.0 - ADAM_B1 ** ADAM_STEP)
    v_hat = v / (1.0 - ADAM_B2 ** ADAM_STEP)
    delta = -ADAM_LR * (m_hat / (jnp.sqrt(v_hat) + ADAM_EPS) + ADAM_WD * w)
    return delta, m, v


def _adamw_side(own, parts, w, m, v, n_steps):
    rows, cols = w.shape
    tr = rows // n_steps

    def fn(step, ins, outs):
        o_ref, p_ref, w_ref, m_ref, v_ref = ins
        g = o_ref[...]
        for s in range(parts.shape[0]):
            g = g + p_ref[s].astype(F32)
        delta, mn, vn = _adamw_values(w_ref[...], g, m_ref[...], v_ref[...])
        for ref, val in zip(outs, (g, delta, mn, vn)):
            ref[...] = val

    row = ((tr, cols), lambda s: (s, 0))
    stack = ((parts.shape[0], tr, cols), lambda s: (0, s, 0))
    shapes = [jax.ShapeDtypeStruct((rows, cols), F32)] * 4
    return [(own, *row), (parts, *stack), (w, *row), (m, *row), (v, *row)], shapes, [row] * 4, fn


def _sum_adamw(name, own, parts, w, m, v):
    rows, cols = w.shape
    n_parts = parts.shape[0]
    tr = rows
    min_rows = 8 if parts.dtype == F32 else 16
    while tr * cols * 4 > 1024 * 1024 and tr % (2 * min_rows) == 0:
        tr //= 2

    def body(*refs):
        if own is None:
            p_ref, w_ref, m_ref, v_ref, g_ref, d_ref, mo_ref, vo_ref = refs
            g = p_ref[0].astype(F32)
            rest = range(1, n_parts)
        else:
            o_ref, p_ref, w_ref, m_ref, v_ref, g_ref, d_ref, mo_ref, vo_ref = refs
            g = o_ref[...]
            rest = range(n_parts)
        for s in rest:
            g = g + p_ref[s].astype(F32)
        delta, mn, vn = _adamw_values(w_ref[...], g, m_ref[...], v_ref[...])
        g_ref[...] = g
        d_ref[...] = delta
        mo_ref[...] = mn
        vo_ref[...] = vn

    spec = _bs((tr, cols), lambda i: (i, 0))
    lead = [] if own is None else [own]
    return pl.pallas_call(
        body, name=name, grid=(rows // tr,),
        in_specs=[spec] * len(lead) + [_bs((n_parts, tr, cols), lambda i: (0, i, 0)), spec, spec, spec],
        out_specs=[spec] * 4, out_shape=[jax.ShapeDtypeStruct((rows, cols), F32)] * 4,
        compiler_params=_params(1),
    )(*lead, parts, w, m, v)


def _rows128(a):
    return a.reshape(-1, 128)


def kernel(x, ln_mix_g, ln_mix_b, w_in, w_pool, pool_scale, conv_w, conv_b, w_rg_a, b_rg_a, w_rg_i, b_rg_i, rg_lambda, w_out, ln_ffn_g, ln_ffn_b, w_mlp_in, w_mlp_out, loss_target, m_ln_mix_g, m_ln_mix_b, m_w_in, m_w_pool, m_pool_scale, m_conv_w, m_conv_b, m_w_rg_a, m_b_rg_a, m_w_rg_i, m_b_rg_i, m_rg_lambda, m_w_out, m_ln_ffn_g, m_ln_ffn_b, m_w_mlp_in, m_w_mlp_out, v_ln_mix_g, v_ln_mix_b, v_w_in, v_w_pool, v_pool_scale, v_conv_w, v_conv_b, v_w_rg_a, v_b_rg_a, v_w_rg_i, v_b_rg_i, v_rg_lambda, v_w_out, v_ln_ffn_g, v_ln_ffn_b, v_w_mlp_in, v_w_mlp_out):
    seq, d_model = x.shape[1], x.shape[2]
    dh = d_model // 2
    lh = dh // N_HEADS
    pg = dh // len(POOL_WINDOWS)
    d_ff = w_mlp_in.shape[2] * N_DEV
    assert lh == 128 and conv_w.shape[3] == lh and w_pool.shape[2] * N_DEV == pg

    xs = x[0]
    tgt = loss_target[0]

    def small_pack(cw, ba, bi, lam):
        return jnp.concatenate([cw.reshape(4, lh), ba.reshape(2, lh), bi.reshape(2, lh), lam.reshape(2, lh),
                                jnp.zeros((SMALL_ROWS - 10, lh), F32)], axis=0)

    pack_mine = small_pack(conv_w, b_rg_a, b_rg_i, rg_lambda)
    pack_bits = lax.bitcast_convert_type(pack_mine, BF16).reshape(1, SMALL_ROWS, 2 * lh)
    win_gather = _SplitGather("gather_w_in", [(w_in[0], 1), (w_pool[0], 1), (pack_bits, 0)], BF16, after=pack_mine)
    wout_gather = _SplitGather("gather_w_out", [(w_out[0], 0)], BF16, after=win_gather.token)
    w1_gather = _SplitGather("gather_w_mlp_in", [(w_mlp_in[0], 1)], BF16, after=wout_gather.token)
    w2_gather = _SplitGather("gather_w_mlp_out", [(w_mlp_out[0], 0)], BF16, after=w1_gather.token)
    xb = _to_bf16("x_bf16", x[0], w2_gather.token)
    win_full, wpool_full, pack_bits_full = win_gather.wait(after=win_gather.relay(after=xb))
    pack_full = lax.bitcast_convert_type(pack_bits_full.reshape(N_DEV, SMALL_ROWS, lh, 2), F32)
    wcat = jnp.concatenate([w_rg_a[0, 0], w_rg_i[0, 0], w_rg_a[0, 1], w_rg_i[0, 1]], axis=-1).astype(BF16)
    vec = lambda i, j, k: (0, 0)
    row_full = lambda i, j, k: (i, 0)

    def after(token):
        return (token, _sp((8, 128), vec))

    def sds(shape, dtype):
        return jax.ShapeDtypeStruct(shape, dtype)

    def plain_epi(acc, i, ex, out):
        out[0][...] = acc

    def bf16_epi(acc, i, ex, out):
        out[0][...] = acc.astype(BF16)

    t = _tiles(seq, d_model, d_ff)

    (p3,) = _matmul(
        "proj", xb, win_full, _sp((t.rows, d_model), lambda i, j, k: (i, 0)), _sp((d_model, dh), lambda i, j, k: (0, j)),
        grid=(seq // t.rows, 3, 1),
        out_shape=[sds((3, seq, dh), F32)], out_specs=[_sp((None, t.rows, dh), lambda i, j, k: (j, i, 0))],
        epilogue=plain_epi)

    d_pool, y_half = _pool_fwd(p3, wpool_full, pool_scale, seq, d_model)
    y, h0p, h1p = _lru_fwd(p3, y_half, pack_full, conv_b, wcat, wout_gather.relay(after=y_half), seq, d_model)
    (wout_full,) = wout_gather.wait(after=y)
    relay_token = w1_gather.relay(after=wout_full)

    mix_rows = 2 * t.ln_rows

    def mix_epi(acc, i, ex, out):
        x_ref, g_ref, b_ref = ex[:3]
        for part in range(2):
            rows = pl.ds(part * t.ln_rows, t.ln_rows)
            z = ALPHA * x_ref[rows, :] + acc[part * t.ln_rows:(part + 1) * t.ln_rows, :]
            x1, _, _ = _ln_fwd(z, g_ref[...], b_ref[...])
            out[0][rows, :] = z
            out[1][rows, :] = x1
            out[2][rows, :] = x1.astype(BF16)

    z1, x1, x1b = _matmul(
        "mix_out", y, wout_full, _sp((mix_rows, d_model), row_full), _sp((d_model, d_model), vec, single=True),
        grid=(seq // mix_rows, 1, 1),
        extras=[(xs, _sp((mix_rows, d_model), row_full)), (ln_mix_g, _sp((1, d_model), vec)),
                (ln_mix_b, _sp((1, d_model), vec)), after(relay_token)],
        out_shape=[sds((seq, d_model), F32), sds((seq, d_model), F32), sds((seq, d_model), BF16)],
        out_specs=[_sp((mix_rows, d_model), row_full)] * 3, epilogue=mix_epi)
    (w1_full,) = w1_gather.wait(after=x1b)

    def mlp_in_epi(acc, i, ex, out, cols):
        h = jnp.maximum(acc, 0.0)
        out[0][:, cols] = (h * h).astype(BF16)
        out[1][:, cols] = (2.0 * h).astype(BF16)

    hmid, dact = _matmul(
        "mlp_in", x1b, w1_full, _sp((t.rows, d_model), lambda i, j, k: (i, 0)),
        _sp((d_model, t.ff_cols), lambda i, j, k: (0, j)),
        grid=(seq // t.rows, d_ff // t.ff_cols, 1), j_outer=True,
        out_shape=[sds((seq, d_ff), BF16)] * 2, out_specs=[_sp((t.rows, t.ff_cols), lambda i, j, k: (i, j))] * 2,
        epilogue=mlp_in_epi, n_split=t.ff_split)
    (w2_full,) = w2_gather.wait(after=w2_gather.relay(after=hmid))

    (ffn,) = _matmul(
        "mlp_out", hmid, w2_full, _sp((t.rows, t.ff_k), lambda i, j, k: (i, k)),
        _sp((t.ff_k, d_model), lambda i, j, k: (k, 0)),
        grid=(seq // t.rows, 1, d_ff // t.ff_k),
        out_shape=[sds((seq, d_model), F32)], out_specs=[_sp((t.rows, d_model), row_full)])
    dz2, dz2b, g_ffn_g, g_ffn_b, loss_part = _ln_loss_bwd(ffn, x1, tgt, ln_ffn_g, ln_ffn_b, t.ln_rows)

    (g_w2,) = _matmul(
        "grad_w_mlp_out", hmid, dz2b, _sp((seq, t.grad_rows), lambda i, j, k: (0, i)),
        _sp((seq, d_model), vec, single=True),
        grid=(d_ff // t.grad_rows, 1, 1), ta=True,
        out_shape=[sds((d_ff, d_model), BF16)], out_specs=[_sp((t.grad_rows, d_model), row_full)],
        epilogue=bf16_epi)
    scatter_w2 = _SplitReduceScatter("scatter_w_mlp_out", [g_w2.reshape(N_DEV, d_ff // N_DEV, d_model)])

    def dpre_epi(acc, i, ex, out, cols):
        out[0][:, cols] = (acc * ex[0][:, cols].astype(F32)).astype(BF16)

    (dpre,) = _matmul(
        "mlp_dpre", dz2b, w2_full, _sp((t.rows, d_model), lambda i, j, k: (i, 0)),
        _sp((t.ff_cols, d_model), lambda i, j, k: (j, 0)),
        grid=(seq // t.rows, d_ff // t.ff_cols, 1), j_outer=True, tb=True,
        extras=[(dact, _sp((t.rows, t.ff_cols), lambda i, j, k: (i, j))), after(scatter_w2.token)],
        out_shape=[sds((seq, d_ff), BF16)], out_specs=[_sp((t.rows, t.ff_cols), lambda i, j, k: (i, j))],
        epilogue=dpre_epi, n_split=t.ff_split)
    token_w2 = scatter_w2.combine_and_send(after=dpre)

    (dx1_mlp,) = _matmul(
        "mlp_dx", dpre, w1_full, _sp((t.rows, t.ff_k), lambda i, j, k: (i, k)),
        _sp((d_model, t.ff_k), lambda i, j, k: (0, k)),
        grid=(seq // t.rows, 1, d_ff // t.ff_k), tb=True, extras=[after(token_w2)],
        out_shape=[sds((seq, d_model), F32)], out_specs=[_sp((t.rows, d_model), row_full)])
    def block_epi(acc, i, ex, out):
        out[0][0] = acc.astype(BF16)

    fs = d_ff // N_DEV
    g_w1, dz1, dz1b, g_mix_g, g_mix_b = _matmul(
        "grad_w_mlp_in", x1b, dpre, _sp((seq, t.grad_rows), lambda i, j, k: (0, i)),
        _sp((seq, fs), lambda i, j, k: (0, j)),
        grid=(d_model // t.grad_rows, N_DEV, 1), j_outer=True, ta=True,
        out_shape=[sds((N_DEV, d_model, fs), BF16)],
        out_specs=[_sp((1, t.grad_rows, fs), lambda i, j, k: (j, i, 0))], epilogue=block_epi,
        side=_ln_bwd_side(dx1_mlp, dz2, z1, ln_mix_g, ln_mix_b, d_model // t.grad_rows * N_DEV))

    (dy,) = _matmul(
        "mix_dy", dz1b, wout_full, _sp((t.rows, d_model), lambda i, j, k: (i, 0)),
        _sp((dh, d_model), lambda i, j, k: (j, 0)),
        grid=(seq // t.rows, 2, 1), j_outer=True, tb=True,
        out_shape=[sds((seq, d_model), F32)], out_specs=[_sp((t.rows, dh), lambda i, j, k: (i, j))],
        epilogue=plain_epi)
    (g_wout,) = _matmul(
        "grad_w_out", y, dz1b, _sp((seq, t.grad_rows), lambda i, j, k: (0, i)), _sp((seq, d_model), vec, single=True),
        grid=(d_model // t.grad_rows, 1, 1), ta=True,
        out_shape=[sds((d_model, d_model), BF16)], out_specs=[_sp((t.grad_rows, d_model), row_full)],
        epilogue=bf16_epi)
    scatter_w1 = _SplitReduceScatter("scatter_w_mlp_in", [g_w1, g_wout.reshape(N_DEV, d_model // N_DEV, d_model)])

    dproj_pool, g_wpool, g_pscale = _pool_bwd(d_pool, dy, wpool_full, pool_scale, scatter_w1.token, seq, d_model)
    token_w1 = scatter_w1.combine_and_send(after=dproj_pool)
    dproj, g_pack, g_convb, g_wcat = _lru_bwd(p3, dy, h0p, h1p, dproj_pool, pack_full, conv_b, wcat,
                                              token_w1, seq, d_model)
    g_wa = jnp.stack([g_wcat[:, :, 0:lh], g_wcat[:, :, 2 * lh:3 * lh]])
    g_wi = jnp.stack([g_wcat[:, :, lh:2 * lh], g_wcat[:, :, 3 * lh:4 * lh]])

    rep_parts = [_rows128(g_wa), _rows128(g_wi), _rows128(g_mix_g), _rows128(g_mix_b), _rows128(g_ffn_g),
                 _rows128(g_ffn_b), _rows128(g_pscale), _rows128(g_convb)]
    rep_rows = [p.shape[0] for p in rep_parts]
    n_rep = sum(rep_rows)
    small = jnp.concatenate(rep_parts + [_rows128(g_pack), loss_part], axis=0)
    small_gather = _SplitGather("gather_small_grads", [(small[None], 0)], F32, after=small)

    ws = 3 * dh // N_DEV

    def pair_epi(acc, i, ex, out):
        out[0][0] = acc[:, :ws].astype(BF16)
        out[0][1] = acc[:, ws:].astype(BF16)

    def adam_big(name, own_landed, w, m, v):
        own, landed = own_landed
        shp = w.shape
        two = lambda a: a.reshape(-1, shp[-1])
        res = _sum_adamw(name, own, landed, two(w), two(m), two(v))
        return [r.reshape(shp) for r in res]

    (r_w2,) = scatter_w2.wait(after=small_gather.token)
    n_steps = d_model // t.grad_rows * (N_DEV // 2)
    g_win, *o_w2 = _matmul(
        "grad_w_in", xb, dproj, _sp((seq, t.grad_rows), lambda i, j, k: (0, i)),
        _sp((seq, 2 * ws), lambda i, j, k: (0, j)),
        grid=(d_model // t.grad_rows, N_DEV // 2, 1), ta=True,
        out_shape=[sds((N_DEV, d_model, ws), BF16)],
        out_specs=[_sp((2, t.grad_rows, ws), lambda i, j, k: (j, i, 0))], epilogue=pair_epi,
        side=_adamw_side(r_w2[0], r_w2[1], w_mlp_out[0], m_w_mlp_out[0], v_w_mlp_out[0], n_steps))
    o_w2 = [r.reshape(w_mlp_out.shape) for r in o_w2]
    scatter_mix = _SplitReduceScatter(
        "scatter_mixer", [g_win, g_wpool.reshape(N_DEV, pg // N_DEV * len(POOL_WINDOWS), pg)])

    r_w1, r_wout = scatter_w1.wait(after=scatter_mix.token)
    o_w1 = adam_big("adam_w_mlp_in", r_w1, w_mlp_in, m_w_mlp_in, v_w_mlp_in)
    token_mix = scatter_mix.combine_and_send(after=o_w1[0])

    def dx_epi(acc, i, ex, out):
        out[0][...] = ALPHA * ex[0][...] + acc

    dx_rows = t.ln_rows * 2
    dx, *o_wout = _matmul(
        "grad_x", dproj, win_full, _sp((dx_rows, 3 * dh), lambda i, j, k: (i, 0)),
        _sp((d_model, 3 * dh), vec, single=True),
        grid=(seq // dx_rows, 1, 1), tb=True,
        extras=[(dz1, _sp((dx_rows, d_model), row_full)), after(token_mix)],
        out_shape=[sds((seq, d_model), F32)], out_specs=[_sp((dx_rows, d_model), row_full)],
        epilogue=dx_epi,
        side=_adamw_side(r_wout[0], r_wout[1], w_out[0], m_w_out[0], v_w_out[0], seq // dx_rows))
    o_wout = [r.reshape(w_out.shape) for r in o_wout]
    r_win, r_wpool = scatter_mix.wait(after=dx)
    o_win = adam_big("adam_w_in", r_win, w_in, m_w_in, v_w_in)
    o_wpool = adam_big("adam_w_pool", r_wpool, w_pool, m_w_pool, v_w_pool)

    small_gather.relay(after=o_win[0])
    (small_all,) = small_gather.wait(after=o_wpool[0])

    rep_w = [w_rg_a, w_rg_i, ln_mix_g, ln_mix_b, ln_ffn_g, ln_ffn_b, pool_scale, conv_b]
    rep_m = [m_w_rg_a, m_w_rg_i, m_ln_mix_g, m_ln_mix_b, m_ln_ffn_g, m_ln_ffn_b, m_pool_scale, m_conv_b]
    rep_v = [v_w_rg_a, v_w_rg_i, v_ln_mix_g, v_ln_mix_b, v_ln_ffn_g, v_ln_ffn_b, v_pool_scale, v_conv_b]
    cat = lambda arrs: jnp.concatenate([_rows128(a) for a in arrs], axis=0)
    o_rep = _sum_adamw("adam_replicated", None, small_all, cat(rep_w), cat(rep_m), cat(rep_v))

    my_idx = _dev_index(_where_am_i())
    head_parts = lax.dynamic_slice_in_dim(small_all, n_rep + my_idx * SMALL_ROWS, SMALL_ROWS, axis=1)
    o_head = _sum_adamw("adam_head", None, head_parts, pack_mine,
                        small_pack(m_conv_w, m_b_rg_a, m_b_rg_i, m_rg_lambda),
                        small_pack(v_conv_w, v_b_rg_a, v_b_rg_i, v_rg_lambda))

    def unpack_rep(packed):
        out, r = [], 0
        for wgt, rows in zip(rep_w, rep_rows):
            out.append(packed[r:r + rows].reshape(wgt.shape))
            r += rows
        return out

    def unpack_head(packed):
        return [packed[0:4].reshape(conv_w.shape), packed[4:6].reshape(b_rg_a.shape),
                packed[6:8].reshape(b_rg_i.shape), packed[8:10].reshape(rg_lambda.shape)]

    loss = _sum_blocks("loss_sum", small_all[:, n_rep + N_HEADS * SMALL_ROWS:, :])[0, 0]

    outs = [loss, dx[None]]
    for kind in range(4):
        ra, ri, mg, mb, fg, fb, ps, cb = unpack_rep(o_rep[kind])
        cw, ba, bi, lam = unpack_head(o_head[kind])
        outs += [mg, mb, o_win[kind], o_wpool[kind], ps, cw, cb, ra, ba, ri, bi, lam, o_wout[kind], fg, fb,
                 o_w1[kind], o_w2[kind]]
    return tuple(outs)
```

```python
import functools

import jax
import jax.numpy as jnp
from jax import lax
from jax.experimental import pallas as pl
from jax.experimental.pallas import tpu as pltpu

F32 = jnp.float32
BF16 = jnp.bfloat16
MESH = pl.DeviceIdType.MESH
ANY = pl.BlockSpec(memory_space=pl.ANY)

N_DEV = 8
POOL_WINDOWS = (2, 4, 8, 16)
N_HEADS = 8
RG_C = 8.0
LN_EPS = 1e-5
ALPHA = 2.0 ** 0.25
ADAM_LR = 0.001
ADAM_B1 = 0.9
ADAM_B2 = 0.999
ADAM_EPS = 1e-08
ADAM_WD = 0.01
ADAM_STEP = 10

VMEM_LIMIT = 56 * 1024 * 1024
SEQ_CHUNK = 128
WIN_HALO = 16
CONV_HALO = 8
SMALL_ROWS = 16


def _params(n_grid):
    return pltpu.CompilerParams(dimension_semantics=("arbitrary",) * n_grid, vmem_limit_bytes=VMEM_LIMIT)


def _shift(v, j):
    n = v.shape[0]
    s = (-j) % n
    return v if s == 0 else pltpu.roll(v, s, 0)


def _sigmoid(x):
    return 0.5 * jnp.tanh(0.5 * x) + 0.5


def _softplus(z):
    e = jnp.exp(-jnp.abs(z))
    u = 1.0 + e
    log1p = jnp.where(u == 1.0, e, jnp.log(u) * (e / jnp.where(u == 1.0, 1.0, u - 1.0)))
    return jnp.maximum(z, 0.0) + log1p


_GELU_C = 0.7978845608028654
_GELU_K = 0.044715


def _gelu_and_grad(x):
    x2 = x * x
    t = jnp.tanh(_GELU_C * (x + _GELU_K * x * x2))
    g = 0.5 * x * (1.0 + t)
    dg = 0.5 * (1.0 + t) + 0.5 * x * (1.0 - t * t) * (_GELU_C * (1.0 + 3.0 * _GELU_K * x2))
    return g, dg


def _ln_fwd(z, g, b):
    mu = jnp.mean(z, axis=-1, keepdims=True)
    zc = z - mu
    var = jnp.mean(zc * zc, axis=-1, keepdims=True)
    rstd = lax.rsqrt(var + LN_EPS)
    xhat = zc * rstd
    return xhat * g + b, xhat, rstd


def _ln_bwd(dy, xhat, rstd, g):
    dxhat = dy * g
    m1 = jnp.mean(dxhat, axis=-1, keepdims=True)
    m2 = jnp.mean(dxhat * xhat, axis=-1, keepdims=True)
    dz = rstd * (dxhat - m1 - xhat * m2)
    dg = jnp.sum(dy * xhat, axis=0, keepdims=True)
    db = jnp.sum(dy, axis=0, keepdims=True)
    return dz, dg, db


def _acc_rows(ref, first, val):
    @pl.when(first)
    def _():
        ref[...] = val

    @pl.when(jnp.logical_not(first))
    def _():
        ref[...] += val


def _sp(shape, fn, single=False):
    return shape, fn, single


def _matmul(name, a, b, a_spec, b_spec, *, grid, j_outer=False, ta=False, tb=False, extras=(), out_shape, out_specs,
            epilogue=None, n_split=1, side=None):
    ni, nj, nk = grid
    n_ex = len(extras)
    dims = (((0 if ta else 1,), (1 if tb else 0,)), ((), ()))
    side_in, side_shape, side_out, side_fn = side if side is not None else ((), (), (), None)
    n_main_out = len(out_shape)
    inner = ni if j_outer else nj

    def mk(spec):
        shape, fn, single = spec
        index = (lambda g0, g1, g2: fn(g1, g0, g2)) if j_outer else fn
        return pl.BlockSpec(shape, index, pipeline_mode=pl.Buffered(1)) if single else pl.BlockSpec(shape, index)

    def mk_side(block, fn):
        return pl.BlockSpec(block, lambda g0, g1, g2: fn(g0 * inner + g1))

    def body(a_ref, b_ref, *rest):
        ex_refs = rest[:n_ex]
        out_refs = rest[n_ex + len(side_in):n_ex + len(side_in) + n_main_out]
        if side_fn is not None:
            side_fn(pl.program_id(0) * inner + pl.program_id(1), rest[n_ex:n_ex + len(side_in)],
                    rest[n_ex + len(side_in) + n_main_out:])
        i = pl.program_id(1 if j_outer else 0)
        if n_split > 1:
            av = a_ref[...].astype(BF16)
            width = b_ref.shape[0 if tb else 1] // n_split
            for c in range(n_split):
                cols = pl.ds(c * width, width)
                bv = (b_ref[cols, :] if tb else b_ref[:, cols]).astype(BF16)
                epilogue(lax.dot_general(av, bv, dims, preferred_element_type=F32), i, ex_refs, out_refs, cols)
            return
        part = lax.dot_general(a_ref[...].astype(BF16), b_ref[...].astype(BF16), dims, preferred_element_type=F32)
        if nk == 1:
            epilogue(part, i, ex_refs, out_refs)
        else:
            @pl.when(pl.program_id(2) == 0)
            def _():
                out_refs[0][...] = part

            @pl.when(pl.program_id(2) > 0)
            def _():
                out_refs[0][...] += part

    return pl.pallas_call(
        body, name=name, grid=(nj, ni, nk) if j_outer else (ni, nj, nk),
        in_specs=[mk(a_spec), mk(b_spec)] + [mk(s) for _, s in extras] + [mk_side(blk, fn) for _, blk, fn in side_in],
        out_specs=[mk(s) for s in out_specs] + [mk_side(blk, fn) for blk, fn in side_out],
        out_shape=list(out_shape) + list(side_shape),
        compiler_params=_params(3),
    )(a, b, *[x for x, _ in extras], *[x for x, _, _ in side_in])


def _bs(shape, fn):
    return pl.BlockSpec(shape, fn)


def _where_am_i():
    x, y, c = lax.axis_index("x"), lax.axis_index("y"), lax.axis_index("c")
    return x, y, c


def _dev_index(p):
    return 4 * p[0] + 2 * p[1] + p[2]


def _slab(ref, axis, idx, size):
    sl = [slice(None)] * len(ref.shape)
    sl[axis] = pl.ds(idx * size, size)
    return ref.at[tuple(sl)]


HBM = pl.BlockSpec(memory_space=pltpu.HBM)
SEM = pl.BlockSpec(memory_space=pltpu.SEMAPHORE)
DATAFLOW = pltpu.SideEffectType.DATAFLOW_SIDE_EFFECTING


def _in_hbm(a):
    return pltpu.with_memory_space_constraint(a, pltpu.HBM)


def _token_shape():
    return jax.ShapeDtypeStruct((8, 128), F32)


def _split_start(name, n_sems, bufs, issue):
    nb = len(bufs)

    def body(*refs):
        issue(refs[:nb], refs[nb], refs[nb + 1])
        refs[-1][...] = jnp.zeros((8, 128), F32)

    outs = pl.pallas_call(
        body, name=name,
        out_shape=(pltpu.SemaphoreType.DMA((n_sems,)), pltpu.SemaphoreType.DMA((n_sems,)),
                   *[pltpu.HBM(b.shape, b.dtype) for b in bufs], _token_shape()),
        in_specs=[HBM] * nb, out_specs=(SEM, SEM, *[HBM] * nb, pl.BlockSpec(memory_space=pltpu.VMEM)),
        input_output_aliases={i: 2 + i for i in range(nb)},
        compiler_params=pltpu.CompilerParams(has_side_effects=DATAFLOW),
    )(*[_in_hbm(b) for b in bufs])
    return outs[0], outs[1], list(outs[2:2 + nb]), outs[-1]


def _split_relay(name, n_sems, sems, bufs, after, relay):
    nb = len(bufs)

    def body(*refs):
        relay(refs[:nb], refs[nb], refs[nb + 1], refs[nb + 3], refs[nb + 4])
        refs[-1][...] = jnp.zeros((8, 128), F32)

    outs = pl.pallas_call(
        body, name=name,
        out_shape=(pltpu.SemaphoreType.DMA((n_sems,)), pltpu.SemaphoreType.DMA((n_sems,)),
                   *[pltpu.HBM(b.shape, b.dtype) for b in bufs], _token_shape()),
        in_specs=[HBM] * nb + [SEM, SEM, ANY],
        out_specs=(SEM, SEM, *[HBM] * nb, pl.BlockSpec(memory_space=pltpu.VMEM)),
        input_output_aliases={i: 2 + i for i in range(nb)},
        compiler_params=pltpu.CompilerParams(has_side_effects=DATAFLOW),
    )(*bufs, sems[0], sems[1], after)
    return outs[0], outs[1], list(outs[2:2 + nb]), outs[-1]


def _split_wait(name, sems, bufs, after, finish):
    nb = len(bufs)

    def body(*refs):
        finish(refs[:nb], refs[nb], refs[nb + 1])

    outs = pl.pallas_call(
        body, name=name, out_shape=[pltpu.HBM(b.shape, b.dtype) for b in bufs],
        in_specs=[HBM] * nb + [SEM, SEM, ANY], out_specs=[HBM] * nb,
        input_output_aliases={i: i for i in range(nb)},
        compiler_params=pltpu.CompilerParams(has_side_effects=DATAFLOW),
    )(*bufs, sems[0], sems[1], after)
    return list(outs)


def _place(name, items, dtype, after):
    ids = jnp.reshape(_dev_index(_where_am_i()), (1,)).astype(jnp.int32)
    outs = []
    for a, (shard, axis) in enumerate(items):
        rows, cols = shard.shape[-2], shard.shape[-1]
        tr = rows
        while tr * cols * shard.dtype.itemsize > 4 * 1024 * 1024 and tr % 32 == 0:
            tr //= 2
        nt = rows // tr
        full = list(shard.shape)
        full[axis] *= N_DEV
        if shard.ndim == 2 and axis == 0:
            in_spec = _bs((tr, cols), lambda i, ids: (i, 0))
            out_spec = _bs((tr, cols), lambda i, ids, nt=nt: (ids[0] * nt + i, 0))
        elif shard.ndim == 2 and axis == 1:
            in_spec = _bs((tr, cols), lambda i, ids: (i, 0))
            out_spec = _bs((tr, cols), lambda i, ids: (i, ids[0]))
        elif shard.ndim == 3 and axis == 1:
            tr, nt = rows, shard.shape[0]
            in_spec = _bs((None, rows, cols), lambda i, ids: (i, 0, 0))
            out_spec = _bs((None, rows, cols), lambda i, ids: (i, ids[0], 0))
        else:
            assert shard.ndim == 3 and axis == 0 and shard.shape[0] == 1
            in_spec = _bs((None, tr, cols), lambda i, ids: (0, i, 0))
            out_spec = _bs((None, tr, cols), lambda i, ids: (ids[0], i, 0))

        def body(ids_ref, in_ref, after_ref, out_ref):
            del ids_ref, after_ref
            out_ref[...] = in_ref[...].astype(out_ref.dtype)

        outs.append(pl.pallas_call(
            body, name=f"{name}{a}",
            grid_spec=pltpu.PrefetchScalarGridSpec(
                num_scalar_prefetch=1, grid=(nt,), in_specs=[in_spec, ANY], out_specs=out_spec),
            out_shape=jax.ShapeDtypeStruct(tuple(full), dtype), compiler_params=_params(1),
        )(ids, shard, after))
    return outs


class _SplitGather:
    def __init__(self, name, items, dtype, after):
        self.name, self.items, self.n = name, items, len(items)
        fulls = _place(name + "_place", items, dtype, after)
        n = self.n

        def issue(refs, send, recv):
            me, sibling, chips, c = self._geometry()
            for a in range(n):
                self._copy1(refs, send, recv, a, 0, me, sibling).start()
                for j, chip in enumerate(chips):
                    self._copy1(refs, send, recv, a, 1 + j, me, (*chip, c)).start()

        self.send, self.recv, self.bufs, self.token = _split_start(name + "_start", 4 * n, fulls, issue)

    @staticmethod
    def _geometry():
        x, y, c = _where_am_i()
        return (x, y, c), (x, y, 1 - c), [(1 - x, y), (x, 1 - y), (1 - x, 1 - y)], c

    def _blk(self, refs, a, p):
        shard, axis = self.items[a]
        return _slab(refs[a], axis, _dev_index(p), shard.shape[axis])

    def _copy1(self, refs, send, recv, a, k, owner, to):
        return pltpu.make_async_remote_copy(
            src_ref=self._blk(refs, a, owner), dst_ref=self._blk(refs, a, owner), send_sem=send.at[4 * a + k],
            recv_sem=recv.at[4 * a + k], device_id=to, device_id_type=MESH)

    def _copy2(self, refs, send, recv, a, j, owner, to):
        return pltpu.make_async_remote_copy(
            src_ref=self._blk(refs, a, owner), dst_ref=self._blk(refs, a, owner), send_sem=send.at[3 * a + j],
            recv_sem=recv.at[3 * a + j], device_id=to, device_id_type=MESH)

    def relay(self, after):
        n = self.n

        def relay(refs, send_in, recv_in, send_out, recv_out):
            me, sibling, chips, c = self._geometry()
            for a in range(n):
                for j, chip in enumerate(chips):
                    self._copy1(refs, send_in, recv_in, a, 1 + j, (*chip, c), me).wait_recv()
                    self._copy2(refs, send_out, recv_out, a, j, (*chip, c), sibling).start()
            for a in range(n):
                self._copy1(refs, send_in, recv_in, a, 0, sibling, me).wait_recv()
                for k in range(4):
                    self._copy1(refs, send_in, recv_in, a, k, me, sibling).wait_send()

        self.send, self.recv, self.bufs, self.token = _split_relay(
            self.name + "_relay", 3 * n, (self.send, self.recv), self.bufs, after, relay)
        return self.token

    def wait(self, after):
        n = self.n

        def finish(refs, send, recv):
            me, sibling, chips, c = self._geometry()
            for a in range(n):
                for j, chip in enumerate(chips):
                    self._copy2(refs, send, recv, a, j, (*chip, 1 - c), me).wait_recv()
                    self._copy2(refs, send, recv, a, j, (*chip, c), sibling).wait_send()

        return _split_wait(self.name + "_wait", (self.send, self.recv), self.bufs, after, finish)


class _SplitReduceScatter:
    def __init__(self, name, grads):
        self.name, self.n = name, len(grads)
        n = self.n
        g4 = [g.reshape(4, 2, *g.shape[1:]) for g in grads]
        land = [lax.empty((4, 1, *g.shape[1:]), g.dtype) for g in grads]

        def issue(refs, send, recv):
            for a in range(n):
                self._swap(refs, send, recv, a).start()

        self.send, self.recv, self.bufs, self.token = _split_start(name + "_d2d_start", n, g4 + land, issue)

    def _swap(self, refs, send, recv, a):
        x, y, c = _where_am_i()
        return pltpu.make_async_remote_copy(
            src_ref=refs[a].at[:, pl.ds(1 - c, 1)], dst_ref=refs[self.n + a], send_sem=send.at[a], recv_sem=recv.at[a],
            device_id=(x, y, 1 - c), device_id_type=MESH)

    def _hop(self, refs, send, recv, a, m):
        x, y, c = _where_am_i()
        px = (1 - x) if m & 2 else x
        py = (1 - y) if m & 1 else y
        return pltpu.make_async_remote_copy(
            src_ref=refs[a].at[2 * px + py], dst_ref=refs[self.n + a].at[m - 1], send_sem=send.at[3 * a + m - 1],
            recv_sem=recv.at[3 * a + m - 1], device_id=(px, py, c), device_id_type=MESH)

    def combine_and_send(self, after):
        n = self.n

        def finish(refs, send, recv):
            for a in range(n):
                self._swap(refs, send, recv, a).wait()

        bufs = _split_wait(self.name + "_d2d_wait", (self.send, self.recv), self.bufs, after, finish)
        x, y, c = _where_am_i()
        ids = jnp.stack([c, 2 * x + y]).astype(jnp.int32)
        self.own, sums = [], []
        for a in range(n):
            own, hb = _pair_sum(f"{self.name}_sum{a}", bufs[a], bufs[n + a], ids)
            self.own.append(own)
            sums.append(hb)
        land = [lax.empty((3, *h.shape[1:]), h.dtype) for h in sums]

        def issue(refs, send, recv):
            for a in range(n):
                for m in (1, 2, 3):
                    self._hop(refs, send, recv, a, m).start()

        self.send, self.recv, self.bufs, self.token = _split_start(self.name + "_ici_start", 3 * n, sums + land, issue)
        return self.token

    def wait(self, after):
        n = self.n

        def finish(refs, send, recv):
            for a in range(n):
                for m in (1, 2, 3):
                    self._hop(refs, send, recv, a, m).wait()

        bufs = _split_wait(self.name + "_ici_wait", (self.send, self.recv), self.bufs, after, finish)
        return list(zip(self.own, bufs[n:]))


def _pair_sum(name, g4, land, ids):
    rows, cols = g4.shape[2], g4.shape[3]
    tr = rows
    while tr * cols * 2 > 2 * 1024 * 1024 and tr % 32 == 0:
        tr //= 2

    def body(ids_ref, g_ref, l_ref, own_ref, sum_ref):
        h = g_ref[...].astype(F32) + l_ref[...].astype(F32)
        sum_ref[...] = h.astype(sum_ref.dtype)

        @pl.when(pl.program_id(1) == ids_ref[1])
        def _():
            own_ref[...] = h

    return pl.pallas_call(
        body, name=name,
        grid_spec=pltpu.PrefetchScalarGridSpec(
            num_scalar_prefetch=1, grid=(rows // tr, 4),
            in_specs=[_bs((None, None, tr, cols), lambda i, q, ids: (q, ids[0], i, 0)),
                      _bs((None, None, tr, cols), lambda i, q, ids: (q, 0, i, 0))],
            out_specs=[_bs((tr, cols), lambda i, q, ids: (i, 0)), _bs((None, tr, cols), lambda i, q, ids: (q, i, 0))]),
        out_shape=[jax.ShapeDtypeStruct((rows, cols), F32), jax.ShapeDtypeStruct((4, rows, cols), g4.dtype)],
        compiler_params=_params(2),
    )(ids, g4, land)


def _win_sum(ext, w, off):
    s = ext + _shift(ext, -1)
    if w >= 4:
        s = _shift(s, -1) + _shift(s, 1)
    if w >= 8:
        s = _shift(s, -2) + _shift(s, 2)
    if w >= 16:
        s = _shift(s, -4) + _shift(s, 4)
    return _shift(s, off) if off else s


def _inv_count(r0, t, w, seq):
    pos = r0 + lax.broadcasted_iota(jnp.int32, (t, 1), 0)
    cnt = jnp.minimum(pos + w // 2, seq) - jnp.maximum(pos - w // 2, 0)
    return 1.0 / cnt.astype(F32)


def _pool_fwd(p3, w_pool, pool_scale, seq, d_model):
    dp = d_model // 2
    pg = dp // len(POOL_WINDOWS)
    t = min(SEQ_CHUNK, seq)
    n_chunks = seq // t
    h = WIN_HALO

    def body(u_ref, w_ref, sc_ref, d_ref, y_ref, pad_ref):
        g = pl.program_id(0)
        zeros = jnp.zeros((h, pg), F32)
        pad_ref[0:h, :] = zeros
        pad_ref[h + seq:h + seq + h, :] = zeros

        def fill(ci, _):
            r0 = pl.multiple_of(ci * t, t)
            pad_ref[pl.ds(h + r0, t), :] = u_ref[pl.ds(r0, t), :]
            return 0

        lax.fori_loop(0, n_chunks, fill, 0)
        wmat = w_ref[...]
        scale = sc_ref[...]
        for gi, w in enumerate(POOL_WINDOWS):
            @pl.when(g == gi)
            def _(w=w):
                def chunk(ci, _):
                    r0 = pl.multiple_of(ci * t, t)
                    ext = pad_ref[pl.ds(r0, t + 2 * h), :]
                    mean = _win_sum(ext, w, 0)[h:h + t, :] * _inv_count(r0, t, w, seq)
                    d = (mean - ext[h:h + t, :]).astype(BF16)
                    d_ref[pl.ds(r0, t), :] = d
                    q = jnp.dot(d, wmat, preferred_element_type=F32)
                    y_ref[pl.ds(r0, t), :] = (q * scale).astype(BF16)
                    return 0

                lax.fori_loop(0, n_chunks, chunk, 0, unroll=2)

    return pl.pallas_call(
        body, name="pool_fwd", grid=(len(POOL_WINDOWS),),
        in_specs=[_bs((None, seq, pg), lambda g: (0, 0, g)), _bs((None, pg, pg), lambda g: (g, 0, 0)),
                  _bs((1, pg), lambda g: (0, g))],
        out_specs=[_bs((seq, pg), lambda g: (0, g)), _bs((seq, pg), lambda g: (0, g))],
        out_shape=[jax.ShapeDtypeStruct((seq, dp), BF16), jax.ShapeDtypeStruct((seq, d_model), BF16)],
        scratch_shapes=[pltpu.VMEM((seq + 2 * h, pg), F32)],
        compiler_params=_params(1),
    )(p3, w_pool, pool_scale)


def _pool_bwd(d, dy, w_pool, pool_scale, token, seq, d_model):
    dp = d_model // 2
    pg = dp // len(POOL_WINDOWS)
    t = min(SEQ_CHUNK, seq)
    n_chunks = seq // t
    h = WIN_HALO
    tn_dims = (((0,), (0,)), ((), ()))
    nt_dims = (((1,), (1,)), ((), ()))

    def body(d_ref, dy_ref, w_ref, sc_ref, tok_ref, du_ref, dwb_ref, dsc_ref, pad_ref, dd_ref, dw_ref):
        del tok_ref
        g = pl.program_id(0)
        zeros = jnp.zeros((h, pg), F32)
        pad_ref[0:h, :] = zeros
        pad_ref[h + seq:h + seq + h, :] = zeros
        wmat = w_ref[...]
        scale = sc_ref[...]
        for gi, w in enumerate(POOL_WINDOWS):
            @pl.when(g == gi)
            def _(w=w):
                dw_ref[...] = jnp.zeros((pg, pg), F32)

                def first(ci, dsc):
                    r0 = pl.multiple_of(ci * t, t)
                    dv = d_ref[pl.ds(r0, t), :]
                    dyv = dy_ref[pl.ds(r0, t), :]
                    q = jnp.dot(dv, wmat, preferred_element_type=F32)
                    dsc = dsc + jnp.sum(dyv * q, axis=0, keepdims=True)
                    dq = (dyv * scale).astype(BF16)
                    dw_ref[...] += lax.dot_general(dv, dq, tn_dims, preferred_element_type=F32)
                    dd = lax.dot_general(dq, wmat, nt_dims, preferred_element_type=F32)
                    dd_ref[pl.ds(r0, t), :] = dd
                    pad_ref[pl.ds(h + r0, t), :] = dd * _inv_count(r0, t, w, seq)
                    return dsc

                def first_pair(cj, dsc):
                    return first(2 * cj + 1, first(2 * cj, dsc))

                dsc_ref[...] = lax.fori_loop(0, n_chunks // 2, first_pair, jnp.zeros((1, pg), F32))
                dwb_ref[...] = dw_ref[...].reshape(N_DEV, pg // N_DEV, pg).astype(BF16)

                def second(ci, _):
                    r0 = pl.multiple_of(ci * t, t)
                    ext = pad_ref[pl.ds(r0, t + 2 * h), :]
                    back = _win_sum(ext, w, 1)[h:h + t, :]
                    du_ref[pl.ds(r0, t), :] = (back - dd_ref[pl.ds(r0, t), :]).astype(BF16)
                    return 0

                lax.fori_loop(0, n_chunks, second, 0, unroll=2)

    return pl.pallas_call(
        body, name="pool_bwd", grid=(len(POOL_WINDOWS),),
        in_specs=[_bs((seq, pg), lambda g: (0, g)), _bs((seq, pg), lambda g: (0, g)),
                  _bs((None, pg, pg), lambda g: (g, 0, 0)), _bs((1, pg), lambda g: (0, g)),
                  _bs((8, 128), lambda g: (0, 0))],
        out_specs=[_bs((seq, pg), lambda g: (0, g)), _bs((N_DEV, None, pg // N_DEV, pg), lambda g: (0, g, 0, 0)),
                   _bs((1, pg), lambda g: (0, g))],
        out_shape=[jax.ShapeDtypeStruct((seq, 3 * dp), BF16),
                   jax.ShapeDtypeStruct((N_DEV, len(POOL_WINDOWS), pg // N_DEV, pg), BF16),
                   jax.ShapeDtypeStruct((1, dp), F32)],
        scratch_shapes=[pltpu.VMEM((seq + 2 * h, pg), F32), pltpu.VMEM((seq, pg), F32), pltpu.VMEM((pg, pg), F32)],
        compiler_params=_params(1),
    )(d, dy, w_pool, pool_scale, token)


def _tile_scan(n_tiles, lanes, loads, stores):
    row = lax.broadcasted_iota(jnp.int32, (8, lanes), 0)
    group = 8

    def local_scan(n, k):
        aa, bb = loads[n](k)
        for sh in (1, 2, 4):
            if n == 0:
                ok = row >= sh
                ap = jnp.where(ok, pltpu.roll(aa, sh, 0), 1.0)
                bp = jnp.where(ok, pltpu.roll(bb, sh, 0), 0.0)
            else:
                ok = row < 8 - sh
                ap = jnp.where(ok, pltpu.roll(aa, 8 - sh, 0), 1.0)
                bp = jnp.where(ok, pltpu.roll(bb, 8 - sh, 0), 0.0)
            bb = aa * bp + bb
            aa = aa * ap
        return aa, bb

    def step(s, carry):
        carry = list(carry)
        for n in range(2):
            tiles = [s * group + u if n == 0 else n_tiles - 1 - (s * group + u) for u in range(group)]
            local = [local_scan(n, k) for k in tiles]
            for k, (aa, bb) in zip(tiles, local):
                hh = bb + aa * carry[n]
                stores[n](k, hh)
                carry[n] = jnp.broadcast_to(hh[7:8, :] if n == 0 else hh[0:1, :], (8, lanes))
        return tuple(carry)

    zeros = jnp.zeros((8, lanes), F32)
    lax.fori_loop(0, n_tiles // group, step, (zeros, zeros))


def _gate_preacts(xc, wcat_ref):
    xcb = xc.astype(BF16)
    return xcb, jnp.dot(xcb, wcat_ref[...], preferred_element_type=F32)


def _gates(pre, n, pk_ref, sp):
    lh = pre.shape[1] // 4
    r = _sigmoid(pre[:, (2 * n) * lh:(2 * n + 1) * lh] + pk_ref[pl.ds(4 + n, 1), :])
    i = _sigmoid(pre[:, (2 * n + 1) * lh:(2 * n + 2) * lh] + pk_ref[pl.ds(6 + n, 1), :])
    log_a = (-RG_C * r) * sp[n]
    a = jnp.exp(log_a)
    x = 2.0 * log_a
    one_minus_a2 = jnp.where(x > -0.01, -(x * (1.0 + x * (0.5 + x * (1.0 / 6.0)))), 1.0 - a * a)
    m = jnp.sqrt(one_minus_a2)
    return r, i, a, m


def _conv_chunk(upad_ref, pk_ref, cb, r0, t):
    ext = upad_ref[pl.ds(r0, t + 2 * CONV_HALO), :]
    acc = pk_ref[pl.ds(1, 1), :] * ext
    for k in (0, 2, 3):
        acc = acc + pk_ref[pl.ds(k, 1), :] * _shift(ext, k - 1)
    return acc[CONV_HALO:CONV_HALO + t, :] + cb, ext


def _lru_fwd(p3, y_in, pack, conv_b, wcat, token, seq, d_model):
    dl = d_model // 2
    lh = dl // N_HEADS
    t = min(SEQ_CHUNK, seq)
    n_chunks = seq // t
    hal = CONV_HALO
    first_rec_block = (d_model - dl) // lh

    def body(ur_ref, ug_ref, pk_ref, cb_ref, wcat_ref, yin_ref, tok_ref, y_ref, h0_ref, h1_ref,
             upad, a_scr, b_scr):
        del yin_ref, tok_ref
        zeros = jnp.zeros((hal, lh), F32)
        upad[0:hal, :] = zeros
        upad[hal + seq:hal + seq + hal, :] = zeros
        for ref in (h0_ref, h1_ref):
            ref[0:hal, :] = zeros
            ref[hal + seq:hal + seq + hal, :] = zeros

        def fill(ci, _):
            r0 = pl.multiple_of(ci * t, t)
            upad[pl.ds(hal + r0, t), :] = ur_ref[pl.ds(r0, t), :]
            return 0

        lax.fori_loop(0, n_chunks, fill, 0)
        cb = cb_ref[...]
        sp = [_softplus(-pk_ref[pl.ds(8 + n, 1), :]) for n in range(2)]

        def chunk(ci, _):
            r0 = pl.multiple_of(ci * t, t)
            xc, _ext = _conv_chunk(upad, pk_ref, cb, r0, t)
            _, pre = _gate_preacts(xc, wcat_ref)
            for n in range(2):
                _, i, a, m = _gates(pre, n, pk_ref, sp)
                a_scr[n, pl.ds(r0, t), :] = a
                b_scr[n, pl.ds(r0, t), :] = (m * i) * xc
            return 0

        lax.fori_loop(0, n_chunks, chunk, 0, unroll=2)

        def load(n):
            def get(k):
                at = pl.ds(pl.multiple_of(k * 8, 8), 8)
                return a_scr[n, at, :], b_scr[n, at, :]
            return get

        def store(ref):
            def put(k, v):
                ref[pl.ds(pl.multiple_of(hal + k * 8, 8), 8), :] = v
            return put

        _tile_scan(seq // 8, lh, [load(0), load(1)], [store(h0_ref), store(h1_ref)])

        def out(ci, _):
            r0 = pl.multiple_of(ci * t, t)
            hsum = h0_ref[pl.ds(hal + r0, t), :] + h1_ref[pl.ds(hal + r0, t), :]
            gl, _dg = _gelu_and_grad(ug_ref[pl.ds(r0, t), :])
            y_ref[pl.ds(r0, t), :] = (hsum * gl).astype(BF16)
            return 0

        lax.fori_loop(0, n_chunks, out, 0)

    return pl.pallas_call(
        body, name="lru_fwd", grid=(N_HEADS,),
        in_specs=[_bs((None, seq, lh), lambda h: (1, 0, h)), _bs((None, seq, lh), lambda h: (2, 0, h)),
                  _bs((None, SMALL_ROWS, lh), lambda h: (h, 0, 0)), _bs((1, lh), lambda h: (0, h)),
                  _bs((None, lh, 4 * lh), lambda h: (h, 0, 0)),
                  ANY, _bs((8, 128), lambda h: (0, 0))],
        out_specs=[_bs((seq, lh), lambda h: (0, first_rec_block + h)),
                   _bs((seq + 2 * hal, lh), lambda h: (0, h)), _bs((seq + 2 * hal, lh), lambda h: (0, h))],
        out_shape=[jax.ShapeDtypeStruct((seq, d_model), BF16), jax.ShapeDtypeStruct((seq + 2 * hal, dl), F32),
                   jax.ShapeDtypeStruct((seq + 2 * hal, dl), F32)],
        scratch_shapes=[pltpu.VMEM((seq + 2 * hal, lh), F32), pltpu.VMEM((2, seq, lh), F32),
                        pltpu.VMEM((2, seq, lh), F32)],
        input_output_aliases={5: 0},
        compiler_params=_params(1),
    )(p3, p3, pack, conv_b, wcat, y_in, token)


def _lru_bwd(p3, dy, h0p, h1p, dproj_in, pack, conv_b, wcat, token, seq, d_model):
    dl = d_model // 2
    lh = dl // N_HEADS
    t = min(SEQ_CHUNK, seq)
    n_chunks = seq // t
    hal = CONV_HALO
    first_rec_block = (d_model - dl) // lh
    tn_dims = (((0,), (0,)), ((), ()))
    nt_dims = (((1,), (1,)), ((), ()))

    def body(ur_ref, ug_ref, dy_ref, h0_ref, h1_ref, pk_ref, cb_ref, wcat_ref, tok_ref, din_ref,
             dproj_ref, dpk_ref, dcb_ref, dwcat_ref,
             upad, a_scr, dh_scr, g_scr, dxc_pad, dpr_ref, out_sems, gate_scr):
        del din_ref, tok_ref
        zeros = jnp.zeros((hal, lh), F32)
        for ref in (upad, dxc_pad):
            ref[0:hal, :] = zeros
            ref[hal + seq:hal + seq + hal, :] = zeros
        for n in range(2):
            a_scr[n, 0:hal, :] = zeros
            a_scr[n, hal + seq:hal + seq + hal, :] = zeros

        def fill(ci, _):
            r0 = pl.multiple_of(ci * t, t)
            upad[pl.ds(hal + r0, t), :] = ur_ref[pl.ds(r0, t), :]
            return 0

        lax.fori_loop(0, n_chunks, fill, 0)
        cb = cb_ref[...]
        lam = [pk_ref[pl.ds(8 + n, 1), :] for n in range(2)]
        sp = [_softplus(-lam[n]) for n in range(2)]

        def chunk1(ci, _):
            r0 = pl.multiple_of(ci * t, t)
            xc, _ext = _conv_chunk(upad, pk_ref, cb, r0, t)
            _, pre = _gate_preacts(xc, wcat_ref)
            for n in range(2):
                r, i, a, m = _gates(pre, n, pk_ref, sp)
                a_scr[n, pl.ds(hal + r0, t), :] = a
                for q, v in enumerate((r, i, m)):
                    gate_scr[3 * n + q, pl.ds(r0, t), :] = v
            hsum = h0_ref[pl.ds(hal + r0, t), :] + h1_ref[pl.ds(hal + r0, t), :]
            gl, dgl = _gelu_and_grad(ug_ref[pl.ds(r0, t), :])
            dyv = dy_ref[pl.ds(r0, t), :]
            dh_scr[pl.ds(r0, t), :] = dyv * gl
            dpr_ref[1, pl.ds(r0, t), :] = ((dyv * hsum) * dgl).astype(BF16)
            return 0

        lax.fori_loop(0, n_chunks, chunk1, 0, unroll=2)

        def load(n):
            def get(k):
                r0 = pl.multiple_of(k * 8, 8)
                if n == 0:
                    coef = _shift(a_scr[0, pl.ds(pl.multiple_of(hal + r0, 8), 16), :], 1)[0:8, :]
                else:
                    coef = _shift(a_scr[1, pl.ds(pl.multiple_of(hal + r0 - 8, 8), 16), :], -1)[8:16, :]
                return coef, dh_scr[pl.ds(r0, 8), :]
            return get

        def store(n):
            def put(k, v):
                g_scr[n, pl.ds(pl.multiple_of(k * 8, 8), 8), :] = v
            return put

        _tile_scan(seq // 8, lh, [load(1), load(0)], [store(1), store(0)])

        dwcat_ref[...] = jnp.zeros((lh, 4 * lh), F32)

        def chunk3(ci, carry):
            dba, dbi, dlam, dcb = carry
            r0 = pl.multiple_of(ci * t, t)
            xc, _ext = _conv_chunk(upad, pk_ref, cb, r0, t)
            xcb = xc.astype(BF16)
            dxc = jnp.zeros((t, lh), F32)
            dba, dbi, dlam = list(dba), list(dbi), list(dlam)
            dpre = []
            for n in range(2):
                r, i, m = (gate_scr[3 * n + q, pl.ds(r0, t), :] for q in range(3))
                a = a_scr[n, pl.ds(hal + r0, t), :]
                hext = (h0_ref if n == 0 else h1_ref)[pl.ds(r0, t + 2 * hal), :]
                hprev = _shift(hext, -1 if n == 0 else 1)[hal:hal + t, :]
                gb = g_scr[n, pl.ds(r0, t), :]
                da = gb * hprev
                dm = gb * i * xc
                di = gb * m * xc
                dxc = dxc + gb * (m * i)
                dlog_a = da * a - dm * (a * a) / m
                dr = dlog_a * (-RG_C * sp[n])
                dlam[n] = dlam[n] + jnp.sum(dlog_a * r, axis=0, keepdims=True)
                dpr = dr * r * (1.0 - r)
                dpi = di * i * (1.0 - i)
                dba[n] = dba[n] + jnp.sum(dpr, axis=0, keepdims=True)
                dbi[n] = dbi[n] + jnp.sum(dpi, axis=0, keepdims=True)
                dpre += [dpr.astype(BF16), dpi.astype(BF16)]
            dpre = jnp.concatenate(dpre, axis=1)
            dwcat_ref[...] += lax.dot_general(xcb, dpre, tn_dims, preferred_element_type=F32)
            dxc = dxc + lax.dot_general(dpre, wcat_ref[...], nt_dims, preferred_element_type=F32)
            dxc_pad[pl.ds(hal + r0, t), :] = dxc
            dcb = dcb + jnp.sum(dxc, axis=0, keepdims=True)
            return tuple(dba), tuple(dbi), tuple(dlam), dcb

        zr = jnp.zeros((1, lh), F32)
        def chunk3_pair(cj, carry):
            return chunk3(2 * cj + 1, chunk3(2 * cj, carry))

        dba, dbi, dlam, dcb = lax.fori_loop(0, n_chunks // 2, chunk3_pair, ((zr, zr), (zr, zr), (zr, zr), zr))
        dcb_ref[...] = dcb
        for n in range(2):
            dpk_ref[pl.ds(4 + n, 1), :] = dba[n]
            dpk_ref[pl.ds(6 + n, 1), :] = dbi[n]
            dpk_ref[pl.ds(8 + n, 1), :] = dlam[n] * (RG_C * jax.nn.sigmoid(-lam[n]))
        dpk_ref[pl.ds(10, SMALL_ROWS - 10), :] = jnp.zeros((SMALL_ROWS - 10, lh), F32)

        def chunk4(ci, dtap):
            r0 = pl.multiple_of(ci * t, t)
            gext = dxc_pad[pl.ds(r0, t + 2 * hal), :]
            uext = upad[pl.ds(r0, t + 2 * hal), :]
            gmid = gext[hal:hal + t, :]
            du = pk_ref[pl.ds(1, 1), :] * gext
            for k in (0, 2, 3):
                du = du + pk_ref[pl.ds(k, 1), :] * _shift(gext, 1 - k)
            dpr_ref[0, pl.ds(r0, t), :] = du[hal:hal + t, :].astype(BF16)
            out = []
            for k in range(4):
                usl = _shift(uext, k - 1)[hal:hal + t, :]
                out.append(dtap[k] + jnp.sum(gmid * usl, axis=0, keepdims=True))
            return tuple(out)

        dtap = lax.fori_loop(0, n_chunks, chunk4, (zr, zr, zr, zr))
        for k in range(4):
            dpk_ref[pl.ds(k, 1), :] = dtap[k]

        head = pl.program_id(0)
        outs = [pltpu.make_async_copy(
            dpr_ref.at[b], dproj_ref.at[:, pl.ds(pl.multiple_of((1 + b) * dl + head * lh, lh), lh)], out_sems.at[b])
            for b in range(2)]
        for cp in outs:
            cp.start()
        for cp in outs:
            cp.wait()

    return pl.pallas_call(
        body, name="lru_bwd", grid=(N_HEADS,),
        in_specs=[_bs((None, seq, lh), lambda h: (1, 0, h)), _bs((None, seq, lh), lambda h: (2, 0, h)),
                  _bs((seq, lh), lambda h: (0, first_rec_block + h)),
                  _bs((seq + 2 * hal, lh), lambda h: (0, h)), _bs((seq + 2 * hal, lh), lambda h: (0, h)),
                  _bs((None, SMALL_ROWS, lh), lambda h: (h, 0, 0)), _bs((1, lh), lambda h: (0, h)),
                  _bs((None, lh, 4 * lh), lambda h: (h, 0, 0)),
                  _bs((8, 128), lambda h: (0, 0)), ANY],
        out_specs=[ANY, _bs((None, SMALL_ROWS, lh), lambda h: (h, 0, 0)),
                   _bs((1, lh), lambda h: (0, h)), _bs((None, lh, 4 * lh), lambda h: (h, 0, 0))],
        out_shape=[jax.ShapeDtypeStruct((seq, 3 * dl), BF16), jax.ShapeDtypeStruct((N_HEADS, SMALL_ROWS, lh), F32),
                   jax.ShapeDtypeStruct((1, dl), F32), jax.ShapeDtypeStruct((N_HEADS, lh, 4 * lh), F32)],
        scratch_shapes=[pltpu.VMEM((seq + 2 * hal, lh), F32), pltpu.VMEM((2, seq + 2 * hal, lh), F32),
                        pltpu.VMEM((seq, lh), F32), pltpu.VMEM((2, seq, lh), F32),
                        pltpu.VMEM((seq + 2 * hal, lh), F32), pltpu.VMEM((2, seq, lh), BF16),
                        pltpu.SemaphoreType.DMA((2,)), pltpu.VMEM((6, seq, lh), F32)],
        input_output_aliases={9: 0},
        compiler_params=_params(1),
    )(p3, p3, dy, h0p, h1p, pack, conv_b, wcat, token, dproj_in)


class _tiles:
    def __init__(self, seq, d_model, d_ff):
        self.rows = min(1024, seq)
        self.ln_rows = min(256, seq)
        self.ff_cols = min(2048, d_ff)
        self.ff_split = 4
        self.ff_k = min(2048, d_ff)
        self.grad_rows = 512


def _ln_loss_bwd(ffn, x1, tgt, g, b, tr):
    seq, d = ffn.shape

    def body(f_ref, x_ref, t_ref, g_ref, b_ref, dz_ref, dzb_ref, dg_ref, db_ref, loss_ref):
        i = pl.program_id(0)
        gv = g_ref[...]
        z = ALPHA * x_ref[...] + f_ref[...]
        y, xhat, rstd = _ln_fwd(z, gv, b_ref[...])
        err = y - t_ref[...]
        part = 0.5 * jnp.sum(jnp.mean(err * err, axis=-1, keepdims=True), axis=0, keepdims=True)
        dz, dg, db = _ln_bwd(err * (1.0 / d), xhat, rstd, gv)
        dz_ref[...] = dz
        dzb_ref[...] = dz.astype(BF16)
        _acc_rows(dg_ref, i == 0, dg)
        _acc_rows(db_ref, i == 0, db)
        _acc_rows(loss_ref, i == 0, jnp.broadcast_to(part, (8, 128)))

    row = _bs((tr, d), lambda i: (i, 0))
    vec = _bs((1, d), lambda i: (0, 0))
    return pl.pallas_call(
        body, name="ln_ffn_loss", grid=(seq // tr,), in_specs=[row, row, row, vec, vec],
        out_specs=[row, row, vec, vec, _bs((8, 128), lambda i: (0, 0))],
        out_shape=[jax.ShapeDtypeStruct((seq, d), F32), jax.ShapeDtypeStruct((seq, d), BF16),
                   jax.ShapeDtypeStruct((1, d), F32), jax.ShapeDtypeStruct((1, d), F32),
                   jax.ShapeDtypeStruct((8, 128), F32)],
        compiler_params=_params(1),
    )(ffn, x1, tgt, g, b)


def _ln_bwd_side(dx_branch, dres, z, g, b, n_steps):
    seq, d = z.shape
    tr = seq // n_steps

    def fn(step, ins, outs):
        a_ref, r_ref, z_ref, g_ref, b_ref = ins
        dz_ref, dzb_ref, dg_ref, db_ref = outs
        gv = g_ref[...]
        _, xhat, rstd = _ln_fwd(z_ref[...], gv, b_ref[...])
        dz, dg, db = _ln_bwd(ALPHA * r_ref[...] + a_ref[...], xhat, rstd, gv)
        dz_ref[...] = dz
        dzb_ref[...] = dz.astype(BF16)
        _acc_rows(dg_ref, step == 0, dg)
        _acc_rows(db_ref, step == 0, db)

    row = ((tr, d), lambda s: (s, 0))
    vec = ((1, d), lambda s: (0, 0))
    shapes = [jax.ShapeDtypeStruct((seq, d), F32), jax.ShapeDtypeStruct((seq, d), BF16),
              jax.ShapeDtypeStruct((1, d), F32), jax.ShapeDtypeStruct((1, d), F32)]
    return [(dx_branch, *row), (dres, *row), (z, *row), (g, *vec), (b, *vec)], shapes, [row, row, vec, vec], fn


def _to_bf16(name, a, token):
    rows, cols = a.shape
    tr = min(512, rows)

    def body(a_ref, tok_ref, o_ref):
        del tok_ref
        o_ref[...] = a_ref[...].astype(BF16)

    return pl.pallas_call(
        body, name=name, grid=(rows // tr,),
        in_specs=[_bs((tr, cols), lambda i: (i, 0)), _bs((8, 128), lambda i: (0, 0))],
        out_specs=_bs((tr, cols), lambda i: (i, 0)), out_shape=jax.ShapeDtypeStruct((rows, cols), BF16),
        compiler_params=_params(1),
    )(a, token)


def _sum_blocks(name, parts):
    def body(p_ref, o_ref):
        acc = p_ref[0]
        for s in range(1, parts.shape[0]):
            acc = acc + p_ref[s]
        o_ref[...] = acc

    return pl.pallas_call(body, name=name, out_shape=jax.ShapeDtypeStruct(parts.shape[1:], F32))(parts)


def _adamw_values(w, g, m, v):
    m = ADAM_B1 * m + (1.0 - ADAM_B1) * g
    v = ADAM_B2 * v + (1.0 - ADAM_B2) * (g * g)
    m_hat = m / (1.0 - ADAM_B1 ** ADAM_STEP)
    v_hat = v / (1.0 - ADAM_B2 ** ADAM_STEP)
    delta = -ADAM_LR * (m_hat / (jnp.sqrt(v_hat) + ADAM_EPS) + ADAM_WD * w)
    return delta, m, v


def _adamw_side(own, parts, w, m, v, n_steps):
    rows, cols = w.shape
    tr = rows // n_steps

    def fn(step, ins, outs):
        o_ref, p_ref, w_ref, m_ref, v_ref = ins
        g = o_ref[...]
        for s in range(parts.shape[0]):
            g = g + p_ref[s].astype(F32)
        delta, mn, vn = _adamw_values(w_ref[...], g, m_ref[...], v_ref[...])
        for ref, val in zip(outs, (g, delta, mn, vn)):
            ref[...] = val

    row = ((tr, cols), lambda s: (s, 0))
    stack = ((parts.shape[0], tr, cols), lambda s: (0, s, 0))
    shapes = [jax.ShapeDtypeStruct((rows, cols), F32)] * 4
    return [(own, *row), (parts, *stack), (w, *row), (m, *row), (v, *row)], shapes, [row] * 4, fn


def _sum_adamw(name, own, parts, w, m, v):
    rows, cols = w.shape
    n_parts = parts.shape[0]
    tr = rows
    min_rows = 8 if parts.dtype == F32 else 16
    while tr * cols * 4 > 2 * 1024 * 1024 and tr % (2 * min_rows) == 0:
        tr //= 2

    def body(*refs):
        if own is None:
            p_ref, w_ref, m_ref, v_ref, g_ref, d_ref, mo_ref, vo_ref = refs
            g = p_ref[0].astype(F32)
            rest = range(1, n_parts)
        else:
            o_ref, p_ref, w_ref, m_ref, v_ref, g_ref, d_ref, mo_ref, vo_ref = refs
            g = o_ref[...]
            rest = range(n_parts)
        for s in rest:
            g = g + p_ref[s].astype(F32)
        delta, mn, vn = _adamw_values(w_ref[...], g, m_ref[...], v_ref[...])
        g_ref[...] = g
        d_ref[...] = delta
        mo_ref[...] = mn
        vo_ref[...] = vn

    spec = _bs((tr, cols), lambda i: (i, 0))
    lead = [] if own is None else [own]
    return pl.pallas_call(
        body, name=name, grid=(rows // tr,),
        in_specs=[spec] * len(lead) + [_bs((n_parts, tr, cols), lambda i: (0, i, 0)), spec, spec, spec],
        out_specs=[spec] * 4, out_shape=[jax.ShapeDtypeStruct((rows, cols), F32)] * 4,
        compiler_params=_params(1),
    )(*lead, parts, w, m, v)


def _rows128(a):
    return a.reshape(-1, 128)


def kernel(x, ln_mix_g, ln_mix_b, w_in, w_pool, pool_scale, conv_w, conv_b, w_rg_a, b_rg_a, w_rg_i, b_rg_i, rg_lambda, w_out, ln_ffn_g, ln_ffn_b, w_mlp_in, w_mlp_out, loss_target, m_ln_mix_g, m_ln_mix_b, m_w_in, m_w_pool, m_pool_scale, m_conv_w, m_conv_b, m_w_rg_a, m_b_rg_a, m_w_rg_i, m_b_rg_i, m_rg_lambda, m_w_out, m_ln_ffn_g, m_ln_ffn_b, m_w_mlp_in, m_w_mlp_out, v_ln_mix_g, v_ln_mix_b, v_w_in, v_w_pool, v_pool_scale, v_conv_w, v_conv_b, v_w_rg_a, v_b_rg_a, v_w_rg_i, v_b_rg_i, v_rg_lambda, v_w_out, v_ln_ffn_g, v_ln_ffn_b, v_w_mlp_in, v_w_mlp_out):
    seq, d_model = x.shape[1], x.shape[2]
    dh = d_model // 2
    lh = dh // N_HEADS
    pg = dh // len(POOL_WINDOWS)
    d_ff = w_mlp_in.shape[2] * N_DEV
    assert lh == 128 and conv_w.shape[3] == lh and w_pool.shape[2] * N_DEV == pg

    xs = x[0]
    tgt = loss_target[0]

    def small_pack(cw, ba, bi, lam):
        return jnp.concatenate([cw.reshape(4, lh), ba.reshape(2, lh), bi.reshape(2, lh), lam.reshape(2, lh),
                                jnp.zeros((SMALL_ROWS - 10, lh), F32)], axis=0)

    pack_mine = small_pack(conv_w, b_rg_a, b_rg_i, rg_lambda)
    pack_bits = lax.bitcast_convert_type(pack_mine, BF16).reshape(1, SMALL_ROWS, 2 * lh)
    win_gather = _SplitGather("gather_w_in", [(w_in[0], 1), (w_pool[0], 1), (pack_bits, 0)], BF16, after=pack_mine)
    wout_gather = _SplitGather("gather_w_out", [(w_out[0], 0)], BF16, after=win_gather.token)
    w1_gather = _SplitGather("gather_w_mlp_in", [(w_mlp_in[0], 1)], BF16, after=wout_gather.token)
    w2_gather = _SplitGather("gather_w_mlp_out", [(w_mlp_out[0], 0)], BF16, after=w1_gather.token)
    xb = _to_bf16("x_bf16", x[0], w2_gather.token)
    win_full, wpool_full, pack_bits_full = win_gather.wait(after=win_gather.relay(after=xb))
    pack_full = lax.bitcast_convert_type(pack_bits_full.reshape(N_DEV, SMALL_ROWS, lh, 2), F32)
    wcat = jnp.concatenate([w_rg_a[0, 0], w_rg_i[0, 0], w_rg_a[0, 1], w_rg_i[0, 1]], axis=-1).astype(BF16)
    vec = lambda i, j, k: (0, 0)
    row_full = lambda i, j, k: (i, 0)

    def after(token):
        return (token, _sp((8, 128), vec))

    def sds(shape, dtype):
        return jax.ShapeDtypeStruct(shape, dtype)

    def plain_epi(acc, i, ex, out):
        out[0][...] = acc

    def bf16_epi(acc, i, ex, out):
        out[0][...] = acc.astype(BF16)

    t = _tiles(seq, d_model, d_ff)

    (p3,) = _matmul(
        "proj", xb, win_full, _sp((t.rows, d_model), lambda i, j, k: (i, 0)), _sp((d_model, dh), lambda i, j, k: (0, j)),
        grid=(seq // t.rows, 3, 1),
        out_shape=[sds((3, seq, dh), F32)], out_specs=[_sp((None, t.rows, dh), lambda i, j, k: (j, i, 0))],
        epilogue=plain_epi)

    d_pool, y_half = _pool_fwd(p3, wpool_full, pool_scale, seq, d_model)
    y, h0p, h1p = _lru_fwd(p3, y_half, pack_full, conv_b, wcat, wout_gather.relay(after=y_half), seq, d_model)
    (wout_full,) = wout_gather.wait(after=y)
    relay_token = w1_gather.relay(after=wout_full)

    mix_rows = 2 * t.ln_rows

    def mix_epi(acc, i, ex, out):
        x_ref, g_ref, b_ref = ex[:3]
        for part in range(2):
            rows = pl.ds(part * t.ln_rows, t.ln_rows)
            z = ALPHA * x_ref[rows, :] + acc[part * t.ln_rows:(part + 1) * t.ln_rows, :]
            x1, _, _ = _ln_fwd(z, g_ref[...], b_ref[...])
            out[0][rows, :] = z
            out[1][rows, :] = x1
            out[2][rows, :] = x1.astype(BF16)

    z1, x1, x1b = _matmul(
        "mix_out", y, wout_full, _sp((mix_rows, d_model), row_full), _sp((d_model, d_model), vec, single=True),
        grid=(seq // mix_rows, 1, 1),
        extras=[(xs, _sp((mix_rows, d_model), row_full)), (ln_mix_g, _sp((1, d_model), vec)),
                (ln_mix_b, _sp((1, d_model), vec)), after(relay_token)],
        out_shape=[sds((seq, d_model), F32), sds((seq, d_model), F32), sds((seq, d_model), BF16)],
        out_specs=[_sp((mix_rows, d_model), row_full)] * 3, epilogue=mix_epi)
    (w1_full,) = w1_gather.wait(after=x1b)

    def mlp_in_epi(acc, i, ex, out, cols):
        h = jnp.maximum(acc, 0.0)
        out[0][:, cols] = (h * h).astype(BF16)
        out[1][:, cols] = (2.0 * h).astype(BF16)

    hmid, dact = _matmul(
        "mlp_in", x1b, w1_full, _sp((t.rows, d_model), lambda i, j, k: (i, 0)),
        _sp((d_model, t.ff_cols), lambda i, j, k: (0, j)),
        grid=(seq // t.rows, d_ff // t.ff_cols, 1), j_outer=True,
        out_shape=[sds((seq, d_ff), BF16)] * 2, out_specs=[_sp((t.rows, t.ff_cols), lambda i, j, k: (i, j))] * 2,
        epilogue=mlp_in_epi, n_split=t.ff_split)
    (w2_full,) = w2_gather.wait(after=w2_gather.relay(after=hmid))

    (ffn,) = _matmul(
        "mlp_out", hmid, w2_full, _sp((t.rows, t.ff_k), lambda i, j, k: (i, k)),
        _sp((t.ff_k, d_model), lambda i, j, k: (k, 0)),
        grid=(seq // t.rows, 1, d_ff // t.ff_k),
        out_shape=[sds((seq, d_model), F32)], out_specs=[_sp((t.rows, d_model), row_full)])
    dz2, dz2b, g_ffn_g, g_ffn_b, loss_part = _ln_loss_bwd(ffn, x1, tgt, ln_ffn_g, ln_ffn_b, t.ln_rows)

    (g_w2,) = _matmul(
        "grad_w_mlp_out", hmid, dz2b, _sp((seq, t.grad_rows), lambda i, j, k: (0, i)),
        _sp((seq, d_model), vec, single=True),
        grid=(d_ff // t.grad_rows, 1, 1), ta=True,
        out_shape=[sds((d_ff, d_model), BF16)], out_specs=[_sp((t.grad_rows, d_model), row_full)],
        epilogue=bf16_epi)
    scatter_w2 = _SplitReduceScatter("scatter_w_mlp_out", [g_w2.reshape(N_DEV, d_ff // N_DEV, d_model)])

    def dpre_epi(acc, i, ex, out, cols):
        out[0][:, cols] = (acc * ex[0][:, cols].astype(F32)).astype(BF16)

    (dpre,) = _matmul(
        "mlp_dpre", dz2b, w2_full, _sp((t.rows, d_model), lambda i, j, k: (i, 0)),
        _sp((t.ff_cols, d_model), lambda i, j, k: (j, 0)),
        grid=(seq // t.rows, d_ff // t.ff_cols, 1), j_outer=True, tb=True,
        extras=[(dact, _sp((t.rows, t.ff_cols), lambda i, j, k: (i, j))), after(scatter_w2.token)],
        out_shape=[sds((seq, d_ff), BF16)], out_specs=[_sp((t.rows, t.ff_cols), lambda i, j, k: (i, j))],
        epilogue=dpre_epi, n_split=t.ff_split)
    token_w2 = scatter_w2.combine_and_send(after=dpre)

    (dx1_mlp,) = _matmul(
        "mlp_dx", dpre, w1_full, _sp((t.rows, t.ff_k), lambda i, j, k: (i, k)),
        _sp((d_model, t.ff_k), lambda i, j, k: (0, k)),
        grid=(seq // t.rows, 1, d_ff // t.ff_k), tb=True, extras=[after(token_w2)],
        out_shape=[sds((seq, d_model), F32)], out_specs=[_sp((t.rows, d_model), row_full)])
    def block_epi(acc, i, ex, out):
        out[0][0] = acc.astype(BF16)

    fs = d_ff // N_DEV
    g_w1, dz1, dz1b, g_mix_g, g_mix_b = _matmul(
        "grad_w_mlp_in", x1b, dpre, _sp((seq, t.grad_rows), lambda i, j, k: (0, i)),
        _sp((seq, fs), lambda i, j, k: (0, j)),
        grid=(d_model // t.grad_rows, N_DEV, 1), j_outer=True, ta=True,
        out_shape=[sds((N_DEV, d_model, fs), BF16)],
        out_specs=[_sp((1, t.grad_rows, fs), lambda i, j, k: (j, i, 0))], epilogue=block_epi,
        side=_ln_bwd_side(dx1_mlp, dz2, z1, ln_mix_g, ln_mix_b, d_model // t.grad_rows * N_DEV))

    (dy,) = _matmul(
        "mix_dy", dz1b, wout_full, _sp((t.rows, d_model), lambda i, j, k: (i, 0)),
        _sp((dh, d_model), lambda i, j, k: (j, 0)),
        grid=(seq // t.rows, 2, 1), j_outer=True, tb=True,
        out_shape=[sds((seq, d_model), F32)], out_specs=[_sp((t.rows, dh), lambda i, j, k: (i, j))],
        epilogue=plain_epi)
    (g_wout,) = _matmul(
        "grad_w_out", y, dz1b, _sp((seq, t.grad_rows), lambda i, j, k: (0, i)), _sp((seq, d_model), vec, single=True),
        grid=(d_model // t.grad_rows, 1, 1), ta=True,
        out_shape=[sds((d_model, d_model), BF16)], out_specs=[_sp((t.grad_rows, d_model), row_full)],
        epilogue=bf16_epi)
    scatter_w1 = _SplitReduceScatter("scatter_w_mlp_in", [g_w1, g_wout.reshape(N_DEV, d_model // N_DEV, d_model)])

    dproj_pool, g_wpool, g_pscale = _pool_bwd(d_pool, dy, wpool_full, pool_scale, scatter_w1.token, seq, d_model)
    token_w1 = scatter_w1.combine_and_send(after=dproj_pool)
    dproj, g_pack, g_convb, g_wcat = _lru_bwd(p3, dy, h0p, h1p, dproj_pool, pack_full, conv_b, wcat,
                                              token_w1, seq, d_model)
    g_wa = jnp.stack([g_wcat[:, :, 0:lh], g_wcat[:, :, 2 * lh:3 * lh]])
    g_wi = jnp.stack([g_wcat[:, :, lh:2 * lh], g_wcat[:, :, 3 * lh:4 * lh]])

    rep_parts = [_rows128(g_wa), _rows128(g_wi), _rows128(g_mix_g), _rows128(g_mix_b), _rows128(g_ffn_g),
                 _rows128(g_ffn_b), _rows128(g_pscale), _rows128(g_convb)]
    rep_rows = [p.shape[0] for p in rep_parts]
    n_rep = sum(rep_rows)
    small = jnp.concatenate(rep_parts + [_rows128(g_pack), loss_part], axis=0)
    small_gather = _SplitGather("gather_small_grads", [(small[None], 0)], F32, after=small)

    ws = 3 * dh // N_DEV

    def pair_epi(acc, i, ex, out):
        out[0][0] = acc[:, :ws].astype(BF16)
        out[0][1] = acc[:, ws:].astype(BF16)

    def adam_big(name, own_landed, w, m, v):
        own, landed = own_landed
        shp = w.shape
        two = lambda a: a.reshape(-1, shp[-1])
        res = _sum_adamw(name, own, landed, two(w), two(m), two(v))
        return [r.reshape(shp) for r in res]

    (r_w2,) = scatter_w2.wait(after=small_gather.token)
    n_steps = d_model // t.grad_rows * (N_DEV // 2)
    g_win, *o_w2 = _matmul(
        "grad_w_in", xb, dproj, _sp((seq, t.grad_rows), lambda i, j, k: (0, i)),
        _sp((seq, 2 * ws), lambda i, j, k: (0, j)),
        grid=(d_model // t.grad_rows, N_DEV // 2, 1), ta=True,
        out_shape=[sds((N_DEV, d_model, ws), BF16)],
        out_specs=[_sp((2, t.grad_rows, ws), lambda i, j, k: (j, i, 0))], epilogue=pair_epi,
        side=_adamw_side(r_w2[0], r_w2[1], w_mlp_out[0], m_w_mlp_out[0], v_w_mlp_out[0], n_steps))
    o_w2 = [r.reshape(w_mlp_out.shape) for r in o_w2]
    scatter_mix = _SplitReduceScatter(
        "scatter_mixer", [g_win, g_wpool.reshape(N_DEV, pg // N_DEV * len(POOL_WINDOWS), pg)])

    r_w1, r_wout = scatter_w1.wait(after=scatter_mix.token)
    o_w1 = adam_big("adam_w_mlp_in", r_w1, w_mlp_in, m_w_mlp_in, v_w_mlp_in)
    token_mix = scatter_mix.combine_and_send(after=o_w1[0])

    def dx_epi(acc, i, ex, out):
        out[0][...] = ALPHA * ex[0][...] + acc

    dx_rows = t.ln_rows * 2
    dx, *o_wout = _matmul(
        "grad_x", dproj, win_full, _sp((dx_rows, 3 * dh), lambda i, j, k: (i, 0)),
        _sp((d_model, 3 * dh), vec, single=True),
        grid=(seq // dx_rows, 1, 1), tb=True,
        extras=[(dz1, _sp((dx_rows, d_model), row_full)), after(token_mix)],
        out_shape=[sds((seq, d_model), F32)], out_specs=[_sp((dx_rows, d_model), row_full)],
        epilogue=dx_epi,
        side=_adamw_side(r_wout[0], r_wout[1], w_out[0], m_w_out[0], v_w_out[0], seq // dx_rows))
    o_wout = [r.reshape(w_out.shape) for r in o_wout]
    r_win, r_wpool = scatter_mix.wait(after=dx)
    o_win = adam_big("adam_w_in", r_win, w_in, m_w_in, v_w_in)
    o_wpool = adam_big("adam_w_pool", r_wpool, w_pool, m_w_pool, v_w_pool)

    small_gather.relay(after=o_win[0])
    (small_all,) = small_gather.wait(after=o_wpool[0])

    rep_w = [w_rg_a, w_rg_i, ln_mix_g, ln_mix_b, ln_ffn_g, ln_ffn_b, pool_scale, conv_b]
    rep_m = [m_w_rg_a, m_w_rg_i, m_ln_mix_g, m_ln_mix_b, m_ln_ffn_g, m_ln_ffn_b, m_pool_scale, m_conv_b]
    rep_v = [v_w_rg_a, v_w_rg_i, v_ln_mix_g, v_ln_mix_b, v_ln_ffn_g, v_ln_ffn_b, v_pool_scale, v_conv_b]
    cat = lambda arrs: jnp.concatenate([_rows128(a) for a in arrs], axis=0)
    o_rep = _sum_adamw("adam_replicated", None, small_all, cat(rep_w), cat(rep_m), cat(rep_v))

    my_idx = _dev_index(_where_am_i())
    head_parts = lax.dynamic_slice_in_dim(small_all, n_rep + my_idx * SMALL_ROWS, SMALL_ROWS, axis=1)
    o_head = _sum_adamw("adam_head", None, head_parts, pack_mine,
                        small_pack(m_conv_w, m_b_rg_a, m_b_rg_i, m_rg_lambda),
                        small_pack(v_conv_w, v_b_rg_a, v_b_rg_i, v_rg_lambda))

    def unpack_rep(packed):
        out, r = [], 0
        for wgt, rows in zip(rep_w, rep_rows):
            out.append(packed[r:r + rows].reshape(wgt.shape))
            r += rows
        return out

    def unpack_head(packed):
        return [packed[0:4].reshape(conv_w.shape), packed[4:6].reshape(b_rg_a.shape),
                packed[6:8].reshape(b_rg_i.shape), packed[8:10].reshape(rg_lambda.shape)]

    loss = _sum_blocks("loss_sum", small_all[:, n_rep + N_HEADS * SMALL_ROWS:, :])[0, 0]

    outs = [loss, dx[None]]
    for kind in range(4):
        ra, ri, mg, mb, fg, fb, ps, cb = unpack_rep(o_rep[kind])
        cw, ba, bi, lam = unpack_head(o_head[kind])
        outs += [mg, mb, o_win[kind], o_wpool[kind], ps, cw, cb, ra, ba, ri, bi, lam, o_wout[kind], fg, fb,
                 o_w1[kind], o_w2[kind]]
    return tuple(outs)
```

```python
import functools

import jax
import jax.numpy as jnp
from jax import lax
from jax.experimental import pallas as pl
from jax.experimental.pallas import tpu as pltpu

F32 = jnp.float32
BF16 = jnp.bfloat16
MESH = pl.DeviceIdType.MESH
ANY = pl.BlockSpec(memory_space=pl.ANY)

N_DEV = 8
POOL_WINDOWS = (2, 4, 8, 16)
N_HEADS = 8
RG_C = 8.0
LN_EPS = 1e-5
ALPHA = 2.0 ** 0.25
ADAM_LR = 0.001
ADAM_B1 = 0.9
ADAM_B2 = 0.999
ADAM_EPS = 1e-08
ADAM_WD = 0.01
ADAM_STEP = 10

VMEM_LIMIT = 56 * 1024 * 1024
SEQ_CHUNK = 256
WIN_HALO = 16
CONV_HALO = 8
SMALL_ROWS = 16


def _params(n_grid):
    return pltpu.CompilerParams(dimension_semantics=("arbitrary",) * n_grid, vmem_limit_bytes=VMEM_LIMIT)


def _shift(v, j):
    n = v.shape[0]
    s = (-j) % n
    return v if s == 0 else pltpu.roll(v, s, 0)


def _sigmoid(x):
    return 0.5 * jnp.tanh(0.5 * x) + 0.5


def _softplus(z):
    e = jnp.exp(-jnp.abs(z))
    u = 1.0 + e
    log1p = jnp.where(u == 1.0, e, jnp.log(u) * (e / jnp.where(u == 1.0, 1.0, u - 1.0)))
    return jnp.maximum(z, 0.0) + log1p


_GELU_C = 0.7978845608028654
_GELU_K = 0.044715


def _gelu_and_grad(x):
    x2 = x * x
    t = jnp.tanh(_GELU_C * (x + _GELU_K * x * x2))
    g = 0.5 * x * (1.0 + t)
    dg = 0.5 * (1.0 + t) + 0.5 * x * (1.0 - t * t) * (_GELU_C * (1.0 + 3.0 * _GELU_K * x2))
    return g, dg


def _ln_fwd(z, g, b):
    mu = jnp.mean(z, axis=-1, keepdims=True)
    zc = z - mu
    var = jnp.mean(zc * zc, axis=-1, keepdims=True)
    rstd = lax.rsqrt(var + LN_EPS)
    xhat = zc * rstd
    return xhat * g + b, xhat, rstd


def _ln_bwd(dy, xhat, rstd, g):
    dxhat = dy * g
    m1 = jnp.mean(dxhat, axis=-1, keepdims=True)
    m2 = jnp.mean(dxhat * xhat, axis=-1, keepdims=True)
    dz = rstd * (dxhat - m1 - xhat * m2)
    dg = jnp.sum(dy * xhat, axis=0, keepdims=True)
    db = jnp.sum(dy, axis=0, keepdims=True)
    return dz, dg, db


def _acc_rows(ref, first, val):
    @pl.when(first)
    def _():
        ref[...] = val

    @pl.when(jnp.logical_not(first))
    def _():
        ref[...] += val


def _sp(shape, fn, single=False):
    return shape, fn, single


def _matmul(name, a, b, a_spec, b_spec, *, grid, j_outer=False, ta=False, tb=False, extras=(), out_shape, out_specs,
            epilogue=None, n_split=1, side=None):
    ni, nj, nk = grid
    n_ex = len(extras)
    dims = (((0 if ta else 1,), (1 if tb else 0,)), ((), ()))
    side_in, side_shape, side_out, side_fn = side if side is not None else ((), (), (), None)
    n_main_out = len(out_shape)
    inner = ni if j_outer else nj

    def mk(spec):
        shape, fn, single = spec
        index = (lambda g0, g1, g2: fn(g1, g0, g2)) if j_outer else fn
        return pl.BlockSpec(shape, index, pipeline_mode=pl.Buffered(1)) if single else pl.BlockSpec(shape, index)

    def mk_side(block, fn):
        return pl.BlockSpec(block, lambda g0, g1, g2: fn(g0 * inner + g1))

    def body(a_ref, b_ref, *rest):
        ex_refs = rest[:n_ex]
        out_refs = rest[n_ex + len(side_in):n_ex + len(side_in) + n_main_out]
        if side_fn is not None:
            side_fn(pl.program_id(0) * inner + pl.program_id(1), rest[n_ex:n_ex + len(side_in)],
                    rest[n_ex + len(side_in) + n_main_out:])
        i = pl.program_id(1 if j_outer else 0)
        if n_split > 1:
            av = a_ref[...].astype(BF16)
            width = b_ref.shape[0 if tb else 1] // n_split
            for c in range(n_split):
                cols = pl.ds(c * width, width)
                bv = (b_ref[cols, :] if tb else b_ref[:, cols]).astype(BF16)
                epilogue(lax.dot_general(av, bv, dims, preferred_element_type=F32), i, ex_refs, out_refs, cols)
            return
        part = lax.dot_general(a_ref[...].astype(BF16), b_ref[...].astype(BF16), dims, preferred_element_type=F32)
        if nk == 1:
            epilogue(part, i, ex_refs, out_refs)
        else:
            @pl.when(pl.program_id(2) == 0)
            def _():
                out_refs[0][...] = part

            @pl.when(pl.program_id(2) > 0)
            def _():
                out_refs[0][...] += part

    return pl.pallas_call(
        body, name=name, grid=(nj, ni, nk) if j_outer else (ni, nj, nk),
        in_specs=[mk(a_spec), mk(b_spec)] + [mk(s) for _, s in extras] + [mk_side(blk, fn) for _, blk, fn in side_in],
        out_specs=[mk(s) for s in out_specs] + [mk_side(blk, fn) for blk, fn in side_out],
        out_shape=list(out_shape) + list(side_shape),
        compiler_params=_params(3),
    )(a, b, *[x for x, _ in extras], *[x for x, _, _ in side_in])


def _bs(shape, fn):
    return pl.BlockSpec(shape, fn)


def _where_am_i():
    x, y, c = lax.axis_index("x"), lax.axis_index("y"), lax.axis_index("c")
    return x, y, c


def _dev_index(p):
    return 4 * p[0] + 2 * p[1] + p[2]


def _slab(ref, axis, idx, size):
    sl = [slice(None)] * len(ref.shape)
    sl[axis] = pl.ds(idx * size, size)
    return ref.at[tuple(sl)]


HBM = pl.BlockSpec(memory_space=pltpu.HBM)
SEM = pl.BlockSpec(memory_space=pltpu.SEMAPHORE)
DATAFLOW = pltpu.SideEffectType.DATAFLOW_SIDE_EFFECTING


def _in_hbm(a):
    return pltpu.with_memory_space_constraint(a, pltpu.HBM)


def _token_shape():
    return jax.ShapeDtypeStruct((8, 128), F32)


def _split_start(name, n_sems, bufs, issue):
    nb = len(bufs)

    def body(*refs):
        issue(refs[:nb], refs[nb], refs[nb + 1])
        refs[-1][...] = jnp.zeros((8, 128), F32)

    outs = pl.pallas_call(
        body, name=name,
        out_shape=(pltpu.SemaphoreType.DMA((n_sems,)), pltpu.SemaphoreType.DMA((n_sems,)),
                   *[pltpu.HBM(b.shape, b.dtype) for b in bufs], _token_shape()),
        in_specs=[HBM] * nb, out_specs=(SEM, SEM, *[HBM] * nb, pl.BlockSpec(memory_space=pltpu.VMEM)),
        input_output_aliases={i: 2 + i for i in range(nb)},
        compiler_params=pltpu.CompilerParams(has_side_effects=DATAFLOW),
    )(*[_in_hbm(b) for b in bufs])
    return outs[0], outs[1], list(outs[2:2 + nb]), outs[-1]


def _split_relay(name, n_sems, sems, bufs, after, relay):
    nb = len(bufs)

    def body(*refs):
        relay(refs[:nb], refs[nb], refs[nb + 1], refs[nb + 3], refs[nb + 4])
        refs[-1][...] = jnp.zeros((8, 128), F32)

    outs = pl.pallas_call(
        body, name=name,
        out_shape=(pltpu.SemaphoreType.DMA((n_sems,)), pltpu.SemaphoreType.DMA((n_sems,)),
                   *[pltpu.HBM(b.shape, b.dtype) for b in bufs], _token_shape()),
        in_specs=[HBM] * nb + [SEM, SEM, ANY],
        out_specs=(SEM, SEM, *[HBM] * nb, pl.BlockSpec(memory_space=pltpu.VMEM)),
        input_output_aliases={i: 2 + i for i in range(nb)},
        compiler_params=pltpu.CompilerParams(has_side_effects=DATAFLOW),
    )(*bufs, sems[0], sems[1], after)
    return outs[0], outs[1], list(outs[2:2 + nb]), outs[-1]


def _split_wait(name, sems, bufs, after, finish):
    nb = len(bufs)

    def body(*refs):
        finish(refs[:nb], refs[nb], refs[nb + 1])

    outs = pl.pallas_call(
        body, name=name, out_shape=[pltpu.HBM(b.shape, b.dtype) for b in bufs],
        in_specs=[HBM] * nb + [SEM, SEM, ANY], out_specs=[HBM] * nb,
        input_output_aliases={i: i for i in range(nb)},
        compiler_params=pltpu.CompilerParams(has_side_effects=DATAFLOW),
    )(*bufs, sems[0], sems[1], after)
    return list(outs)


def _place(name, items, dtype, after):
    ids = jnp.reshape(_dev_index(_where_am_i()), (1,)).astype(jnp.int32)
    outs = []
    for a, (shard, axis) in enumerate(items):
        rows, cols = shard.shape[-2], shard.shape[-1]
        tr = rows
        while tr * cols * shard.dtype.itemsize > 4 * 1024 * 1024 and tr % 32 == 0:
            tr //= 2
        nt = rows // tr
        full = list(shard.shape)
        full[axis] *= N_DEV
        if shard.ndim == 2 and axis == 0:
            in_spec = _bs((tr, cols), lambda i, ids: (i, 0))
            out_spec = _bs((tr, cols), lambda i, ids, nt=nt: (ids[0] * nt + i, 0))
        elif shard.ndim == 2 and axis == 1:
            in_spec = _bs((tr, cols), lambda i, ids: (i, 0))
            out_spec = _bs((tr, cols), lambda i, ids: (i, ids[0]))
        elif shard.ndim == 3 and axis == 1:
            tr, nt = rows, shard.shape[0]
            in_spec = _bs((None, rows, cols), lambda i, ids: (i, 0, 0))
            out_spec = _bs((None, rows, cols), lambda i, ids: (i, ids[0], 0))
        else:
            assert shard.ndim == 3 and axis == 0 and shard.shape[0] == 1
            in_spec = _bs((None, tr, cols), lambda i, ids: (0, i, 0))
            out_spec = _bs((None, tr, cols), lambda i, ids: (ids[0], i, 0))

        def body(ids_ref, in_ref, after_ref, out_ref):
            del ids_ref, after_ref
            out_ref[...] = in_ref[...].astype(out_ref.dtype)

        outs.append(pl.pallas_call(
            body, name=f"{name}{a}",
            grid_spec=pltpu.PrefetchScalarGridSpec(
                num_scalar_prefetch=1, grid=(nt,), in_specs=[in_spec, ANY], out_specs=out_spec),
            out_shape=jax.ShapeDtypeStruct(tuple(full), dtype), compiler_params=_params(1),
        )(ids, shard, after))
    return outs


class _SplitGather:
    def __init__(self, name, items, dtype, after):
        self.name, self.items, self.n = name, items, len(items)
        fulls = _place(name + "_place", items, dtype, after)
        n = self.n

        def issue(refs, send, recv):
            me, sibling, chips, c = self._geometry()
            for a in range(n):
                self._copy1(refs, send, recv, a, 0, me, sibling).start()
                for j, chip in enumerate(chips):
                    self._copy1(refs, send, recv, a, 1 + j, me, (*chip, c)).start()

        self.send, self.recv, self.bufs, self.token = _split_start(name + "_start", 4 * n, fulls, issue)

    @staticmethod
    def _geometry():
        x, y, c = _where_am_i()
        return (x, y, c), (x, y, 1 - c), [(1 - x, y), (x, 1 - y), (1 - x, 1 - y)], c

    def _blk(self, refs, a, p):
        shard, axis = self.items[a]
        return _slab(refs[a], axis, _dev_index(p), shard.shape[axis])

    def _copy1(self, refs, send, recv, a, k, owner, to):
        return pltpu.make_async_remote_copy(
            src_ref=self._blk(refs, a, owner), dst_ref=self._blk(refs, a, owner), send_sem=send.at[4 * a + k],
            recv_sem=recv.at[4 * a + k], device_id=to, device_id_type=MESH)

    def _copy2(self, refs, send, recv, a, j, owner, to):
        return pltpu.make_async_remote_copy(
            src_ref=self._blk(refs, a, owner), dst_ref=self._blk(refs, a, owner), send_sem=send.at[3 * a + j],
            recv_sem=recv.at[3 * a + j], device_id=to, device_id_type=MESH)

    def relay(self, after):
        n = self.n

        def relay(refs, send_in, recv_in, send_out, recv_out):
            me, sibling, chips, c = self._geometry()
            for a in range(n):
                for j, chip in enumerate(chips):
                    self._copy1(refs, send_in, recv_in, a, 1 + j, (*chip, c), me).wait_recv()
                    self._copy2(refs, send_out, recv_out, a, j, (*chip, c), sibling).start()
            for a in range(n):
                self._copy1(refs, send_in, recv_in, a, 0, sibling, me).wait_recv()
                for k in range(4):
                    self._copy1(refs, send_in, recv_in, a, k, me, sibling).wait_send()

        self.send, self.recv, self.bufs, self.token = _split_relay(
            self.name + "_relay", 3 * n, (self.send, self.recv), self.bufs, after, relay)
        return self.token

    def wait(self, after):
        n = self.n

        def finish(refs, send, recv):
            me, sibling, chips, c = self._geometry()
            for a in range(n):
                for j, chip in enumerate(chips):
                    self._copy2(refs, send, recv, a, j, (*chip, 1 - c), me).wait_recv()
                    self._copy2(refs, send, recv, a, j, (*chip, c), sibling).wait_send()

        return _split_wait(self.name + "_wait", (self.send, self.recv), self.bufs, after, finish)


class _SplitReduceScatter:
    def __init__(self, name, grads):
        self.name, self.n = name, len(grads)
        n = self.n
        g4 = [g.reshape(4, 2, *g.shape[1:]) for g in grads]
        land = [lax.empty((4, 1, *g.shape[1:]), g.dtype) for g in grads]

        def issue(refs, send, recv):
            for a in range(n):
                self._swap(refs, send, recv, a).start()

        self.send, self.recv, self.bufs, self.token = _split_start(name + "_d2d_start", n, g4 + land, issue)

    def _swap(self, refs, send, recv, a):
        x, y, c = _where_am_i()
        return pltpu.make_async_remote_copy(
            src_ref=refs[a].at[:, pl.ds(1 - c, 1)], dst_ref=refs[self.n + a], send_sem=send.at[a], recv_sem=recv.at[a],
            device_id=(x, y, 1 - c), device_id_type=MESH)

    def _hop(self, refs, send, recv, a, m):
        x, y, c = _where_am_i()
        px = (1 - x) if m & 2 else x
        py = (1 - y) if m & 1 else y
        return pltpu.make_async_remote_copy(
            src_ref=refs[a].at[2 * px + py], dst_ref=refs[self.n + a].at[m - 1], send_sem=send.at[3 * a + m - 1],
            recv_sem=recv.at[3 * a + m - 1], device_id=(px, py, c), device_id_type=MESH)

    def combine_and_send(self, after):
        n = self.n

        def finish(refs, send, recv):
            for a in range(n):
                self._swap(refs, send, recv, a).wait()

        bufs = _split_wait(self.name + "_d2d_wait", (self.send, self.recv), self.bufs, after, finish)
        x, y, c = _where_am_i()
        ids = jnp.stack([c, 2 * x + y]).astype(jnp.int32)
        self.own, sums = [], []
        for a in range(n):
            own, hb = _pair_sum(f"{self.name}_sum{a}", bufs[a], bufs[n + a], ids)
            self.own.append(own)
            sums.append(hb)
        land = [lax.empty((3, *h.shape[1:]), h.dtype) for h in sums]

        def issue(refs, send, recv):
            for a in range(n):
                for m in (1, 2, 3):
                    self._hop(refs, send, recv, a, m).start()

        self.send, self.recv, self.bufs, self.token = _split_start(self.name + "_ici_start", 3 * n, sums + land, issue)
        return self.token

    def wait(self, after):
        n = self.n

        def finish(refs, send, recv):
            for a in range(n):
                for m in (1, 2, 3):
                    self._hop(refs, send, recv, a, m).wait()

        bufs = _split_wait(self.name + "_ici_wait", (self.send, self.recv), self.bufs, after, finish)
        return list(zip(self.own, bufs[n:]))


def _pair_sum(name, g4, land, ids):
    rows, cols = g4.shape[2], g4.shape[3]
    tr = rows
    while tr * cols * 2 > 2 * 1024 * 1024 and tr % 32 == 0:
        tr //= 2

    def body(ids_ref, g_ref, l_ref, own_ref, sum_ref):
        h = g_ref[...].astype(F32) + l_ref[...].astype(F32)
        sum_ref[...] = h.astype(sum_ref.dtype)

        @pl.when(pl.program_id(1) == ids_ref[1])
        def _():
            own_ref[...] = h

    return pl.pallas_call(
        body, name=name,
        grid_spec=pltpu.PrefetchScalarGridSpec(
            num_scalar_prefetch=1, grid=(rows // tr, 4),
            in_specs=[_bs((None, None, tr, cols), lambda i, q, ids: (q, ids[0], i, 0)),
                      _bs((None, None, tr, cols), lambda i, q, ids: (q, 0, i, 0))],
            out_specs=[_bs((tr, cols), lambda i, q, ids: (i, 0)), _bs((None, tr, cols), lambda i, q, ids: (q, i, 0))]),
        out_shape=[jax.ShapeDtypeStruct((rows, cols), F32), jax.ShapeDtypeStruct((4, rows, cols), g4.dtype)],
        compiler_params=_params(2),
    )(ids, g4, land)


def _win_sum(ext, w, off):
    s = ext + _shift(ext, -1)
    if w >= 4:
        s = _shift(s, -1) + _shift(s, 1)
    if w >= 8:
        s = _shift(s, -2) + _shift(s, 2)
    if w >= 16:
        s = _shift(s, -4) + _shift(s, 4)
    return _shift(s, off) if off else s


def _inv_count(r0, t, w, seq):
    pos = r0 + lax.broadcasted_iota(jnp.int32, (t, 1), 0)
    cnt = jnp.minimum(pos + w // 2, seq) - jnp.maximum(pos - w // 2, 0)
    return 1.0 / cnt.astype(F32)


def _pool_fwd(p3, w_pool, pool_scale, seq, d_model):
    dp = d_model // 2
    pg = dp // len(POOL_WINDOWS)
    t = min(SEQ_CHUNK, seq)
    n_chunks = seq // t
    h = WIN_HALO

    def body(u_ref, w_ref, sc_ref, d_ref, y_ref, pad_ref):
        g = pl.program_id(0)
        zeros = jnp.zeros((h, pg), F32)
        pad_ref[0:h, :] = zeros
        pad_ref[h + seq:h + seq + h, :] = zeros

        def fill(ci, _):
            r0 = pl.multiple_of(ci * t, t)
            pad_ref[pl.ds(h + r0, t), :] = u_ref[pl.ds(r0, t), :]
            return 0

        lax.fori_loop(0, n_chunks, fill, 0)
        wmat = w_ref[...]
        scale = sc_ref[...]
        for gi, w in enumerate(POOL_WINDOWS):
            @pl.when(g == gi)
            def _(w=w):
                def chunk(ci, _):
                    r0 = pl.multiple_of(ci * t, t)
                    ext = pad_ref[pl.ds(r0, t + 2 * h), :]
                    mean = _win_sum(ext, w, 0)[h:h + t, :] * _inv_count(r0, t, w, seq)
                    d = (mean - ext[h:h + t, :]).astype(BF16)
                    d_ref[pl.ds(r0, t), :] = d
                    q = jnp.dot(d, wmat, preferred_element_type=F32)
                    y_ref[pl.ds(r0, t), :] = (q * scale).astype(BF16)
                    return 0

                lax.fori_loop(0, n_chunks, chunk, 0, unroll=2)

    return pl.pallas_call(
        body, name="pool_fwd", grid=(len(POOL_WINDOWS),),
        in_specs=[_bs((None, seq, pg), lambda g: (0, 0, g)), _bs((None, pg, pg), lambda g: (g, 0, 0)),
                  _bs((1, pg), lambda g: (0, g))],
        out_specs=[_bs((seq, pg), lambda g: (0, g)), _bs((seq, pg), lambda g: (0, g))],
        out_shape=[jax.ShapeDtypeStruct((seq, dp), BF16), jax.ShapeDtypeStruct((seq, d_model), BF16)],
        scratch_shapes=[pltpu.VMEM((seq + 2 * h, pg), F32)],
        compiler_params=_params(1),
    )(p3, w_pool, pool_scale)


def _pool_bwd(d, dy, w_pool, pool_scale, token, seq, d_model):
    dp = d_model // 2
    pg = dp // len(POOL_WINDOWS)
    t = min(SEQ_CHUNK, seq)
    n_chunks = seq // t
    h = WIN_HALO
    tn_dims = (((0,), (0,)), ((), ()))
    nt_dims = (((1,), (1,)), ((), ()))

    def body(d_ref, dy_ref, w_ref, sc_ref, tok_ref, du_ref, dwb_ref, dsc_ref, pad_ref, dd_ref, dw_ref):
        del tok_ref
        g = pl.program_id(0)
        zeros = jnp.zeros((h, pg), F32)
        pad_ref[0:h, :] = zeros
        pad_ref[h + seq:h + seq + h, :] = zeros
        wmat = w_ref[...]
        scale = sc_ref[...]
        for gi, w in enumerate(POOL_WINDOWS):
            @pl.when(g == gi)
            def _(w=w):
                dw_ref[...] = jnp.zeros((pg, pg), F32)

                def first(ci, dsc):
                    r0 = pl.multiple_of(ci * t, t)
                    dv = d_ref[pl.ds(r0, t), :]
                    dyv = dy_ref[pl.ds(r0, t), :]
                    q = jnp.dot(dv, wmat, preferred_element_type=F32)
                    dsc = dsc + jnp.sum(dyv * q, axis=0, keepdims=True)
                    dq = (dyv * scale).astype(BF16)
                    dw_ref[...] += lax.dot_general(dv, dq, tn_dims, preferred_element_type=F32)
                    dd = lax.dot_general(dq, wmat, nt_dims, preferred_element_type=F32)
                    dd_ref[pl.ds(r0, t), :] = dd
                    pad_ref[pl.ds(h + r0, t), :] = dd * _inv_count(r0, t, w, seq)
                    return dsc

                def first_pair(cj, dsc):
                    return first(2 * cj + 1, first(2 * cj, dsc))

                dsc_ref[...] = lax.fori_loop(0, n_chunks // 2, first_pair, jnp.zeros((1, pg), F32))
                dwb_ref[...] = dw_ref[...].reshape(N_DEV, pg // N_DEV, pg).astype(BF16)

                def second(ci, _):
                    r0 = pl.multiple_of(ci * t, t)
                    ext = pad_ref[pl.ds(r0, t + 2 * h), :]
                    back = _win_sum(ext, w, 1)[h:h + t, :]
                    du_ref[pl.ds(r0, t), :] = (back - dd_ref[pl.ds(r0, t), :]).astype(BF16)
                    return 0

                lax.fori_loop(0, n_chunks, second, 0, unroll=2)

    return pl.pallas_call(
        body, name="pool_bwd", grid=(len(POOL_WINDOWS),),
        in_specs=[_bs((seq, pg), lambda g: (0, g)), _bs((seq, pg), lambda g: (0, g)),
                  _bs((None, pg, pg), lambda g: (g, 0, 0)), _bs((1, pg), lambda g: (0, g)),
                  _bs((8, 128), lambda g: (0, 0))],
        out_specs=[_bs((seq, pg), lambda g: (0, g)), _bs((N_DEV, None, pg // N_DEV, pg), lambda g: (0, g, 0, 0)),
                   _bs((1, pg), lambda g: (0, g))],
        out_shape=[jax.ShapeDtypeStruct((seq, 3 * dp), BF16),
                   jax.ShapeDtypeStruct((N_DEV, len(POOL_WINDOWS), pg // N_DEV, pg), BF16),
                   jax.ShapeDtypeStruct((1, dp), F32)],
        scratch_shapes=[pltpu.VMEM((seq + 2 * h, pg), F32), pltpu.VMEM((seq, pg), F32), pltpu.VMEM((pg, pg), F32)],
        compiler_params=_params(1),
    )(d, dy, w_pool, pool_scale, token)


def _tile_scan(n_tiles, lanes, loads, stores):
    row = lax.broadcasted_iota(jnp.int32, (8, lanes), 0)
    group = 8

    def local_scan(n, k):
        aa, bb = loads[n](k)
        for sh in (1, 2, 4):
            if n == 0:
                ok = row >= sh
                ap = jnp.where(ok, pltpu.roll(aa, sh, 0), 1.0)
                bp = jnp.where(ok, pltpu.roll(bb, sh, 0), 0.0)
            else:
                ok = row < 8 - sh
                ap = jnp.where(ok, pltpu.roll(aa, 8 - sh, 0), 1.0)
                bp = jnp.where(ok, pltpu.roll(bb, 8 - sh, 0), 0.0)
            bb = aa * bp + bb
            aa = aa * ap
        return aa, bb

    def step(s, carry):
        carry = list(carry)
        for n in range(2):
            tiles = [s * group + u if n == 0 else n_tiles - 1 - (s * group + u) for u in range(group)]
            local = [local_scan(n, k) for k in tiles]
            for k, (aa, bb) in zip(tiles, local):
                hh = bb + aa * carry[n]
                stores[n](k, hh)
                carry[n] = jnp.broadcast_to(hh[7:8, :] if n == 0 else hh[0:1, :], (8, lanes))
        return tuple(carry)

    zeros = jnp.zeros((8, lanes), F32)
    lax.fori_loop(0, n_tiles // group, step, (zeros, zeros))


def _gate_preacts(xc, wcat_ref):
    xcb = xc.astype(BF16)
    return xcb, jnp.dot(xcb, wcat_ref[...], preferred_element_type=F32)


def _gates(pre, n, pk_ref, sp):
    lh = pre.shape[1] // 4
    r = _sigmoid(pre[:, (2 * n) * lh:(2 * n + 1) * lh] + pk_ref[pl.ds(4 + n, 1), :])
    i = _sigmoid(pre[:, (2 * n + 1) * lh:(2 * n + 2) * lh] + pk_ref[pl.ds(6 + n, 1), :])
    log_a = (-RG_C * r) * sp[n]
    a = jnp.exp(log_a)
    x = 2.0 * log_a
    one_minus_a2 = jnp.where(x > -0.01, -(x * (1.0 + x * (0.5 + x * (1.0 / 6.0)))), 1.0 - a * a)
    m = jnp.sqrt(one_minus_a2)
    return r, i, a, m


def _conv_chunk(upad_ref, pk_ref, cb, r0, t):
    ext = upad_ref[pl.ds(r0, t + 2 * CONV_HALO), :]
    acc = pk_ref[pl.ds(1, 1), :] * ext
    for k in (0, 2, 3):
        acc = acc + pk_ref[pl.ds(k, 1), :] * _shift(ext, k - 1)
    return acc[CONV_HALO:CONV_HALO + t, :] + cb, ext


def _lru_fwd(p3, y_in, pack, conv_b, wcat, token, seq, d_model):
    dl = d_model // 2
    lh = dl // N_HEADS
    t = min(SEQ_CHUNK, seq)
    n_chunks = seq // t
    hal = CONV_HALO
    first_rec_block = (d_model - dl) // lh

    def body(ur_ref, ug_ref, pk_ref, cb_ref, wcat_ref, yin_ref, tok_ref, y_ref, h0_ref, h1_ref,
             upad, a_scr, b_scr):
        del yin_ref, tok_ref
        zeros = jnp.zeros((hal, lh), F32)
        upad[0:hal, :] = zeros
        upad[hal + seq:hal + seq + hal, :] = zeros
        for ref in (h0_ref, h1_ref):
            ref[0:hal, :] = zeros
            ref[hal + seq:hal + seq + hal, :] = zeros

        def fill(ci, _):
            r0 = pl.multiple_of(ci * t, t)
            upad[pl.ds(hal + r0, t), :] = ur_ref[pl.ds(r0, t), :]
            return 0

        lax.fori_loop(0, n_chunks, fill, 0)
        cb = cb_ref[...]
        sp = [_softplus(-pk_ref[pl.ds(8 + n, 1), :]) for n in range(2)]

        def chunk(ci, _):
            r0 = pl.multiple_of(ci * t, t)
            xc, _ext = _conv_chunk(upad, pk_ref, cb, r0, t)
            _, pre = _gate_preacts(xc, wcat_ref)
            for n in range(2):
                _, i, a, m = _gates(pre, n, pk_ref, sp)
                a_scr[n, pl.ds(r0, t), :] = a
                b_scr[n, pl.ds(r0, t), :] = (m * i) * xc
            return 0

        lax.fori_loop(0, n_chunks, chunk, 0, unroll=2)

        def load(n):
            def get(k):
                at = pl.ds(pl.multiple_of(k * 8, 8), 8)
                return a_scr[n, at, :], b_scr[n, at, :]
            return get

        def store(ref):
            def put(k, v):
                ref[pl.ds(pl.multiple_of(hal + k * 8, 8), 8), :] = v
            return put

        _tile_scan(seq // 8, lh, [load(0), load(1)], [store(h0_ref), store(h1_ref)])

        def out(ci, _):
            r0 = pl.multiple_of(ci * t, t)
            hsum = h0_ref[pl.ds(hal + r0, t), :] + h1_ref[pl.ds(hal + r0, t), :]
            gl, _dg = _gelu_and_grad(ug_ref[pl.ds(r0, t), :])
            y_ref[pl.ds(r0, t), :] = (hsum * gl).astype(BF16)
            return 0

        lax.fori_loop(0, n_chunks, out, 0)

    return pl.pallas_call(
        body, name="lru_fwd", grid=(N_HEADS,),
        in_specs=[_bs((None, seq, lh), lambda h: (1, 0, h)), _bs((None, seq, lh), lambda h: (2, 0, h)),
                  _bs((None, SMALL_ROWS, lh), lambda h: (h, 0, 0)), _bs((1, lh), lambda h: (0, h)),
                  _bs((None, lh, 4 * lh), lambda h: (h, 0, 0)),
                  ANY, _bs((8, 128), lambda h: (0, 0))],
        out_specs=[_bs((seq, lh), lambda h: (0, first_rec_block + h)),
                   _bs((seq + 2 * hal, lh), lambda h: (0, h)), _bs((seq + 2 * hal, lh), lambda h: (0, h))],
        out_shape=[jax.ShapeDtypeStruct((seq, d_model), BF16), jax.ShapeDtypeStruct((seq + 2 * hal, dl), F32),
                   jax.ShapeDtypeStruct((seq + 2 * hal, dl), F32)],
        scratch_shapes=[pltpu.VMEM((seq + 2 * hal, lh), F32), pltpu.VMEM((2, seq, lh), F32),
                        pltpu.VMEM((2, seq, lh), F32)],
        input_output_aliases={5: 0},
        compiler_params=_params(1),
    )(p3, p3, pack, conv_b, wcat, y_in, token)


def _lru_bwd(p3, dy, h0p, h1p, dproj_in, pack, conv_b, wcat, token, seq, d_model):
    dl = d_model // 2
    lh = dl // N_HEADS
    t = min(SEQ_CHUNK, seq)
    n_chunks = seq // t
    hal = CONV_HALO
    first_rec_block = (d_model - dl) // lh
    tn_dims = (((0,), (0,)), ((), ()))
    nt_dims = (((1,), (1,)), ((), ()))

    def body(ur_ref, ug_ref, dy_ref, h0_ref, h1_ref, pk_ref, cb_ref, wcat_ref, tok_ref, din_ref,
             dproj_ref, dpk_ref, dcb_ref, dwcat_ref,
             upad, a_scr, dh_scr, g_scr, dxc_pad, dpr_ref, out_sems, gate_scr):
        del din_ref, tok_ref
        zeros = jnp.zeros((hal, lh), F32)
        for ref in (upad, dxc_pad):
            ref[0:hal, :] = zeros
            ref[hal + seq:hal + seq + hal, :] = zeros
        for n in range(2):
            a_scr[n, 0:hal, :] = zeros
            a_scr[n, hal + seq:hal + seq + hal, :] = zeros

        def fill(ci, _):
            r0 = pl.multiple_of(ci * t, t)
            upad[pl.ds(hal + r0, t), :] = ur_ref[pl.ds(r0, t), :]
            return 0

        lax.fori_loop(0, n_chunks, fill, 0)
        cb = cb_ref[...]
        lam = [pk_ref[pl.ds(8 + n, 1), :] for n in range(2)]
        sp = [_softplus(-lam[n]) for n in range(2)]

        def chunk1(ci, _):
            r0 = pl.multiple_of(ci * t, t)
            xc, _ext = _conv_chunk(upad, pk_ref, cb, r0, t)
            _, pre = _gate_preacts(xc, wcat_ref)
            for n in range(2):
                r, i, a, m = _gates(pre, n, pk_ref, sp)
                a_scr[n, pl.ds(hal + r0, t), :] = a
                for q, v in enumerate((r, i, m)):
                    gate_scr[3 * n + q, pl.ds(r0, t), :] = v
            hsum = h0_ref[pl.ds(hal + r0, t), :] + h1_ref[pl.ds(hal + r0, t), :]
            gl, dgl = _gelu_and_grad(ug_ref[pl.ds(r0, t), :])
            dyv = dy_ref[pl.ds(r0, t), :]
            dh_scr[pl.ds(r0, t), :] = dyv * gl
            dpr_ref[1, pl.ds(r0, t), :] = ((dyv * hsum) * dgl).astype(BF16)
            return 0

        lax.fori_loop(0, n_chunks, chunk1, 0, unroll=2)

        def load(n):
            def get(k):
                r0 = pl.multiple_of(k * 8, 8)
                if n == 0:
                    coef = _shift(a_scr[0, pl.ds(pl.multiple_of(hal + r0, 8), 16), :], 1)[0:8, :]
                else:
                    coef = _shift(a_scr[1, pl.ds(pl.multiple_of(hal + r0 - 8, 8), 16), :], -1)[8:16, :]
                return coef, dh_scr[pl.ds(r0, 8), :]
            return get

        def store(n):
            def put(k, v):
                g_scr[n, pl.ds(pl.multiple_of(k * 8, 8), 8), :] = v
            return put

        _tile_scan(seq // 8, lh, [load(1), load(0)], [store(1), store(0)])

        dwcat_ref[...] = jnp.zeros((lh, 4 * lh), F32)

        def chunk3(ci, carry):
            dba, dbi, dlam, dcb = carry
            r0 = pl.multiple_of(ci * t, t)
            xc, _ext = _conv_chunk(upad, pk_ref, cb, r0, t)
            xcb = xc.astype(BF16)
            dxc = jnp.zeros((t, lh), F32)
            dba, dbi, dlam = list(dba), list(dbi), list(dlam)
            dpre = []
            for n in range(2):
                r, i, m = (gate_scr[3 * n + q, pl.ds(r0, t), :] for q in range(3))
                a = a_scr[n, pl.ds(hal + r0, t), :]
                hext = (h0_ref if n == 0 else h1_ref)[pl.ds(r0, t + 2 * hal), :]
                hprev = _shift(hext, -1 if n == 0 else 1)[hal:hal + t, :]
                gb = g_scr[n, pl.ds(r0, t), :]
                da = gb * hprev
                dm = gb * i * xc
                di = gb * m * xc
                dxc = dxc + gb * (m * i)
                dlog_a = da * a - dm * (a * a) / m
                dr = dlog_a * (-RG_C * sp[n])
                dlam[n] = dlam[n] + jnp.sum(dlog_a * r, axis=0, keepdims=True)
                dpr = dr * r * (1.0 - r)
                dpi = di * i * (1.0 - i)
                dba[n] = dba[n] + jnp.sum(dpr, axis=0, keepdims=True)
                dbi[n] = dbi[n] + jnp.sum(dpi, axis=0, keepdims=True)
                dpre += [dpr.astype(BF16), dpi.astype(BF16)]
            dpre = jnp.concatenate(dpre, axis=1)
            dwcat_ref[...] += lax.dot_general(xcb, dpre, tn_dims, preferred_element_type=F32)
            dxc = dxc + lax.dot_general(dpre, wcat_ref[...], nt_dims, preferred_element_type=F32)
            dxc_pad[pl.ds(hal + r0, t), :] = dxc
            dcb = dcb + jnp.sum(dxc, axis=0, keepdims=True)
            return tuple(dba), tuple(dbi), tuple(dlam), dcb

        zr = jnp.zeros((1, lh), F32)
        def chunk3_pair(cj, carry):
            return chunk3(2 * cj + 1, chunk3(2 * cj, carry))

        dba, dbi, dlam, dcb = lax.fori_loop(0, n_chunks // 2, chunk3_pair, ((zr, zr), (zr, zr), (zr, zr), zr))
        dcb_ref[...] = dcb
        for n in range(2):
            dpk_ref[pl.ds(4 + n, 1), :] = dba[n]
            dpk_ref[pl.ds(6 + n, 1), :] = dbi[n]
            dpk_ref[pl.ds(8 + n, 1), :] = dlam[n] * (RG_C * jax.nn.sigmoid(-lam[n]))
        dpk_ref[pl.ds(10, SMALL_ROWS - 10), :] = jnp.zeros((SMALL_ROWS - 10, lh), F32)

        def chunk4(ci, dtap):
            r0 = pl.multiple_of(ci * t, t)
            gext = dxc_pad[pl.ds(r0, t + 2 * hal), :]
            uext = upad[pl.ds(r0, t + 2 * hal), :]
            gmid = gext[hal:hal + t, :]
            du = pk_ref[pl.ds(1, 1), :] * gext
            for k in (0, 2, 3):
                du = du + pk_ref[pl.ds(k, 1), :] * _shift(gext, 1 - k)
            dpr_ref[0, pl.ds(r0, t), :] = du[hal:hal + t, :].astype(BF16)
            out = []
            for k in range(4):
                usl = _shift(uext, k - 1)[hal:hal + t, :]
                out.append(dtap[k] + jnp.sum(gmid * usl, axis=0, keepdims=True))
            return tuple(out)

        dtap = lax.fori_loop(0, n_chunks, chunk4, (zr, zr, zr, zr))
        for k in range(4):
            dpk_ref[pl.ds(k, 1), :] = dtap[k]

        head = pl.program_id(0)
        outs = [pltpu.make_async_copy(
            dpr_ref.at[b], dproj_ref.at[:, pl.ds(pl.multiple_of((1 + b) * dl + head * lh, lh), lh)], out_sems.at[b])
            for b in range(2)]
        for cp in outs:
            cp.start()
        for cp in outs:
            cp.wait()

    return pl.pallas_call(
        body, name="lru_bwd", grid=(N_HEADS,),
        in_specs=[_bs((None, seq, lh), lambda h: (1, 0, h)), _bs((None, seq, lh), lambda h: (2, 0, h)),
                  _bs((seq, lh), lambda h: (0, first_rec_block + h)),
                  _bs((seq + 2 * hal, lh), lambda h: (0, h)), _bs((seq + 2 * hal, lh), lambda h: (0, h)),
                  _bs((None, SMALL_ROWS, lh), lambda h: (h, 0, 0)), _bs((1, lh), lambda h: (0, h)),
                  _bs((None, lh, 4 * lh), lambda h: (h, 0, 0)),
                  _bs((8, 128), lambda h: (0, 0)), ANY],
        out_specs=[ANY, _bs((None, SMALL_ROWS, lh), lambda h: (h, 0, 0)),
                   _bs((1, lh), lambda h: (0, h)), _bs((None, lh, 4 * lh), lambda h: (h, 0, 0))],
        out_shape=[jax.ShapeDtypeStruct((seq, 3 * dl), BF16), jax.ShapeDtypeStruct((N_HEADS, SMALL_ROWS, lh), F32),
                   jax.ShapeDtypeStruct((1, dl), F32), jax.ShapeDtypeStruct((N_HEADS, lh, 4 * lh), F32)],
        scratch_shapes=[pltpu.VMEM((seq + 2 * hal, lh), F32), pltpu.VMEM((2, seq + 2 * hal, lh), F32),
                        pltpu.VMEM((seq, lh), F32), pltpu.VMEM((2, seq, lh), F32),
                        pltpu.VMEM((seq + 2 * hal, lh), F32), pltpu.VMEM((2, seq, lh), BF16),
                        pltpu.SemaphoreType.DMA((2,)), pltpu.VMEM((6, seq, lh), F32)],
        input_output_aliases={9: 0},
        compiler_params=_params(1),
    )(p3, p3, dy, h0p, h1p, pack, conv_b, wcat, token, dproj_in)


class _tiles:
    def __init__(self, seq, d_model, d_ff):
        self.rows = min(1024, seq)
        self.ln_rows = min(256, seq)
        self.ff_cols = min(2048, d_ff)
        self.ff_split = 4
        self.ff_k = min(2048, d_ff)
        self.grad_rows = 512


def _ln_loss_bwd(ffn, x1, tgt, g, b, tr):
    seq, d = ffn.shape

    def body(f_ref, x_ref, t_ref, g_ref, b_ref, dz_ref, dzb_ref, dg_ref, db_ref, loss_ref):
        i = pl.program_id(0)
        gv = g_ref[...]
        z = ALPHA * x_ref[...] + f_ref[...]
        y, xhat, rstd = _ln_fwd(z, gv, b_ref[...])
        err = y - t_ref[...]
        part = 0.5 * jnp.sum(jnp.mean(err * err, axis=-1, keepdims=True), axis=0, keepdims=True)
        dz, dg, db = _ln_bwd(err * (1.0 / d), xhat, rstd, gv)
        dz_ref[...] = dz
        dzb_ref[...] = dz.astype(BF16)
        _acc_rows(dg_ref, i == 0, dg)
        _acc_rows(db_ref, i == 0, db)
        _acc_rows(loss_ref, i == 0, jnp.broadcast_to(part, (8, 128)))

    row = _bs((tr, d), lambda i: (i, 0))
    vec = _bs((1, d), lambda i: (0, 0))
    return pl.pallas_call(
        body, name="ln_ffn_loss", grid=(seq // tr,), in_specs=[row, row, row, vec, vec],
        out_specs=[row, row, vec, vec, _bs((8, 128), lambda i: (0, 0))],
        out_shape=[jax.ShapeDtypeStruct((seq, d), F32), jax.ShapeDtypeStruct((seq, d), BF16),
                   jax.ShapeDtypeStruct((1, d), F32), jax.ShapeDtypeStruct((1, d), F32),
                   jax.ShapeDtypeStruct((8, 128), F32)],
        compiler_params=_params(1),
    )(ffn, x1, tgt, g, b)


def _ln_bwd_side(dx_branch, dres, z, g, b, n_steps):
    seq, d = z.shape
    tr = seq // n_steps

    def fn(step, ins, outs):
        a_ref, r_ref, z_ref, g_ref, b_ref = ins
        dz_ref, dzb_ref, dg_ref, db_ref = outs
        gv = g_ref[...]
        _, xhat, rstd = _ln_fwd(z_ref[...], gv, b_ref[...])
        dz, dg, db = _ln_bwd(ALPHA * r_ref[...] + a_ref[...], xhat, rstd, gv)
        dz_ref[...] = dz
        dzb_ref[...] = dz.astype(BF16)
        _acc_rows(dg_ref, step == 0, dg)
        _acc_rows(db_ref, step == 0, db)

    row = ((tr, d), lambda s: (s, 0))
    vec = ((1, d), lambda s: (0, 0))
    shapes = [jax.ShapeDtypeStruct((seq, d), F32), jax.ShapeDtypeStruct((seq, d), BF16),
              jax.ShapeDtypeStruct((1, d), F32), jax.ShapeDtypeStruct((1, d), F32)]
    return [(dx_branch, *row), (dres, *row), (z, *row), (g, *vec), (b, *vec)], shapes, [row, row, vec, vec], fn


def _to_bf16(name, a, token):
    rows, cols = a.shape
    tr = min(512, rows)

    def body(a_ref, tok_ref, o_ref):
        del tok_ref
        o_ref[...] = a_ref[...].astype(BF16)

    return pl.pallas_call(
        body, name=name, grid=(rows // tr,),
        in_specs=[_bs((tr, cols), lambda i: (i, 0)), _bs((8, 128), lambda i: (0, 0))],
        out_specs=_bs((tr, cols), lambda i: (i, 0)), out_shape=jax.ShapeDtypeStruct((rows, cols), BF16),
        compiler_params=_params(1),
    )(a, token)


def _sum_blocks(name, parts):
    def body(p_ref, o_ref):
        acc = p_ref[0]
        for s in range(1, parts.shape[0]):
            acc = acc + p_ref[s]
        o_ref[...] = acc

    return pl.pallas_call(body, name=name, out_shape=jax.ShapeDtypeStruct(parts.shape[1:], F32))(parts)


def _adamw_values(w, g, m, v):
    m = ADAM_B1 * m + (1.0 - ADAM_B1) * g
    v = ADAM_B2 * v + (1.0 - ADAM_B2) * (g * g)
    m_hat = m / (1.0 - ADAM_B1 ** ADAM_STEP)
    v_hat = v / (1.0 - ADAM_B2 ** ADAM_STEP)
    delta = -ADAM_LR * (m_hat / (jnp.sqrt(v_hat) + ADAM_EPS) + ADAM_WD * w)
    return delta, m, v


def _adamw_side(own, parts, w, m, v, n_steps):
    rows, cols = w.shape
    tr = rows // n_steps

    def fn(step, ins, outs):
        o_ref, p_ref, w_ref, m_ref, v_ref = ins
        g = o_ref[...]
        for s in range(parts.shape[0]):
            g = g + p_ref[s].astype(F32)
        delta, mn, vn = _adamw_values(w_ref[...], g, m_ref[...], v_ref[...])
        for ref, val in zip(outs, (g, delta, mn, vn)):
            ref[...] = val

    row = ((tr, cols), lambda s: (s, 0))
    stack = ((parts.shape[0], tr, cols), lambda s: (0, s, 0))
    shapes = [jax.ShapeDtypeStruct((rows, cols), F32)] * 4
    return [(own, *row), (parts, *stack), (w, *row), (m, *row), (v, *row)], shapes, [row] * 4, fn


def _sum_adamw(name, own, parts, w, m, v):
    rows, cols = w.shape
    n_parts = parts.shape[0]
    tr = rows
    min_rows = 8 if parts.dtype == F32 else 16
    while tr * cols * 4 > 2 * 1024 * 1024 and tr % (2 * min_rows) == 0:
        tr //= 2

    def body(*refs):
        if own is None:
            p_ref, w_ref, m_ref, v_ref, g_ref, d_ref, mo_ref, vo_ref = refs
            g = p_ref[0].astype(F32)
            rest = range(1, n_parts)
        else:
            o_ref, p_ref, w_ref, m_ref, v_ref, g_ref, d_ref, mo_ref, vo_ref = refs
            g = o_ref[...]
            rest = range(n_parts)
        for s in rest:
            g = g + p_ref[s].astype(F32)
        delta, mn, vn = _adamw_values(w_ref[...], g, m_ref[...], v_ref[...])
        g_ref[...] = g
        d_ref[...] = delta
        mo_ref[...] = mn
        vo_ref[...] = vn

    spec = _bs((tr, cols), lambda i: (i, 0))
    lead = [] if own is None else [own]
    return pl.pallas_call(
        body, name=name, grid=(rows // tr,),
        in_specs=[spec] * len(lead) + [_bs((n_parts, tr, cols), lambda i: (0, i, 0)), spec, spec, spec],
        out_specs=[spec] * 4, out_shape=[jax.ShapeDtypeStruct((rows, cols), F32)] * 4,
        compiler_params=_params(1),
    )(*lead, parts, w, m, v)


def _rows128(a):
    return a.reshape(-1, 128)


def kernel(x, ln_mix_g, ln_mix_b, w_in, w_pool, pool_scale, conv_w, conv_b, w_rg_a, b_rg_a, w_rg_i, b_rg_i, rg_lambda, w_out, ln_ffn_g, ln_ffn_b, w_mlp_in, w_mlp_out, loss_target, m_ln_mix_g, m_ln_mix_b, m_w_in, m_w_pool, m_pool_scale, m_conv_w, m_conv_b, m_w_rg_a, m_b_rg_a, m_w_rg_i, m_b_rg_i, m_rg_lambda, m_w_out, m_ln_ffn_g, m_ln_ffn_b, m_w_mlp_in, m_w_mlp_out, v_ln_mix_g, v_ln_mix_b, v_w_in, v_w_pool, v_pool_scale, v_conv_w, v_conv_b, v_w_rg_a, v_b_rg_a, v_w_rg_i, v_b_rg_i, v_rg_lambda, v_w_out, v_ln_ffn_g, v_ln_ffn_b, v_w_mlp_in, v_w_mlp_out):
    seq, d_model = x.shape[1], x.shape[2]
    dh = d_model // 2
    lh = dh // N_HEADS
    pg = dh // len(POOL_WINDOWS)
    d_ff = w_mlp_in.shape[2] * N_DEV
    assert lh == 128 and conv_w.shape[3] == lh and w_pool.shape[2] * N_DEV == pg

    xs = x[0]
    tgt = loss_target[0]

    def small_pack(cw, ba, bi, lam):
        return jnp.concatenate([cw.reshape(4, lh), ba.reshape(2, lh), bi.reshape(2, lh), lam.reshape(2, lh),
                                jnp.zeros((SMALL_ROWS - 10, lh), F32)], axis=0)

    pack_mine = small_pack(conv_w, b_rg_a, b_rg_i, rg_lambda)
    pack_bits = lax.bitcast_convert_type(pack_mine, BF16).reshape(1, SMALL_ROWS, 2 * lh)
    win_gather = _SplitGather("gather_w_in", [(w_in[0], 1), (w_pool[0], 1), (pack_bits, 0)], BF16, after=pack_mine)
    wout_gather = _SplitGather("gather_w_out", [(w_out[0], 0)], BF16, after=win_gather.token)
    w1_gather = _SplitGather("gather_w_mlp_in", [(w_mlp_in[0], 1)], BF16, after=wout_gather.token)
    w2_gather = _SplitGather("gather_w_mlp_out", [(w_mlp_out[0], 0)], BF16, after=w1_gather.token)
    xb = _to_bf16("x_bf16", x[0], w2_gather.token)
    win_full, wpool_full, pack_bits_full = win_gather.wait(after=win_gather.relay(after=xb))
    pack_full = lax.bitcast_convert_type(pack_bits_full.reshape(N_DEV, SMALL_ROWS, lh, 2), F32)
    wcat = jnp.concatenate([w_rg_a[0, 0], w_rg_i[0, 0], w_rg_a[0, 1], w_rg_i[0, 1]], axis=-1).astype(BF16)
    vec = lambda i, j, k: (0, 0)
    row_full = lambda i, j, k: (i, 0)

    def after(token):
        return (token, _sp((8, 128), vec))

    def sds(shape, dtype):
        return jax.ShapeDtypeStruct(shape, dtype)

    def plain_epi(acc, i, ex, out):
        out[0][...] = acc

    def bf16_epi(acc, i, ex, out):
        out[0][...] = acc.astype(BF16)

    t = _tiles(seq, d_model, d_ff)

    (p3,) = _matmul(
        "proj", xb, win_full, _sp((t.rows, d_model), lambda i, j, k: (i, 0)), _sp((d_model, dh), lambda i, j, k: (0, j)),
        grid=(seq // t.rows, 3, 1),
        out_shape=[sds((3, seq, dh), F32)], out_specs=[_sp((None, t.rows, dh), lambda i, j, k: (j, i, 0))],
        epilogue=plain_epi)

    d_pool, y_half = _pool_fwd(p3, wpool_full, pool_scale, seq, d_model)
    y, h0p, h1p = _lru_fwd(p3, y_half, pack_full, conv_b, wcat, wout_gather.relay(after=y_half), seq, d_model)
    (wout_full,) = wout_gather.wait(after=y)
    relay_token = w1_gather.relay(after=wout_full)

    mix_rows = 2 * t.ln_rows

    def mix_epi(acc, i, ex, out):
        x_ref, g_ref, b_ref = ex[:3]
        for part in range(2):
            rows = pl.ds(part * t.ln_rows, t.ln_rows)
            z = ALPHA * x_ref[rows, :] + acc[part * t.ln_rows:(part + 1) * t.ln_rows, :]
            x1, _, _ = _ln_fwd(z, g_ref[...], b_ref[...])
            out[0][rows, :] = z
            out[1][rows, :] = x1
            out[2][rows, :] = x1.astype(BF16)

    z1, x1, x1b = _matmul(
        "mix_out", y, wout_full, _sp((mix_rows, d_model), row_full), _sp((d_model, d_model), vec, single=True),
        grid=(seq // mix_rows, 1, 1),
        extras=[(xs, _sp((mix_rows, d_model), row_full)), (ln_mix_g, _sp((1, d_model), vec)),
                (ln_mix_b, _sp((1, d_model), vec)), after(relay_token)],
        out_shape=[sds((seq, d_model), F32), sds((seq, d_model), F32), sds((seq, d_model), BF16)],
        out_specs=[_sp((mix_rows, d_model), row_full)] * 3, epilogue=mix_epi)
    (w1_full,) = w1_gather.wait(after=x1b)

    def mlp_in_epi(acc, i, ex, out, cols):
        h = jnp.maximum(acc, 0.0)
        out[0][:, cols] = (h * h).astype(BF16)
        out[1][:, cols] = (2.0 * h).astype(BF16)

    hmid, dact = _matmul(
        "mlp_in", x1b, w1_full, _sp((t.rows, d_model), lambda i, j, k: (i, 0)),
        _sp((d_model, t.ff_cols), lambda i, j, k: (0, j)),
        grid=(seq // t.rows, d_ff // t.ff_cols, 1), j_outer=True,
        out_shape=[sds((seq, d_ff), BF16)] * 2, out_specs=[_sp((t.rows, t.ff_cols), lambda i, j, k: (i, j))] * 2,
        epilogue=mlp_in_epi, n_split=t.ff_split)
    (w2_full,) = w2_gather.wait(after=w2_gather.relay(after=hmid))

    (ffn,) = _matmul(
        "mlp_out", hmid, w2_full, _sp((t.rows, t.ff_k), lambda i, j, k: (i, k)),
        _sp((t.ff_k, d_model), lambda i, j, k: (k, 0)),
        grid=(seq // t.rows, 1, d_ff // t.ff_k),
        out_shape=[sds((seq, d_model), F32)], out_specs=[_sp((t.rows, d_model), row_full)])
    dz2, dz2b, g_ffn_g, g_ffn_b, loss_part = _ln_loss_bwd(ffn, x1, tgt, ln_ffn_g, ln_ffn_b, t.ln_rows)

    (g_w2,) = _matmul(
        "grad_w_mlp_out", hmid, dz2b, _sp((seq, t.grad_rows), lambda i, j, k: (0, i)),
        _sp((seq, d_model), vec, single=True),
        grid=(d_ff // t.grad_rows, 1, 1), ta=True,
        out_shape=[sds((d_ff, d_model), BF16)], out_specs=[_sp((t.grad_rows, d_model), row_full)],
        epilogue=bf16_epi)
    scatter_w2 = _SplitReduceScatter("scatter_w_mlp_out", [g_w2.reshape(N_DEV, d_ff // N_DEV, d_model)])

    def dpre_epi(acc, i, ex, out, cols):
        out[0][:, cols] = (acc * ex[0][:, cols].astype(F32)).astype(BF16)

    (dpre,) = _matmul(
        "mlp_dpre", dz2b, w2_full, _sp((t.rows, d_model), lambda i, j, k: (i, 0)),
        _sp((t.ff_cols, d_model), lambda i, j, k: (j, 0)),
        grid=(seq // t.rows, d_ff // t.ff_cols, 1), j_outer=True, tb=True,
        extras=[(dact, _sp((t.rows, t.ff_cols), lambda i, j, k: (i, j))), after(scatter_w2.token)],
        out_shape=[sds((seq, d_ff), BF16)], out_specs=[_sp((t.rows, t.ff_cols), lambda i, j, k: (i, j))],
        epilogue=dpre_epi, n_split=t.ff_split)
    token_w2 = scatter_w2.combine_and_send(after=dpre)

    (dx1_mlp,) = _matmul(
        "mlp_dx", dpre, w1_full, _sp((t.rows, t.ff_k), lambda i, j, k: (i, k)),
        _sp((d_model, t.ff_k), lambda i, j, k: (0, k)),
        grid=(seq // t.rows, 1, d_ff // t.ff_k), tb=True, extras=[after(token_w2)],
        out_shape=[sds((seq, d_model), F32)], out_specs=[_sp((t.rows, d_model), row_full)])
    def block_epi(acc, i, ex, out):
        out[0][0] = acc.astype(BF16)

    fs = d_ff // N_DEV
    g_w1, dz1, dz1b, g_mix_g, g_mix_b = _matmul(
        "grad_w_mlp_in", x1b, dpre, _sp((seq, t.grad_rows), lambda i, j, k: (0, i)),
        _sp((seq, fs), lambda i, j, k: (0, j)),
        grid=(d_model // t.grad_rows, N_DEV, 1), j_outer=True, ta=True,
        out_shape=[sds((N_DEV, d_model, fs), BF16)],
        out_specs=[_sp((1, t.grad_rows, fs), lambda i, j, k: (j, i, 0))], epilogue=block_epi,
        side=_ln_bwd_side(dx1_mlp, dz2, z1, ln_mix_g, ln_mix_b, d_model // t.grad_rows * N_DEV))

    (dy,) = _matmul(
        "mix_dy", dz1b, wout_full, _sp((t.rows, d_model), lambda i, j, k: (i, 0)),
        _sp((dh, d_model), lambda i, j, k: (j, 0)),
        grid=(seq // t.rows, 2, 1), j_outer=True, tb=True,
        out_shape=[sds((seq, d_model), F32)], out_specs=[_sp((t.rows, dh), lambda i, j, k: (i, j))],
        epilogue=plain_epi)
    (g_wout,) = _matmul(
        "grad_w_out", y, dz1b, _sp((seq, t.grad_rows), lambda i, j, k: (0, i)), _sp((seq, d_model), vec, single=True),
        grid=(d_model // t.grad_rows, 1, 1), ta=True,
        out_shape=[sds((d_model, d_model), BF16)], out_specs=[_sp((t.grad_rows, d_model), row_full)],
        epilogue=bf16_epi)
    scatter_w1 = _SplitReduceScatter("scatter_w_mlp_in", [g_w1, g_wout.reshape(N_DEV, d_model // N_DEV, d_model)])

    dproj_pool, g_wpool, g_pscale = _pool_bwd(d_pool, dy, wpool_full, pool_scale, scatter_w1.token, seq, d_model)
    token_w1 = scatter_w1.combine_and_send(after=dproj_pool)
    dproj, g_pack, g_convb, g_wcat = _lru_bwd(p3, dy, h0p, h1p, dproj_pool, pack_full, conv_b, wcat,
                                              token_w1, seq, d_model)
    g_wa = jnp.stack([g_wcat[:, :, 0:lh], g_wcat[:, :, 2 * lh:3 * lh]])
    g_wi = jnp.stack([g_wcat[:, :, lh:2 * lh], g_wcat[:, :, 3 * lh:4 * lh]])

    rep_parts = [_rows128(g_wa), _rows128(g_wi), _rows128(g_mix_g), _rows128(g_mix_b), _rows128(g_ffn_g),
                 _rows128(g_ffn_b), _rows128(g_pscale), _rows128(g_convb)]
    rep_rows = [p.shape[0] for p in rep_parts]
    n_rep = sum(rep_rows)
    small = jnp.concatenate(rep_parts + [_rows128(g_pack), loss_part], axis=0)
    small_gather = _SplitGather("gather_small_grads", [(small[None], 0)], F32, after=small)

    ws = 3 * dh // N_DEV

    def pair_epi(acc, i, ex, out):
        out[0][0] = acc[:, :ws].astype(BF16)
        out[0][1] = acc[:, ws:].astype(BF16)

    def adam_big(name, own_landed, w, m, v):
        own, landed = own_landed
        shp = w.shape
        two = lambda a: a.reshape(-1, shp[-1])
        res = _sum_adamw(name, own, landed, two(w), two(m), two(v))
        return [r.reshape(shp) for r in res]

    (r_w2,) = scatter_w2.wait(after=small_gather.token)
    n_steps = d_model // t.grad_rows * (N_DEV // 2)
    g_win, *o_w2 = _matmul(
        "grad_w_in", xb, dproj, _sp((seq, t.grad_rows), lambda i, j, k: (0, i)),
        _sp((seq, 2 * ws), lambda i, j, k: (0, j)),
        grid=(d_model // t.grad_rows, N_DEV // 2, 1), ta=True,
        out_shape=[sds((N_DEV, d_model, ws), BF16)],
        out_specs=[_sp((2, t.grad_rows, ws), lambda i, j, k: (j, i, 0))], epilogue=pair_epi,
        side=_adamw_side(r_w2[0], r_w2[1], w_mlp_out[0], m_w_mlp_out[0], v_w_mlp_out[0], n_steps))
    o_w2 = [r.reshape(w_mlp_out.shape) for r in o_w2]
    scatter_mix = _SplitReduceScatter(
        "scatter_mixer", [g_win, g_wpool.reshape(N_DEV, pg // N_DEV * len(POOL_WINDOWS), pg)])

    r_w1, r_wout = scatter_w1.wait(after=scatter_mix.token)
    o_w1 = adam_big("adam_w_mlp_in", r_w1, w_mlp_in, m_w_mlp_in, v_w_mlp_in)
    token_mix = scatter_mix.combine_and_send(after=o_w1[0])

    def dx_epi(acc, i, ex, out):
        out[0][...] = ALPHA * ex[0][...] + acc

    dx_rows = t.ln_rows * 2
    dx, *o_wout = _matmul(
        "grad_x", dproj, win_full, _sp((dx_rows, 3 * dh), lambda i, j, k: (i, 0)),
        _sp((d_model, 3 * dh), vec, single=True),
        grid=(seq // dx_rows, 1, 1), tb=True,
        extras=[(dz1, _sp((dx_rows, d_model), row_full)), after(token_mix)],
        out_shape=[sds((seq, d_model), F32)], out_specs=[_sp((dx_rows, d_model), row_full)],
        epilogue=dx_epi,
        side=_adamw_side(r_wout[0], r_wout[1], w_out[0], m_w_out[0], v_w_out[0], seq // dx_rows))
    o_wout = [r.reshape(w_out.shape) for r in o_wout]
    r_win, r_wpool = scatter_mix.wait(after=dx)
    o_win = adam_big("adam_w_in", r_win, w_in, m_w_in, v_w_in)
    o_wpool = adam_big("adam_w_pool", r_wpool, w_pool, m_w_pool, v_w_pool)

    small_gather.relay(after=o_win[0])
    (small_all,) = small_gather.wait(after=o_wpool[0])

    rep_w = [w_rg_a, w_rg_i, ln_mix_g, ln_mix_b, ln_ffn_g, ln_ffn_b, pool_scale, conv_b]
    rep_m = [m_w_rg_a, m_w_rg_i, m_ln_mix_g, m_ln_mix_b, m_ln_ffn_g, m_ln_ffn_b, m_pool_scale, m_conv_b]
    rep_v = [v_w_rg_a, v_w_rg_i, v_ln_mix_g, v_ln_mix_b, v_ln_ffn_g, v_ln_ffn_b, v_pool_scale, v_conv_b]
    cat = lambda arrs: jnp.concatenate([_rows128(a) for a in arrs], axis=0)
    o_rep = _sum_adamw("adam_replicated", None, small_all, cat(rep_w), cat(rep_m), cat(rep_v))

    my_idx = _dev_index(_where_am_i())
    head_parts = lax.dynamic_slice_in_dim(small_all, n_rep + my_idx * SMALL_ROWS, SMALL_ROWS, axis=1)
    o_head = _sum_adamw("adam_head", None, head_parts, pack_mine,
                        small_pack(m_conv_w, m_b_rg_a, m_b_rg_i, m_rg_lambda),
                        small_pack(v_conv_w, v_b_rg_a, v_b_rg_i, v_rg_lambda))

    def unpack_rep(packed):
        out, r = [], 0
        for wgt, rows in zip(rep_w, rep_rows):
            out.append(packed[r:r + rows].reshape(wgt.shape))
            r += rows
        return out

    def unpack_head(packed):
        return [packed[0:4].reshape(conv_w.shape), packed[4:6].reshape(b_rg_a.shape),
                packed[6:8].reshape(b_rg_i.shape), packed[8:10].reshape(rg_lambda.shape)]

    loss = _sum_blocks("loss_sum", small_all[:, n_rep + N_HEADS * SMALL_ROWS:, :])[0, 0]

    outs = [loss, dx[None]]
    for kind in range(4):
        ra, ri, mg, mb, fg, fb, ps, cb = unpack_rep(o_rep[kind])
        cw, ba, bi, lam = unpack_head(o_head[kind])
        outs += [mg, mb, o_win[kind], o_wpool[kind], ps, cw, cb, ra, ba, ri, bi, lam, o_wout[kind], fg, fb,
                 o_w1[kind], o_w2[kind]]
    return tuple(outs)
```

```python
import functools

import jax
import jax.numpy as jnp
from jax import lax
from jax.experimental import pallas as pl
from jax.experimental.pallas import tpu as pltpu

F32 = jnp.float32
BF16 = jnp.bfloat16
MESH = pl.DeviceIdType.MESH
ANY = pl.BlockSpec(memory_space=pl.ANY)

N_DEV = 8
POOL_WINDOWS = (2, 4, 8, 16)
N_HEADS = 8
RG_C = 8.0
LN_EPS = 1e-5
ALPHA = 2.0 ** 0.25
ADAM_LR = 0.001
ADAM_B1 = 0.9
ADAM_B2 = 0.999
ADAM_EPS = 1e-08
ADAM_WD = 0.01
ADAM_STEP = 10

VMEM_LIMIT = 56 * 1024 * 1024
SEQ_CHUNK = 256
POOL_CHUNK = 512
WIN_HALO = 16
CONV_HALO = 8
SMALL_ROWS = 16


def _params(n_grid):
    return pltpu.CompilerParams(dimension_semantics=("arbitrary",) * n_grid, vmem_limit_bytes=VMEM_LIMIT)


def _shift(v, j):
    n = v.shape[0]
    s = (-j) % n
    return v if s == 0 else pltpu.roll(v, s, 0)


def _sigmoid(x):
    return 0.5 * jnp.tanh(0.5 * x) + 0.5


def _softplus(z):
    e = jnp.exp(-jnp.abs(z))
    u = 1.0 + e
    log1p = jnp.where(u == 1.0, e, jnp.log(u) * (e / jnp.where(u == 1.0, 1.0, u - 1.0)))
    return jnp.maximum(z, 0.0) + log1p


_GELU_C = 0.7978845608028654
_GELU_K = 0.044715


def _gelu_and_grad(x):
    x2 = x * x
    t = jnp.tanh(_GELU_C * (x + _GELU_K * x * x2))
    g = 0.5 * x * (1.0 + t)
    dg = 0.5 * (1.0 + t) + 0.5 * x * (1.0 - t * t) * (_GELU_C * (1.0 + 3.0 * _GELU_K * x2))
    return g, dg


def _ln_fwd(z, g, b):
    mu = jnp.mean(z, axis=-1, keepdims=True)
    zc = z - mu
    var = jnp.mean(zc * zc, axis=-1, keepdims=True)
    rstd = lax.rsqrt(var + LN_EPS)
    xhat = zc * rstd
    return xhat * g + b, xhat, rstd


def _ln_bwd(dy, xhat, rstd, g):
    dxhat = dy * g
    m1 = jnp.mean(dxhat, axis=-1, keepdims=True)
    m2 = jnp.mean(dxhat * xhat, axis=-1, keepdims=True)
    dz = rstd * (dxhat - m1 - xhat * m2)
    dg = jnp.sum(dy * xhat, axis=0, keepdims=True)
    db = jnp.sum(dy, axis=0, keepdims=True)
    return dz, dg, db


def _acc_rows(ref, first, val):
    @pl.when(first)
    def _():
        ref[...] = val

    @pl.when(jnp.logical_not(first))
    def _():
        ref[...] += val


def _sp(shape, fn, single=False):
    return shape, fn, single


def _matmul(name, a, b, a_spec, b_spec, *, grid, j_outer=False, ta=False, tb=False, extras=(), out_shape, out_specs,
            epilogue=None, n_split=1, side=None):
    ni, nj, nk = grid
    n_ex = len(extras)
    dims = (((0 if ta else 1,), (1 if tb else 0,)), ((), ()))
    side_in, side_shape, side_out, side_fn = side if side is not None else ((), (), (), None)
    n_main_out = len(out_shape)
    inner = ni if j_outer else nj

    def mk(spec):
        shape, fn, single = spec
        index = (lambda g0, g1, g2: fn(g1, g0, g2)) if j_outer else fn
        return pl.BlockSpec(shape, index, pipeline_mode=pl.Buffered(1)) if single else pl.BlockSpec(shape, index)

    def mk_side(block, fn):
        return pl.BlockSpec(block, lambda g0, g1, g2: fn(g0 * inner + g1))

    def body(a_ref, b_ref, *rest):
        ex_refs = rest[:n_ex]
        out_refs = rest[n_ex + len(side_in):n_ex + len(side_in) + n_main_out]
        if side_fn is not None:
            side_fn(pl.program_id(0) * inner + pl.program_id(1), rest[n_ex:n_ex + len(side_in)],
                    rest[n_ex + len(side_in) + n_main_out:])
        i = pl.program_id(1 if j_outer else 0)
        if n_split > 1:
            av = a_ref[...].astype(BF16)
            width = b_ref.shape[0 if tb else 1] // n_split
            for c in range(n_split):
                cols = pl.ds(c * width, width)
                bv = (b_ref[cols, :] if tb else b_ref[:, cols]).astype(BF16)
                epilogue(lax.dot_general(av, bv, dims, preferred_element_type=F32), i, ex_refs, out_refs, cols)
            return
        part = lax.dot_general(a_ref[...].astype(BF16), b_ref[...].astype(BF16), dims, preferred_element_type=F32)
        if nk == 1:
            epilogue(part, i, ex_refs, out_refs)
        else:
            @pl.when(pl.program_id(2) == 0)
            def _():
                out_refs[0][...] = part

            @pl.when(pl.program_id(2) > 0)
            def _():
                out_refs[0][...] += part

    return pl.pallas_call(
        body, name=name, grid=(nj, ni, nk) if j_outer else (ni, nj, nk),
        in_specs=[mk(a_spec), mk(b_spec)] + [mk(s) for _, s in extras] + [mk_side(blk, fn) for _, blk, fn in side_in],
        out_specs=[mk(s) for s in out_specs] + [mk_side(blk, fn) for blk, fn in side_out],
        out_shape=list(out_shape) + list(side_shape),
        compiler_params=_params(3),
    )(a, b, *[x for x, _ in extras], *[x for x, _, _ in side_in])


def _bs(shape, fn):
    return pl.BlockSpec(shape, fn)


def _where_am_i():
    x, y, c = lax.axis_index("x"), lax.axis_index("y"), lax.axis_index("c")
    return x, y, c


def _dev_index(p):
    return 4 * p[0] + 2 * p[1] + p[2]


def _slab(ref, axis, idx, size):
    sl = [slice(None)] * len(ref.shape)
    sl[axis] = pl.ds(idx * size, size)
    return ref.at[tuple(sl)]


HBM = pl.BlockSpec(memory_space=pltpu.HBM)
SEM = pl.BlockSpec(memory_space=pltpu.SEMAPHORE)
DATAFLOW = pltpu.SideEffectType.DATAFLOW_SIDE_EFFECTING


def _in_hbm(a):
    return pltpu.with_memory_space_constraint(a, pltpu.HBM)


def _token_shape():
    return jax.ShapeDtypeStruct((8, 128), F32)


def _split_start(name, n_sems, bufs, issue):
    nb = len(bufs)

    def body(*refs):
        issue(refs[:nb], refs[nb], refs[nb + 1])
        refs[-1][...] = jnp.zeros((8, 128), F32)

    outs = pl.pallas_call(
        body, name=name,
        out_shape=(pltpu.SemaphoreType.DMA((n_sems,)), pltpu.SemaphoreType.DMA((n_sems,)),
                   *[pltpu.HBM(b.shape, b.dtype) for b in bufs], _token_shape()),
        in_specs=[HBM] * nb, out_specs=(SEM, SEM, *[HBM] * nb, pl.BlockSpec(memory_space=pltpu.VMEM)),
        input_output_aliases={i: 2 + i for i in range(nb)},
        compiler_params=pltpu.CompilerParams(has_side_effects=DATAFLOW),
    )(*[_in_hbm(b) for b in bufs])
    return outs[0], outs[1], list(outs[2:2 + nb]), outs[-1]


def _split_relay(name, n_sems, sems, bufs, after, relay):
    nb = len(bufs)

    def body(*refs):
        relay(refs[:nb], refs[nb], refs[nb + 1], refs[nb + 3], refs[nb + 4])
        refs[-1][...] = jnp.zeros((8, 128), F32)

    outs = pl.pallas_call(
        body, name=name,
        out_shape=(pltpu.SemaphoreType.DMA((n_sems,)), pltpu.SemaphoreType.DMA((n_sems,)),
                   *[pltpu.HBM(b.shape, b.dtype) for b in bufs], _token_shape()),
        in_specs=[HBM] * nb + [SEM, SEM, ANY],
        out_specs=(SEM, SEM, *[HBM] * nb, pl.BlockSpec(memory_space=pltpu.VMEM)),
        input_output_aliases={i: 2 + i for i in range(nb)},
        compiler_params=pltpu.CompilerParams(has_side_effects=DATAFLOW),
    )(*bufs, sems[0], sems[1], after)
    return outs[0], outs[1], list(outs[2:2 + nb]), outs[-1]


def _split_wait(name, sems, bufs, after, finish):
    nb = len(bufs)

    def body(*refs):
        finish(refs[:nb], refs[nb], refs[nb + 1])

    outs = pl.pallas_call(
        body, name=name, out_shape=[pltpu.HBM(b.shape, b.dtype) for b in bufs],
        in_specs=[HBM] * nb + [SEM, SEM, ANY], out_specs=[HBM] * nb,
        input_output_aliases={i: i for i in range(nb)},
        compiler_params=pltpu.CompilerParams(has_side_effects=DATAFLOW),
    )(*bufs, sems[0], sems[1], after)
    return list(outs)


def _place(name, items, dtype, after):
    ids = jnp.reshape(_dev_index(_where_am_i()), (1,)).astype(jnp.int32)
    outs = []
    for a, (shard, axis) in enumerate(items):
        rows, cols = shard.shape[-2], shard.shape[-1]
        tr = rows
        while tr * cols * shard.dtype.itemsize > 4 * 1024 * 1024 and tr % 32 == 0:
            tr //= 2
        nt = rows // tr
        full = list(shard.shape)
        full[axis] *= N_DEV
        if shard.ndim == 2 and axis == 0:
            in_spec = _bs((tr, cols), lambda i, ids: (i, 0))
            out_spec = _bs((tr, cols), lambda i, ids, nt=nt: (ids[0] * nt + i, 0))
        elif shard.ndim == 2 and axis == 1:
            in_spec = _bs((tr, cols), lambda i, ids: (i, 0))
            out_spec = _bs((tr, cols), lambda i, ids: (i, ids[0]))
        elif shard.ndim == 3 and axis == 1:
            tr, nt = rows, shard.shape[0]
            in_spec = _bs((None, rows, cols), lambda i, ids: (i, 0, 0))
            out_spec = _bs((None, rows, cols), lambda i, ids: (i, ids[0], 0))
        else:
            assert shard.ndim == 3 and axis == 0 and shard.shape[0] == 1
            in_spec = _bs((None, tr, cols), lambda i, ids: (0, i, 0))
            out_spec = _bs((None, tr, cols), lambda i, ids: (ids[0], i, 0))

        def body(ids_ref, in_ref, after_ref, out_ref):
            del ids_ref, after_ref
            out_ref[...] = in_ref[...].astype(out_ref.dtype)

        outs.append(pl.pallas_call(
            body, name=f"{name}{a}",
            grid_spec=pltpu.PrefetchScalarGridSpec(
                num_scalar_prefetch=1, grid=(nt,), in_specs=[in_spec, ANY], out_specs=out_spec),
            out_shape=jax.ShapeDtypeStruct(tuple(full), dtype), compiler_params=_params(1),
        )(ids, shard, after))
    return outs


class _SplitGather:
    def __init__(self, name, items, dtype, after):
        self.name, self.items, self.n = name, items, len(items)
        fulls = _place(name + "_place", items, dtype, after)
        n = self.n

        def issue(refs, send, recv):
            me, sibling, chips, c = self._geometry()
            for a in range(n):
                self._copy1(refs, send, recv, a, 0, me, sibling).start()
                for j, chip in enumerate(chips):
                    self._copy1(refs, send, recv, a, 1 + j, me, (*chip, c)).start()

        self.send, self.recv, self.bufs, self.token = _split_start(name + "_start", 4 * n, fulls, issue)

    @staticmethod
    def _geometry():
        x, y, c = _where_am_i()
        return (x, y, c), (x, y, 1 - c), [(1 - x, y), (x, 1 - y), (1 - x, 1 - y)], c

    def _blk(self, refs, a, p):
        shard, axis = self.items[a]
        return _slab(refs[a], axis, _dev_index(p), shard.shape[axis])

    def _copy1(self, refs, send, recv, a, k, owner, to):
        return pltpu.make_async_remote_copy(
            src_ref=self._blk(refs, a, owner), dst_ref=self._blk(refs, a, owner), send_sem=send.at[4 * a + k],
            recv_sem=recv.at[4 * a + k], device_id=to, device_id_type=MESH)

    def _copy2(self, refs, send, recv, a, j, owner, to):
        return pltpu.make_async_remote_copy(
            src_ref=self._blk(refs, a, owner), dst_ref=self._blk(refs, a, owner), send_sem=send.at[3 * a + j],
            recv_sem=recv.at[3 * a + j], device_id=to, device_id_type=MESH)

    def relay(self, after):
        n = self.n

        def relay(refs, send_in, recv_in, send_out, recv_out):
            me, sibling, chips, c = self._geometry()
            for a in range(n):
                for j, chip in enumerate(chips):
                    self._copy1(refs, send_in, recv_in, a, 1 + j, (*chip, c), me).wait_recv()
                    self._copy2(refs, send_out, recv_out, a, j, (*chip, c), sibling).start()
            for a in range(n):
                self._copy1(refs, send_in, recv_in, a, 0, sibling, me).wait_recv()
                for k in range(4):
                    self._copy1(refs, send_in, recv_in, a, k, me, sibling).wait_send()

        self.send, self.recv, self.bufs, self.token = _split_relay(
            self.name + "_relay", 3 * n, (self.send, self.recv), self.bufs, after, relay)
        return self.token

    def wait(self, after):
        n = self.n

        def finish(refs, send, recv):
            me, sibling, chips, c = self._geometry()
            for a in range(n):
                for j, chip in enumerate(chips):
                    self._copy2(refs, send, recv, a, j, (*chip, 1 - c), me).wait_recv()
                    self._copy2(refs, send, recv, a, j, (*chip, c), sibling).wait_send()

        return _split_wait(self.name + "_wait", (self.send, self.recv), self.bufs, after, finish)


class _SplitReduceScatter:
    def __init__(self, name, grads):
        self.name, self.n = name, len(grads)
        n = self.n
        g4 = [g.reshape(4, 2, *g.shape[1:]) for g in grads]
        land = [lax.empty((4, 1, *g.shape[1:]), g.dtype) for g in grads]

        def issue(refs, send, recv):
            for a in range(n):
                self._swap(refs, send, recv, a).start()

        self.send, self.recv, self.bufs, self.token = _split_start(name + "_d2d_start", n, g4 + land, issue)

    def _swap(self, refs, send, recv, a):
        x, y, c = _where_am_i()
        return pltpu.make_async_remote_copy(
            src_ref=refs[a].at[:, pl.ds(1 - c, 1)], dst_ref=refs[self.n + a], send_sem=send.at[a], recv_sem=recv.at[a],
            device_id=(x, y, 1 - c), device_id_type=MESH)

    def _hop(self, refs, send, recv, a, m):
        x, y, c = _where_am_i()
        px = (1 - x) if m & 2 else x
        py = (1 - y) if m & 1 else y
        return pltpu.make_async_remote_copy(
            src_ref=refs[a].at[2 * px + py], dst_ref=refs[self.n + a].at[m - 1], send_sem=send.at[3 * a + m - 1],
            recv_sem=recv.at[3 * a + m - 1], device_id=(px, py, c), device_id_type=MESH)

    def combine_and_send(self, after):
        n = self.n

        def finish(refs, send, recv):
            for a in range(n):
                self._swap(refs, send, recv, a).wait()

        bufs = _split_wait(self.name + "_d2d_wait", (self.send, self.recv), self.bufs, after, finish)
        x, y, c = _where_am_i()
        ids = jnp.stack([c, 2 * x + y]).astype(jnp.int32)
        self.own, sums = [], []
        for a in range(n):
            own, hb = _pair_sum(f"{self.name}_sum{a}", bufs[a], bufs[n + a], ids)
            self.own.append(own)
            sums.append(hb)
        land = [lax.empty((3, *h.shape[1:]), h.dtype) for h in sums]

        def issue(refs, send, recv):
            for a in range(n):
                for m in (1, 2, 3):
                    self._hop(refs, send, recv, a, m).start()

        self.send, self.recv, self.bufs, self.token = _split_start(self.name + "_ici_start", 3 * n, sums + land, issue)
        return self.token

    def wait(self, after):
        n = self.n

        def finish(refs, send, recv):
            for a in range(n):
                for m in (1, 2, 3):
                    self._hop(refs, send, recv, a, m).wait()

        bufs = _split_wait(self.name + "_ici_wait", (self.send, self.recv), self.bufs, after, finish)
        return list(zip(self.own, bufs[n:]))


def _pair_sum(name, g4, land, ids):
    rows, cols = g4.shape[2], g4.shape[3]
    tr = rows
    while tr * cols * 2 > 2 * 1024 * 1024 and tr % 32 == 0:
        tr //= 2

    def body(ids_ref, g_ref, l_ref, own_ref, sum_ref):
        h = g_ref[...].astype(F32) + l_ref[...].astype(F32)
        sum_ref[...] = h.astype(sum_ref.dtype)

        @pl.when(pl.program_id(1) == ids_ref[1])
        def _():
            own_ref[...] = h

    return pl.pallas_call(
        body, name=name,
        grid_spec=pltpu.PrefetchScalarGridSpec(
            num_scalar_prefetch=1, grid=(rows // tr, 4),
            in_specs=[_bs((None, None, tr, cols), lambda i, q, ids: (q, ids[0], i, 0)),
                      _bs((None, None, tr, cols), lambda i, q, ids: (q, 0, i, 0))],
            out_specs=[_bs((tr, cols), lambda i, q, ids: (i, 0)), _bs((None, tr, cols), lambda i, q, ids: (q, i, 0))]),
        out_shape=[jax.ShapeDtypeStruct((rows, cols), F32), jax.ShapeDtypeStruct((4, rows, cols), g4.dtype)],
        compiler_params=_params(2),
    )(ids, g4, land)


def _win_sum(ext, w, off):
    s = ext + _shift(ext, -1)
    if w >= 4:
        s = _shift(s, -1) + _shift(s, 1)
    if w >= 8:
        s = _shift(s, -2) + _shift(s, 2)
    if w >= 16:
        s = _shift(s, -4) + _shift(s, 4)
    return _shift(s, off) if off else s


def _inv_count(r0, t, w, seq):
    pos = r0 + lax.broadcasted_iota(jnp.int32, (t, 1), 0)
    cnt = jnp.minimum(pos + w // 2, seq) - jnp.maximum(pos - w // 2, 0)
    return 1.0 / cnt.astype(F32)


def _pool_fwd(p3, w_pool, pool_scale, seq, d_model):
    dp = d_model // 2
    pg = dp // len(POOL_WINDOWS)
    t = min(POOL_CHUNK, seq)
    n_chunks = seq // t
    h = WIN_HALO

    def body(u_ref, w_ref, sc_ref, d_ref, y_ref, pad_ref):
        g = pl.program_id(0)
        zeros = jnp.zeros((h, pg), F32)
        pad_ref[0:h, :] = zeros
        pad_ref[h + seq:h + seq + h, :] = zeros

        def fill(ci, _):
            r0 = pl.multiple_of(ci * t, t)
            pad_ref[pl.ds(h + r0, t), :] = u_ref[pl.ds(r0, t), :]
            return 0

        lax.fori_loop(0, n_chunks, fill, 0)
        wmat = w_ref[...]
        scale = sc_ref[...]
        for gi, w in enumerate(POOL_WINDOWS):
            @pl.when(g == gi)
            def _(w=w):
                def chunk(ci, _):
                    r0 = pl.multiple_of(ci * t, t)
                    ext = pad_ref[pl.ds(r0, t + 2 * h), :]
                    mean = _win_sum(ext, w, 0)[h:h + t, :] * _inv_count(r0, t, w, seq)
                    d = (mean - ext[h:h + t, :]).astype(BF16)
                    d_ref[pl.ds(r0, t), :] = d
                    q = jnp.dot(d, wmat, preferred_element_type=F32)
                    y_ref[pl.ds(r0, t), :] = (q * scale).astype(BF16)
                    return 0

                lax.fori_loop(0, n_chunks, chunk, 0, unroll=2)

    return pl.pallas_call(
        body, name="pool_fwd", grid=(len(POOL_WINDOWS),),
        in_specs=[_bs((None, seq, pg), lambda g: (0, 0, g)), _bs((None, pg, pg), lambda g: (g, 0, 0)),
                  _bs((1, pg), lambda g: (0, g))],
        out_specs=[_bs((seq, pg), lambda g: (0, g)), _bs((seq, pg), lambda g: (0, g))],
        out_shape=[jax.ShapeDtypeStruct((seq, dp), BF16), jax.ShapeDtypeStruct((seq, d_model), BF16)],
        scratch_shapes=[pltpu.VMEM((seq + 2 * h, pg), F32)],
        compiler_params=_params(1),
    )(p3, w_pool, pool_scale)


def _pool_bwd(d, dy, w_pool, pool_scale, token, seq, d_model):
    dp = d_model // 2
    pg = dp // len(POOL_WINDOWS)
    t = min(POOL_CHUNK, seq)
    n_chunks = seq // t
    h = WIN_HALO
    tn_dims = (((0,), (0,)), ((), ()))
    nt_dims = (((1,), (1,)), ((), ()))

    def body(d_ref, dy_ref, w_ref, sc_ref, tok_ref, du_ref, dwb_ref, dsc_ref, pad_ref, dd_ref, dw_ref):
        del tok_ref
        g = pl.program_id(0)
        zeros = jnp.zeros((h, pg), F32)
        pad_ref[0:h, :] = zeros
        pad_ref[h + seq:h + seq + h, :] = zeros
        wmat = w_ref[...]
        scale = sc_ref[...]
        for gi, w in enumerate(POOL_WINDOWS):
            @pl.when(g == gi)
            def _(w=w):
                dw_ref[...] = jnp.zeros((pg, pg), F32)

                def first(ci, dsc):
                    r0 = pl.multiple_of(ci * t, t)
                    dv = d_ref[pl.ds(r0, t), :]
                    dyv = dy_ref[pl.ds(r0, t), :]
                    q = jnp.dot(dv, wmat, preferred_element_type=F32)
                    dsc = dsc + jnp.sum(dyv * q, axis=0, keepdims=True)
                    dq = (dyv * scale).astype(BF16)
                    dw_ref[...] += lax.dot_general(dv, dq, tn_dims, preferred_element_type=F32)
                    dd = lax.dot_general(dq, wmat, nt_dims, preferred_element_type=F32)
                    dd_ref[pl.ds(r0, t), :] = dd
                    pad_ref[pl.ds(h + r0, t), :] = dd * _inv_count(r0, t, w, seq)
                    return dsc

                def first_pair(cj, dsc):
                    return first(2 * cj + 1, first(2 * cj, dsc))

                dsc_ref[...] = lax.fori_loop(0, n_chunks // 2, first_pair, jnp.zeros((1, pg), F32))
                dwb_ref[...] = dw_ref[...].reshape(N_DEV, pg // N_DEV, pg).astype(BF16)

                def second(ci, _):
                    r0 = pl.multiple_of(ci * t, t)
                    ext = pad_ref[pl.ds(r0, t + 2 * h), :]
                    back = _win_sum(ext, w, 1)[h:h + t, :]
                    du_ref[pl.ds(r0, t), :] = (back - dd_ref[pl.ds(r0, t), :]).astype(BF16)
                    return 0

                lax.fori_loop(0, n_chunks, second, 0, unroll=2)

    return pl.pallas_call(
        body, name="pool_bwd", grid=(len(POOL_WINDOWS),),
        in_specs=[_bs((seq, pg), lambda g: (0, g)), _bs((seq, pg), lambda g: (0, g)),
                  _bs((None, pg, pg), lambda g: (g, 0, 0)), _bs((1, pg), lambda g: (0, g)),
                  _bs((8, 128), lambda g: (0, 0))],
        out_specs=[_bs((seq, pg), lambda g: (0, g)), _bs((N_DEV, None, pg // N_DEV, pg), lambda g: (0, g, 0, 0)),
                   _bs((1, pg), lambda g: (0, g))],
        out_shape=[jax.ShapeDtypeStruct((seq, 3 * dp), BF16),
                   jax.ShapeDtypeStruct((N_DEV, len(POOL_WINDOWS), pg // N_DEV, pg), BF16),
                   jax.ShapeDtypeStruct((1, dp), F32)],
        scratch_shapes=[pltpu.VMEM((seq + 2 * h, pg), F32), pltpu.VMEM((seq, pg), F32), pltpu.VMEM((pg, pg), F32)],
        compiler_params=_params(1),
    )(d, dy, w_pool, pool_scale, token)


def _tile_scan(n_tiles, lanes, loads, stores):
    row = lax.broadcasted_iota(jnp.int32, (8, lanes), 0)
    group = 8

    def local_scan(n, k):
        aa, bb = loads[n](k)
        for sh in (1, 2, 4):
            if n == 0:
                ok = row >= sh
                ap = jnp.where(ok, pltpu.roll(aa, sh, 0), 1.0)
                bp = jnp.where(ok, pltpu.roll(bb, sh, 0), 0.0)
            else:
                ok = row < 8 - sh
                ap = jnp.where(ok, pltpu.roll(aa, 8 - sh, 0), 1.0)
                bp = jnp.where(ok, pltpu.roll(bb, 8 - sh, 0), 0.0)
            bb = aa * bp + bb
            aa = aa * ap
        return aa, bb

    def step(s, carry):
        carry = list(carry)
        for n in range(2):
            tiles = [s * group + u if n == 0 else n_tiles - 1 - (s * group + u) for u in range(group)]
            local = [local_scan(n, k) for k in tiles]
            for k, (aa, bb) in zip(tiles, local):
                hh = bb + aa * carry[n]
                stores[n](k, hh)
                carry[n] = jnp.broadcast_to(hh[7:8, :] if n == 0 else hh[0:1, :], (8, lanes))
        return tuple(carry)

    zeros = jnp.zeros((8, lanes), F32)
    lax.fori_loop(0, n_tiles // group, step, (zeros, zeros))


def _gate_preacts(xc, wcat_ref):
    xcb = xc.astype(BF16)
    return xcb, jnp.dot(xcb, wcat_ref[...], preferred_element_type=F32)


def _gates(pre, n, pk_ref, sp):
    lh = pre.shape[1] // 4
    r = _sigmoid(pre[:, (2 * n) * lh:(2 * n + 1) * lh] + pk_ref[pl.ds(4 + n, 1), :])
    i = _sigmoid(pre[:, (2 * n + 1) * lh:(2 * n + 2) * lh] + pk_ref[pl.ds(6 + n, 1), :])
    log_a = (-RG_C * r) * sp[n]
    a = jnp.exp(log_a)
    x = 2.0 * log_a
    one_minus_a2 = jnp.where(x > -0.01, -(x * (1.0 + x * (0.5 + x * (1.0 / 6.0)))), 1.0 - a * a)
    m = jnp.sqrt(one_minus_a2)
    return r, i, a, m


def _conv_chunk(upad_ref, pk_ref, cb, r0, t):
    ext = upad_ref[pl.ds(r0, t + 2 * CONV_HALO), :]
    acc = pk_ref[pl.ds(1, 1), :] * ext
    for k in (0, 2, 3):
        acc = acc + pk_ref[pl.ds(k, 1), :] * _shift(ext, k - 1)
    return acc[CONV_HALO:CONV_HALO + t, :] + cb, ext


def _lru_fwd(p3, y_in, pack, conv_b, wcat, token, seq, d_model):
    dl = d_model // 2
    lh = dl // N_HEADS
    t = min(SEQ_CHUNK, seq)
    n_chunks = seq // t
    hal = CONV_HALO
    first_rec_block = (d_model - dl) // lh

    def body(ur_ref, ug_ref, pk_ref, cb_ref, wcat_ref, yin_ref, tok_ref, y_ref, h0_ref, h1_ref,
             upad, a_scr, b_scr):
        del yin_ref, tok_ref
        zeros = jnp.zeros((hal, lh), F32)
        upad[0:hal, :] = zeros
        upad[hal + seq:hal + seq + hal, :] = zeros
        for ref in (h0_ref, h1_ref):
            ref[0:hal, :] = zeros
            ref[hal + seq:hal + seq + hal, :] = zeros

        def fill(ci, _):
            r0 = pl.multiple_of(ci * t, t)
            upad[pl.ds(hal + r0, t), :] = ur_ref[pl.ds(r0, t), :]
            return 0

        lax.fori_loop(0, n_chunks, fill, 0)
        cb = cb_ref[...]
        sp = [_softplus(-pk_ref[pl.ds(8 + n, 1), :]) for n in range(2)]

        def chunk(ci, _):
            r0 = pl.multiple_of(ci * t, t)
            xc, _ext = _conv_chunk(upad, pk_ref, cb, r0, t)
            _, pre = _gate_preacts(xc, wcat_ref)
            for n in range(2):
                _, i, a, m = _gates(pre, n, pk_ref, sp)
                a_scr[n, pl.ds(r0, t), :] = a
                b_scr[n, pl.ds(r0, t), :] = (m * i) * xc
            return 0

        lax.fori_loop(0, n_chunks, chunk, 0, unroll=2)

        def load(n):
            def get(k):
                at = pl.ds(pl.multiple_of(k * 8, 8), 8)
                return a_scr[n, at, :], b_scr[n, at, :]
            return get

        def store(ref):
            def put(k, v):
                ref[pl.ds(pl.multiple_of(hal + k * 8, 8), 8), :] = v
            return put

        _tile_scan(seq // 8, lh, [load(0), load(1)], [store(h0_ref), store(h1_ref)])

        def out(ci, _):
            r0 = pl.multiple_of(ci * t, t)
            hsum = h0_ref[pl.ds(hal + r0, t), :] + h1_ref[pl.ds(hal + r0, t), :]
            gl, _dg = _gelu_and_grad(ug_ref[pl.ds(r0, t), :])
            y_ref[pl.ds(r0, t), :] = (hsum * gl).astype(BF16)
            return 0

        lax.fori_loop(0, n_chunks, out, 0)

    return pl.pallas_call(
        body, name="lru_fwd", grid=(N_HEADS,),
        in_specs=[_bs((None, seq, lh), lambda h: (1, 0, h)), _bs((None, seq, lh), lambda h: (2, 0, h)),
                  _bs((None, SMALL_ROWS, lh), lambda h: (h, 0, 0)), _bs((1, lh), lambda h: (0, h)),
                  _bs((None, lh, 4 * lh), lambda h: (h, 0, 0)),
                  ANY, _bs((8, 128), lambda h: (0, 0))],
        out_specs=[_bs((seq, lh), lambda h: (0, first_rec_block + h)),
                   _bs((seq + 2 * hal, lh), lambda h: (0, h)), _bs((seq + 2 * hal, lh), lambda h: (0, h))],
        out_shape=[jax.ShapeDtypeStruct((seq, d_model), BF16), jax.ShapeDtypeStruct((seq + 2 * hal, dl), F32),
                   jax.ShapeDtypeStruct((seq + 2 * hal, dl), F32)],
        scratch_shapes=[pltpu.VMEM((seq + 2 * hal, lh), F32), pltpu.VMEM((2, seq, lh), F32),
                        pltpu.VMEM((2, seq, lh), F32)],
        input_output_aliases={5: 0},
        compiler_params=_params(1),
    )(p3, p3, pack, conv_b, wcat, y_in, token)


def _lru_bwd(p3, dy, h0p, h1p, dproj_in, pack, conv_b, wcat, token, seq, d_model):
    dl = d_model // 2
    lh = dl // N_HEADS
    t = min(SEQ_CHUNK, seq)
    n_chunks = seq // t
    hal = CONV_HALO
    first_rec_block = (d_model - dl) // lh
    tn_dims = (((0,), (0,)), ((), ()))
    nt_dims = (((1,), (1,)), ((), ()))

    def body(ur_ref, ug_ref, dy_ref, h0_ref, h1_ref, pk_ref, cb_ref, wcat_ref, tok_ref, din_ref,
             dproj_ref, dpk_ref, dcb_ref, dwcat_ref,
             upad, a_scr, dh_scr, g_scr, dxc_pad, dpr_ref, out_sems, gate_scr):
        del din_ref, tok_ref
        zeros = jnp.zeros((hal, lh), F32)
        for ref in (upad, dxc_pad):
            ref[0:hal, :] = zeros
            ref[hal + seq:hal + seq + hal, :] = zeros
        for n in range(2):
            a_scr[n, 0:hal, :] = zeros
            a_scr[n, hal + seq:hal + seq + hal, :] = zeros

        def fill(ci, _):
            r0 = pl.multiple_of(ci * t, t)
            upad[pl.ds(hal + r0, t), :] = ur_ref[pl.ds(r0, t), :]
            return 0

        lax.fori_loop(0, n_chunks, fill, 0)
        cb = cb_ref[...]
        lam = [pk_ref[pl.ds(8 + n, 1), :] for n in range(2)]
        sp = [_softplus(-lam[n]) for n in range(2)]

        def chunk1(ci, _):
            r0 = pl.multiple_of(ci * t, t)
            xc, _ext = _conv_chunk(upad, pk_ref, cb, r0, t)
            _, pre = _gate_preacts(xc, wcat_ref)
            for n in range(2):
                r, i, a, m = _gates(pre, n, pk_ref, sp)
                a_scr[n, pl.ds(hal + r0, t), :] = a
                for q, v in enumerate((r, i, m)):
                    gate_scr[3 * n + q, pl.ds(r0, t), :] = v
            hsum = h0_ref[pl.ds(hal + r0, t), :] + h1_ref[pl.ds(hal + r0, t), :]
            gl, dgl = _gelu_and_grad(ug_ref[pl.ds(r0, t), :])
            dyv = dy_ref[pl.ds(r0, t), :]
            dh_scr[pl.ds(r0, t), :] = dyv * gl
            dpr_ref[1, pl.ds(r0, t), :] = ((dyv * hsum) * dgl).astype(BF16)
            return 0

        lax.fori_loop(0, n_chunks, chunk1, 0, unroll=2)

        def load(n):
            def get(k):
                r0 = pl.multiple_of(k * 8, 8)
                if n == 0:
                    coef = _shift(a_scr[0, pl.ds(pl.multiple_of(hal + r0, 8), 16), :], 1)[0:8, :]
                else:
                    coef = _shift(a_scr[1, pl.ds(pl.multiple_of(hal + r0 - 8, 8), 16), :], -1)[8:16, :]
                return coef, dh_scr[pl.ds(r0, 8), :]
            return get

        def store(n):
            def put(k, v):
                g_scr[n, pl.ds(pl.multiple_of(k * 8, 8), 8), :] = v
            return put

        _tile_scan(seq // 8, lh, [load(1), load(0)], [store(1), store(0)])

        dwcat_ref[...] = jnp.zeros((lh, 4 * lh), F32)

        def chunk3(ci, carry):
            dba, dbi, dlam, dcb = carry
            r0 = pl.multiple_of(ci * t, t)
            xc, _ext = _conv_chunk(upad, pk_ref, cb, r0, t)
            xcb = xc.astype(BF16)
            dxc = jnp.zeros((t, lh), F32)
            dba, dbi, dlam = list(dba), list(dbi), list(dlam)
            dpre = []
            for n in range(2):
                r, i, m = (gate_scr[3 * n + q, pl.ds(r0, t), :] for q in range(3))
                a = a_scr[n, pl.ds(hal + r0, t), :]
                hext = (h0_ref if n == 0 else h1_ref)[pl.ds(r0, t + 2 * hal), :]
                hprev = _shift(hext, -1 if n == 0 else 1)[hal:hal + t, :]
                gb = g_scr[n, pl.ds(r0, t), :]
                da = gb * hprev
                dm = gb * i * xc
                di = gb * m * xc
                dxc = dxc + gb * (m * i)
                dlog_a = da * a - dm * (a * a) / m
                dr = dlog_a * (-RG_C * sp[n])
                dlam[n] = dlam[n] + jnp.sum(dlog_a * r, axis=0, keepdims=True)
                dpr = dr * r * (1.0 - r)
                dpi = di * i * (1.0 - i)
                dba[n] = dba[n] + jnp.sum(dpr, axis=0, keepdims=True)
                dbi[n] = dbi[n] + jnp.sum(dpi, axis=0, keepdims=True)
                dpre += [dpr.astype(BF16), dpi.astype(BF16)]
            dpre = jnp.concatenate(dpre, axis=1)
            dwcat_ref[...] += lax.dot_general(xcb, dpre, tn_dims, preferred_element_type=F32)
            dxc = dxc + lax.dot_general(dpre, wcat_ref[...], nt_dims, preferred_element_type=F32)
            dxc_pad[pl.ds(hal + r0, t), :] = dxc
            dcb = dcb + jnp.sum(dxc, axis=0, keepdims=True)
            return tuple(dba), tuple(dbi), tuple(dlam), dcb

        zr = jnp.zeros((1, lh), F32)
        def chunk3_pair(cj, carry):
            return chunk3(2 * cj + 1, chunk3(2 * cj, carry))

        dba, dbi, dlam, dcb = lax.fori_loop(0, n_chunks // 2, chunk3_pair, ((zr, zr), (zr, zr), (zr, zr), zr))
        dcb_ref[...] = dcb
        for n in range(2):
            dpk_ref[pl.ds(4 + n, 1), :] = dba[n]
            dpk_ref[pl.ds(6 + n, 1), :] = dbi[n]
            dpk_ref[pl.ds(8 + n, 1), :] = dlam[n] * (RG_C * jax.nn.sigmoid(-lam[n]))
        dpk_ref[pl.ds(10, SMALL_ROWS - 10), :] = jnp.zeros((SMALL_ROWS - 10, lh), F32)

        def chunk4(ci, dtap):
            r0 = pl.multiple_of(ci * t, t)
            gext = dxc_pad[pl.ds(r0, t + 2 * hal), :]
            uext = upad[pl.ds(r0, t + 2 * hal), :]
            gmid = gext[hal:hal + t, :]
            du = pk_ref[pl.ds(1, 1), :] * gext
            for k in (0, 2, 3):
                du = du + pk_ref[pl.ds(k, 1), :] * _shift(gext, 1 - k)
            dpr_ref[0, pl.ds(r0, t), :] = du[hal:hal + t, :].astype(BF16)
            out = []
            for k in range(4):
                usl = _shift(uext, k - 1)[hal:hal + t, :]
                out.append(dtap[k] + jnp.sum(gmid * usl, axis=0, keepdims=True))
            return tuple(out)

        dtap = lax.fori_loop(0, n_chunks, chunk4, (zr, zr, zr, zr))
        for k in range(4):
            dpk_ref[pl.ds(k, 1), :] = dtap[k]

        head = pl.program_id(0)
        outs = [pltpu.make_async_copy(
            dpr_ref.at[b], dproj_ref.at[:, pl.ds(pl.multiple_of((1 + b) * dl + head * lh, lh), lh)], out_sems.at[b])
            for b in range(2)]
        for cp in outs:
            cp.start()
        for cp in outs:
            cp.wait()

    return pl.pallas_call(
        body, name="lru_bwd", grid=(N_HEADS,),
        in_specs=[_bs((None, seq, lh), lambda h: (1, 0, h)), _bs((None, seq, lh), lambda h: (2, 0, h)),
                  _bs((seq, lh), lambda h: (0, first_rec_block + h)),
                  _bs((seq + 2 * hal, lh), lambda h: (0, h)), _bs((seq + 2 * hal, lh), lambda h: (0, h)),
                  _bs((None, SMALL_ROWS, lh), lambda h: (h, 0, 0)), _bs((1, lh), lambda h: (0, h)),
                  _bs((None, lh, 4 * lh), lambda h: (h, 0, 0)),
                  _bs((8, 128), lambda h: (0, 0)), ANY],
        out_specs=[ANY, _bs((None, SMALL_ROWS, lh), lambda h: (h, 0, 0)),
                   _bs((1, lh), lambda h: (0, h)), _bs((None, lh, 4 * lh), lambda h: (h, 0, 0))],
        out_shape=[jax.ShapeDtypeStruct((seq, 3 * dl), BF16), jax.ShapeDtypeStruct((N_HEADS, SMALL_ROWS, lh), F32),
                   jax.ShapeDtypeStruct((1, dl), F32), jax.ShapeDtypeStruct((N_HEADS, lh, 4 * lh), F32)],
        scratch_shapes=[pltpu.VMEM((seq + 2 * hal, lh), F32), pltpu.VMEM((2, seq + 2 * hal, lh), F32),
                        pltpu.VMEM((seq, lh), F32), pltpu.VMEM((2, seq, lh), F32),
                        pltpu.VMEM((seq + 2 * hal, lh), F32), pltpu.VMEM((2, seq, lh), BF16),
                        pltpu.SemaphoreType.DMA((2,)), pltpu.VMEM((6, seq, lh), F32)],
        input_output_aliases={9: 0},
        compiler_params=_params(1),
    )(p3, p3, dy, h0p, h1p, pack, conv_b, wcat, token, dproj_in)


class _tiles:
    def __init__(self, seq, d_model, d_ff):
        self.rows = min(1024, seq)
        self.ln_rows = min(256, seq)
        self.ff_cols = min(2048, d_ff)
        self.ff_split = 4
        self.ff_k = min(2048, d_ff)
        self.grad_rows = 512


def _ln_loss_bwd(ffn, x1, tgt, g, b, tr):
    seq, d = ffn.shape

    def body(f_ref, x_ref, t_ref, g_ref, b_ref, dz_ref, dzb_ref, dg_ref, db_ref, loss_ref):
        i = pl.program_id(0)
        gv = g_ref[...]
        z = ALPHA * x_ref[...] + f_ref[...]
        y, xhat, rstd = _ln_fwd(z, gv, b_ref[...])
        err = y - t_ref[...]
        part = 0.5 * jnp.sum(jnp.mean(err * err, axis=-1, keepdims=True), axis=0, keepdims=True)
        dz, dg, db = _ln_bwd(err * (1.0 / d), xhat, rstd, gv)
        dz_ref[...] = dz
        dzb_ref[...] = dz.astype(BF16)
        _acc_rows(dg_ref, i == 0, dg)
        _acc_rows(db_ref, i == 0, db)
        _acc_rows(loss_ref, i == 0, jnp.broadcast_to(part, (8, 128)))

    row = _bs((tr, d), lambda i: (i, 0))
    vec = _bs((1, d), lambda i: (0, 0))
    return pl.pallas_call(
        body, name="ln_ffn_loss", grid=(seq // tr,), in_specs=[row, row, row, vec, vec],
        out_specs=[row, row, vec, vec, _bs((8, 128), lambda i: (0, 0))],
        out_shape=[jax.ShapeDtypeStruct((seq, d), F32), jax.ShapeDtypeStruct((seq, d), BF16),
                   jax.ShapeDtypeStruct((1, d), F32), jax.ShapeDtypeStruct((1, d), F32),
                   jax.ShapeDtypeStruct((8, 128), F32)],
        compiler_params=_params(1),
    )(ffn, x1, tgt, g, b)


def _ln_bwd_side(dx_branch, dres, z, g, b, n_steps):
    seq, d = z.shape
    tr = seq // n_steps

    def fn(step, ins, outs):
        a_ref, r_ref, z_ref, g_ref, b_ref = ins
        dz_ref, dzb_ref, dg_ref, db_ref = outs
        gv = g_ref[...]
        _, xhat, rstd = _ln_fwd(z_ref[...], gv, b_ref[...])
        dz, dg, db = _ln_bwd(ALPHA * r_ref[...] + a_ref[...], xhat, rstd, gv)
        dz_ref[...] = dz
        dzb_ref[...] = dz.astype(BF16)
        _acc_rows(dg_ref, step == 0, dg)
        _acc_rows(db_ref, step == 0, db)

    row = ((tr, d), lambda s: (s, 0))
    vec = ((1, d), lambda s: (0, 0))
    shapes = [jax.ShapeDtypeStruct((seq, d), F32), jax.ShapeDtypeStruct((seq, d), BF16),
              jax.ShapeDtypeStruct((1, d), F32), jax.ShapeDtypeStruct((1, d), F32)]
    return [(dx_branch, *row), (dres, *row), (z, *row), (g, *vec), (b, *vec)], shapes, [row, row, vec, vec], fn


def _to_bf16(name, a, token):
    rows, cols = a.shape
    tr = min(512, rows)

    def body(a_ref, tok_ref, o_ref):
        del tok_ref
        o_ref[...] = a_ref[...].astype(BF16)

    return pl.pallas_call(
        body, name=name, grid=(rows // tr,),
        in_specs=[_bs((tr, cols), lambda i: (i, 0)), _bs((8, 128), lambda i: (0, 0))],
        out_specs=_bs((tr, cols), lambda i: (i, 0)), out_shape=jax.ShapeDtypeStruct((rows, cols), BF16),
        compiler_params=_params(1),
    )(a, token)


def _sum_blocks(name, parts):
    def body(p_ref, o_ref):
        acc = p_ref[0]
        for s in range(1, parts.shape[0]):
            acc = acc + p_ref[s]
        o_ref[...] = acc

    return pl.pallas_call(body, name=name, out_shape=jax.ShapeDtypeStruct(parts.shape[1:], F32))(parts)


def _adamw_values(w, g, m, v):
    m = ADAM_B1 * m + (1.0 - ADAM_B1) * g
    v = ADAM_B2 * v + (1.0 - ADAM_B2) * (g * g)
    m_hat = m / (1.0 - ADAM_B1 ** ADAM_STEP)
    v_hat = v / (1.0 - ADAM_B2 ** ADAM_STEP)
    delta = -ADAM_LR * (m_hat / (jnp.sqrt(v_hat) + ADAM_EPS) + ADAM_WD * w)
    return delta, m, v


def _adamw_side(own, parts, w, m, v, n_steps):
    rows, cols = w.shape
    tr = rows // n_steps

    def fn(step, ins, outs):
        o_ref, p_ref, w_ref, m_ref, v_ref = ins
        g = o_ref[...]
        for s in range(parts.shape[0]):
            g = g + p_ref[s].astype(F32)
        delta, mn, vn = _adamw_values(w_ref[...], g, m_ref[...], v_ref[...])
        for ref, val in zip(outs, (g, delta, mn, vn)):
            ref[...] = val

    row = ((tr, cols), lambda s: (s, 0))
    stack = ((parts.shape[0], tr, cols), lambda s: (0, s, 0))
    shapes = [jax.ShapeDtypeStruct((rows, cols), F32)] * 4
    return [(own, *row), (parts, *stack), (w, *row), (m, *row), (v, *row)], shapes, [row] * 4, fn


def _sum_adamw(name, own, parts, w, m, v):
    rows, cols = w.shape
    n_parts = parts.shape[0]
    tr = rows
    min_rows = 8 if parts.dtype == F32 else 16
    while tr * cols * 4 > 2 * 1024 * 1024 and tr % (2 * min_rows) == 0:
        tr //= 2

    def body(*refs):
        if own is None:
            p_ref, w_ref, m_ref, v_ref, g_ref, d_ref, mo_ref, vo_ref = refs
            g = p_ref[0].astype(F32)
            rest = range(1, n_parts)
        else:
            o_ref, p_ref, w_ref, m_ref, v_ref, g_ref, d_ref, mo_ref, vo_ref = refs
            g = o_ref[...]
            rest = range(n_parts)
        for s in rest:
            g = g + p_ref[s].astype(F32)
        delta, mn, vn = _adamw_values(w_ref[...], g, m_ref[...], v_ref[...])
        g_ref[...] = g
        d_ref[...] = delta
        mo_ref[...] = mn
        vo_ref[...] = vn

    spec = _bs((tr, cols), lambda i: (i, 0))
    lead = [] if own is None else [own]
    return pl.pallas_call(
        body, name=name, grid=(rows // tr,),
        in_specs=[spec] * len(lead) + [_bs((n_parts, tr, cols), lambda i: (0, i, 0)), spec, spec, spec],
        out_specs=[spec] * 4, out_shape=[jax.ShapeDtypeStruct((rows, cols), F32)] * 4,
        compiler_params=_params(1),
    )(*lead, parts, w, m, v)


def _rows128(a):
    return a.reshape(-1, 128)


def kernel(x, ln_mix_g, ln_mix_b, w_in, w_pool, pool_scale, conv_w, conv_b, w_rg_a, b_rg_a, w_rg_i, b_rg_i, rg_lambda, w_out, ln_ffn_g, ln_ffn_b, w_mlp_in, w_mlp_out, loss_target, m_ln_mix_g, m_ln_mix_b, m_w_in, m_w_pool, m_pool_scale, m_conv_w, m_conv_b, m_w_rg_a, m_b_rg_a, m_w_rg_i, m_b_rg_i, m_rg_lambda, m_w_out, m_ln_ffn_g, m_ln_ffn_b, m_w_mlp_in, m_w_mlp_out, v_ln_mix_g, v_ln_mix_b, v_w_in, v_w_pool, v_pool_scale, v_conv_w, v_conv_b, v_w_rg_a, v_b_rg_a, v_w_rg_i, v_b_rg_i, v_rg_lambda, v_w_out, v_ln_ffn_g, v_ln_ffn_b, v_w_mlp_in, v_w_mlp_out):
    seq, d_model = x.shape[1], x.shape[2]
    dh = d_model // 2
    lh = dh // N_HEADS
    pg = dh // len(POOL_WINDOWS)
    d_ff = w_mlp_in.shape[2] * N_DEV
    assert lh == 128 and conv_w.shape[3] == lh and w_pool.shape[2] * N_DEV == pg

    xs = x[0]
    tgt = loss_target[0]

    def small_pack(cw, ba, bi, lam):
        return jnp.concatenate([cw.reshape(4, lh), ba.reshape(2, lh), bi.reshape(2, lh), lam.reshape(2, lh),
                                jnp.zeros((SMALL_ROWS - 10, lh), F32)], axis=0)

    pack_mine = small_pack(conv_w, b_rg_a, b_rg_i, rg_lambda)
    pack_bits = lax.bitcast_convert_type(pack_mine, BF16).reshape(1, SMALL_ROWS, 2 * lh)
    win_gather = _SplitGather("gather_w_in", [(w_in[0], 1), (w_pool[0], 1), (pack_bits, 0)], BF16, after=pack_mine)
    wout_gather = _SplitGather("gather_w_out", [(w_out[0], 0)], BF16, after=win_gather.token)
    w1_gather = _SplitGather("gather_w_mlp_in", [(w_mlp_in[0], 1)], BF16, after=wout_gather.token)
    w2_gather = _SplitGather("gather_w_mlp_out", [(w_mlp_out[0], 0)], BF16, after=w1_gather.token)
    xb = _to_bf16("x_bf16", x[0], w2_gather.token)
    win_full, wpool_full, pack_bits_full = win_gather.wait(after=win_gather.relay(after=xb))
    pack_full = lax.bitcast_convert_type(pack_bits_full.reshape(N_DEV, SMALL_ROWS, lh, 2), F32)
    wcat = jnp.concatenate([w_rg_a[0, 0], w_rg_i[0, 0], w_rg_a[0, 1], w_rg_i[0, 1]], axis=-1).astype(BF16)
    vec = lambda i, j, k: (0, 0)
    row_full = lambda i, j, k: (i, 0)

    def after(token):
        return (token, _sp((8, 128), vec))

    def sds(shape, dtype):
        return jax.ShapeDtypeStruct(shape, dtype)

    def plain_epi(acc, i, ex, out):
        out[0][...] = acc

    def bf16_epi(acc, i, ex, out):
        out[0][...] = acc.astype(BF16)

    t = _tiles(seq, d_model, d_ff)

    (p3,) = _matmul(
        "proj", xb, win_full, _sp((t.rows, d_model), lambda i, j, k: (i, 0)), _sp((d_model, dh), lambda i, j, k: (0, j)),
        grid=(seq // t.rows, 3, 1),
        out_shape=[sds((3, seq, dh), F32)], out_specs=[_sp((None, t.rows, dh), lambda i, j, k: (j, i, 0))],
        epilogue=plain_epi)

    d_pool, y_half = _pool_fwd(p3, wpool_full, pool_scale, seq, d_model)
    y, h0p, h1p = _lru_fwd(p3, y_half, pack_full, conv_b, wcat, wout_gather.relay(after=y_half), seq, d_model)
    (wout_full,) = wout_gather.wait(after=y)
    relay_token = w1_gather.relay(after=wout_full)

    mix_rows = 2 * t.ln_rows

    def mix_epi(acc, i, ex, out):
        x_ref, g_ref, b_ref = ex[:3]
        for part in range(2):
            rows = pl.ds(part * t.ln_rows, t.ln_rows)
            z = ALPHA * x_ref[rows, :] + acc[part * t.ln_rows:(part + 1) * t.ln_rows, :]
            x1, _, _ = _ln_fwd(z, g_ref[...], b_ref[...])
            out[0][rows, :] = z
            out[1][rows, :] = x1
            out[2][rows, :] = x1.astype(BF16)

    z1, x1, x1b = _matmul(
        "mix_out", y, wout_full, _sp((mix_rows, d_model), row_full), _sp((d_model, d_model), vec, single=True),
        grid=(seq // mix_rows, 1, 1),
        extras=[(xs, _sp((mix_rows, d_model), row_full)), (ln_mix_g, _sp((1, d_model), vec)),
                (ln_mix_b, _sp((1, d_model), vec)), after(relay_token)],
        out_shape=[sds((seq, d_model), F32), sds((seq, d_model), F32), sds((seq, d_model), BF16)],
        out_specs=[_sp((mix_rows, d_model), row_full)] * 3, epilogue=mix_epi)
    (w1_full,) = w1_gather.wait(after=x1b)

    def mlp_in_epi(acc, i, ex, out, cols):
        h = jnp.maximum(acc, 0.0)
        out[0][:, cols] = (h * h).astype(BF16)
        out[1][:, cols] = (2.0 * h).astype(BF16)

    hmid, dact = _matmul(
        "mlp_in", x1b, w1_full, _sp((t.rows, d_model), lambda i, j, k: (i, 0)),
        _sp((d_model, t.ff_cols), lambda i, j, k: (0, j)),
        grid=(seq // t.rows, d_ff // t.ff_cols, 1), j_outer=True,
        out_shape=[sds((seq, d_ff), BF16)] * 2, out_specs=[_sp((t.rows, t.ff_cols), lambda i, j, k: (i, j))] * 2,
        epilogue=mlp_in_epi, n_split=t.ff_split)
    (w2_full,) = w2_gather.wait(after=w2_gather.relay(after=hmid))

    (ffn,) = _matmul(
        "mlp_out", hmid, w2_full, _sp((t.rows, t.ff_k), lambda i, j, k: (i, k)),
        _sp((t.ff_k, d_model), lambda i, j, k: (k, 0)),
        grid=(seq // t.rows, 1, d_ff // t.ff_k),
        out_shape=[sds((seq, d_model), F32)], out_specs=[_sp((t.rows, d_model), row_full)])
    dz2, dz2b, g_ffn_g, g_ffn_b, loss_part = _ln_loss_bwd(ffn, x1, tgt, ln_ffn_g, ln_ffn_b, t.ln_rows)

    (g_w2,) = _matmul(
        "grad_w_mlp_out", hmid, dz2b, _sp((seq, t.grad_rows), lambda i, j, k: (0, i)),
        _sp((seq, d_model), vec, single=True),
        grid=(d_ff // t.grad_rows, 1, 1), ta=True,
        out_shape=[sds((d_ff, d_model), BF16)], out_specs=[_sp((t.grad_rows, d_model), row_full)],
        epilogue=bf16_epi)
    scatter_w2 = _SplitReduceScatter("scatter_w_mlp_out", [g_w2.reshape(N_DEV, d_ff // N_DEV, d_model)])

    def dpre_epi(acc, i, ex, out, cols):
        out[0][:, cols] = (acc * ex[0][:, cols].astype(F32)).astype(BF16)

    (dpre,) = _matmul(
        "mlp_dpre", dz2b, w2_full, _sp((t.rows, d_model), lambda i, j, k: (i, 0)),
        _sp((t.ff_cols, d_model), lambda i, j, k: (j, 0)),
        grid=(seq // t.rows, d_ff // t.ff_cols, 1), j_outer=True, tb=True,
        extras=[(dact, _sp((t.rows, t.ff_cols), lambda i, j, k: (i, j))), after(scatter_w2.token)],
        out_shape=[sds((seq, d_ff), BF16)], out_specs=[_sp((t.rows, t.ff_cols), lambda i, j, k: (i, j))],
        epilogue=dpre_epi, n_split=t.ff_split)
    token_w2 = scatter_w2.combine_and_send(after=dpre)

    (dx1_mlp,) = _matmul(
        "mlp_dx", dpre, w1_full, _sp((t.rows, t.ff_k), lambda i, j, k: (i, k)),
        _sp((d_model, t.ff_k), lambda i, j, k: (0, k)),
        grid=(seq // t.rows, 1, d_ff // t.ff_k), tb=True, extras=[after(token_w2)],
        out_shape=[sds((seq, d_model), F32)], out_specs=[_sp((t.rows, d_model), row_full)])
    def block_epi(acc, i, ex, out):
        out[0][0] = acc.astype(BF16)

    fs = d_ff // N_DEV
    g_w1, dz1, dz1b, g_mix_g, g_mix_b = _matmul(
        "grad_w_mlp_in", x1b, dpre, _sp((seq, t.grad_rows), lambda i, j, k: (0, i)),
        _sp((seq, fs), lambda i, j, k: (0, j)),
        grid=(d_model // t.grad_rows, N_DEV, 1), j_outer=True, ta=True,
        out_shape=[sds((N_DEV, d_model, fs), BF16)],
        out_specs=[_sp((1, t.grad_rows, fs), lambda i, j, k: (j, i, 0))], epilogue=block_epi,
        side=_ln_bwd_side(dx1_mlp, dz2, z1, ln_mix_g, ln_mix_b, d_model // t.grad_rows * N_DEV))

    (dy,) = _matmul(
        "mix_dy", dz1b, wout_full, _sp((t.rows, d_model), lambda i, j, k: (i, 0)),
        _sp((dh, d_model), lambda i, j, k: (j, 0)),
        grid=(seq // t.rows, 2, 1), j_outer=True, tb=True,
        out_shape=[sds((seq, d_model), F32)], out_specs=[_sp((t.rows, dh), lambda i, j, k: (i, j))],
        epilogue=plain_epi)
    (g_wout,) = _matmul(
        "grad_w_out", y, dz1b, _sp((seq, t.grad_rows), lambda i, j, k: (0, i)), _sp((seq, d_model), vec, single=True),
        grid=(d_model // t.grad_rows, 1, 1), ta=True,
        out_shape=[sds((d_model, d_model), BF16)], out_specs=[_sp((t.grad_rows, d_model), row_full)],
        epilogue=bf16_epi)
    scatter_w1 = _SplitReduceScatter("scatter_w_mlp_in", [g_w1, g_wout.reshape(N_DEV, d_model // N_DEV, d_model)])

    dproj_pool, g_wpool, g_pscale = _pool_bwd(d_pool, dy, wpool_full, pool_scale, scatter_w1.token, seq, d_model)
    token_w1 = scatter_w1.combine_and_send(after=dproj_pool)
    dproj, g_pack, g_convb, g_wcat = _lru_bwd(p3, dy, h0p, h1p, dproj_pool, pack_full, conv_b, wcat,
                                              token_w1, seq, d_model)
    g_wa = jnp.stack([g_wcat[:, :, 0:lh], g_wcat[:, :, 2 * lh:3 * lh]])
    g_wi = jnp.stack([g_wcat[:, :, lh:2 * lh], g_wcat[:, :, 3 * lh:4 * lh]])

    rep_parts = [_rows128(g_wa), _rows128(g_wi), _rows128(g_mix_g), _rows128(g_mix_b), _rows128(g_ffn_g),
                 _rows128(g_ffn_b), _rows128(g_pscale), _rows128(g_convb)]
    rep_rows = [p.shape[0] for p in rep_parts]
    n_rep = sum(rep_rows)
    small = jnp.concatenate(rep_parts + [_rows128(g_pack), loss_part], axis=0)
    small_gather = _SplitGather("gather_small_grads", [(small[None], 0)], F32, after=small)

    ws = 3 * dh // N_DEV

    def pair_epi(acc, i, ex, out):
        out[0][0] = acc[:, :ws].astype(BF16)
        out[0][1] = acc[:, ws:].astype(BF16)

    def adam_big(name, own_landed, w, m, v):
        own, landed = own_landed
        shp = w.shape
        two = lambda a: a.reshape(-1, shp[-1])
        res = _sum_adamw(name, own, landed, two(w), two(m), two(v))
        return [r.reshape(shp) for r in res]

    (r_w2,) = scatter_w2.wait(after=small_gather.token)
    n_steps = d_model // t.grad_rows * (N_DEV // 2)
    g_win, *o_w2 = _matmul(
        "grad_w_in", xb, dproj, _sp((seq, t.grad_rows), lambda i, j, k: (0, i)),
        _sp((seq, 2 * ws), lambda i, j, k: (0, j)),
        grid=(d_model // t.grad_rows, N_DEV // 2, 1), ta=True,
        out_shape=[sds((N_DEV, d_model, ws), BF16)],
        out_specs=[_sp((2, t.grad_rows, ws), lambda i, j, k: (j, i, 0))], epilogue=pair_epi,
        side=_adamw_side(r_w2[0], r_w2[1], w_mlp_out[0], m_w_mlp_out[0], v_w_mlp_out[0], n_steps))
    o_w2 = [r.reshape(w_mlp_out.shape) for r in o_w2]
    scatter_mix = _SplitReduceScatter(
        "scatter_mixer", [g_win, g_wpool.reshape(N_DEV, pg // N_DEV * len(POOL_WINDOWS), pg)])

    r_w1, r_wout = scatter_w1.wait(after=scatter_mix.token)
    o_w1 = adam_big("adam_w_mlp_in", r_w1, w_mlp_in, m_w_mlp_in, v_w_mlp_in)
    token_mix = scatter_mix.combine_and_send(after=o_w1[0])

    def dx_epi(acc, i, ex, out):
        out[0][...] = ALPHA * ex[0][...] + acc

    dx_rows = t.ln_rows * 2
    dx, *o_wout = _matmul(
        "grad_x", dproj, win_full, _sp((dx_rows, 3 * dh), lambda i, j, k: (i, 0)),
        _sp((d_model, 3 * dh), vec, single=True),
        grid=(seq // dx_rows, 1, 1), tb=True,
        extras=[(dz1, _sp((dx_rows, d_model), row_full)), after(token_mix)],
        out_shape=[sds((seq, d_model), F32)], out_specs=[_sp((dx_rows, d_model), row_full)],
        epilogue=dx_epi,
        side=_adamw_side(r_wout[0], r_wout[1], w_out[0], m_w_out[0], v_w_out[0], seq // dx_rows))
    o_wout = [r.reshape(w_out.shape) for r in o_wout]
    r_win, r_wpool = scatter_mix.wait(after=dx)
    o_win = adam_big("adam_w_in", r_win, w_in, m_w_in, v_w_in)
    o_wpool = adam_big("adam_w_pool", r_wpool, w_pool, m_w_pool, v_w_pool)

    small_gather.relay(after=o_win[0])
    (small_all,) = small_gather.wait(after=o_wpool[0])

    rep_w = [w_rg_a, w_rg_i, ln_mix_g, ln_mix_b, ln_ffn_g, ln_ffn_b, pool_scale, conv_b]
    rep_m = [m_w_rg_a, m_w_rg_i, m_ln_mix_g, m_ln_mix_b, m_ln_ffn_g, m_ln_ffn_b, m_pool_scale, m_conv_b]
    rep_v = [v_w_rg_a, v_w_rg_i, v_ln_mix_g, v_ln_mix_b, v_ln_ffn_g, v_ln_ffn_b, v_pool_scale, v_conv_b]
    cat = lambda arrs: jnp.concatenate([_rows128(a) for a in arrs], axis=0)
    o_rep = _sum_adamw("adam_replicated", None, small_all, cat(rep_w), cat(rep_m), cat(rep_v))

    my_idx = _dev_index(_where_am_i())
    head_parts = lax.dynamic_slice_in_dim(small_all, n_rep + my_idx * SMALL_ROWS, SMALL_ROWS, axis=1)
    o_head = _sum_adamw("adam_head", None, head_parts, pack_mine,
                        small_pack(m_conv_w, m_b_rg_a, m_b_rg_i, m_rg_lambda),
                        small_pack(v_conv_w, v_b_rg_a, v_b_rg_i, v_rg_lambda))

    def unpack_rep(packed):
        out, r = [], 0
        for wgt, rows in zip(rep_w, rep_rows):
            out.append(packed[r:r + rows].reshape(wgt.shape))
            r += rows
        return out

    def unpack_head(packed):
        return [packed[0:4].reshape(conv_w.shape), packed[4:6].reshape(b_rg_a.shape),
                packed[6:8].reshape(b_rg_i.shape), packed[8:10].reshape(rg_lambda.shape)]

    loss = _sum_blocks("loss_sum", small_all[:, n_rep + N_HEADS * SMALL_ROWS:, :])[0, 0]

    outs = [loss, dx[None]]
    for kind in range(4):
        ra, ri, mg, mb, fg, fb, ps, cb = unpack_rep(o_rep[kind])
        cw, ba, bi, lam = unpack_head(o_head[kind])
        outs += [mg, mb, o_win[kind], o_wpool[kind], ps, cw, cb, ra, ba, ri, bi, lam, o_wout[kind], fg, fb,
                 o_w1[kind], o_w2[kind]]
    return tuple(outs)
```

```python
import functools

import jax
import jax.numpy as jnp
from jax import lax
from jax.experimental import pallas as pl
from jax.experimental.pallas import tpu as pltpu

F32 = jnp.float32
BF16 = jnp.bfloat16
MESH = pl.DeviceIdType.MESH
ANY = pl.BlockSpec(memory_space=pl.ANY)

N_DEV = 8
POOL_WINDOWS = (2, 4, 8, 16)
N_HEADS = 8
RG_C = 8.0
LN_EPS = 1e-5
ALPHA = 2.0 ** 0.25
ADAM_LR = 0.001
ADAM_B1 = 0.9
ADAM_B2 = 0.999
ADAM_EPS = 1e-08
ADAM_WD = 0.01
ADAM_STEP = 10

VMEM_LIMIT = 56 * 1024 * 1024
SEQ_CHUNK = 256
POOL_CHUNK = 512
WIN_HALO = 16
CONV_HALO = 8
SMALL_ROWS = 16


def _params(n_grid):
    return pltpu.CompilerParams(dimension_semantics=("arbitrary",) * n_grid, vmem_limit_bytes=VMEM_LIMIT)


def _shift(v, j):
    n = v.shape[0]
    s = (-j) % n
    return v if s == 0 else pltpu.roll(v, s, 0)


def _sigmoid(x):
    return 0.5 * jnp.tanh(0.5 * x) + 0.5


def _softplus(z):
    e = jnp.exp(-jnp.abs(z))
    u = 1.0 + e
    log1p = jnp.where(u == 1.0, e, jnp.log(u) * (e / jnp.where(u == 1.0, 1.0, u - 1.0)))
    return jnp.maximum(z, 0.0) + log1p


_GELU_C = 0.7978845608028654
_GELU_K = 0.044715


def _gelu_and_grad(x):
    x2 = x * x
    t = jnp.tanh(_GELU_C * (x + _GELU_K * x * x2))
    g = 0.5 * x * (1.0 + t)
    dg = 0.5 * (1.0 + t) + 0.5 * x * (1.0 - t * t) * (_GELU_C * (1.0 + 3.0 * _GELU_K * x2))
    return g, dg


def _ln_fwd(z, g, b):
    mu = jnp.mean(z, axis=-1, keepdims=True)
    zc = z - mu
    var = jnp.mean(zc * zc, axis=-1, keepdims=True)
    rstd = lax.rsqrt(var + LN_EPS)
    xhat = zc * rstd
    return xhat * g + b, xhat, rstd


def _ln_bwd(dy, xhat, rstd, g):
    dxhat = dy * g
    m1 = jnp.mean(dxhat, axis=-1, keepdims=True)
    m2 = jnp.mean(dxhat * xhat, axis=-1, keepdims=True)
    dz = rstd * (dxhat - m1 - xhat * m2)
    dg = jnp.sum(dy * xhat, axis=0, keepdims=True)
    db = jnp.sum(dy, axis=0, keepdims=True)
    return dz, dg, db


def _acc_rows(ref, first, val):
    @pl.when(first)
    def _():
        ref[...] = val

    @pl.when(jnp.logical_not(first))
    def _():
        ref[...] += val


def _sp(shape, fn, single=False):
    return shape, fn, single


def _matmul(name, a, b, a_spec, b_spec, *, grid, j_outer=False, ta=False, tb=False, extras=(), out_shape, out_specs,
            epilogue=None, n_split=1, side=None):
    ni, nj, nk = grid
    n_ex = len(extras)
    dims = (((0 if ta else 1,), (1 if tb else 0,)), ((), ()))
    side_in, side_shape, side_out, side_fn = side if side is not None else ((), (), (), None)
    n_main_out = len(out_shape)
    inner = ni if j_outer else nj

    def mk(spec):
        shape, fn, single = spec
        index = (lambda g0, g1, g2: fn(g1, g0, g2)) if j_outer else fn
        return pl.BlockSpec(shape, index, pipeline_mode=pl.Buffered(1)) if single else pl.BlockSpec(shape, index)

    def mk_side(block, fn):
        return pl.BlockSpec(block, lambda g0, g1, g2: fn(g0 * inner + g1))

    def body(a_ref, b_ref, *rest):
        ex_refs = rest[:n_ex]
        out_refs = rest[n_ex + len(side_in):n_ex + len(side_in) + n_main_out]
        if side_fn is not None:
            side_fn(pl.program_id(0) * inner + pl.program_id(1), rest[n_ex:n_ex + len(side_in)],
                    rest[n_ex + len(side_in) + n_main_out:])
        i = pl.program_id(1 if j_outer else 0)
        if n_split > 1:
            av = a_ref[...].astype(BF16)
            width = b_ref.shape[0 if tb else 1] // n_split
            for c in range(n_split):
                cols = pl.ds(c * width, width)
                bv = (b_ref[cols, :] if tb else b_ref[:, cols]).astype(BF16)
                epilogue(lax.dot_general(av, bv, dims, preferred_element_type=F32), i, ex_refs, out_refs, cols)
            return
        part = lax.dot_general(a_ref[...].astype(BF16), b_ref[...].astype(BF16), dims, preferred_element_type=F32)
        if nk == 1:
            epilogue(part, i, ex_refs, out_refs)
        else:
            @pl.when(pl.program_id(2) == 0)
            def _():
                out_refs[0][...] = part

            @pl.when(pl.program_id(2) > 0)
            def _():
                out_refs[0][...] += part

    return pl.pallas_call(
        body, name=name, grid=(nj, ni, nk) if j_outer else (ni, nj, nk),
        in_specs=[mk(a_spec), mk(b_spec)] + [mk(s) for _, s in extras] + [mk_side(blk, fn) for _, blk, fn in side_in],
        out_specs=[mk(s) for s in out_specs] + [mk_side(blk, fn) for blk, fn in side_out],
        out_shape=list(out_shape) + list(side_shape),
        compiler_params=_params(3),
    )(a, b, *[x for x, _ in extras], *[x for x, _, _ in side_in])


def _bs(shape, fn):
    return pl.BlockSpec(shape, fn)


def _where_am_i():
    x, y, c = lax.axis_index("x"), lax.axis_index("y"), lax.axis_index("c")
    return x, y, c


def _dev_index(p):
    return 4 * p[0] + 2 * p[1] + p[2]


def _slab(ref, axis, idx, size):
    sl = [slice(None)] * len(ref.shape)
    sl[axis] = pl.ds(idx * size, size)
    return ref.at[tuple(sl)]


HBM = pl.BlockSpec(memory_space=pltpu.HBM)
SEM = pl.BlockSpec(memory_space=pltpu.SEMAPHORE)
DATAFLOW = pltpu.SideEffectType.DATAFLOW_SIDE_EFFECTING


def _in_hbm(a):
    return pltpu.with_memory_space_constraint(a, pltpu.HBM)


def _token_shape():
    return jax.ShapeDtypeStruct((8, 128), F32)


def _split_start(name, n_sems, bufs, issue):
    nb = len(bufs)

    def body(*refs):
        issue(refs[:nb], refs[nb], refs[nb + 1])
        refs[-1][...] = jnp.zeros((8, 128), F32)

    outs = pl.pallas_call(
        body, name=name,
        out_shape=(pltpu.SemaphoreType.DMA((n_sems,)), pltpu.SemaphoreType.DMA((n_sems,)),
                   *[pltpu.HBM(b.shape, b.dtype) for b in bufs], _token_shape()),
        in_specs=[HBM] * nb, out_specs=(SEM, SEM, *[HBM] * nb, pl.BlockSpec(memory_space=pltpu.VMEM)),
        input_output_aliases={i: 2 + i for i in range(nb)},
        compiler_params=pltpu.CompilerParams(has_side_effects=DATAFLOW),
    )(*[_in_hbm(b) for b in bufs])
    return outs[0], outs[1], list(outs[2:2 + nb]), outs[-1]


def _split_relay(name, n_sems, sems, bufs, after, relay):
    nb = len(bufs)

    def body(*refs):
        relay(refs[:nb], refs[nb], refs[nb + 1], refs[nb + 3], refs[nb + 4])
        refs[-1][...] = jnp.zeros((8, 128), F32)

    outs = pl.pallas_call(
        body, name=name,
        out_shape=(pltpu.SemaphoreType.DMA((n_sems,)), pltpu.SemaphoreType.DMA((n_sems,)),
                   *[pltpu.HBM(b.shape, b.dtype) for b in bufs], _token_shape()),
        in_specs=[HBM] * nb + [SEM, SEM, ANY],
        out_specs=(SEM, SEM, *[HBM] * nb, pl.BlockSpec(memory_space=pltpu.VMEM)),
        input_output_aliases={i: 2 + i for i in range(nb)},
        compiler_params=pltpu.CompilerParams(has_side_effects=DATAFLOW),
    )(*bufs, sems[0], sems[1], after)
    return outs[0], outs[1], list(outs[2:2 + nb]), outs[-1]


def _split_wait(name, sems, bufs, after, finish):
    nb = len(bufs)

    def body(*refs):
        finish(refs[:nb], refs[nb], refs[nb + 1])

    outs = pl.pallas_call(
        body, name=name, out_shape=[pltpu.HBM(b.shape, b.dtype) for b in bufs],
        in_specs=[HBM] * nb + [SEM, SEM, ANY], out_specs=[HBM] * nb,
        input_output_aliases={i: i for i in range(nb)},
        compiler_params=pltpu.CompilerParams(has_side_effects=DATAFLOW),
    )(*bufs, sems[0], sems[1], after)
    return list(outs)


def _place(name, items, dtype, after):
    ids = jnp.reshape(_dev_index(_where_am_i()), (1,)).astype(jnp.int32)
    outs = []
    for a, (shard, axis) in enumerate(items):
        rows, cols = shard.shape[-2], shard.shape[-1]
        tr = rows
        while tr * cols * shard.dtype.itemsize > 4 * 1024 * 1024 and tr % 32 == 0:
            tr //= 2
        nt = rows // tr
        full = list(shard.shape)
        full[axis] *= N_DEV
        if shard.ndim == 2 and axis == 0:
            in_spec = _bs((tr, cols), lambda i, ids: (i, 0))
            out_spec = _bs((tr, cols), lambda i, ids, nt=nt: (ids[0] * nt + i, 0))
        elif shard.ndim == 2 and axis == 1:
            in_spec = _bs((tr, cols), lambda i, ids: (i, 0))
            out_spec = _bs((tr, cols), lambda i, ids: (i, ids[0]))
        elif shard.ndim == 3 and axis == 1:
            tr, nt = rows, shard.shape[0]
            in_spec = _bs((None, rows, cols), lambda i, ids: (i, 0, 0))
            out_spec = _bs((None, rows, cols), lambda i, ids: (i, ids[0], 0))
        else:
            assert shard.ndim == 3 and axis == 0 and shard.shape[0] == 1
            in_spec = _bs((None, tr, cols), lambda i, ids: (0, i, 0))
            out_spec = _bs((None, tr, cols), lambda i, ids: (ids[0], i, 0))

        def body(ids_ref, in_ref, after_ref, out_ref):
            del ids_ref, after_ref
            out_ref[...] = in_ref[...].astype(out_ref.dtype)

        outs.append(pl.pallas_call(
            body, name=f"{name}{a}",
            grid_spec=pltpu.PrefetchScalarGridSpec(
                num_scalar_prefetch=1, grid=(nt,), in_specs=[in_spec, ANY], out_specs=out_spec),
            out_shape=jax.ShapeDtypeStruct(tuple(full), dtype), compiler_params=_params(1),
        )(ids, shard, after))
    return outs


class _SplitGather:
    def __init__(self, name, items, dtype, after):
        self.name, self.items, self.n = name, items, len(items)
        fulls = _place(name + "_place", items, dtype, after)
        n = self.n

        def issue(refs, send, recv):
            me, sibling, chips, c = self._geometry()
            for a in range(n):
                self._copy1(refs, send, recv, a, 0, me, sibling).start()
                for j, chip in enumerate(chips):
                    self._copy1(refs, send, recv, a, 1 + j, me, (*chip, c)).start()

        self.send, self.recv, self.bufs, self.token = _split_start(name + "_start", 4 * n, fulls, issue)

    @staticmethod
    def _geometry():
        x, y, c = _where_am_i()
        return (x, y, c), (x, y, 1 - c), [(1 - x, y), (x, 1 - y), (1 - x, 1 - y)], c

    def _blk(self, refs, a, p):
        shard, axis = self.items[a]
        return _slab(refs[a], axis, _dev_index(p), shard.shape[axis])

    def _copy1(self, refs, send, recv, a, k, owner, to):
        return pltpu.make_async_remote_copy(
            src_ref=self._blk(refs, a, owner), dst_ref=self._blk(refs, a, owner), send_sem=send.at[4 * a + k],
            recv_sem=recv.at[4 * a + k], device_id=to, device_id_type=MESH)

    def _copy2(self, refs, send, recv, a, j, owner, to):
        return pltpu.make_async_remote_copy(
            src_ref=self._blk(refs, a, owner), dst_ref=self._blk(refs, a, owner), send_sem=send.at[3 * a + j],
            recv_sem=recv.at[3 * a + j], device_id=to, device_id_type=MESH)

    def relay(self, after):
        n = self.n

        def relay(refs, send_in, recv_in, send_out, recv_out):
            me, sibling, chips, c = self._geometry()
            for a in range(n):
                for j, chip in enumerate(chips):
                    self._copy1(refs, send_in, recv_in, a, 1 + j, (*chip, c), me).wait_recv()
                    self._copy2(refs, send_out, recv_out, a, j, (*chip, c), sibling).start()
            for a in range(n):
                self._copy1(refs, send_in, recv_in, a, 0, sibling, me).wait_recv()
                for k in range(4):
                    self._copy1(refs, send_in, recv_in, a, k, me, sibling).wait_send()

        self.send, self.recv, self.bufs, self.token = _split_relay(
            self.name + "_relay", 3 * n, (self.send, self.recv), self.bufs, after, relay)
        return self.token

    def wait(self, after):
        n = self.n

        def finish(refs, send, recv):
            me, sibling, chips, c = self._geometry()
            for a in range(n):
                for j, chip in enumerate(chips):
                    self._copy2(refs, send, recv, a, j, (*chip, 1 - c), me).wait_recv()
                    self._copy2(refs, send, recv, a, j, (*chip, c), sibling).wait_send()

        return _split_wait(self.name + "_wait", (self.send, self.recv), self.bufs, after, finish)


class _SplitReduceScatter:
    def __init__(self, name, grads):
        self.name, self.n = name, len(grads)
        n = self.n
        g4 = [g.reshape(4, 2, *g.shape[1:]) for g in grads]
        land = [lax.empty((4, 1, *g.shape[1:]), g.dtype) for g in grads]

        def issue(refs, send, recv):
            for a in range(n):
                self._swap(refs, send, recv, a).start()

        self.send, self.recv, self.bufs, self.token = _split_start(name + "_d2d_start", n, g4 + land, issue)

    def _swap(self, refs, send, recv, a):
        x, y, c = _where_am_i()
        return pltpu.make_async_remote_copy(
            src_ref=refs[a].at[:, pl.ds(1 - c, 1)], dst_ref=refs[self.n + a], send_sem=send.at[a], recv_sem=recv.at[a],
            device_id=(x, y, 1 - c), device_id_type=MESH)

    def _hop(self, refs, send, recv, a, m):
        x, y, c = _where_am_i()
        px = (1 - x) if m & 2 else x
        py = (1 - y) if m & 1 else y
        return pltpu.make_async_remote_copy(
            src_ref=refs[a].at[2 * px + py], dst_ref=refs[self.n + a].at[m - 1], send_sem=send.at[3 * a + m - 1],
            recv_sem=recv.at[3 * a + m - 1], device_id=(px, py, c), device_id_type=MESH)

    def combine_and_send(self, after):
        n = self.n

        def finish(refs, send, recv):
            for a in range(n):
                self._swap(refs, send, recv, a).wait()

        bufs = _split_wait(self.name + "_d2d_wait", (self.send, self.recv), self.bufs, after, finish)
        x, y, c = _where_am_i()
        ids = jnp.stack([c, 2 * x + y]).astype(jnp.int32)
        self.own, sums = [], []
        for a in range(n):
            own, hb = _pair_sum(f"{self.name}_sum{a}", bufs[a], bufs[n + a], ids)
            self.own.append(own)
            sums.append(hb)
        land = [lax.empty((3, *h.shape[1:]), h.dtype) for h in sums]

        def issue(refs, send, recv):
            for a in range(n):
                for m in (1, 2, 3):
                    self._hop(refs, send, recv, a, m).start()

        self.send, self.recv, self.bufs, self.token = _split_start(self.name + "_ici_start", 3 * n, sums + land, issue)
        return self.token

    def wait(self, after):
        n = self.n

        def finish(refs, send, recv):
            for a in range(n):
                for m in (1, 2, 3):
                    self._hop(refs, send, recv, a, m).wait()

        bufs = _split_wait(self.name + "_ici_wait", (self.send, self.recv), self.bufs, after, finish)
        return list(zip(self.own, bufs[n:]))


def _pair_sum(name, g4, land, ids):
    rows, cols = g4.shape[2], g4.shape[3]
    tr = rows
    while tr * cols * 2 > 2 * 1024 * 1024 and tr % 32 == 0:
        tr //= 2

    def body(ids_ref, g_ref, l_ref, own_ref, sum_ref):
        h = g_ref[...].astype(F32) + l_ref[...].astype(F32)
        sum_ref[...] = h.astype(sum_ref.dtype)

        @pl.when(pl.program_id(1) == ids_ref[1])
        def _():
            own_ref[...] = h

    return pl.pallas_call(
        body, name=name,
        grid_spec=pltpu.PrefetchScalarGridSpec(
            num_scalar_prefetch=1, grid=(rows // tr, 4),
            in_specs=[_bs((None, None, tr, cols), lambda i, q, ids: (q, ids[0], i, 0)),
                      _bs((None, None, tr, cols), lambda i, q, ids: (q, 0, i, 0))],
            out_specs=[_bs((tr, cols), lambda i, q, ids: (i, 0)), _bs((None, tr, cols), lambda i, q, ids: (q, i, 0))]),
        out_shape=[jax.ShapeDtypeStruct((rows, cols), F32), jax.ShapeDtypeStruct((4, rows, cols), g4.dtype)],
        compiler_params=_params(2),
    )(ids, g4, land)


def _win_sum(ext, w, off):
    s = ext + _shift(ext, -1)
    if w >= 4:
        s = _shift(s, -1) + _shift(s, 1)
    if w >= 8:
        s = _shift(s, -2) + _shift(s, 2)
    if w >= 16:
        s = _shift(s, -4) + _shift(s, 4)
    return _shift(s, off) if off else s


def _inv_count(r0, t, w, seq):
    pos = r0 + lax.broadcasted_iota(jnp.int32, (t, 1), 0)
    cnt = jnp.minimum(pos + w // 2, seq) - jnp.maximum(pos - w // 2, 0)
    return 1.0 / cnt.astype(F32)


def _pool_fwd(p3, w_pool, pool_scale, seq, d_model):
    dp = d_model // 2
    pg = dp // len(POOL_WINDOWS)
    t = min(POOL_CHUNK, seq)
    n_chunks = seq // t
    h = WIN_HALO

    def body(u_ref, w_ref, sc_ref, d_ref, y_ref, pad_ref):
        g = pl.program_id(0)
        zeros = jnp.zeros((h, pg), F32)
        pad_ref[0:h, :] = zeros
        pad_ref[h + seq:h + seq + h, :] = zeros

        def fill(ci, _):
            r0 = pl.multiple_of(ci * t, t)
            pad_ref[pl.ds(h + r0, t), :] = u_ref[pl.ds(r0, t), :]
            return 0

        lax.fori_loop(0, n_chunks, fill, 0)
        wmat = w_ref[...]
        scale = sc_ref[...]
        for gi, w in enumerate(POOL_WINDOWS):
            @pl.when(g == gi)
            def _(w=w):
                def chunk(ci, _):
                    r0 = pl.multiple_of(ci * t, t)
                    ext = pad_ref[pl.ds(r0, t + 2 * h), :]
                    mean = _win_sum(ext, w, 0)[h:h + t, :] * _inv_count(r0, t, w, seq)
                    d = (mean - ext[h:h + t, :]).astype(BF16)
                    d_ref[pl.ds(r0, t), :] = d
                    q = jnp.dot(d, wmat, preferred_element_type=F32)
                    y_ref[pl.ds(r0, t), :] = (q * scale).astype(BF16)
                    return 0

                lax.fori_loop(0, n_chunks, chunk, 0, unroll=2)

    return pl.pallas_call(
        body, name="pool_fwd", grid=(len(POOL_WINDOWS),),
        in_specs=[_bs((None, seq, pg), lambda g: (0, 0, g)), _bs((None, pg, pg), lambda g: (g, 0, 0)),
                  _bs((1, pg), lambda g: (0, g))],
        out_specs=[_bs((seq, pg), lambda g: (0, g)), _bs((seq, pg), lambda g: (0, g))],
        out_shape=[jax.ShapeDtypeStruct((seq, dp), BF16), jax.ShapeDtypeStruct((seq, d_model), BF16)],
        scratch_shapes=[pltpu.VMEM((seq + 2 * h, pg), F32)],
        compiler_params=_params(1),
    )(p3, w_pool, pool_scale)


def _pool_bwd(d, dy, w_pool, pool_scale, token, seq, d_model):
    dp = d_model // 2
    pg = dp // len(POOL_WINDOWS)
    t = min(POOL_CHUNK, seq)
    n_chunks = seq // t
    h = WIN_HALO
    tn_dims = (((0,), (0,)), ((), ()))
    nt_dims = (((1,), (1,)), ((), ()))

    def body(d_ref, dy_ref, w_ref, sc_ref, tok_ref, du_ref, dwb_ref, dsc_ref, pad_ref, dd_ref, dw_ref):
        del tok_ref
        g = pl.program_id(0)
        zeros = jnp.zeros((h, pg), F32)
        pad_ref[0:h, :] = zeros
        pad_ref[h + seq:h + seq + h, :] = zeros
        wmat = w_ref[...]
        scale = sc_ref[...]
        for gi, w in enumerate(POOL_WINDOWS):
            @pl.when(g == gi)
            def _(w=w):
                dw_ref[...] = jnp.zeros((pg, pg), F32)

                def first(ci, dsc):
                    r0 = pl.multiple_of(ci * t, t)
                    dv = d_ref[pl.ds(r0, t), :]
                    dyv = dy_ref[pl.ds(r0, t), :]
                    q = jnp.dot(dv, wmat, preferred_element_type=F32)
                    dsc = dsc + jnp.sum(dyv * q, axis=0, keepdims=True)
                    dq = (dyv * scale).astype(BF16)
                    dw_ref[...] += lax.dot_general(dv, dq, tn_dims, preferred_element_type=F32)
                    dd = lax.dot_general(dq, wmat, nt_dims, preferred_element_type=F32)
                    dd_ref[pl.ds(r0, t), :] = dd
                    pad_ref[pl.ds(h + r0, t), :] = dd * _inv_count(r0, t, w, seq)
                    return dsc

                def first_pair(cj, dsc):
                    return first(2 * cj + 1, first(2 * cj, dsc))

                dsc_ref[...] = lax.fori_loop(0, n_chunks // 2, first_pair, jnp.zeros((1, pg), F32))
                dwb_ref[...] = dw_ref[...].reshape(N_DEV, pg // N_DEV, pg).astype(BF16)

                def second(ci, _):
                    r0 = pl.multiple_of(ci * t, t)
                    ext = pad_ref[pl.ds(r0, t + 2 * h), :]
                    back = _win_sum(ext, w, 1)[h:h + t, :]
                    du_ref[pl.ds(r0, t), :] = (back - dd_ref[pl.ds(r0, t), :]).astype(BF16)
                    return 0

                lax.fori_loop(0, n_chunks, second, 0, unroll=2)

    return pl.pallas_call(
        body, name="pool_bwd", grid=(len(POOL_WINDOWS),),
        in_specs=[_bs((seq, pg), lambda g: (0, g)), _bs((seq, pg), lambda g: (0, g)),
                  _bs((None, pg, pg), lambda g: (g, 0, 0)), _bs((1, pg), lambda g: (0, g)),
                  _bs((8, 128), lambda g: (0, 0))],
        out_specs=[_bs((seq, pg), lambda g: (0, g)), _bs((N_DEV, None, pg // N_DEV, pg), lambda g: (0, g, 0, 0)),
                   _bs((1, pg), lambda g: (0, g))],
        out_shape=[jax.ShapeDtypeStruct((seq, 3 * dp), BF16),
                   jax.ShapeDtypeStruct((N_DEV, len(POOL_WINDOWS), pg // N_DEV, pg), BF16),
                   jax.ShapeDtypeStruct((1, dp), F32)],
        scratch_shapes=[pltpu.VMEM((seq + 2 * h, pg), F32), pltpu.VMEM((seq, pg), F32), pltpu.VMEM((pg, pg), F32)],
        compiler_params=_params(1),
    )(d, dy, w_pool, pool_scale, token)


def _tile_scan(n_tiles, lanes, loads, stores):
    row = lax.broadcasted_iota(jnp.int32, (8, lanes), 0)
    group = 8

    def local_scan(n, k):
        aa, bb = loads[n](k)
        for sh in (1, 2, 4):
            if n == 0:
                ok = row >= sh
                ap = jnp.where(ok, pltpu.roll(aa, sh, 0), 1.0)
                bp = jnp.where(ok, pltpu.roll(bb, sh, 0), 0.0)
            else:
                ok = row < 8 - sh
                ap = jnp.where(ok, pltpu.roll(aa, 8 - sh, 0), 1.0)
                bp = jnp.where(ok, pltpu.roll(bb, 8 - sh, 0), 0.0)
            bb = aa * bp + bb
            aa = aa * ap
        return aa, bb

    def step(s, carry):
        carry = list(carry)
        for n in range(2):
            tiles = [s * group + u if n == 0 else n_tiles - 1 - (s * group + u) for u in range(group)]
            local = [local_scan(n, k) for k in tiles]
            for k, (aa, bb) in zip(tiles, local):
                hh = bb + aa * carry[n]
                stores[n](k, hh)
                carry[n] = jnp.broadcast_to(hh[7:8, :] if n == 0 else hh[0:1, :], (8, lanes))
        return tuple(carry)

    zeros = jnp.zeros((8, lanes), F32)
    lax.fori_loop(0, n_tiles // group, step, (zeros, zeros))


def _gate_preacts(xc, wcat_ref):
    xcb = xc.astype(BF16)
    return xcb, jnp.dot(xcb, wcat_ref[...], preferred_element_type=F32)


def _gates(pre, n, pk_ref, sp):
    lh = pre.shape[1] // 4
    r = _sigmoid(pre[:, (2 * n) * lh:(2 * n + 1) * lh] + pk_ref[pl.ds(4 + n, 1), :])
    i = _sigmoid(pre[:, (2 * n + 1) * lh:(2 * n + 2) * lh] + pk_ref[pl.ds(6 + n, 1), :])
    log_a = (-RG_C * r) * sp[n]
    a = jnp.exp(log_a)
    x = 2.0 * log_a
    one_minus_a2 = jnp.where(x > -0.01, -(x * (1.0 + x * (0.5 + x * (1.0 / 6.0)))), 1.0 - a * a)
    m = jnp.sqrt(one_minus_a2)
    return r, i, a, m


def _conv_chunk(upad_ref, pk_ref, cb, r0, t):
    ext = upad_ref[pl.ds(r0, t + 2 * CONV_HALO), :]
    acc = pk_ref[pl.ds(1, 1), :] * ext
    for k in (0, 2, 3):
        acc = acc + pk_ref[pl.ds(k, 1), :] * _shift(ext, k - 1)
    return acc[CONV_HALO:CONV_HALO + t, :] + cb, ext


def _lru_fwd(p3, y_in, pack, conv_b, wcat, token, seq, d_model):
    dl = d_model // 2
    lh = dl // N_HEADS
    t = min(SEQ_CHUNK, seq)
    n_chunks = seq // t
    hal = CONV_HALO
    first_rec_block = (d_model - dl) // lh

    def body(ur_ref, ug_ref, pk_ref, cb_ref, wcat_ref, yin_ref, tok_ref, y_ref, h0_ref, h1_ref,
             upad, a_scr, b_scr):
        del yin_ref, tok_ref
        zeros = jnp.zeros((hal, lh), F32)
        upad[0:hal, :] = zeros
        upad[hal + seq:hal + seq + hal, :] = zeros
        for ref in (h0_ref, h1_ref):
            ref[0:hal, :] = zeros
            ref[hal + seq:hal + seq + hal, :] = zeros

        def fill(ci, _):
            r0 = pl.multiple_of(ci * t, t)
            upad[pl.ds(hal + r0, t), :] = ur_ref[pl.ds(r0, t), :]
            return 0

        lax.fori_loop(0, n_chunks, fill, 0)
        cb = cb_ref[...]
        sp = [_softplus(-pk_ref[pl.ds(8 + n, 1), :]) for n in range(2)]

        def chunk(ci, _):
            r0 = pl.multiple_of(ci * t, t)
            xc, _ext = _conv_chunk(upad, pk_ref, cb, r0, t)
            _, pre = _gate_preacts(xc, wcat_ref)
            for n in range(2):
                _, i, a, m = _gates(pre, n, pk_ref, sp)
                a_scr[n, pl.ds(r0, t), :] = a
                b_scr[n, pl.ds(r0, t), :] = (m * i) * xc
            return 0

        lax.fori_loop(0, n_chunks, chunk, 0, unroll=2)

        def load(n):
            def get(k):
                at = pl.ds(pl.multiple_of(k * 8, 8), 8)
                return a_scr[n, at, :], b_scr[n, at, :]
            return get

        def store(ref):
            def put(k, v):
                ref[pl.ds(pl.multiple_of(hal + k * 8, 8), 8), :] = v
            return put

        _tile_scan(seq // 8, lh, [load(0), load(1)], [store(h0_ref), store(h1_ref)])

        def out(ci, _):
            r0 = pl.multiple_of(ci * t, t)
            hsum = h0_ref[pl.ds(hal + r0, t), :] + h1_ref[pl.ds(hal + r0, t), :]
            gl, _dg = _gelu_and_grad(ug_ref[pl.ds(r0, t), :])
            y_ref[pl.ds(r0, t), :] = (hsum * gl).astype(BF16)
            return 0

        lax.fori_loop(0, n_chunks, out, 0)

    return pl.pallas_call(
        body, name="lru_fwd", grid=(N_HEADS,),
        in_specs=[_bs((None, seq, lh), lambda h: (1, 0, h)), _bs((None, seq, lh), lambda h: (2, 0, h)),
                  _bs((None, SMALL_ROWS, lh), lambda h: (h, 0, 0)), _bs((1, lh), lambda h: (0, h)),
                  _bs((None, lh, 4 * lh), lambda h: (h, 0, 0)),
                  ANY, _bs((8, 128), lambda h: (0, 0))],
        out_specs=[_bs((seq, lh), lambda h: (0, first_rec_block + h)),
                   _bs((seq + 2 * hal, lh), lambda h: (0, h)), _bs((seq + 2 * hal, lh), lambda h: (0, h))],
        out_shape=[jax.ShapeDtypeStruct((seq, d_model), BF16), jax.ShapeDtypeStruct((seq + 2 * hal, dl), F32),
                   jax.ShapeDtypeStruct((seq + 2 * hal, dl), F32)],
        scratch_shapes=[pltpu.VMEM((seq + 2 * hal, lh), F32), pltpu.VMEM((2, seq, lh), F32),
                        pltpu.VMEM((2, seq, lh), F32)],
        input_output_aliases={5: 0},
        compiler_params=_params(1),
    )(p3, p3, pack, conv_b, wcat, y_in, token)


def _lru_bwd(p3, dy, h0p, h1p, dproj_in, pack, conv_b, wcat, token, seq, d_model):
    dl = d_model // 2
    lh = dl // N_HEADS
    t = min(SEQ_CHUNK, seq)
    n_chunks = seq // t
    hal = CONV_HALO
    first_rec_block = (d_model - dl) // lh
    tn_dims = (((0,), (0,)), ((), ()))
    nt_dims = (((1,), (1,)), ((), ()))

    def body(ur_ref, ug_ref, dy_ref, h0_ref, h1_ref, pk_ref, cb_ref, wcat_ref, tok_ref, din_ref,
             dproj_ref, dpk_ref, dcb_ref, dwcat_ref,
             upad, a_scr, dh_scr, g_scr, dxc_pad, dpr_ref, out_sems, gate_scr):
        del din_ref, tok_ref
        zeros = jnp.zeros((hal, lh), F32)
        for ref in (upad, dxc_pad):
            ref[0:hal, :] = zeros
            ref[hal + seq:hal + seq + hal, :] = zeros
        for n in range(2):
            a_scr[n, 0:hal, :] = zeros
            a_scr[n, hal + seq:hal + seq + hal, :] = zeros

        def fill(ci, _):
            r0 = pl.multiple_of(ci * t, t)
            upad[pl.ds(hal + r0, t), :] = ur_ref[pl.ds(r0, t), :]
            return 0

        lax.fori_loop(0, n_chunks, fill, 0)
        cb = cb_ref[...]
        lam = [pk_ref[pl.ds(8 + n, 1), :] for n in range(2)]
        sp = [_softplus(-lam[n]) for n in range(2)]

        def chunk1(ci, _):
            r0 = pl.multiple_of(ci * t, t)
            xc, _ext = _conv_chunk(upad, pk_ref, cb, r0, t)
            _, pre = _gate_preacts(xc, wcat_ref)
            for n in range(2):
                r, i, a, m = _gates(pre, n, pk_ref, sp)
                a_scr[n, pl.ds(hal + r0, t), :] = a
                for q, v in enumerate((r, i, m)):
                    gate_scr[3 * n + q, pl.ds(r0, t), :] = v
            hsum = h0_ref[pl.ds(hal + r0, t), :] + h1_ref[pl.ds(hal + r0, t), :]
            gl, dgl = _gelu_and_grad(ug_ref[pl.ds(r0, t), :])
            dyv = dy_ref[pl.ds(r0, t), :]
            dh_scr[pl.ds(r0, t), :] = dyv * gl
            dpr_ref[1, pl.ds(r0, t), :] = ((dyv * hsum) * dgl).astype(BF16)
            return 0

        lax.fori_loop(0, n_chunks, chunk1, 0, unroll=2)

        def load(n):
            def get(k):
                r0 = pl.multiple_of(k * 8, 8)
                if n == 0:
                    coef = _shift(a_scr[0, pl.ds(pl.multiple_of(hal + r0, 8), 16), :], 1)[0:8, :]
                else:
                    coef = _shift(a_scr[1, pl.ds(pl.multiple_of(hal + r0 - 8, 8), 16), :], -1)[8:16, :]
                return coef, dh_scr[pl.ds(r0, 8), :]
            return get

        def store(n):
            def put(k, v):
                g_scr[n, pl.ds(pl.multiple_of(k * 8, 8), 8), :] = v
            return put

        _tile_scan(seq // 8, lh, [load(1), load(0)], [store(1), store(0)])

        dwcat_ref[...] = jnp.zeros((lh, 4 * lh), F32)

        def chunk3(ci, carry):
            dba, dbi, dlam, dcb = carry
            r0 = pl.multiple_of(ci * t, t)
            xc, _ext = _conv_chunk(upad, pk_ref, cb, r0, t)
            xcb = xc.astype(BF16)
            dxc = jnp.zeros((t, lh), F32)
            dba, dbi, dlam = list(dba), list(dbi), list(dlam)
            dpre = []
            for n in range(2):
                r, i, m = (gate_scr[3 * n + q, pl.ds(r0, t), :] for q in range(3))
                a = a_scr[n, pl.ds(hal + r0, t), :]
                hext = (h0_ref if n == 0 else h1_ref)[pl.ds(r0, t + 2 * hal), :]
                hprev = _shift(hext, -1 if n == 0 else 1)[hal:hal + t, :]
                gb = g_scr[n, pl.ds(r0, t), :]
                da = gb * hprev
                dm = gb * i * xc
                di = gb * m * xc
                dxc = dxc + gb * (m * i)
                dlog_a = da * a - dm * (a * a) / m
                dr = dlog_a * (-RG_C * sp[n])
                dlam[n] = dlam[n] + jnp.sum(dlog_a * r, axis=0, keepdims=True)
                dpr = dr * r * (1.0 - r)
                dpi = di * i * (1.0 - i)
                dba[n] = dba[n] + jnp.sum(dpr, axis=0, keepdims=True)
                dbi[n] = dbi[n] + jnp.sum(dpi, axis=0, keepdims=True)
                dpre += [dpr.astype(BF16), dpi.astype(BF16)]
            dpre = jnp.concatenate(dpre, axis=1)
            dwcat_ref[...] += lax.dot_general(xcb, dpre, tn_dims, preferred_element_type=F32)
            dxc = dxc + lax.dot_general(dpre, wcat_ref[...], nt_dims, preferred_element_type=F32)
            dxc_pad[pl.ds(hal + r0, t), :] = dxc
            dcb = dcb + jnp.sum(dxc, axis=0, keepdims=True)
            return tuple(dba), tuple(dbi), tuple(dlam), dcb

        zr = jnp.zeros((1, lh), F32)
        def chunk3_pair(cj, carry):
            return chunk3(2 * cj + 1, chunk3(2 * cj, carry))

        dba, dbi, dlam, dcb = lax.fori_loop(0, n_chunks // 2, chunk3_pair, ((zr, zr), (zr, zr), (zr, zr), zr))
        dcb_ref[...] = dcb
        for n in range(2):
            dpk_ref[pl.ds(4 + n, 1), :] = dba[n]
            dpk_ref[pl.ds(6 + n, 1), :] = dbi[n]
            dpk_ref[pl.ds(8 + n, 1), :] = dlam[n] * (RG_C * jax.nn.sigmoid(-lam[n]))
        dpk_ref[pl.ds(10, SMALL_ROWS - 10), :] = jnp.zeros((SMALL_ROWS - 10, lh), F32)

        def chunk4(ci, dtap):
            r0 = pl.multiple_of(ci * t, t)
            gext = dxc_pad[pl.ds(r0, t + 2 * hal), :]
            uext = upad[pl.ds(r0, t + 2 * hal), :]
            gmid = gext[hal:hal + t, :]
            du = pk_ref[pl.ds(1, 1), :] * gext
            for k in (0, 2, 3):
                du = du + pk_ref[pl.ds(k, 1), :] * _shift(gext, 1 - k)
            dpr_ref[0, pl.ds(r0, t), :] = du[hal:hal + t, :].astype(BF16)
            out = []
            for k in range(4):
                usl = _shift(uext, k - 1)[hal:hal + t, :]
                out.append(dtap[k] + jnp.sum(gmid * usl, axis=0, keepdims=True))
            return tuple(out)

        dtap = lax.fori_loop(0, n_chunks, chunk4, (zr, zr, zr, zr))
        for k in range(4):
            dpk_ref[pl.ds(k, 1), :] = dtap[k]

        head = pl.program_id(0)
        outs = [pltpu.make_async_copy(
            dpr_ref.at[b], dproj_ref.at[:, pl.ds(pl.multiple_of((1 + b) * dl + head * lh, lh), lh)], out_sems.at[b])
            for b in range(2)]
        for cp in outs:
            cp.start()
        for cp in outs:
            cp.wait()

    return pl.pallas_call(
        body, name="lru_bwd", grid=(N_HEADS,),
        in_specs=[_bs((None, seq, lh), lambda h: (1, 0, h)), _bs((None, seq, lh), lambda h: (2, 0, h)),
                  _bs((seq, lh), lambda h: (0, first_rec_block + h)),
                  _bs((seq + 2 * hal, lh), lambda h: (0, h)), _bs((seq + 2 * hal, lh), lambda h: (0, h)),
                  _bs((None, SMALL_ROWS, lh), lambda h: (h, 0, 0)), _bs((1, lh), lambda h: (0, h)),
                  _bs((None, lh, 4 * lh), lambda h: (h, 0, 0)),
                  _bs((8, 128), lambda h: (0, 0)), ANY],
        out_specs=[ANY, _bs((None, SMALL_ROWS, lh), lambda h: (h, 0, 0)),
                   _bs((1, lh), lambda h: (0, h)), _bs((None, lh, 4 * lh), lambda h: (h, 0, 0))],
        out_shape=[jax.ShapeDtypeStruct((seq, 3 * dl), BF16), jax.ShapeDtypeStruct((N_HEADS, SMALL_ROWS, lh), F32),
                   jax.ShapeDtypeStruct((1, dl), F32), jax.ShapeDtypeStruct((N_HEADS, lh, 4 * lh), F32)],
        scratch_shapes=[pltpu.VMEM((seq + 2 * hal, lh), F32), pltpu.VMEM((2, seq + 2 * hal, lh), F32),
                        pltpu.VMEM((seq, lh), F32), pltpu.VMEM((2, seq, lh), F32),
                        pltpu.VMEM((seq + 2 * hal, lh), F32), pltpu.VMEM((2, seq, lh), BF16),
                        pltpu.SemaphoreType.DMA((2,)), pltpu.VMEM((6, seq, lh), F32)],
        input_output_aliases={9: 0},
        compiler_params=_params(1),
    )(p3, p3, dy, h0p, h1p, pack, conv_b, wcat, token, dproj_in)


class _tiles:
    def __init__(self, seq, d_model, d_ff):
        self.rows = min(1024, seq)
        self.ln_rows = min(256, seq)
        self.ff_cols = min(2048, d_ff)
        self.ff_split = 4
        self.ff_k = min(2048, d_ff)
        self.grad_rows = 512


def _ln_loss_bwd(ffn, x1, tgt, g, b, tr):
    seq, d = ffn.shape

    def body(f_ref, x_ref, t_ref, g_ref, b_ref, dz_ref, dzb_ref, dg_ref, db_ref, loss_ref):
        i = pl.program_id(0)
        gv = g_ref[...]
        z = ALPHA * x_ref[...] + f_ref[...]
        y, xhat, rstd = _ln_fwd(z, gv, b_ref[...])
        err = y - t_ref[...]
        part = 0.5 * jnp.sum(jnp.mean(err * err, axis=-1, keepdims=True), axis=0, keepdims=True)
        dz, dg, db = _ln_bwd(err * (1.0 / d), xhat, rstd, gv)
        dz_ref[...] = dz
        dzb_ref[...] = dz.astype(BF16)
        _acc_rows(dg_ref, i == 0, dg)
        _acc_rows(db_ref, i == 0, db)
        _acc_rows(loss_ref, i == 0, jnp.broadcast_to(part, (8, 128)))

    row = _bs((tr, d), lambda i: (i, 0))
    vec = _bs((1, d), lambda i: (0, 0))
    return pl.pallas_call(
        body, name="ln_ffn_loss", grid=(seq // tr,), in_specs=[row, row, row, vec, vec],
        out_specs=[row, row, vec, vec, _bs((8, 128), lambda i: (0, 0))],
        out_shape=[jax.ShapeDtypeStruct((seq, d), F32), jax.ShapeDtypeStruct((seq, d), BF16),
                   jax.ShapeDtypeStruct((1, d), F32), jax.ShapeDtypeStruct((1, d), F32),
                   jax.ShapeDtypeStruct((8, 128), F32)],
        compiler_params=_params(1),
    )(ffn, x1, tgt, g, b)


def _ln_bwd_side(dx_branch, dres, z, g, b, n_steps):
    seq, d = z.shape
    tr = seq // n_steps

    def fn(step, ins, outs):
        a_ref, r_ref, z_ref, g_ref, b_ref = ins
        dz_ref, dzb_ref, dg_ref, db_ref = outs
        gv = g_ref[...]
        _, xhat, rstd = _ln_fwd(z_ref[...], gv, b_ref[...])
        dz, dg, db = _ln_bwd(ALPHA * r_ref[...] + a_ref[...], xhat, rstd, gv)
        dz_ref[...] = dz
        dzb_ref[...] = dz.astype(BF16)
        _acc_rows(dg_ref, step == 0, dg)
        _acc_rows(db_ref, step == 0, db)

    row = ((tr, d), lambda s: (s, 0))
    vec = ((1, d), lambda s: (0, 0))
    shapes = [jax.ShapeDtypeStruct((seq, d), F32), jax.ShapeDtypeStruct((seq, d), BF16),
              jax.ShapeDtypeStruct((1, d), F32), jax.ShapeDtypeStruct((1, d), F32)]
    return [(dx_branch, *row), (dres, *row), (z, *row), (g, *vec), (b, *vec)], shapes, [row, row, vec, vec], fn


def _to_bf16(name, a, token):
    rows, cols = a.shape
    tr = min(512, rows)

    def body(a_ref, tok_ref, o_ref):
        del tok_ref
        o_ref[...] = a_ref[...].astype(BF16)

    return pl.pallas_call(
        body, name=name, grid=(rows // tr,),
        in_specs=[_bs((tr, cols), lambda i: (i, 0)), _bs((8, 128), lambda i: (0, 0))],
        out_specs=_bs((tr, cols), lambda i: (i, 0)), out_shape=jax.ShapeDtypeStruct((rows, cols), BF16),
        compiler_params=_params(1),
    )(a, token)


def _sum_blocks(name, parts):
    def body(p_ref, o_ref):
        acc = p_ref[0]
        for s in range(1, parts.shape[0]):
            acc = acc + p_ref[s]
        o_ref[...] = acc

    return pl.pallas_call(body, name=name, out_shape=jax.ShapeDtypeStruct(parts.shape[1:], F32))(parts)


def _adamw_values(w, g, m, v):
    m = ADAM_B1 * m + (1.0 - ADAM_B1) * g
    v = ADAM_B2 * v + (1.0 - ADAM_B2) * (g * g)
    m_hat = m / (1.0 - ADAM_B1 ** ADAM_STEP)
    v_hat = v / (1.0 - ADAM_B2 ** ADAM_STEP)
    delta = -ADAM_LR * (m_hat / (jnp.sqrt(v_hat) + ADAM_EPS) + ADAM_WD * w)
    return delta, m, v


def _adamw_side(own, parts, w, m, v, n_steps):
    rows, cols = w.shape
    tr = rows // n_steps

    def fn(step, ins, outs):
        o_ref, p_ref, w_ref, m_ref, v_ref = ins
        g = o_ref[...]
        for s in range(parts.shape[0]):
            g = g + p_ref[s].astype(F32)
        delta, mn, vn = _adamw_values(w_ref[...], g, m_ref[...], v_ref[...])
        for ref, val in zip(outs, (g, delta, mn, vn)):
            ref[...] = val

    row = ((tr, cols), lambda s: (s, 0))
    stack = ((parts.shape[0], tr, cols), lambda s: (0, s, 0))
    shapes = [jax.ShapeDtypeStruct((rows, cols), F32)] * 4
    return [(own, *row), (parts, *stack), (w, *row), (m, *row), (v, *row)], shapes, [row] * 4, fn


def _sum_adamw(name, own, parts, w, m, v):
    rows, cols = w.shape
    n_parts = parts.shape[0]
    tr = rows
    min_rows = 8 if parts.dtype == F32 else 16
    while tr * cols * 4 > 2 * 1024 * 1024 and tr % (2 * min_rows) == 0:
        tr //= 2

    def body(*refs):
        if own is None:
            p_ref, w_ref, m_ref, v_ref, g_ref, d_ref, mo_ref, vo_ref = refs
            g = p_ref[0].astype(F32)
            rest = range(1, n_parts)
        else:
            o_ref, p_ref, w_ref, m_ref, v_ref, g_ref, d_ref, mo_ref, vo_ref = refs
            g = o_ref[...]
            rest = range(n_parts)
        for s in rest:
            g = g + p_ref[s].astype(F32)
        delta, mn, vn = _adamw_values(w_ref[...], g, m_ref[...], v_ref[...])
        g_ref[...] = g
        d_ref[...] = delta
        mo_ref[...] = mn
        vo_ref[...] = vn

    spec = _bs((tr, cols), lambda i: (i, 0))
    lead = [] if own is None else [own]
    return pl.pallas_call(
        body, name=name, grid=(rows // tr,),
        in_specs=[spec] * len(lead) + [_bs((n_parts, tr, cols), lambda i: (0, i, 0)), spec, spec, spec],
        out_specs=[spec] * 4, out_shape=[jax.ShapeDtypeStruct((rows, cols), F32)] * 4,
        compiler_params=_params(1),
    )(*lead, parts, w, m, v)


def _rows128(a):
    return a.reshape(-1, 128)


def kernel(x, ln_mix_g, ln_mix_b, w_in, w_pool, pool_scale, conv_w, conv_b, w_rg_a, b_rg_a, w_rg_i, b_rg_i, rg_lambda, w_out, ln_ffn_g, ln_ffn_b, w_mlp_in, w_mlp_out, loss_target, m_ln_mix_g, m_ln_mix_b, m_w_in, m_w_pool, m_pool_scale, m_conv_w, m_conv_b, m_w_rg_a, m_b_rg_a, m_w_rg_i, m_b_rg_i, m_rg_lambda, m_w_out, m_ln_ffn_g, m_ln_ffn_b, m_w_mlp_in, m_w_mlp_out, v_ln_mix_g, v_ln_mix_b, v_w_in, v_w_pool, v_pool_scale, v_conv_w, v_conv_b, v_w_rg_a, v_b_rg_a, v_w_rg_i, v_b_rg_i, v_rg_lambda, v_w_out, v_ln_ffn_g, v_ln_ffn_b, v_w_mlp_in, v_w_mlp_out):
    seq, d_model = x.shape[1], x.shape[2]
    dh = d_model // 2
    lh = dh // N_HEADS
    pg = dh // len(POOL_WINDOWS)
    d_ff = w_mlp_in.shape[2] * N_DEV
    assert lh == 128 and conv_w.shape[3] == lh and w_pool.shape[2] * N_DEV == pg

    xs = x[0]
    tgt = loss_target[0]

    def small_pack(cw, ba, bi, lam):
        return jnp.concatenate([cw.reshape(4, lh), ba.reshape(2, lh), bi.reshape(2, lh), lam.reshape(2, lh),
                                jnp.zeros((SMALL_ROWS - 10, lh), F32)], axis=0)

    pack_mine = small_pack(conv_w, b_rg_a, b_rg_i, rg_lambda)
    pack_bits = lax.bitcast_convert_type(pack_mine, BF16).reshape(1, SMALL_ROWS, 2 * lh)
    win_gather = _SplitGather("gather_w_in", [(w_in[0], 1), (w_pool[0], 1), (pack_bits, 0)], BF16, after=pack_mine)
    wout_gather = _SplitGather("gather_w_out", [(w_out[0], 0)], BF16, after=win_gather.token)
    w1_gather = _SplitGather("gather_w_mlp_in", [(w_mlp_in[0], 1)], BF16, after=wout_gather.token)
    w2_gather = _SplitGather("gather_w_mlp_out", [(w_mlp_out[0], 0)], BF16, after=w1_gather.token)
    xb = _to_bf16("x_bf16", x[0], w2_gather.token)
    win_full, wpool_full, pack_bits_full = win_gather.wait(after=win_gather.relay(after=xb))
    pack_full = lax.bitcast_convert_type(pack_bits_full.reshape(N_DEV, SMALL_ROWS, lh, 2), F32)
    wcat = jnp.concatenate([w_rg_a[0, 0], w_rg_i[0, 0], w_rg_a[0, 1], w_rg_i[0, 1]], axis=-1).astype(BF16)
    vec = lambda i, j, k: (0, 0)
    row_full = lambda i, j, k: (i, 0)

    def after(token):
        return (token, _sp((8, 128), vec))

    def sds(shape, dtype):
        return jax.ShapeDtypeStruct(shape, dtype)

    def plain_epi(acc, i, ex, out):
        out[0][...] = acc

    def bf16_epi(acc, i, ex, out):
        out[0][...] = acc.astype(BF16)

    t = _tiles(seq, d_model, d_ff)

    (p3,) = _matmul(
        "proj", xb, win_full, _sp((t.rows, d_model), lambda i, j, k: (i, 0)), _sp((d_model, dh), lambda i, j, k: (0, j)),
        grid=(seq // t.rows, 3, 1),
        out_shape=[sds((3, seq, dh), F32)], out_specs=[_sp((None, t.rows, dh), lambda i, j, k: (j, i, 0))],
        epilogue=plain_epi)

    d_pool, y_half = _pool_fwd(p3, wpool_full, pool_scale, seq, d_model)
    y, h0p, h1p = _lru_fwd(p3, y_half, pack_full, conv_b, wcat, wout_gather.relay(after=y_half), seq, d_model)
    (wout_full,) = wout_gather.wait(after=y)
    relay_token = w1_gather.relay(after=wout_full)

    mix_rows = 2 * t.ln_rows

    def mix_epi(acc, i, ex, out):
        x_ref, g_ref, b_ref = ex[:3]
        for part in range(2):
            rows = pl.ds(part * t.ln_rows, t.ln_rows)
            z = ALPHA * x_ref[rows, :] + acc[part * t.ln_rows:(part + 1) * t.ln_rows, :]
            x1, _, _ = _ln_fwd(z, g_ref[...], b_ref[...])
            out[0][rows, :] = z
            out[1][rows, :] = x1
            out[2][rows, :] = x1.astype(BF16)

    z1, x1, x1b = _matmul(
        "mix_out", y, wout_full, _sp((mix_rows, d_model), row_full), _sp((d_model, d_model), vec, single=True),
        grid=(seq // mix_rows, 1, 1),
        extras=[(xs, _sp((mix_rows, d_model), row_full)), (ln_mix_g, _sp((1, d_model), vec)),
                (ln_mix_b, _sp((1, d_model), vec)), after(relay_token)],
        out_shape=[sds((seq, d_model), F32), sds((seq, d_model), F32), sds((seq, d_model), BF16)],
        out_specs=[_sp((mix_rows, d_model), row_full)] * 3, epilogue=mix_epi)
    (w1_full,) = w1_gather.wait(after=x1b)

    def mlp_in_epi(acc, i, ex, out, cols):
        h = jnp.maximum(acc, 0.0)
        out[0][:, cols] = (h * h).astype(BF16)
        out[1][:, cols] = (2.0 * h).astype(BF16)

    hmid, dact = _matmul(
        "mlp_in", x1b, w1_full, _sp((t.rows, d_model), lambda i, j, k: (i, 0)),
        _sp((d_model, t.ff_cols), lambda i, j, k: (0, j)),
        grid=(seq // t.rows, d_ff // t.ff_cols, 1), j_outer=True,
        out_shape=[sds((seq, d_ff), BF16)] * 2, out_specs=[_sp((t.rows, t.ff_cols), lambda i, j, k: (i, j))] * 2,
        epilogue=mlp_in_epi, n_split=t.ff_split)
    (w2_full,) = w2_gather.wait(after=w2_gather.relay(after=hmid))

    (ffn,) = _matmul(
        "mlp_out", hmid, w2_full, _sp((t.rows, t.ff_k), lambda i, j, k: (i, k)),
        _sp((t.ff_k, d_model), lambda i, j, k: (k, 0)),
        grid=(seq // t.rows, 1, d_ff // t.ff_k),
        out_shape=[sds((seq, d_model), F32)], out_specs=[_sp((t.rows, d_model), row_full)])
    dz2, dz2b, g_ffn_g, g_ffn_b, loss_part = _ln_loss_bwd(ffn, x1, tgt, ln_ffn_g, ln_ffn_b, t.ln_rows)

    (g_w2,) = _matmul(
        "grad_w_mlp_out", hmid, dz2b, _sp((seq, t.grad_rows), lambda i, j, k: (0, i)),
        _sp((seq, d_model), vec, single=True),
        grid=(d_ff // t.grad_rows, 1, 1), ta=True,
        out_shape=[sds((d_ff, d_model), BF16)], out_specs=[_sp((t.grad_rows, d_model), row_full)],
        epilogue=bf16_epi)
    scatter_w2 = _SplitReduceScatter("scatter_w_mlp_out", [g_w2.reshape(N_DEV, d_ff // N_DEV, d_model)])

    def dpre_epi(acc, i, ex, out, cols):
        out[0][:, cols] = (acc * ex[0][:, cols].astype(F32)).astype(BF16)

    (dpre,) = _matmul(
        "mlp_dpre", dz2b, w2_full, _sp((t.rows, d_model), lambda i, j, k: (i, 0)),
        _sp((t.ff_cols, d_model), lambda i, j, k: (j, 0)),
        grid=(seq // t.rows, d_ff // t.ff_cols, 1), j_outer=True, tb=True,
        extras=[(dact, _sp((t.rows, t.ff_cols), lambda i, j, k: (i, j))), after(scatter_w2.token)],
        out_shape=[sds((seq, d_ff), BF16)], out_specs=[_sp((t.rows, t.ff_cols), lambda i, j, k: (i, j))],
        epilogue=dpre_epi, n_split=t.ff_split)
    token_w2 = scatter_w2.combine_and_send(after=dpre)

    (dx1_mlp,) = _matmul(
        "mlp_dx", dpre, w1_full, _sp((t.rows, t.ff_k), lambda i, j, k: (i, k)),
        _sp((d_model, t.ff_k), lambda i, j, k: (0, k)),
        grid=(seq // t.rows, 1, d_ff // t.ff_k), tb=True, extras=[after(token_w2)],
        out_shape=[sds((seq, d_model), F32)], out_specs=[_sp((t.rows, d_model), row_full)])
    def block_epi(acc, i, ex, out):
        out[0][0] = acc.astype(BF16)

    fs = d_ff // N_DEV
    g_w1, dz1, dz1b, g_mix_g, g_mix_b = _matmul(
        "grad_w_mlp_in", x1b, dpre, _sp((seq, t.grad_rows), lambda i, j, k: (0, i)),
        _sp((seq, fs), lambda i, j, k: (0, j)),
        grid=(d_model // t.grad_rows, N_DEV, 1), j_outer=True, ta=True,
        out_shape=[sds((N_DEV, d_model, fs), BF16)],
        out_specs=[_sp((1, t.grad_rows, fs), lambda i, j, k: (j, i, 0))], epilogue=block_epi,
        side=_ln_bwd_side(dx1_mlp, dz2, z1, ln_mix_g, ln_mix_b, d_model // t.grad_rows * N_DEV))

    (dy,) = _matmul(
        "mix_dy", dz1b, wout_full, _sp((t.rows, d_model), lambda i, j, k: (i, 0)),
        _sp((dh, d_model), lambda i, j, k: (j, 0)),
        grid=(seq // t.rows, 2, 1), j_outer=True, tb=True,
        out_shape=[sds((seq, d_model), F32)], out_specs=[_sp((t.rows, dh), lambda i, j, k: (i, j))],
        epilogue=plain_epi)
    (g_wout,) = _matmul(
        "grad_w_out", y, dz1b, _sp((seq, t.grad_rows), lambda i, j, k: (0, i)), _sp((seq, d_model), vec, single=True),
        grid=(d_model // t.grad_rows, 1, 1), ta=True,
        out_shape=[sds((d_model, d_model), BF16)], out_specs=[_sp((t.grad_rows, d_model), row_full)],
        epilogue=bf16_epi)
    scatter_w1 = _SplitReduceScatter("scatter_w_mlp_in", [g_w1, g_wout.reshape(N_DEV, d_model // N_DEV, d_model)])

    dproj_pool, g_wpool, g_pscale = _pool_bwd(d_pool, dy, wpool_full, pool_scale, scatter_w1.token, seq, d_model)
    token_w1 = scatter_w1.combine_and_send(after=dproj_pool)
    dproj, g_pack, g_convb, g_wcat = _lru_bwd(p3, dy, h0p, h1p, dproj_pool, pack_full, conv_b, wcat,
                                              token_w1, seq, d_model)
    g_wa = jnp.stack([g_wcat[:, :, 0:lh], g_wcat[:, :, 2 * lh:3 * lh]])
    g_wi = jnp.stack([g_wcat[:, :, lh:2 * lh], g_wcat[:, :, 3 * lh:4 * lh]])

    rep_parts = [_rows128(g_wa), _rows128(g_wi), _rows128(g_mix_g), _rows128(g_mix_b), _rows128(g_ffn_g),
                 _rows128(g_ffn_b), _rows128(g_pscale), _rows128(g_convb)]
    rep_rows = [p.shape[0] for p in rep_parts]
    n_rep = sum(rep_rows)
    small = jnp.concatenate(rep_parts + [_rows128(g_pack), loss_part], axis=0)
    small_gather = _SplitGather("gather_small_grads", [(small[None], 0)], F32, after=small)

    ws = 3 * dh // N_DEV

    def pair_epi(acc, i, ex, out):
        out[0][0] = acc[:, :ws].astype(BF16)
        out[0][1] = acc[:, ws:].astype(BF16)

    def adam_big(name, own_landed, w, m, v):
        own, landed = own_landed
        shp = w.shape
        two = lambda a: a.reshape(-1, shp[-1])
        res = _sum_adamw(name, own, landed, two(w), two(m), two(v))
        return [r.reshape(shp) for r in res]

    (r_w2,) = scatter_w2.wait(after=small_gather.token)
    n_steps = d_model // t.grad_rows * (N_DEV // 2)
    g_win, *o_w2 = _matmul(
        "grad_w_in", xb, dproj, _sp((seq, t.grad_rows), lambda i, j, k: (0, i)),
        _sp((seq, 2 * ws), lambda i, j, k: (0, j)),
        grid=(d_model // t.grad_rows, N_DEV // 2, 1), ta=True,
        out_shape=[sds((N_DEV, d_model, ws), BF16)],
        out_specs=[_sp((2, t.grad_rows, ws), lambda i, j, k: (j, i, 0))], epilogue=pair_epi,
        side=_adamw_side(r_w2[0], r_w2[1], w_mlp_out[0], m_w_mlp_out[0], v_w_mlp_out[0], n_steps))
    o_w2 = [r.reshape(w_mlp_out.shape) for r in o_w2]
    scatter_mix = _SplitReduceScatter(
        "scatter_mixer", [g_win, g_wpool.reshape(N_DEV, pg // N_DEV * len(POOL_WINDOWS), pg)])

    r_w1, r_wout = scatter_w1.wait(after=scatter_mix.token)
    o_w1 = adam_big("adam_w_mlp_in", r_w1, w_mlp_in, m_w_mlp_in, v_w_mlp_in)
    token_mix = small_gather.relay(after=scatter_mix.combine_and_send(after=o_w1[0]))

    def dx_epi(acc, i, ex, out):
        out[0][...] = ALPHA * ex[0][...] + acc

    dx_rows = t.ln_rows * 2
    dx, *o_wout = _matmul(
        "grad_x", dproj, win_full, _sp((dx_rows, 3 * dh), lambda i, j, k: (i, 0)),
        _sp((d_model, 3 * dh), vec, single=True),
        grid=(seq // dx_rows, 1, 1), tb=True,
        extras=[(dz1, _sp((dx_rows, d_model), row_full)), after(token_mix)],
        out_shape=[sds((seq, d_model), F32)], out_specs=[_sp((dx_rows, d_model), row_full)],
        epilogue=dx_epi,
        side=_adamw_side(r_wout[0], r_wout[1], w_out[0], m_w_out[0], v_w_out[0], seq // dx_rows))
    o_wout = [r.reshape(w_out.shape) for r in o_wout]
    r_win, r_wpool = scatter_mix.wait(after=dx)
    o_win = adam_big("adam_w_in", r_win, w_in, m_w_in, v_w_in)
    o_wpool = adam_big("adam_w_pool", r_wpool, w_pool, m_w_pool, v_w_pool)

    (small_all,) = small_gather.wait(after=o_wpool[0])

    rep_w = [w_rg_a, w_rg_i, ln_mix_g, ln_mix_b, ln_ffn_g, ln_ffn_b, pool_scale, conv_b]
    rep_m = [m_w_rg_a, m_w_rg_i, m_ln_mix_g, m_ln_mix_b, m_ln_ffn_g, m_ln_ffn_b, m_pool_scale, m_conv_b]
    rep_v = [v_w_rg_a, v_w_rg_i, v_ln_mix_g, v_ln_mix_b, v_ln_ffn_g, v_ln_ffn_b, v_pool_scale, v_conv_b]
    cat = lambda arrs: jnp.concatenate([_rows128(a) for a in arrs], axis=0)
    o_rep = _sum_adamw("adam_replicated", None, small_all, cat(rep_w), cat(rep_m), cat(rep_v))

    my_idx = _dev_index(_where_am_i())
    head_parts = lax.dynamic_slice_in_dim(small_all, n_rep + my_idx * SMALL_ROWS, SMALL_ROWS, axis=1)
    o_head = _sum_adamw("adam_head", None, head_parts, pack_mine,
                        small_pack(m_conv_w, m_b_rg_a, m_b_rg_i, m_rg_lambda),
                        small_pack(v_conv_w, v_b_rg_a, v_b_rg_i, v_rg_lambda))

    def unpack_rep(packed):
        out, r = [], 0
        for wgt, rows in zip(rep_w, rep_rows):
            out.append(packed[r:r + rows].reshape(wgt.shape))
            r += rows
        return out

    def unpack_head(packed):
        return [packed[0:4].reshape(conv_w.shape), packed[4:6].reshape(b_rg_a.shape),
                packed[6:8].reshape(b_rg_i.shape), packed[8:10].reshape(rg_lambda.shape)]

    loss = _sum_blocks("loss_sum", small_all[:, n_rep + N_HEADS * SMALL_ROWS:, :])[0, 0]

    outs = [loss, dx[None]]
    for kind in range(4):
        ra, ri, mg, mb, fg, fb, ps, cb = unpack_rep(o_rep[kind])
        cw, ba, bi, lam = unpack_head(o_head[kind])
        outs += [mg, mb, o_win[kind], o_wpool[kind], ps, cw, cb, ra, ba, ri, bi, lam, o_wout[kind], fg, fb,
                 o_w1[kind], o_w2[kind]]
    return tuple(outs)
```

```python
import functools

import jax
import jax.numpy as jnp
from jax import lax
from jax.experimental import pallas as pl
from jax.experimental.pallas import tpu as pltpu

F32 = jnp.float32
BF16 = jnp.bfloat16
MESH = pl.DeviceIdType.MESH
ANY = pl.BlockSpec(memory_space=pl.ANY)

N_DEV = 8
POOL_WINDOWS = (2, 4, 8, 16)
N_HEADS = 8
RG_C = 8.0
LN_EPS = 1e-5
ALPHA = 2.0 ** 0.25
ADAM_LR = 0.001
ADAM_B1 = 0.9
ADAM_B2 = 0.999
ADAM_EPS = 1e-08
ADAM_WD = 0.01
ADAM_STEP = 10

VMEM_LIMIT = 56 * 1024 * 1024
SEQ_CHUNK = 256
POOL_CHUNK = 512
WIN_HALO = 16
CONV_HALO = 8
SMALL_ROWS = 16


def _params(n_grid):
    return pltpu.CompilerParams(dimension_semantics=("arbitrary",) * n_grid, vmem_limit_bytes=VMEM_LIMIT)


def _shift(v, j):
    n = v.shape[0]
    s = (-j) % n
    return v if s == 0 else pltpu.roll(v, s, 0)


def _sigmoid(x):
    return 0.5 * jnp.tanh(0.5 * x) + 0.5


def _softplus(z):
    e = jnp.exp(-jnp.abs(z))
    u = 1.0 + e
    log1p = jnp.where(u == 1.0, e, jnp.log(u) * (e / jnp.where(u == 1.0, 1.0, u - 1.0)))
    return jnp.maximum(z, 0.0) + log1p


_GELU_C = 0.7978845608028654
_GELU_K = 0.044715


def _gelu_and_grad(x):
    x2 = x * x
    t = jnp.tanh(_GELU_C * (x + _GELU_K * x * x2))
    g = 0.5 * x * (1.0 + t)
    dg = 0.5 * (1.0 + t) + 0.5 * x * (1.0 - t * t) * (_GELU_C * (1.0 + 3.0 * _GELU_K * x2))
    return g, dg


def _ln_fwd(z, g, b):
    mu = jnp.mean(z, axis=-1, keepdims=True)
    zc = z - mu
    var = jnp.mean(zc * zc, axis=-1, keepdims=True)
    rstd = lax.rsqrt(var + LN_EPS)
    xhat = zc * rstd
    return xhat * g + b, xhat, rstd


def _ln_bwd(dy, xhat, rstd, g):
    dxhat = dy * g
    m1 = jnp.mean(dxhat, axis=-1, keepdims=True)
    m2 = jnp.mean(dxhat * xhat, axis=-1, keepdims=True)
    dz = rstd * (dxhat - m1 - xhat * m2)
    dg = jnp.sum(dy * xhat, axis=0, keepdims=True)
    db = jnp.sum(dy, axis=0, keepdims=True)
    return dz, dg, db


def _acc_rows(ref, first, val):
    @pl.when(first)
    def _():
        ref[...] = val

    @pl.when(jnp.logical_not(first))
    def _():
        ref[...] += val


def _sp(shape, fn, single=False):
    return shape, fn, single


def _matmul(name, a, b, a_spec, b_spec, *, grid, j_outer=False, ta=False, tb=False, extras=(), out_shape, out_specs,
            epilogue=None, n_split=1, side=None):
    ni, nj, nk = grid
    n_ex = len(extras)
    dims = (((0 if ta else 1,), (1 if tb else 0,)), ((), ()))
    side_in, side_shape, side_out, side_fn = side if side is not None else ((), (), (), None)
    n_main_out = len(out_shape)
    inner = ni if j_outer else nj

    def mk(spec):
        shape, fn, single = spec
        index = (lambda g0, g1, g2: fn(g1, g0, g2)) if j_outer else fn
        return pl.BlockSpec(shape, index, pipeline_mode=pl.Buffered(1)) if single else pl.BlockSpec(shape, index)

    def mk_side(block, fn):
        return pl.BlockSpec(block, lambda g0, g1, g2: fn(g0 * inner + g1))

    def body(a_ref, b_ref, *rest):
        ex_refs = rest[:n_ex]
        out_refs = rest[n_ex + len(side_in):n_ex + len(side_in) + n_main_out]
        if side_fn is not None:
            side_fn(pl.program_id(0) * inner + pl.program_id(1), rest[n_ex:n_ex + len(side_in)],
                    rest[n_ex + len(side_in) + n_main_out:])
        i = pl.program_id(1 if j_outer else 0)
        if n_split > 1:
            av = a_ref[...].astype(BF16)
            width = b_ref.shape[0 if tb else 1] // n_split
            for c in range(n_split):
                cols = pl.ds(c * width, width)
                bv = (b_ref[cols, :] if tb else b_ref[:, cols]).astype(BF16)
                epilogue(lax.dot_general(av, bv, dims, preferred_element_type=F32), i, ex_refs, out_refs, cols)
            return
        part = lax.dot_general(a_ref[...].astype(BF16), b_ref[...].astype(BF16), dims, preferred_element_type=F32)
        if nk == 1:
            epilogue(part, i, ex_refs, out_refs)
        else:
            @pl.when(pl.program_id(2) == 0)
            def _():
                out_refs[0][...] = part

            @pl.when(pl.program_id(2) > 0)
            def _():
                out_refs[0][...] += part

    return pl.pallas_call(
        body, name=name, grid=(nj, ni, nk) if j_outer else (ni, nj, nk),
        in_specs=[mk(a_spec), mk(b_spec)] + [mk(s) for _, s in extras] + [mk_side(blk, fn) for _, blk, fn in side_in],
        out_specs=[mk(s) for s in out_specs] + [mk_side(blk, fn) for blk, fn in side_out],
        out_shape=list(out_shape) + list(side_shape),
        compiler_params=_params(3),
    )(a, b, *[x for x, _ in extras], *[x for x, _, _ in side_in])


def _bs(shape, fn):
    return pl.BlockSpec(shape, fn)


def _where_am_i():
    x, y, c = lax.axis_index("x"), lax.axis_index("y"), lax.axis_index("c")
    return x, y, c


def _dev_index(p):
    return 4 * p[0] + 2 * p[1] + p[2]


def _slab(ref, axis, idx, size):
    sl = [slice(None)] * len(ref.shape)
    sl[axis] = pl.ds(idx * size, size)
    return ref.at[tuple(sl)]


HBM = pl.BlockSpec(memory_space=pltpu.HBM)
SEM = pl.BlockSpec(memory_space=pltpu.SEMAPHORE)
DATAFLOW = pltpu.SideEffectType.DATAFLOW_SIDE_EFFECTING


def _in_hbm(a):
    return pltpu.with_memory_space_constraint(a, pltpu.HBM)


def _token_shape():
    return jax.ShapeDtypeStruct((8, 128), F32)


def _split_start(name, n_sems, bufs, issue):
    nb = len(bufs)

    def body(*refs):
        issue(refs[:nb], refs[nb], refs[nb + 1])
        refs[-1][...] = jnp.zeros((8, 128), F32)

    outs = pl.pallas_call(
        body, name=name,
        out_shape=(pltpu.SemaphoreType.DMA((n_sems,)), pltpu.SemaphoreType.DMA((n_sems,)),
                   *[pltpu.HBM(b.shape, b.dtype) for b in bufs], _token_shape()),
        in_specs=[HBM] * nb, out_specs=(SEM, SEM, *[HBM] * nb, pl.BlockSpec(memory_space=pltpu.VMEM)),
        input_output_aliases={i: 2 + i for i in range(nb)},
        compiler_params=pltpu.CompilerParams(has_side_effects=DATAFLOW),
    )(*[_in_hbm(b) for b in bufs])
    return outs[0], outs[1], list(outs[2:2 + nb]), outs[-1]


def _split_relay(name, n_sems, sems, bufs, after, relay):
    nb = len(bufs)

    def body(*refs):
        relay(refs[:nb], refs[nb], refs[nb + 1], refs[nb + 3], refs[nb + 4])
        refs[-1][...] = jnp.zeros((8, 128), F32)

    outs = pl.pallas_call(
        body, name=name,
        out_shape=(pltpu.SemaphoreType.DMA((n_sems,)), pltpu.SemaphoreType.DMA((n_sems,)),
                   *[pltpu.HBM(b.shape, b.dtype) for b in bufs], _token_shape()),
        in_specs=[HBM] * nb + [SEM, SEM, ANY],
        out_specs=(SEM, SEM, *[HBM] * nb, pl.BlockSpec(memory_space=pltpu.VMEM)),
        input_output_aliases={i: 2 + i for i in range(nb)},
        compiler_params=pltpu.CompilerParams(has_side_effects=DATAFLOW),
    )(*bufs, sems[0], sems[1], after)
    return outs[0], outs[1], list(outs[2:2 + nb]), outs[-1]


def _split_wait(name, sems, bufs, after, finish):
    nb = len(bufs)

    def body(*refs):
        finish(refs[:nb], refs[nb], refs[nb + 1])

    outs = pl.pallas_call(
        body, name=name, out_shape=[pltpu.HBM(b.shape, b.dtype) for b in bufs],
        in_specs=[HBM] * nb + [SEM, SEM, ANY], out_specs=[HBM] * nb,
        input_output_aliases={i: i for i in range(nb)},
        compiler_params=pltpu.CompilerParams(has_side_effects=DATAFLOW),
    )(*bufs, sems[0], sems[1], after)
    return list(outs)


def _place(name, items, dtype, after):
    ids = jnp.reshape(_dev_index(_where_am_i()), (1,)).astype(jnp.int32)
    outs = []
    for a, (shard, axis) in enumerate(items):
        rows, cols = shard.shape[-2], shard.shape[-1]
        tr = rows
        while tr * cols * shard.dtype.itemsize > 4 * 1024 * 1024 and tr % 32 == 0:
            tr //= 2
        nt = rows // tr
        full = list(shard.shape)
        full[axis] *= N_DEV
        if shard.ndim == 2 and axis == 0:
            in_spec = _bs((tr, cols), lambda i, ids: (i, 0))
            out_spec = _bs((tr, cols), lambda i, ids, nt=nt: (ids[0] * nt + i, 0))
        elif shard.ndim == 2 and axis == 1:
            in_spec = _bs((tr, cols), lambda i, ids: (i, 0))
            out_spec = _bs((tr, cols), lambda i, ids: (i, ids[0]))
        elif shard.ndim == 3 and axis == 1:
            tr, nt = rows, shard.shape[0]
            in_spec = _bs((None, rows, cols), lambda i, ids: (i, 0, 0))
            out_spec = _bs((None, rows, cols), lambda i, ids: (i, ids[0], 0))
        else:
            assert shard.ndim == 3 and axis == 0 and shard.shape[0] == 1
            in_spec = _bs((None, tr, cols), lambda i, ids: (0, i, 0))
            out_spec = _bs((None, tr, cols), lambda i, ids: (ids[0], i, 0))

        def body(ids_ref, in_ref, after_ref, out_ref):
            del ids_ref, after_ref
            out_ref[...] = in_ref[...].astype(out_ref.dtype)

        outs.append(pl.pallas_call(
            body, name=f"{name}{a}",
            grid_spec=pltpu.PrefetchScalarGridSpec(
                num_scalar_prefetch=1, grid=(nt,), in_specs=[in_spec, ANY], out_specs=out_spec),
            out_shape=jax.ShapeDtypeStruct(tuple(full), dtype), compiler_params=_params(1),
        )(ids, shard, after))
    return outs


class _SplitGather:
    def __init__(self, name, items, dtype, after):
        self.name, self.items, self.n = name, items, len(items)
        fulls = _place(name + "_place", items, dtype, after)
        n = self.n

        def issue(refs, send, recv):
            me, sibling, chips, c = self._geometry()
            for a in range(n):
                self._copy1(refs, send, recv, a, 0, me, sibling).start()
                for j, chip in enumerate(chips):
                    self._copy1(refs, send, recv, a, 1 + j, me, (*chip, c)).start()

        self.send, self.recv, self.bufs, self.token = _split_start(name + "_start", 4 * n, fulls, issue)

    @staticmethod
    def _geometry():
        x, y, c = _where_am_i()
        return (x, y, c), (x, y, 1 - c), [(1 - x, y), (x, 1 - y), (1 - x, 1 - y)], c

    def _blk(self, refs, a, p):
        shard, axis = self.items[a]
        return _slab(refs[a], axis, _dev_index(p), shard.shape[axis])

    def _copy1(self, refs, send, recv, a, k, owner, to):
        return pltpu.make_async_remote_copy(
            src_ref=self._blk(refs, a, owner), dst_ref=self._blk(refs, a, owner), send_sem=send.at[4 * a + k],
            recv_sem=recv.at[4 * a + k], device_id=to, device_id_type=MESH)

    def _copy2(self, refs, send, recv, a, j, owner, to):
        return pltpu.make_async_remote_copy(
            src_ref=self._blk(refs, a, owner), dst_ref=self._blk(refs, a, owner), send_sem=send.at[3 * a + j],
            recv_sem=recv.at[3 * a + j], device_id=to, device_id_type=MESH)

    def relay(self, after):
        n = self.n

        def relay(refs, send_in, recv_in, send_out, recv_out):
            me, sibling, chips, c = self._geometry()
            for a in range(n):
                for j, chip in enumerate(chips):
                    self._copy1(refs, send_in, recv_in, a, 1 + j, (*chip, c), me).wait_recv()
                    self._copy2(refs, send_out, recv_out, a, j, (*chip, c), sibling).start()
            for a in range(n):
                self._copy1(refs, send_in, recv_in, a, 0, sibling, me).wait_recv()
                for k in range(4):
                    self._copy1(refs, send_in, recv_in, a, k, me, sibling).wait_send()

        self.send, self.recv, self.bufs, self.token = _split_relay(
            self.name + "_relay", 3 * n, (self.send, self.recv), self.bufs, after, relay)
        return self.token

    def wait(self, after):
        n = self.n

        def finish(refs, send, recv):
            me, sibling, chips, c = self._geometry()
            for a in range(n):
                for j, chip in enumerate(chips):
                    self._copy2(refs, send, recv, a, j, (*chip, 1 - c), me).wait_recv()
                    self._copy2(refs, send, recv, a, j, (*chip, c), sibling).wait_send()

        return _split_wait(self.name + "_wait", (self.send, self.recv), self.bufs, after, finish)


class _SplitReduceScatter:
    def __init__(self, name, grads):
        self.name, self.n = name, len(grads)
        n = self.n
        g4 = [g.reshape(4, 2, *g.shape[1:]) for g in grads]
        land = [lax.empty((4, 1, *g.shape[1:]), g.dtype) for g in grads]

        def issue(refs, send, recv):
            for a in range(n):
                self._swap(refs, send, recv, a).start()

        self.send, self.recv, self.bufs, self.token = _split_start(name + "_d2d_start", n, g4 + land, issue)

    def _swap(self, refs, send, recv, a):
        x, y, c = _where_am_i()
        return pltpu.make_async_remote_copy(
            src_ref=refs[a].at[:, pl.ds(1 - c, 1)], dst_ref=refs[self.n + a], send_sem=send.at[a], recv_sem=recv.at[a],
            device_id=(x, y, 1 - c), device_id_type=MESH)

    def _hop(self, refs, send, recv, a, m):
        x, y, c = _where_am_i()
        px = (1 - x) if m & 2 else x
        py = (1 - y) if m & 1 else y
        return pltpu.make_async_remote_copy(
            src_ref=refs[a].at[2 * px + py], dst_ref=refs[self.n + a].at[m - 1], send_sem=send.at[3 * a + m - 1],
            recv_sem=recv.at[3 * a + m - 1], device_id=(px, py, c), device_id_type=MESH)

    def combine_and_send(self, after):
        n = self.n

        def finish(refs, send, recv):
            for a in range(n):
                self._swap(refs, send, recv, a).wait()

        bufs = _split_wait(self.name + "_d2d_wait", (self.send, self.recv), self.bufs, after, finish)
        x, y, c = _where_am_i()
        ids = jnp.stack([c, 2 * x + y]).astype(jnp.int32)
        self.own, sums = [], []
        for a in range(n):
            own, hb = _pair_sum(f"{self.name}_sum{a}", bufs[a], bufs[n + a], ids)
            self.own.append(own)
            sums.append(hb)
        land = [lax.empty((3, *h.shape[1:]), h.dtype) for h in sums]

        def issue(refs, send, recv):
            for a in range(n):
                for m in (1, 2, 3):
                    self._hop(refs, send, recv, a, m).start()

        self.send, self.recv, self.bufs, self.token = _split_start(self.name + "_ici_start", 3 * n, sums + land, issue)
        return self.token

    def wait(self, after):
        n = self.n

        def finish(refs, send, recv):
            for a in range(n):
                for m in (1, 2, 3):
                    self._hop(refs, send, recv, a, m).wait()

        bufs = _split_wait(self.name + "_ici_wait", (self.send, self.recv), self.bufs, after, finish)
        return list(zip(self.own, bufs[n:]))


def _pair_sum(name, g4, land, ids):
    rows, cols = g4.shape[2], g4.shape[3]
    tr = rows
    while tr * cols * 2 > 2 * 1024 * 1024 and tr % 32 == 0:
        tr //= 2

    def body(ids_ref, g_ref, l_ref, own_ref, sum_ref):
        h = g_ref[...].astype(F32) + l_ref[...].astype(F32)
        sum_ref[...] = h.astype(sum_ref.dtype)

        @pl.when(pl.program_id(1) == ids_ref[1])
        def _():
            own_ref[...] = h

    return pl.pallas_call(
        body, name=name,
        grid_spec=pltpu.PrefetchScalarGridSpec(
            num_scalar_prefetch=1, grid=(rows // tr, 4),
            in_specs=[_bs((None, None, tr, cols), lambda i, q, ids: (q, ids[0], i, 0)),
                      _bs((None, None, tr, cols), lambda i, q, ids: (q, 0, i, 0))],
            out_specs=[_bs((tr, cols), lambda i, q, ids: (i, 0)), _bs((None, tr, cols), lambda i, q, ids: (q, i, 0))]),
        out_shape=[jax.ShapeDtypeStruct((rows, cols), F32), jax.ShapeDtypeStruct((4, rows, cols), g4.dtype)],
        compiler_params=_params(2),
    )(ids, g4, land)


def _win_sum(ext, w, off):
    s = ext + _shift(ext, -1)
    if w >= 4:
        s = _shift(s, -1) + _shift(s, 1)
    if w >= 8:
        s = _shift(s, -2) + _shift(s, 2)
    if w >= 16:
        s = _shift(s, -4) + _shift(s, 4)
    return _shift(s, off) if off else s


def _inv_count(r0, t, w, seq):
    pos = r0 + lax.broadcasted_iota(jnp.int32, (t, 1), 0)
    cnt = jnp.minimum(pos + w // 2, seq) - jnp.maximum(pos - w // 2, 0)
    return 1.0 / cnt.astype(F32)


def _pool_fwd(p3, w_pool, pool_scale, seq, d_model):
    dp = d_model // 2
    pg = dp // len(POOL_WINDOWS)
    t = min(POOL_CHUNK, seq)
    n_chunks = seq // t
    h = WIN_HALO

    def body(u_ref, w_ref, sc_ref, d_ref, y_ref, pad_ref):
        g = pl.program_id(0)
        zeros = jnp.zeros((h, pg), F32)
        pad_ref[0:h, :] = zeros
        pad_ref[h + seq:h + seq + h, :] = zeros

        def fill(ci, _):
            r0 = pl.multiple_of(ci * t, t)
            pad_ref[pl.ds(h + r0, t), :] = u_ref[pl.ds(r0, t), :]
            return 0

        lax.fori_loop(0, n_chunks, fill, 0)
        wmat = w_ref[...]
        scale = sc_ref[...]
        for gi, w in enumerate(POOL_WINDOWS):
            @pl.when(g == gi)
            def _(w=w):
                def chunk(ci, _):
                    r0 = pl.multiple_of(ci * t, t)
                    ext = pad_ref[pl.ds(r0, t + 2 * h), :]
                    mean = _win_sum(ext, w, 0)[h:h + t, :] * _inv_count(r0, t, w, seq)
                    d = (mean - ext[h:h + t, :]).astype(BF16)
                    d_ref[pl.ds(r0, t), :] = d
                    q = jnp.dot(d, wmat, preferred_element_type=F32)
                    y_ref[pl.ds(r0, t), :] = (q * scale).astype(BF16)
                    return 0

                lax.fori_loop(0, n_chunks, chunk, 0, unroll=2)

    return pl.pallas_call(
        body, name="pool_fwd", grid=(len(POOL_WINDOWS),),
        in_specs=[_bs((None, seq, pg), lambda g: (0, 0, g)), _bs((None, pg, pg), lambda g: (g, 0, 0)),
                  _bs((1, pg), lambda g: (0, g))],
        out_specs=[_bs((seq, pg), lambda g: (0, g)), _bs((seq, pg), lambda g: (0, g))],
        out_shape=[jax.ShapeDtypeStruct((seq, dp), BF16), jax.ShapeDtypeStruct((seq, d_model), BF16)],
        scratch_shapes=[pltpu.VMEM((seq + 2 * h, pg), F32)],
        compiler_params=_params(1),
    )(p3, w_pool, pool_scale)


def _pool_bwd(d, dy, w_pool, pool_scale, token, seq, d_model):
    dp = d_model // 2
    pg = dp // len(POOL_WINDOWS)
    t = min(POOL_CHUNK, seq)
    n_chunks = seq // t
    h = WIN_HALO
    tn_dims = (((0,), (0,)), ((), ()))
    nt_dims = (((1,), (1,)), ((), ()))

    def body(d_ref, dy_ref, w_ref, sc_ref, tok_ref, du_ref, dwb_ref, dsc_ref, pad_ref, dd_ref, dw_ref):
        del tok_ref
        g = pl.program_id(0)
        zeros = jnp.zeros((h, pg), F32)
        pad_ref[0:h, :] = zeros
        pad_ref[h + seq:h + seq + h, :] = zeros
        wmat = w_ref[...]
        scale = sc_ref[...]
        for gi, w in enumerate(POOL_WINDOWS):
            @pl.when(g == gi)
            def _(w=w):
                dw_ref[...] = jnp.zeros((pg, pg), F32)

                def first(ci, dsc):
                    r0 = pl.multiple_of(ci * t, t)
                    dv = d_ref[pl.ds(r0, t), :]
                    dyv = dy_ref[pl.ds(r0, t), :]
                    q = jnp.dot(dv, wmat, preferred_element_type=F32)
                    dsc = dsc + jnp.sum(dyv * q, axis=0, keepdims=True)
                    dq = (dyv * scale).astype(BF16)
                    dw_ref[...] += lax.dot_general(dv, dq, tn_dims, preferred_element_type=F32)
                    dd = lax.dot_general(dq, wmat, nt_dims, preferred_element_type=F32)
                    dd_ref[pl.ds(r0, t), :] = dd
                    pad_ref[pl.ds(h + r0, t), :] = dd * _inv_count(r0, t, w, seq)
                    return dsc

                def first_pair(cj, dsc):
                    return first(2 * cj + 1, first(2 * cj, dsc))

                dsc_ref[...] = lax.fori_loop(0, n_chunks // 2, first_pair, jnp.zeros((1, pg), F32))
                dwb_ref[...] = dw_ref[...].reshape(N_DEV, pg // N_DEV, pg).astype(BF16)

                def second(ci, _):
                    r0 = pl.multiple_of(ci * t, t)
                    ext = pad_ref[pl.ds(r0, t + 2 * h), :]
                    back = _win_sum(ext, w, 1)[h:h + t, :]
                    du_ref[pl.ds(r0, t), :] = (back - dd_ref[pl.ds(r0, t), :]).astype(BF16)
                    return 0

                lax.fori_loop(0, n_chunks, second, 0, unroll=2)

    return pl.pallas_call(
        body, name="pool_bwd", grid=(len(POOL_WINDOWS),),
        in_specs=[_bs((seq, pg), lambda g: (0, g)), _bs((seq, pg), lambda g: (0, g)),
                  _bs((None, pg, pg), lambda g: (g, 0, 0)), _bs((1, pg), lambda g: (0, g)),
                  _bs((8, 128), lambda g: (0, 0))],
        out_specs=[_bs((seq, pg), lambda g: (0, g)), _bs((N_DEV, None, pg // N_DEV, pg), lambda g: (0, g, 0, 0)),
                   _bs((1, pg), lambda g: (0, g))],
        out_shape=[jax.ShapeDtypeStruct((seq, 3 * dp), BF16),
                   jax.ShapeDtypeStruct((N_DEV, len(POOL_WINDOWS), pg // N_DEV, pg), BF16),
                   jax.ShapeDtypeStruct((1, dp), F32)],
        scratch_shapes=[pltpu.VMEM((seq + 2 * h, pg), F32), pltpu.VMEM((seq, pg), F32), pltpu.VMEM((pg, pg), F32)],
        compiler_params=_params(1),
    )(d, dy, w_pool, pool_scale, token)


def _tile_scan(n_tiles, lanes, loads, stores):
    row = lax.broadcasted_iota(jnp.int32, (8, lanes), 0)
    group = 8

    def local_scan(n, k):
        aa, bb = loads[n](k)
        for sh in (1, 2, 4):
            if n == 0:
                ok = row >= sh
                ap = jnp.where(ok, pltpu.roll(aa, sh, 0), 1.0)
                bp = jnp.where(ok, pltpu.roll(bb, sh, 0), 0.0)
            else:
                ok = row < 8 - sh
                ap = jnp.where(ok, pltpu.roll(aa, 8 - sh, 0), 1.0)
                bp = jnp.where(ok, pltpu.roll(bb, 8 - sh, 0), 0.0)
            bb = aa * bp + bb
            aa = aa * ap
        return aa, bb

    def step(s, carry):
        carry = list(carry)
        for n in range(2):
            tiles = [s * group + u if n == 0 else n_tiles - 1 - (s * group + u) for u in range(group)]
            local = [local_scan(n, k) for k in tiles]
            for k, (aa, bb) in zip(tiles, local):
                hh = bb + aa * carry[n]
                stores[n](k, hh)
                carry[n] = jnp.broadcast_to(hh[7:8, :] if n == 0 else hh[0:1, :], (8, lanes))
        return tuple(carry)

    zeros = jnp.zeros((8, lanes), F32)
    lax.fori_loop(0, n_tiles // group, step, (zeros, zeros))


def _gate_preacts(xc, wcat_ref):
    xcb = xc.astype(BF16)
    return xcb, jnp.dot(xcb, wcat_ref[...], preferred_element_type=F32)


def _gates(pre, n, pk_ref, sp):
    lh = pre.shape[1] // 4
    r = _sigmoid(pre[:, (2 * n) * lh:(2 * n + 1) * lh] + pk_ref[pl.ds(4 + n, 1), :])
    i = _sigmoid(pre[:, (2 * n + 1) * lh:(2 * n + 2) * lh] + pk_ref[pl.ds(6 + n, 1), :])
    log_a = (-RG_C * r) * sp[n]
    a = jnp.exp(log_a)
    x = 2.0 * log_a
    one_minus_a2 = jnp.where(x > -0.01, -(x * (1.0 + x * (0.5 + x * (1.0 / 6.0)))), 1.0 - a * a)
    m = jnp.sqrt(one_minus_a2)
    return r, i, a, m


def _conv_chunk(upad_ref, pk_ref, cb, r0, t):
    ext = upad_ref[pl.ds(r0, t + 2 * CONV_HALO), :]
    acc = pk_ref[pl.ds(1, 1), :] * ext
    for k in (0, 2, 3):
        acc = acc + pk_ref[pl.ds(k, 1), :] * _shift(ext, k - 1)
    return acc[CONV_HALO:CONV_HALO + t, :] + cb, ext


def _lru_fwd(p3, y_in, pack, conv_b, wcat, token, seq, d_model):
    dl = d_model // 2
    lh = dl // N_HEADS
    t = min(SEQ_CHUNK, seq)
    n_chunks = seq // t
    hal = CONV_HALO
    first_rec_block = (d_model - dl) // lh

    def body(ur_ref, ug_ref, pk_ref, cb_ref, wcat_ref, yin_ref, tok_ref, y_ref, h0_ref, h1_ref,
             upad, a_scr, b_scr):
        del yin_ref, tok_ref
        zeros = jnp.zeros((hal, lh), F32)
        upad[0:hal, :] = zeros
        upad[hal + seq:hal + seq + hal, :] = zeros
        for ref in (h0_ref, h1_ref):
            ref[0:hal, :] = zeros
            ref[hal + seq:hal + seq + hal, :] = zeros

        def fill(ci, _):
            r0 = pl.multiple_of(ci * t, t)
            upad[pl.ds(hal + r0, t), :] = ur_ref[pl.ds(r0, t), :]
            return 0

        lax.fori_loop(0, n_chunks, fill, 0)
        cb = cb_ref[...]
        sp = [_softplus(-pk_ref[pl.ds(8 + n, 1), :]) for n in range(2)]

        def chunk(ci, _):
            r0 = pl.multiple_of(ci * t, t)
            xc, _ext = _conv_chunk(upad, pk_ref, cb, r0, t)
            _, pre = _gate_preacts(xc, wcat_ref)
            for n in range(2):
                _, i, a, m = _gates(pre, n, pk_ref, sp)
                a_scr[n, pl.ds(r0, t), :] = a
                b_scr[n, pl.ds(r0, t), :] = (m * i) * xc
            return 0

        lax.fori_loop(0, n_chunks, chunk, 0, unroll=2)

        def load(n):
            def get(k):
                at = pl.ds(pl.multiple_of(k * 8, 8), 8)
                return a_scr[n, at, :], b_scr[n, at, :]
            return get

        def store(ref):
            def put(k, v):
                ref[pl.ds(pl.multiple_of(hal + k * 8, 8), 8), :] = v
            return put

        _tile_scan(seq // 8, lh, [load(0), load(1)], [store(h0_ref), store(h1_ref)])

        def out(ci, _):
            r0 = pl.multiple_of(ci * t, t)
            hsum = h0_ref[pl.ds(hal + r0, t), :] + h1_ref[pl.ds(hal + r0, t), :]
            gl, _dg = _gelu_and_grad(ug_ref[pl.ds(r0, t), :])
            y_ref[pl.ds(r0, t), :] = (hsum * gl).astype(BF16)
            return 0

        lax.fori_loop(0, n_chunks, out, 0)

    return pl.pallas_call(
        body, name="lru_fwd", grid=(N_HEADS,),
        in_specs=[_bs((None, seq, lh), lambda h: (1, 0, h)), _bs((None, seq, lh), lambda h: (2, 0, h)),
                  _bs((None, SMALL_ROWS, lh), lambda h: (h, 0, 0)), _bs((1, lh), lambda h: (0, h)),
                  _bs((None, lh, 4 * lh), lambda h: (h, 0, 0)),
                  ANY, _bs((8, 128), lambda h: (0, 0))],
        out_specs=[_bs((seq, lh), lambda h: (0, first_rec_block + h)),
                   _bs((seq + 2 * hal, lh), lambda h: (0, h)), _bs((seq + 2 * hal, lh), lambda h: (0, h))],
        out_shape=[jax.ShapeDtypeStruct((seq, d_model), BF16), jax.ShapeDtypeStruct((seq + 2 * hal, dl), F32),
                   jax.ShapeDtypeStruct((seq + 2 * hal, dl), F32)],
        scratch_shapes=[pltpu.VMEM((seq + 2 * hal, lh), F32), pltpu.VMEM((2, seq, lh), F32),
                        pltpu.VMEM((2, seq, lh), F32)],
        input_output_aliases={5: 0},
        compiler_params=_params(1),
    )(p3, p3, pack, conv_b, wcat, y_in, token)


def _lru_bwd(p3, dy, h0p, h1p, dproj_in, pack, conv_b, wcat, token, seq, d_model):
    dl = d_model // 2
    lh = dl // N_HEADS
    t = min(SEQ_CHUNK, seq)
    n_chunks = seq // t
    hal = CONV_HALO
    first_rec_block = (d_model - dl) // lh
    tn_dims = (((0,), (0,)), ((), ()))
    nt_dims = (((1,), (1,)), ((), ()))

    def body(ur_ref, ug_ref, dy_ref, h0_ref, h1_ref, pk_ref, cb_ref, wcat_ref, tok_ref, din_ref,
             dproj_ref, dpk_ref, dcb_ref, dwcat_ref,
             upad, a_scr, dh_scr, g_scr, dxc_pad, dpr_ref, out_sems, gate_scr):
        del din_ref, tok_ref
        zeros = jnp.zeros((hal, lh), F32)
        for ref in (upad, dxc_pad):
            ref[0:hal, :] = zeros
            ref[hal + seq:hal + seq + hal, :] = zeros
        for n in range(2):
            a_scr[n, 0:hal, :] = zeros
            a_scr[n, hal + seq:hal + seq + hal, :] = zeros

        def fill(ci, _):
            r0 = pl.multiple_of(ci * t, t)
            upad[pl.ds(hal + r0, t), :] = ur_ref[pl.ds(r0, t), :]
            return 0

        lax.fori_loop(0, n_chunks, fill, 0)
        cb = cb_ref[...]
        lam = [pk_ref[pl.ds(8 + n, 1), :] for n in range(2)]
        sp = [_softplus(-lam[n]) for n in range(2)]

        def chunk1(ci, _):
            r0 = pl.multiple_of(ci * t, t)
            xc, _ext = _conv_chunk(upad, pk_ref, cb, r0, t)
            _, pre = _gate_preacts(xc, wcat_ref)
            for n in range(2):
                r, i, a, m = _gates(pre, n, pk_ref, sp)
                a_scr[n, pl.ds(hal + r0, t), :] = a
                for q, v in enumerate((r, i, m)):
                    gate_scr[3 * n + q, pl.ds(r0, t), :] = v
            hsum = h0_ref[pl.ds(hal + r0, t), :] + h1_ref[pl.ds(hal + r0, t), :]
            gl, dgl = _gelu_and_grad(ug_ref[pl.ds(r0, t), :])
            dyv = dy_ref[pl.ds(r0, t), :]
            dh_scr[pl.ds(r0, t), :] = dyv * gl
            dpr_ref[1, pl.ds(r0, t), :] = ((dyv * hsum) * dgl).astype(BF16)
            return 0

        lax.fori_loop(0, n_chunks, chunk1, 0, unroll=2)

        def load(n):
            def get(k):
                r0 = pl.multiple_of(k * 8, 8)
                if n == 0:
                    coef = _shift(a_scr[0, pl.ds(pl.multiple_of(hal + r0, 8), 16), :], 1)[0:8, :]
                else:
                    coef = _shift(a_scr[1, pl.ds(pl.multiple_of(hal + r0 - 8, 8), 16), :], -1)[8:16, :]
                return coef, dh_scr[pl.ds(r0, 8), :]
            return get

        def store(n):
            def put(k, v):
                g_scr[n, pl.ds(pl.multiple_of(k * 8, 8), 8), :] = v
            return put

        _tile_scan(seq // 8, lh, [load(1), load(0)], [store(1), store(0)])

        dwcat_ref[...] = jnp.zeros((lh, 4 * lh), F32)

        def chunk3(ci, carry):
            dba, dbi, dlam, dcb = carry
            r0 = pl.multiple_of(ci * t, t)
            xc, _ext = _conv_chunk(upad, pk_ref, cb, r0, t)
            xcb = xc.astype(BF16)
            dxc = jnp.zeros((t, lh), F32)
            dba, dbi, dlam = list(dba), list(dbi), list(dlam)
            dpre = []
            for n in range(2):
                r, i, m = (gate_scr[3 * n + q, pl.ds(r0, t), :] for q in range(3))
                a = a_scr[n, pl.ds(hal + r0, t), :]
                hext = (h0_ref if n == 0 else h1_ref)[pl.ds(r0, t + 2 * hal), :]
                hprev = _shift(hext, -1 if n == 0 else 1)[hal:hal + t, :]
                gb = g_scr[n, pl.ds(r0, t), :]
                da = gb * hprev
                dm = gb * i * xc
                di = gb * m * xc
                dxc = dxc + gb * (m * i)
                dlog_a = da * a - dm * (a * a) / m
                dr = dlog_a * (-RG_C * sp[n])
                dlam[n] = dlam[n] + jnp.sum(dlog_a * r, axis=0, keepdims=True)
                dpr = dr * r * (1.0 - r)
                dpi = di * i * (1.0 - i)
                dba[n] = dba[n] + jnp.sum(dpr, axis=0, keepdims=True)
                dbi[n] = dbi[n] + jnp.sum(dpi, axis=0, keepdims=True)
                dpre += [dpr.astype(BF16), dpi.astype(BF16)]
            dpre = jnp.concatenate(dpre, axis=1)
            dwcat_ref[...] += lax.dot_general(xcb, dpre, tn_dims, preferred_element_type=F32)
            dxc = dxc + lax.dot_general(dpre, wcat_ref[...], nt_dims, preferred_element_type=F32)
            dxc_pad[pl.ds(hal + r0, t), :] = dxc
            dcb = dcb + jnp.sum(dxc, axis=0, keepdims=True)
            return tuple(dba), tuple(dbi), tuple(dlam), dcb

        zr = jnp.zeros((1, lh), F32)
        def chunk3_pair(cj, carry):
            return chunk3(2 * cj + 1, chunk3(2 * cj, carry))

        dba, dbi, dlam, dcb = lax.fori_loop(0, n_chunks // 2, chunk3_pair, ((zr, zr), (zr, zr), (zr, zr), zr))
        dcb_ref[...] = dcb
        for n in range(2):
            dpk_ref[pl.ds(4 + n, 1), :] = dba[n]
            dpk_ref[pl.ds(6 + n, 1), :] = dbi[n]
            dpk_ref[pl.ds(8 + n, 1), :] = dlam[n] * (RG_C * jax.nn.sigmoid(-lam[n]))
        dpk_ref[pl.ds(10, SMALL_ROWS - 10), :] = jnp.zeros((SMALL_ROWS - 10, lh), F32)

        def chunk4(ci, dtap):
            r0 = pl.multiple_of(ci * t, t)
            gext = dxc_pad[pl.ds(r0, t + 2 * hal), :]
            uext = upad[pl.ds(r0, t + 2 * hal), :]
            gmid = gext[hal:hal + t, :]
            du = pk_ref[pl.ds(1, 1), :] * gext
            for k in (0, 2, 3):
                du = du + pk_ref[pl.ds(k, 1), :] * _shift(gext, 1 - k)
            dpr_ref[0, pl.ds(r0, t), :] = du[hal:hal + t, :].astype(BF16)
            out = []
            for k in range(4):
                usl = _shift(uext, k - 1)[hal:hal + t, :]
                out.append(dtap[k] + jnp.sum(gmid * usl, axis=0, keepdims=True))
            return tuple(out)

        dtap = lax.fori_loop(0, n_chunks, chunk4, (zr, zr, zr, zr))
        for k in range(4):
            dpk_ref[pl.ds(k, 1), :] = dtap[k]

        head = pl.program_id(0)
        outs = [pltpu.make_async_copy(
            dpr_ref.at[b], dproj_ref.at[:, pl.ds(pl.multiple_of((1 + b) * dl + head * lh, lh), lh)], out_sems.at[b])
            for b in range(2)]
        for cp in outs:
            cp.start()
        for cp in outs:
            cp.wait()

    return pl.pallas_call(
        body, name="lru_bwd", grid=(N_HEADS,),
        in_specs=[_bs((None, seq, lh), lambda h: (1, 0, h)), _bs((None, seq, lh), lambda h: (2, 0, h)),
                  _bs((seq, lh), lambda h: (0, first_rec_block + h)),
                  _bs((seq + 2 * hal, lh), lambda h: (0, h)), _bs((seq + 2 * hal, lh), lambda h: (0, h)),
                  _bs((None, SMALL_ROWS, lh), lambda h: (h, 0, 0)), _bs((1, lh), lambda h: (0, h)),
                  _bs((None, lh, 4 * lh), lambda h: (h, 0, 0)),
                  _bs((8, 128), lambda h: (0, 0)), ANY],
        out_specs=[ANY, _bs((None, SMALL_ROWS, lh), lambda h: (h, 0, 0)),
                   _bs((1, lh), lambda h: (0, h)), _bs((None, lh, 4 * lh), lambda h: (h, 0, 0))],
        out_shape=[jax.ShapeDtypeStruct((seq, 3 * dl), BF16), jax.ShapeDtypeStruct((N_HEADS, SMALL_ROWS, lh), F32),
                   jax.ShapeDtypeStruct((1, dl), F32), jax.ShapeDtypeStruct((N_HEADS, lh, 4 * lh), F32)],
        scratch_shapes=[pltpu.VMEM((seq + 2 * hal, lh), F32), pltpu.VMEM((2, seq + 2 * hal, lh), F32),
                        pltpu.VMEM((seq, lh), F32), pltpu.VMEM((2, seq, lh), F32),
                        pltpu.VMEM((seq + 2 * hal, lh), F32), pltpu.VMEM((2, seq, lh), BF16),
                        pltpu.SemaphoreType.DMA((2,)), pltpu.VMEM((6, seq, lh), F32)],
        input_output_aliases={9: 0},
        compiler_params=_params(1),
    )(p3, p3, dy, h0p, h1p, pack, conv_b, wcat, token, dproj_in)


class _tiles:
    def __init__(self, seq, d_model, d_ff):
        self.rows = min(1024, seq)
        self.ln_rows = min(256, seq)
        self.ff_cols = min(2048, d_ff)
        self.ff_split = 4
        self.ff_k = min(2048, d_ff)
        self.grad_rows = 512


def _ln_loss_bwd(ffn, x1, tgt, g, b, tr):
    seq, d = ffn.shape

    def body(f_ref, x_ref, t_ref, g_ref, b_ref, dz_ref, dzb_ref, dg_ref, db_ref, loss_ref):
        i = pl.program_id(0)
        gv = g_ref[...]
        z = ALPHA * x_ref[...] + f_ref[...]
        y, xhat, rstd = _ln_fwd(z, gv, b_ref[...])
        err = y - t_ref[...]
        part = 0.5 * jnp.sum(jnp.mean(err * err, axis=-1, keepdims=True), axis=0, keepdims=True)
        dz, dg, db = _ln_bwd(err * (1.0 / d), xhat, rstd, gv)
        dz_ref[...] = dz
        dzb_ref[...] = dz.astype(BF16)
        _acc_rows(dg_ref, i == 0, dg)
        _acc_rows(db_ref, i == 0, db)
        _acc_rows(loss_ref, i == 0, jnp.broadcast_to(part, (8, 128)))

    row = _bs((tr, d), lambda i: (i, 0))
    vec = _bs((1, d), lambda i: (0, 0))
    return pl.pallas_call(
        body, name="ln_ffn_loss", grid=(seq // tr,), in_specs=[row, row, row, vec, vec],
        out_specs=[row, row, vec, vec, _bs((8, 128), lambda i: (0, 0))],
        out_shape=[jax.ShapeDtypeStruct((seq, d), F32), jax.ShapeDtypeStruct((seq, d), BF16),
                   jax.ShapeDtypeStruct((1, d), F32), jax.ShapeDtypeStruct((1, d), F32),
                   jax.ShapeDtypeStruct((8, 128), F32)],
        compiler_params=_params(1),
    )(ffn, x1, tgt, g, b)


def _ln_bwd_side(dx_branch, dres, z, g, b, n_steps):
    seq, d = z.shape
    tr = seq // n_steps

    def fn(step, ins, outs):
        a_ref, r_ref, z_ref, g_ref, b_ref = ins
        dz_ref, dzb_ref, dg_ref, db_ref = outs
        gv = g_ref[...]
        _, xhat, rstd = _ln_fwd(z_ref[...], gv, b_ref[...])
        dz, dg, db = _ln_bwd(ALPHA * r_ref[...] + a_ref[...], xhat, rstd, gv)
        dz_ref[...] = dz
        dzb_ref[...] = dz.astype(BF16)
        _acc_rows(dg_ref, step == 0, dg)
        _acc_rows(db_ref, step == 0, db)

    row = ((tr, d), lambda s: (s, 0))
    vec = ((1, d), lambda s: (0, 0))
    shapes = [jax.ShapeDtypeStruct((seq, d), F32), jax.ShapeDtypeStruct((seq, d), BF16),
              jax.ShapeDtypeStruct((1, d), F32), jax.ShapeDtypeStruct((1, d), F32)]
    return [(dx_branch, *row), (dres, *row), (z, *row), (g, *vec), (b, *vec)], shapes, [row, row, vec, vec], fn


def _to_bf16(name, a, token):
    rows, cols = a.shape
    tr = min(512, rows)

    def body(a_ref, tok_ref, o_ref):
        del tok_ref
        o_ref[...] = a_ref[...].astype(BF16)

    return pl.pallas_call(
        body, name=name, grid=(rows // tr,),
        in_specs=[_bs((tr, cols), lambda i: (i, 0)), _bs((8, 128), lambda i: (0, 0))],
        out_specs=_bs((tr, cols), lambda i: (i, 0)), out_shape=jax.ShapeDtypeStruct((rows, cols), BF16),
        compiler_params=_params(1),
    )(a, token)


def _sum_blocks(name, parts):
    def body(p_ref, o_ref):
        acc = p_ref[0]
        for s in range(1, parts.shape[0]):
            acc = acc + p_ref[s]
        o_ref[...] = acc

    return pl.pallas_call(body, name=name, out_shape=jax.ShapeDtypeStruct(parts.shape[1:], F32))(parts)


def _adamw_values(w, g, m, v):
    m = ADAM_B1 * m + (1.0 - ADAM_B1) * g
    v = ADAM_B2 * v + (1.0 - ADAM_B2) * (g * g)
    m_hat = m / (1.0 - ADAM_B1 ** ADAM_STEP)
    v_hat = v / (1.0 - ADAM_B2 ** ADAM_STEP)
    delta = -ADAM_LR * (m_hat / (jnp.sqrt(v_hat) + ADAM_EPS) + ADAM_WD * w)
    return delta, m, v


def _adamw_side(own, parts, w, m, v, n_steps):
    rows, cols = w.shape
    tr = rows // n_steps

    def fn(step, ins, outs):
        o_ref, p_ref, w_ref, m_ref, v_ref = ins
        g = o_ref[...]
        for s in range(parts.shape[0]):
            g = g + p_ref[s].astype(F32)
        delta, mn, vn = _adamw_values(w_ref[...], g, m_ref[...], v_ref[...])
        for ref, val in zip(outs, (g, delta, mn, vn)):
            ref[...] = val

    row = ((tr, cols), lambda s: (s, 0))
    stack = ((parts.shape[0], tr, cols), lambda s: (0, s, 0))
    shapes = [jax.ShapeDtypeStruct((rows, cols), F32)] * 4
    return [(own, *row), (parts, *stack), (w, *row), (m, *row), (v, *row)], shapes, [row] * 4, fn


def _sum_adamw(name, own, parts, w, m, v):
    rows, cols = w.shape
    n_parts = parts.shape[0]
    tr = rows
    min_rows = 8 if parts.dtype == F32 else 16
    while tr * cols * 4 > 2 * 1024 * 1024 and tr % (2 * min_rows) == 0:
        tr //= 2

    def body(*refs):
        if own is None:
            p_ref, w_ref, m_ref, v_ref, g_ref, d_ref, mo_ref, vo_ref = refs
            g = p_ref[0].astype(F32)
            rest = range(1, n_parts)
        else:
            o_ref, p_ref, w_ref, m_ref, v_ref, g_ref, d_ref, mo_ref, vo_ref = refs
            g = o_ref[...]
            rest = range(n_parts)
        for s in rest:
            g = g + p_ref[s].astype(F32)
        delta, mn, vn = _adamw_values(w_ref[...], g, m_ref[...], v_ref[...])
        g_ref[...] = g
        d_ref[...] = delta
        mo_ref[...] = mn
        vo_ref[...] = vn

    spec = _bs((tr, cols), lambda i: (i, 0))
    lead = [] if own is None else [own]
    return pl.pallas_call(
        body, name=name, grid=(rows // tr,),
        in_specs=[spec] * len(lead) + [_bs((n_parts, tr, cols), lambda i: (0, i, 0)), spec, spec, spec],
        out_specs=[spec] * 4, out_shape=[jax.ShapeDtypeStruct((rows, cols), F32)] * 4,
        compiler_params=_params(1),
    )(*lead, parts, w, m, v)


def _rows128(a):
    return a.reshape(-1, 128)


def kernel(x, ln_mix_g, ln_mix_b, w_in, w_pool, pool_scale, conv_w, conv_b, w_rg_a, b_rg_a, w_rg_i, b_rg_i, rg_lambda, w_out, ln_ffn_g, ln_ffn_b, w_mlp_in, w_mlp_out, loss_target, m_ln_mix_g, m_ln_mix_b, m_w_in, m_w_pool, m_pool_scale, m_conv_w, m_conv_b, m_w_rg_a, m_b_rg_a, m_w_rg_i, m_b_rg_i, m_rg_lambda, m_w_out, m_ln_ffn_g, m_ln_ffn_b, m_w_mlp_in, m_w_mlp_out, v_ln_mix_g, v_ln_mix_b, v_w_in, v_w_pool, v_pool_scale, v_conv_w, v_conv_b, v_w_rg_a, v_b_rg_a, v_w_rg_i, v_b_rg_i, v_rg_lambda, v_w_out, v_ln_ffn_g, v_ln_ffn_b, v_w_mlp_in, v_w_mlp_out):
    seq, d_model = x.shape[1], x.shape[2]
    dh = d_model // 2
    lh = dh // N_HEADS
    pg = dh // len(POOL_WINDOWS)
    d_ff = w_mlp_in.shape[2] * N_DEV
    assert lh == 128 and conv_w.shape[3] == lh and w_pool.shape[2] * N_DEV == pg

    xs = x[0]
    tgt = loss_target[0]

    def small_pack(cw, ba, bi, lam):
        return jnp.concatenate([cw.reshape(4, lh), ba.reshape(2, lh), bi.reshape(2, lh), lam.reshape(2, lh),
                                jnp.zeros((SMALL_ROWS - 10, lh), F32)], axis=0)

    pack_mine = small_pack(conv_w, b_rg_a, b_rg_i, rg_lambda)
    pack_bits = lax.bitcast_convert_type(pack_mine, BF16).reshape(1, SMALL_ROWS, 2 * lh)
    win_gather = _SplitGather("gather_w_in", [(w_in[0], 1), (w_pool[0], 1), (pack_bits, 0)], BF16, after=pack_mine)
    wout_gather = _SplitGather("gather_w_out", [(w_out[0], 0)], BF16, after=win_gather.token)
    w1_gather = _SplitGather("gather_w_mlp_in", [(w_mlp_in[0], 1)], BF16, after=wout_gather.token)
    w2_gather = _SplitGather("gather_w_mlp_out", [(w_mlp_out[0], 0)], BF16, after=w1_gather.token)
    xb = _to_bf16("x_bf16", x[0], w2_gather.token)
    win_full, wpool_full, pack_bits_full = win_gather.wait(after=win_gather.relay(after=xb))
    pack_full = lax.bitcast_convert_type(pack_bits_full.reshape(N_DEV, SMALL_ROWS, lh, 2), F32)
    wcat = jnp.concatenate([w_rg_a[0, 0], w_rg_i[0, 0], w_rg_a[0, 1], w_rg_i[0, 1]], axis=-1).astype(BF16)
    vec = lambda i, j, k: (0, 0)
    row_full = lambda i, j, k: (i, 0)

    def after(token):
        return (token, _sp((8, 128), vec))

    def sds(shape, dtype):
        return jax.ShapeDtypeStruct(shape, dtype)

    def plain_epi(acc, i, ex, out):
        out[0][...] = acc

    def bf16_epi(acc, i, ex, out):
        out[0][...] = acc.astype(BF16)

    t = _tiles(seq, d_model, d_ff)

    (p3,) = _matmul(
        "proj", xb, win_full, _sp((t.rows, d_model), lambda i, j, k: (i, 0)), _sp((d_model, dh), lambda i, j, k: (0, j)),
        grid=(seq // t.rows, 3, 1),
        out_shape=[sds((3, seq, dh), F32)], out_specs=[_sp((None, t.rows, dh), lambda i, j, k: (j, i, 0))],
        epilogue=plain_epi)

    d_pool, y_half = _pool_fwd(p3, wpool_full, pool_scale, seq, d_model)
    y, h0p, h1p = _lru_fwd(p3, y_half, pack_full, conv_b, wcat, wout_gather.relay(after=y_half), seq, d_model)
    (wout_full,) = wout_gather.wait(after=y)

    mix_rows = 2 * t.ln_rows

    def mix_epi(acc, i, ex, out):
        x_ref, g_ref, b_ref = ex[:3]
        for part in range(2):
            rows = pl.ds(part * t.ln_rows, t.ln_rows)
            z = ALPHA * x_ref[rows, :] + acc[part * t.ln_rows:(part + 1) * t.ln_rows, :]
            x1, _, _ = _ln_fwd(z, g_ref[...], b_ref[...])
            out[0][rows, :] = z
            out[1][rows, :] = x1
            out[2][rows, :] = x1.astype(BF16)

    z1, x1, x1b = _matmul(
        "mix_out", y, wout_full, _sp((mix_rows, d_model), row_full), _sp((d_model, d_model), vec, single=True),
        grid=(seq // mix_rows, 1, 1),
        extras=[(xs, _sp((mix_rows, d_model), row_full)), (ln_mix_g, _sp((1, d_model), vec)),
                (ln_mix_b, _sp((1, d_model), vec))],
        out_shape=[sds((seq, d_model), F32), sds((seq, d_model), F32), sds((seq, d_model), BF16)],
        out_specs=[_sp((mix_rows, d_model), row_full)] * 3, epilogue=mix_epi)
    (w1_full,) = w1_gather.wait(after=w1_gather.relay(after=x1b))

    def mlp_in_epi(acc, i, ex, out, cols):
        h = jnp.maximum(acc, 0.0)
        out[0][:, cols] = (h * h).astype(BF16)
        out[1][:, cols] = (2.0 * h).astype(BF16)

    hmid, dact = _matmul(
        "mlp_in", x1b, w1_full, _sp((t.rows, d_model), lambda i, j, k: (i, 0)),
        _sp((d_model, t.ff_cols), lambda i, j, k: (0, j)),
        grid=(seq // t.rows, d_ff // t.ff_cols, 1), j_outer=True,
        out_shape=[sds((seq, d_ff), BF16)] * 2, out_specs=[_sp((t.rows, t.ff_cols), lambda i, j, k: (i, j))] * 2,
        epilogue=mlp_in_epi, n_split=t.ff_split)
    (w2_full,) = w2_gather.wait(after=w2_gather.relay(after=hmid))

    (ffn,) = _matmul(
        "mlp_out", hmid, w2_full, _sp((t.rows, t.ff_k), lambda i, j, k: (i, k)),
        _sp((t.ff_k, d_model), lambda i, j, k: (k, 0)),
        grid=(seq // t.rows, 1, d_ff // t.ff_k),
        out_shape=[sds((seq, d_model), F32)], out_specs=[_sp((t.rows, d_model), row_full)])
    dz2, dz2b, g_ffn_g, g_ffn_b, loss_part = _ln_loss_bwd(ffn, x1, tgt, ln_ffn_g, ln_ffn_b, t.ln_rows)

    (g_w2,) = _matmul(
        "grad_w_mlp_out", hmid, dz2b, _sp((seq, t.grad_rows), lambda i, j, k: (0, i)),
        _sp((seq, d_model), vec, single=True),
        grid=(d_ff // t.grad_rows, 1, 1), ta=True,
        out_shape=[sds((d_ff, d_model), BF16)], out_specs=[_sp((t.grad_rows, d_model), row_full)],
        epilogue=bf16_epi)
    scatter_w2 = _SplitReduceScatter("scatter_w_mlp_out", [g_w2.reshape(N_DEV, d_ff // N_DEV, d_model)])

    def dpre_epi(acc, i, ex, out, cols):
        out[0][:, cols] = (acc * ex[0][:, cols].astype(F32)).astype(BF16)

    (dpre,) = _matmul(
        "mlp_dpre", dz2b, w2_full, _sp((t.rows, d_model), lambda i, j, k: (i, 0)),
        _sp((t.ff_cols, d_model), lambda i, j, k: (j, 0)),
        grid=(seq // t.rows, d_ff // t.ff_cols, 1), j_outer=True, tb=True,
        extras=[(dact, _sp((t.rows, t.ff_cols), lambda i, j, k: (i, j))), after(scatter_w2.token)],
        out_shape=[sds((seq, d_ff), BF16)], out_specs=[_sp((t.rows, t.ff_cols), lambda i, j, k: (i, j))],
        epilogue=dpre_epi, n_split=t.ff_split)
    token_w2 = scatter_w2.combine_and_send(after=dpre)

    (dx1_mlp,) = _matmul(
        "mlp_dx", dpre, w1_full, _sp((t.rows, t.ff_k), lambda i, j, k: (i, k)),
        _sp((d_model, t.ff_k), lambda i, j, k: (0, k)),
        grid=(seq // t.rows, 1, d_ff // t.ff_k), tb=True, extras=[after(token_w2)],
        out_shape=[sds((seq, d_model), F32)], out_specs=[_sp((t.rows, d_model), row_full)])
    def block_epi(acc, i, ex, out):
        out[0][0] = acc.astype(BF16)

    fs = d_ff // N_DEV
    g_w1, dz1, dz1b, g_mix_g, g_mix_b = _matmul(
        "grad_w_mlp_in", x1b, dpre, _sp((seq, t.grad_rows), lambda i, j, k: (0, i)),
        _sp((seq, fs), lambda i, j, k: (0, j)),
        grid=(d_model // t.grad_rows, N_DEV, 1), j_outer=True, ta=True,
        out_shape=[sds((N_DEV, d_model, fs), BF16)],
        out_specs=[_sp((1, t.grad_rows, fs), lambda i, j, k: (j, i, 0))], epilogue=block_epi,
        side=_ln_bwd_side(dx1_mlp, dz2, z1, ln_mix_g, ln_mix_b, d_model // t.grad_rows * N_DEV))

    (dy,) = _matmul(
        "mix_dy", dz1b, wout_full, _sp((t.rows, d_model), lambda i, j, k: (i, 0)),
        _sp((dh, d_model), lambda i, j, k: (j, 0)),
        grid=(seq // t.rows, 2, 1), j_outer=True, tb=True,
        out_shape=[sds((seq, d_model), F32)], out_specs=[_sp((t.rows, dh), lambda i, j, k: (i, j))],
        epilogue=plain_epi)
    (g_wout,) = _matmul(
        "grad_w_out", y, dz1b, _sp((seq, t.grad_rows), lambda i, j, k: (0, i)), _sp((seq, d_model), vec, single=True),
        grid=(d_model // t.grad_rows, 1, 1), ta=True,
        out_shape=[sds((d_model, d_model), BF16)], out_specs=[_sp((t.grad_rows, d_model), row_full)],
        epilogue=bf16_epi)
    scatter_w1 = _SplitReduceScatter("scatter_w_mlp_in", [g_w1, g_wout.reshape(N_DEV, d_model // N_DEV, d_model)])

    dproj_pool, g_wpool, g_pscale = _pool_bwd(d_pool, dy, wpool_full, pool_scale, scatter_w1.token, seq, d_model)
    token_w1 = scatter_w1.combine_and_send(after=dproj_pool)
    dproj, g_pack, g_convb, g_wcat = _lru_bwd(p3, dy, h0p, h1p, dproj_pool, pack_full, conv_b, wcat,
                                              token_w1, seq, d_model)
    g_wa = jnp.stack([g_wcat[:, :, 0:lh], g_wcat[:, :, 2 * lh:3 * lh]])
    g_wi = jnp.stack([g_wcat[:, :, lh:2 * lh], g_wcat[:, :, 3 * lh:4 * lh]])

    rep_parts = [_rows128(g_wa), _rows128(g_wi), _rows128(g_mix_g), _rows128(g_mix_b), _rows128(g_ffn_g),
                 _rows128(g_ffn_b), _rows128(g_pscale), _rows128(g_convb)]
    rep_rows = [p.shape[0] for p in rep_parts]
    n_rep = sum(rep_rows)
    small = jnp.concatenate(rep_parts + [_rows128(g_pack), loss_part], axis=0)
    small_gather = _SplitGather("gather_small_grads", [(small[None], 0)], F32, after=small)

    ws = 3 * dh // N_DEV

    def pair_epi(acc, i, ex, out):
        out[0][0] = acc[:, :ws].astype(BF16)
        out[0][1] = acc[:, ws:].astype(BF16)

    def adam_big(name, own_landed, w, m, v):
        own, landed = own_landed
        shp = w.shape
        two = lambda a: a.reshape(-1, shp[-1])
        res = _sum_adamw(name, own, landed, two(w), two(m), two(v))
        return [r.reshape(shp) for r in res]

    (r_w2,) = scatter_w2.wait(after=small_gather.token)
    n_steps = d_model // t.grad_rows * (N_DEV // 2)
    g_win, *o_w2 = _matmul(
        "grad_w_in", xb, dproj, _sp((seq, t.grad_rows), lambda i, j, k: (0, i)),
        _sp((seq, 2 * ws), lambda i, j, k: (0, j)),
        grid=(d_model // t.grad_rows, N_DEV // 2, 1), ta=True,
        out_shape=[sds((N_DEV, d_model, ws), BF16)],
        out_specs=[_sp((2, t.grad_rows, ws), lambda i, j, k: (j, i, 0))], epilogue=pair_epi,
        side=_adamw_side(r_w2[0], r_w2[1], w_mlp_out[0], m_w_mlp_out[0], v_w_mlp_out[0], n_steps))
    o_w2 = [r.reshape(w_mlp_out.shape) for r in o_w2]
    scatter_mix = _SplitReduceScatter(
        "scatter_mixer", [g_win, g_wpool.reshape(N_DEV, pg // N_DEV * len(POOL_WINDOWS), pg)])

    r_w1, r_wout = scatter_w1.wait(after=scatter_mix.token)
    o_w1 = adam_big("adam_w_mlp_in", r_w1, w_mlp_in, m_w_mlp_in, v_w_mlp_in)
    token_mix = small_gather.relay(after=scatter_mix.combine_and_send(after=o_w1[0]))

    def dx_epi(acc, i, ex, out):
        out[0][...] = ALPHA * ex[0][...] + acc

    dx_rows = t.ln_rows * 2
    dx, *o_wout = _matmul(
        "grad_x", dproj, win_full, _sp((dx_rows, 3 * dh), lambda i, j, k: (i, 0)),
        _sp((d_model, 3 * dh), vec, single=True),
        grid=(seq // dx_rows, 1, 1), tb=True,
        extras=[(dz1, _sp((dx_rows, d_model), row_full)), after(token_mix)],
        out_shape=[sds((seq, d_model), F32)], out_specs=[_sp((dx_rows, d_model), row_full)],
        epilogue=dx_epi,
        side=_adamw_side(r_wout[0], r_wout[1], w_out[0], m_w_out[0], v_w_out[0], seq // dx_rows))
    o_wout = [r.reshape(w_out.shape) for r in o_wout]
    r_win, r_wpool = scatter_mix.wait(after=dx)
    o_win = adam_big("adam_w_in", r_win, w_in, m_w_in, v_w_in)
    o_wpool = adam_big("adam_w_pool", r_wpool, w_pool, m_w_pool, v_w_pool)

    (small_all,) = small_gather.wait(after=o_wpool[0])

    rep_w = [w_rg_a, w_rg_i, ln_mix_g, ln_mix_b, ln_ffn_g, ln_ffn_b, pool_scale, conv_b]
    rep_m = [m_w_rg_a, m_w_rg_i, m_ln_mix_g, m_ln_mix_b, m_ln_ffn_g, m_ln_ffn_b, m_pool_scale, m_conv_b]
    rep_v = [v_w_rg_a, v_w_rg_i, v_ln_mix_g, v_ln_mix_b, v_ln_ffn_g, v_ln_ffn_b, v_pool_scale, v_conv_b]
    cat = lambda arrs: jnp.concatenate([_rows128(a) for a in arrs], axis=0)
    o_rep = _sum_adamw("adam_replicated", None, small_all, cat(rep_w), cat(rep_m), cat(rep_v))

    my_idx = _dev_index(_where_am_i())
    head_parts = lax.dynamic_slice_in_dim(small_all, n_rep + my_idx * SMALL_ROWS, SMALL_ROWS, axis=1)
    o_head = _sum_adamw("adam_head", None, head_parts, pack_mine,
                        small_pack(m_conv_w, m_b_rg_a, m_b_rg_i, m_rg_lambda),
                        small_pack(v_conv_w, v_b_rg_a, v_b_rg_i, v_rg_lambda))

    def unpack_rep(packed):
        out, r = [], 0
        for wgt, rows in zip(rep_w, rep_rows):
            out.append(packed[r:r + rows].reshape(wgt.shape))
            r += rows
        return out

    def unpack_head(packed):
        return [packed[0:4].reshape(conv_w.shape), packed[4:6].reshape(b_rg_a.shape),
                packed[6:8].reshape(b_rg_i.shape), packed[8:10].reshape(rg_lambda.shape)]

    loss = _sum_blocks("loss_sum", small_all[:, n_rep + N_HEADS * SMALL_ROWS:, :])[0, 0]

    outs = [loss, dx[None]]
    for kind in range(4):
        ra, ri, mg, mb, fg, fb, ps, cb = unpack_rep(o_rep[kind])
        cw, ba, bi, lam = unpack_head(o_head[kind])
        outs += [mg, mb, o_win[kind], o_wpool[kind], ps, cw, cb, ra, ba, ri, bi, lam, o_wout[kind], fg, fb,
                 o_w1[kind], o_w2[kind]]
    return tuple(outs)
```

```python
import functools

import jax
import jax.numpy as jnp
from jax import lax
from jax.experimental import pallas as pl
from jax.experimental.pallas import tpu as pltpu

F32 = jnp.float32
BF16 = jnp.bfloat16
MESH = pl.DeviceIdType.MESH
ANY = pl.BlockSpec(memory_space=pl.ANY)

N_DEV = 8
POOL_WINDOWS = (2, 4, 8, 16)
N_HEADS = 8
RG_C = 8.0
LN_EPS = 1e-5
ALPHA = 2.0 ** 0.25
ADAM_LR = 0.001
ADAM_B1 = 0.9
ADAM_B2 = 0.999
ADAM_EPS = 1e-08
ADAM_WD = 0.01
ADAM_STEP = 10

VMEM_LIMIT = 56 * 1024 * 1024
SEQ_CHUNK = 256
POOL_CHUNK = 512
WIN_HALO = 16
CONV_HALO = 8
SMALL_ROWS = 16


def _params(n_grid):
    return pltpu.CompilerParams(dimension_semantics=("arbitrary",) * n_grid, vmem_limit_bytes=VMEM_LIMIT)


def _shift(v, j):
    n = v.shape[0]
    s = (-j) % n
    return v if s == 0 else pltpu.roll(v, s, 0)


def _sigmoid(x):
    return 0.5 * jnp.tanh(0.5 * x) + 0.5


def _softplus(z):
    e = jnp.exp(-jnp.abs(z))
    u = 1.0 + e
    log1p = jnp.where(u == 1.0, e, jnp.log(u) * (e / jnp.where(u == 1.0, 1.0, u - 1.0)))
    return jnp.maximum(z, 0.0) + log1p


_GELU_C = 0.7978845608028654
_GELU_K = 0.044715


def _gelu_and_grad(x):
    x2 = x * x
    t = jnp.tanh(_GELU_C * (x + _GELU_K * x * x2))
    g = 0.5 * x * (1.0 + t)
    dg = 0.5 * (1.0 + t) + 0.5 * x * (1.0 - t * t) * (_GELU_C * (1.0 + 3.0 * _GELU_K * x2))
    return g, dg


def _ln_fwd(z, g, b):
    mu = jnp.mean(z, axis=-1, keepdims=True)
    zc = z - mu
    var = jnp.mean(zc * zc, axis=-1, keepdims=True)
    rstd = lax.rsqrt(var + LN_EPS)
    xhat = zc * rstd
    return xhat * g + b, xhat, rstd


def _ln_bwd(dy, xhat, rstd, g):
    dxhat = dy * g
    m1 = jnp.mean(dxhat, axis=-1, keepdims=True)
    m2 = jnp.mean(dxhat * xhat, axis=-1, keepdims=True)
    dz = rstd * (dxhat - m1 - xhat * m2)
    dg = jnp.sum(dy * xhat, axis=0, keepdims=True)
    db = jnp.sum(dy, axis=0, keepdims=True)
    return dz, dg, db


def _acc_rows(ref, first, val):
    @pl.when(first)
    def _():
        ref[...] = val

    @pl.when(jnp.logical_not(first))
    def _():
        ref[...] += val


def _sp(shape, fn, single=False):
    return shape, fn, single


def _matmul(name, a, b, a_spec, b_spec, *, grid, j_outer=False, ta=False, tb=False, extras=(), out_shape, out_specs,
            epilogue=None, n_split=1, side=None):
    ni, nj, nk = grid
    n_ex = len(extras)
    dims = (((0 if ta else 1,), (1 if tb else 0,)), ((), ()))
    side_in, side_shape, side_out, side_fn = side if side is not None else ((), (), (), None)
    n_main_out = len(out_shape)
    inner = ni if j_outer else nj

    def mk(spec):
        shape, fn, single = spec
        index = (lambda g0, g1, g2: fn(g1, g0, g2)) if j_outer else fn
        return pl.BlockSpec(shape, index, pipeline_mode=pl.Buffered(1)) if single else pl.BlockSpec(shape, index)

    def mk_side(block, fn):
        return pl.BlockSpec(block, lambda g0, g1, g2: fn(g0 * inner + g1))

    def body(a_ref, b_ref, *rest):
        ex_refs = rest[:n_ex]
        out_refs = rest[n_ex + len(side_in):n_ex + len(side_in) + n_main_out]
        if side_fn is not None:
            side_fn(pl.program_id(0) * inner + pl.program_id(1), rest[n_ex:n_ex + len(side_in)],
                    rest[n_ex + len(side_in) + n_main_out:])
        i = pl.program_id(1 if j_outer else 0)
        if n_split > 1:
            av = a_ref[...].astype(BF16)
            width = b_ref.shape[0 if tb else 1] // n_split
            for c in range(n_split):
                cols = pl.ds(c * width, width)
                bv = (b_ref[cols, :] if tb else b_ref[:, cols]).astype(BF16)
                epilogue(lax.dot_general(av, bv, dims, preferred_element_type=F32), i, ex_refs, out_refs, cols)
            return
        part = lax.dot_general(a_ref[...].astype(BF16), b_ref[...].astype(BF16), dims, preferred_element_type=F32)
        if nk == 1:
            epilogue(part, i, ex_refs, out_refs)
        else:
            @pl.when(pl.program_id(2) == 0)
            def _():
                out_refs[0][...] = part

            @pl.when(pl.program_id(2) > 0)
            def _():
                out_refs[0][...] += part

    return pl.pallas_call(
        body, name=name, grid=(nj, ni, nk) if j_outer else (ni, nj, nk),
        in_specs=[mk(a_spec), mk(b_spec)] + [mk(s) for _, s in extras] + [mk_side(blk, fn) for _, blk, fn in side_in],
        out_specs=[mk(s) for s in out_specs] + [mk_side(blk, fn) for blk, fn in side_out],
        out_shape=list(out_shape) + list(side_shape),
        compiler_params=_params(3),
    )(a, b, *[x for x, _ in extras], *[x for x, _, _ in side_in])


def _bs(shape, fn):
    return pl.BlockSpec(shape, fn)


def _where_am_i():
    x, y, c = lax.axis_index("x"), lax.axis_index("y"), lax.axis_index("c")
    return x, y, c


def _dev_index(p):
    return 4 * p[0] + 2 * p[1] + p[2]


def _slab(ref, axis, idx, size):
    sl = [slice(None)] * len(ref.shape)
    sl[axis] = pl.ds(idx * size, size)
    return ref.at[tuple(sl)]


HBM = pl.BlockSpec(memory_space=pltpu.HBM)
SEM = pl.BlockSpec(memory_space=pltpu.SEMAPHORE)
DATAFLOW = pltpu.SideEffectType.DATAFLOW_SIDE_EFFECTING


def _in_hbm(a):
    return pltpu.with_memory_space_constraint(a, pltpu.HBM)


def _token_shape():
    return jax.ShapeDtypeStruct((8, 128), F32)


def _split_start(name, n_sems, bufs, issue):
    nb = len(bufs)

    def body(*refs):
        issue(refs[:nb], refs[nb], refs[nb + 1])
        refs[-1][...] = jnp.zeros((8, 128), F32)

    outs = pl.pallas_call(
        body, name=name,
        out_shape=(pltpu.SemaphoreType.DMA((n_sems,)), pltpu.SemaphoreType.DMA((n_sems,)),
                   *[pltpu.HBM(b.shape, b.dtype) for b in bufs], _token_shape()),
        in_specs=[HBM] * nb, out_specs=(SEM, SEM, *[HBM] * nb, pl.BlockSpec(memory_space=pltpu.VMEM)),
        input_output_aliases={i: 2 + i for i in range(nb)},
        compiler_params=pltpu.CompilerParams(has_side_effects=DATAFLOW),
    )(*[_in_hbm(b) for b in bufs])
    return outs[0], outs[1], list(outs[2:2 + nb]), outs[-1]


def _split_relay(name, n_sems, sems, bufs, after, relay):
    nb = len(bufs)

    def body(*refs):
        relay(refs[:nb], refs[nb], refs[nb + 1], refs[nb + 3], refs[nb + 4])
        refs[-1][...] = jnp.zeros((8, 128), F32)

    outs = pl.pallas_call(
        body, name=name,
        out_shape=(pltpu.SemaphoreType.DMA((n_sems,)), pltpu.SemaphoreType.DMA((n_sems,)),
                   *[pltpu.HBM(b.shape, b.dtype) for b in bufs], _token_shape()),
        in_specs=[HBM] * nb + [SEM, SEM, ANY],
        out_specs=(SEM, SEM, *[HBM] * nb, pl.BlockSpec(memory_space=pltpu.VMEM)),
        input_output_aliases={i: 2 + i for i in range(nb)},
        compiler_params=pltpu.CompilerParams(has_side_effects=DATAFLOW),
    )(*bufs, sems[0], sems[1], after)
    return outs[0], outs[1], list(outs[2:2 + nb]), outs[-1]


def _split_wait(name, sems, bufs, after, finish):
    nb = len(bufs)

    def body(*refs):
        finish(refs[:nb], refs[nb], refs[nb + 1])

    outs = pl.pallas_call(
        body, name=name, out_shape=[pltpu.HBM(b.shape, b.dtype) for b in bufs],
        in_specs=[HBM] * nb + [SEM, SEM, ANY], out_specs=[HBM] * nb,
        input_output_aliases={i: i for i in range(nb)},
        compiler_params=pltpu.CompilerParams(has_side_effects=DATAFLOW),
    )(*bufs, sems[0], sems[1], after)
    return list(outs)


def _place(name, items, dtype, after):
    ids = jnp.reshape(_dev_index(_where_am_i()), (1,)).astype(jnp.int32)
    outs = []
    for a, (shard, axis) in enumerate(items):
        rows, cols = shard.shape[-2], shard.shape[-1]
        tr = rows
        while tr * cols * shard.dtype.itemsize > 4 * 1024 * 1024 and tr % 32 == 0:
            tr //= 2
        nt = rows // tr
        full = list(shard.shape)
        full[axis] *= N_DEV
        if shard.ndim == 2 and axis == 0:
            in_spec = _bs((tr, cols), lambda i, ids: (i, 0))
            out_spec = _bs((tr, cols), lambda i, ids, nt=nt: (ids[0] * nt + i, 0))
        elif shard.ndim == 2 and axis == 1:
            in_spec = _bs((tr, cols), lambda i, ids: (i, 0))
            out_spec = _bs((tr, cols), lambda i, ids: (i, ids[0]))
        elif shard.ndim == 3 and axis == 1:
            tr, nt = rows, shard.shape[0]
            in_spec = _bs((None, rows, cols), lambda i, ids: (i, 0, 0))
            out_spec = _bs((None, rows, cols), lambda i, ids: (i, ids[0], 0))
        else:
            assert shard.ndim == 3 and axis == 0 and shard.shape[0] == 1
            in_spec = _bs((None, tr, cols), lambda i, ids: (0, i, 0))
            out_spec = _bs((None, tr, cols), lambda i, ids: (ids[0], i, 0))

        def body(ids_ref, in_ref, after_ref, out_ref):
            del ids_ref, after_ref
            out_ref[...] = in_ref[...].astype(out_ref.dtype)

        outs.append(pl.pallas_call(
            body, name=f"{name}{a}",
            grid_spec=pltpu.PrefetchScalarGridSpec(
                num_scalar_prefetch=1, grid=(nt,), in_specs=[in_spec, ANY], out_specs=out_spec),
            out_shape=jax.ShapeDtypeStruct(tuple(full), dtype), compiler_params=_params(1),
        )(ids, shard, after))
    return outs


class _SplitGather:
    def __init__(self, name, items, dtype, after):
        self.name, self.items, self.n = name, items, len(items)
        fulls = _place(name + "_place", items, dtype, after)
        n = self.n

        def issue(refs, send, recv):
            me, sibling, chips, c = self._geometry()
            for a in range(n):
                self._copy1(refs, send, recv, a, 0, me, sibling).start()
                for j, chip in enumerate(chips):
                    self._copy1(refs, send, recv, a, 1 + j, me, (*chip, c)).start()

        self.send, self.recv, self.bufs, self.token = _split_start(name + "_start", 4 * n, fulls, issue)

    @staticmethod
    def _geometry():
        x, y, c = _where_am_i()
        return (x, y, c), (x, y, 1 - c), [(1 - x, y), (x, 1 - y), (1 - x, 1 - y)], c

    def _blk(self, refs, a, p):
        shard, axis = self.items[a]
        return _slab(refs[a], axis, _dev_index(p), shard.shape[axis])

    def _copy1(self, refs, send, recv, a, k, owner, to):
        return pltpu.make_async_remote_copy(
            src_ref=self._blk(refs, a, owner), dst_ref=self._blk(refs, a, owner), send_sem=send.at[4 * a + k],
            recv_sem=recv.at[4 * a + k], device_id=to, device_id_type=MESH)

    def _copy2(self, refs, send, recv, a, j, owner, to):
        return pltpu.make_async_remote_copy(
            src_ref=self._blk(refs, a, owner), dst_ref=self._blk(refs, a, owner), send_sem=send.at[3 * a + j],
            recv_sem=recv.at[3 * a + j], device_id=to, device_id_type=MESH)

    def relay(self, after):
        n = self.n

        def relay(refs, send_in, recv_in, send_out, recv_out):
            me, sibling, chips, c = self._geometry()
            for a in range(n):
                for j, chip in enumerate(chips):
                    self._copy1(refs, send_in, recv_in, a, 1 + j, (*chip, c), me).wait_recv()
                    self._copy2(refs, send_out, recv_out, a, j, (*chip, c), sibling).start()
            for a in range(n):
                self._copy1(refs, send_in, recv_in, a, 0, sibling, me).wait_recv()
                for k in range(4):
                    self._copy1(refs, send_in, recv_in, a, k, me, sibling).wait_send()

        self.send, self.recv, self.bufs, self.token = _split_relay(
            self.name + "_relay", 3 * n, (self.send, self.recv), self.bufs, after, relay)
        return self.token

    def wait(self, after):
        n = self.n

        def finish(refs, send, recv):
            me, sibling, chips, c = self._geometry()
            for a in range(n):
                for j, chip in enumerate(chips):
                    self._copy2(refs, send, recv, a, j, (*chip, 1 - c), me).wait_recv()
                    self._copy2(refs, send, recv, a, j, (*chip, c), sibling).wait_send()

        return _split_wait(self.name + "_wait", (self.send, self.recv), self.bufs, after, finish)


class _SplitReduceScatter:
    def __init__(self, name, grads):
        self.name, self.n = name, len(grads)
        n = self.n
        g4 = [g.reshape(4, 2, *g.shape[1:]) for g in grads]
        land = [lax.empty((4, 1, *g.shape[1:]), g.dtype) for g in grads]

        def issue(refs, send, recv):
            for a in range(n):
                self._swap(refs, send, recv, a).start()

        self.send, self.recv, self.bufs, self.token = _split_start(name + "_d2d_start", n, g4 + land, issue)

    def _swap(self, refs, send, recv, a):
        x, y, c = _where_am_i()
        return pltpu.make_async_remote_copy(
            src_ref=refs[a].at[:, pl.ds(1 - c, 1)], dst_ref=refs[self.n + a], send_sem=send.at[a], recv_sem=recv.at[a],
            device_id=(x, y, 1 - c), device_id_type=MESH)

    def _hop(self, refs, send, recv, a, m):
        x, y, c = _where_am_i()
        px = (1 - x) if m & 2 else x
        py = (1 - y) if m & 1 else y
        return pltpu.make_async_remote_copy(
            src_ref=refs[a].at[2 * px + py], dst_ref=refs[self.n + a].at[m - 1], send_sem=send.at[3 * a + m - 1],
            recv_sem=recv.at[3 * a + m - 1], device_id=(px, py, c), device_id_type=MESH)

    def combine_and_send(self, after):
        n = self.n

        def finish(refs, send, recv):
            for a in range(n):
                self._swap(refs, send, recv, a).wait()

        bufs = _split_wait(self.name + "_d2d_wait", (self.send, self.recv), self.bufs, after, finish)
        x, y, c = _where_am_i()
        ids = jnp.stack([c, 2 * x + y]).astype(jnp.int32)
        self.own, sums = [], []
        for a in range(n):
            own, hb = _pair_sum(f"{self.name}_sum{a}", bufs[a], bufs[n + a], ids)
            self.own.append(own)
            sums.append(hb)
        land = [lax.empty((3, *h.shape[1:]), h.dtype) for h in sums]

        def issue(refs, send, recv):
            for a in range(n):
                for m in (1, 2, 3):
                    self._hop(refs, send, recv, a, m).start()

        self.send, self.recv, self.bufs, self.token = _split_start(self.name + "_ici_start", 3 * n, sums + land, issue)
        return self.token

    def wait(self, after):
        n = self.n

        def finish(refs, send, recv):
            for a in range(n):
                for m in (1, 2, 3):
                    self._hop(refs, send, recv, a, m).wait()

        bufs = _split_wait(self.name + "_ici_wait", (self.send, self.recv), self.bufs, after, finish)
        return list(zip(self.own, bufs[n:]))


def _pair_sum(name, g4, land, ids):
    rows, cols = g4.shape[2], g4.shape[3]
    tr = rows
    while tr * cols * 2 > 2 * 1024 * 1024 and tr % 32 == 0:
        tr //= 2

    def body(ids_ref, g_ref, l_ref, own_ref, sum_ref):
        h = g_ref[...].astype(F32) + l_ref[...].astype(F32)
        sum_ref[...] = h.astype(sum_ref.dtype)

        @pl.when(pl.program_id(1) == ids_ref[1])
        def _():
            own_ref[...] = h

    return pl.pallas_call(
        body, name=name,
        grid_spec=pltpu.PrefetchScalarGridSpec(
            num_scalar_prefetch=1, grid=(rows // tr, 4),
            in_specs=[_bs((None, None, tr, cols), lambda i, q, ids: (q, ids[0], i, 0)),
                      _bs((None, None, tr, cols), lambda i, q, ids: (q, 0, i, 0))],
            out_specs=[_bs((tr, cols), lambda i, q, ids: (i, 0)), _bs((None, tr, cols), lambda i, q, ids: (q, i, 0))]),
        out_shape=[jax.ShapeDtypeStruct((rows, cols), F32), jax.ShapeDtypeStruct((4, rows, cols), g4.dtype)],
        compiler_params=_params(2),
    )(ids, g4, land)


def _win_sum(ext, w, off):
    s = ext + _shift(ext, -1)
    if w >= 4:
        s = _shift(s, -1) + _shift(s, 1)
    if w >= 8:
        s = _shift(s, -2) + _shift(s, 2)
    if w >= 16:
        s = _shift(s, -4) + _shift(s, 4)
    return _shift(s, off) if off else s


def _inv_count(r0, t, w, seq):
    pos = r0 + lax.broadcasted_iota(jnp.int32, (t, 1), 0)
    cnt = jnp.minimum(pos + w // 2, seq) - jnp.maximum(pos - w // 2, 0)
    return 1.0 / cnt.astype(F32)


def _pool_fwd(p3, w_pool, pool_scale, seq, d_model):
    dp = d_model // 2
    pg = dp // len(POOL_WINDOWS)
    t = min(POOL_CHUNK, seq)
    n_chunks = seq // t
    h = WIN_HALO

    def body(u_ref, w_ref, sc_ref, d_ref, y_ref, pad_ref):
        g = pl.program_id(0)
        zeros = jnp.zeros((h, pg), F32)
        pad_ref[0:h, :] = zeros
        pad_ref[h + seq:h + seq + h, :] = zeros

        def fill(ci, _):
            r0 = pl.multiple_of(ci * t, t)
            pad_ref[pl.ds(h + r0, t), :] = u_ref[pl.ds(r0, t), :]
            return 0

        lax.fori_loop(0, n_chunks, fill, 0)
        wmat = w_ref[...]
        scale = sc_ref[...]
        for gi, w in enumerate(POOL_WINDOWS):
            @pl.when(g == gi)
            def _(w=w):
                def chunk(ci, _):
                    r0 = pl.multiple_of(ci * t, t)
                    ext = pad_ref[pl.ds(r0, t + 2 * h), :]
                    mean = _win_sum(ext, w, 0)[h:h + t, :] * _inv_count(r0, t, w, seq)
                    d = (mean - ext[h:h + t, :]).astype(BF16)
                    d_ref[pl.ds(r0, t), :] = d
                    q = jnp.dot(d, wmat, preferred_element_type=F32)
                    y_ref[pl.ds(r0, t), :] = (q * scale).astype(BF16)
                    return 0

                lax.fori_loop(0, n_chunks, chunk, 0, unroll=2)

    return pl.pallas_call(
        body, name="pool_fwd", grid=(len(POOL_WINDOWS),),
        in_specs=[_bs((None, seq, pg), lambda g: (0, 0, g)), _bs((None, pg, pg), lambda g: (g, 0, 0)),
                  _bs((1, pg), lambda g: (0, g))],
        out_specs=[_bs((seq, pg), lambda g: (0, g)), _bs((seq, pg), lambda g: (0, g))],
        out_shape=[jax.ShapeDtypeStruct((seq, dp), BF16), jax.ShapeDtypeStruct((seq, d_model), BF16)],
        scratch_shapes=[pltpu.VMEM((seq + 2 * h, pg), F32)],
        compiler_params=_params(1),
    )(p3, w_pool, pool_scale)


def _pool_bwd(d, dy, w_pool, pool_scale, token, seq, d_model):
    dp = d_model // 2
    pg = dp // len(POOL_WINDOWS)
    t = min(POOL_CHUNK, seq)
    n_chunks = seq // t
    h = WIN_HALO
    tn_dims = (((0,), (0,)), ((), ()))
    nt_dims = (((1,), (1,)), ((), ()))

    def body(d_ref, dy_ref, w_ref, sc_ref, tok_ref, du_ref, dwb_ref, dsc_ref, pad_ref, dd_ref, dw_ref):
        del tok_ref
        g = pl.program_id(0)
        zeros = jnp.zeros((h, pg), F32)
        pad_ref[0:h, :] = zeros
        pad_ref[h + seq:h + seq + h, :] = zeros
        wmat = w_ref[...]
        scale = sc_ref[...]
        for gi, w in enumerate(POOL_WINDOWS):
            @pl.when(g == gi)
            def _(w=w):
                dw_ref[...] = jnp.zeros((pg, pg), F32)

                def first(ci, dsc):
                    r0 = pl.multiple_of(ci * t, t)
                    dv = d_ref[pl.ds(r0, t), :]
                    dyv = dy_ref[pl.ds(r0, t), :]
                    q = jnp.dot(dv, wmat, preferred_element_type=F32)
                    dsc = dsc + jnp.sum(dyv * q, axis=0, keepdims=True)
                    dq = (dyv * scale).astype(BF16)
                    dw_ref[...] += lax.dot_general(dv, dq, tn_dims, preferred_element_type=F32)
                    dd = lax.dot_general(dq, wmat, nt_dims, preferred_element_type=F32)
                    dd_ref[pl.ds(r0, t), :] = dd
                    pad_ref[pl.ds(h + r0, t), :] = dd * _inv_count(r0, t, w, seq)
                    return dsc

                def first_pair(cj, dsc):
                    return first(2 * cj + 1, first(2 * cj, dsc))

                dsc_ref[...] = lax.fori_loop(0, n_chunks // 2, first_pair, jnp.zeros((1, pg), F32))
                dwb_ref[...] = dw_ref[...].reshape(N_DEV, pg // N_DEV, pg).astype(BF16)

                def second(ci, _):
                    r0 = pl.multiple_of(ci * t, t)
                    ext = pad_ref[pl.ds(r0, t + 2 * h), :]
                    back = _win_sum(ext, w, 1)[h:h + t, :]
                    du_ref[pl.ds(r0, t), :] = (back - dd_ref[pl.ds(r0, t), :]).astype(BF16)
                    return 0

                lax.fori_loop(0, n_chunks, second, 0, unroll=2)

    return pl.pallas_call(
        body, name="pool_bwd", grid=(len(POOL_WINDOWS),),
        in_specs=[_bs((seq, pg), lambda g: (0, g)), _bs((seq, pg), lambda g: (0, g)),
                  _bs((None, pg, pg), lambda g: (g, 0, 0)), _bs((1, pg), lambda g: (0, g)),
                  _bs((8, 128), lambda g: (0, 0))],
        out_specs=[_bs((seq, pg), lambda g: (0, g)), _bs((N_DEV, None, pg // N_DEV, pg), lambda g: (0, g, 0, 0)),
                   _bs((1, pg), lambda g: (0, g))],
        out_shape=[jax.ShapeDtypeStruct((seq, 3 * dp), BF16),
                   jax.ShapeDtypeStruct((N_DEV, len(POOL_WINDOWS), pg // N_DEV, pg), BF16),
                   jax.ShapeDtypeStruct((1, dp), F32)],
        scratch_shapes=[pltpu.VMEM((seq + 2 * h, pg), F32), pltpu.VMEM((seq, pg), F32), pltpu.VMEM((pg, pg), F32)],
        compiler_params=_params(1),
    )(d, dy, w_pool, pool_scale, token)


def _tile_scan(n_tiles, lanes, loads, stores):
    row = lax.broadcasted_iota(jnp.int32, (8, lanes), 0)
    group = 8

    def local_scan(n, k):
        aa, bb = loads[n](k)
        for sh in (1, 2, 4):
            if n == 0:
                ok = row >= sh
                ap = jnp.where(ok, pltpu.roll(aa, sh, 0), 1.0)
                bp = jnp.where(ok, pltpu.roll(bb, sh, 0), 0.0)
            else:
                ok = row < 8 - sh
                ap = jnp.where(ok, pltpu.roll(aa, 8 - sh, 0), 1.0)
                bp = jnp.where(ok, pltpu.roll(bb, 8 - sh, 0), 0.0)
            bb = aa * bp + bb
            aa = aa * ap
        return aa, bb

    def step(s, carry):
        carry = list(carry)
        for n in range(2):
            tiles = [s * group + u if n == 0 else n_tiles - 1 - (s * group + u) for u in range(group)]
            local = [local_scan(n, k) for k in tiles]
            for k, (aa, bb) in zip(tiles, local):
                hh = bb + aa * carry[n]
                stores[n](k, hh)
                carry[n] = jnp.broadcast_to(hh[7:8, :] if n == 0 else hh[0:1, :], (8, lanes))
        return tuple(carry)

    zeros = jnp.zeros((8, lanes), F32)
    lax.fori_loop(0, n_tiles // group, step, (zeros, zeros))


def _gate_preacts(xc, wcat_ref):
    xcb = xc.astype(BF16)
    return xcb, jnp.dot(xcb, wcat_ref[...], preferred_element_type=F32)


def _gates(pre, n, pk_ref, sp):
    lh = pre.shape[1] // 4
    r = _sigmoid(pre[:, (2 * n) * lh:(2 * n + 1) * lh] + pk_ref[pl.ds(4 + n, 1), :])
    i = _sigmoid(pre[:, (2 * n + 1) * lh:(2 * n + 2) * lh] + pk_ref[pl.ds(6 + n, 1), :])
    log_a = (-RG_C * r) * sp[n]
    a = jnp.exp(log_a)
    x = 2.0 * log_a
    one_minus_a2 = jnp.where(x > -0.01, -(x * (1.0 + x * (0.5 + x * (1.0 / 6.0)))), 1.0 - a * a)
    m = jnp.sqrt(one_minus_a2)
    return r, i, a, m


def _conv_chunk(upad_ref, pk_ref, cb, r0, t):
    ext = upad_ref[pl.ds(r0, t + 2 * CONV_HALO), :]
    acc = pk_ref[pl.ds(1, 1), :] * ext
    for k in (0, 2, 3):
        acc = acc + pk_ref[pl.ds(k, 1), :] * _shift(ext, k - 1)
    return acc[CONV_HALO:CONV_HALO + t, :] + cb, ext


def _lru_fwd(p3, y_in, pack, conv_b, wcat, token, seq, d_model):
    dl = d_model // 2
    lh = dl // N_HEADS
    t = min(SEQ_CHUNK, seq)
    n_chunks = seq // t
    hal = CONV_HALO
    first_rec_block = (d_model - dl) // lh

    def body(ur_ref, ug_ref, pk_ref, cb_ref, wcat_ref, yin_ref, tok_ref, y_ref, h0_ref, h1_ref,
             upad, a_scr, b_scr):
        del yin_ref, tok_ref
        zeros = jnp.zeros((hal, lh), F32)
        upad[0:hal, :] = zeros
        upad[hal + seq:hal + seq + hal, :] = zeros
        for ref in (h0_ref, h1_ref):
            ref[0:hal, :] = zeros
            ref[hal + seq:hal + seq + hal, :] = zeros

        def fill(ci, _):
            r0 = pl.multiple_of(ci * t, t)
            upad[pl.ds(hal + r0, t), :] = ur_ref[pl.ds(r0, t), :]
            return 0

        lax.fori_loop(0, n_chunks, fill, 0)
        cb = cb_ref[...]
        sp = [_softplus(-pk_ref[pl.ds(8 + n, 1), :]) for n in range(2)]

        def chunk(ci, _):
            r0 = pl.multiple_of(ci * t, t)
            xc, _ext = _conv_chunk(upad, pk_ref, cb, r0, t)
            _, pre = _gate_preacts(xc, wcat_ref)
            for n in range(2):
                _, i, a, m = _gates(pre, n, pk_ref, sp)
                a_scr[n, pl.ds(r0, t), :] = a
                b_scr[n, pl.ds(r0, t), :] = (m * i) * xc
            return 0

        lax.fori_loop(0, n_chunks, chunk, 0, unroll=2)

        def load(n):
            def get(k):
                at = pl.ds(pl.multiple_of(k * 8, 8), 8)
                return a_scr[n, at, :], b_scr[n, at, :]
            return get

        def store(ref):
            def put(k, v):
                ref[pl.ds(pl.multiple_of(hal + k * 8, 8), 8), :] = v
            return put

        _tile_scan(seq // 8, lh, [load(0), load(1)], [store(h0_ref), store(h1_ref)])

        def out(ci, _):
            r0 = pl.multiple_of(ci * t, t)
            hsum = h0_ref[pl.ds(hal + r0, t), :] + h1_ref[pl.ds(hal + r0, t), :]
            gl, _dg = _gelu_and_grad(ug_ref[pl.ds(r0, t), :])
            y_ref[pl.ds(r0, t), :] = (hsum * gl).astype(BF16)
            return 0

        lax.fori_loop(0, n_chunks, out, 0)

    return pl.pallas_call(
        body, name="lru_fwd", grid=(N_HEADS,),
        in_specs=[_bs((None, seq, lh), lambda h: (1, 0, h)), _bs((None, seq, lh), lambda h: (2, 0, h)),
                  _bs((None, SMALL_ROWS, lh), lambda h: (h, 0, 0)), _bs((1, lh), lambda h: (0, h)),
                  _bs((None, lh, 4 * lh), lambda h: (h, 0, 0)),
                  ANY, _bs((8, 128), lambda h: (0, 0))],
        out_specs=[_bs((seq, lh), lambda h: (0, first_rec_block + h)),
                   _bs((seq + 2 * hal, lh), lambda h: (0, h)), _bs((seq + 2 * hal, lh), lambda h: (0, h))],
        out_shape=[jax.ShapeDtypeStruct((seq, d_model), BF16), jax.ShapeDtypeStruct((seq + 2 * hal, dl), F32),
                   jax.ShapeDtypeStruct((seq + 2 * hal, dl), F32)],
        scratch_shapes=[pltpu.VMEM((seq + 2 * hal, lh), F32), pltpu.VMEM((2, seq, lh), F32),
                        pltpu.VMEM((2, seq, lh), F32)],
        input_output_aliases={5: 0},
        compiler_params=_params(1),
    )(p3, p3, pack, conv_b, wcat, y_in, token)


def _lru_bwd(p3, dy, h0p, h1p, dproj_in, pack, conv_b, wcat, token, seq, d_model):
    dl = d_model // 2
    lh = dl // N_HEADS
    t = min(SEQ_CHUNK, seq)
    n_chunks = seq // t
    hal = CONV_HALO
    first_rec_block = (d_model - dl) // lh
    tn_dims = (((0,), (0,)), ((), ()))
    nt_dims = (((1,), (1,)), ((), ()))

    def body(ur_ref, ug_ref, dy_ref, h0_ref, h1_ref, pk_ref, cb_ref, wcat_ref, tok_ref, din_ref,
             dproj_ref, dpk_ref, dcb_ref, dwcat_ref,
             upad, a_scr, dh_scr, g_scr, dxc_pad, dpr_ref, out_sems, gate_scr):
        del din_ref, tok_ref
        head = pl.program_id(0)
        slot = head % 2
        zeros = jnp.zeros((hal, lh), F32)
        for ref in (upad, dxc_pad):
            ref[0:hal, :] = zeros
            ref[hal + seq:hal + seq + hal, :] = zeros
        for n in range(2):
            a_scr[n, 0:hal, :] = zeros
            a_scr[n, hal + seq:hal + seq + hal, :] = zeros

        def fill(ci, _):
            r0 = pl.multiple_of(ci * t, t)
            upad[pl.ds(hal + r0, t), :] = ur_ref[pl.ds(r0, t), :]
            return 0

        lax.fori_loop(0, n_chunks, fill, 0)
        cb = cb_ref[...]
        lam = [pk_ref[pl.ds(8 + n, 1), :] for n in range(2)]
        sp = [_softplus(-lam[n]) for n in range(2)]

        def chunk1(ci, _):
            r0 = pl.multiple_of(ci * t, t)
            xc, _ext = _conv_chunk(upad, pk_ref, cb, r0, t)
            _, pre = _gate_preacts(xc, wcat_ref)
            for n in range(2):
                r, i, a, m = _gates(pre, n, pk_ref, sp)
                a_scr[n, pl.ds(hal + r0, t), :] = a
                for q, v in enumerate((r, i, m)):
                    gate_scr[3 * n + q, pl.ds(r0, t), :] = v
            hsum = h0_ref[pl.ds(hal + r0, t), :] + h1_ref[pl.ds(hal + r0, t), :]
            gl, dgl = _gelu_and_grad(ug_ref[pl.ds(r0, t), :])
            dyv = dy_ref[pl.ds(r0, t), :]
            dh_scr[pl.ds(r0, t), :] = dyv * gl
            dpr_ref[slot, 1, pl.ds(r0, t), :] = ((dyv * hsum) * dgl).astype(BF16)
            return 0

        lax.fori_loop(0, n_chunks, chunk1, 0, unroll=2)

        def load(n):
            def get(k):
                r0 = pl.multiple_of(k * 8, 8)
                if n == 0:
                    coef = _shift(a_scr[0, pl.ds(pl.multiple_of(hal + r0, 8), 16), :], 1)[0:8, :]
                else:
                    coef = _shift(a_scr[1, pl.ds(pl.multiple_of(hal + r0 - 8, 8), 16), :], -1)[8:16, :]
                return coef, dh_scr[pl.ds(r0, 8), :]
            return get

        def store(n):
            def put(k, v):
                g_scr[n, pl.ds(pl.multiple_of(k * 8, 8), 8), :] = v
            return put

        _tile_scan(seq // 8, lh, [load(1), load(0)], [store(1), store(0)])

        dwcat_ref[...] = jnp.zeros((lh, 4 * lh), F32)

        def chunk3(ci, carry):
            dba, dbi, dlam, dcb = carry
            r0 = pl.multiple_of(ci * t, t)
            xc, _ext = _conv_chunk(upad, pk_ref, cb, r0, t)
            xcb = xc.astype(BF16)
            dxc = jnp.zeros((t, lh), F32)
            dba, dbi, dlam = list(dba), list(dbi), list(dlam)
            dpre = []
            for n in range(2):
                r, i, m = (gate_scr[3 * n + q, pl.ds(r0, t), :] for q in range(3))
                a = a_scr[n, pl.ds(hal + r0, t), :]
                hext = (h0_ref if n == 0 else h1_ref)[pl.ds(r0, t + 2 * hal), :]
                hprev = _shift(hext, -1 if n == 0 else 1)[hal:hal + t, :]
                gb = g_scr[n, pl.ds(r0, t), :]
                da = gb * hprev
                dm = gb * i * xc
                di = gb * m * xc
                dxc = dxc + gb * (m * i)
                dlog_a = da * a - dm * (a * a) / m
                dr = dlog_a * (-RG_C * sp[n])
                dlam[n] = dlam[n] + jnp.sum(dlog_a * r, axis=0, keepdims=True)
                dpr = dr * r * (1.0 - r)
                dpi = di * i * (1.0 - i)
                dba[n] = dba[n] + jnp.sum(dpr, axis=0, keepdims=True)
                dbi[n] = dbi[n] + jnp.sum(dpi, axis=0, keepdims=True)
                dpre += [dpr.astype(BF16), dpi.astype(BF16)]
            dpre = jnp.concatenate(dpre, axis=1)
            dwcat_ref[...] += lax.dot_general(xcb, dpre, tn_dims, preferred_element_type=F32)
            dxc = dxc + lax.dot_general(dpre, wcat_ref[...], nt_dims, preferred_element_type=F32)
            dxc_pad[pl.ds(hal + r0, t), :] = dxc
            dcb = dcb + jnp.sum(dxc, axis=0, keepdims=True)
            return tuple(dba), tuple(dbi), tuple(dlam), dcb

        zr = jnp.zeros((1, lh), F32)
        def chunk3_pair(cj, carry):
            return chunk3(2 * cj + 1, chunk3(2 * cj, carry))

        dba, dbi, dlam, dcb = lax.fori_loop(0, n_chunks // 2, chunk3_pair, ((zr, zr), (zr, zr), (zr, zr), zr))
        dcb_ref[...] = dcb
        for n in range(2):
            dpk_ref[pl.ds(4 + n, 1), :] = dba[n]
            dpk_ref[pl.ds(6 + n, 1), :] = dbi[n]
            dpk_ref[pl.ds(8 + n, 1), :] = dlam[n] * (RG_C * jax.nn.sigmoid(-lam[n]))
        dpk_ref[pl.ds(10, SMALL_ROWS - 10), :] = jnp.zeros((SMALL_ROWS - 10, lh), F32)

        def chunk4(ci, dtap):
            r0 = pl.multiple_of(ci * t, t)
            gext = dxc_pad[pl.ds(r0, t + 2 * hal), :]
            uext = upad[pl.ds(r0, t + 2 * hal), :]
            gmid = gext[hal:hal + t, :]
            du = pk_ref[pl.ds(1, 1), :] * gext
            for k in (0, 2, 3):
                du = du + pk_ref[pl.ds(k, 1), :] * _shift(gext, 1 - k)
            dpr_ref[slot, 0, pl.ds(r0, t), :] = du[hal:hal + t, :].astype(BF16)
            out = []
            for k in range(4):
                usl = _shift(uext, k - 1)[hal:hal + t, :]
                out.append(dtap[k] + jnp.sum(gmid * usl, axis=0, keepdims=True))
            return tuple(out)

        dtap = lax.fori_loop(0, n_chunks, chunk4, (zr, zr, zr, zr))
        for k in range(4):
            dpk_ref[pl.ds(k, 1), :] = dtap[k]

        def out_copy(s, b):
            return pltpu.make_async_copy(
                dpr_ref.at[s, b], dproj_ref.at[:, pl.ds(pl.multiple_of((1 + b) * dl + head * lh, lh), lh)],
                out_sems.at[s, b])

        for b in range(2):
            out_copy(slot, b).start()

        @pl.when(head > 0)
        def _():
            for b in range(2):
                out_copy(1 - slot, b).wait()

        @pl.when(head == N_HEADS - 1)
        def _():
            for b in range(2):
                out_copy(slot, b).wait()

    return pl.pallas_call(
        body, name="lru_bwd", grid=(N_HEADS,),
        in_specs=[_bs((None, seq, lh), lambda h: (1, 0, h)), _bs((None, seq, lh), lambda h: (2, 0, h)),
                  _bs((seq, lh), lambda h: (0, first_rec_block + h)),
                  _bs((seq + 2 * hal, lh), lambda h: (0, h)), _bs((seq + 2 * hal, lh), lambda h: (0, h)),
                  _bs((None, SMALL_ROWS, lh), lambda h: (h, 0, 0)), _bs((1, lh), lambda h: (0, h)),
                  _bs((None, lh, 4 * lh), lambda h: (h, 0, 0)),
                  _bs((8, 128), lambda h: (0, 0)), ANY],
        out_specs=[ANY, _bs((None, SMALL_ROWS, lh), lambda h: (h, 0, 0)),
                   _bs((1, lh), lambda h: (0, h)), _bs((None, lh, 4 * lh), lambda h: (h, 0, 0))],
        out_shape=[jax.ShapeDtypeStruct((seq, 3 * dl), BF16), jax.ShapeDtypeStruct((N_HEADS, SMALL_ROWS, lh), F32),
                   jax.ShapeDtypeStruct((1, dl), F32), jax.ShapeDtypeStruct((N_HEADS, lh, 4 * lh), F32)],
        scratch_shapes=[pltpu.VMEM((seq + 2 * hal, lh), F32), pltpu.VMEM((2, seq + 2 * hal, lh), F32),
                        pltpu.VMEM((seq, lh), F32), pltpu.VMEM((2, seq, lh), F32),
                        pltpu.VMEM((seq + 2 * hal, lh), F32), pltpu.VMEM((2, 2, seq, lh), BF16),
                        pltpu.SemaphoreType.DMA((2, 2)), pltpu.VMEM((6, seq, lh), F32)],
        input_output_aliases={9: 0},
        compiler_params=_params(1),
    )(p3, p3, dy, h0p, h1p, pack, conv_b, wcat, token, dproj_in)


class _tiles:
    def __init__(self, seq, d_model, d_ff):
        self.rows = min(1024, seq)
        self.ln_rows = min(256, seq)
        self.ff_cols = min(2048, d_ff)
        self.ff_split = 4
        self.ff_k = min(2048, d_ff)
        self.grad_rows = 512


def _ln_loss_bwd(ffn, x1, tgt, g, b, tr):
    seq, d = ffn.shape

    def body(f_ref, x_ref, t_ref, g_ref, b_ref, dz_ref, dzb_ref, dg_ref, db_ref, loss_ref):
        i = pl.program_id(0)
        gv = g_ref[...]
        z = ALPHA * x_ref[...] + f_ref[...]
        y, xhat, rstd = _ln_fwd(z, gv, b_ref[...])
        err = y - t_ref[...]
        part = 0.5 * jnp.sum(jnp.mean(err * err, axis=-1, keepdims=True), axis=0, keepdims=True)
        dz, dg, db = _ln_bwd(err * (1.0 / d), xhat, rstd, gv)
        dz_ref[...] = dz
        dzb_ref[...] = dz.astype(BF16)
        _acc_rows(dg_ref, i == 0, dg)
        _acc_rows(db_ref, i == 0, db)
        _acc_rows(loss_ref, i == 0, jnp.broadcast_to(part, (8, 128)))

    row = _bs((tr, d), lambda i: (i, 0))
    vec = _bs((1, d), lambda i: (0, 0))
    return pl.pallas_call(
        body, name="ln_ffn_loss", grid=(seq // tr,), in_specs=[row, row, row, vec, vec],
        out_specs=[row, row, vec, vec, _bs((8, 128), lambda i: (0, 0))],
        out_shape=[jax.ShapeDtypeStruct((seq, d), F32), jax.ShapeDtypeStruct((seq, d), BF16),
                   jax.ShapeDtypeStruct((1, d), F32), jax.ShapeDtypeStruct((1, d), F32),
                   jax.ShapeDtypeStruct((8, 128), F32)],
        compiler_params=_params(1),
    )(ffn, x1, tgt, g, b)


def _ln_bwd_side(dx_branch, dres, z, g, b, n_steps):
    seq, d = z.shape
    tr = seq // n_steps

    def fn(step, ins, outs):
        a_ref, r_ref, z_ref, g_ref, b_ref = ins
        dz_ref, dzb_ref, dg_ref, db_ref = outs
        gv = g_ref[...]
        _, xhat, rstd = _ln_fwd(z_ref[...], gv, b_ref[...])
        dz, dg, db = _ln_bwd(ALPHA * r_ref[...] + a_ref[...], xhat, rstd, gv)
        dz_ref[...] = dz
        dzb_ref[...] = dz.astype(BF16)
        _acc_rows(dg_ref, step == 0, dg)
        _acc_rows(db_ref, step == 0, db)

    row = ((tr, d), lambda s: (s, 0))
    vec = ((1, d), lambda s: (0, 0))
    shapes = [jax.ShapeDtypeStruct((seq, d), F32), jax.ShapeDtypeStruct((seq, d), BF16),
              jax.ShapeDtypeStruct((1, d), F32), jax.ShapeDtypeStruct((1, d), F32)]
    return [(dx_branch, *row), (dres, *row), (z, *row), (g, *vec), (b, *vec)], shapes, [row, row, vec, vec], fn


def _to_bf16(name, a, token):
    rows, cols = a.shape
    tr = min(512, rows)

    def body(a_ref, tok_ref, o_ref):
        del tok_ref
        o_ref[...] = a_ref[...].astype(BF16)

    return pl.pallas_call(
        body, name=name, grid=(rows // tr,),
        in_specs=[_bs((tr, cols), lambda i: (i, 0)), _bs((8, 128), lambda i: (0, 0))],
        out_specs=_bs((tr, cols), lambda i: (i, 0)), out_shape=jax.ShapeDtypeStruct((rows, cols), BF16),
        compiler_params=_params(1),
    )(a, token)


def _sum_blocks(name, parts):
    def body(p_ref, o_ref):
        acc = p_ref[0]
        for s in range(1, parts.shape[0]):
            acc = acc + p_ref[s]
        o_ref[...] = acc

    return pl.pallas_call(body, name=name, out_shape=jax.ShapeDtypeStruct(parts.shape[1:], F32))(parts)


def _adamw_values(w, g, m, v):
    m = ADAM_B1 * m + (1.0 - ADAM_B1) * g
    v = ADAM_B2 * v + (1.0 - ADAM_B2) * (g * g)
    m_hat = m / (1.0 - ADAM_B1 ** ADAM_STEP)
    v_hat = v / (1.0 - ADAM_B2 ** ADAM_STEP)
    delta = -ADAM_LR * (m_hat / (jnp.sqrt(v_hat) + ADAM_EPS) + ADAM_WD * w)
    return delta, m, v


def _adamw_side(own, parts, w, m, v, n_steps):
    rows, cols = w.shape
    tr = rows // n_steps

    def fn(step, ins, outs):
        o_ref, p_ref, w_ref, m_ref, v_ref = ins
        g = o_ref[...]
        for s in range(parts.shape[0]):
            g = g + p_ref[s].astype(F32)
        delta, mn, vn = _adamw_values(w_ref[...], g, m_ref[...], v_ref[...])
        for ref, val in zip(outs, (g, delta, mn, vn)):
            ref[...] = val

    row = ((tr, cols), lambda s: (s, 0))
    stack = ((parts.shape[0], tr, cols), lambda s: (0, s, 0))
    shapes = [jax.ShapeDtypeStruct((rows, cols), F32)] * 4
    return [(own, *row), (parts, *stack), (w, *row), (m, *row), (v, *row)], shapes, [row] * 4, fn


def _sum_adamw(name, own, parts, w, m, v):
    rows, cols = w.shape
    n_parts = parts.shape[0]
    tr = rows
    min_rows = 8 if parts.dtype == F32 else 16
    while tr * cols * 4 > 2 * 1024 * 1024 and tr % (2 * min_rows) == 0:
        tr //= 2

    def body(*refs):
        if own is None:
            p_ref, w_ref, m_ref, v_ref, g_ref, d_ref, mo_ref, vo_ref = refs
            g = p_ref[0].astype(F32)
            rest = range(1, n_parts)
        else:
            o_ref, p_ref, w_ref, m_ref, v_ref, g_ref, d_ref, mo_ref, vo_ref = refs
            g = o_ref[...]
            rest = range(n_parts)
        for s in rest:
            g = g + p_ref[s].astype(F32)
        delta, mn, vn = _adamw_values(w_ref[...], g, m_ref[...], v_ref[...])
        g_ref[...] = g
        d_ref[...] = delta
        mo_ref[...] = mn
        vo_ref[...] = vn

    spec = _bs((tr, cols), lambda i: (i, 0))
    lead = [] if own is None else [own]
    return pl.pallas_call(
        body, name=name, grid=(rows // tr,),
        in_specs=[spec] * len(lead) + [_bs((n_parts, tr, cols), lambda i: (0, i, 0)), spec, spec, spec],
        out_specs=[spec] * 4, out_shape=[jax.ShapeDtypeStruct((rows, cols), F32)] * 4,
        compiler_params=_params(1),
    )(*lead, parts, w, m, v)


def _rows128(a):
    return a.reshape(-1, 128)


def kernel(x, ln_mix_g, ln_mix_b, w_in, w_pool, pool_scale, conv_w, conv_b, w_rg_a, b_rg_a, w_rg_i, b_rg_i, rg_lambda, w_out, ln_ffn_g, ln_ffn_b, w_mlp_in, w_mlp_out, loss_target, m_ln_mix_g, m_ln_mix_b, m_w_in, m_w_pool, m_pool_scale, m_conv_w, m_conv_b, m_w_rg_a, m_b_rg_a, m_w_rg_i, m_b_rg_i, m_rg_lambda, m_w_out, m_ln_ffn_g, m_ln_ffn_b, m_w_mlp_in, m_w_mlp_out, v_ln_mix_g, v_ln_mix_b, v_w_in, v_w_pool, v_pool_scale, v_conv_w, v_conv_b, v_w_rg_a, v_b_rg_a, v_w_rg_i, v_b_rg_i, v_rg_lambda, v_w_out, v_ln_ffn_g, v_ln_ffn_b, v_w_mlp_in, v_w_mlp_out):
    seq, d_model = x.shape[1], x.shape[2]
    dh = d_model // 2
    lh = dh // N_HEADS
    pg = dh // len(POOL_WINDOWS)
    d_ff = w_mlp_in.shape[2] * N_DEV
    assert lh == 128 and conv_w.shape[3] == lh and w_pool.shape[2] * N_DEV == pg

    xs = x[0]
    tgt = loss_target[0]

    def small_pack(cw, ba, bi, lam):
        return jnp.concatenate([cw.reshape(4, lh), ba.reshape(2, lh), bi.reshape(2, lh), lam.reshape(2, lh),
                                jnp.zeros((SMALL_ROWS - 10, lh), F32)], axis=0)

    pack_mine = small_pack(conv_w, b_rg_a, b_rg_i, rg_lambda)
    pack_bits = lax.bitcast_convert_type(pack_mine, BF16).reshape(1, SMALL_ROWS, 2 * lh)
    win_gather = _SplitGather("gather_w_in", [(w_in[0], 1), (w_pool[0], 1), (pack_bits, 0)], BF16, after=pack_mine)
    wout_gather = _SplitGather("gather_w_out", [(w_out[0], 0)], BF16, after=win_gather.token)
    w1_gather = _SplitGather("gather_w_mlp_in", [(w_mlp_in[0], 1)], BF16, after=wout_gather.token)
    w2_gather = _SplitGather("gather_w_mlp_out", [(w_mlp_out[0], 0)], BF16, after=w1_gather.token)
    xb = _to_bf16("x_bf16", x[0], w2_gather.token)
    win_full, wpool_full, pack_bits_full = win_gather.wait(after=win_gather.relay(after=xb))
    pack_full = lax.bitcast_convert_type(pack_bits_full.reshape(N_DEV, SMALL_ROWS, lh, 2), F32)
    wcat = jnp.concatenate([w_rg_a[0, 0], w_rg_i[0, 0], w_rg_a[0, 1], w_rg_i[0, 1]], axis=-1).astype(BF16)
    vec = lambda i, j, k: (0, 0)
    row_full = lambda i, j, k: (i, 0)

    def after(token):
        return (token, _sp((8, 128), vec))

    def sds(shape, dtype):
        return jax.ShapeDtypeStruct(shape, dtype)

    def plain_epi(acc, i, ex, out):
        out[0][...] = acc

    def bf16_epi(acc, i, ex, out):
        out[0][...] = acc.astype(BF16)

    t = _tiles(seq, d_model, d_ff)

    (p3,) = _matmul(
        "proj", xb, win_full, _sp((t.rows, d_model), lambda i, j, k: (i, 0)), _sp((d_model, dh), lambda i, j, k: (0, j)),
        grid=(seq // t.rows, 3, 1),
        out_shape=[sds((3, seq, dh), F32)], out_specs=[_sp((None, t.rows, dh), lambda i, j, k: (j, i, 0))],
        epilogue=plain_epi)

    d_pool, y_half = _pool_fwd(p3, wpool_full, pool_scale, seq, d_model)
    y, h0p, h1p = _lru_fwd(p3, y_half, pack_full, conv_b, wcat, wout_gather.relay(after=y_half), seq, d_model)
    (wout_full,) = wout_gather.wait(after=y)

    mix_rows = 2 * t.ln_rows

    def mix_epi(acc, i, ex, out):
        x_ref, g_ref, b_ref = ex[:3]
        for part in range(2):
            rows = pl.ds(part * t.ln_rows, t.ln_rows)
            z = ALPHA * x_ref[rows, :] + acc[part * t.ln_rows:(part + 1) * t.ln_rows, :]
            x1, _, _ = _ln_fwd(z, g_ref[...], b_ref[...])
            out[0][rows, :] = z
            out[1][rows, :] = x1
            out[2][rows, :] = x1.astype(BF16)

    z1, x1, x1b = _matmul(
        "mix_out", y, wout_full, _sp((mix_rows, d_model), row_full), _sp((d_model, d_model), vec, single=True),
        grid=(seq // mix_rows, 1, 1),
        extras=[(xs, _sp((mix_rows, d_model), row_full)), (ln_mix_g, _sp((1, d_model), vec)),
                (ln_mix_b, _sp((1, d_model), vec))],
        out_shape=[sds((seq, d_model), F32), sds((seq, d_model), F32), sds((seq, d_model), BF16)],
        out_specs=[_sp((mix_rows, d_model), row_full)] * 3, epilogue=mix_epi)
    (w1_full,) = w1_gather.wait(after=w1_gather.relay(after=x1b))

    def mlp_in_epi(acc, i, ex, out, cols):
        h = jnp.maximum(acc, 0.0)
        out[0][:, cols] = (h * h).astype(BF16)
        out[1][:, cols] = (2.0 * h).astype(BF16)

    hmid, dact = _matmul(
        "mlp_in", x1b, w1_full, _sp((t.rows, d_model), lambda i, j, k: (i, 0)),
        _sp((d_model, t.ff_cols), lambda i, j, k: (0, j)),
        grid=(seq // t.rows, d_ff // t.ff_cols, 1), j_outer=True,
        out_shape=[sds((seq, d_ff), BF16)] * 2, out_specs=[_sp((t.rows, t.ff_cols), lambda i, j, k: (i, j))] * 2,
        epilogue=mlp_in_epi, n_split=t.ff_split)
    (w2_full,) = w2_gather.wait(after=w2_gather.relay(after=hmid))

    (ffn,) = _matmul(
        "mlp_out", hmid, w2_full, _sp((t.rows, t.ff_k), lambda i, j, k: (i, k)),
        _sp((t.ff_k, d_model), lambda i, j, k: (k, 0)),
        grid=(seq // t.rows, 1, d_ff // t.ff_k),
        out_shape=[sds((seq, d_model), F32)], out_specs=[_sp((t.rows, d_model), row_full)])
    dz2, dz2b, g_ffn_g, g_ffn_b, loss_part = _ln_loss_bwd(ffn, x1, tgt, ln_ffn_g, ln_ffn_b, t.ln_rows)

    (g_w2,) = _matmul(
        "grad_w_mlp_out", hmid, dz2b, _sp((seq, t.grad_rows), lambda i, j, k: (0, i)),
        _sp((seq, d_model), vec, single=True),
        grid=(d_ff // t.grad_rows, 1, 1), ta=True,
        out_shape=[sds((d_ff, d_model), BF16)], out_specs=[_sp((t.grad_rows, d_model), row_full)],
        epilogue=bf16_epi)
    scatter_w2 = _SplitReduceScatter("scatter_w_mlp_out", [g_w2.reshape(N_DEV, d_ff // N_DEV, d_model)])

    def dpre_epi(acc, i, ex, out, cols):
        out[0][:, cols] = (acc * ex[0][:, cols].astype(F32)).astype(BF16)

    (dpre,) = _matmul(
        "mlp_dpre", dz2b, w2_full, _sp((t.rows, d_model), lambda i, j, k: (i, 0)),
        _sp((t.ff_cols, d_model), lambda i, j, k: (j, 0)),
        grid=(seq // t.rows, d_ff // t.ff_cols, 1), j_outer=True, tb=True,
        extras=[(dact, _sp((t.rows, t.ff_cols), lambda i, j, k: (i, j))), after(scatter_w2.token)],
        out_shape=[sds((seq, d_ff), BF16)], out_specs=[_sp((t.rows, t.ff_cols), lambda i, j, k: (i, j))],
        epilogue=dpre_epi, n_split=t.ff_split)
    token_w2 = scatter_w2.combine_and_send(after=dpre)

    (dx1_mlp,) = _matmul(
        "mlp_dx", dpre, w1_full, _sp((t.rows, t.ff_k), lambda i, j, k: (i, k)),
        _sp((d_model, t.ff_k), lambda i, j, k: (0, k)),
        grid=(seq // t.rows, 1, d_ff // t.ff_k), tb=True, extras=[after(token_w2)],
        out_shape=[sds((seq, d_model), F32)], out_specs=[_sp((t.rows, d_model), row_full)])
    def block_epi(acc, i, ex, out):
        out[0][0] = acc.astype(BF16)

    fs = d_ff // N_DEV
    g_w1, dz1, dz1b, g_mix_g, g_mix_b = _matmul(
        "grad_w_mlp_in", x1b, dpre, _sp((seq, t.grad_rows), lambda i, j, k: (0, i)),
        _sp((seq, fs), lambda i, j, k: (0, j)),
        grid=(d_model // t.grad_rows, N_DEV, 1), j_outer=True, ta=True,
        out_shape=[sds((N_DEV, d_model, fs), BF16)],
        out_specs=[_sp((1, t.grad_rows, fs), lambda i, j, k: (j, i, 0))], epilogue=block_epi,
        side=_ln_bwd_side(dx1_mlp, dz2, z1, ln_mix_g, ln_mix_b, d_model // t.grad_rows * N_DEV))

    (dy,) = _matmul(
        "mix_dy", dz1b, wout_full, _sp((t.rows, d_model), lambda i, j, k: (i, 0)),
        _sp((dh, d_model), lambda i, j, k: (j, 0)),
        grid=(seq // t.rows, 2, 1), j_outer=True, tb=True,
        out_shape=[sds((seq, d_model), F32)], out_specs=[_sp((t.rows, dh), lambda i, j, k: (i, j))],
        epilogue=plain_epi)
    (g_wout,) = _matmul(
        "grad_w_out", y, dz1b, _sp((seq, t.grad_rows), lambda i, j, k: (0, i)), _sp((seq, d_model), vec, single=True),
        grid=(d_model // t.grad_rows, 1, 1), ta=True,
        out_shape=[sds((d_model, d_model), BF16)], out_specs=[_sp((t.grad_rows, d_model), row_full)],
        epilogue=bf16_epi)
    scatter_w1 = _SplitReduceScatter("scatter_w_mlp_in", [g_w1, g_wout.reshape(N_DEV, d_model // N_DEV, d_model)])

    dproj_pool, g_wpool, g_pscale = _pool_bwd(d_pool, dy, wpool_full, pool_scale, scatter_w1.token, seq, d_model)
    token_w1 = scatter_w1.combine_and_send(after=dproj_pool)
    dproj, g_pack, g_convb, g_wcat = _lru_bwd(p3, dy, h0p, h1p, dproj_pool, pack_full, conv_b, wcat,
                                              token_w1, seq, d_model)
    g_wa = jnp.stack([g_wcat[:, :, 0:lh], g_wcat[:, :, 2 * lh:3 * lh]])
    g_wi = jnp.stack([g_wcat[:, :, lh:2 * lh], g_wcat[:, :, 3 * lh:4 * lh]])

    rep_parts = [_rows128(g_wa), _rows128(g_wi), _rows128(g_mix_g), _rows128(g_mix_b), _rows128(g_ffn_g),
                 _rows128(g_ffn_b), _rows128(g_pscale), _rows128(g_convb)]
    rep_rows = [p.shape[0] for p in rep_parts]
    n_rep = sum(rep_rows)
    small = jnp.concatenate(rep_parts + [_rows128(g_pack), loss_part], axis=0)
    small_gather = _SplitGather("gather_small_grads", [(small[None], 0)], F32, after=small)

    ws = 3 * dh // N_DEV

    def pair_epi(acc, i, ex, out):
        out[0][0] = acc[:, :ws].astype(BF16)
        out[0][1] = acc[:, ws:].astype(BF16)

    def adam_big(name, own_landed, w, m, v):
        own, landed = own_landed
        shp = w.shape
        two = lambda a: a.reshape(-1, shp[-1])
        res = _sum_adamw(name, own, landed, two(w), two(m), two(v))
        return [r.reshape(shp) for r in res]

    (r_w2,) = scatter_w2.wait(after=small_gather.token)
    n_steps = d_model // t.grad_rows * (N_DEV // 2)
    g_win, *o_w2 = _matmul(
        "grad_w_in", xb, dproj, _sp((seq, t.grad_rows), lambda i, j, k: (0, i)),
        _sp((seq, 2 * ws), lambda i, j, k: (0, j)),
        grid=(d_model // t.grad_rows, N_DEV // 2, 1), ta=True,
        out_shape=[sds((N_DEV, d_model, ws), BF16)],
        out_specs=[_sp((2, t.grad_rows, ws), lambda i, j, k: (j, i, 0))], epilogue=pair_epi,
        side=_adamw_side(r_w2[0], r_w2[1], w_mlp_out[0], m_w_mlp_out[0], v_w_mlp_out[0], n_steps))
    o_w2 = [r.reshape(w_mlp_out.shape) for r in o_w2]
    scatter_mix = _SplitReduceScatter(
        "scatter_mixer", [g_win, g_wpool.reshape(N_DEV, pg // N_DEV * len(POOL_WINDOWS), pg)])

    r_w1, r_wout = scatter_w1.wait(after=scatter_mix.token)
    o_w1 = adam_big("adam_w_mlp_in", r_w1, w_mlp_in, m_w_mlp_in, v_w_mlp_in)
    token_mix = small_gather.relay(after=scatter_mix.combine_and_send(after=o_w1[0]))

    def dx_epi(acc, i, ex, out):
        out[0][...] = ALPHA * ex[0][...] + acc

    dx_rows = t.ln_rows * 2
    dx, *o_wout = _matmul(
        "grad_x", dproj, win_full, _sp((dx_rows, 3 * dh), lambda i, j, k: (i, 0)),
        _sp((d_model, 3 * dh), vec, single=True),
        grid=(seq // dx_rows, 1, 1), tb=True,
        extras=[(dz1, _sp((dx_rows, d_model), row_full)), after(token_mix)],
        out_shape=[sds((seq, d_model), F32)], out_specs=[_sp((dx_rows, d_model), row_full)],
        epilogue=dx_epi,
        side=_adamw_side(r_wout[0], r_wout[1], w_out[0], m_w_out[0], v_w_out[0], seq // dx_rows))
    o_wout = [r.reshape(w_out.shape) for r in o_wout]
    r_win, r_wpool = scatter_mix.wait(after=dx)
    o_win = adam_big("adam_w_in", r_win, w_in, m_w_in, v_w_in)
    o_wpool = adam_big("adam_w_pool", r_wpool, w_pool, m_w_pool, v_w_pool)

    (small_all,) = small_gather.wait(after=o_wpool[0])

    rep_w = [w_rg_a, w_rg_i, ln_mix_g, ln_mix_b, ln_ffn_g, ln_ffn_b, pool_scale, conv_b]
    rep_m = [m_w_rg_a, m_w_rg_i, m_ln_mix_g, m_ln_mix_b, m_ln_ffn_g, m_ln_ffn_b, m_pool_scale, m_conv_b]
    rep_v = [v_w_rg_a, v_w_rg_i, v_ln_mix_g, v_ln_mix_b, v_ln_ffn_g, v_ln_ffn_b, v_pool_scale, v_conv_b]
    cat = lambda arrs: jnp.concatenate([_rows128(a) for a in arrs], axis=0)
    o_rep = _sum_adamw("adam_replicated", None, small_all, cat(rep_w), cat(rep_m), cat(rep_v))

    my_idx = _dev_index(_where_am_i())
    head_parts = lax.dynamic_slice_in_dim(small_all, n_rep + my_idx * SMALL_ROWS, SMALL_ROWS, axis=1)
    o_head = _sum_adamw("adam_head", None, head_parts, pack_mine,
                        small_pack(m_conv_w, m_b_rg_a, m_b_rg_i, m_rg_lambda),
                        small_pack(v_conv_w, v_b_rg_a, v_b_rg_i, v_rg_lambda))

    def unpack_rep(packed):
        out, r = [], 0
        for wgt, rows in zip(rep_w, rep_rows):
            out.append(packed[r:r + rows].reshape(wgt.shape))
            r += rows
        return out

    def unpack_head(packed):
        return [packed[0:4].reshape(conv_w.shape), packed[4:6].reshape(b_rg_a.shape),
                packed[6:8].reshape(b_rg_i.shape), packed[8:10].reshape(rg_lambda.shape)]

    loss = _sum_blocks("loss_sum", small_all[:, n_rep + N_HEADS * SMALL_ROWS:, :])[0, 0]

    outs = [loss, dx[None]]
    for kind in range(4):
        ra, ri, mg, mb, fg, fb, ps, cb = unpack_rep(o_rep[kind])
        cw, ba, bi, lam = unpack_head(o_head[kind])
        outs += [mg, mb, o_win[kind], o_wpool[kind], ps, cw, cb, ra, ba, ri, bi, lam, o_wout[kind], fg, fb,
                 o_w1[kind], o_w2[kind]]
    return tuple(outs)
```

```python
import functools

import jax
import jax.numpy as jnp
from jax import lax
from jax.experimental import pallas as pl
from jax.experimental.pallas import tpu as pltpu

F32 = jnp.float32
BF16 = jnp.bfloat16
MESH = pl.DeviceIdType.MESH
ANY = pl.BlockSpec(memory_space=pl.ANY)

N_DEV = 8
POOL_WINDOWS = (2, 4, 8, 16)
N_HEADS = 8
RG_C = 8.0
LN_EPS = 1e-5
ALPHA = 2.0 ** 0.25
ADAM_LR = 0.001
ADAM_B1 = 0.9
ADAM_B2 = 0.999
ADAM_EPS = 1e-08
ADAM_WD = 0.01
ADAM_STEP = 10

VMEM_LIMIT = 56 * 1024 * 1024
SEQ_CHUNK = 256
POOL_CHUNK = 512
WIN_HALO = 16
CONV_HALO = 8
SMALL_ROWS = 16


def _params(n_grid):
    return pltpu.CompilerParams(dimension_semantics=("arbitrary",) * n_grid, vmem_limit_bytes=VMEM_LIMIT)


def _shift(v, j):
    n = v.shape[0]
    s = (-j) % n
    return v if s == 0 else pltpu.roll(v, s, 0)


def _sigmoid(x):
    return 0.5 * jnp.tanh(0.5 * x) + 0.5


def _softplus(z):
    e = jnp.exp(-jnp.abs(z))
    u = 1.0 + e
    log1p = jnp.where(u == 1.0, e, jnp.log(u) * (e / jnp.where(u == 1.0, 1.0, u - 1.0)))
    return jnp.maximum(z, 0.0) + log1p


_GELU_C = 0.7978845608028654
_GELU_K = 0.044715


def _gelu_and_grad(x):
    x2 = x * x
    t = jnp.tanh(_GELU_C * (x + _GELU_K * x * x2))
    g = 0.5 * x * (1.0 + t)
    dg = 0.5 * (1.0 + t) + 0.5 * x * (1.0 - t * t) * (_GELU_C * (1.0 + 3.0 * _GELU_K * x2))
    return g, dg


def _ln_fwd(z, g, b):
    mu = jnp.mean(z, axis=-1, keepdims=True)
    zc = z - mu
    var = jnp.mean(zc * zc, axis=-1, keepdims=True)
    rstd = lax.rsqrt(var + LN_EPS)
    xhat = zc * rstd
    return xhat * g + b, xhat, rstd


def _ln_bwd(dy, xhat, rstd, g):
    dxhat = dy * g
    m1 = jnp.mean(dxhat, axis=-1, keepdims=True)
    m2 = jnp.mean(dxhat * xhat, axis=-1, keepdims=True)
    dz = rstd * (dxhat - m1 - xhat * m2)
    dg = jnp.sum(dy * xhat, axis=0, keepdims=True)
    db = jnp.sum(dy, axis=0, keepdims=True)
    return dz, dg, db


def _acc_rows(ref, first, val):
    @pl.when(first)
    def _():
        ref[...] = val

    @pl.when(jnp.logical_not(first))
    def _():
        ref[...] += val


def _sp(shape, fn, single=False):
    return shape, fn, single


def _matmul(name, a, b, a_spec, b_spec, *, grid, j_outer=False, ta=False, tb=False, extras=(), out_shape, out_specs,
            epilogue=None, n_split=1, side=None):
    ni, nj, nk = grid
    n_ex = len(extras)
    dims = (((0 if ta else 1,), (1 if tb else 0,)), ((), ()))
    side_in, side_shape, side_out, side_fn = side if side is not None else ((), (), (), None)
    n_main_out = len(out_shape)
    inner = ni if j_outer else nj

    def mk(spec):
        shape, fn, single = spec
        index = (lambda g0, g1, g2: fn(g1, g0, g2)) if j_outer else fn
        return pl.BlockSpec(shape, index, pipeline_mode=pl.Buffered(1)) if single else pl.BlockSpec(shape, index)

    def mk_side(block, fn):
        return pl.BlockSpec(block, lambda g0, g1, g2: fn(g0 * inner + g1))

    def body(a_ref, b_ref, *rest):
        ex_refs = rest[:n_ex]
        out_refs = rest[n_ex + len(side_in):n_ex + len(side_in) + n_main_out]
        if side_fn is not None:
            side_fn(pl.program_id(0) * inner + pl.program_id(1), rest[n_ex:n_ex + len(side_in)],
                    rest[n_ex + len(side_in) + n_main_out:])
        i = pl.program_id(1 if j_outer else 0)
        if n_split > 1:
            av = a_ref[...].astype(BF16)
            width = b_ref.shape[0 if tb else 1] // n_split
            for c in range(n_split):
                cols = pl.ds(c * width, width)
                bv = (b_ref[cols, :] if tb else b_ref[:, cols]).astype(BF16)
                epilogue(lax.dot_general(av, bv, dims, preferred_element_type=F32), i, ex_refs, out_refs, cols)
            return
        part = lax.dot_general(a_ref[...].astype(BF16), b_ref[...].astype(BF16), dims, preferred_element_type=F32)
        if nk == 1:
            epilogue(part, i, ex_refs, out_refs)
        else:
            @pl.when(pl.program_id(2) == 0)
            def _():
                out_refs[0][...] = part

            @pl.when(pl.program_id(2) > 0)
            def _():
                out_refs[0][...] += part

    return pl.pallas_call(
        body, name=name, grid=(nj, ni, nk) if j_outer else (ni, nj, nk),
        in_specs=[mk(a_spec), mk(b_spec)] + [mk(s) for _, s in extras] + [mk_side(blk, fn) for _, blk, fn in side_in],
        out_specs=[mk(s) for s in out_specs] + [mk_side(blk, fn) for blk, fn in side_out],
        out_shape=list(out_shape) + list(side_shape),
        compiler_params=_params(3),
    )(a, b, *[x for x, _ in extras], *[x for x, _, _ in side_in])


def _bs(shape, fn):
    return pl.BlockSpec(shape, fn)


def _where_am_i():
    x, y, c = lax.axis_index("x"), lax.axis_index("y"), lax.axis_index("c")
    return x, y, c


def _dev_index(p):
    return 4 * p[0] + 2 * p[1] + p[2]


def _slab(ref, axis, idx, size):
    sl = [slice(None)] * len(ref.shape)
    sl[axis] = pl.ds(idx * size, size)
    return ref.at[tuple(sl)]


HBM = pl.BlockSpec(memory_space=pltpu.HBM)
SEM = pl.BlockSpec(memory_space=pltpu.SEMAPHORE)
DATAFLOW = pltpu.SideEffectType.DATAFLOW_SIDE_EFFECTING


def _in_hbm(a):
    return pltpu.with_memory_space_constraint(a, pltpu.HBM)


def _token_shape():
    return jax.ShapeDtypeStruct((8, 128), F32)


def _split_start(name, n_sems, bufs, issue):
    nb = len(bufs)

    def body(*refs):
        issue(refs[:nb], refs[nb], refs[nb + 1])
        refs[-1][...] = jnp.zeros((8, 128), F32)

    outs = pl.pallas_call(
        body, name=name,
        out_shape=(pltpu.SemaphoreType.DMA((n_sems,)), pltpu.SemaphoreType.DMA((n_sems,)),
                   *[pltpu.HBM(b.shape, b.dtype) for b in bufs], _token_shape()),
        in_specs=[HBM] * nb, out_specs=(SEM, SEM, *[HBM] * nb, pl.BlockSpec(memory_space=pltpu.VMEM)),
        input_output_aliases={i: 2 + i for i in range(nb)},
        compiler_params=pltpu.CompilerParams(has_side_effects=DATAFLOW),
    )(*[_in_hbm(b) for b in bufs])
    return outs[0], outs[1], list(outs[2:2 + nb]), outs[-1]


def _split_relay(name, n_sems, sems, bufs, after, relay):
    nb = len(bufs)

    def body(*refs):
        relay(refs[:nb], refs[nb], refs[nb + 1], refs[nb + 3], refs[nb + 4])
        refs[-1][...] = jnp.zeros((8, 128), F32)

    outs = pl.pallas_call(
        body, name=name,
        out_shape=(pltpu.SemaphoreType.DMA((n_sems,)), pltpu.SemaphoreType.DMA((n_sems,)),
                   *[pltpu.HBM(b.shape, b.dtype) for b in bufs], _token_shape()),
        in_specs=[HBM] * nb + [SEM, SEM, ANY],
        out_specs=(SEM, SEM, *[HBM] * nb, pl.BlockSpec(memory_space=pltpu.VMEM)),
        input_output_aliases={i: 2 + i for i in range(nb)},
        compiler_params=pltpu.CompilerParams(has_side_effects=DATAFLOW),
    )(*bufs, sems[0], sems[1], after)
    return outs[0], outs[1], list(outs[2:2 + nb]), outs[-1]


def _split_wait(name, sems, bufs, after, finish):
    nb = len(bufs)

    def body(*refs):
        finish(refs[:nb], refs[nb], refs[nb + 1])

    outs = pl.pallas_call(
        body, name=name, out_shape=[pltpu.HBM(b.shape, b.dtype) for b in bufs],
        in_specs=[HBM] * nb + [SEM, SEM, ANY], out_specs=[HBM] * nb,
        input_output_aliases={i: i for i in range(nb)},
        compiler_params=pltpu.CompilerParams(has_side_effects=DATAFLOW),
    )(*bufs, sems[0], sems[1], after)
    return list(outs)


def _place(name, items, dtype, after):
    ids = jnp.reshape(_dev_index(_where_am_i()), (1,)).astype(jnp.int32)
    outs = []
    for a, (shard, axis) in enumerate(items):
        rows, cols = shard.shape[-2], shard.shape[-1]
        tr = rows
        while tr * cols * shard.dtype.itemsize > 4 * 1024 * 1024 and tr % 32 == 0:
            tr //= 2
        nt = rows // tr
        full = list(shard.shape)
        full[axis] *= N_DEV
        if shard.ndim == 2 and axis == 0:
            in_spec = _bs((tr, cols), lambda i, ids: (i, 0))
            out_spec = _bs((tr, cols), lambda i, ids, nt=nt: (ids[0] * nt + i, 0))
        elif shard.ndim == 2 and axis == 1:
            in_spec = _bs((tr, cols), lambda i, ids: (i, 0))
            out_spec = _bs((tr, cols), lambda i, ids: (i, ids[0]))
        elif shard.ndim == 3 and axis == 1:
            tr, nt = rows, shard.shape[0]
            in_spec = _bs((None, rows, cols), lambda i, ids: (i, 0, 0))
            out_spec = _bs((None, rows, cols), lambda i, ids: (i, ids[0], 0))
        else:
            assert shard.ndim == 3 and axis == 0 and shard.shape[0] == 1
            in_spec = _bs((None, tr, cols), lambda i, ids: (0, i, 0))
            out_spec = _bs((None, tr, cols), lambda i, ids: (ids[0], i, 0))

        def body(ids_ref, in_ref, after_ref, out_ref):
            del ids_ref, after_ref
            out_ref[...] = in_ref[...].astype(out_ref.dtype)

        outs.append(pl.pallas_call(
            body, name=f"{name}{a}",
            grid_spec=pltpu.PrefetchScalarGridSpec(
                num_scalar_prefetch=1, grid=(nt,), in_specs=[in_spec, ANY], out_specs=out_spec),
            out_shape=jax.ShapeDtypeStruct(tuple(full), dtype), compiler_params=_params(1),
        )(ids, shard, after))
    return outs


class _SplitGather:
    def __init__(self, name, items, dtype, after):
        self.name, self.items, self.n = name, items, len(items)
        fulls = _place(name + "_place", items, dtype, after)
        n = self.n

        def issue(refs, send, recv):
            me, sibling, chips, c = self._geometry()
            for a in range(n):
                self._copy1(refs, send, recv, a, 0, me, sibling).start()
                for j, chip in enumerate(chips):
                    self._copy1(refs, send, recv, a, 1 + j, me, (*chip, c)).start()

        self.send, self.recv, self.bufs, self.token = _split_start(name + "_start", 4 * n, fulls, issue)

    @staticmethod
    def _geometry():
        x, y, c = _where_am_i()
        return (x, y, c), (x, y, 1 - c), [(1 - x, y), (x, 1 - y), (1 - x, 1 - y)], c

    def _blk(self, refs, a, p):
        shard, axis = self.items[a]
        return _slab(refs[a], axis, _dev_index(p), shard.shape[axis])

    def _copy1(self, refs, send, recv, a, k, owner, to):
        return pltpu.make_async_remote_copy(
            src_ref=self._blk(refs, a, owner), dst_ref=self._blk(refs, a, owner), send_sem=send.at[4 * a + k],
            recv_sem=recv.at[4 * a + k], device_id=to, device_id_type=MESH)

    def _copy2(self, refs, send, recv, a, j, owner, to):
        return pltpu.make_async_remote_copy(
            src_ref=self._blk(refs, a, owner), dst_ref=self._blk(refs, a, owner), send_sem=send.at[3 * a + j],
            recv_sem=recv.at[3 * a + j], device_id=to, device_id_type=MESH)

    def relay(self, after):
        n = self.n

        def relay(refs, send_in, recv_in, send_out, recv_out):
            me, sibling, chips, c = self._geometry()
            for a in range(n):
                for j, chip in enumerate(chips):
                    self._copy1(refs, send_in, recv_in, a, 1 + j, (*chip, c), me).wait_recv()
                    self._copy2(refs, send_out, recv_out, a, j, (*chip, c), sibling).start()
            for a in range(n):
                self._copy1(refs, send_in, recv_in, a, 0, sibling, me).wait_recv()
                for k in range(4):
                    self._copy1(refs, send_in, recv_in, a, k, me, sibling).wait_send()

        self.send, self.recv, self.bufs, self.token = _split_relay(
            self.name + "_relay", 3 * n, (self.send, self.recv), self.bufs, after, relay)
        return self.token

    def wait(self, after):
        n = self.n

        def finish(refs, send, recv):
            me, sibling, chips, c = self._geometry()
            for a in range(n):
                for j, chip in enumerate(chips):
                    self._copy2(refs, send, recv, a, j, (*chip, 1 - c), me).wait_recv()
                    self._copy2(refs, send, recv, a, j, (*chip, c), sibling).wait_send()

        return _split_wait(self.name + "_wait", (self.send, self.recv), self.bufs, after, finish)


class _SplitReduceScatter:
    def __init__(self, name, grads):
        self.name, self.n = name, len(grads)
        n = self.n
        g4 = [g.reshape(4, 2, *g.shape[1:]) for g in grads]
        land = [lax.empty((4, 1, *g.shape[1:]), g.dtype) for g in grads]

        def issue(refs, send, recv):
            for a in range(n):
                self._swap(refs, send, recv, a).start()

        self.send, self.recv, self.bufs, self.token = _split_start(name + "_d2d_start", n, g4 + land, issue)

    def _swap(self, refs, send, recv, a):
        x, y, c = _where_am_i()
        return pltpu.make_async_remote_copy(
            src_ref=refs[a].at[:, pl.ds(1 - c, 1)], dst_ref=refs[self.n + a], send_sem=send.at[a], recv_sem=recv.at[a],
            device_id=(x, y, 1 - c), device_id_type=MESH)

    def _hop(self, refs, send, recv, a, m):
        x, y, c = _where_am_i()
        px = (1 - x) if m & 2 else x
        py = (1 - y) if m & 1 else y
        return pltpu.make_async_remote_copy(
            src_ref=refs[a].at[2 * px + py], dst_ref=refs[self.n + a].at[m - 1], send_sem=send.at[3 * a + m - 1],
            recv_sem=recv.at[3 * a + m - 1], device_id=(px, py, c), device_id_type=MESH)

    def combine_and_send(self, after):
        n = self.n

        def finish(refs, send, recv):
            for a in range(n):
                self._swap(refs, send, recv, a).wait()

        bufs = _split_wait(self.name + "_d2d_wait", (self.send, self.recv), self.bufs, after, finish)
        x, y, c = _where_am_i()
        ids = jnp.stack([c, 2 * x + y]).astype(jnp.int32)
        self.own, sums = [], []
        for a in range(n):
            own, hb = _pair_sum(f"{self.name}_sum{a}", bufs[a], bufs[n + a], ids)
            self.own.append(own)
            sums.append(hb)
        land = [lax.empty((3, *h.shape[1:]), h.dtype) for h in sums]

        def issue(refs, send, recv):
            for a in range(n):
                for m in (1, 2, 3):
                    self._hop(refs, send, recv, a, m).start()

        self.send, self.recv, self.bufs, self.token = _split_start(self.name + "_ici_start", 3 * n, sums + land, issue)
        return self.token

    def wait(self, after):
        n = self.n

        def finish(refs, send, recv):
            for a in range(n):
                for m in (1, 2, 3):
                    self._hop(refs, send, recv, a, m).wait()

        bufs = _split_wait(self.name + "_ici_wait", (self.send, self.recv), self.bufs, after, finish)
        return list(zip(self.own, bufs[n:]))


def _pair_sum(name, g4, land, ids):
    rows, cols = g4.shape[2], g4.shape[3]
    tr = rows
    while tr * cols * 2 > 2 * 1024 * 1024 and tr % 32 == 0:
        tr //= 2

    def body(ids_ref, g_ref, l_ref, own_ref, sum_ref):
        h = g_ref[...].astype(F32) + l_ref[...].astype(F32)
        sum_ref[...] = h.astype(sum_ref.dtype)

        @pl.when(pl.program_id(1) == ids_ref[1])
        def _():
            own_ref[...] = h

    return pl.pallas_call(
        body, name=name,
        grid_spec=pltpu.PrefetchScalarGridSpec(
            num_scalar_prefetch=1, grid=(rows // tr, 4),
            in_specs=[_bs((None, None, tr, cols), lambda i, q, ids: (q, ids[0], i, 0)),
                      _bs((None, None, tr, cols), lambda i, q, ids: (q, 0, i, 0))],
            out_specs=[_bs((tr, cols), lambda i, q, ids: (i, 0)), _bs((None, tr, cols), lambda i, q, ids: (q, i, 0))]),
        out_shape=[jax.ShapeDtypeStruct((rows, cols), F32), jax.ShapeDtypeStruct((4, rows, cols), g4.dtype)],
        compiler_params=_params(2),
    )(ids, g4, land)


def _win_sum(ext, w, off):
    s = ext + _shift(ext, -1)
    if w >= 4:
        s = _shift(s, -1) + _shift(s, 1)
    if w >= 8:
        s = _shift(s, -2) + _shift(s, 2)
    if w >= 16:
        s = _shift(s, -4) + _shift(s, 4)
    return _shift(s, off) if off else s


def _inv_count(r0, t, w, seq):
    pos = r0 + lax.broadcasted_iota(jnp.int32, (t, 1), 0)
    cnt = jnp.minimum(pos + w // 2, seq) - jnp.maximum(pos - w // 2, 0)
    return 1.0 / cnt.astype(F32)


def _pool_fwd(p3, w_pool, pool_scale, seq, d_model):
    dp = d_model // 2
    pg = dp // len(POOL_WINDOWS)
    t = min(POOL_CHUNK, seq)
    n_chunks = seq // t
    h = WIN_HALO

    def body(u_ref, w_ref, sc_ref, d_ref, y_ref, pad_ref):
        g = pl.program_id(0)
        zeros = jnp.zeros((h, pg), F32)
        pad_ref[0:h, :] = zeros
        pad_ref[h + seq:h + seq + h, :] = zeros

        def fill(ci, _):
            r0 = pl.multiple_of(ci * t, t)
            pad_ref[pl.ds(h + r0, t), :] = u_ref[pl.ds(r0, t), :]
            return 0

        lax.fori_loop(0, n_chunks, fill, 0)
        wmat = w_ref[...]
        scale = sc_ref[...]
        for gi, w in enumerate(POOL_WINDOWS):
            @pl.when(g == gi)
            def _(w=w):
                def chunk(ci, _):
                    r0 = pl.multiple_of(ci * t, t)
                    ext = pad_ref[pl.ds(r0, t + 2 * h), :]
                    mean = _win_sum(ext, w, 0)[h:h + t, :] * _inv_count(r0, t, w, seq)
                    d = (mean - ext[h:h + t, :]).astype(BF16)
                    d_ref[pl.ds(r0, t), :] = d
                    q = jnp.dot(d, wmat, preferred_element_type=F32)
                    y_ref[pl.ds(r0, t), :] = (q * scale).astype(BF16)
                    return 0

                lax.fori_loop(0, n_chunks, chunk, 0, unroll=2)

    return pl.pallas_call(
        body, name="pool_fwd", grid=(len(POOL_WINDOWS),),
        in_specs=[_bs((None, seq, pg), lambda g: (0, 0, g)), _bs((None, pg, pg), lambda g: (g, 0, 0)),
                  _bs((1, pg), lambda g: (0, g))],
        out_specs=[_bs((seq, pg), lambda g: (0, g)), _bs((seq, pg), lambda g: (0, g))],
        out_shape=[jax.ShapeDtypeStruct((seq, dp), BF16), jax.ShapeDtypeStruct((seq, d_model), BF16)],
        scratch_shapes=[pltpu.VMEM((seq + 2 * h, pg), F32)],
        compiler_params=_params(1),
    )(p3, w_pool, pool_scale)


def _pool_bwd(d, dy, w_pool, pool_scale, token, seq, d_model):
    dp = d_model // 2
    pg = dp // len(POOL_WINDOWS)
    t = min(POOL_CHUNK, seq)
    n_chunks = seq // t
    h = WIN_HALO
    tn_dims = (((0,), (0,)), ((), ()))
    nt_dims = (((1,), (1,)), ((), ()))

    def body(d_ref, dy_ref, w_ref, sc_ref, tok_ref, du_ref, dwb_ref, dsc_ref, pad_ref, dd_ref, dw_ref):
        del tok_ref
        g = pl.program_id(0)
        zeros = jnp.zeros((h, pg), F32)
        pad_ref[0:h, :] = zeros
        pad_ref[h + seq:h + seq + h, :] = zeros
        wmat = w_ref[...]
        scale = sc_ref[...]
        for gi, w in enumerate(POOL_WINDOWS):
            @pl.when(g == gi)
            def _(w=w):
                dw_ref[...] = jnp.zeros((pg, pg), F32)

                def first(ci, dsc):
                    r0 = pl.multiple_of(ci * t, t)
                    dv = d_ref[pl.ds(r0, t), :]
                    dyv = dy_ref[pl.ds(r0, t), :]
                    q = jnp.dot(dv, wmat, preferred_element_type=F32)
                    dsc = dsc + jnp.sum(dyv * q, axis=0, keepdims=True)
                    dq = (dyv * scale).astype(BF16)
                    dw_ref[...] += lax.dot_general(dv, dq, tn_dims, preferred_element_type=F32)
                    dd = lax.dot_general(dq, wmat, nt_dims, preferred_element_type=F32)
                    dd_ref[pl.ds(r0, t), :] = dd
                    pad_ref[pl.ds(h + r0, t), :] = dd * _inv_count(r0, t, w, seq)
                    return dsc

                def first_pair(cj, dsc):
                    return first(2 * cj + 1, first(2 * cj, dsc))

                dsc_ref[...] = lax.fori_loop(0, n_chunks // 2, first_pair, jnp.zeros((1, pg), F32))
                dwb_ref[...] = dw_ref[...].reshape(N_DEV, pg // N_DEV, pg).astype(BF16)

                def second(ci, _):
                    r0 = pl.multiple_of(ci * t, t)
                    ext = pad_ref[pl.ds(r0, t + 2 * h), :]
                    back = _win_sum(ext, w, 1)[h:h + t, :]
                    du_ref[pl.ds(r0, t), :] = (back - dd_ref[pl.ds(r0, t), :]).astype(BF16)
                    return 0

                lax.fori_loop(0, n_chunks, second, 0, unroll=2)

    return pl.pallas_call(
        body, name="pool_bwd", grid=(len(POOL_WINDOWS),),
        in_specs=[_bs((seq, pg), lambda g: (0, g)), _bs((seq, pg), lambda g: (0, g)),
                  _bs((None, pg, pg), lambda g: (g, 0, 0)), _bs((1, pg), lambda g: (0, g)),
                  _bs((8, 128), lambda g: (0, 0))],
        out_specs=[_bs((seq, pg), lambda g: (0, g)), _bs((N_DEV, None, pg // N_DEV, pg), lambda g: (0, g, 0, 0)),
                   _bs((1, pg), lambda g: (0, g))],
        out_shape=[jax.ShapeDtypeStruct((seq, 3 * dp), BF16),
                   jax.ShapeDtypeStruct((N_DEV, len(POOL_WINDOWS), pg // N_DEV, pg), BF16),
                   jax.ShapeDtypeStruct((1, dp), F32)],
        scratch_shapes=[pltpu.VMEM((seq + 2 * h, pg), F32), pltpu.VMEM((seq, pg), F32), pltpu.VMEM((pg, pg), F32)],
        compiler_params=_params(1),
    )(d, dy, w_pool, pool_scale, token)


def _tile_scan(n_tiles, lanes, loads, stores):
    row = lax.broadcasted_iota(jnp.int32, (8, lanes), 0)
    group = 8

    def local_scan(n, k):
        aa, bb = loads[n](k)
        for sh in (1, 2, 4):
            if n == 0:
                ok = row >= sh
                ap = jnp.where(ok, pltpu.roll(aa, sh, 0), 1.0)
                bp = jnp.where(ok, pltpu.roll(bb, sh, 0), 0.0)
            else:
                ok = row < 8 - sh
                ap = jnp.where(ok, pltpu.roll(aa, 8 - sh, 0), 1.0)
                bp = jnp.where(ok, pltpu.roll(bb, 8 - sh, 0), 0.0)
            bb = aa * bp + bb
            aa = aa * ap
        return aa, bb

    def step(s, carry):
        carry = list(carry)
        for n in range(2):
            tiles = [s * group + u if n == 0 else n_tiles - 1 - (s * group + u) for u in range(group)]
            local = [local_scan(n, k) for k in tiles]
            for k, (aa, bb) in zip(tiles, local):
                hh = bb + aa * carry[n]
                stores[n](k, hh)
                carry[n] = jnp.broadcast_to(hh[7:8, :] if n == 0 else hh[0:1, :], (8, lanes))
        return tuple(carry)

    zeros = jnp.zeros((8, lanes), F32)
    lax.fori_loop(0, n_tiles // group, step, (zeros, zeros))


def _gate_preacts(xc, wcat_ref):
    xcb = xc.astype(BF16)
    return xcb, jnp.dot(xcb, wcat_ref[...], preferred_element_type=F32)


def _gates(pre, n, pk_ref, sp):
    lh = pre.shape[1] // 4
    r = _sigmoid(pre[:, (2 * n) * lh:(2 * n + 1) * lh] + pk_ref[pl.ds(4 + n, 1), :])
    i = _sigmoid(pre[:, (2 * n + 1) * lh:(2 * n + 2) * lh] + pk_ref[pl.ds(6 + n, 1), :])
    log_a = (-RG_C * r) * sp[n]
    a = jnp.exp(log_a)
    x = 2.0 * log_a
    one_minus_a2 = jnp.where(x > -0.01, -(x * (1.0 + x * (0.5 + x * (1.0 / 6.0)))), 1.0 - a * a)
    m = jnp.sqrt(one_minus_a2)
    return r, i, a, m


def _conv_chunk(upad_ref, pk_ref, cb, r0, t):
    ext = upad_ref[pl.ds(r0, t + 2 * CONV_HALO), :]
    acc = pk_ref[pl.ds(1, 1), :] * ext
    for k in (0, 2, 3):
        acc = acc + pk_ref[pl.ds(k, 1), :] * _shift(ext, k - 1)
    return acc[CONV_HALO:CONV_HALO + t, :] + cb, ext


def _lru_fwd(p3, y_in, pack, conv_b, wcat, token, seq, d_model):
    dl = d_model // 2
    lh = dl // N_HEADS
    t = min(SEQ_CHUNK, seq)
    n_chunks = seq // t
    hal = CONV_HALO
    first_rec_block = (d_model - dl) // lh

    def body(ur_ref, ug_ref, pk_ref, cb_ref, wcat_ref, yin_ref, tok_ref, y_ref, h0_ref, h1_ref,
             upad, a_scr, b_scr):
        del yin_ref, tok_ref
        zeros = jnp.zeros((hal, lh), F32)
        upad[0:hal, :] = zeros
        upad[hal + seq:hal + seq + hal, :] = zeros
        for ref in (h0_ref, h1_ref):
            ref[0:hal, :] = zeros
            ref[hal + seq:hal + seq + hal, :] = zeros

        def fill(ci, _):
            r0 = pl.multiple_of(ci * t, t)
            upad[pl.ds(hal + r0, t), :] = ur_ref[pl.ds(r0, t), :]
            return 0

        lax.fori_loop(0, n_chunks, fill, 0)
        cb = cb_ref[...]
        sp = [_softplus(-pk_ref[pl.ds(8 + n, 1), :]) for n in range(2)]

        def chunk(ci, _):
            r0 = pl.multiple_of(ci * t, t)
            xc, _ext = _conv_chunk(upad, pk_ref, cb, r0, t)
            _, pre = _gate_preacts(xc, wcat_ref)
            for n in range(2):
                _, i, a, m = _gates(pre, n, pk_ref, sp)
                a_scr[n, pl.ds(r0, t), :] = a
                b_scr[n, pl.ds(r0, t), :] = (m * i) * xc
            return 0

        lax.fori_loop(0, n_chunks, chunk, 0, unroll=2)

        def load(n):
            def get(k):
                at = pl.ds(pl.multiple_of(k * 8, 8), 8)
                return a_scr[n, at, :], b_scr[n, at, :]
            return get

        def store(ref):
            def put(k, v):
                ref[pl.ds(pl.multiple_of(hal + k * 8, 8), 8), :] = v
            return put

        _tile_scan(seq // 8, lh, [load(0), load(1)], [store(h0_ref), store(h1_ref)])

        def out(ci, _):
            r0 = pl.multiple_of(ci * t, t)
            hsum = h0_ref[pl.ds(hal + r0, t), :] + h1_ref[pl.ds(hal + r0, t), :]
            gl, _dg = _gelu_and_grad(ug_ref[pl.ds(r0, t), :])
            y_ref[pl.ds(r0, t), :] = (hsum * gl).astype(BF16)
            return 0

        lax.fori_loop(0, n_chunks, out, 0)

    return pl.pallas_call(
        body, name="lru_fwd", grid=(N_HEADS,),
        in_specs=[_bs((None, seq, lh), lambda h: (1, 0, h)), _bs((None, seq, lh), lambda h: (2, 0, h)),
                  _bs((None, SMALL_ROWS, lh), lambda h: (h, 0, 0)), _bs((1, lh), lambda h: (0, h)),
                  _bs((None, lh, 4 * lh), lambda h: (h, 0, 0)),
                  ANY, _bs((8, 128), lambda h: (0, 0))],
        out_specs=[_bs((seq, lh), lambda h: (0, first_rec_block + h)),
                   _bs((seq + 2 * hal, lh), lambda h: (0, h)), _bs((seq + 2 * hal, lh), lambda h: (0, h))],
        out_shape=[jax.ShapeDtypeStruct((seq, d_model), BF16), jax.ShapeDtypeStruct((seq + 2 * hal, dl), F32),
                   jax.ShapeDtypeStruct((seq + 2 * hal, dl), F32)],
        scratch_shapes=[pltpu.VMEM((seq + 2 * hal, lh), F32), pltpu.VMEM((2, seq, lh), F32),
                        pltpu.VMEM((2, seq, lh), F32)],
        input_output_aliases={5: 0},
        compiler_params=_params(1),
    )(p3, p3, pack, conv_b, wcat, y_in, token)


def _lru_bwd(p3, dy, h0p, h1p, dproj_in, pack, conv_b, wcat, token, seq, d_model):
    dl = d_model // 2
    lh = dl // N_HEADS
    t = min(SEQ_CHUNK, seq)
    n_chunks = seq // t
    hal = CONV_HALO
    first_rec_block = (d_model - dl) // lh
    tn_dims = (((0,), (0,)), ((), ()))
    nt_dims = (((1,), (1,)), ((), ()))

    def body(ur_ref, ug_ref, dy_ref, h0_ref, h1_ref, pk_ref, cb_ref, wcat_ref, tok_ref, din_ref,
             dproj_ref, dpk_ref, dcb_ref, dwcat_ref,
             upad, a_scr, dh_scr, g_scr, dxc_pad, dpr_ref, out_sems, gate_scr):
        del din_ref, tok_ref
        head = pl.program_id(0)
        slot = head % 2
        zeros = jnp.zeros((hal, lh), F32)
        for ref in (upad, dxc_pad):
            ref[0:hal, :] = zeros
            ref[hal + seq:hal + seq + hal, :] = zeros
        for n in range(2):
            a_scr[n, 0:hal, :] = zeros
            a_scr[n, hal + seq:hal + seq + hal, :] = zeros

        def fill(ci, _):
            r0 = pl.multiple_of(ci * t, t)
            upad[pl.ds(hal + r0, t), :] = ur_ref[pl.ds(r0, t), :]
            return 0

        lax.fori_loop(0, n_chunks, fill, 0)
        cb = cb_ref[...]
        lam = [pk_ref[pl.ds(8 + n, 1), :] for n in range(2)]
        sp = [_softplus(-lam[n]) for n in range(2)]

        def chunk1(ci, _):
            r0 = pl.multiple_of(ci * t, t)
            xc, _ext = _conv_chunk(upad, pk_ref, cb, r0, t)
            _, pre = _gate_preacts(xc, wcat_ref)
            for n in range(2):
                r, i, a, m = _gates(pre, n, pk_ref, sp)
                a_scr[n, pl.ds(hal + r0, t), :] = a
                for q, v in enumerate((r, i, m)):
                    gate_scr[3 * n + q, pl.ds(r0, t), :] = v
            hsum = h0_ref[pl.ds(hal + r0, t), :] + h1_ref[pl.ds(hal + r0, t), :]
            gl, dgl = _gelu_and_grad(ug_ref[pl.ds(r0, t), :])
            dyv = dy_ref[pl.ds(r0, t), :]
            dh_scr[pl.ds(r0, t), :] = dyv * gl
            dpr_ref[slot, 1, pl.ds(r0, t), :] = ((dyv * hsum) * dgl).astype(BF16)
            return 0

        lax.fori_loop(0, n_chunks, chunk1, 0, unroll=2)

        def load(n):
            def get(k):
                r0 = pl.multiple_of(k * 8, 8)
                if n == 0:
                    coef = _shift(a_scr[0, pl.ds(pl.multiple_of(hal + r0, 8), 16), :], 1)[0:8, :]
                else:
                    coef = _shift(a_scr[1, pl.ds(pl.multiple_of(hal + r0 - 8, 8), 16), :], -1)[8:16, :]
                return coef, dh_scr[pl.ds(r0, 8), :]
            return get

        def store(n):
            def put(k, v):
                g_scr[n, pl.ds(pl.multiple_of(k * 8, 8), 8), :] = v
            return put

        _tile_scan(seq // 8, lh, [load(1), load(0)], [store(1), store(0)])

        dwcat_ref[...] = jnp.zeros((lh, 4 * lh), F32)

        def chunk3(ci, carry):
            dba, dbi, dlam, dcb = carry
            r0 = pl.multiple_of(ci * t, t)
            xc, _ext = _conv_chunk(upad, pk_ref, cb, r0, t)
            xcb = xc.astype(BF16)
            dxc = jnp.zeros((t, lh), F32)
            dba, dbi, dlam = list(dba), list(dbi), list(dlam)
            dpre = []
            for n in range(2):
                r, i, m = (gate_scr[3 * n + q, pl.ds(r0, t), :] for q in range(3))
                a = a_scr[n, pl.ds(hal + r0, t), :]
                hext = (h0_ref if n == 0 else h1_ref)[pl.ds(r0, t + 2 * hal), :]
                hprev = _shift(hext, -1 if n == 0 else 1)[hal:hal + t, :]
                gb = g_scr[n, pl.ds(r0, t), :]
                da = gb * hprev
                dm = gb * i * xc
                di = gb * m * xc
                dxc = dxc + gb * (m * i)
                dlog_a = da * a - dm * (a * a) / m
                dr = dlog_a * (-RG_C * sp[n])
                dlam[n] = dlam[n] + jnp.sum(dlog_a * r, axis=0, keepdims=True)
                dpr = dr * r * (1.0 - r)
                dpi = di * i * (1.0 - i)
                dba[n] = dba[n] + jnp.sum(dpr, axis=0, keepdims=True)
                dbi[n] = dbi[n] + jnp.sum(dpi, axis=0, keepdims=True)
                dpre += [dpr.astype(BF16), dpi.astype(BF16)]
            dpre = jnp.concatenate(dpre, axis=1)
            dwcat_ref[...] += lax.dot_general(xcb, dpre, tn_dims, preferred_element_type=F32)
            dxc = dxc + lax.dot_general(dpre, wcat_ref[...], nt_dims, preferred_element_type=F32)
            dxc_pad[pl.ds(hal + r0, t), :] = dxc
            dcb = dcb + jnp.sum(dxc, axis=0, keepdims=True)
            return tuple(dba), tuple(dbi), tuple(dlam), dcb

        zr = jnp.zeros((1, lh), F32)
        def chunk3_pair(cj, carry):
            return chunk3(2 * cj + 1, chunk3(2 * cj, carry))

        dba, dbi, dlam, dcb = lax.fori_loop(0, n_chunks // 2, chunk3_pair, ((zr, zr), (zr, zr), (zr, zr), zr))
        dcb_ref[...] = dcb
        for n in range(2):
            dpk_ref[pl.ds(4 + n, 1), :] = dba[n]
            dpk_ref[pl.ds(6 + n, 1), :] = dbi[n]
            dpk_ref[pl.ds(8 + n, 1), :] = dlam[n] * (RG_C * jax.nn.sigmoid(-lam[n]))
        dpk_ref[pl.ds(10, SMALL_ROWS - 10), :] = jnp.zeros((SMALL_ROWS - 10, lh), F32)

        def chunk4(ci, dtap):
            r0 = pl.multiple_of(ci * t, t)
            gext = dxc_pad[pl.ds(r0, t + 2 * hal), :]
            uext = upad[pl.ds(r0, t + 2 * hal), :]
            gmid = gext[hal:hal + t, :]
            du = pk_ref[pl.ds(1, 1), :] * gext
            for k in (0, 2, 3):
                du = du + pk_ref[pl.ds(k, 1), :] * _shift(gext, 1 - k)
            dpr_ref[slot, 0, pl.ds(r0, t), :] = du[hal:hal + t, :].astype(BF16)
            out = []
            for k in range(4):
                usl = _shift(uext, k - 1)[hal:hal + t, :]
                out.append(dtap[k] + jnp.sum(gmid * usl, axis=0, keepdims=True))
            return tuple(out)

        dtap = lax.fori_loop(0, n_chunks, chunk4, (zr, zr, zr, zr))
        for k in range(4):
            dpk_ref[pl.ds(k, 1), :] = dtap[k]

        def out_copy(s, b):
            return pltpu.make_async_copy(
                dpr_ref.at[s, b], dproj_ref.at[:, pl.ds(pl.multiple_of((1 + b) * dl + head * lh, lh), lh)],
                out_sems.at[s, b])

        for b in range(2):
            out_copy(slot, b).start()

        @pl.when(head > 0)
        def _():
            for b in range(2):
                out_copy(1 - slot, b).wait()

        @pl.when(head == N_HEADS - 1)
        def _():
            for b in range(2):
                out_copy(slot, b).wait()

    return pl.pallas_call(
        body, name="lru_bwd", grid=(N_HEADS,),
        in_specs=[_bs((None, seq, lh), lambda h: (1, 0, h)), _bs((None, seq, lh), lambda h: (2, 0, h)),
                  _bs((seq, lh), lambda h: (0, first_rec_block + h)),
                  _bs((seq + 2 * hal, lh), lambda h: (0, h)), _bs((seq + 2 * hal, lh), lambda h: (0, h)),
                  _bs((None, SMALL_ROWS, lh), lambda h: (h, 0, 0)), _bs((1, lh), lambda h: (0, h)),
                  _bs((None, lh, 4 * lh), lambda h: (h, 0, 0)),
                  _bs((8, 128), lambda h: (0, 0)), ANY],
        out_specs=[ANY, _bs((None, SMALL_ROWS, lh), lambda h: (h, 0, 0)),
                   _bs((1, lh), lambda h: (0, h)), _bs((None, lh, 4 * lh), lambda h: (h, 0, 0))],
        out_shape=[jax.ShapeDtypeStruct((seq, 3 * dl), BF16), jax.ShapeDtypeStruct((N_HEADS, SMALL_ROWS, lh), F32),
                   jax.ShapeDtypeStruct((1, dl), F32), jax.ShapeDtypeStruct((N_HEADS, lh, 4 * lh), F32)],
        scratch_shapes=[pltpu.VMEM((seq + 2 * hal, lh), F32), pltpu.VMEM((2, seq + 2 * hal, lh), F32),
                        pltpu.VMEM((seq, lh), F32), pltpu.VMEM((2, seq, lh), F32),
                        pltpu.VMEM((seq + 2 * hal, lh), F32), pltpu.VMEM((2, 2, seq, lh), BF16),
                        pltpu.SemaphoreType.DMA((2, 2)), pltpu.VMEM((6, seq, lh), F32)],
        input_output_aliases={9: 0},
        compiler_params=_params(1),
    )(p3, p3, dy, h0p, h1p, pack, conv_b, wcat, token, dproj_in)


class _tiles:
    def __init__(self, seq, d_model, d_ff):
        self.rows = min(1024, seq)
        self.ln_rows = min(256, seq)
        self.ff_cols = min(2048, d_ff)
        self.ff_split = 4
        self.ff_k = min(2048, d_ff)
        self.grad_rows = 512


def _ln_loss_bwd(ffn, x1, tgt, g, b, tr):
    seq, d = ffn.shape

    def body(f_ref, x_ref, t_ref, g_ref, b_ref, dz_ref, dzb_ref, dg_ref, db_ref, loss_ref):
        i = pl.program_id(0)
        gv = g_ref[...]
        z = ALPHA * x_ref[...] + f_ref[...]
        y, xhat, rstd = _ln_fwd(z, gv, b_ref[...])
        err = y - t_ref[...]
        part = 0.5 * jnp.sum(jnp.mean(err * err, axis=-1, keepdims=True), axis=0, keepdims=True)
        dz, dg, db = _ln_bwd(err * (1.0 / d), xhat, rstd, gv)
        dz_ref[...] = dz
        dzb_ref[...] = dz.astype(BF16)
        _acc_rows(dg_ref, i == 0, dg)
        _acc_rows(db_ref, i == 0, db)
        _acc_rows(loss_ref, i == 0, jnp.broadcast_to(part, (8, 128)))

    row = _bs((tr, d), lambda i: (i, 0))
    vec = _bs((1, d), lambda i: (0, 0))
    return pl.pallas_call(
        body, name="ln_ffn_loss", grid=(seq // tr,), in_specs=[row, row, row, vec, vec],
        out_specs=[row, row, vec, vec, _bs((8, 128), lambda i: (0, 0))],
        out_shape=[jax.ShapeDtypeStruct((seq, d), F32), jax.ShapeDtypeStruct((seq, d), BF16),
                   jax.ShapeDtypeStruct((1, d), F32), jax.ShapeDtypeStruct((1, d), F32),
                   jax.ShapeDtypeStruct((8, 128), F32)],
        compiler_params=_params(1),
    )(ffn, x1, tgt, g, b)


def _ln_bwd_side(dx_branch, dres, z, g, b, n_steps):
    seq, d = z.shape
    tr = seq // n_steps

    def fn(step, ins, outs):
        a_ref, r_ref, z_ref, g_ref, b_ref = ins
        dz_ref, dzb_ref, dg_ref, db_ref = outs
        gv = g_ref[...]
        _, xhat, rstd = _ln_fwd(z_ref[...], gv, b_ref[...])
        dz, dg, db = _ln_bwd(ALPHA * r_ref[...] + a_ref[...], xhat, rstd, gv)
        dz_ref[...] = dz
        dzb_ref[...] = dz.astype(BF16)
        _acc_rows(dg_ref, step == 0, dg)
        _acc_rows(db_ref, step == 0, db)

    row = ((tr, d), lambda s: (s, 0))
    vec = ((1, d), lambda s: (0, 0))
    shapes = [jax.ShapeDtypeStruct((seq, d), F32), jax.ShapeDtypeStruct((seq, d), BF16),
              jax.ShapeDtypeStruct((1, d), F32), jax.ShapeDtypeStruct((1, d), F32)]
    return [(dx_branch, *row), (dres, *row), (z, *row), (g, *vec), (b, *vec)], shapes, [row, row, vec, vec], fn


def _to_bf16(name, a, token):
    rows, cols = a.shape
    tr = min(512, rows)

    def body(a_ref, tok_ref, o_ref):
        del tok_ref
        o_ref[...] = a_ref[...].astype(BF16)

    return pl.pallas_call(
        body, name=name, grid=(rows // tr,),
        in_specs=[_bs((tr, cols), lambda i: (i, 0)), _bs((8, 128), lambda i: (0, 0))],
        out_specs=_bs((tr, cols), lambda i: (i, 0)), out_shape=jax.ShapeDtypeStruct((rows, cols), BF16),
        compiler_params=_params(1),
    )(a, token)


def _sum_blocks(name, parts):
    def body(p_ref, o_ref):
        acc = p_ref[0]
        for s in range(1, parts.shape[0]):
            acc = acc + p_ref[s]
        o_ref[...] = acc

    return pl.pallas_call(body, name=name, out_shape=jax.ShapeDtypeStruct(parts.shape[1:], F32))(parts)


def _adamw_values(w, g, m, v):
    m = ADAM_B1 * m + (1.0 - ADAM_B1) * g
    v = ADAM_B2 * v + (1.0 - ADAM_B2) * (g * g)
    m_hat = m / (1.0 - ADAM_B1 ** ADAM_STEP)
    v_hat = v / (1.0 - ADAM_B2 ** ADAM_STEP)
    delta = -ADAM_LR * (m_hat / (jnp.sqrt(v_hat) + ADAM_EPS) + ADAM_WD * w)
    return delta, m, v


def _adamw_side(own, parts, w, m, v, n_steps):
    rows, cols = w.shape
    tr = rows // n_steps

    def fn(step, ins, outs):
        o_ref, p_ref, w_ref, m_ref, v_ref = ins
        g = o_ref[...]
        for s in range(parts.shape[0]):
            g = g + p_ref[s].astype(F32)
        delta, mn, vn = _adamw_values(w_ref[...], g, m_ref[...], v_ref[...])
        for ref, val in zip(outs, (g, delta, mn, vn)):
            ref[...] = val

    row = ((tr, cols), lambda s: (s, 0))
    stack = ((parts.shape[0], tr, cols), lambda s: (0, s, 0))
    shapes = [jax.ShapeDtypeStruct((rows, cols), F32)] * 4
    return [(own, *row), (parts, *stack), (w, *row), (m, *row), (v, *row)], shapes, [row] * 4, fn


def _sum_adamw(name, own, parts, w, m, v):
    rows, cols = w.shape
    n_parts = parts.shape[0]
    tr = rows
    min_rows = 8 if parts.dtype == F32 else 16
    while tr * cols * 4 > 2 * 1024 * 1024 and tr % (2 * min_rows) == 0:
        tr //= 2

    def body(*refs):
        if own is None:
            p_ref, w_ref, m_ref, v_ref, g_ref, d_ref, mo_ref, vo_ref = refs
            g = p_ref[0].astype(F32)
            rest = range(1, n_parts)
        else:
            o_ref, p_ref, w_ref, m_ref, v_ref, g_ref, d_ref, mo_ref, vo_ref = refs
            g = o_ref[...]
            rest = range(n_parts)
        for s in rest:
            g = g + p_ref[s].astype(F32)
        delta, mn, vn = _adamw_values(w_ref[...], g, m_ref[...], v_ref[...])
        g_ref[...] = g
        d_ref[...] = delta
        mo_ref[...] = mn
        vo_ref[...] = vn

    spec = _bs((tr, cols), lambda i: (i, 0))
    lead = [] if own is None else [own]
    return pl.pallas_call(
        body, name=name, grid=(rows // tr,),
        in_specs=[spec] * len(lead) + [_bs((n_parts, tr, cols), lambda i: (0, i, 0)), spec, spec, spec],
        out_specs=[spec] * 4, out_shape=[jax.ShapeDtypeStruct((rows, cols), F32)] * 4,
        compiler_params=_params(1),
    )(*lead, parts, w, m, v)


def _rows128(a):
    return a.reshape(-1, 128)


def kernel(x, ln_mix_g, ln_mix_b, w_in, w_pool, pool_scale, conv_w, conv_b, w_rg_a, b_rg_a, w_rg_i, b_rg_i, rg_lambda, w_out, ln_ffn_g, ln_ffn_b, w_mlp_in, w_mlp_out, loss_target, m_ln_mix_g, m_ln_mix_b, m_w_in, m_w_pool, m_pool_scale, m_conv_w, m_conv_b, m_w_rg_a, m_b_rg_a, m_w_rg_i, m_b_rg_i, m_rg_lambda, m_w_out, m_ln_ffn_g, m_ln_ffn_b, m_w_mlp_in, m_w_mlp_out, v_ln_mix_g, v_ln_mix_b, v_w_in, v_w_pool, v_pool_scale, v_conv_w, v_conv_b, v_w_rg_a, v_b_rg_a, v_w_rg_i, v_b_rg_i, v_rg_lambda, v_w_out, v_ln_ffn_g, v_ln_ffn_b, v_w_mlp_in, v_w_mlp_out):
    seq, d_model = x.shape[1], x.shape[2]
    dh = d_model // 2
    lh = dh // N_HEADS
    pg = dh // len(POOL_WINDOWS)
    d_ff = w_mlp_in.shape[2] * N_DEV
    assert lh == 128 and conv_w.shape[3] == lh and w_pool.shape[2] * N_DEV == pg

    xs = x[0]
    tgt = loss_target[0]

    def small_pack(cw, ba, bi, lam):
        return jnp.concatenate([cw.reshape(4, lh), ba.reshape(2, lh), bi.reshape(2, lh), lam.reshape(2, lh),
                                jnp.zeros((SMALL_ROWS - 10, lh), F32)], axis=0)

    pack_mine = small_pack(conv_w, b_rg_a, b_rg_i, rg_lambda)
    pack_bits = lax.bitcast_convert_type(pack_mine, BF16).reshape(1, SMALL_ROWS, 2 * lh)
    win_gather = _SplitGather("gather_w_in", [(w_in[0], 1), (w_pool[0], 1), (pack_bits, 0)], BF16, after=pack_mine)
    wout_gather = _SplitGather("gather_w_out", [(w_out[0], 0)], BF16, after=win_gather.token)
    w1_gather = _SplitGather("gather_w_mlp_in", [(w_mlp_in[0], 1)], BF16, after=wout_gather.token)
    w2_gather = _SplitGather("gather_w_mlp_out", [(w_mlp_out[0], 0)], BF16, after=w1_gather.token)
    xb = _to_bf16("x_bf16", x[0], w2_gather.token)
    win_full, wpool_full, pack_bits_full = win_gather.wait(after=win_gather.relay(after=xb))
    pack_full = lax.bitcast_convert_type(pack_bits_full.reshape(N_DEV, SMALL_ROWS, lh, 2), F32)
    wcat = jnp.concatenate([w_rg_a[0, 0], w_rg_i[0, 0], w_rg_a[0, 1], w_rg_i[0, 1]], axis=-1).astype(BF16)
    vec = lambda i, j, k: (0, 0)
    row_full = lambda i, j, k: (i, 0)

    def after(token):
        return (token, _sp((8, 128), vec))

    def sds(shape, dtype):
        return jax.ShapeDtypeStruct(shape, dtype)

    def plain_epi(acc, i, ex, out):
        out[0][...] = acc

    def bf16_epi(acc, i, ex, out):
        out[0][...] = acc.astype(BF16)

    t = _tiles(seq, d_model, d_ff)

    (p3,) = _matmul(
        "proj", xb, win_full, _sp((t.rows, d_model), lambda i, j, k: (i, 0)), _sp((d_model, dh), lambda i, j, k: (0, j)),
        grid=(seq // t.rows, 3, 1),
        out_shape=[sds((3, seq, dh), F32)], out_specs=[_sp((None, t.rows, dh), lambda i, j, k: (j, i, 0))],
        epilogue=plain_epi)

    d_pool, y_half = _pool_fwd(p3, wpool_full, pool_scale, seq, d_model)
    y, h0p, h1p = _lru_fwd(p3, y_half, pack_full, conv_b, wcat, wout_gather.relay(after=y_half), seq, d_model)
    (wout_full,) = wout_gather.wait(after=y)

    mix_rows = 2 * t.ln_rows

    def mix_epi(acc, i, ex, out):
        x_ref, g_ref, b_ref = ex[:3]
        for part in range(2):
            rows = pl.ds(part * t.ln_rows, t.ln_rows)
            z = ALPHA * x_ref[rows, :] + acc[part * t.ln_rows:(part + 1) * t.ln_rows, :]
            x1, _, _ = _ln_fwd(z, g_ref[...], b_ref[...])
            out[0][rows, :] = z
            out[1][rows, :] = x1
            out[2][rows, :] = x1.astype(BF16)

    z1, x1, x1b = _matmul(
        "mix_out", y, wout_full, _sp((mix_rows, d_model), row_full), _sp((d_model, d_model), vec, single=True),
        grid=(seq // mix_rows, 1, 1),
        extras=[(xs, _sp((mix_rows, d_model), row_full)), (ln_mix_g, _sp((1, d_model), vec)),
                (ln_mix_b, _sp((1, d_model), vec))],
        out_shape=[sds((seq, d_model), F32), sds((seq, d_model), F32), sds((seq, d_model), BF16)],
        out_specs=[_sp((mix_rows, d_model), row_full)] * 3, epilogue=mix_epi)
    (w1_full,) = w1_gather.wait(after=w1_gather.relay(after=x1b))

    def mlp_in_epi(acc, i, ex, out, cols):
        h = jnp.maximum(acc, 0.0)
        out[0][:, cols] = (h * h).astype(BF16)
        out[1][:, cols] = (2.0 * h).astype(BF16)

    hmid, dact = _matmul(
        "mlp_in", x1b, w1_full, _sp((t.rows, d_model), lambda i, j, k: (i, 0)),
        _sp((d_model, t.ff_cols), lambda i, j, k: (0, j)),
        grid=(seq // t.rows, d_ff // t.ff_cols, 1), j_outer=True,
        out_shape=[sds((seq, d_ff), BF16)] * 2, out_specs=[_sp((t.rows, t.ff_cols), lambda i, j, k: (i, j))] * 2,
        epilogue=mlp_in_epi, n_split=t.ff_split)
    (w2_full,) = w2_gather.wait(after=w2_gather.relay(after=hmid))

    (ffn,) = _matmul(
        "mlp_out", hmid, w2_full, _sp((t.rows, t.ff_k), lambda i, j, k: (i, k)),
        _sp((t.ff_k, d_model), lambda i, j, k: (k, 0)),
        grid=(seq // t.rows, 1, d_ff // t.ff_k),
        out_shape=[sds((seq, d_model), F32)], out_specs=[_sp((t.rows, d_model), row_full)])
    dz2, dz2b, g_ffn_g, g_ffn_b, loss_part = _ln_loss_bwd(ffn, x1, tgt, ln_ffn_g, ln_ffn_b, t.ln_rows)

    (g_w2,) = _matmul(
        "grad_w_mlp_out", hmid, dz2b, _sp((seq, t.grad_rows), lambda i, j, k: (0, i)),
        _sp((seq, d_model), vec, single=True),
        grid=(d_ff // t.grad_rows, 1, 1), ta=True,
        out_shape=[sds((d_ff, d_model), BF16)], out_specs=[_sp((t.grad_rows, d_model), row_full)],
        epilogue=bf16_epi)
    scatter_w2 = _SplitReduceScatter("scatter_w_mlp_out", [g_w2.reshape(N_DEV, d_ff // N_DEV, d_model)])

    def dpre_epi(acc, i, ex, out, cols):
        out[0][:, cols] = (acc * ex[0][:, cols].astype(F32)).astype(BF16)

    (dpre,) = _matmul(
        "mlp_dpre", dz2b, w2_full, _sp((t.rows, d_model), lambda i, j, k: (i, 0)),
        _sp((t.ff_cols, d_model), lambda i, j, k: (j, 0)),
        grid=(seq // t.rows, d_ff // t.ff_cols, 1), j_outer=True, tb=True,
        extras=[(dact, _sp((t.rows, t.ff_cols), lambda i, j, k: (i, j))), after(scatter_w2.token)],
        out_shape=[sds((seq, d_ff), BF16)], out_specs=[_sp((t.rows, t.ff_cols), lambda i, j, k: (i, j))],
        epilogue=dpre_epi, n_split=t.ff_split)
    token_w2 = scatter_w2.combine_and_send(after=dpre)

    (dx1_mlp,) = _matmul(
        "mlp_dx", dpre, w1_full, _sp((t.rows, t.ff_k), lambda i, j, k: (i, k)),
        _sp((d_model, t.ff_k), lambda i, j, k: (0, k)),
        grid=(seq // t.rows, 1, d_ff // t.ff_k), tb=True, extras=[after(token_w2)],
        out_shape=[sds((seq, d_model), F32)], out_specs=[_sp((t.rows, d_model), row_full)])
    def block_epi(acc, i, ex, out):
        out[0][0] = acc.astype(BF16)

    fs = d_ff // N_DEV
    g_w1, dz1, dz1b, g_mix_g, g_mix_b = _matmul(
        "grad_w_mlp_in", x1b, dpre, _sp((seq, t.grad_rows), lambda i, j, k: (0, i)),
        _sp((seq, fs), lambda i, j, k: (0, j)),
        grid=(d_model // t.grad_rows, N_DEV, 1), j_outer=True, ta=True,
        out_shape=[sds((N_DEV, d_model, fs), BF16)],
        out_specs=[_sp((1, t.grad_rows, fs), lambda i, j, k: (j, i, 0))], epilogue=block_epi,
        side=_ln_bwd_side(dx1_mlp, dz2, z1, ln_mix_g, ln_mix_b, d_model // t.grad_rows * N_DEV))

    (dy,) = _matmul(
        "mix_dy", dz1b, wout_full, _sp((t.rows, d_model), lambda i, j, k: (i, 0)),
        _sp((d_model, d_model), vec, single=True),
        grid=(seq // t.rows, 1, 1), tb=True,
        out_shape=[sds((seq, d_model), F32)], out_specs=[_sp((t.rows, d_model), row_full)],
        epilogue=plain_epi)
    (g_wout,) = _matmul(
        "grad_w_out", y, dz1b, _sp((seq, t.grad_rows), lambda i, j, k: (0, i)), _sp((seq, d_model), vec, single=True),
        grid=(d_model // t.grad_rows, 1, 1), ta=True,
        out_shape=[sds((d_model, d_model), BF16)], out_specs=[_sp((t.grad_rows, d_model), row_full)],
        epilogue=bf16_epi)
    scatter_w1 = _SplitReduceScatter("scatter_w_mlp_in", [g_w1, g_wout.reshape(N_DEV, d_model // N_DEV, d_model)])

    dproj_pool, g_wpool, g_pscale = _pool_bwd(d_pool, dy, wpool_full, pool_scale, scatter_w1.token, seq, d_model)
    token_w1 = scatter_w1.combine_and_send(after=dproj_pool)
    dproj, g_pack, g_convb, g_wcat = _lru_bwd(p3, dy, h0p, h1p, dproj_pool, pack_full, conv_b, wcat,
                                              token_w1, seq, d_model)
    g_wa = jnp.stack([g_wcat[:, :, 0:lh], g_wcat[:, :, 2 * lh:3 * lh]])
    g_wi = jnp.stack([g_wcat[:, :, lh:2 * lh], g_wcat[:, :, 3 * lh:4 * lh]])

    rep_parts = [_rows128(g_wa), _rows128(g_wi), _rows128(g_mix_g), _rows128(g_mix_b), _rows128(g_ffn_g),
                 _rows128(g_ffn_b), _rows128(g_pscale), _rows128(g_convb)]
    rep_rows = [p.shape[0] for p in rep_parts]
    n_rep = sum(rep_rows)
    small = jnp.concatenate(rep_parts + [_rows128(g_pack), loss_part], axis=0)
    small_gather = _SplitGather("gather_small_grads", [(small[None], 0)], F32, after=small)

    ws = 3 * dh // N_DEV

    def pair_epi(acc, i, ex, out):
        out[0][0] = acc[:, :ws].astype(BF16)
        out[0][1] = acc[:, ws:].astype(BF16)

    def adam_big(name, own_landed, w, m, v):
        own, landed = own_landed
        shp = w.shape
        two = lambda a: a.reshape(-1, shp[-1])
        res = _sum_adamw(name, own, landed, two(w), two(m), two(v))
        return [r.reshape(shp) for r in res]

    (r_w2,) = scatter_w2.wait(after=small_gather.token)
    n_steps = d_model // t.grad_rows * (N_DEV // 2)
    g_win, *o_w2 = _matmul(
        "grad_w_in", xb, dproj, _sp((seq, t.grad_rows), lambda i, j, k: (0, i)),
        _sp((seq, 2 * ws), lambda i, j, k: (0, j)),
        grid=(d_model // t.grad_rows, N_DEV // 2, 1), ta=True,
        out_shape=[sds((N_DEV, d_model, ws), BF16)],
        out_specs=[_sp((2, t.grad_rows, ws), lambda i, j, k: (j, i, 0))], epilogue=pair_epi,
        side=_adamw_side(r_w2[0], r_w2[1], w_mlp_out[0], m_w_mlp_out[0], v_w_mlp_out[0], n_steps))
    o_w2 = [r.reshape(w_mlp_out.shape) for r in o_w2]
    scatter_mix = _SplitReduceScatter(
        "scatter_mixer", [g_win, g_wpool.reshape(N_DEV, pg // N_DEV * len(POOL_WINDOWS), pg)])

    r_w1, r_wout = scatter_w1.wait(after=scatter_mix.token)
    o_w1 = adam_big("adam_w_mlp_in", r_w1, w_mlp_in, m_w_mlp_in, v_w_mlp_in)
    token_mix = small_gather.relay(after=scatter_mix.combine_and_send(after=o_w1[0]))

    def dx_epi(acc, i, ex, out):
        out[0][...] = ALPHA * ex[0][...] + acc

    dx_rows = t.ln_rows * 2
    dx, *o_wout = _matmul(
        "grad_x", dproj, win_full, _sp((dx_rows, 3 * dh), lambda i, j, k: (i, 0)),
        _sp((d_model, 3 * dh), vec, single=True),
        grid=(seq // dx_rows, 1, 1), tb=True,
        extras=[(dz1, _sp((dx_rows, d_model), row_full)), after(token_mix)],
        out_shape=[sds((seq, d_model), F32)], out_specs=[_sp((dx_rows, d_model), row_full)],
        epilogue=dx_epi,
        side=_adamw_side(r_wout[0], r_wout[1], w_out[0], m_w_out[0], v_w_out[0], seq // dx_rows))
    o_wout = [r.reshape(w_out.shape) for r in o_wout]
    r_win, r_wpool = scatter_mix.wait(after=dx)
    o_win = adam_big("adam_w_in", r_win, w_in, m_w_in, v_w_in)
    o_wpool = adam_big("adam_w_pool", r_wpool, w_pool, m_w_pool, v_w_pool)

    (small_all,) = small_gather.wait(after=o_wpool[0])

    rep_w = [w_rg_a, w_rg_i, ln_mix_g, ln_mix_b, ln_ffn_g, ln_ffn_b, pool_scale, conv_b]
    rep_m = [m_w_rg_a, m_w_rg_i, m_ln_mix_g, m_ln_mix_b, m_ln_ffn_g, m_ln_ffn_b, m_pool_scale, m_conv_b]
    rep_v = [v_w_rg_a, v_w_rg_i, v_ln_mix_g, v_ln_mix_b, v_ln_ffn_g, v_ln_ffn_b, v_pool_scale, v_conv_b]
    cat = lambda arrs: jnp.concatenate([_rows128(a) for a in arrs], axis=0)
    o_rep = _sum_adamw("adam_replicated", None, small_all, cat(rep_w), cat(rep_m), cat(rep_v))

    my_idx = _dev_index(_where_am_i())
    head_parts = lax.dynamic_slice_in_dim(small_all, n_rep + my_idx * SMALL_ROWS, SMALL_ROWS, axis=1)
    o_head = _sum_adamw("adam_head", None, head_parts, pack_mine,
                        small_pack(m_conv_w, m_b_rg_a, m_b_rg_i, m_rg_lambda),
                        small_pack(v_conv_w, v_b_rg_a, v_b_rg_i, v_rg_lambda))

    def unpack_rep(packed):
        out, r = [], 0
        for wgt, rows in zip(rep_w, rep_rows):
            out.append(packed[r:r + rows].reshape(wgt.shape))
            r += rows
        return out

    def unpack_head(packed):
        return [packed[0:4].reshape(conv_w.shape), packed[4:6].reshape(b_rg_a.shape),
                packed[6:8].reshape(b_rg_i.shape), packed[8:10].reshape(rg_lambda.shape)]

    loss = _sum_blocks("loss_sum", small_all[:, n_rep + N_HEADS * SMALL_ROWS:, :])[0, 0]

    outs = [loss, dx[None]]
    for kind in range(4):
        ra, ri, mg, mb, fg, fb, ps, cb = unpack_rep(o_rep[kind])
        cw, ba, bi, lam = unpack_head(o_head[kind])
        outs += [mg, mb, o_win[kind], o_wpool[kind], ps, cw, cb, ra, ba, ri, bi, lam, o_wout[kind], fg, fb,
                 o_w1[kind], o_w2[kind]]
    return tuple(outs)
```

```python
import functools

import jax
import jax.numpy as jnp
from jax import lax
from jax.experimental import pallas as pl
from jax.experimental.pallas import tpu as pltpu

F32 = jnp.float32
BF16 = jnp.bfloat16
MESH = pl.DeviceIdType.MESH
ANY = pl.BlockSpec(memory_space=pl.ANY)

N_DEV = 8
POOL_WINDOWS = (2, 4, 8, 16)
N_HEADS = 8
RG_C = 8.0
LN_EPS = 1e-5
ALPHA = 2.0 ** 0.25
ADAM_LR = 0.001
ADAM_B1 = 0.9
ADAM_B2 = 0.999
ADAM_EPS = 1e-08
ADAM_WD = 0.01
ADAM_STEP = 10

VMEM_LIMIT = 56 * 1024 * 1024
SEQ_CHUNK = 256
POOL_CHUNK = 512
WIN_HALO = 16
CONV_HALO = 8
SMALL_ROWS = 16


def _params(n_grid):
    return pltpu.CompilerParams(dimension_semantics=("arbitrary",) * n_grid, vmem_limit_bytes=VMEM_LIMIT)


def _shift(v, j):
    n = v.shape[0]
    s = (-j) % n
    return v if s == 0 else pltpu.roll(v, s, 0)


def _sigmoid(x):
    return 0.5 * jnp.tanh(0.5 * x) + 0.5


def _softplus(z):
    e = jnp.exp(-jnp.abs(z))
    u = 1.0 + e
    log1p = jnp.where(u == 1.0, e, jnp.log(u) * (e / jnp.where(u == 1.0, 1.0, u - 1.0)))
    return jnp.maximum(z, 0.0) + log1p


_GELU_C = 0.7978845608028654
_GELU_K = 0.044715


def _gelu_and_grad(x):
    x2 = x * x
    t = jnp.tanh(_GELU_C * (x + _GELU_K * x * x2))
    g = 0.5 * x * (1.0 + t)
    dg = 0.5 * (1.0 + t) + 0.5 * x * (1.0 - t * t) * (_GELU_C * (1.0 + 3.0 * _GELU_K * x2))
    return g, dg


def _ln_fwd(z, g, b):
    mu = jnp.mean(z, axis=-1, keepdims=True)
    zc = z - mu
    var = jnp.mean(zc * zc, axis=-1, keepdims=True)
    rstd = lax.rsqrt(var + LN_EPS)
    xhat = zc * rstd
    return xhat * g + b, xhat, rstd


def _ln_bwd(dy, xhat, rstd, g):
    dxhat = dy * g
    m1 = jnp.mean(dxhat, axis=-1, keepdims=True)
    m2 = jnp.mean(dxhat * xhat, axis=-1, keepdims=True)
    dz = rstd * (dxhat - m1 - xhat * m2)
    dg = jnp.sum(dy * xhat, axis=0, keepdims=True)
    db = jnp.sum(dy, axis=0, keepdims=True)
    return dz, dg, db


def _acc_rows(ref, first, val):
    @pl.when(first)
    def _():
        ref[...] = val

    @pl.when(jnp.logical_not(first))
    def _():
        ref[...] += val


def _sp(shape, fn, single=False):
    return shape, fn, single


def _matmul(name, a, b, a_spec, b_spec, *, grid, j_outer=False, ta=False, tb=False, extras=(), out_shape, out_specs,
            epilogue=None, n_split=1, side=None):
    ni, nj, nk = grid
    n_ex = len(extras)
    dims = (((0 if ta else 1,), (1 if tb else 0,)), ((), ()))
    side_in, side_shape, side_out, side_fn = side if side is not None else ((), (), (), None)
    n_main_out = len(out_shape)
    inner = ni if j_outer else nj

    def mk(spec):
        shape, fn, single = spec
        index = (lambda g0, g1, g2: fn(g1, g0, g2)) if j_outer else fn
        return pl.BlockSpec(shape, index, pipeline_mode=pl.Buffered(1)) if single else pl.BlockSpec(shape, index)

    def mk_side(block, fn):
        return pl.BlockSpec(block, lambda g0, g1, g2: fn(g0 * inner + g1))

    def body(a_ref, b_ref, *rest):
        ex_refs = rest[:n_ex]
        out_refs = rest[n_ex + len(side_in):n_ex + len(side_in) + n_main_out]
        if side_fn is not None:
            side_fn(pl.program_id(0) * inner + pl.program_id(1), rest[n_ex:n_ex + len(side_in)],
                    rest[n_ex + len(side_in) + n_main_out:])
        i = pl.program_id(1 if j_outer else 0)
        if n_split > 1:
            av = a_ref[...].astype(BF16)
            width = b_ref.shape[0 if tb else 1] // n_split
            for c in range(n_split):
                cols = pl.ds(c * width, width)
                bv = (b_ref[cols, :] if tb else b_ref[:, cols]).astype(BF16)
                epilogue(lax.dot_general(av, bv, dims, preferred_element_type=F32), i, ex_refs, out_refs, cols)
            return
        part = lax.dot_general(a_ref[...].astype(BF16), b_ref[...].astype(BF16), dims, preferred_element_type=F32)
        if nk == 1:
            epilogue(part, i, ex_refs, out_refs)
        else:
            @pl.when(pl.program_id(2) == 0)
            def _():
                out_refs[0][...] = part

            @pl.when(pl.program_id(2) > 0)
            def _():
                out_refs[0][...] += part

    return pl.pallas_call(
        body, name=name, grid=(nj, ni, nk) if j_outer else (ni, nj, nk),
        in_specs=[mk(a_spec), mk(b_spec)] + [mk(s) for _, s in extras] + [mk_side(blk, fn) for _, blk, fn in side_in],
        out_specs=[mk(s) for s in out_specs] + [mk_side(blk, fn) for blk, fn in side_out],
        out_shape=list(out_shape) + list(side_shape),
        compiler_params=_params(3),
    )(a, b, *[x for x, _ in extras], *[x for x, _, _ in side_in])


def _bs(shape, fn):
    return pl.BlockSpec(shape, fn)


def _where_am_i():
    x, y, c = lax.axis_index("x"), lax.axis_index("y"), lax.axis_index("c")
    return x, y, c


def _dev_index(p):
    return 4 * p[0] + 2 * p[1] + p[2]


def _slab(ref, axis, idx, size):
    sl = [slice(None)] * len(ref.shape)
    sl[axis] = pl.ds(idx * size, size)
    return ref.at[tuple(sl)]


HBM = pl.BlockSpec(memory_space=pltpu.HBM)
SEM = pl.BlockSpec(memory_space=pltpu.SEMAPHORE)
DATAFLOW = pltpu.SideEffectType.DATAFLOW_SIDE_EFFECTING


def _in_hbm(a):
    return pltpu.with_memory_space_constraint(a, pltpu.HBM)


def _token_shape():
    return jax.ShapeDtypeStruct((8, 128), F32)


def _split_start(name, n_sems, bufs, issue):
    nb = len(bufs)

    def body(*refs):
        issue(refs[:nb], refs[nb], refs[nb + 1])
        refs[-1][...] = jnp.zeros((8, 128), F32)

    outs = pl.pallas_call(
        body, name=name,
        out_shape=(pltpu.SemaphoreType.DMA((n_sems,)), pltpu.SemaphoreType.DMA((n_sems,)),
                   *[pltpu.HBM(b.shape, b.dtype) for b in bufs], _token_shape()),
        in_specs=[HBM] * nb, out_specs=(SEM, SEM, *[HBM] * nb, pl.BlockSpec(memory_space=pltpu.VMEM)),
        input_output_aliases={i: 2 + i for i in range(nb)},
        compiler_params=pltpu.CompilerParams(has_side_effects=DATAFLOW),
    )(*[_in_hbm(b) for b in bufs])
    return outs[0], outs[1], list(outs[2:2 + nb]), outs[-1]


def _split_relay(name, n_sems, sems, bufs, after, relay):
    nb = len(bufs)

    def body(*refs):
        relay(refs[:nb], refs[nb], refs[nb + 1], refs[nb + 3], refs[nb + 4])
        refs[-1][...] = jnp.zeros((8, 128), F32)

    outs = pl.pallas_call(
        body, name=name,
        out_shape=(pltpu.SemaphoreType.DMA((n_sems,)), pltpu.SemaphoreType.DMA((n_sems,)),
                   *[pltpu.HBM(b.shape, b.dtype) for b in bufs], _token_shape()),
        in_specs=[HBM] * nb + [SEM, SEM, ANY],
        out_specs=(SEM, SEM, *[HBM] * nb, pl.BlockSpec(memory_space=pltpu.VMEM)),
        input_output_aliases={i: 2 + i for i in range(nb)},
        compiler_params=pltpu.CompilerParams(has_side_effects=DATAFLOW),
    )(*bufs, sems[0], sems[1], after)
    return outs[0], outs[1], list(outs[2:2 + nb]), outs[-1]


def _split_wait(name, sems, bufs, after, finish):
    nb = len(bufs)

    def body(*refs):
        finish(refs[:nb], refs[nb], refs[nb + 1])

    outs = pl.pallas_call(
        body, name=name, out_shape=[pltpu.HBM(b.shape, b.dtype) for b in bufs],
        in_specs=[HBM] * nb + [SEM, SEM, ANY], out_specs=[HBM] * nb,
        input_output_aliases={i: i for i in range(nb)},
        compiler_params=pltpu.CompilerParams(has_side_effects=DATAFLOW),
    )(*bufs, sems[0], sems[1], after)
    return list(outs)


def _place(name, items, dtype, after):
    ids = jnp.reshape(_dev_index(_where_am_i()), (1,)).astype(jnp.int32)
    outs = []
    for a, (shard, axis) in enumerate(items):
        rows, cols = shard.shape[-2], shard.shape[-1]
        tr = rows
        while tr * cols * shard.dtype.itemsize > 4 * 1024 * 1024 and tr % 32 == 0:
            tr //= 2
        nt = rows // tr
        full = list(shard.shape)
        full[axis] *= N_DEV
        if shard.ndim == 2 and axis == 0:
            in_spec = _bs((tr, cols), lambda i, ids: (i, 0))
            out_spec = _bs((tr, cols), lambda i, ids, nt=nt: (ids[0] * nt + i, 0))
        elif shard.ndim == 2 and axis == 1:
            in_spec = _bs((tr, cols), lambda i, ids: (i, 0))
            out_spec = _bs((tr, cols), lambda i, ids: (i, ids[0]))
        elif shard.ndim == 3 and axis == 1:
            tr, nt = rows, shard.shape[0]
            in_spec = _bs((None, rows, cols), lambda i, ids: (i, 0, 0))
            out_spec = _bs((None, rows, cols), lambda i, ids: (i, ids[0], 0))
        else:
            assert shard.ndim == 3 and axis == 0 and shard.shape[0] == 1
            in_spec = _bs((None, tr, cols), lambda i, ids: (0, i, 0))
            out_spec = _bs((None, tr, cols), lambda i, ids: (ids[0], i, 0))

        def body(ids_ref, in_ref, after_ref, out_ref):
            del ids_ref, after_ref
            out_ref[...] = in_ref[...].astype(out_ref.dtype)

        outs.append(pl.pallas_call(
            body, name=f"{name}{a}",
            grid_spec=pltpu.PrefetchScalarGridSpec(
                num_scalar_prefetch=1, grid=(nt,), in_specs=[in_spec, ANY], out_specs=out_spec),
            out_shape=jax.ShapeDtypeStruct(tuple(full), dtype), compiler_params=_params(1),
        )(ids, shard, after))
    return outs


class _SplitGather:
    def __init__(self, name, items, dtype, after):
        self.name, self.items, self.n = name, items, len(items)
        fulls = _place(name + "_place", items, dtype, after)
        n = self.n

        def issue(refs, send, recv):
            me, sibling, chips, c = self._geometry()
            for a in range(n):
                self._copy1(refs, send, recv, a, 0, me, sibling).start()
                for j, chip in enumerate(chips):
                    self._copy1(refs, send, recv, a, 1 + j, me, (*chip, c)).start()

        self.send, self.recv, self.bufs, self.token = _split_start(name + "_start", 4 * n, fulls, issue)

    @staticmethod
    def _geometry():
        x, y, c = _where_am_i()
        return (x, y, c), (x, y, 1 - c), [(1 - x, y), (x, 1 - y), (1 - x, 1 - y)], c

    def _blk(self, refs, a, p):
        shard, axis = self.items[a]
        return _slab(refs[a], axis, _dev_index(p), shard.shape[axis])

    def _copy1(self, refs, send, recv, a, k, owner, to):
        return pltpu.make_async_remote_copy(
            src_ref=self._blk(refs, a, owner), dst_ref=self._blk(refs, a, owner), send_sem=send.at[4 * a + k],
            recv_sem=recv.at[4 * a + k], device_id=to, device_id_type=MESH)

    def _copy2(self, refs, send, recv, a, j, owner, to):
        return pltpu.make_async_remote_copy(
            src_ref=self._blk(refs, a, owner), dst_ref=self._blk(refs, a, owner), send_sem=send.at[3 * a + j],
            recv_sem=recv.at[3 * a + j], device_id=to, device_id_type=MESH)

    def relay(self, after):
        n = self.n

        def relay(refs, send_in, recv_in, send_out, recv_out):
            me, sibling, chips, c = self._geometry()
            for a in range(n):
                for j, chip in enumerate(chips):
                    self._copy1(refs, send_in, recv_in, a, 1 + j, (*chip, c), me).wait_recv()
                    self._copy2(refs, send_out, recv_out, a, j, (*chip, c), sibling).start()
            for a in range(n):
                self._copy1(refs, send_in, recv_in, a, 0, sibling, me).wait_recv()
                for k in range(4):
                    self._copy1(refs, send_in, recv_in, a, k, me, sibling).wait_send()

        self.send, self.recv, self.bufs, self.token = _split_relay(
            self.name + "_relay", 3 * n, (self.send, self.recv), self.bufs, after, relay)
        return self.token

    def wait(self, after):
        n = self.n

        def finish(refs, send, recv):
            me, sibling, chips, c = self._geometry()
            for a in range(n):
                for j, chip in enumerate(chips):
                    self._copy2(refs, send, recv, a, j, (*chip, 1 - c), me).wait_recv()
                    self._copy2(refs, send, recv, a, j, (*chip, c), sibling).wait_send()

        return _split_wait(self.name + "_wait", (self.send, self.recv), self.bufs, after, finish)


class _SplitReduceScatter:
    def __init__(self, name, grads):
        self.name, self.n = name, len(grads)
        n = self.n
        g4 = [g.reshape(4, 2, *g.shape[1:]) for g in grads]
        land = [lax.empty((4, 1, *g.shape[1:]), g.dtype) for g in grads]

        def issue(refs, send, recv):
            for a in range(n):
                self._swap(refs, send, recv, a).start()

        self.send, self.recv, self.bufs, self.token = _split_start(name + "_d2d_start", n, g4 + land, issue)

    def _swap(self, refs, send, recv, a):
        x, y, c = _where_am_i()
        return pltpu.make_async_remote_copy(
            src_ref=refs[a].at[:, pl.ds(1 - c, 1)], dst_ref=refs[self.n + a], send_sem=send.at[a], recv_sem=recv.at[a],
            device_id=(x, y, 1 - c), device_id_type=MESH)

    def _hop(self, refs, send, recv, a, m):
        x, y, c = _where_am_i()
        px = (1 - x) if m & 2 else x
        py = (1 - y) if m & 1 else y
        return pltpu.make_async_remote_copy(
            src_ref=refs[a].at[2 * px + py], dst_ref=refs[self.n + a].at[m - 1], send_sem=send.at[3 * a + m - 1],
            recv_sem=recv.at[3 * a + m - 1], device_id=(px, py, c), device_id_type=MESH)

    def combine_and_send(self, after):
        n = self.n

        def finish(refs, send, recv):
            for a in range(n):
                self._swap(refs, send, recv, a).wait()

        bufs = _split_wait(self.name + "_d2d_wait", (self.send, self.recv), self.bufs, after, finish)
        x, y, c = _where_am_i()
        ids = jnp.stack([c, 2 * x + y]).astype(jnp.int32)
        self.own, sums = [], []
        for a in range(n):
            own, hb = _pair_sum(f"{self.name}_sum{a}", bufs[a], bufs[n + a], ids)
            self.own.append(own)
            sums.append(hb)
        land = [lax.empty((3, *h.shape[1:]), h.dtype) for h in sums]

        def issue(refs, send, recv):
            for a in range(n):
                for m in (1, 2, 3):
                    self._hop(refs, send, recv, a, m).start()

        self.send, self.recv, self.bufs, self.token = _split_start(self.name + "_ici_start", 3 * n, sums + land, issue)
        return self.token

    def wait(self, after):
        n = self.n

        def finish(refs, send, recv):
            for a in range(n):
                for m in (1, 2, 3):
                    self._hop(refs, send, recv, a, m).wait()

        bufs = _split_wait(self.name + "_ici_wait", (self.send, self.recv), self.bufs, after, finish)
        return list(zip(self.own, bufs[n:]))


def _pair_sum(name, g4, land, ids):
    rows, cols = g4.shape[2], g4.shape[3]
    tr = rows
    while tr * cols * 2 > 2 * 1024 * 1024 and tr % 32 == 0:
        tr //= 2

    def body(ids_ref, g_ref, l_ref, own_ref, sum_ref):
        h = g_ref[...].astype(F32) + l_ref[...].astype(F32)
        sum_ref[...] = h.astype(sum_ref.dtype)

        @pl.when(pl.program_id(1) == ids_ref[1])
        def _():
            own_ref[...] = h

    return pl.pallas_call(
        body, name=name,
        grid_spec=pltpu.PrefetchScalarGridSpec(
            num_scalar_prefetch=1, grid=(rows // tr, 4),
            in_specs=[_bs((None, None, tr, cols), lambda i, q, ids: (q, ids[0], i, 0)),
                      _bs((None, None, tr, cols), lambda i, q, ids: (q, 0, i, 0))],
            out_specs=[_bs((tr, cols), lambda i, q, ids: (i, 0)), _bs((None, tr, cols), lambda i, q, ids: (q, i, 0))]),
        out_shape=[jax.ShapeDtypeStruct((rows, cols), F32), jax.ShapeDtypeStruct((4, rows, cols), g4.dtype)],
        compiler_params=_params(2),
    )(ids, g4, land)


def _win_sum(ext, w, off):
    s = ext + _shift(ext, -1)
    if w >= 4:
        s = _shift(s, -1) + _shift(s, 1)
    if w >= 8:
        s = _shift(s, -2) + _shift(s, 2)
    if w >= 16:
        s = _shift(s, -4) + _shift(s, 4)
    return _shift(s, off) if off else s


def _inv_count(r0, t, w, seq):
    pos = r0 + lax.broadcasted_iota(jnp.int32, (t, 1), 0)
    cnt = jnp.minimum(pos + w // 2, seq) - jnp.maximum(pos - w // 2, 0)
    return 1.0 / cnt.astype(F32)


def _pool_fwd(p3, w_pool, pool_scale, seq, d_model):
    dp = d_model // 2
    pg = dp // len(POOL_WINDOWS)
    t = min(POOL_CHUNK, seq)
    n_chunks = seq // t
    h = WIN_HALO

    def body(u_ref, w_ref, sc_ref, d_ref, y_ref, pad_ref):
        g = pl.program_id(0)
        zeros = jnp.zeros((h, pg), F32)
        pad_ref[0:h, :] = zeros
        pad_ref[h + seq:h + seq + h, :] = zeros

        def fill(ci, _):
            r0 = pl.multiple_of(ci * t, t)
            pad_ref[pl.ds(h + r0, t), :] = u_ref[pl.ds(r0, t), :]
            return 0

        lax.fori_loop(0, n_chunks, fill, 0)
        wmat = w_ref[...]
        scale = sc_ref[...]
        for gi, w in enumerate(POOL_WINDOWS):
            @pl.when(g == gi)
            def _(w=w):
                def chunk(ci, _):
                    r0 = pl.multiple_of(ci * t, t)
                    ext = pad_ref[pl.ds(r0, t + 2 * h), :]
                    mean = _win_sum(ext, w, 0)[h:h + t, :] * _inv_count(r0, t, w, seq)
                    d = (mean - ext[h:h + t, :]).astype(BF16)
                    d_ref[pl.ds(r0, t), :] = d
                    q = jnp.dot(d, wmat, preferred_element_type=F32)
                    y_ref[pl.ds(r0, t), :] = (q * scale).astype(BF16)
                    return 0

                lax.fori_loop(0, n_chunks, chunk, 0, unroll=2)

    return pl.pallas_call(
        body, name="pool_fwd", grid=(len(POOL_WINDOWS),),
        in_specs=[_bs((None, seq, pg), lambda g: (0, 0, g)), _bs((None, pg, pg), lambda g: (g, 0, 0)),
                  _bs((1, pg), lambda g: (0, g))],
        out_specs=[_bs((seq, pg), lambda g: (0, g)), _bs((seq, pg), lambda g: (0, g))],
        out_shape=[jax.ShapeDtypeStruct((seq, dp), BF16), jax.ShapeDtypeStruct((seq, d_model), BF16)],
        scratch_shapes=[pltpu.VMEM((seq + 2 * h, pg), F32)],
        compiler_params=_params(1),
    )(p3, w_pool, pool_scale)


def _pool_bwd(d, dy, w_pool, pool_scale, token, seq, d_model):
    dp = d_model // 2
    pg = dp // len(POOL_WINDOWS)
    t = min(POOL_CHUNK, seq)
    n_chunks = seq // t
    h = WIN_HALO
    tn_dims = (((0,), (0,)), ((), ()))
    nt_dims = (((1,), (1,)), ((), ()))

    def body(d_ref, dy_ref, w_ref, sc_ref, tok_ref, du_ref, dwb_ref, dsc_ref, pad_ref, dd_ref, dw_ref):
        del tok_ref
        g = pl.program_id(0)
        zeros = jnp.zeros((h, pg), F32)
        pad_ref[0:h, :] = zeros
        pad_ref[h + seq:h + seq + h, :] = zeros
        wmat = w_ref[...]
        scale = sc_ref[...]
        for gi, w in enumerate(POOL_WINDOWS):
            @pl.when(g == gi)
            def _(w=w):
                dw_ref[...] = jnp.zeros((pg, pg), F32)

                def first(ci, dsc):
                    r0 = pl.multiple_of(ci * t, t)
                    dv = d_ref[pl.ds(r0, t), :]
                    dyv = dy_ref[pl.ds(r0, t), :]
                    q = jnp.dot(dv, wmat, preferred_element_type=F32)
                    dsc = dsc + jnp.sum(dyv * q, axis=0, keepdims=True)
                    dq = (dyv * scale).astype(BF16)
                    dw_ref[...] += lax.dot_general(dv, dq, tn_dims, preferred_element_type=F32)
                    dd = lax.dot_general(dq, wmat, nt_dims, preferred_element_type=F32)
                    dd_ref[pl.ds(r0, t), :] = dd
                    pad_ref[pl.ds(h + r0, t), :] = dd * _inv_count(r0, t, w, seq)
                    return dsc

                def first_pair(cj, dsc):
                    return first(2 * cj + 1, first(2 * cj, dsc))

                dsc_ref[...] = lax.fori_loop(0, n_chunks // 2, first_pair, jnp.zeros((1, pg), F32))
                dwb_ref[...] = dw_ref[...].reshape(N_DEV, pg // N_DEV, pg).astype(BF16)

                def second(ci, _):
                    r0 = pl.multiple_of(ci * t, t)
                    ext = pad_ref[pl.ds(r0, t + 2 * h), :]
                    back = _win_sum(ext, w, 1)[h:h + t, :]
                    du_ref[pl.ds(r0, t), :] = (back - dd_ref[pl.ds(r0, t), :]).astype(BF16)
                    return 0

                lax.fori_loop(0, n_chunks, second, 0, unroll=2)

    return pl.pallas_call(
        body, name="pool_bwd", grid=(len(POOL_WINDOWS),),
        in_specs=[_bs((seq, pg), lambda g: (0, g)), _bs((seq, pg), lambda g: (0, g)),
                  _bs((None, pg, pg), lambda g: (g, 0, 0)), _bs((1, pg), lambda g: (0, g)),
                  _bs((8, 128), lambda g: (0, 0))],
        out_specs=[_bs((seq, pg), lambda g: (0, g)), _bs((N_DEV, None, pg // N_DEV, pg), lambda g: (0, g, 0, 0)),
                   _bs((1, pg), lambda g: (0, g))],
        out_shape=[jax.ShapeDtypeStruct((seq, 3 * dp), BF16),
                   jax.ShapeDtypeStruct((N_DEV, len(POOL_WINDOWS), pg // N_DEV, pg), BF16),
                   jax.ShapeDtypeStruct((1, dp), F32)],
        scratch_shapes=[pltpu.VMEM((seq + 2 * h, pg), F32), pltpu.VMEM((seq, pg), F32), pltpu.VMEM((pg, pg), F32)],
        compiler_params=_params(1),
    )(d, dy, w_pool, pool_scale, token)


def _tile_scan(n_tiles, lanes, loads, stores):
    row = lax.broadcasted_iota(jnp.int32, (8, lanes), 0)
    group = 8

    def local_scan(n, k):
        aa, bb = loads[n](k)
        for sh in (1, 2, 4):
            if n == 0:
                ok = row >= sh
                ap = jnp.where(ok, pltpu.roll(aa, sh, 0), 1.0)
                bp = jnp.where(ok, pltpu.roll(bb, sh, 0), 0.0)
            else:
                ok = row < 8 - sh
                ap = jnp.where(ok, pltpu.roll(aa, 8 - sh, 0), 1.0)
                bp = jnp.where(ok, pltpu.roll(bb, 8 - sh, 0), 0.0)
            bb = aa * bp + bb
            aa = aa * ap
        return aa, bb

    def step(s, carry):
        carry = list(carry)
        for n in range(2):
            tiles = [s * group + u if n == 0 else n_tiles - 1 - (s * group + u) for u in range(group)]
            local = [local_scan(n, k) for k in tiles]
            for k, (aa, bb) in zip(tiles, local):
                hh = bb + aa * carry[n]
                stores[n](k, hh)
                carry[n] = jnp.broadcast_to(hh[7:8, :] if n == 0 else hh[0:1, :], (8, lanes))
        return tuple(carry)

    zeros = jnp.zeros((8, lanes), F32)
    lax.fori_loop(0, n_tiles // group, step, (zeros, zeros))


def _gate_preacts(xc, wcat_ref):
    xcb = xc.astype(BF16)
    return xcb, jnp.dot(xcb, wcat_ref[...], preferred_element_type=F32)


def _gates(pre, n, pk_ref, sp):
    lh = pre.shape[1] // 4
    r = _sigmoid(pre[:, (2 * n) * lh:(2 * n + 1) * lh] + pk_ref[pl.ds(4 + n, 1), :])
    i = _sigmoid(pre[:, (2 * n + 1) * lh:(2 * n + 2) * lh] + pk_ref[pl.ds(6 + n, 1), :])
    log_a = (-RG_C * r) * sp[n]
    a = jnp.exp(log_a)
    x = 2.0 * log_a
    one_minus_a2 = jnp.where(x > -0.01, -(x * (1.0 + x * (0.5 + x * (1.0 / 6.0)))), 1.0 - a * a)
    m = jnp.sqrt(one_minus_a2)
    return r, i, a, m


def _conv_chunk(upad_ref, pk_ref, cb, r0, t):
    ext = upad_ref[pl.ds(r0, t + 2 * CONV_HALO), :]
    acc = pk_ref[pl.ds(1, 1), :] * ext
    for k in (0, 2, 3):
        acc = acc + pk_ref[pl.ds(k, 1), :] * _shift(ext, k - 1)
    return acc[CONV_HALO:CONV_HALO + t, :] + cb, ext


def _lru_fwd(p3, y_in, pack, conv_b, wcat, token, seq, d_model):
    dl = d_model // 2
    lh = dl // N_HEADS
    t = min(SEQ_CHUNK, seq)
    n_chunks = seq // t
    hal = CONV_HALO
    first_rec_block = (d_model - dl) // lh

    def body(ur_ref, ug_ref, pk_ref, cb_ref, wcat_ref, yin_ref, tok_ref, y_ref, h0_ref, h1_ref,
             upad, a_scr, b_scr):
        del yin_ref, tok_ref
        zeros = jnp.zeros((hal, lh), F32)
        upad[0:hal, :] = zeros
        upad[hal + seq:hal + seq + hal, :] = zeros
        for ref in (h0_ref, h1_ref):
            ref[0:hal, :] = zeros
            ref[hal + seq:hal + seq + hal, :] = zeros

        def fill(ci, _):
            r0 = pl.multiple_of(ci * t, t)
            upad[pl.ds(hal + r0, t), :] = ur_ref[pl.ds(r0, t), :]
            return 0

        lax.fori_loop(0, n_chunks, fill, 0)
        cb = cb_ref[...]
        sp = [_softplus(-pk_ref[pl.ds(8 + n, 1), :]) for n in range(2)]

        def chunk(ci, _):
            r0 = pl.multiple_of(ci * t, t)
            xc, _ext = _conv_chunk(upad, pk_ref, cb, r0, t)
            _, pre = _gate_preacts(xc, wcat_ref)
            for n in range(2):
                _, i, a, m = _gates(pre, n, pk_ref, sp)
                a_scr[n, pl.ds(r0, t), :] = a
                b_scr[n, pl.ds(r0, t), :] = (m * i) * xc
            return 0

        lax.fori_loop(0, n_chunks, chunk, 0, unroll=2)

        def load(n):
            def get(k):
                at = pl.ds(pl.multiple_of(k * 8, 8), 8)
                return a_scr[n, at, :], b_scr[n, at, :]
            return get

        def store(ref):
            def put(k, v):
                ref[pl.ds(pl.multiple_of(hal + k * 8, 8), 8), :] = v
            return put

        _tile_scan(seq // 8, lh, [load(0), load(1)], [store(h0_ref), store(h1_ref)])

        def out(ci, _):
            r0 = pl.multiple_of(ci * t, t)
            hsum = h0_ref[pl.ds(hal + r0, t), :] + h1_ref[pl.ds(hal + r0, t), :]
            gl, _dg = _gelu_and_grad(ug_ref[pl.ds(r0, t), :])
            y_ref[pl.ds(r0, t), :] = (hsum * gl).astype(BF16)
            return 0

        lax.fori_loop(0, n_chunks, out, 0)

    return pl.pallas_call(
        body, name="lru_fwd", grid=(N_HEADS,),
        in_specs=[_bs((None, seq, lh), lambda h: (1, 0, h)), _bs((None, seq, lh), lambda h: (2, 0, h)),
                  _bs((None, SMALL_ROWS, lh), lambda h: (h, 0, 0)), _bs((1, lh), lambda h: (0, h)),
                  _bs((None, lh, 4 * lh), lambda h: (h, 0, 0)),
                  ANY, _bs((8, 128), lambda h: (0, 0))],
        out_specs=[_bs((seq, lh), lambda h: (0, first_rec_block + h)),
                   _bs((seq + 2 * hal, lh), lambda h: (0, h)), _bs((seq + 2 * hal, lh), lambda h: (0, h))],
        out_shape=[jax.ShapeDtypeStruct((seq, d_model), BF16), jax.ShapeDtypeStruct((seq + 2 * hal, dl), F32),
                   jax.ShapeDtypeStruct((seq + 2 * hal, dl), F32)],
        scratch_shapes=[pltpu.VMEM((seq + 2 * hal, lh), F32), pltpu.VMEM((2, seq, lh), F32),
                        pltpu.VMEM((2, seq, lh), F32)],
        input_output_aliases={5: 0},
        compiler_params=_params(1),
    )(p3, p3, pack, conv_b, wcat, y_in, token)


def _lru_bwd(p3, dy, h0p, h1p, dproj_in, pack, conv_b, wcat, token, seq, d_model):
    dl = d_model // 2
    lh = dl // N_HEADS
    t = min(SEQ_CHUNK, seq)
    n_chunks = seq // t
    hal = CONV_HALO
    first_rec_block = (d_model - dl) // lh
    tn_dims = (((0,), (0,)), ((), ()))
    nt_dims = (((1,), (1,)), ((), ()))

    def body(ur_ref, ug_ref, dy_ref, h0_ref, h1_ref, pk_ref, cb_ref, wcat_ref, tok_ref, din_ref,
             dproj_ref, dpk_ref, dcb_ref, dwcat_ref,
             upad, a_scr, dh_scr, g_scr, dxc_pad, dpr_ref, out_sems, gate_scr):
        del din_ref, tok_ref
        head = pl.program_id(0)
        slot = head % 2
        zeros = jnp.zeros((hal, lh), F32)
        for ref in (upad, dxc_pad):
            ref[0:hal, :] = zeros
            ref[hal + seq:hal + seq + hal, :] = zeros
        for n in range(2):
            a_scr[n, 0:hal, :] = zeros
            a_scr[n, hal + seq:hal + seq + hal, :] = zeros

        def fill(ci, _):
            r0 = pl.multiple_of(ci * t, t)
            upad[pl.ds(hal + r0, t), :] = ur_ref[pl.ds(r0, t), :]
            return 0

        lax.fori_loop(0, n_chunks, fill, 0)
        cb = cb_ref[...]
        lam = [pk_ref[pl.ds(8 + n, 1), :] for n in range(2)]
        sp = [_softplus(-lam[n]) for n in range(2)]

        def chunk1(ci, _):
            r0 = pl.multiple_of(ci * t, t)
            xc, _ext = _conv_chunk(upad, pk_ref, cb, r0, t)
            _, pre = _gate_preacts(xc, wcat_ref)
            for n in range(2):
                r, i, a, m = _gates(pre, n, pk_ref, sp)
                a_scr[n, pl.ds(hal + r0, t), :] = a
                for q, v in enumerate((r, i, m)):
                    gate_scr[3 * n + q, pl.ds(r0, t), :] = v
            hsum = h0_ref[pl.ds(hal + r0, t), :] + h1_ref[pl.ds(hal + r0, t), :]
            gl, dgl = _gelu_and_grad(ug_ref[pl.ds(r0, t), :])
            dyv = dy_ref[pl.ds(r0, t), :]
            dh_scr[pl.ds(r0, t), :] = dyv * gl
            dpr_ref[slot, 1, pl.ds(r0, t), :] = ((dyv * hsum) * dgl).astype(BF16)
            return 0

        lax.fori_loop(0, n_chunks, chunk1, 0, unroll=2)

        def load(n):
            def get(k):
                r0 = pl.multiple_of(k * 8, 8)
                if n == 0:
                    coef = _shift(a_scr[0, pl.ds(pl.multiple_of(hal + r0, 8), 16), :], 1)[0:8, :]
                else:
                    coef = _shift(a_scr[1, pl.ds(pl.multiple_of(hal + r0 - 8, 8), 16), :], -1)[8:16, :]
                return coef, dh_scr[pl.ds(r0, 8), :]
            return get

        def store(n):
            def put(k, v):
                g_scr[n, pl.ds(pl.multiple_of(k * 8, 8), 8), :] = v
            return put

        _tile_scan(seq // 8, lh, [load(1), load(0)], [store(1), store(0)])

        dwcat_ref[...] = jnp.zeros((lh, 4 * lh), F32)

        def chunk3(ci, carry):
            dba, dbi, dlam, dcb = carry
            r0 = pl.multiple_of(ci * t, t)
            xc, _ext = _conv_chunk(upad, pk_ref, cb, r0, t)
            xcb = xc.astype(BF16)
            dxc = jnp.zeros((t, lh), F32)
            dba, dbi, dlam = list(dba), list(dbi), list(dlam)
            dpre = []
            for n in range(2):
                r, i, m = (gate_scr[3 * n + q, pl.ds(r0, t), :] for q in range(3))
                a = a_scr[n, pl.ds(hal + r0, t), :]
                hext = (h0_ref if n == 0 else h1_ref)[pl.ds(r0, t + 2 * hal), :]
                hprev = _shift(hext, -1 if n == 0 else 1)[hal:hal + t, :]
                gb = g_scr[n, pl.ds(r0, t), :]
                da = gb * hprev
                dm = gb * i * xc
                di = gb * m * xc
                dxc = dxc + gb * (m * i)
                dlog_a = da * a - dm * (a * a) / m
                dr = dlog_a * (-RG_C * sp[n])
                dlam[n] = dlam[n] + jnp.sum(dlog_a * r, axis=0, keepdims=True)
                dpr = dr * r * (1.0 - r)
                dpi = di * i * (1.0 - i)
                dba[n] = dba[n] + jnp.sum(dpr, axis=0, keepdims=True)
                dbi[n] = dbi[n] + jnp.sum(dpi, axis=0, keepdims=True)
                dpre += [dpr.astype(BF16), dpi.astype(BF16)]
            dpre = jnp.concatenate(dpre, axis=1)
            dwcat_ref[...] += lax.dot_general(xcb, dpre, tn_dims, preferred_element_type=F32)
            dxc = dxc + lax.dot_general(dpre, wcat_ref[...], nt_dims, preferred_element_type=F32)
            dxc_pad[pl.ds(hal + r0, t), :] = dxc
            dcb = dcb + jnp.sum(dxc, axis=0, keepdims=True)
            return tuple(dba), tuple(dbi), tuple(dlam), dcb

        zr = jnp.zeros((1, lh), F32)
        def chunk3_pair(cj, carry):
            return chunk3(2 * cj + 1, chunk3(2 * cj, carry))

        dba, dbi, dlam, dcb = lax.fori_loop(0, n_chunks // 2, chunk3_pair, ((zr, zr), (zr, zr), (zr, zr), zr))
        dcb_ref[...] = dcb
        for n in range(2):
            dpk_ref[pl.ds(4 + n, 1), :] = dba[n]
            dpk_ref[pl.ds(6 + n, 1), :] = dbi[n]
            dpk_ref[pl.ds(8 + n, 1), :] = dlam[n] * (RG_C * jax.nn.sigmoid(-lam[n]))
        dpk_ref[pl.ds(10, SMALL_ROWS - 10), :] = jnp.zeros((SMALL_ROWS - 10, lh), F32)

        def chunk4(ci, dtap):
            r0 = pl.multiple_of(ci * t, t)
            gext = dxc_pad[pl.ds(r0, t + 2 * hal), :]
            uext = upad[pl.ds(r0, t + 2 * hal), :]
            gmid = gext[hal:hal + t, :]
            du = pk_ref[pl.ds(1, 1), :] * gext
            for k in (0, 2, 3):
                du = du + pk_ref[pl.ds(k, 1), :] * _shift(gext, 1 - k)
            dpr_ref[slot, 0, pl.ds(r0, t), :] = du[hal:hal + t, :].astype(BF16)
            out = []
            for k in range(4):
                usl = _shift(uext, k - 1)[hal:hal + t, :]
                out.append(dtap[k] + jnp.sum(gmid * usl, axis=0, keepdims=True))
            return tuple(out)

        dtap = lax.fori_loop(0, n_chunks, chunk4, (zr, zr, zr, zr))
        for k in range(4):
            dpk_ref[pl.ds(k, 1), :] = dtap[k]

        def out_copy(s, b):
            return pltpu.make_async_copy(
                dpr_ref.at[s, b], dproj_ref.at[:, pl.ds(pl.multiple_of((1 + b) * dl + head * lh, lh), lh)],
                out_sems.at[s, b])

        for b in range(2):
            out_copy(slot, b).start()

        @pl.when(head > 0)
        def _():
            for b in range(2):
                out_copy(1 - slot, b).wait()

        @pl.when(head == N_HEADS - 1)
        def _():
            for b in range(2):
                out_copy(slot, b).wait()

    return pl.pallas_call(
        body, name="lru_bwd", grid=(N_HEADS,),
        in_specs=[_bs((None, seq, lh), lambda h: (1, 0, h)), _bs((None, seq, lh), lambda h: (2, 0, h)),
                  _bs((seq, lh), lambda h: (0, first_rec_block + h)),
                  _bs((seq + 2 * hal, lh), lambda h: (0, h)), _bs((seq + 2 * hal, lh), lambda h: (0, h)),
                  _bs((None, SMALL_ROWS, lh), lambda h: (h, 0, 0)), _bs((1, lh), lambda h: (0, h)),
                  _bs((None, lh, 4 * lh), lambda h: (h, 0, 0)),
                  _bs((8, 128), lambda h: (0, 0)), ANY],
        out_specs=[ANY, _bs((None, SMALL_ROWS, lh), lambda h: (h, 0, 0)),
                   _bs((1, lh), lambda h: (0, h)), _bs((None, lh, 4 * lh), lambda h: (h, 0, 0))],
        out_shape=[jax.ShapeDtypeStruct((seq, 3 * dl), BF16), jax.ShapeDtypeStruct((N_HEADS, SMALL_ROWS, lh), F32),
                   jax.ShapeDtypeStruct((1, dl), F32), jax.ShapeDtypeStruct((N_HEADS, lh, 4 * lh), F32)],
        scratch_shapes=[pltpu.VMEM((seq + 2 * hal, lh), F32), pltpu.VMEM((2, seq + 2 * hal, lh), F32),
                        pltpu.VMEM((seq, lh), F32), pltpu.VMEM((2, seq, lh), F32),
                        pltpu.VMEM((seq + 2 * hal, lh), F32), pltpu.VMEM((2, 2, seq, lh), BF16),
                        pltpu.SemaphoreType.DMA((2, 2)), pltpu.VMEM((6, seq, lh), F32)],
        input_output_aliases={9: 0},
        compiler_params=_params(1),
    )(p3, p3, dy, h0p, h1p, pack, conv_b, wcat, token, dproj_in)


class _tiles:
    def __init__(self, seq, d_model, d_ff):
        self.rows = min(1024, seq)
        self.ln_rows = min(256, seq)
        self.ff_cols = min(2048, d_ff)
        self.ff_split = 4
        self.ff_k = min(2048, d_ff)
        self.grad_rows = 512


def _ln_loss_bwd(ffn, x1, tgt, g, b, tr):
    seq, d = ffn.shape

    def body(f_ref, x_ref, t_ref, g_ref, b_ref, dz_ref, dzb_ref, dg_ref, db_ref, loss_ref):
        i = pl.program_id(0)
        gv = g_ref[...]
        z = ALPHA * x_ref[...] + f_ref[...]
        y, xhat, rstd = _ln_fwd(z, gv, b_ref[...])
        err = y - t_ref[...]
        part = 0.5 * jnp.sum(jnp.mean(err * err, axis=-1, keepdims=True), axis=0, keepdims=True)
        dz, dg, db = _ln_bwd(err * (1.0 / d), xhat, rstd, gv)
        dz_ref[...] = dz
        dzb_ref[...] = dz.astype(BF16)
        _acc_rows(dg_ref, i == 0, dg)
        _acc_rows(db_ref, i == 0, db)
        _acc_rows(loss_ref, i == 0, jnp.broadcast_to(part, (8, 128)))

    row = _bs((tr, d), lambda i: (i, 0))
    vec = _bs((1, d), lambda i: (0, 0))
    return pl.pallas_call(
        body, name="ln_ffn_loss", grid=(seq // tr,), in_specs=[row, row, row, vec, vec],
        out_specs=[row, row, vec, vec, _bs((8, 128), lambda i: (0, 0))],
        out_shape=[jax.ShapeDtypeStruct((seq, d), F32), jax.ShapeDtypeStruct((seq, d), BF16),
                   jax.ShapeDtypeStruct((1, d), F32), jax.ShapeDtypeStruct((1, d), F32),
                   jax.ShapeDtypeStruct((8, 128), F32)],
        compiler_params=_params(1),
    )(ffn, x1, tgt, g, b)


def _ln_bwd_side(dx_branch, dres, z, g, b, n_steps):
    seq, d = z.shape
    tr = seq // n_steps

    def fn(step, ins, outs):
        a_ref, r_ref, z_ref, g_ref, b_ref = ins
        dz_ref, dzb_ref, dg_ref, db_ref = outs
        gv = g_ref[...]
        _, xhat, rstd = _ln_fwd(z_ref[...], gv, b_ref[...])
        dz, dg, db = _ln_bwd(ALPHA * r_ref[...] + a_ref[...], xhat, rstd, gv)
        dz_ref[...] = dz
        dzb_ref[...] = dz.astype(BF16)
        _acc_rows(dg_ref, step == 0, dg)
        _acc_rows(db_ref, step == 0, db)

    row = ((tr, d), lambda s: (s, 0))
    vec = ((1, d), lambda s: (0, 0))
    shapes = [jax.ShapeDtypeStruct((seq, d), F32), jax.ShapeDtypeStruct((seq, d), BF16),
              jax.ShapeDtypeStruct((1, d), F32), jax.ShapeDtypeStruct((1, d), F32)]
    return [(dx_branch, *row), (dres, *row), (z, *row), (g, *vec), (b, *vec)], shapes, [row, row, vec, vec], fn


def _to_bf16(name, a, token):
    rows, cols = a.shape
    tr = min(512, rows)

    def body(a_ref, tok_ref, o_ref):
        del tok_ref
        o_ref[...] = a_ref[...].astype(BF16)

    return pl.pallas_call(
        body, name=name, grid=(rows // tr,),
        in_specs=[_bs((tr, cols), lambda i: (i, 0)), _bs((8, 128), lambda i: (0, 0))],
        out_specs=_bs((tr, cols), lambda i: (i, 0)), out_shape=jax.ShapeDtypeStruct((rows, cols), BF16),
        compiler_params=_params(1),
    )(a, token)


def _sum_blocks(name, parts):
    def body(p_ref, o_ref):
        acc = p_ref[0]
        for s in range(1, parts.shape[0]):
            acc = acc + p_ref[s]
        o_ref[...] = acc

    return pl.pallas_call(body, name=name, out_shape=jax.ShapeDtypeStruct(parts.shape[1:], F32))(parts)


def _adamw_values(w, g, m, v):
    m = ADAM_B1 * m + (1.0 - ADAM_B1) * g
    v = ADAM_B2 * v + (1.0 - ADAM_B2) * (g * g)
    m_hat = m / (1.0 - ADAM_B1 ** ADAM_STEP)
    v_hat = v / (1.0 - ADAM_B2 ** ADAM_STEP)
    delta = -ADAM_LR * (m_hat / (jnp.sqrt(v_hat) + ADAM_EPS) + ADAM_WD * w)
    return delta, m, v


def _adamw_side(own, parts, w, m, v, n_steps):
    rows, cols = w.shape
    tr = rows // n_steps

    def fn(step, ins, outs):
        o_ref, p_ref, w_ref, m_ref, v_ref = ins
        g = o_ref[...]
        for s in range(parts.shape[0]):
            g = g + p_ref[s].astype(F32)
        delta, mn, vn = _adamw_values(w_ref[...], g, m_ref[...], v_ref[...])
        for ref, val in zip(outs, (g, delta, mn, vn)):
            ref[...] = val

    row = ((tr, cols), lambda s: (s, 0))
    stack = ((parts.shape[0], tr, cols), lambda s: (0, s, 0))
    shapes = [jax.ShapeDtypeStruct((rows, cols), F32)] * 4
    return [(own, *row), (parts, *stack), (w, *row), (m, *row), (v, *row)], shapes, [row] * 4, fn


def _sum_adamw(name, own, parts, w, m, v):
    rows, cols = w.shape
    n_parts = parts.shape[0]
    tr = rows
    min_rows = 8 if parts.dtype == F32 else 16
    while tr * cols * 4 > 2 * 1024 * 1024 and tr % (2 * min_rows) == 0:
        tr //= 2

    def body(*refs):
        if own is None:
            p_ref, w_ref, m_ref, v_ref, g_ref, d_ref, mo_ref, vo_ref = refs
            g = p_ref[0].astype(F32)
            rest = range(1, n_parts)
        else:
            o_ref, p_ref, w_ref, m_ref, v_ref, g_ref, d_ref, mo_ref, vo_ref = refs
            g = o_ref[...]
            rest = range(n_parts)
        for s in rest:
            g = g + p_ref[s].astype(F32)
        delta, mn, vn = _adamw_values(w_ref[...], g, m_ref[...], v_ref[...])
        g_ref[...] = g
        d_ref[...] = delta
        mo_ref[...] = mn
        vo_ref[...] = vn

    spec = _bs((tr, cols), lambda i: (i, 0))
    lead = [] if own is None else [own]
    return pl.pallas_call(
        body, name=name, grid=(rows // tr,),
        in_specs=[spec] * len(lead) + [_bs((n_parts, tr, cols), lambda i: (0, i, 0)), spec, spec, spec],
        out_specs=[spec] * 4, out_shape=[jax.ShapeDtypeStruct((rows, cols), F32)] * 4,
        compiler_params=_params(1),
    )(*lead, parts, w, m, v)


def _rows128(a):
    return a.reshape(-1, 128)


def kernel(x, ln_mix_g, ln_mix_b, w_in, w_pool, pool_scale, conv_w, conv_b, w_rg_a, b_rg_a, w_rg_i, b_rg_i, rg_lambda, w_out, ln_ffn_g, ln_ffn_b, w_mlp_in, w_mlp_out, loss_target, m_ln_mix_g, m_ln_mix_b, m_w_in, m_w_pool, m_pool_scale, m_conv_w, m_conv_b, m_w_rg_a, m_b_rg_a, m_w_rg_i, m_b_rg_i, m_rg_lambda, m_w_out, m_ln_ffn_g, m_ln_ffn_b, m_w_mlp_in, m_w_mlp_out, v_ln_mix_g, v_ln_mix_b, v_w_in, v_w_pool, v_pool_scale, v_conv_w, v_conv_b, v_w_rg_a, v_b_rg_a, v_w_rg_i, v_b_rg_i, v_rg_lambda, v_w_out, v_ln_ffn_g, v_ln_ffn_b, v_w_mlp_in, v_w_mlp_out):
    seq, d_model = x.shape[1], x.shape[2]
    dh = d_model // 2
    lh = dh // N_HEADS
    pg = dh // len(POOL_WINDOWS)
    d_ff = w_mlp_in.shape[2] * N_DEV
    assert lh == 128 and conv_w.shape[3] == lh and w_pool.shape[2] * N_DEV == pg

    xs = x[0]
    tgt = loss_target[0]

    def small_pack(cw, ba, bi, lam):
        return jnp.concatenate([cw.reshape(4, lh), ba.reshape(2, lh), bi.reshape(2, lh), lam.reshape(2, lh),
                                jnp.zeros((SMALL_ROWS - 10, lh), F32)], axis=0)

    pack_mine = small_pack(conv_w, b_rg_a, b_rg_i, rg_lambda)
    pack_bits = lax.bitcast_convert_type(pack_mine, BF16).reshape(1, SMALL_ROWS, 2 * lh)
    win_gather = _SplitGather("gather_w_in", [(w_in[0], 1), (w_pool[0], 1), (pack_bits, 0)], BF16, after=pack_mine)
    wout_gather = _SplitGather("gather_w_out", [(w_out[0], 0)], BF16, after=win_gather.token)
    w1_gather = _SplitGather("gather_w_mlp_in", [(w_mlp_in[0], 1)], BF16, after=wout_gather.token)
    w2_gather = _SplitGather("gather_w_mlp_out", [(w_mlp_out[0], 0)], BF16, after=w1_gather.token)
    xb = _to_bf16("x_bf16", x[0], w2_gather.token)
    win_full, wpool_full, pack_bits_full = win_gather.wait(after=win_gather.relay(after=xb))
    pack_full = lax.bitcast_convert_type(pack_bits_full.reshape(N_DEV, SMALL_ROWS, lh, 2), F32)
    wcat = jnp.concatenate([w_rg_a[0, 0], w_rg_i[0, 0], w_rg_a[0, 1], w_rg_i[0, 1]], axis=-1).astype(BF16)
    vec = lambda i, j, k: (0, 0)
    row_full = lambda i, j, k: (i, 0)

    def after(token):
        return (token, _sp((8, 128), vec))

    def sds(shape, dtype):
        return jax.ShapeDtypeStruct(shape, dtype)

    def plain_epi(acc, i, ex, out):
        out[0][...] = acc

    def bf16_epi(acc, i, ex, out):
        out[0][...] = acc.astype(BF16)

    t = _tiles(seq, d_model, d_ff)

    (p3,) = _matmul(
        "proj", xb, win_full, _sp((t.rows, d_model), lambda i, j, k: (i, 0)), _sp((d_model, dh), lambda i, j, k: (0, j)),
        grid=(seq // t.rows, 3, 1),
        out_shape=[sds((3, seq, dh), F32)], out_specs=[_sp((None, t.rows, dh), lambda i, j, k: (j, i, 0))],
        epilogue=plain_epi)

    d_pool, y_half = _pool_fwd(p3, wpool_full, pool_scale, seq, d_model)
    y, h0p, h1p = _lru_fwd(p3, y_half, pack_full, conv_b, wcat, wout_gather.relay(after=y_half), seq, d_model)
    (wout_full,) = wout_gather.wait(after=y)

    mix_rows = 2 * t.ln_rows

    def mix_epi(acc, i, ex, out):
        x_ref, g_ref, b_ref = ex[:3]
        for part in range(2):
            rows = pl.ds(part * t.ln_rows, t.ln_rows)
            z = ALPHA * x_ref[rows, :] + acc[part * t.ln_rows:(part + 1) * t.ln_rows, :]
            x1, _, _ = _ln_fwd(z, g_ref[...], b_ref[...])
            out[0][rows, :] = z
            out[1][rows, :] = x1
            out[2][rows, :] = x1.astype(BF16)

    z1, x1, x1b = _matmul(
        "mix_out", y, wout_full, _sp((mix_rows, d_model), row_full), _sp((d_model, d_model), vec, single=True),
        grid=(seq // mix_rows, 1, 1),
        extras=[(xs, _sp((mix_rows, d_model), row_full)), (ln_mix_g, _sp((1, d_model), vec)),
                (ln_mix_b, _sp((1, d_model), vec))],
        out_shape=[sds((seq, d_model), F32), sds((seq, d_model), F32), sds((seq, d_model), BF16)],
        out_specs=[_sp((mix_rows, d_model), row_full)] * 3, epilogue=mix_epi)
    (w1_full,) = w1_gather.wait(after=w1_gather.relay(after=x1b))

    def mlp_in_epi(acc, i, ex, out, cols):
        h = jnp.maximum(acc, 0.0)
        out[0][:, cols] = (h * h).astype(BF16)
        out[1][:, cols] = (2.0 * h).astype(BF16)

    hmid, dact = _matmul(
        "mlp_in", x1b, w1_full, _sp((t.rows, d_model), lambda i, j, k: (i, 0)),
        _sp((d_model, t.ff_cols), lambda i, j, k: (0, j)),
        grid=(seq // t.rows, d_ff // t.ff_cols, 1), j_outer=True,
        out_shape=[sds((seq, d_ff), BF16)] * 2, out_specs=[_sp((t.rows, t.ff_cols), lambda i, j, k: (i, j))] * 2,
        epilogue=mlp_in_epi, n_split=t.ff_split)
    (w2_full,) = w2_gather.wait(after=w2_gather.relay(after=hmid))

    (ffn,) = _matmul(
        "mlp_out", hmid, w2_full, _sp((t.rows, t.ff_k), lambda i, j, k: (i, k)),
        _sp((t.ff_k, d_model), lambda i, j, k: (k, 0)),
        grid=(seq // t.rows, 1, d_ff // t.ff_k),
        out_shape=[sds((seq, d_model), F32)], out_specs=[_sp((t.rows, d_model), row_full)])
    dz2, dz2b, g_ffn_g, g_ffn_b, loss_part = _ln_loss_bwd(ffn, x1, tgt, ln_ffn_g, ln_ffn_b, t.ln_rows // 2)

    (g_w2,) = _matmul(
        "grad_w_mlp_out", hmid, dz2b, _sp((seq, t.grad_rows), lambda i, j, k: (0, i)),
        _sp((seq, d_model), vec, single=True),
        grid=(d_ff // t.grad_rows, 1, 1), ta=True,
        out_shape=[sds((d_ff, d_model), BF16)], out_specs=[_sp((t.grad_rows, d_model), row_full)],
        epilogue=bf16_epi)
    scatter_w2 = _SplitReduceScatter("scatter_w_mlp_out", [g_w2.reshape(N_DEV, d_ff // N_DEV, d_model)])

    def dpre_epi(acc, i, ex, out, cols):
        out[0][:, cols] = (acc * ex[0][:, cols].astype(F32)).astype(BF16)

    (dpre,) = _matmul(
        "mlp_dpre", dz2b, w2_full, _sp((t.rows, d_model), lambda i, j, k: (i, 0)),
        _sp((t.ff_cols, d_model), lambda i, j, k: (j, 0)),
        grid=(seq // t.rows, d_ff // t.ff_cols, 1), j_outer=True, tb=True,
        extras=[(dact, _sp((t.rows, t.ff_cols), lambda i, j, k: (i, j))), after(scatter_w2.token)],
        out_shape=[sds((seq, d_ff), BF16)], out_specs=[_sp((t.rows, t.ff_cols), lambda i, j, k: (i, j))],
        epilogue=dpre_epi, n_split=t.ff_split)
    token_w2 = scatter_w2.combine_and_send(after=dpre)

    (dx1_mlp,) = _matmul(
        "mlp_dx", dpre, w1_full, _sp((t.rows, t.ff_k), lambda i, j, k: (i, k)),
        _sp((d_model, t.ff_k), lambda i, j, k: (0, k)),
        grid=(seq // t.rows, 1, d_ff // t.ff_k), tb=True, extras=[after(token_w2)],
        out_shape=[sds((seq, d_model), F32)], out_specs=[_sp((t.rows, d_model), row_full)])
    def block_epi(acc, i, ex, out):
        out[0][0] = acc.astype(BF16)

    fs = d_ff // N_DEV
    g_w1, dz1, dz1b, g_mix_g, g_mix_b = _matmul(
        "grad_w_mlp_in", x1b, dpre, _sp((seq, t.grad_rows), lambda i, j, k: (0, i)),
        _sp((seq, fs), lambda i, j, k: (0, j)),
        grid=(d_model // t.grad_rows, N_DEV, 1), j_outer=True, ta=True,
        out_shape=[sds((N_DEV, d_model, fs), BF16)],
        out_specs=[_sp((1, t.grad_rows, fs), lambda i, j, k: (j, i, 0))], epilogue=block_epi,
        side=_ln_bwd_side(dx1_mlp, dz2, z1, ln_mix_g, ln_mix_b, d_model // t.grad_rows * N_DEV))

    (dy,) = _matmul(
        "mix_dy", dz1b, wout_full, _sp((t.rows, d_model), lambda i, j, k: (i, 0)),
        _sp((d_model, d_model), vec, single=True),
        grid=(seq // t.rows, 1, 1), tb=True,
        out_shape=[sds((seq, d_model), F32)], out_specs=[_sp((t.rows, d_model), row_full)],
        epilogue=plain_epi)
    (g_wout,) = _matmul(
        "grad_w_out", y, dz1b, _sp((seq, t.grad_rows), lambda i, j, k: (0, i)), _sp((seq, d_model), vec, single=True),
        grid=(d_model // t.grad_rows, 1, 1), ta=True,
        out_shape=[sds((d_model, d_model), BF16)], out_specs=[_sp((t.grad_rows, d_model), row_full)],
        epilogue=bf16_epi)
    scatter_w1 = _SplitReduceScatter("scatter_w_mlp_in", [g_w1, g_wout.reshape(N_DEV, d_model // N_DEV, d_model)])

    dproj_pool, g_wpool, g_pscale = _pool_bwd(d_pool, dy, wpool_full, pool_scale, scatter_w1.token, seq, d_model)
    token_w1 = scatter_w1.combine_and_send(after=dproj_pool)
    dproj, g_pack, g_convb, g_wcat = _lru_bwd(p3, dy, h0p, h1p, dproj_pool, pack_full, conv_b, wcat,
                                              token_w1, seq, d_model)
    g_wa = jnp.stack([g_wcat[:, :, 0:lh], g_wcat[:, :, 2 * lh:3 * lh]])
    g_wi = jnp.stack([g_wcat[:, :, lh:2 * lh], g_wcat[:, :, 3 * lh:4 * lh]])

    rep_parts = [_rows128(g_wa), _rows128(g_wi), _rows128(g_mix_g), _rows128(g_mix_b), _rows128(g_ffn_g),
                 _rows128(g_ffn_b), _rows128(g_pscale), _rows128(g_convb)]
    rep_rows = [p.shape[0] for p in rep_parts]
    n_rep = sum(rep_rows)
    small = jnp.concatenate(rep_parts + [_rows128(g_pack), loss_part], axis=0)
    small_gather = _SplitGather("gather_small_grads", [(small[None], 0)], F32, after=small)

    ws = 3 * dh // N_DEV

    def pair_epi(acc, i, ex, out):
        out[0][0] = acc[:, :ws].astype(BF16)
        out[0][1] = acc[:, ws:].astype(BF16)

    def adam_big(name, own_landed, w, m, v):
        own, landed = own_landed
        shp = w.shape
        two = lambda a: a.reshape(-1, shp[-1])
        res = _sum_adamw(name, own, landed, two(w), two(m), two(v))
        return [r.reshape(shp) for r in res]

    (r_w2,) = scatter_w2.wait(after=small_gather.token)
    n_steps = d_model // t.grad_rows * (N_DEV // 2)
    g_win, *o_w2 = _matmul(
        "grad_w_in", xb, dproj, _sp((seq, t.grad_rows), lambda i, j, k: (0, i)),
        _sp((seq, 2 * ws), lambda i, j, k: (0, j)),
        grid=(d_model // t.grad_rows, N_DEV // 2, 1), ta=True,
        out_shape=[sds((N_DEV, d_model, ws), BF16)],
        out_specs=[_sp((2, t.grad_rows, ws), lambda i, j, k: (j, i, 0))], epilogue=pair_epi,
        side=_adamw_side(r_w2[0], r_w2[1], w_mlp_out[0], m_w_mlp_out[0], v_w_mlp_out[0], n_steps))
    o_w2 = [r.reshape(w_mlp_out.shape) for r in o_w2]
    scatter_mix = _SplitReduceScatter(
        "scatter_mixer", [g_win, g_wpool.reshape(N_DEV, pg // N_DEV * len(POOL_WINDOWS), pg)])

    r_w1, r_wout = scatter_w1.wait(after=scatter_mix.token)
    o_w1 = adam_big("adam_w_mlp_in", r_w1, w_mlp_in, m_w_mlp_in, v_w_mlp_in)
    token_mix = small_gather.relay(after=scatter_mix.combine_and_send(after=o_w1[0]))

    def dx_epi(acc, i, ex, out):
        out[0][...] = ALPHA * ex[0][...] + acc

    dx_rows = t.ln_rows * 2
    dx, *o_wout = _matmul(
        "grad_x", dproj, win_full, _sp((dx_rows, 3 * dh), lambda i, j, k: (i, 0)),
        _sp((d_model, 3 * dh), vec, single=True),
        grid=(seq // dx_rows, 1, 1), tb=True,
        extras=[(dz1, _sp((dx_rows, d_model), row_full)), after(token_mix)],
        out_shape=[sds((seq, d_model), F32)], out_specs=[_sp((dx_rows, d_model), row_full)],
        epilogue=dx_epi,
        side=_adamw_side(r_wout[0], r_wout[1], w_out[0], m_w_out[0], v_w_out[0], seq // dx_rows))
    o_wout = [r.reshape(w_out.shape) for r in o_wout]
    r_win, r_wpool = scatter_mix.wait(after=dx)
    o_win = adam_big("adam_w_in", r_win, w_in, m_w_in, v_w_in)
    o_wpool = adam_big("adam_w_pool", r_wpool, w_pool, m_w_pool, v_w_pool)

    (small_all,) = small_gather.wait(after=o_wpool[0])

    rep_w = [w_rg_a, w_rg_i, ln_mix_g, ln_mix_b, ln_ffn_g, ln_ffn_b, pool_scale, conv_b]
    rep_m = [m_w_rg_a, m_w_rg_i, m_ln_mix_g, m_ln_mix_b, m_ln_ffn_g, m_ln_ffn_b, m_pool_scale, m_conv_b]
    rep_v = [v_w_rg_a, v_w_rg_i, v_ln_mix_g, v_ln_mix_b, v_ln_ffn_g, v_ln_ffn_b, v_pool_scale, v_conv_b]
    cat = lambda arrs: jnp.concatenate([_rows128(a) for a in arrs], axis=0)
    o_rep = _sum_adamw("adam_replicated", None, small_all, cat(rep_w), cat(rep_m), cat(rep_v))

    my_idx = _dev_index(_where_am_i())
    head_parts = lax.dynamic_slice_in_dim(small_all, n_rep + my_idx * SMALL_ROWS, SMALL_ROWS, axis=1)
    o_head = _sum_adamw("adam_head", None, head_parts, pack_mine,
                        small_pack(m_conv_w, m_b_rg_a, m_b_rg_i, m_rg_lambda),
                        small_pack(v_conv_w, v_b_rg_a, v_b_rg_i, v_rg_lambda))

    def unpack_rep(packed):
        out, r = [], 0
        for wgt, rows in zip(rep_w, rep_rows):
            out.append(packed[r:r + rows].reshape(wgt.shape))
            r += rows
        return out

    def unpack_head(packed):
        return [packed[0:4].reshape(conv_w.shape), packed[4:6].reshape(b_rg_a.shape),
                packed[6:8].reshape(b_rg_i.shape), packed[8:10].reshape(rg_lambda.shape)]

    loss = _sum_blocks("loss_sum", small_all[:, n_rep + N_HEADS * SMALL_ROWS:, :])[0, 0]

    outs = [loss, dx[None]]
    for kind in range(4):
        ra, ri, mg, mb, fg, fb, ps, cb = unpack_rep(o_rep[kind])
        cw, ba, bi, lam = unpack_head(o_head[kind])
        outs += [mg, mb, o_win[kind], o_wpool[kind], ps, cw, cb, ra, ba, ri, bi, lam, o_wout[kind], fg, fb,
                 o_w1[kind], o_w2[kind]]
    return tuple(outs)
```
